```python
import math
import jax
import jax.numpy as jnp
from jax import lax
import numpy as np

D_MODEL = 2048
BATCH = 16
SEQ = 256
DEPTH = 2
DEC_BATCH = 4
DEC_SEQ = 4096
PAST_LEN = 512

GRID_W = 64
POS_BASE = 10000.0
N_EVEN = (DEPTH + 1) // 2
N_ODD = DEPTH // 2
N_DIR = 2
EPS = 1e-6

MIX_W_EVEN = 2 * D_MODEL
S5_W = MIX_W_EVEN // 4
S5_GROUP_CH = 16
S5_GROUPS = S5_W // S5_GROUP_CH
S5_STATE = 64
GLA_DV_W = MIX_W_EVEN - S5_W
GLA_HEADS = 6
GLA_DV = GLA_DV_W // GLA_HEADS
GLA_DK = GLA_DV // 2
GLA_DK_W = GLA_HEADS * GLA_DK
GLA_RANK = 16
GLA_NORMALIZER = 16.0
GLA_CHUNK = 64
GLA_LOG_DECAY_MIN = -1.0
EVEN_SIZES = (S5_W, S5_W, GLA_DK_W, GLA_DK_W, GLA_DV_W, GLA_DV_W, N_DIR * GLA_RANK)
EVEN_IN = 2 * S5_W + 2 * GLA_DK_W + 2 * GLA_DV_W + N_DIR * GLA_RANK

RWKV_W = D_MODEL
RWKV_HEAD = 64
RWKV_HEADS = RWKV_W // RWKV_HEAD
RWKV_DECAY_RANK = 96
RWKV_ICLR_RANK = 96
RWKV_LNX_EPS = 64e-5
ODD_SIZES = (RWKV_W, RWKV_W, RWKV_W, RWKV_W, N_DIR * RWKV_DECAY_RANK, N_DIR * RWKV_ICLR_RANK)
ODD_IN = 4 * RWKV_W + N_DIR * (RWKV_DECAY_RANK + RWKV_ICLR_RANK)

kernel_name = 'bidir_s5_gla_rwkv7_prefix_diffusion_step'


def _split_cols(t, sizes):
    offsets, acc = [], 0
    for s in sizes[:-1]:
        acc += s
        offsets.append(acc)
    return jnp.split(t, offsets, axis=-1)


def _rev(t):
    return jnp.flip(t, axis=1)


def _rms(x, w):
    xf = x.astype(jnp.float32)
    return xf * lax.rsqrt(jnp.mean(xf * xf, axis=-1, keepdims=True) + EPS) * w.astype(jnp.float32)


def _adaln(cond, w, b):
    m = jax.nn.silu(cond.astype(jnp.float32)) @ w.astype(jnp.float32) + b.astype(jnp.float32)
    shift, scale, gate = jnp.split(m, 3, axis=-1)
    return shift[..., None, :], scale[..., None, :], gate[..., None, :]


def _grid_pos_embed(n_tokens):
    rows = n_tokens // GRID_W
    row_id = jnp.broadcast_to(jnp.arange(rows, dtype=jnp.float32)[:, None], (rows, GRID_W)).reshape(-1)
    col_id = jnp.broadcast_to(jnp.arange(GRID_W, dtype=jnp.float32)[None, :], (rows, GRID_W)).reshape(-1)
    quarter = D_MODEL // 4
    omega = 1.0 / (POS_BASE ** (jnp.arange(quarter, dtype=jnp.float32) / quarter))
    def axis_emb(pos):
        ang = pos[:, None] * omega[None, :]
        return jnp.concatenate([jnp.sin(ang), jnp.cos(ang)], axis=-1)
    return jnp.concatenate([axis_emb(row_id), axis_emb(col_id)], axis=-1)


def _complex_affine_combine(e1, e2):
    a1r, a1i, b1r, b1i = e1
    a2r, a2i, b2r, b2i = e2
    return (a1r * a2r - a1i * a2i,
            a1r * a2i + a1i * a2r,
            a2r * b1r - a2i * b1i + b2r,
            a2r * b1i + a2i * b1r + b2i)


def _s5_direction(u, lam_re, lam_im, log_step, b_re, b_im, c_re, c_im, h0_re, h0_im):
    lam_re = lam_re.astype(jnp.float32)
    lam_im = lam_im.astype(jnp.float32)
    dt = jnp.exp(log_step.astype(jnp.float32))[:, None]
    mag = jnp.exp(lam_re * dt)
    ab_re, ab_im = mag * jnp.cos(lam_im * dt), mag * jnp.sin(lam_im * dt)
    den = lam_re * lam_re + lam_im * lam_im
    f_re = ((ab_re - 1.0) * lam_re + ab_im * lam_im) / den
    f_im = (ab_im * lam_re - (ab_re - 1.0) * lam_im) / den
    bb_re = f_re[..., None] * b_re - f_im[..., None] * b_im
    bb_im = f_re[..., None] * b_im + f_im[..., None] * b_re
    bu_re = jnp.einsum('blgh,gph->blgp', u, bb_re)
    bu_im = jnp.einsum('blgh,gph->blgp', u, bb_im)
    a_re = jnp.broadcast_to(ab_re, bu_re.shape)
    a_im = jnp.broadcast_to(ab_im, bu_im.shape)
    pw_re, pw_im, s_re, s_im = lax.associative_scan(_complex_affine_combine, (a_re, a_im, bu_re, bu_im), axis=1)
    h0r, h0i = h0_re[:, None], h0_im[:, None]
    h_re = s_re + pw_re * h0r - pw_im * h0i
    h_im = s_im + pw_re * h0i + pw_im * h0r
    y = jnp.einsum('blgp,ghp->blgh', h_re, c_re) - jnp.einsum('blgp,ghp->blgh', h_im, c_im)
    return y, h_re[:, -1], h_im[:, -1]


def _gla_direction(q, k, v, log_a, s0):
    bsz, L, H, _ = q.shape
    n = L // GLA_CHUNK
    def chunks(t):
        return t.reshape(bsz, n, GLA_CHUNK, H, t.shape[-1]).transpose(1, 0, 3, 2, 4)
    causal = jnp.tril(jnp.ones((GLA_CHUNK, GLA_CHUNK), dtype=bool))
    def step(S, xs):
        qc, kc, vc, gc = xs
        b = jnp.cumsum(gc, axis=-2)
        b_last = b[..., -1:, :]
        q_dec = qc * jnp.exp(b)
        k_inv = kc * jnp.exp(-b)
        k_end = kc * jnp.exp(b_last - b)
        att = jnp.where(causal, jnp.einsum('bhcd,bhsd->bhcs', q_dec, k_inv), 0.0)
        o = jnp.einsum('bhcs,bhse->bhce', att, vc) + jnp.einsum('bhcd,bhde->bhce', q_dec, S)
        S = jnp.exp(b_last[..., 0, :])[..., None] * S + jnp.einsum('bhcd,bhce->bhde', k_end, vc)
        return S, o
    S, o = lax.scan(step, s0, (chunks(q), chunks(k), chunks(v), chunks(log_a)))
    return o.transpose(1, 0, 3, 2, 4).reshape(bsz, L, H, -1), S


def _rwkv_direction(r, w, k, v, a, b, s0):
    def tm(t):
        return jnp.swapaxes(t, 0, 1)
    def step(S, xs):
        r_t, w_t, k_t, v_t, a_t, b_t = xs
        sa = jnp.einsum('bhij,bhj->bhi', S, a_t)
        S = S * w_t[:, :, None, :] + sa[..., None] * b_t[:, :, None, :] + v_t[..., None] * k_t[:, :, None, :]
        return S, jnp.einsum('bhij,bhj->bhi', S, r_t)
    S, y = lax.scan(step, s0, tuple(tm(t) for t in (r, w, k, v, a, b)))
    return tm(y), S


def _even_mixer(h, s5_re0, s5_im0, gla0, w_in, w_out, lam_re, lam_im, log_step, b_re, b_im, c_re, c_im,
                d_skip, glu_w, glu_b, dec_up, dec_b, gla_nw):
    bsz, L, _ = h.shape
    u, s5_g, q, k, v, gla_g, dec_lr = _split_cols(h @ w_in, EVEN_SIZES)
    u_g = u.reshape(bsz, L, S5_GROUPS, S5_GROUP_CH)
    y = d_skip * u
    fin_re, fin_im = [], []
    for d in range(N_DIR):
        u_d = _rev(u_g) if d else u_g
        y_d, hr, hi = _s5_direction(u_d, lam_re[d], lam_im[d], log_step[d], b_re[d], b_im[d], c_re[d], c_im[d],
                                    s5_re0[:, d].astype(jnp.float32), s5_im0[:, d].astype(jnp.float32))
        y = y + (_rev(y_d) if d else y_d).reshape(bsz, L, S5_W)
        fin_re.append(hr)
        fin_im.append(hi)
    gy = jax.nn.gelu(y)
    s5_out = gy * jax.nn.sigmoid(gy @ glu_w + glu_b) * jax.nn.silu(s5_g)
    q = q.reshape(bsz, L, GLA_HEADS, GLA_DK) * (GLA_DK ** -0.5)
    k = k.reshape(bsz, L, GLA_HEADS, GLA_DK)
    v = v.reshape(bsz, L, GLA_HEADS, GLA_DV)
    dec_lr = dec_lr.reshape(bsz, L, N_DIR, GLA_RANK)
    o = 0.0
    fin_gla = []
    for d in range(N_DIR):
        z = dec_lr[:, :, d] @ dec_up[d] + dec_b[d]
        log_a = jnp.maximum(jax.nn.log_sigmoid(z) / GLA_NORMALIZER, GLA_LOG_DECAY_MIN)
        seq = (q, k, v, log_a.reshape(bsz, L, GLA_HEADS, GLA_DK))
        if d:
            seq = tuple(_rev(t) for t in seq)
        o_d, S = _gla_direction(*seq, gla0[:, d].astype(jnp.float32))
        o = o + (_rev(o_d) if d else o_d)
        fin_gla.append(S)
    o = o * lax.rsqrt(jnp.mean(o * o, axis=-1, keepdims=True) + EPS) * gla_nw.reshape(GLA_HEADS, GLA_DV)
    gla_out = o.reshape(bsz, L, GLA_DV_W) * jax.nn.silu(gla_g)
    out = jnp.concatenate([s5_out, gla_out], axis=-1) @ w_out
    return out, jnp.stack(fin_re, axis=1), jnp.stack(fin_im, axis=1), jnp.stack(fin_gla, axis=1)


def _odd_mixer(h, rwkv0, w_in, w_out, mu, w0, w2, a0, a2, k_k, k_a, r_k, lnx_w, lnx_b):
    bsz, L, _ = h.shape
    zero = jnp.zeros_like(h[:, :1])
    h_prev = jnp.concatenate([zero, h[:, :-1]], axis=1)
    h_next = jnp.concatenate([h[:, 1:], zero], axis=1)
    xs = h + mu[0] * (h_prev - h) + mu[1] * (h_next - h)
    r, k, v, g, w_lr, a_lr = _split_cols(xs @ w_in, ODD_SIZES)
    w_lr = jnp.tanh(w_lr).reshape(bsz, L, N_DIR, RWKV_DECAY_RANK)
    a_lr = a_lr.reshape(bsz, L, N_DIR, RWKV_ICLR_RANK)
    def heads(t):
        return t.reshape(bsz, L, RWKV_HEADS, RWKV_HEAD)
    kk = heads(k * k_k)
    kk = kk / jnp.maximum(jnp.sqrt(jnp.sum(kk * kk, axis=-1, keepdims=True)), 1e-12)
    r_h, v_h = heads(r), heads(v)
    wkv, bonus = 0.0, 0.0
    finals = []
    for d in range(N_DIR):
        w_log = -jax.nn.softplus(-(w0[d] + w_lr[:, :, d] @ w2[d])) - 0.5
        decay = jnp.exp(-jnp.exp(w_log))
        a = jax.nn.sigmoid(a0[d] + a_lr[:, :, d] @ a2[d])
        k_d = heads(k * (1.0 + (a - 1.0) * k_a))
        seq = (r_h, heads(decay), k_d, v_h, -kk, kk * heads(a))
        if d:
            seq = tuple(_rev(t) for t in seq)
        y_d, S = _rwkv_direction(*seq, rwkv0[:, d].astype(jnp.float32))
        wkv = wkv + (_rev(y_d) if d else y_d)
        bonus = bonus + jnp.sum(r_h * k_d * r_k, axis=-1, keepdims=True) * v_h
        finals.append(S)
    mean = jnp.mean(wkv, axis=-1, keepdims=True)
    var = jnp.mean(jnp.square(wkv - mean), axis=-1, keepdims=True)
    ln = ((wkv - mean) * lax.rsqrt(var + RWKV_LNX_EPS) * lnx_w.reshape(RWKV_HEADS, RWKV_HEAD)
          + lnx_b.reshape(RWKV_HEADS, RWKV_HEAD))
    out = (ln + bonus).reshape(bsz, L, RWKV_W) * jax.nn.silu(g)
    return out @ w_out, jnp.stack(finals, axis=1)


def setup_inputs(seed: int = 0) -> dict:
    key = jax.random.key(seed)
    ks = jax.random.split(key, 40)
    f32 = jnp.float32
    def nrm(i, shape, scale):
        return jax.random.normal(ks[i], shape, f32) * scale
    def uni(i, shape, lo, hi):
        return jax.random.uniform(ks[i], shape, f32, lo, hi)
    return {
        'x_prompt': nrm(0, (BATCH, SEQ, D_MODEL), 1.0),
        'x_sample': nrm(1, (DEC_BATCH, DEC_SEQ, D_MODEL), 1.0),
        'state_s5_re': nrm(2, (DEC_BATCH, N_EVEN, N_DIR, S5_GROUPS, S5_STATE), 0.1),
        'state_s5_im': nrm(3, (DEC_BATCH, N_EVEN, N_DIR, S5_GROUPS, S5_STATE), 0.1),
        'state_gla': nrm(4, (DEC_BATCH, N_EVEN, N_DIR, GLA_HEADS, GLA_DK, GLA_DV), 1.0),
        'state_rwkv': nrm(5, (DEC_BATCH, N_ODD, N_DIR, RWKV_HEADS, RWKV_HEAD, RWKV_HEAD), 0.5),
        'c': nrm(6, (DEC_BATCH, D_MODEL), 1.0),
        'c_ctx': nrm(7, (D_MODEL,), 1.0),
        'norm_w': 1.0 + nrm(8, (DEPTH, D_MODEL), 0.02),
        'ada_w': nrm(9, (DEPTH, D_MODEL, 3 * D_MODEL), 0.5 * D_MODEL ** -0.5),
        'ada_b': nrm(10, (DEPTH, 3 * D_MODEL), 0.02),
        'final_norm_w': 1.0 + nrm(11, (D_MODEL,), 0.02),
        'e_w_in': nrm(12, (N_EVEN, D_MODEL, EVEN_IN), D_MODEL ** -0.5),
        'e_w_out': nrm(13, (N_EVEN, MIX_W_EVEN, D_MODEL), MIX_W_EVEN ** -0.5),
        's5_lambda_re': -0.5 + nrm(14, (N_EVEN, N_DIR, S5_GROUPS, S5_STATE), 0.01),
        's5_lambda_im': math.pi * jnp.arange(S5_STATE, dtype=f32) + nrm(15, (N_EVEN, N_DIR, S5_GROUPS, S5_STATE), 0.01),
        's5_log_step': uni(16, (N_EVEN, N_DIR, S5_GROUPS), math.log(1e-3), math.log(1e-1)),
        's5_b_re': nrm(17, (N_EVEN, N_DIR, S5_GROUPS, S5_STATE, S5_GROUP_CH), (2 * S5_GROUP_CH) ** -0.5),
        's5_b_im': nrm(18, (N_EVEN, N_DIR, S5_GROUPS, S5_STATE, S5_GROUP_CH), (2 * S5_GROUP_CH) ** -0.5),
        's5_c_re': nrm(19, (N_EVEN, N_DIR, S5_GROUPS, S5_GROUP_CH, S5_STATE), (2 * S5_STATE) ** -0.5),
        's5_c_im': nrm(20, (N_EVEN, N_DIR, S5_GROUPS, S5_GROUP_CH, S5_STATE), (2 * S5_STATE) ** -0.5),
        's5_d': nrm(21, (N_EVEN, S5_W), 1.0),
        's5_glu_w': nrm(22, (N_EVEN, S5_W, S5_W), S5_W ** -0.5),
        's5_glu_b': nrm(23, (N_EVEN, S5_W), 0.02),
        'gla_decay_up': nrm(24, (N_EVEN, N_DIR, GLA_RANK, GLA_DK_W), GLA_RANK ** -0.5),
        'gla_decay_b': nrm(25, (N_EVEN, N_DIR, GLA_DK_W), 0.1),
        'gla_norm_w': 1.0 + nrm(26, (N_EVEN, GLA_DV_W), 0.02),
        'o_w_in': nrm(27, (N_ODD, D_MODEL, ODD_IN), D_MODEL ** -0.5),
        'o_w_out': nrm(28, (N_ODD, RWKV_W, D_MODEL), RWKV_W ** -0.5),
        'rwkv_mu': uni(29, (N_ODD, 2, D_MODEL), 0.0, 0.5),
        'rwkv_w0': jnp.linspace(-6.5, -1.5, RWKV_W, dtype=f32) + nrm(30, (N_ODD, N_DIR, RWKV_W), 0.1),
        'rwkv_w2': nrm(31, (N_ODD, N_DIR, RWKV_DECAY_RANK, RWKV_W), 0.1 * RWKV_DECAY_RANK ** -0.5),
        'rwkv_a0': nrm(32, (N_ODD, N_DIR, RWKV_W), 0.1),
        'rwkv_a2': nrm(33, (N_ODD, N_DIR, RWKV_ICLR_RANK, RWKV_W), 0.1 * RWKV_ICLR_RANK ** -0.5),
        'rwkv_k_k': 0.85 + nrm(34, (N_ODD, RWKV_W), 0.05),
        'rwkv_k_a': 1.0 + nrm(35, (N_ODD, RWKV_W), 0.05),
        'rwkv_r_k': nrm(36, (N_ODD, RWKV_HEADS, RWKV_HEAD), 0.1),
        'rwkv_lnx_w': 1.0 + nrm(37, (N_ODD, RWKV_W), 0.02),
        'rwkv_lnx_b': nrm(38, (N_ODD, RWKV_W), 0.02),
    }


def reference(x_prompt, x_sample, state_s5_re, state_s5_im, state_gla, state_rwkv, c,
              c_ctx, norm_w, ada_w, ada_b, final_norm_w,
              e_w_in, e_w_out, s5_lambda_re, s5_lambda_im, s5_log_step, s5_b_re, s5_b_im, s5_c_re, s5_c_im,
              s5_d, s5_glu_w, s5_glu_b, gla_decay_up, gla_decay_b, gla_norm_w,
              o_w_in, o_w_out, rwkv_mu, rwkv_w0, rwkv_w2, rwkv_a0, rwkv_a2, rwkv_k_k, rwkv_k_a, rwkv_r_k,
              rwkv_lnx_w, rwkv_lnx_b):
    f32 = jnp.float32
    bp = x_prompt.shape[0]
    x_ctx = x_prompt
    x_lat = (x_sample.astype(f32) + _grid_pos_embed(x_sample.shape[1])[None]).astype(x_sample.dtype)
    z_s5 = jnp.zeros((bp, N_DIR, S5_GROUPS, S5_STATE), f32)
    z_gla = jnp.zeros((bp, N_DIR, GLA_HEADS, GLA_DK, GLA_DV), f32)
    z_rwkv = jnp.zeros((bp, N_DIR, RWKV_HEADS, RWKV_HEAD, RWKV_HEAD), f32)
    new_s5_re, new_s5_im, new_gla, new_rwkv = [], [], [], []
    for i in range(DEPTH):
        j = i // 2
        sh_c, sc_c, gt_c = _adaln(c_ctx, ada_w[i], ada_b[i])
        sh_l, sc_l, gt_l = _adaln(c, ada_w[i], ada_b[i])
        h_ctx = _rms(x_ctx, norm_w[i]) * (1.0 + sc_c) + sh_c
        h_lat = _rms(x_lat, norm_w[i]) * (1.0 + sc_l) + sh_l
        if i % 2 == 0:
            p = (e_w_in[j], e_w_out[j], s5_lambda_re[j], s5_lambda_im[j], s5_log_step[j], s5_b_re[j], s5_b_im[j],
                 s5_c_re[j], s5_c_im[j], s5_d[j], s5_glu_w[j], s5_glu_b[j], gla_decay_up[j], gla_decay_b[j],
                 gla_norm_w[j])
            o_ctx, fr, fi, fg = _even_mixer(h_ctx, z_s5, z_s5, z_gla, *p)
            o_lat, _, _, _ = _even_mixer(h_lat, state_s5_re[:, j], state_s5_im[:, j], state_gla[:, j], *p)
            new_s5_re.append(fr)
            new_s5_im.append(fi)
            new_gla.append(fg)
        else:
            p = (o_w_in[j], o_w_out[j], rwkv_mu[j], rwkv_w0[j], rwkv_w2[j], rwkv_a0[j], rwkv_a2[j],
                 rwkv_k_k[j], rwkv_k_a[j], rwkv_r_k[j], rwkv_lnx_w[j], rwkv_lnx_b[j])
            o_ctx, fw = _odd_mixer(h_ctx, z_rwkv, *p)
            o_lat, _ = _odd_mixer(h_lat, state_rwkv[:, j], *p)
            new_rwkv.append(fw)
        x_ctx = x_ctx + (gt_c * o_ctx).astype(x_ctx.dtype)
        x_lat = x_lat + (gt_l * o_lat).astype(x_lat.dtype)
    y_prompt = _rms(x_ctx, final_norm_w).astype(x_prompt.dtype)
    y_sample = _rms(x_lat, final_norm_w).astype(x_sample.dtype)
    return (y_prompt, y_sample, jnp.stack(new_s5_re, axis=1), jnp.stack(new_s5_im, axis=1),
            jnp.stack(new_gla, axis=1), jnp.stack(new_rwkv, axis=1))
```

```python
import functools
import math

import jax
import jax.numpy as jnp
from jax import lax
from jax.experimental import pallas as pl
from jax.experimental.pallas import tpu as pltpu

D_MODEL = 2048
GRID_W = 64
POS_BASE = 10000.0
N_DIR = 2
EPS = 1e-6
S5_W = 1024
S5_GROUP_CH = 16
S5_GROUPS = 64
S5_STATE = 64
GLA_HEADS = 6
GLA_DV = 512
GLA_DK = 256
GLA_DK_W = 1536
GLA_DV_W = 3072
GLA_RANK = 16
GLA_NORMALIZER = 16.0
GLA_CHUNK = 64
GLA_LOG_DECAY_MIN = -1.0
EVEN_SIZES = (S5_W, S5_W, GLA_DK_W, GLA_DK_W, GLA_DV_W, GLA_DV_W, N_DIR * GLA_RANK)
RWKV_W = 2048
RWKV_HEAD = 64
RWKV_HEADS = 32
RWKV_DECAY_RANK = 96
RWKV_ICLR_RANK = 96
RWKV_LNX_EPS = 64e-5
ODD_SIZES = (RWKV_W, RWKV_W, RWKV_W, RWKV_W, N_DIR * RWKV_DECAY_RANK, N_DIR * RWKV_ICLR_RANK)

VMEM_LIMIT = 48 * 1024 * 1024


def _mm_kernel(x_ref, w_ref, o_ref):
    o_ref[...] = jnp.dot(x_ref[...], w_ref[...], preferred_element_type=jnp.float32)


def _pick(n, prefs):
    for p in prefs:
        if n % p == 0:
            return p
    return n


def matmul(x, w):
    m, k = x.shape
    n = w.shape[1]
    x = x.astype(jnp.bfloat16)
    w = w.astype(jnp.bfloat16)
    tm = _pick(m, (1024, 512, 256, 128, 64, 32, 16, 8))
    tn = _pick(n, (512, 384, 256, 128))
    return pl.pallas_call(
        _mm_kernel,
        grid=(m // tm, n // tn),
        in_specs=[pl.BlockSpec((tm, k), lambda i, j: (i, 0)),
                  pl.BlockSpec((k, tn), lambda i, j: (0, j))],
        out_specs=pl.BlockSpec((tm, tn), lambda i, j: (i, j)),
        out_shape=jax.ShapeDtypeStruct((m, n), jnp.float32),
        compiler_params=pltpu.CompilerParams(
            dimension_semantics=("arbitrary", "arbitrary"), vmem_limit_bytes=VMEM_LIMIT),
        name="proj_matmul",
    )(x, w)


def _mm3(h, w):
    b, l, k = h.shape
    return matmul(h.reshape(b * l, k), w).reshape(b, l, -1)


def _split_cols(t, sizes):
    offsets, acc = [], 0
    for s in sizes[:-1]:
        acc += s
        offsets.append(acc)
    return jnp.split(t, offsets, axis=-1)


def _rev(t):
    return jnp.flip(t, axis=1)


def _rms(x, w):
    return x * lax.rsqrt(jnp.mean(x * x, axis=-1, keepdims=True) + EPS) * w


def _adaln(cond, w, b):
    m = jnp.dot(jax.nn.silu(cond), w, precision=lax.Precision.HIGHEST) + b
    shift, scale, gate = jnp.split(m, 3, axis=-1)
    return shift[..., None, :], scale[..., None, :], gate[..., None, :]


def _grid_pos_embed(n_tokens):
    rows = n_tokens // GRID_W
    row_id = jnp.broadcast_to(jnp.arange(rows, dtype=jnp.float32)[:, None], (rows, GRID_W)).reshape(-1)
    col_id = jnp.broadcast_to(jnp.arange(GRID_W, dtype=jnp.float32)[None, :], (rows, GRID_W)).reshape(-1)
    quarter = D_MODEL // 4
    omega = 1.0 / (POS_BASE ** (jnp.arange(quarter, dtype=jnp.float32) / quarter))

    def axis_emb(pos):
        ang = pos[:, None] * omega[None, :]
        return jnp.concatenate([jnp.sin(ang), jnp.cos(ang)], axis=-1)
    return jnp.concatenate([axis_emb(row_id), axis_emb(col_id)], axis=-1)


def _complex_affine_combine(e1, e2):
    a1r, a1i, b1r, b1i = e1
    a2r, a2i, b2r, b2i = e2
    return (a1r * a2r - a1i * a2i, a1r * a2i + a1i * a2r,
            a2r * b1r - a2i * b1i + b2r, a2r * b1i + a2i * b1r + b2i)


def _s5_direction(u, lam_re, lam_im, log_step, b_re, b_im, c_re, c_im, h0_re, h0_im):
    dt = jnp.exp(log_step)[:, None]
    mag = jnp.exp(lam_re * dt)
    ab_re, ab_im = mag * jnp.cos(lam_im * dt), mag * jnp.sin(lam_im * dt)
    den = lam_re * lam_re + lam_im * lam_im
    f_re = ((ab_re - 1.0) * lam_re + ab_im * lam_im) / den
    f_im = (ab_im * lam_re - (ab_re - 1.0) * lam_im) / den
    bb_re = f_re[..., None] * b_re - f_im[..., None] * b_im
    bb_im = f_re[..., None] * b_im + f_im[..., None] * b_re
    bu_re = jnp.einsum('blgh,gph->blgp', u, bb_re)
    bu_im = jnp.einsum('blgh,gph->blgp', u, bb_im)
    a_re = jnp.broadcast_to(ab_re, bu_re.shape)
    a_im = jnp.broadcast_to(ab_im, bu_im.shape)
    pw_re, pw_im, s_re, s_im = lax.associative_scan(_complex_affine_combine, (a_re, a_im, bu_re, bu_im), axis=1)
    h0r, h0i = h0_re[:, None], h0_im[:, None]
    h_re = s_re + pw_re * h0r - pw_im * h0i
    h_im = s_im + pw_re * h0i + pw_im * h0r
    y = jnp.einsum('blgp,ghp->blgh', h_re, c_re) - jnp.einsum('blgp,ghp->blgh', h_im, c_im)
    return y, h_re[:, -1], h_im[:, -1]


def _gla_direction(q, k, v, log_a, s0):
    bsz, L, H, _ = q.shape
    n = L // GLA_CHUNK

    def chunks(t):
        return t.reshape(bsz, n, GLA_CHUNK, H, t.shape[-1]).transpose(1, 0, 3, 2, 4)
    causal = jnp.tril(jnp.ones((GLA_CHUNK, GLA_CHUNK), dtype=bool))

    def step(S, xs):
        qc, kc, vc, gc = xs
        b = jnp.cumsum(gc, axis=-2)
        b_last = b[..., -1:, :]
        q_dec = qc * jnp.exp(b)
        k_inv = kc * jnp.exp(-b)
        k_end = kc * jnp.exp(b_last - b)
        att = jnp.where(causal, jnp.einsum('bhcd,bhsd->bhcs', q_dec, k_inv), 0.0)
        o = jnp.einsum('bhcs,bhse->bhce', att, vc) + jnp.einsum('bhcd,bhde->bhce', q_dec, S)
        S = jnp.exp(b_last[..., 0, :])[..., None] * S + jnp.einsum('bhcd,bhce->bhde', k_end, vc)
        return S, o
    S, o = lax.scan(step, s0, (chunks(q), chunks(k), chunks(v), chunks(log_a)))
    return o.transpose(1, 0, 3, 2, 4).reshape(bsz, L, H, -1), S


def _rwkv_direction(r, w, k, v, a, b, s0):
    def tm(t):
        return jnp.swapaxes(t, 0, 1)

    def step(S, xs):
        r_t, w_t, k_t, v_t, a_t, b_t = xs
        sa = jnp.einsum('bhij,bhj->bhi', S, a_t)
        S = S * w_t[:, :, None, :] + sa[..., None] * b_t[:, :, None, :] + v_t[..., None] * k_t[:, :, None, :]
        return S, jnp.einsum('bhij,bhj->bhi', S, r_t)
    S, y = lax.scan(step, s0, tuple(tm(t) for t in (r, w, k, v, a, b)))
    return tm(y), S


def _even_mixer(h, s5_re0, s5_im0, gla0, w_in, w_out, lam_re, lam_im, log_step, b_re, b_im, c_re, c_im,
                d_skip, glu_w, glu_b, dec_up, dec_b, gla_nw):
    bsz, L, _ = h.shape
    n_main = sum(EVEN_SIZES[:-1])
    main = _mm3(h, w_in[:, :n_main])
    w_tail = jnp.pad(w_in[:, n_main:], ((0, 0), (0, 128 - N_DIR * GLA_RANK)))
    dec_lr = _mm3(h, w_tail)[..., :N_DIR * GLA_RANK]
    u, s5_g, q, k, v, gla_g = _split_cols(main, EVEN_SIZES[:-1])
    u_g = u.reshape(bsz, L, S5_GROUPS, S5_GROUP_CH)
    y = d_skip * u
    fin_re, fin_im = [], []
    for d in range(N_DIR):
        u_d = _rev(u_g) if d else u_g
        y_d, hr, hi = _s5_direction(u_d, lam_re[d], lam_im[d], log_step[d], b_re[d], b_im[d], c_re[d], c_im[d],
                                    s5_re0[:, d], s5_im0[:, d])
        y = y + (_rev(y_d) if d else y_d).reshape(bsz, L, S5_W)
        fin_re.append(hr)
        fin_im.append(hi)
    gy = jax.nn.gelu(y)
    s5_out = gy * jax.nn.sigmoid(_mm3(gy, glu_w) + glu_b) * jax.nn.silu(s5_g)
    q = q.reshape(bsz, L, GLA_HEADS, GLA_DK) * (GLA_DK ** -0.5)
    k = k.reshape(bsz, L, GLA_HEADS, GLA_DK)
    v = v.reshape(bsz, L, GLA_HEADS, GLA_DV)
    dec_lr = dec_lr.reshape(bsz, L, N_DIR, GLA_RANK)
    o = 0.0
    fin_gla = []
    for d in range(N_DIR):
        z = dec_lr[:, :, d] @ dec_up[d] + dec_b[d]
        log_a = jnp.maximum(jax.nn.log_sigmoid(z) / GLA_NORMALIZER, GLA_LOG_DECAY_MIN)
        seq = (q, k, v, log_a.reshape(bsz, L, GLA_HEADS, GLA_DK))
        if d:
            seq = tuple(_rev(t) for t in seq)
        o_d, S = _gla_direction(*seq, gla0[:, d])
        o = o + (_rev(o_d) if d else o_d)
        fin_gla.append(S)
    o = o * lax.rsqrt(jnp.mean(o * o, axis=-1, keepdims=True) + EPS) * gla_nw.reshape(GLA_HEADS, GLA_DV)
    gla_out = o.reshape(bsz, L, GLA_DV_W) * jax.nn.silu(gla_g)
    out = _mm3(jnp.concatenate([s5_out, gla_out], axis=-1), w_out)
    return out, jnp.stack(fin_re, axis=1), jnp.stack(fin_im, axis=1), jnp.stack(fin_gla, axis=1)


def _odd_mixer(h, rwkv0, w_in, w_out, mu, w0, w2, a0, a2, k_k, k_a, r_k, lnx_w, lnx_b):
    bsz, L, _ = h.shape
    zero = jnp.zeros_like(h[:, :1])
    h_prev = jnp.concatenate([zero, h[:, :-1]], axis=1)
    h_next = jnp.concatenate([h[:, 1:], zero], axis=1)
    xs = h + mu[0] * (h_prev - h) + mu[1] * (h_next - h)
    n_main = sum(ODD_SIZES[:4])
    main = _mm3(xs, w_in[:, :n_main])
    tail = _mm3(xs, w_in[:, n_main:])
    r, k, v, g = _split_cols(main, ODD_SIZES[:4])
    w_lr, a_lr = _split_cols(tail, ODD_SIZES[4:])
    w_lr = jnp.tanh(w_lr).reshape(bsz, L, N_DIR, RWKV_DECAY_RANK)
    a_lr = a_lr.reshape(bsz, L, N_DIR, RWKV_ICLR_RANK)

    def heads(t):
        return t.reshape(bsz, L, RWKV_HEADS, RWKV_HEAD)
    kk = heads(k * k_k)
    kk = kk / jnp.maximum(jnp.sqrt(jnp.sum(kk * kk, axis=-1, keepdims=True)), 1e-12)
    r_h, v_h = heads(r), heads(v)
    wkv, bonus = 0.0, 0.0
    finals = []
    for d in range(N_DIR):
        w_log = -jax.nn.softplus(-(w0[d] + w_lr[:, :, d] @ w2[d])) - 0.5
        decay = jnp.exp(-jnp.exp(w_log))
        a = jax.nn.sigmoid(a0[d] + a_lr[:, :, d] @ a2[d])
        k_d = heads(k * (1.0 + (a - 1.0) * k_a))
        seq = (r_h, heads(decay), k_d, v_h, -kk, kk * heads(a))
        if d:
            seq = tuple(_rev(t) for t in seq)
        y_d, S = _rwkv_direction(*seq, rwkv0[:, d])
        wkv = wkv + (_rev(y_d) if d else y_d)
        bonus = bonus + jnp.sum(r_h * k_d * r_k, axis=-1, keepdims=True) * v_h
        finals.append(S)
    mean = jnp.mean(wkv, axis=-1, keepdims=True)
    var = jnp.mean(jnp.square(wkv - mean), axis=-1, keepdims=True)
    ln = ((wkv - mean) * lax.rsqrt(var + RWKV_LNX_EPS) * lnx_w.reshape(RWKV_HEADS, RWKV_HEAD)
          + lnx_b.reshape(RWKV_HEADS, RWKV_HEAD))
    out = (ln + bonus).reshape(bsz, L, RWKV_W) * jax.nn.silu(g)
    return _mm3(out, w_out), jnp.stack(finals, axis=1)


def kernel(x_prompt, x_sample, state_s5_re, state_s5_im, state_gla, state_rwkv, c, c_ctx, norm_w, ada_w, ada_b, final_norm_w, e_w_in, e_w_out, s5_lambda_re, s5_lambda_im, s5_log_step, s5_b_re, s5_b_im, s5_c_re, s5_c_im, s5_d, s5_glu_w, s5_glu_b, gla_decay_up, gla_decay_b, gla_norm_w, o_w_in, o_w_out, rwkv_mu, rwkv_w0, rwkv_w2, rwkv_a0, rwkv_a2, rwkv_k_k, rwkv_k_a, rwkv_r_k, rwkv_lnx_w, rwkv_lnx_b):
    f32 = jnp.float32
    bp = x_prompt.shape[0]
    depth = norm_w.shape[0]
    x_ctx = x_prompt
    x_lat = x_sample + _grid_pos_embed(x_sample.shape[1])[None]
    z_s5 = jnp.zeros((bp, N_DIR, S5_GROUPS, S5_STATE), f32)
    z_gla = jnp.zeros((bp, N_DIR, GLA_HEADS, GLA_DK, GLA_DV), f32)
    z_rwkv = jnp.zeros((bp, N_DIR, RWKV_HEADS, RWKV_HEAD, RWKV_HEAD), f32)
    new_s5_re, new_s5_im, new_gla, new_rwkv = [], [], [], []
    for i in range(depth):
        j = i // 2
        sh_c, sc_c, gt_c = _adaln(c_ctx, ada_w[i], ada_b[i])
        sh_l, sc_l, gt_l = _adaln(c, ada_w[i], ada_b[i])
        h_ctx = _rms(x_ctx, norm_w[i]) * (1.0 + sc_c) + sh_c
        h_lat = _rms(x_lat, norm_w[i]) * (1.0 + sc_l) + sh_l
        if i % 2 == 0:
            p = (e_w_in[j], e_w_out[j], s5_lambda_re[j], s5_lambda_im[j], s5_log_step[j], s5_b_re[j], s5_b_im[j],
                 s5_c_re[j], s5_c_im[j], s5_d[j], s5_glu_w[j], s5_glu_b[j], gla_decay_up[j], gla_decay_b[j],
                 gla_norm_w[j])
            o_ctx, fr, fi, fg = _even_mixer(h_ctx, z_s5, z_s5, z_gla, *p)
            o_lat, _, _, _ = _even_mixer(h_lat, state_s5_re[:, j], state_s5_im[:, j], state_gla[:, j], *p)
            new_s5_re.append(fr)
            new_s5_im.append(fi)
            new_gla.append(fg)
        else:
            p = (o_w_in[j], o_w_out[j], rwkv_mu[j], rwkv_w0[j], rwkv_w2[j], rwkv_a0[j], rwkv_a2[j],
                 rwkv_k_k[j], rwkv_k_a[j], rwkv_r_k[j], rwkv_lnx_w[j], rwkv_lnx_b[j])
            o_ctx, fw = _odd_mixer(h_ctx, z_rwkv, *p)
            o_lat, _ = _odd_mixer(h_lat, state_rwkv[:, j], *p)
            new_rwkv.append(fw)
        x_ctx = x_ctx + gt_c * o_ctx
        x_lat = x_lat + gt_l * o_lat
    y_prompt = _rms(x_ctx, final_norm_w)
    y_sample = _rms(x_lat, final_norm_w)
    return (y_prompt, y_sample, jnp.stack(new_s5_re, axis=1), jnp.stack(new_s5_im, axis=1),
            jnp.stack(new_gla, axis=1), jnp.stack(new_rwkv, axis=1))
```

```python
import functools
import math

import jax
import jax.numpy as jnp
from jax import lax
from jax.experimental import pallas as pl
from jax.experimental.pallas import tpu as pltpu

D_MODEL = 2048
GRID_W = 64
POS_BASE = 10000.0
N_DIR = 2
EPS = 1e-6
S5_W = 1024
S5_GROUP_CH = 16
S5_GROUPS = 64
S5_STATE = 64
S5_CHUNK = 16
GLA_HEADS = 6
GLA_DV = 512
GLA_DK = 256
GLA_DK_W = 1536
GLA_DV_W = 3072
GLA_RANK = 16
GLA_NORMALIZER = 16.0
GLA_CHUNK = 64
GLA_LOG_DECAY_MIN = -1.0
EVEN_SIZES = (S5_W, S5_W, GLA_DK_W, GLA_DK_W, GLA_DV_W, GLA_DV_W, N_DIR * GLA_RANK)
RWKV_W = 2048
RWKV_HEAD = 64
RWKV_HEADS = 32
RWKV_DECAY_RANK = 96
RWKV_ICLR_RANK = 96
RWKV_LNX_EPS = 64e-5
ODD_SIZES = (RWKV_W, RWKV_W, RWKV_W, RWKV_W, N_DIR * RWKV_DECAY_RANK, N_DIR * RWKV_ICLR_RANK)
RWKV_TBLK = 64
RWKV_PAIRS = 8
RWKV_GROUP_T = 8
LANES = 128

VMEM_LIMIT = 48 * 1024 * 1024
HI = lax.Precision.HIGHEST
BF16 = jnp.bfloat16
F32 = jnp.float32


def _mm_kernel(x_ref, w_ref, o_ref):
    o_ref[...] = jnp.dot(x_ref[...], w_ref[...], preferred_element_type=F32)


def _pick(n, prefs):
    for p in prefs:
        if n % p == 0:
            return p
    return n


def matmul(x, w):
    m, k = x.shape
    n = w.shape[1]
    x = x.astype(BF16)
    w = w.astype(BF16)
    tm = _pick(m, (1024, 512, 256, 128, 64, 32, 16, 8))
    tn = _pick(n, (512, 384, 256, 128))
    return pl.pallas_call(
        _mm_kernel,
        grid=(m // tm, n // tn),
        in_specs=[pl.BlockSpec((tm, k), lambda i, j: (i, 0)),
                  pl.BlockSpec((k, tn), lambda i, j: (0, j))],
        out_specs=pl.BlockSpec((tm, tn), lambda i, j: (i, j)),
        out_shape=jax.ShapeDtypeStruct((m, n), F32),
        compiler_params=pltpu.CompilerParams(
            dimension_semantics=("arbitrary", "arbitrary"), vmem_limit_bytes=VMEM_LIMIT),
        name="proj_matmul",
    )(x, w)


def _mm3(h, w):
    b, l, k = h.shape
    return matmul(h.reshape(b * l, k), w).reshape(b, l, -1)


def s5_operators(lam_re, lam_im, log_step, b_re, b_im, c_re, c_im, d_skip):
    T = S5_CHUNK
    dt = jnp.exp(log_step)[..., None]
    mag = jnp.exp(lam_re * dt)
    ab_re, ab_im = mag * jnp.cos(lam_im * dt), mag * jnp.sin(lam_im * dt)
    den = lam_re * lam_re + lam_im * lam_im
    f_re = ((ab_re - 1.0) * lam_re + ab_im * lam_im) / den
    f_im = (ab_im * lam_re - (ab_re - 1.0) * lam_im) / den
    bb_re = f_re[..., None] * b_re - f_im[..., None] * b_im
    bb_im = f_re[..., None] * b_im + f_im[..., None] * b_re
    kk = jnp.arange(T + 1, dtype=F32)[:, None, None, None]
    pmag = jnp.exp(kk * (lam_re * dt))
    pr = pmag * jnp.cos(kk * (lam_im * dt))
    pi = pmag * jnp.sin(kk * (lam_im * dt))
    zr = pr[:T, :, :, :, None] * bb_re - pi[:T, :, :, :, None] * bb_im
    zi = pr[:T, :, :, :, None] * bb_im + pi[:T, :, :, :, None] * bb_re
    kern = (jnp.einsum('dghp,kdgpj->kdghj', c_re, zr, precision=HI)
            - jnp.einsum('dghp,kdgpj->kdghj', c_im, zi, precision=HI))
    t_idx = jnp.arange(T)[:, None]
    s_idx = jnp.arange(T)[None, :]
    lag_f = t_idx - s_idx
    lag_b = s_idx - t_idx
    m_f = jnp.where((lag_f >= 0)[:, :, None, None, None], kern[:, 0][jnp.clip(lag_f, 0, T - 1)], 0.0)
    m_b = jnp.where((lag_b >= 0)[:, :, None, None, None], kern[:, 1][jnp.clip(lag_b, 0, T - 1)], 0.0)
    m = m_f + m_b
    eye_t = jnp.eye(T, dtype=F32)[:, :, None, None, None]
    eye_h = jnp.eye(S5_GROUP_CH, dtype=F32)[None, None, None]
    m = m + eye_t * eye_h * d_skip.reshape(S5_GROUPS, S5_GROUP_CH)[None, None, :, :, None]
    g = m.shape[2]
    m_t = m.transpose(2, 1, 4, 0, 3).reshape(g, T * S5_GROUP_CH, T * S5_GROUP_CH)
    pf_r, pf_i = pr[T - 1::-1][:T, 0], pi[T - 1::-1][:T, 0]
    pb_r, pb_i = pr[:T, 1], pi[:T, 1]

    def f_mat(p_r, p_i, d):
        re = p_r[..., None] * bb_re[d][None] - p_i[..., None] * bb_im[d][None]
        im = p_r[..., None] * bb_im[d][None] + p_i[..., None] * bb_re[d][None]
        re = re.transpose(1, 0, 3, 2).reshape(g, T * S5_GROUP_CH, S5_STATE)
        im = im.transpose(1, 0, 3, 2).reshape(g, T * S5_GROUP_CH, S5_STATE)
        return re, im
    ff_re, ff_im = f_mat(pf_r, pf_i, 0)
    fb_re, fb_im = f_mat(pb_r, pb_i, 1)
    a_t = jnp.concatenate([m_t, ff_re, ff_im, fb_re, fb_im], axis=-1)
    ef_r, ef_i = pr[1:T + 1, 0], pi[1:T + 1, 0]
    eb_r, eb_i = pr[T:0:-1, 1], pi[T:0:-1, 1]

    def e_mat(p_r, p_i, d):
        er = c_re[d][None] * p_r[:, :, None, :] - c_im[d][None] * p_i[:, :, None, :]
        ei = -(c_re[d][None] * p_i[:, :, None, :] + c_im[d][None] * p_r[:, :, None, :])
        er = er.transpose(1, 3, 0, 2).reshape(g, S5_STATE, T * S5_GROUP_CH)
        ei = ei.transpose(1, 3, 0, 2).reshape(g, S5_STATE, T * S5_GROUP_CH)
        return er, ei
    efr, efi = e_mat(ef_r, ef_i, 0)
    ebr, ebi = e_mat(eb_r, eb_i, 1)
    e_t = jnp.concatenate([efr, efi, ebr, ebi], axis=1)
    lam_t = jnp.concatenate([pr[T, 0], pi[T, 0], pr[T, 1], pi[T, 1]], axis=-1)[:, None, :]
    return a_t.astype(BF16), e_t.astype(BF16), lam_t


def _s5_kernel(n_chunks, bsz, ut_ref, at_ref, et_ref, lam_ref, h0_ref, y_ref, hfin_ref, z_ref, hent_ref):
    P = S5_STATE
    z_ref[...] = jnp.dot(ut_ref[0], at_ref[0], preferred_element_type=F32)
    lam = lam_ref[0]
    fr, fi = lam[:, 0:P], lam[:, P:2 * P]
    br, bi = lam[:, 2 * P:3 * P], lam[:, 3 * P:4 * P]
    h0 = h0_ref[0]

    def step(c, carry):
        hfr, hfi, hbr, hbi = carry
        rf = pl.ds(pl.multiple_of(c * bsz, 8), bsz)
        cb = n_chunks - 1 - c
        rb = pl.ds(pl.multiple_of(cb * bsz, 8), bsz)
        hent_ref[rf, 0:P] = hfr
        hent_ref[rf, P:2 * P] = hfi
        hent_ref[rb, 2 * P:3 * P] = hbr
        hent_ref[rb, 3 * P:4 * P] = hbi
        gfr = z_ref[rf, 4 * P:5 * P]
        gfi = z_ref[rf, 5 * P:6 * P]
        gbr = z_ref[rb, 6 * P:7 * P]
        gbi = z_ref[rb, 7 * P:8 * P]
        return (fr * hfr - fi * hfi + gfr, fr * hfi + fi * hfr + gfi,
                br * hbr - bi * hbi + gbr, br * hbi + bi * hbr + gbi)
    init = (h0[:, 0:P], h0[:, P:2 * P], h0[:, 2 * P:3 * P], h0[:, 3 * P:4 * P])
    hfr, hfi, hbr, hbi = lax.fori_loop(0, n_chunks, step, init)
    hfin_ref[0, :, 0:P] = hfr
    hfin_ref[0, :, P:2 * P] = hfi
    hfin_ref[0, :, 2 * P:3 * P] = hbr
    hfin_ref[0, :, 3 * P:4 * P] = hbi
    y_ref[0] = z_ref[:, 0:4 * P] + jnp.dot(hent_ref[...].astype(BF16), et_ref[0], preferred_element_type=F32)


def s5_scan(u, ops, h0_re, h0_im):
    a_t, e_t, lam_t = ops
    b_real, L, _ = u.shape
    T, G, H, P = S5_CHUNK, S5_GROUPS, S5_GROUP_CH, S5_STATE
    n = L // T
    bsz = -(-b_real // 8) * 8
    cols = n * bsz
    ut = u.reshape(b_real, n, T, G, H).transpose(3, 1, 0, 2, 4).astype(BF16)
    ut = jnp.pad(ut, ((0, 0), (0, 0), (0, bsz - b_real), (0, 0), (0, 0))).reshape(G, cols, T * H)
    h0 = jnp.concatenate([h0_re[:, 0], h0_im[:, 0], h0_re[:, 1], h0_im[:, 1]], axis=-1)
    h0 = jnp.pad(h0.transpose(1, 0, 2), ((0, 0), (0, bsz - b_real), (0, 0)))
    yt, hfin = pl.pallas_call(
        functools.partial(_s5_kernel, n, bsz),
        grid=(G,),
        in_specs=[pl.BlockSpec((1, cols, T * H), lambda g: (g, 0, 0)),
                  pl.BlockSpec((1, T * H, 8 * P), lambda g: (g, 0, 0)),
                  pl.BlockSpec((1, 4 * P, T * H), lambda g: (g, 0, 0)),
                  pl.BlockSpec((1, 1, 4 * P), lambda g: (g, 0, 0)),
                  pl.BlockSpec((1, bsz, 4 * P), lambda g: (g, 0, 0))],
        out_specs=[pl.BlockSpec((1, cols, T * H), lambda g: (g, 0, 0)),
                   pl.BlockSpec((1, bsz, 4 * P), lambda g: (g, 0, 0))],
        out_shape=[jax.ShapeDtypeStruct((G, cols, T * H), F32),
                   jax.ShapeDtypeStruct((G, bsz, 4 * P), F32)],
        scratch_shapes=[pltpu.VMEM((cols, 8 * P), F32), pltpu.VMEM((cols, 4 * P), F32)],
        compiler_params=pltpu.CompilerParams(dimension_semantics=("arbitrary",), vmem_limit_bytes=VMEM_LIMIT),
        name="s5_chunk_scan",
    )(ut, a_t, e_t, lam_t, h0)
    y = yt.reshape(G, n, bsz, T, H)[:, :, :b_real].transpose(2, 1, 3, 0, 4).reshape(b_real, L, G * H)
    hfin = hfin[:, :b_real].transpose(1, 0, 2)
    fin_re = jnp.stack([hfin[..., 0:P], hfin[..., 2 * P:3 * P]], axis=1)
    fin_im = jnp.stack([hfin[..., P:2 * P], hfin[..., 3 * P:4 * P]], axis=1)
    return y, fin_re, fin_im


def _dot_t(a, b):
    return lax.dot_general(a, b, (((1,), (1,)), ((), ())), preferred_element_type=F32)


def _gla_kernel(n_chunks, q_ref, k_ref, v_ref, g_ref, lr_ref, up_ref, db_ref, nw_ref, s0_ref,
                out_ref, sfin_ref, s_ref, of_ref):
    C = GLA_CHUNK
    d = pl.program_id(2)
    c = pl.program_id(3)
    cidx = jnp.where(d == 0, c, n_chunks - 1 - c)

    @pl.when(c == 0)
    def _():
        s_ref[...] = s0_ref[0, 0, 0]

    z = jnp.dot(lr_ref[0], up_ref[0], precision=HI, preferred_element_type=F32) + db_ref[0]
    gc = jnp.maximum(jax.nn.log_sigmoid(z) * (1.0 / GLA_NORMALIZER), GLA_LOG_DECAY_MIN)
    row = lax.broadcasted_iota(jnp.int32, (C, C), 0)
    col = lax.broadcasted_iota(jnp.int32, (C, C), 1)
    seen = jnp.where(d == 0, row - col, col - row) >= 0
    bcum = jnp.dot(seen.astype(F32), gc, precision=HI, preferred_element_type=F32)
    b_last = jnp.sum(gc, axis=0, keepdims=True)
    q = q_ref[0] * (GLA_DK ** -0.5)
    k = k_ref[0]
    v = v_ref[0].astype(BF16)
    q_dec = (q * jnp.exp(bcum)).astype(BF16)
    k_inv = (k * jnp.exp(-bcum)).astype(BF16)
    k_end = (k * jnp.exp(b_last - bcum)).astype(BF16)
    att = jnp.where(seen, _dot_t(q_dec, k_inv), 0.0).astype(BF16)
    s_old = s_ref[...]
    o = (jnp.dot(att, v, preferred_element_type=F32)
         + jnp.dot(q_dec, s_old.astype(BF16), preferred_element_type=F32))
    eye = (lax.broadcasted_iota(jnp.int32, (GLA_DK, GLA_DK), 0)
           == lax.broadcasted_iota(jnp.int32, (GLA_DK, GLA_DK), 1)).astype(F32)
    bl_col = lax.dot_general(eye, jnp.broadcast_to(b_last, (8, GLA_DK)), (((1,), (1,)), ((), ())),
                             precision=HI, preferred_element_type=F32)[:, 0:1]
    kv = lax.dot_general(k_end, v, (((0,), (0,)), ((), ())), preferred_element_type=F32)
    s_ref[...] = jnp.exp(bl_col) * s_old + kv

    rows = pl.ds(pl.multiple_of(cidx * C, C), C)

    @pl.when(d == 0)
    def _():
        of_ref[rows, :] = o

    @pl.when(d == 1)
    def _():
        tot = of_ref[rows, :] + o
        nrm = tot * lax.rsqrt(jnp.mean(tot * tot, axis=-1, keepdims=True) + EPS) * nw_ref[0]
        gate = g_ref[0]
        out_ref[0] = (nrm * (gate * jax.nn.sigmoid(gate))).astype(out_ref.dtype)

    @pl.when(c == n_chunks - 1)
    def _():
        sfin_ref[0, 0, 0] = s_ref[...]


def gla_mix(main, dec_lr, dec_up, dec_b, gla_nw, s0):
    bsz, L, _ = main.shape
    C, H, DK, DV = GLA_CHUNK, GLA_HEADS, GLA_DK, GLA_DV
    n = L // C
    q_blk = sum(EVEN_SIZES[:2]) // DK
    k_blk = sum(EVEN_SIZES[:3]) // DK
    v_blk = sum(EVEN_SIZES[:4]) // DV
    g_blk = sum(EVEN_SIZES[:5]) // DV
    up = jnp.zeros((N_DIR, LANES, GLA_DK_W), F32)
    for d in range(N_DIR):
        up = up.at[d, d * GLA_RANK:(d + 1) * GLA_RANK].set(dec_up[d])
    db = dec_b.reshape(N_DIR, 1, GLA_DK_W)
    nw = gla_nw.reshape(1, GLA_DV_W)

    def chunk(d, c):
        return c + d * (n - 1 - 2 * c)

    def out_chunk(d, c):
        return (n - 1) - d * c
    out, sfin = pl.pallas_call(
        functools.partial(_gla_kernel, n),
        grid=(bsz, H, N_DIR, n),
        in_specs=[pl.BlockSpec((1, C, DK), lambda b, h, d, c: (b, chunk(d, c), q_blk + h)),
                  pl.BlockSpec((1, C, DK), lambda b, h, d, c: (b, chunk(d, c), k_blk + h)),
                  pl.BlockSpec((1, C, DV), lambda b, h, d, c: (b, chunk(d, c), v_blk + h)),
                  pl.BlockSpec((1, C, DV), lambda b, h, d, c: (b, chunk(d, c), g_blk + h)),
                  pl.BlockSpec((1, C, LANES), lambda b, h, d, c: (b, chunk(d, c), 0)),
                  pl.BlockSpec((1, LANES, DK), lambda b, h, d, c: (d, 0, h)),
                  pl.BlockSpec((1, 1, DK), lambda b, h, d, c: (d, 0, h)),
                  pl.BlockSpec((1, DV), lambda b, h, d, c: (0, h)),
                  pl.BlockSpec((1, 1, 1, DK, DV), lambda b, h, d, c: (b, d, h, 0, 0))],
        out_specs=[pl.BlockSpec((1, C, DV), lambda b, h, d, c: (b, out_chunk(d, c), h)),
                   pl.BlockSpec((1, 1, 1, DK, DV), lambda b, h, d, c: (b, d, h, 0, 0))],
        out_shape=[jax.ShapeDtypeStruct((bsz, L, GLA_DV_W), BF16),
                   jax.ShapeDtypeStruct((bsz, N_DIR, H, DK, DV), F32)],
        scratch_shapes=[pltpu.VMEM((DK, DV), F32), pltpu.VMEM((L, DV), F32)],
        compiler_params=pltpu.CompilerParams(
            dimension_semantics=("arbitrary",) * 4, vmem_limit_bytes=VMEM_LIMIT),
        name="gla_chunk_scan",
    )(main, main, main, main, dec_lr, up, db, nw, s0)
    return out, sfin


def _rwkv_kernel(n_blk, r_ref, v_ref, a_ref, w_ref, k_ref, b_ref, s0_ref, y_ref, sfin_ref,
                 s_ref, vc_ref, sr_ref):
    TB, N, GT = RWKV_TBLK, RWKV_HEAD, RWKV_GROUP_T
    d = pl.program_id(2)
    c = pl.program_id(3)

    @pl.when(c == 0)
    def _():
        s_ref[...] = s0_ref[0, 0]

    lane = lax.broadcasted_iota(jnp.int32, (N, LANES), 1)
    row = lax.broadcasted_iota(jnp.int32, (N, LANES), 0)
    lo_half = lane < N
    diag = (lane % N) == row
    same_head = ((lax.broadcasted_iota(jnp.int32, (LANES, LANES), 0) < N)
                 == (lax.broadcasted_iota(jnp.int32, (LANES, LANES), 1) < N)).astype(BF16)
    fwd = d == 0
    y_ref[...] = jnp.zeros_like(y_ref)

    def row_of(tile, j):
        return jnp.where(fwd, tile[j:j + 1], tile[GT - 1 - j:GT - j])

    def group(i, carry):
        g8 = jnp.where(fwd, i, TB // GT - 1 - i)
        rs = pl.ds(pl.multiple_of(g8 * GT, GT), GT)
        v_tile = v_ref[0, rs, :]
        for p in range(RWKV_PAIRS):
            ls = slice(p * LANES, (p + 1) * LANES)
            vd = jnp.concatenate([jnp.where(diag, row_of(v_tile, j)[:, ls], 0.0) for j in range(GT)], axis=0)
            vd_hi = vd.astype(BF16)
            vd_lo = (vd - vd_hi.astype(F32)).astype(BF16)
            vcol = (jnp.dot(vd_hi, same_head, preferred_element_type=F32)
                    + jnp.dot(vd_lo, same_head, preferred_element_type=F32))
            vc_ref[:, ls] = vcol
        tiles = (r_ref[0, rs, :], a_ref[0, rs, :], w_ref[0, 0, rs, :], k_ref[0, 0, rs, :], b_ref[0, 0, rs, :])
        for j in range(GT):
            r_row, a_row, w_row, k_row, b_row = (row_of(x, j) for x in tiles)
            for p in range(RWKV_PAIRS):
                ls = slice(p * LANES, (p + 1) * LANES)
                s = s_ref[:, ls]
                prod = s * a_row[:, ls]
                sa_lo = jnp.sum(jnp.where(lo_half, prod, 0.0), axis=1, keepdims=True)
                sa_hi = jnp.sum(jnp.where(lo_half, 0.0, prod), axis=1, keepdims=True)
                sa = jnp.where(lo_half, sa_lo, sa_hi)
                s_new = s * w_row[:, ls] + sa * b_row[:, ls] + vc_ref[j * N:(j + 1) * N, ls] * k_row[:, ls]
                s_ref[:, ls] = s_new
                sr_ref[j * N:(j + 1) * N, ls] = (s_new * r_row[:, ls]).astype(BF16)
        for p in range(RWKV_PAIRS):
            ls = slice(p * LANES, (p + 1) * LANES)
            ycol = jnp.dot(sr_ref[:, ls], same_head, preferred_element_type=F32)
            acc = jnp.zeros((N, LANES), F32)
            for j in range(GT):
                t = g8 * GT + jnp.where(fwd, j, GT - 1 - j)
                acc = jnp.where((lane % N) == t, ycol[j * N:(j + 1) * N], acc)
            y_ref[0, 0, 0, :, ls] = jnp.where((lane % N) // GT == g8, acc, y_ref[0, 0, 0, :, ls])
        return carry
    lax.fori_loop(0, TB // GT, group, 0)

    @pl.when(c == n_blk - 1)
    def _():
        sfin_ref[0, 0] = s_ref[...]


def rwkv_scan(r, v, a_neg, decay, k_d, b_d, s0):
    bsz, L, W = r.shape
    TB, N, H = RWKV_TBLK, RWKV_HEAD, RWKV_HEADS
    n = L // TB
    gw = RWKV_PAIRS * LANES
    s0t = s0.transpose(0, 1, 3, 2, 4).reshape(bsz, N_DIR, N, W)

    def blk(d, c):
        return c + d * (n - 1 - 2 * c)
    seq = pl.BlockSpec((1, TB, gw), lambda b, g, d, c: (b, blk(d, c), g))
    seq_d = pl.BlockSpec((1, 1, TB, gw), lambda b, g, d, c: (d, b, blk(d, c), g))
    st = pl.BlockSpec((1, 1, N, gw), lambda b, g, d, c: (b, d, 0, g))
    gt = RWKV_GROUP_T
    yt, sfin = pl.pallas_call(
        functools.partial(_rwkv_kernel, n),
        grid=(bsz, W // gw, N_DIR, n),
        in_specs=[seq, seq, seq, seq_d, seq_d, seq_d, st],
        out_specs=[pl.BlockSpec((1, 1, 1, N, gw), lambda b, g, d, c: (d, b, blk(d, c), 0, g)), st],
        out_shape=[jax.ShapeDtypeStruct((N_DIR, bsz, n, N, W), F32),
                   jax.ShapeDtypeStruct((bsz, N_DIR, N, W), F32)],
        scratch_shapes=[pltpu.VMEM((N, gw), F32), pltpu.VMEM((gt * N, gw), F32), pltpu.VMEM((gt * N, gw), BF16)],
        compiler_params=pltpu.CompilerParams(
            dimension_semantics=("arbitrary",) * 4, vmem_limit_bytes=VMEM_LIMIT),
        name="rwkv_scan",
    )(r, v, a_neg, decay, k_d, b_d, s0t)
    yt = yt[0] + yt[1]
    wkv = yt.reshape(bsz, n, N, W // LANES, 2, TB).transpose(0, 1, 5, 3, 4, 2).reshape(bsz, L, W)
    sfin = sfin.reshape(bsz, N_DIR, N, H, N).transpose(0, 1, 3, 2, 4)
    return wkv, sfin


def _split_cols(t, sizes):
    offsets, acc = [], 0
    for s in sizes[:-1]:
        acc += s
        offsets.append(acc)
    return jnp.split(t, offsets, axis=-1)


def _rms(x, w):
    return x * lax.rsqrt(jnp.mean(x * x, axis=-1, keepdims=True) + EPS) * w


def _adaln(cond, w, b):
    m = jnp.dot(jax.nn.silu(cond), w, precision=HI) + b
    shift, scale, gate = jnp.split(m, 3, axis=-1)
    return shift[..., None, :], scale[..., None, :], gate[..., None, :]


def _grid_pos_embed(n_tokens):
    rows = n_tokens // GRID_W
    row_id = jnp.broadcast_to(jnp.arange(rows, dtype=F32)[:, None], (rows, GRID_W)).reshape(-1)
    col_id = jnp.broadcast_to(jnp.arange(GRID_W, dtype=F32)[None, :], (rows, GRID_W)).reshape(-1)
    quarter = D_MODEL // 4
    omega = 1.0 / (POS_BASE ** (jnp.arange(quarter, dtype=F32) / quarter))

    def axis_emb(pos):
        ang = pos[:, None] * omega[None, :]
        return jnp.concatenate([jnp.sin(ang), jnp.cos(ang)], axis=-1)
    return jnp.concatenate([axis_emb(row_id), axis_emb(col_id)], axis=-1)


def _even_mixer(h, s5_re0, s5_im0, gla0, w_in, w_out, s5_ops, glu_w, glu_b, dec_up, dec_b, gla_nw):
    bsz, L, _ = h.shape
    n_main = sum(EVEN_SIZES[:-1])
    main = _mm3(h, w_in[:, :n_main])
    w_tail = jnp.pad(w_in[:, n_main:], ((0, 0), (0, LANES - N_DIR * GLA_RANK)))
    dec_lr = _mm3(h, w_tail)
    u, s5_g = main[..., :S5_W], main[..., S5_W:2 * S5_W]
    y, fin_re, fin_im = s5_scan(u, s5_ops, s5_re0, s5_im0)
    gy = jax.nn.gelu(y)
    s5_out = gy * jax.nn.sigmoid(_mm3(gy, glu_w) + glu_b) * jax.nn.silu(s5_g)
    gla_out, fin_gla = gla_mix(main, dec_lr, dec_up, dec_b, gla_nw, gla0)
    out = _mm3(jnp.concatenate([s5_out.astype(BF16), gla_out], axis=-1), w_out)
    return out, fin_re, fin_im, fin_gla


def _odd_mixer(h, rwkv0, w_in, w_out, mu, w0, w2, a0, a2, k_k, k_a, r_k, lnx_w, lnx_b):
    bsz, L, _ = h.shape
    zero = jnp.zeros_like(h[:, :1])
    h_prev = jnp.concatenate([zero, h[:, :-1]], axis=1)
    h_next = jnp.concatenate([h[:, 1:], zero], axis=1)
    xs = h + mu[0] * (h_prev - h) + mu[1] * (h_next - h)
    n_main = sum(ODD_SIZES[:4])
    main = _mm3(xs, w_in[:, :n_main])
    tail = _mm3(xs, w_in[:, n_main:])
    r, k, v, g = _split_cols(main, ODD_SIZES[:4])
    w_lr, a_lr = _split_cols(tail, ODD_SIZES[4:])
    w_lr = jnp.tanh(w_lr).reshape(bsz, L, N_DIR, RWKV_DECAY_RANK)
    a_lr = a_lr.reshape(bsz, L, N_DIR, RWKV_ICLR_RANK)

    def heads(t):
        return t.reshape(bsz, L, RWKV_HEADS, RWKV_HEAD)
    kk = heads(k * k_k)
    kk = kk / jnp.maximum(jnp.sqrt(jnp.sum(kk * kk, axis=-1, keepdims=True)), 1e-12)
    kk = kk.reshape(bsz, L, RWKV_W)
    r_h, v_h = heads(r), heads(v)
    decays, k_ds, b_ds = [], [], []
    bonus = 0.0
    for d in range(N_DIR):
        w_log = -jax.nn.softplus(-(w0[d] + _mm3(w_lr[:, :, d], w2[d]))) - 0.5
        decays.append(jnp.exp(-jnp.exp(w_log)))
        a = jax.nn.sigmoid(a0[d] + _mm3(a_lr[:, :, d], a2[d]))
        k_d = k * (1.0 + (a - 1.0) * k_a)
        k_ds.append(k_d)
        b_ds.append(kk * a)
        bonus = bonus + jnp.sum(r_h * heads(k_d) * r_k, axis=-1, keepdims=True) * v_h
    wkv, finals = rwkv_scan(r, v, -kk, jnp.stack(decays), jnp.stack(k_ds), jnp.stack(b_ds), rwkv0)
    wkv = heads(wkv)
    mean = jnp.mean(wkv, axis=-1, keepdims=True)
    var = jnp.mean(jnp.square(wkv - mean), axis=-1, keepdims=True)
    ln = ((wkv - mean) * lax.rsqrt(var + RWKV_LNX_EPS) * lnx_w.reshape(RWKV_HEADS, RWKV_HEAD)
          + lnx_b.reshape(RWKV_HEADS, RWKV_HEAD))
    out = (ln + bonus).reshape(bsz, L, RWKV_W) * jax.nn.silu(g)
    return _mm3(out, w_out), finals


def kernel(x_prompt, x_sample, state_s5_re, state_s5_im, state_gla, state_rwkv, c, c_ctx, norm_w, ada_w, ada_b, final_norm_w, e_w_in, e_w_out, s5_lambda_re, s5_lambda_im, s5_log_step, s5_b_re, s5_b_im, s5_c_re, s5_c_im, s5_d, s5_glu_w, s5_glu_b, gla_decay_up, gla_decay_b, gla_norm_w, o_w_in, o_w_out, rwkv_mu, rwkv_w0, rwkv_w2, rwkv_a0, rwkv_a2, rwkv_k_k, rwkv_k_a, rwkv_r_k, rwkv_lnx_w, rwkv_lnx_b):
    bp = x_prompt.shape[0]
    depth = norm_w.shape[0]
    x_ctx = x_prompt
    x_lat = x_sample + _grid_pos_embed(x_sample.shape[1])[None]
    z_s5 = jnp.zeros((bp, N_DIR, S5_GROUPS, S5_STATE), F32)
    z_gla = jnp.zeros((bp, N_DIR, GLA_HEADS, GLA_DK, GLA_DV), F32)
    z_rwkv = jnp.zeros((bp, N_DIR, RWKV_HEADS, RWKV_HEAD, RWKV_HEAD), F32)
    new_s5_re, new_s5_im, new_gla, new_rwkv = [], [], [], []
    for i in range(depth):
        j = i // 2
        sh_c, sc_c, gt_c = _adaln(c_ctx, ada_w[i], ada_b[i])
        sh_l, sc_l, gt_l = _adaln(c, ada_w[i], ada_b[i])
        h_ctx = _rms(x_ctx, norm_w[i]) * (1.0 + sc_c) + sh_c
        h_lat = _rms(x_lat, norm_w[i]) * (1.0 + sc_l) + sh_l
        if i % 2 == 0:
            s5_ops = s5_operators(s5_lambda_re[j], s5_lambda_im[j], s5_log_step[j], s5_b_re[j], s5_b_im[j],
                                  s5_c_re[j], s5_c_im[j], s5_d[j])
            p = (e_w_in[j], e_w_out[j], s5_ops, s5_glu_w[j], s5_glu_b[j], gla_decay_up[j], gla_decay_b[j],
                 gla_norm_w[j])
            o_ctx, fr, fi, fg = _even_mixer(h_ctx, z_s5, z_s5, z_gla, *p)
            o_lat, _, _, _ = _even_mixer(h_lat, state_s5_re[:, j], state_s5_im[:, j], state_gla[:, j], *p)
            new_s5_re.append(fr)
            new_s5_im.append(fi)
            new_gla.append(fg)
        else:
            p = (o_w_in[j], o_w_out[j], rwkv_mu[j], rwkv_w0[j], rwkv_w2[j], rwkv_a0[j], rwkv_a2[j],
                 rwkv_k_k[j], rwkv_k_a[j], rwkv_r_k[j], rwkv_lnx_w[j], rwkv_lnx_b[j])
            o_ctx, fw = _odd_mixer(h_ctx, z_rwkv, *p)
            o_lat, _ = _odd_mixer(h_lat, state_rwkv[:, j], *p)
            new_rwkv.append(fw)
        x_ctx = x_ctx + gt_c * o_ctx
        x_lat = x_lat + gt_l * o_lat
    y_prompt = _rms(x_ctx, final_norm_w)
    y_sample = _rms(x_lat, final_norm_w)
    return (y_prompt, y_sample, jnp.stack(new_s5_re, axis=1), jnp.stack(new_s5_im, axis=1),
            jnp.stack(new_gla, axis=1), jnp.stack(new_rwkv, axis=1))
```

```python
import functools
import math

import jax
import jax.numpy as jnp
from jax import lax
from jax.experimental import pallas as pl
from jax.experimental.pallas import tpu as pltpu

D_MODEL = 2048
GRID_W = 64
POS_BASE = 10000.0
N_DIR = 2
EPS = 1e-6
S5_W = 1024
S5_GROUP_CH = 16
S5_GROUPS = 64
S5_STATE = 64
S5_CHUNK = 16
GLA_HEADS = 6
GLA_DV = 512
GLA_DK = 256
GLA_DK_W = 1536
GLA_DV_W = 3072
GLA_RANK = 16
GLA_NORMALIZER = 16.0
GLA_CHUNK = 64
GLA_LOG_DECAY_MIN = -1.0
EVEN_SIZES = (S5_W, S5_W, GLA_DK_W, GLA_DK_W, GLA_DV_W, GLA_DV_W, N_DIR * GLA_RANK)
RWKV_W = 2048
RWKV_HEAD = 64
RWKV_HEADS = 32
RWKV_DECAY_RANK = 96
RWKV_ICLR_RANK = 96
RWKV_LNX_EPS = 64e-5
ODD_SIZES = (RWKV_W, RWKV_W, RWKV_W, RWKV_W, N_DIR * RWKV_DECAY_RANK, N_DIR * RWKV_ICLR_RANK)
RWKV_TBLK = 64
RWKV_PAIRS = 8
RWKV_GROUP_T = 8
RWKV_CHUNK = 64
RWKV_CPAIRS = 8
RWKV_SUB = 16
LANES = 128

VMEM_LIMIT = 48 * 1024 * 1024
HI = lax.Precision.HIGHEST
BF16 = jnp.bfloat16
F32 = jnp.float32


def _mm_kernel(x_ref, w_ref, o_ref):
    o_ref[...] = jnp.dot(x_ref[...], w_ref[...], preferred_element_type=F32)


def _pick(n, prefs):
    for p in prefs:
        if n % p == 0:
            return p
    return n


def matmul(x, w):
    m, k = x.shape
    n = w.shape[1]
    x = x.astype(BF16)
    w = w.astype(BF16)
    tm = _pick(m, (1024, 512, 256, 128, 64, 32, 16, 8))
    tn = _pick(n, (512, 384, 256, 128))
    return pl.pallas_call(
        _mm_kernel,
        grid=(m // tm, n // tn),
        in_specs=[pl.BlockSpec((tm, k), lambda i, j: (i, 0)),
                  pl.BlockSpec((k, tn), lambda i, j: (0, j))],
        out_specs=pl.BlockSpec((tm, tn), lambda i, j: (i, j)),
        out_shape=jax.ShapeDtypeStruct((m, n), F32),
        compiler_params=pltpu.CompilerParams(
            dimension_semantics=("arbitrary", "arbitrary"), vmem_limit_bytes=VMEM_LIMIT),
        name="proj_matmul",
    )(x, w)


def _mm3(h, w):
    b, l, k = h.shape
    return matmul(h.reshape(b * l, k), w).reshape(b, l, -1)


def s5_operators(lam_re, lam_im, log_step, b_re, b_im, c_re, c_im, d_skip):
    T = S5_CHUNK
    dt = jnp.exp(log_step)[..., None]
    mag = jnp.exp(lam_re * dt)
    ab_re, ab_im = mag * jnp.cos(lam_im * dt), mag * jnp.sin(lam_im * dt)
    den = lam_re * lam_re + lam_im * lam_im
    f_re = ((ab_re - 1.0) * lam_re + ab_im * lam_im) / den
    f_im = (ab_im * lam_re - (ab_re - 1.0) * lam_im) / den
    bb_re = f_re[..., None] * b_re - f_im[..., None] * b_im
    bb_im = f_re[..., None] * b_im + f_im[..., None] * b_re
    kk = jnp.arange(T + 1, dtype=F32)[:, None, None, None]
    pmag = jnp.exp(kk * (lam_re * dt))
    pr = pmag * jnp.cos(kk * (lam_im * dt))
    pi = pmag * jnp.sin(kk * (lam_im * dt))
    zr = pr[:T, :, :, :, None] * bb_re - pi[:T, :, :, :, None] * bb_im
    zi = pr[:T, :, :, :, None] * bb_im + pi[:T, :, :, :, None] * bb_re
    kern = (jnp.einsum('dghp,kdgpj->kdghj', c_re, zr, precision=HI)
            - jnp.einsum('dghp,kdgpj->kdghj', c_im, zi, precision=HI))
    t_idx = jnp.arange(T)[:, None]
    s_idx = jnp.arange(T)[None, :]
    lag_f = t_idx - s_idx
    lag_b = s_idx - t_idx
    m_f = jnp.where((lag_f >= 0)[:, :, None, None, None], kern[:, 0][jnp.clip(lag_f, 0, T - 1)], 0.0)
    m_b = jnp.where((lag_b >= 0)[:, :, None, None, None], kern[:, 1][jnp.clip(lag_b, 0, T - 1)], 0.0)
    m = m_f + m_b
    eye_t = jnp.eye(T, dtype=F32)[:, :, None, None, None]
    eye_h = jnp.eye(S5_GROUP_CH, dtype=F32)[None, None, None]
    m = m + eye_t * eye_h * d_skip.reshape(S5_GROUPS, S5_GROUP_CH)[None, None, :, :, None]
    g = m.shape[2]
    m_t = m.transpose(2, 1, 4, 0, 3).reshape(g, T * S5_GROUP_CH, T * S5_GROUP_CH)
    pf_r, pf_i = pr[T - 1::-1][:T, 0], pi[T - 1::-1][:T, 0]
    pb_r, pb_i = pr[:T, 1], pi[:T, 1]

    def f_mat(p_r, p_i, d):
        re = p_r[..., None] * bb_re[d][None] - p_i[..., None] * bb_im[d][None]
        im = p_r[..., None] * bb_im[d][None] + p_i[..., None] * bb_re[d][None]
        re = re.transpose(1, 0, 3, 2).reshape(g, T * S5_GROUP_CH, S5_STATE)
        im = im.transpose(1, 0, 3, 2).reshape(g, T * S5_GROUP_CH, S5_STATE)
        return re, im
    ff_re, ff_im = f_mat(pf_r, pf_i, 0)
    fb_re, fb_im = f_mat(pb_r, pb_i, 1)
    a_t = jnp.concatenate([m_t, ff_re, ff_im, fb_re, fb_im], axis=-1)
    ef_r, ef_i = pr[1:T + 1, 0], pi[1:T + 1, 0]
    eb_r, eb_i = pr[T:0:-1, 1], pi[T:0:-1, 1]

    def e_mat(p_r, p_i, d):
        er = c_re[d][None] * p_r[:, :, None, :] - c_im[d][None] * p_i[:, :, None, :]
        ei = -(c_re[d][None] * p_i[:, :, None, :] + c_im[d][None] * p_r[:, :, None, :])
        er = er.transpose(1, 3, 0, 2).reshape(g, S5_STATE, T * S5_GROUP_CH)
        ei = ei.transpose(1, 3, 0, 2).reshape(g, S5_STATE, T * S5_GROUP_CH)
        return er, ei
    efr, efi = e_mat(ef_r, ef_i, 0)
    ebr, ebi = e_mat(eb_r, eb_i, 1)
    e_t = jnp.concatenate([efr, efi, ebr, ebi], axis=1)
    lam_t = jnp.concatenate([pr[T, 0], pi[T, 0], pr[T, 1], pi[T, 1]], axis=-1)[:, None, :]
    return a_t.astype(BF16), e_t.astype(BF16), lam_t


def _s5_kernel(n_chunks, bsz, ut_ref, at_ref, et_ref, lam_ref, h0_ref, y_ref, hfin_ref, z_ref, hent_ref):
    P = S5_STATE
    z_ref[...] = jnp.dot(ut_ref[0], at_ref[0], preferred_element_type=F32)
    lam = lam_ref[0]
    fr, fi = lam[:, 0:P], lam[:, P:2 * P]
    br, bi = lam[:, 2 * P:3 * P], lam[:, 3 * P:4 * P]
    h0 = h0_ref[0]

    def step(c, carry):
        hfr, hfi, hbr, hbi = carry
        rf = pl.ds(pl.multiple_of(c * bsz, 8), bsz)
        cb = n_chunks - 1 - c
        rb = pl.ds(pl.multiple_of(cb * bsz, 8), bsz)
        hent_ref[rf, 0:P] = hfr
        hent_ref[rf, P:2 * P] = hfi
        hent_ref[rb, 2 * P:3 * P] = hbr
        hent_ref[rb, 3 * P:4 * P] = hbi
        gfr = z_ref[rf, 4 * P:5 * P]
        gfi = z_ref[rf, 5 * P:6 * P]
        gbr = z_ref[rb, 6 * P:7 * P]
        gbi = z_ref[rb, 7 * P:8 * P]
        return (fr * hfr - fi * hfi + gfr, fr * hfi + fi * hfr + gfi,
                br * hbr - bi * hbi + gbr, br * hbi + bi * hbr + gbi)
    init = (h0[:, 0:P], h0[:, P:2 * P], h0[:, 2 * P:3 * P], h0[:, 3 * P:4 * P])
    hfr, hfi, hbr, hbi = lax.fori_loop(0, n_chunks, step, init)
    hfin_ref[0, :, 0:P] = hfr
    hfin_ref[0, :, P:2 * P] = hfi
    hfin_ref[0, :, 2 * P:3 * P] = hbr
    hfin_ref[0, :, 3 * P:4 * P] = hbi
    y_ref[0] = z_ref[:, 0:4 * P] + jnp.dot(hent_ref[...].astype(BF16), et_ref[0], preferred_element_type=F32)


def s5_scan(u, ops, h0_re, h0_im):
    a_t, e_t, lam_t = ops
    b_real, L, _ = u.shape
    T, G, H, P = S5_CHUNK, S5_GROUPS, S5_GROUP_CH, S5_STATE
    n = L // T
    bsz = -(-b_real // 8) * 8
    cols = n * bsz
    ut = u.reshape(b_real, n, T, G, H).transpose(3, 1, 0, 2, 4).astype(BF16)
    ut = jnp.pad(ut, ((0, 0), (0, 0), (0, bsz - b_real), (0, 0), (0, 0))).reshape(G, cols, T * H)
    h0 = jnp.concatenate([h0_re[:, 0], h0_im[:, 0], h0_re[:, 1], h0_im[:, 1]], axis=-1)
    h0 = jnp.pad(h0.transpose(1, 0, 2), ((0, 0), (0, bsz - b_real), (0, 0)))
    yt, hfin = pl.pallas_call(
        functools.partial(_s5_kernel, n, bsz),
        grid=(G,),
        in_specs=[pl.BlockSpec((1, cols, T * H), lambda g: (g, 0, 0)),
                  pl.BlockSpec((1, T * H, 8 * P), lambda g: (g, 0, 0)),
                  pl.BlockSpec((1, 4 * P, T * H), lambda g: (g, 0, 0)),
                  pl.BlockSpec((1, 1, 4 * P), lambda g: (g, 0, 0)),
                  pl.BlockSpec((1, bsz, 4 * P), lambda g: (g, 0, 0))],
        out_specs=[pl.BlockSpec((1, cols, T * H), lambda g: (g, 0, 0)),
                   pl.BlockSpec((1, bsz, 4 * P), lambda g: (g, 0, 0))],
        out_shape=[jax.ShapeDtypeStruct((G, cols, T * H), F32),
                   jax.ShapeDtypeStruct((G, bsz, 4 * P), F32)],
        scratch_shapes=[pltpu.VMEM((cols, 8 * P), F32), pltpu.VMEM((cols, 4 * P), F32)],
        compiler_params=pltpu.CompilerParams(dimension_semantics=("arbitrary",), vmem_limit_bytes=VMEM_LIMIT),
        name="s5_chunk_scan",
    )(ut, a_t, e_t, lam_t, h0)
    y = yt.reshape(G, n, bsz, T, H)[:, :, :b_real].transpose(2, 1, 3, 0, 4).reshape(b_real, L, G * H)
    hfin = hfin[:, :b_real].transpose(1, 0, 2)
    fin_re = jnp.stack([hfin[..., 0:P], hfin[..., 2 * P:3 * P]], axis=1)
    fin_im = jnp.stack([hfin[..., P:2 * P], hfin[..., 3 * P:4 * P]], axis=1)
    return y, fin_re, fin_im


def _dot_t(a, b):
    return lax.dot_general(a, b, (((1,), (1,)), ((), ())), preferred_element_type=F32)


def _gla_kernel(n_chunks, q_ref, k_ref, v_ref, g_ref, lr_ref, up_ref, db_ref, nw_ref, s0_ref,
                out_ref, sfin_ref, s_ref, of_ref):
    C = GLA_CHUNK
    d = pl.program_id(2)
    c = pl.program_id(3)
    cidx = jnp.where(d == 0, c, n_chunks - 1 - c)

    @pl.when(c == 0)
    def _():
        s_ref[...] = s0_ref[0, 0, 0]

    z = jnp.dot(lr_ref[0], up_ref[0], precision=HI, preferred_element_type=F32) + db_ref[0]
    gc = jnp.maximum(jax.nn.log_sigmoid(z) * (1.0 / GLA_NORMALIZER), GLA_LOG_DECAY_MIN)
    row = lax.broadcasted_iota(jnp.int32, (C, C), 0)
    col = lax.broadcasted_iota(jnp.int32, (C, C), 1)
    seen = jnp.where(d == 0, row - col, col - row) >= 0
    bcum = jnp.dot(seen.astype(F32), gc, precision=HI, preferred_element_type=F32)
    b_last = jnp.sum(gc, axis=0, keepdims=True)
    q = q_ref[0] * (GLA_DK ** -0.5)
    k = k_ref[0]
    v = v_ref[0].astype(BF16)
    q_dec = (q * jnp.exp(bcum)).astype(BF16)
    k_inv = (k * jnp.exp(-bcum)).astype(BF16)
    k_end = (k * jnp.exp(b_last - bcum)).astype(BF16)
    att = jnp.where(seen, _dot_t(q_dec, k_inv), 0.0).astype(BF16)
    s_old = s_ref[...]
    o = (jnp.dot(att, v, preferred_element_type=F32)
         + jnp.dot(q_dec, s_old.astype(BF16), preferred_element_type=F32))
    eye = (lax.broadcasted_iota(jnp.int32, (GLA_DK, GLA_DK), 0)
           == lax.broadcasted_iota(jnp.int32, (GLA_DK, GLA_DK), 1)).astype(F32)
    bl_col = lax.dot_general(eye, jnp.broadcast_to(b_last, (8, GLA_DK)), (((1,), (1,)), ((), ())),
                             precision=HI, preferred_element_type=F32)[:, 0:1]
    kv = lax.dot_general(k_end, v, (((0,), (0,)), ((), ())), preferred_element_type=F32)
    s_ref[...] = jnp.exp(bl_col) * s_old + kv

    rows = pl.ds(pl.multiple_of(cidx * C, C), C)

    @pl.when(d == 0)
    def _():
        of_ref[rows, :] = o

    @pl.when(d == 1)
    def _():
        tot = of_ref[rows, :] + o
        nrm = tot * lax.rsqrt(jnp.mean(tot * tot, axis=-1, keepdims=True) + EPS) * nw_ref[0]
        gate = g_ref[0]
        out_ref[0] = (nrm * (gate * jax.nn.sigmoid(gate))).astype(out_ref.dtype)

    @pl.when(c == n_chunks - 1)
    def _():
        sfin_ref[0, 0, 0] = s_ref[...]


def gla_mix(main, dec_lr, dec_up, dec_b, gla_nw, s0):
    bsz, L, _ = main.shape
    C, H, DK, DV = GLA_CHUNK, GLA_HEADS, GLA_DK, GLA_DV
    n = L // C
    q_blk = sum(EVEN_SIZES[:2]) // DK
    k_blk = sum(EVEN_SIZES[:3]) // DK
    v_blk = sum(EVEN_SIZES[:4]) // DV
    g_blk = sum(EVEN_SIZES[:5]) // DV
    up = jnp.zeros((N_DIR, LANES, GLA_DK_W), F32)
    for d in range(N_DIR):
        up = up.at[d, d * GLA_RANK:(d + 1) * GLA_RANK].set(dec_up[d])
    db = dec_b.reshape(N_DIR, 1, GLA_DK_W)
    nw = gla_nw.reshape(1, GLA_DV_W)

    def chunk(d, c):
        return c + d * (n - 1 - 2 * c)

    def out_chunk(d, c):
        return (n - 1) - d * c
    out, sfin = pl.pallas_call(
        functools.partial(_gla_kernel, n),
        grid=(bsz, H, N_DIR, n),
        in_specs=[pl.BlockSpec((1, C, DK), lambda b, h, d, c: (b, chunk(d, c), q_blk + h)),
                  pl.BlockSpec((1, C, DK), lambda b, h, d, c: (b, chunk(d, c), k_blk + h)),
                  pl.BlockSpec((1, C, DV), lambda b, h, d, c: (b, chunk(d, c), v_blk + h)),
                  pl.BlockSpec((1, C, DV), lambda b, h, d, c: (b, chunk(d, c), g_blk + h)),
                  pl.BlockSpec((1, C, LANES), lambda b, h, d, c: (b, chunk(d, c), 0)),
                  pl.BlockSpec((1, LANES, DK), lambda b, h, d, c: (d, 0, h)),
                  pl.BlockSpec((1, 1, DK), lambda b, h, d, c: (d, 0, h)),
                  pl.BlockSpec((1, DV), lambda b, h, d, c: (0, h)),
                  pl.BlockSpec((1, 1, 1, DK, DV), lambda b, h, d, c: (b, d, h, 0, 0))],
        out_specs=[pl.BlockSpec((1, C, DV), lambda b, h, d, c: (b, out_chunk(d, c), h)),
                   pl.BlockSpec((1, 1, 1, DK, DV), lambda b, h, d, c: (b, d, h, 0, 0))],
        out_shape=[jax.ShapeDtypeStruct((bsz, L, GLA_DV_W), BF16),
                   jax.ShapeDtypeStruct((bsz, N_DIR, H, DK, DV), F32)],
        scratch_shapes=[pltpu.VMEM((DK, DV), F32), pltpu.VMEM((L, DV), F32)],
        compiler_params=pltpu.CompilerParams(
            dimension_semantics=("arbitrary",) * 4, vmem_limit_bytes=VMEM_LIMIT),
        name="gla_chunk_scan",
    )(main, main, main, main, dec_lr, up, db, nw, s0)
    return out, sfin


def _rwkv_kernel(n_blk, r_ref, v_ref, a_ref, w_ref, k_ref, b_ref, s0_ref, y_ref, sfin_ref,
                 s_ref, vc_ref, sr_ref):
    TB, N, GT = RWKV_TBLK, RWKV_HEAD, RWKV_GROUP_T
    d = pl.program_id(2)
    c = pl.program_id(3)

    @pl.when(c == 0)
    def _():
        s_ref[...] = s0_ref[0, 0]

    lane = lax.broadcasted_iota(jnp.int32, (N, LANES), 1)
    row = lax.broadcasted_iota(jnp.int32, (N, LANES), 0)
    lo_half = lane < N
    diag = (lane % N) == row
    same_head = ((lax.broadcasted_iota(jnp.int32, (LANES, LANES), 0) < N)
                 == (lax.broadcasted_iota(jnp.int32, (LANES, LANES), 1) < N)).astype(BF16)
    fwd = d == 0
    y_ref[...] = jnp.zeros_like(y_ref)

    def row_of(tile, j):
        return jnp.where(fwd, tile[j:j + 1], tile[GT - 1 - j:GT - j])

    def group(i, carry):
        g8 = jnp.where(fwd, i, TB // GT - 1 - i)
        rs = pl.ds(pl.multiple_of(g8 * GT, GT), GT)
        v_tile = v_ref[0, rs, :]
        for p in range(RWKV_PAIRS):
            ls = slice(p * LANES, (p + 1) * LANES)
            vd = jnp.concatenate([jnp.where(diag, row_of(v_tile, j)[:, ls], 0.0) for j in range(GT)], axis=0)
            vd_hi = vd.astype(BF16)
            vd_lo = (vd - vd_hi.astype(F32)).astype(BF16)
            vcol = (jnp.dot(vd_hi, same_head, preferred_element_type=F32)
                    + jnp.dot(vd_lo, same_head, preferred_element_type=F32))
            vc_ref[:, ls] = vcol
        tiles = (r_ref[0, rs, :], a_ref[0, rs, :], w_ref[0, 0, rs, :], k_ref[0, 0, rs, :], b_ref[0, 0, rs, :])
        for j in range(GT):
            r_row, a_row, w_row, k_row, b_row = (row_of(x, j) for x in tiles)
            for p in range(RWKV_PAIRS):
                ls = slice(p * LANES, (p + 1) * LANES)
                s = s_ref[:, ls]
                prod = s * a_row[:, ls]
                sa_lo = jnp.sum(jnp.where(lo_half, prod, 0.0), axis=1, keepdims=True)
                sa_hi = jnp.sum(jnp.where(lo_half, 0.0, prod), axis=1, keepdims=True)
                sa = jnp.where(lo_half, sa_lo, sa_hi)
                s_new = s * w_row[:, ls] + sa * b_row[:, ls] + vc_ref[j * N:(j + 1) * N, ls] * k_row[:, ls]
                s_ref[:, ls] = s_new
                sr_ref[j * N:(j + 1) * N, ls] = (s_new * r_row[:, ls]).astype(BF16)
        for p in range(RWKV_PAIRS):
            ls = slice(p * LANES, (p + 1) * LANES)
            ycol = jnp.dot(sr_ref[:, ls], same_head, preferred_element_type=F32)
            acc = jnp.zeros((N, LANES), F32)
            for j in range(GT):
                t = g8 * GT + jnp.where(fwd, j, GT - 1 - j)
                acc = jnp.where((lane % N) == t, ycol[j * N:(j + 1) * N], acc)
            y_ref[0, 0, 0, :, ls] = jnp.where((lane % N) // GT == g8, acc, y_ref[0, 0, 0, :, ls])
        return carry
    lax.fori_loop(0, TB // GT, group, 0)

    @pl.when(c == n_blk - 1)
    def _():
        sfin_ref[0, 0] = s_ref[...]


def rwkv_scan(r, v, a_neg, decay, k_d, b_d, s0):
    bsz, L, W = r.shape
    TB, N, H = RWKV_TBLK, RWKV_HEAD, RWKV_HEADS
    n = L // TB
    gw = RWKV_PAIRS * LANES
    s0t = s0.transpose(0, 1, 3, 2, 4).reshape(bsz, N_DIR, N, W)

    def blk(d, c):
        return c + d * (n - 1 - 2 * c)
    seq = pl.BlockSpec((1, TB, gw), lambda b, g, d, c: (b, blk(d, c), g))
    seq_d = pl.BlockSpec((1, 1, TB, gw), lambda b, g, d, c: (d, b, blk(d, c), g))
    st = pl.BlockSpec((1, 1, N, gw), lambda b, g, d, c: (b, d, 0, g))
    gt = RWKV_GROUP_T
    yt, sfin = pl.pallas_call(
        functools.partial(_rwkv_kernel, n),
        grid=(bsz, W // gw, N_DIR, n),
        in_specs=[seq, seq, seq, seq_d, seq_d, seq_d, st],
        out_specs=[pl.BlockSpec((1, 1, 1, N, gw), lambda b, g, d, c: (d, b, blk(d, c), 0, g)), st],
        out_shape=[jax.ShapeDtypeStruct((N_DIR, bsz, n, N, W), F32),
                   jax.ShapeDtypeStruct((bsz, N_DIR, N, W), F32)],
        scratch_shapes=[pltpu.VMEM((N, gw), F32), pltpu.VMEM((gt * N, gw), F32), pltpu.VMEM((gt * N, gw), BF16)],
        compiler_params=pltpu.CompilerParams(
            dimension_semantics=("arbitrary",) * 4, vmem_limit_bytes=VMEM_LIMIT),
        name="rwkv_scan",
    )(r, v, a_neg, decay, k_d, b_d, s0t)
    yt = yt[0] + yt[1]
    wkv = yt.reshape(bsz, n, N, W // LANES, 2, TB).transpose(0, 1, 5, 3, 4, 2).reshape(bsz, L, W)
    sfin = sfin.reshape(bsz, N_DIR, N, H, N).transpose(0, 1, 3, 2, 4)
    return wkv, sfin


def _split_bf16(x):
    hi = x.astype(BF16)
    return hi, (x - hi.astype(F32)).astype(BF16)


def _mxu(x, y, dims=(((1,), (0,)), ((), ())), split=False):
    def d(a, b):
        return lax.dot_general(a, b, dims, preferred_element_type=F32)
    if not split:
        return d(x.astype(BF16), y.astype(BF16))
    xh, xl = _split_bf16(x)
    yh, yl = _split_bf16(y)
    return d(xh, yh) + (d(xh, yl) + d(xl, yh))


def _rwkv_chunk_kernel(n_chunks, r_ref, lw_ref, k_ref, v_ref, a_ref, b_ref, h0_ref, y_ref, hfin_ref, h_ref):
    T, N = RWKV_CHUNK, RWKV_HEAD
    d = pl.program_id(2)
    c = pl.program_id(3)

    @pl.when(c == 0)
    def _():
        h_ref[...] = h0_ref[0, 0]

    lane = lax.broadcasted_iota(jnp.int32, (T, LANES), 1)
    row = lax.broadcasted_iota(jnp.int32, (T, LANES), 0)
    lo = lane < N
    col = lane % N
    order = jnp.where(d == 0, row - col, col - row)
    seen = order >= 0
    before = order > 0
    eye = row == col
    sq_r = lax.broadcasted_iota(jnp.int32, (T, T), 0)
    sq_c = lax.broadcasted_iota(jnp.int32, (T, T), 1)
    seen_sq = (jnp.where(d == 0, sq_r - sq_c, sq_c - sq_r) >= 0).astype(F32)
    row_dims = (((0,), (0,)), ((), ()))
    lane_dims = (((1,), (1,)), ((), ()))

    def bd(x):
        return jnp.concatenate([jnp.where(lo, x, 0.0), jnp.where(lo, 0.0, x)], axis=0)

    def pp(x, y, split=False):
        return _mxu(x, bd(y), split=split)

    def ptp(x, y):
        full = _mxu(x, y, row_dims, split=True)
        return jnp.where(lo, full[:N], full[N:])

    lw_all = lw_ref[0, 0]
    cs_all = jnp.dot(seen_sq, lw_all, precision=HI, preferred_element_type=F32)
    tot_all = jnp.sum(lw_all, axis=0, keepdims=True)
    pairs = range(RWKV_CPAIRS)
    sl = [slice(p * LANES, (p + 1) * LANES) for p in pairs]
    cs = [cs_all[:, s] for s in sl]
    tot = [tot_all[:, s] for s in sl]
    e_out = [jnp.exp(-cs[p]) for p in pairs]
    at = [a_ref[0, :, sl[p]] * jnp.exp(cs[p] - lw_all[:, sl[p]]) for p in pairs]
    rt = [r_ref[0, :, sl[p]] * jnp.exp(cs[p]) for p in pairs]
    ar = [jnp.concatenate([at[p], rt[p]], axis=0) for p in pairs]
    g1 = [_mxu(ar[p], bd(b_ref[0, 0, :, sl[p]] * e_out[p]), lane_dims) for p in pairs]
    g2 = [_mxu(ar[p], bd(k_ref[0, 0, :, sl[p]] * e_out[p]), lane_dims) for p in pairs]
    a_ab = [jnp.where(before, g1[p][:T], 0.0) for p in pairs]
    a_rb = [jnp.where(seen, g1[p][T:], 0.0) for p in pairs]
    a_ak = [jnp.where(before, g2[p][:T], 0.0) for p in pairs]
    a_rk = [jnp.where(seen, g2[p][T:], 0.0) for p in pairs]
    w = [jnp.where(eye, 1.0, a_ab[p]) for p in pairs]
    apow = a_ab
    for _ in range(5):
        apow = [pp(apow[p], apow[p]) for p in pairs]
        w = [w[p] + pp(w[p], apow[p]) for p in pairs]
    v = [v_ref[0, :, sl[p]] for p in pairs]
    akv = [pp(a_ak[p], v[p]) for p in pairs]
    u_loc = [pp(w[p], akv[p]) for p in pairs]
    a_hat = [pp(w[p], at[p]) for p in pairs]
    h0 = [h_ref[:, sl[p]] for p in pairs]
    q_hat = [rt[p] + pp(a_rb[p], a_hat[p]) for p in pairs]
    y_loc = [pp(a_rb[p], u_loc[p]) + pp(a_rk[p], v[p]) for p in pairs]
    for p in pairs:
        y_ref[0, 0, :, sl[p]] = pp(q_hat[p], h0[p]) + y_loc[p]
    e_end = [jnp.exp(tot[p] - cs[p]) for p in pairs]
    bh = [b_ref[0, 0, :, sl[p]] * e_end[p] for p in pairs]
    phi = [jnp.where(eye, jnp.exp(tot[p]), 0.0) + ptp(bh[p], a_hat[p]) for p in pairs]
    gam = [ptp(jnp.concatenate([bh[p], k_ref[0, 0, :, sl[p]] * e_end[p]], axis=0),
               jnp.concatenate([u_loc[p], v[p]], axis=0)) for p in pairs]
    for p in pairs:
        h_ref[:, sl[p]] = pp(phi[p], h0[p], split=True) + gam[p]

    @pl.when(c == n_chunks - 1)
    def _():
        hfin_ref[0, 0] = h_ref[...]


def rwkv_chunk_scan(r, v, a_neg, log_decay, k_d, b_d, s0):
    bsz, L, W = r.shape
    T, N, H = RWKV_CHUNK, RWKV_HEAD, RWKV_HEADS
    n = L // T
    gw = RWKV_CPAIRS * LANES
    h0 = s0.transpose(0, 1, 4, 2, 3).reshape(bsz, N_DIR, N, W)

    def blk(d, c):
        return c + d * (n - 1 - 2 * c)
    seq = pl.BlockSpec((1, T, gw), lambda b, g, d, c: (b, blk(d, c), g))
    seq_d = pl.BlockSpec((1, 1, T, gw), lambda b, g, d, c: (d, b, blk(d, c), g))
    st = pl.BlockSpec((1, 1, N, gw), lambda b, g, d, c: (b, d, 0, g))
    y, hfin = pl.pallas_call(
        functools.partial(_rwkv_chunk_kernel, n),
        grid=(bsz, W // gw, N_DIR, n),
        in_specs=[seq, seq_d, seq_d, seq, seq, seq_d, st],
        out_specs=[seq_d, st],
        out_shape=[jax.ShapeDtypeStruct((N_DIR, bsz, L, W), F32),
                   jax.ShapeDtypeStruct((bsz, N_DIR, N, W), F32)],
        scratch_shapes=[pltpu.VMEM((N, gw), F32)],
        compiler_params=pltpu.CompilerParams(
            dimension_semantics=("arbitrary",) * 4, vmem_limit_bytes=VMEM_LIMIT),
        name="rwkv_chunk_scan",
    )(r, log_decay, k_d, v, a_neg, b_d, h0)
    sfin = hfin.reshape(bsz, N_DIR, N, H, N).transpose(0, 1, 3, 4, 2)
    return y[0] + y[1], sfin


def _rwkv_fs_kernel(n_chunks, rev, r_ref, lw_ref, k_ref, v_ref, a_ref, b_ref, h0_ref, y_ref, hfin_ref, h_ref):
    T, N, SB = RWKV_CHUNK, RWKV_HEAD, RWKV_SUB
    NB = T // SB
    c = pl.program_id(2)

    @pl.when(c == 0)
    def _():
        h_ref[...] = h0_ref[0]

    lane = lax.broadcasted_iota(jnp.int32, (T, LANES), 1)
    row = lax.broadcasted_iota(jnp.int32, (T, LANES), 0)
    lo = lane < N
    col = lane % N
    order = (col - row) if rev else (row - col)
    seen = order >= 0
    before = order > 0
    eye = row == col
    sq_r = lax.broadcasted_iota(jnp.int32, (T, T), 0)
    sq_c = lax.broadcasted_iota(jnp.int32, (T, T), 1)
    seen_sq = (((sq_c - sq_r) if rev else (sq_r - sq_c)) >= 0).astype(F32)
    same_head = ((lax.broadcasted_iota(jnp.int32, (LANES, LANES), 0) < N)
                 == (lax.broadcasted_iota(jnp.int32, (LANES, LANES), 1) < N)).astype(BF16)
    col_sb = lax.broadcasted_iota(jnp.int32, (SB, LANES), 1) % N
    row_dims = (((0,), (0,)), ((), ()))
    lane_dims = (((1,), (1,)), ((), ()))

    def bd(x):
        return jnp.concatenate([jnp.where(lo, x, 0.0), jnp.where(lo, 0.0, x)], axis=0)

    def pp(x, y, split=False):
        return _mxu(x, bd(y), split=split)

    def ptp(x, y):
        full = _mxu(x, y, row_dims, split=True)
        return jnp.where(lo, full[:N], full[N:])

    lw_all = lw_ref[0]
    cs_all = jnp.dot(seen_sq, lw_all, precision=HI, preferred_element_type=F32)
    tot_all = jnp.sum(lw_all, axis=0, keepdims=True)
    pairs = range(RWKV_CPAIRS)
    sl = [slice(p * LANES, (p + 1) * LANES) for p in pairs]
    cs = [cs_all[:, s] for s in sl]
    tot = [tot_all[:, s] for s in sl]
    e_out = [jnp.exp(-cs[p]) for p in pairs]
    at = [a_ref[0, :, sl[p]] * jnp.exp(cs[p] - lw_all[:, sl[p]]) for p in pairs]
    rt = [r_ref[0, :, sl[p]] * jnp.exp(cs[p]) for p in pairs]
    ar = [jnp.concatenate([at[p], rt[p]], axis=0) for p in pairs]
    g1 = [_mxu(ar[p], bd(b_ref[0, :, sl[p]] * e_out[p]), lane_dims) for p in pairs]
    g2 = [_mxu(ar[p], bd(k_ref[0, :, sl[p]] * e_out[p]), lane_dims) for p in pairs]
    a_ab = [jnp.where(before, g1[p][:T], 0.0) for p in pairs]
    a_rb = [jnp.where(seen, g1[p][T:], 0.0) for p in pairs]
    a_ak = [jnp.where(before, g2[p][:T], 0.0) for p in pairs]
    a_rk = [jnp.where(seen, g2[p][T:], 0.0) for p in pairs]
    v = [v_ref[0, :, sl[p]] for p in pairs]
    akv = [pp(a_ak[p], v[p]) for p in pairs]
    za = [[None] * NB for _ in pairs]
    zu = [[None] * NB for _ in pairs]
    zero_blk = jnp.zeros((SB, LANES), F32)
    for kpos in range(NB):
        bk = NB - 1 - kpos if rev else kpos
        rows = slice(bk * SB, (bk + 1) * SB)
        done = [(m > bk) if rev else (m < bk) for m in range(NB)]
        cur_a = [at[p][rows] for p in pairs]
        cur_u = [akv[p][rows] for p in pairs]
        if kpos > 0:
            for p in pairs:
                zc_a = jnp.concatenate([za[p][m] if done[m] else zero_blk for m in range(NB)], axis=0)
                zc_u = jnp.concatenate([zu[p][m] if done[m] else zero_blk for m in range(NB)], axis=0)
                off = _mxu(a_ab[p][rows], jnp.concatenate([bd(zc_a), bd(zc_u)], axis=1))
                cur_a[p] = cur_a[p] + off[:, :LANES]
                cur_u[p] = cur_u[p] + off[:, LANES:]
        abc = []
        for p in pairs:
            ablk = a_ab[p][rows]
            picked = jnp.concatenate([jnp.where(col_sb == bk * SB + s, ablk, 0.0) for s in range(SB)], axis=0)
            abc.append(jnp.dot(picked.astype(BF16), same_head, preferred_element_type=F32))
        for j in range(SB - 1):
            s = SB - 1 - j if rev else j
            for p in pairs:
                coef = abc[p][s * SB:(s + 1) * SB]
                cur_a[p] = cur_a[p] + coef * cur_a[p][s:s + 1]
                cur_u[p] = cur_u[p] + coef * cur_u[p][s:s + 1]
        for p in pairs:
            za[p][bk] = cur_a[p]
            zu[p][bk] = cur_u[p]
    a_hat = [jnp.concatenate(za[p], axis=0) for p in pairs]
    u_loc = [jnp.concatenate(zu[p], axis=0) for p in pairs]
    h0 = [h_ref[:, sl[p]] for p in pairs]
    q_hat = [rt[p] + pp(a_rb[p], a_hat[p]) for p in pairs]
    y_loc = [pp(a_rb[p], u_loc[p]) + pp(a_rk[p], v[p]) for p in pairs]
    for p in pairs:
        y_ref[0, :, sl[p]] = pp(q_hat[p], h0[p]) + y_loc[p]
    e_end = [jnp.exp(tot[p] - cs[p]) for p in pairs]
    bh = [b_ref[0, :, sl[p]] * e_end[p] for p in pairs]
    phi = [jnp.where(eye, jnp.exp(tot[p]), 0.0) + ptp(bh[p], a_hat[p]) for p in pairs]
    gam = [ptp(jnp.concatenate([bh[p], k_ref[0, :, sl[p]] * e_end[p]], axis=0),
               jnp.concatenate([u_loc[p], v[p]], axis=0)) for p in pairs]
    for p in pairs:
        h_ref[:, sl[p]] = pp(phi[p], h0[p], split=True) + gam[p]

    @pl.when(c == n_chunks - 1)
    def _():
        hfin_ref[0] = h_ref[...]


def rwkv_direction(rev, r, v, a_neg, log_decay, k_d, b_d, s0):
    bsz, L, W = r.shape
    T, N, H = RWKV_CHUNK, RWKV_HEAD, RWKV_HEADS
    n = L // T
    gw = RWKV_CPAIRS * LANES
    h0 = s0.transpose(0, 3, 1, 2).reshape(bsz, N, W)
    seq = pl.BlockSpec((1, T, gw), lambda b, g, c: (b, (n - 1 - c) if rev else c, g))
    st = pl.BlockSpec((1, N, gw), lambda b, g, c: (b, 0, g))
    y, hfin = pl.pallas_call(
        functools.partial(_rwkv_fs_kernel, n, rev),
        grid=(bsz, W // gw, n),
        in_specs=[seq, seq, seq, seq, seq, seq, st],
        out_specs=[seq, st],
        out_shape=[jax.ShapeDtypeStruct((bsz, L, W), F32), jax.ShapeDtypeStruct((bsz, N, W), F32)],
        scratch_shapes=[pltpu.VMEM((N, gw), F32)],
        compiler_params=pltpu.CompilerParams(
            dimension_semantics=("arbitrary",) * 3, vmem_limit_bytes=VMEM_LIMIT),
        name="rwkv_bwd_chunks" if rev else "rwkv_fwd_chunks",
    )(r, log_decay, k_d, v, a_neg, b_d, h0)
    return y, hfin.reshape(bsz, N, H, N).transpose(0, 2, 3, 1)


def _split_cols(t, sizes):
    offsets, acc = [], 0
    for s in sizes[:-1]:
        acc += s
        offsets.append(acc)
    return jnp.split(t, offsets, axis=-1)


def _rms(x, w):
    return x * lax.rsqrt(jnp.mean(x * x, axis=-1, keepdims=True) + EPS) * w


def _adaln(cond, w, b):
    m = jnp.dot(jax.nn.silu(cond), w, precision=HI) + b
    shift, scale, gate = jnp.split(m, 3, axis=-1)
    return shift[..., None, :], scale[..., None, :], gate[..., None, :]


def _grid_pos_embed(n_tokens):
    rows = n_tokens // GRID_W
    row_id = jnp.broadcast_to(jnp.arange(rows, dtype=F32)[:, None], (rows, GRID_W)).reshape(-1)
    col_id = jnp.broadcast_to(jnp.arange(GRID_W, dtype=F32)[None, :], (rows, GRID_W)).reshape(-1)
    quarter = D_MODEL // 4
    omega = 1.0 / (POS_BASE ** (jnp.arange(quarter, dtype=F32) / quarter))

    def axis_emb(pos):
        ang = pos[:, None] * omega[None, :]
        return jnp.concatenate([jnp.sin(ang), jnp.cos(ang)], axis=-1)
    return jnp.concatenate([axis_emb(row_id), axis_emb(col_id)], axis=-1)


def _even_mixer(h, s5_re0, s5_im0, gla0, w_in, w_out, s5_ops, glu_w, glu_b, dec_up, dec_b, gla_nw):
    bsz, L, _ = h.shape
    n_main = sum(EVEN_SIZES[:-1])
    main = _mm3(h, w_in[:, :n_main])
    w_tail = jnp.pad(w_in[:, n_main:], ((0, 0), (0, LANES - N_DIR * GLA_RANK)))
    dec_lr = _mm3(h, w_tail)
    u, s5_g = main[..., :S5_W], main[..., S5_W:2 * S5_W]
    y, fin_re, fin_im = s5_scan(u, s5_ops, s5_re0, s5_im0)
    gy = jax.nn.gelu(y)
    s5_out = gy * jax.nn.sigmoid(_mm3(gy, glu_w) + glu_b) * jax.nn.silu(s5_g)
    gla_out, fin_gla = gla_mix(main, dec_lr, dec_up, dec_b, gla_nw, gla0)
    out = _mm3(jnp.concatenate([s5_out.astype(BF16), gla_out], axis=-1), w_out)
    return out, fin_re, fin_im, fin_gla


def _odd_mixer(h, rwkv0, w_in, w_out, mu, w0, w2, a0, a2, k_k, k_a, r_k, lnx_w, lnx_b):
    bsz, L, _ = h.shape
    zero = jnp.zeros_like(h[:, :1])
    h_prev = jnp.concatenate([zero, h[:, :-1]], axis=1)
    h_next = jnp.concatenate([h[:, 1:], zero], axis=1)
    xs = h + mu[0] * (h_prev - h) + mu[1] * (h_next - h)
    n_main = sum(ODD_SIZES[:4])
    main = _mm3(xs, w_in[:, :n_main])
    tail = _mm3(xs, w_in[:, n_main:])
    r, k, v, g = _split_cols(main, ODD_SIZES[:4])
    w_lr, a_lr = _split_cols(tail, ODD_SIZES[4:])
    w_lr = jnp.tanh(w_lr).reshape(bsz, L, N_DIR, RWKV_DECAY_RANK)
    a_lr = a_lr.reshape(bsz, L, N_DIR, RWKV_ICLR_RANK)

    def heads(t):
        return t.reshape(bsz, L, RWKV_HEADS, RWKV_HEAD)
    kk = heads(k * k_k)
    kk = kk / jnp.maximum(jnp.sqrt(jnp.sum(kk * kk, axis=-1, keepdims=True)), 1e-12)
    kk = kk.reshape(bsz, L, RWKV_W)
    r_h, v_h = heads(r), heads(v)
    wkv, bonus = 0.0, 0.0
    finals = []
    for d in range(N_DIR):
        w_log = -jax.nn.softplus(-(w0[d] + _mm3(w_lr[:, :, d], w2[d]))) - 0.5
        log_decay = -jnp.exp(w_log)
        a = jax.nn.sigmoid(a0[d] + _mm3(a_lr[:, :, d], a2[d]))
        k_d = k * (1.0 + (a - 1.0) * k_a)
        y_d, fin = rwkv_direction(bool(d), r, v, -kk, log_decay, k_d, kk * a, rwkv0[:, d])
        wkv = wkv + y_d
        finals.append(fin)
        bonus = bonus + jnp.sum(r_h * heads(k_d) * r_k, axis=-1, keepdims=True) * v_h
    finals = jnp.stack(finals, axis=1)
    wkv = heads(wkv)
    mean = jnp.mean(wkv, axis=-1, keepdims=True)
    var = jnp.mean(jnp.square(wkv - mean), axis=-1, keepdims=True)
    ln = ((wkv - mean) * lax.rsqrt(var + RWKV_LNX_EPS) * lnx_w.reshape(RWKV_HEADS, RWKV_HEAD)
          + lnx_b.reshape(RWKV_HEADS, RWKV_HEAD))
    out = (ln + bonus).reshape(bsz, L, RWKV_W) * jax.nn.silu(g)
    return _mm3(out, w_out), finals


def kernel(x_prompt, x_sample, state_s5_re, state_s5_im, state_gla, state_rwkv, c, c_ctx, norm_w, ada_w, ada_b, final_norm_w, e_w_in, e_w_out, s5_lambda_re, s5_lambda_im, s5_log_step, s5_b_re, s5_b_im, s5_c_re, s5_c_im, s5_d, s5_glu_w, s5_glu_b, gla_decay_up, gla_decay_b, gla_norm_w, o_w_in, o_w_out, rwkv_mu, rwkv_w0, rwkv_w2, rwkv_a0, rwkv_a2, rwkv_k_k, rwkv_k_a, rwkv_r_k, rwkv_lnx_w, rwkv_lnx_b):
    bp = x_prompt.shape[0]
    depth = norm_w.shape[0]
    x_ctx = x_prompt
    x_lat = x_sample + _grid_pos_embed(x_sample.shape[1])[None]
    z_s5 = jnp.zeros((bp, N_DIR, S5_GROUPS, S5_STATE), F32)
    z_gla = jnp.zeros((bp, N_DIR, GLA_HEADS, GLA_DK, GLA_DV), F32)
    z_rwkv = jnp.zeros((bp, N_DIR, RWKV_HEADS, RWKV_HEAD, RWKV_HEAD), F32)
    new_s5_re, new_s5_im, new_gla, new_rwkv = [], [], [], []
    for i in range(depth):
        j = i // 2
        sh_c, sc_c, gt_c = _adaln(c_ctx, ada_w[i], ada_b[i])
        sh_l, sc_l, gt_l = _adaln(c, ada_w[i], ada_b[i])
        h_ctx = _rms(x_ctx, norm_w[i]) * (1.0 + sc_c) + sh_c
        h_lat = _rms(x_lat, norm_w[i]) * (1.0 + sc_l) + sh_l
        if i % 2 == 0:
            s5_ops = s5_operators(s5_lambda_re[j], s5_lambda_im[j], s5_log_step[j], s5_b_re[j], s5_b_im[j],
                                  s5_c_re[j], s5_c_im[j], s5_d[j])
            p = (e_w_in[j], e_w_out[j], s5_ops, s5_glu_w[j], s5_glu_b[j], gla_decay_up[j], gla_decay_b[j],
                 gla_norm_w[j])
            o_ctx, fr, fi, fg = _even_mixer(h_ctx, z_s5, z_s5, z_gla, *p)
            o_lat, _, _, _ = _even_mixer(h_lat, state_s5_re[:, j], state_s5_im[:, j], state_gla[:, j], *p)
            new_s5_re.append(fr)
            new_s5_im.append(fi)
            new_gla.append(fg)
        else:
            p = (o_w_in[j], o_w_out[j], rwkv_mu[j], rwkv_w0[j], rwkv_w2[j], rwkv_a0[j], rwkv_a2[j],
                 rwkv_k_k[j], rwkv_k_a[j], rwkv_r_k[j], rwkv_lnx_w[j], rwkv_lnx_b[j])
            o_ctx, fw = _odd_mixer(h_ctx, z_rwkv, *p)
            o_lat, _ = _odd_mixer(h_lat, state_rwkv[:, j], *p)
            new_rwkv.append(fw)
        x_ctx = x_ctx + gt_c * o_ctx
        x_lat = x_lat + gt_l * o_lat
    y_prompt = _rms(x_ctx, final_norm_w)
    y_sample = _rms(x_lat, final_norm_w)
    return (y_prompt, y_sample, jnp.stack(new_s5_re, axis=1), jnp.stack(new_s5_im, axis=1),
            jnp.stack(new_gla, axis=1), jnp.stack(new_rwkv, axis=1))
```

```python
import functools
import math

import jax
import jax.numpy as jnp
from jax import lax
from jax.experimental import pallas as pl
from jax.experimental.pallas import tpu as pltpu

D_MODEL = 2048
GRID_W = 64
POS_BASE = 10000.0
N_DIR = 2
EPS = 1e-6
S5_W = 1024
S5_GROUP_CH = 16
S5_GROUPS = 64
S5_STATE = 64
S5_CHUNK = 16
GLA_HEADS = 6
GLA_DV = 512
GLA_DK = 256
GLA_DK_W = 1536
GLA_DV_W = 3072
GLA_RANK = 16
GLA_NORMALIZER = 16.0
GLA_CHUNK = 64
GLA_NC = 4
GLA_LOG_DECAY_MIN = -1.0
EVEN_SIZES = (S5_W, S5_W, GLA_DK_W, GLA_DK_W, GLA_DV_W, GLA_DV_W, N_DIR * GLA_RANK)
RWKV_W = 2048
RWKV_HEAD = 64
RWKV_HEADS = 32
RWKV_DECAY_RANK = 96
RWKV_ICLR_RANK = 96
RWKV_LNX_EPS = 64e-5
ODD_SIZES = (RWKV_W, RWKV_W, RWKV_W, RWKV_W, N_DIR * RWKV_DECAY_RANK, N_DIR * RWKV_ICLR_RANK)
RWKV_TBLK = 64
RWKV_PAIRS = 8
RWKV_GROUP_T = 8
RWKV_CHUNK = 64
RWKV_CPAIRS = 8
RWKV_SUB = 16
LANES = 128

VMEM_LIMIT = 48 * 1024 * 1024
HI = lax.Precision.HIGHEST
BF16 = jnp.bfloat16
F32 = jnp.float32


def _mm_kernel(x_ref, w_ref, o_ref):
    o_ref[...] = jnp.dot(x_ref[...], w_ref[...], preferred_element_type=F32)


def _pick(n, prefs):
    for p in prefs:
        if n % p == 0:
            return p
    return n


def matmul(x, w):
    m, k = x.shape
    n = w.shape[1]
    x = x.astype(BF16)
    w = w.astype(BF16)
    tm = _pick(m, (1024, 512, 256, 128, 64, 32, 16, 8))
    tn = _pick(n, (512, 384, 256, 128))
    return pl.pallas_call(
        _mm_kernel,
        grid=(m // tm, n // tn),
        in_specs=[pl.BlockSpec((tm, k), lambda i, j: (i, 0)),
                  pl.BlockSpec((k, tn), lambda i, j: (0, j))],
        out_specs=pl.BlockSpec((tm, tn), lambda i, j: (i, j)),
        out_shape=jax.ShapeDtypeStruct((m, n), F32),
        compiler_params=pltpu.CompilerParams(
            dimension_semantics=("arbitrary", "arbitrary"), vmem_limit_bytes=VMEM_LIMIT),
        name="proj_matmul",
    )(x, w)


def _mm3(h, w):
    b, l, k = h.shape
    return matmul(h.reshape(b * l, k), w).reshape(b, l, -1)


def s5_operators(lam_re, lam_im, log_step, b_re, b_im, c_re, c_im, d_skip):
    T = S5_CHUNK
    dt = jnp.exp(log_step)[..., None]
    mag = jnp.exp(lam_re * dt)
    ab_re, ab_im = mag * jnp.cos(lam_im * dt), mag * jnp.sin(lam_im * dt)
    den = lam_re * lam_re + lam_im * lam_im
    f_re = ((ab_re - 1.0) * lam_re + ab_im * lam_im) / den
    f_im = (ab_im * lam_re - (ab_re - 1.0) * lam_im) / den
    bb_re = f_re[..., None] * b_re - f_im[..., None] * b_im
    bb_im = f_re[..., None] * b_im + f_im[..., None] * b_re
    kk = jnp.arange(T + 1, dtype=F32)[:, None, None, None]
    pmag = jnp.exp(kk * (lam_re * dt))
    pr = pmag * jnp.cos(kk * (lam_im * dt))
    pi = pmag * jnp.sin(kk * (lam_im * dt))
    zr = pr[:T, :, :, :, None] * bb_re - pi[:T, :, :, :, None] * bb_im
    zi = pr[:T, :, :, :, None] * bb_im + pi[:T, :, :, :, None] * bb_re
    kern = (jnp.einsum('dghp,kdgpj->kdghj', c_re, zr, precision=HI)
            - jnp.einsum('dghp,kdgpj->kdghj', c_im, zi, precision=HI))
    t_idx = jnp.arange(T)[:, None]
    s_idx = jnp.arange(T)[None, :]
    lag_f = t_idx - s_idx
    lag_b = s_idx - t_idx
    m_f = jnp.where((lag_f >= 0)[:, :, None, None, None], kern[:, 0][jnp.clip(lag_f, 0, T - 1)], 0.0)
    m_b = jnp.where((lag_b >= 0)[:, :, None, None, None], kern[:, 1][jnp.clip(lag_b, 0, T - 1)], 0.0)
    m = m_f + m_b
    eye_t = jnp.eye(T, dtype=F32)[:, :, None, None, None]
    eye_h = jnp.eye(S5_GROUP_CH, dtype=F32)[None, None, None]
    m = m + eye_t * eye_h * d_skip.reshape(S5_GROUPS, S5_GROUP_CH)[None, None, :, :, None]
    g = m.shape[2]
    m_t = m.transpose(2, 1, 4, 0, 3).reshape(g, T * S5_GROUP_CH, T * S5_GROUP_CH)
    pf_r, pf_i = pr[T - 1::-1][:T, 0], pi[T - 1::-1][:T, 0]
    pb_r, pb_i = pr[:T, 1], pi[:T, 1]

    def f_mat(p_r, p_i, d):
        re = p_r[..., None] * bb_re[d][None] - p_i[..., None] * bb_im[d][None]
        im = p_r[..., None] * bb_im[d][None] + p_i[..., None] * bb_re[d][None]
        re = re.transpose(1, 0, 3, 2).reshape(g, T * S5_GROUP_CH, S5_STATE)
        im = im.transpose(1, 0, 3, 2).reshape(g, T * S5_GROUP_CH, S5_STATE)
        return re, im
    ff_re, ff_im = f_mat(pf_r, pf_i, 0)
    fb_re, fb_im = f_mat(pb_r, pb_i, 1)
    a_t = jnp.concatenate([m_t, ff_re, fb_re, ff_im, fb_im], axis=-1)
    ef_r, ef_i = pr[1:T + 1, 0], pi[1:T + 1, 0]
    eb_r, eb_i = pr[T:0:-1, 1], pi[T:0:-1, 1]

    def e_mat(p_r, p_i, d):
        er = c_re[d][None] * p_r[:, :, None, :] - c_im[d][None] * p_i[:, :, None, :]
        ei = -(c_re[d][None] * p_i[:, :, None, :] + c_im[d][None] * p_r[:, :, None, :])
        er = er.transpose(1, 3, 0, 2).reshape(g, S5_STATE, T * S5_GROUP_CH)
        ei = ei.transpose(1, 3, 0, 2).reshape(g, S5_STATE, T * S5_GROUP_CH)
        return er, ei
    efr, efi = e_mat(ef_r, ef_i, 0)
    ebr, ebi = e_mat(eb_r, eb_i, 1)
    e_t = jnp.concatenate([efr, ebr, efi, ebi], axis=1)
    lam_t = jnp.concatenate([pr[T, 0], pr[T, 1], pi[T, 0], pi[T, 1]], axis=-1)[:, None, :]
    return a_t.astype(BF16), e_t.astype(BF16), lam_t


def _s5_kernel(n_chunks, bsz, ut_ref, at_ref, et_ref, lam_ref, h0_ref, y_ref, hfin_ref, z_ref, hent_ref):
    P = S5_STATE
    z_ref[...] = jnp.dot(ut_ref[0], at_ref[0], preferred_element_type=F32)
    lam = lam_ref[0]
    a_re, a_im = lam[:, 0:2 * P], lam[:, 2 * P:4 * P]
    h0 = h0_ref[0]
    fwd_lanes = lax.broadcasted_iota(jnp.int32, (bsz, 2 * P), 1) < P

    def step(c, carry):
        h_re, h_im = carry
        rf = pl.ds(pl.multiple_of(c * bsz, 8), bsz)
        rb = pl.ds(pl.multiple_of((n_chunks - 1 - c) * bsz, 8), bsz)
        hent_ref[rf, 0:P] = h_re[:, 0:P]
        hent_ref[rb, P:2 * P] = h_re[:, P:2 * P]
        hent_ref[rf, 2 * P:3 * P] = h_im[:, 0:P]
        hent_ref[rb, 3 * P:4 * P] = h_im[:, P:2 * P]
        g_re = jnp.where(fwd_lanes, z_ref[rf, 4 * P:6 * P], z_ref[rb, 4 * P:6 * P])
        g_im = jnp.where(fwd_lanes, z_ref[rf, 6 * P:8 * P], z_ref[rb, 6 * P:8 * P])
        return a_re * h_re - a_im * h_im + g_re, a_re * h_im + a_im * h_re + g_im
    h_re, h_im = lax.fori_loop(0, n_chunks, step, (h0[:, 0:2 * P], h0[:, 2 * P:4 * P]))
    hfin_ref[0, :, 0:2 * P] = h_re
    hfin_ref[0, :, 2 * P:4 * P] = h_im
    y_ref[0] =z_ref[:, 0:4 * P] + jnp.dot(hent_ref[...].astype(BF16), et_ref[0], preferred_element_type=F32)


def s5_scan(u, ops, h0_re, h0_im):
    a_t, e_t, lam_t = ops
    b_real, L, _ = u.shape
    T, G, H, P = S5_CHUNK, S5_GROUPS, S5_GROUP_CH, S5_STATE
    n = L // T
    bsz = -(-b_real // 8) * 8
    cols = n * bsz
    ut = u.reshape(b_real, n, T, G, H).transpose(3, 1, 0, 2, 4).astype(BF16)
    ut = jnp.pad(ut, ((0, 0), (0, 0), (0, bsz - b_real), (0, 0), (0, 0))).reshape(G, cols, T * H)
    h0 = jnp.concatenate([h0_re[:, 0], h0_re[:, 1], h0_im[:, 0], h0_im[:, 1]], axis=-1)
    h0 = jnp.pad(h0.transpose(1, 0, 2), ((0, 0), (0, bsz - b_real), (0, 0)))
    yt, hfin = pl.pallas_call(
        functools.partial(_s5_kernel, n, bsz),
        grid=(G,),
        in_specs=[pl.BlockSpec((1, cols, T * H), lambda g: (g, 0, 0)),
                  pl.BlockSpec((1, T * H, 8 * P), lambda g: (g, 0, 0)),
                  pl.BlockSpec((1, 4 * P, T * H), lambda g: (g, 0, 0)),
                  pl.BlockSpec((1, 1, 4 * P), lambda g: (g, 0, 0)),
                  pl.BlockSpec((1, bsz, 4 * P), lambda g: (g, 0, 0))],
        out_specs=[pl.BlockSpec((1, cols, T * H), lambda g: (g, 0, 0)),
                   pl.BlockSpec((1, bsz, 4 * P), lambda g: (g, 0, 0))],
        out_shape=[jax.ShapeDtypeStruct((G, cols, T * H), F32),
                   jax.ShapeDtypeStruct((G, bsz, 4 * P), F32)],
        scratch_shapes=[pltpu.VMEM((cols, 8 * P), F32), pltpu.VMEM((cols, 4 * P), F32)],
        compiler_params=pltpu.CompilerParams(dimension_semantics=("arbitrary",), vmem_limit_bytes=VMEM_LIMIT),
        name="s5_chunk_scan",
    )(ut, a_t, e_t, lam_t, h0)
    y = yt.reshape(G, n, bsz, T, H)[:, :, :b_real].transpose(2, 1, 3, 0, 4).reshape(b_real, L, G * H)
    hfin = hfin[:, :b_real].transpose(1, 0, 2)
    fin_re = jnp.stack([hfin[..., 0:P], hfin[..., P:2 * P]], axis=1)
    fin_im = jnp.stack([hfin[..., 2 * P:3 * P], hfin[..., 3 * P:4 * P]], axis=1)
    return y, fin_re, fin_im


def _dot_t(a, b):
    return lax.dot_general(a, b, (((1,), (1,)), ((), ())), preferred_element_type=F32)


def _gla_kernel(n_chunks, q_ref, k_ref, v_ref, g_ref, lr_ref, up_ref, db_ref, nw_ref, s0_ref,
                out_ref, sfin_ref, s_ref, of_ref):
    C = GLA_CHUNK
    d = pl.program_id(2)
    c = pl.program_id(3)
    cidx = jnp.where(d == 0, c, n_chunks - 1 - c)

    @pl.when(c == 0)
    def _():
        s_ref[...] = s0_ref[0, 0, 0]

    z = jnp.dot(lr_ref[0], up_ref[0], precision=HI, preferred_element_type=F32) + db_ref[0]
    gc = jnp.maximum(jax.nn.log_sigmoid(z) * (1.0 / GLA_NORMALIZER), GLA_LOG_DECAY_MIN)
    row = lax.broadcasted_iota(jnp.int32, (C, C), 0)
    col = lax.broadcasted_iota(jnp.int32, (C, C), 1)
    seen = jnp.where(d == 0, row - col, col - row) >= 0
    bcum = jnp.dot(seen.astype(F32), gc, precision=HI, preferred_element_type=F32)
    b_last = jnp.sum(gc, axis=0, keepdims=True)
    q = q_ref[0] * (GLA_DK ** -0.5)
    k = k_ref[0]
    v = v_ref[0].astype(BF16)
    q_dec = (q * jnp.exp(bcum)).astype(BF16)
    k_inv = (k * jnp.exp(-bcum)).astype(BF16)
    k_end = (k * jnp.exp(b_last - bcum)).astype(BF16)
    att = jnp.where(seen, _dot_t(q_dec, k_inv), 0.0).astype(BF16)
    s_old = s_ref[...]
    o = (jnp.dot(att, v, preferred_element_type=F32)
         + jnp.dot(q_dec, s_old.astype(BF16), preferred_element_type=F32))
    eye = (lax.broadcasted_iota(jnp.int32, (GLA_DK, GLA_DK), 0)
           == lax.broadcasted_iota(jnp.int32, (GLA_DK, GLA_DK), 1)).astype(F32)
    bl_col = lax.dot_general(eye, jnp.broadcast_to(b_last, (8, GLA_DK)), (((1,), (1,)), ((), ())),
                             precision=HI, preferred_element_type=F32)[:, 0:1]
    kv = lax.dot_general(k_end, v, (((0,), (0,)), ((), ())), preferred_element_type=F32)
    s_ref[...] = jnp.exp(bl_col) * s_old + kv

    rows = pl.ds(pl.multiple_of(cidx * C, C), C)

    @pl.when(d == 0)
    def _():
        of_ref[rows, :] = o

    @pl.when(d == 1)
    def _():
        tot = of_ref[rows, :] + o
        nrm = tot * lax.rsqrt(jnp.mean(tot * tot, axis=-1, keepdims=True) + EPS) * nw_ref[0]
        gate = g_ref[0]
        out_ref[0] = (nrm * (gate * jax.nn.sigmoid(gate))).astype(out_ref.dtype)

    @pl.when(c == n_chunks - 1)
    def _():
        sfin_ref[0, 0, 0] = s_ref[...]


def _dot_mask(mask_bf16, x, x_rows_to_sublanes=False):
    def d(b):
        if x_rows_to_sublanes:
            return lax.dot_general(b, mask_bf16, (((0,), (0,)), ((), ())), preferred_element_type=F32)
        return jnp.dot(mask_bf16, b, preferred_element_type=F32)
    x1 = x.astype(BF16)
    r1 = x - x1.astype(F32)
    x2 = r1.astype(BF16)
    x3 = (r1 - x2.astype(F32)).astype(BF16)
    return d(x1) + (d(x2) + d(x3))


def _gla_block_kernel(n_blocks, q_ref, k_ref, v_ref, g_ref, lr_ref, up_ref, db_ref, nw_ref, s0_ref,
                      out_ref, sfin_ref, s_ref, of_ref, qd_ref, ov_ref, kv_ref, dc_ref):
    C, NC = GLA_CHUNK, GLA_NC
    R = C * NC
    d = pl.program_id(2)
    c = pl.program_id(3)
    bidx = jnp.where(d == 0, c, n_blocks - 1 - c)

    @pl.when(c == 0)
    def _():
        s_ref[...] = s0_ref[0, 0, 0]

    z = jnp.dot(lr_ref[0], up_ref[0], precision=HI, preferred_element_type=F32) + db_ref[0]
    gc = jnp.maximum(jax.nn.log_sigmoid(z) * (1.0 / GLA_NORMALIZER), GLA_LOG_DECAY_MIN)
    row = lax.broadcasted_iota(jnp.int32, (R, R), 0)
    col = lax.broadcasted_iota(jnp.int32, (R, R), 1)
    same_chunk = (row // C) == (col // C)
    order = jnp.where(d == 0, row - col, col - row)
    seen_blk = (same_chunk & (order >= 0)).astype(BF16)
    bcum = _dot_mask(seen_blk, gc)
    btot = _dot_mask(same_chunk.astype(BF16), gc)
    q_dec = (q_ref[0] * (GLA_DK ** -0.5) * jnp.exp(bcum)).astype(BF16)
    k = k_ref[0]
    k_inv = (k * jnp.exp(-bcum)).astype(BF16)
    k_end = (k * jnp.exp(btot - bcum)).astype(BF16)
    v = v_ref[0].astype(BF16)
    row_c = lax.broadcasted_iota(jnp.int32, (C, C), 0)
    col_c = lax.broadcasted_iota(jnp.int32, (C, C), 1)
    seen_c = jnp.where(d == 0, row_c - col_c, col_c - row_c) >= 0
    ones_c = jnp.ones((C, LANES), BF16)
    qd_ref[...] = q_dec.reshape(NC, C, GLA_DK)
    for i in range(NC):
        rs = slice(i * C, (i + 1) * C)
        att = jnp.where(seen_c, _dot_t(q_dec[rs], k_inv[rs]), 0.0).astype(BF16)
        ov_ref[i] = jnp.dot(att, v[rs], preferred_element_type=F32)
        kv_ref[i] = lax.dot_general(k_end[rs], v[rs], (((0,), (0,)), ((), ())), preferred_element_type=F32)
        dc_ref[i] = _dot_mask(ones_c, gc[rs], x_rows_to_sublanes=True)

    for i in range(NC):
        ci = jnp.where(d == 0, i, NC - 1 - i)
        s_old = s_ref[...]
        o = ov_ref[ci] + jnp.dot(qd_ref[ci], s_old.astype(BF16), preferred_element_type=F32)
        s_ref[...] = jnp.exp(dc_ref[ci][:, 0:1]) * s_old + kv_ref[ci]
        rows = pl.ds(pl.multiple_of(bidx * R + ci * C, C), C)
        orow = pl.ds(pl.multiple_of(ci * C, C), C)

        @pl.when(d == 0)
        def _():
            of_ref[rows, :] = o

        @pl.when(d == 1)
        def _():
            tot = of_ref[rows, :] + o
            nrm = tot * lax.rsqrt(jnp.mean(tot * tot, axis=-1, keepdims=True) + EPS) * nw_ref[0]
            gate = g_ref[0, orow, :]
            out_ref[0, orow, :] = (nrm * (gate * jax.nn.sigmoid(gate))).astype(out_ref.dtype)

    @pl.when(c == n_blocks - 1)
    def _():
        sfin_ref[0, 0, 0] = s_ref[...]


def gla_mix(main, dec_lr, dec_up, dec_b, gla_nw, s0):
    bsz, L, _ = main.shape
    H, DK, DV = GLA_HEADS, GLA_DK, GLA_DV
    C = GLA_CHUNK * GLA_NC
    n = L // C
    q_blk = sum(EVEN_SIZES[:2]) // DK
    k_blk = sum(EVEN_SIZES[:3]) // DK
    v_blk = sum(EVEN_SIZES[:4]) // DV
    g_blk = sum(EVEN_SIZES[:5]) // DV
    up = jnp.zeros((N_DIR, LANES, GLA_DK_W), F32)
    for d in range(N_DIR):
        up = up.at[d, d * GLA_RANK:(d + 1) * GLA_RANK].set(dec_up[d])
    db = dec_b.reshape(N_DIR, 1, GLA_DK_W)
    nw = gla_nw.reshape(1, GLA_DV_W)

    def chunk(d, c):
        return c + d * (n - 1 - 2 * c)

    def out_chunk(d, c):
        return (n - 1) - d * c
    out, sfin = pl.pallas_call(
        functools.partial(_gla_block_kernel, n),
        grid=(bsz, H, N_DIR, n),
        in_specs=[pl.BlockSpec((1, C, DK), lambda b, h, d, c: (b, chunk(d, c), q_blk + h)),
                  pl.BlockSpec((1, C, DK), lambda b, h, d, c: (b, chunk(d, c), k_blk + h)),
                  pl.BlockSpec((1, C, DV), lambda b, h, d, c: (b, chunk(d, c), v_blk + h)),
                  pl.BlockSpec((1, C, DV), lambda b, h, d, c: (b, chunk(d, c), g_blk + h)),
                  pl.BlockSpec((1, C, LANES), lambda b, h, d, c: (b, chunk(d, c), 0)),
                  pl.BlockSpec((1, LANES, DK), lambda b, h, d, c: (d, 0, h)),
                  pl.BlockSpec((1, 1, DK), lambda b, h, d, c: (d, 0, h)),
                  pl.BlockSpec((1, DV), lambda b, h, d, c: (0, h)),
                  pl.BlockSpec((1, 1, 1, DK, DV), lambda b, h, d, c: (b, d, h, 0, 0))],
        out_specs=[pl.BlockSpec((1, C, DV), lambda b, h, d, c: (b, out_chunk(d, c), h)),
                   pl.BlockSpec((1, 1, 1, DK, DV), lambda b, h, d, c: (b, d, h, 0, 0))],
        out_shape=[jax.ShapeDtypeStruct((bsz, L, GLA_DV_W), BF16),
                   jax.ShapeDtypeStruct((bsz, N_DIR, H, DK, DV), F32)],
        scratch_shapes=[pltpu.VMEM((DK, DV), F32), pltpu.VMEM((L, DV), F32),
                        pltpu.VMEM((GLA_NC, GLA_CHUNK, DK), BF16), pltpu.VMEM((GLA_NC, GLA_CHUNK, DV), F32),
                        pltpu.VMEM((GLA_NC, DK, DV), F32), pltpu.VMEM((GLA_NC, DK, LANES), F32)],
        compiler_params=pltpu.CompilerParams(
            dimension_semantics=("arbitrary",) * 4, vmem_limit_bytes=VMEM_LIMIT),
        name="gla_chunk_scan",
    )(main, main, main, main, dec_lr, up, db, nw, s0)
    return out, sfin


def _rwkv_kernel(n_blk, r_ref, v_ref, a_ref, w_ref, k_ref, b_ref, s0_ref, y_ref, sfin_ref,
                 s_ref, vc_ref, sr_ref):
    TB, N, GT = RWKV_TBLK, RWKV_HEAD, RWKV_GROUP_T
    d = pl.program_id(2)
    c = pl.program_id(3)

    @pl.when(c == 0)
    def _():
        s_ref[...] = s0_ref[0, 0]

    lane = lax.broadcasted_iota(jnp.int32, (N, LANES), 1)
    row = lax.broadcasted_iota(jnp.int32, (N, LANES), 0)
    lo_half = lane < N
    diag = (lane % N) == row
    same_head = ((lax.broadcasted_iota(jnp.int32, (LANES, LANES), 0) < N)
                 == (lax.broadcasted_iota(jnp.int32, (LANES, LANES), 1) < N)).astype(BF16)
    fwd = d == 0
    y_ref[...] = jnp.zeros_like(y_ref)

    def row_of(tile, j):
        return jnp.where(fwd, tile[j:j + 1], tile[GT - 1 - j:GT - j])

    def group(i, carry):
        g8 = jnp.where(fwd, i, TB // GT - 1 - i)
        rs = pl.ds(pl.multiple_of(g8 * GT, GT), GT)
        v_tile = v_ref[0, rs, :]
        for p in range(RWKV_PAIRS):
            ls = slice(p * LANES, (p + 1) * LANES)
            vd = jnp.concatenate([jnp.where(diag, row_of(v_tile, j)[:, ls], 0.0) for j in range(GT)], axis=0)
            vd_hi = vd.astype(BF16)
            vd_lo = (vd - vd_hi.astype(F32)).astype(BF16)
            vcol = (jnp.dot(vd_hi, same_head, preferred_element_type=F32)
                    + jnp.dot(vd_lo, same_head, preferred_element_type=F32))
            vc_ref[:, ls] = vcol
        tiles = (r_ref[0, rs, :], a_ref[0, rs, :], w_ref[0, 0, rs, :], k_ref[0, 0, rs, :], b_ref[0, 0, rs, :])
        for j in range(GT):
            r_row, a_row, w_row, k_row, b_row = (row_of(x, j) for x in tiles)
            for p in range(RWKV_PAIRS):
                ls = slice(p * LANES, (p + 1) * LANES)
                s = s_ref[:, ls]
                prod = s * a_row[:, ls]
                sa_lo = jnp.sum(jnp.where(lo_half, prod, 0.0), axis=1, keepdims=True)
                sa_hi = jnp.sum(jnp.where(lo_half, 0.0, prod), axis=1, keepdims=True)
                sa = jnp.where(lo_half, sa_lo, sa_hi)
                s_new = s * w_row[:, ls] + sa * b_row[:, ls] + vc_ref[j * N:(j + 1) * N, ls] * k_row[:, ls]
                s_ref[:, ls] = s_new
                sr_ref[j * N:(j + 1) * N, ls] = (s_new * r_row[:, ls]).astype(BF16)
        for p in range(RWKV_PAIRS):
            ls = slice(p * LANES, (p + 1) * LANES)
            ycol = jnp.dot(sr_ref[:, ls], same_head, preferred_element_type=F32)
            acc = jnp.zeros((N, LANES), F32)
            for j in range(GT):
                t = g8 * GT + jnp.where(fwd, j, GT - 1 - j)
                acc = jnp.where((lane % N) == t, ycol[j * N:(j + 1) * N], acc)
            y_ref[0, 0, 0, :, ls] = jnp.where((lane % N) // GT == g8, acc, y_ref[0, 0, 0, :, ls])
        return carry
    lax.fori_loop(0, TB // GT, group, 0)

    @pl.when(c == n_blk - 1)
    def _():
        sfin_ref[0, 0] = s_ref[...]


def rwkv_scan(r, v, a_neg, decay, k_d, b_d, s0):
    bsz, L, W = r.shape
    TB, N, H = RWKV_TBLK, RWKV_HEAD, RWKV_HEADS
    n = L // TB
    gw = RWKV_PAIRS * LANES
    s0t = s0.transpose(0, 1, 3, 2, 4).reshape(bsz, N_DIR, N, W)

    def blk(d, c):
        return c + d * (n - 1 - 2 * c)
    seq = pl.BlockSpec((1, TB, gw), lambda b, g, d, c: (b, blk(d, c), g))
    seq_d = pl.BlockSpec((1, 1, TB, gw), lambda b, g, d, c: (d, b, blk(d, c), g))
    st = pl.BlockSpec((1, 1, N, gw), lambda b, g, d, c: (b, d, 0, g))
    gt = RWKV_GROUP_T
    yt, sfin = pl.pallas_call(
        functools.partial(_rwkv_kernel, n),
        grid=(bsz, W // gw, N_DIR, n),
        in_specs=[seq, seq, seq, seq_d, seq_d, seq_d, st],
        out_specs=[pl.BlockSpec((1, 1, 1, N, gw), lambda b, g, d, c: (d, b, blk(d, c), 0, g)), st],
        out_shape=[jax.ShapeDtypeStruct((N_DIR, bsz, n, N, W), F32),
                   jax.ShapeDtypeStruct((bsz, N_DIR, N, W), F32)],
        scratch_shapes=[pltpu.VMEM((N, gw), F32), pltpu.VMEM((gt * N, gw), F32), pltpu.VMEM((gt * N, gw), BF16)],
        compiler_params=pltpu.CompilerParams(
            dimension_semantics=("arbitrary",) * 4, vmem_limit_bytes=VMEM_LIMIT),
        name="rwkv_scan",
    )(r, v, a_neg, decay, k_d, b_d, s0t)
    yt = yt[0] + yt[1]
    wkv = yt.reshape(bsz, n, N, W // LANES, 2, TB).transpose(0, 1, 5, 3, 4, 2).reshape(bsz, L, W)
    sfin = sfin.reshape(bsz, N_DIR, N, H, N).transpose(0, 1, 3, 2, 4)
    return wkv, sfin


def _split_bf16(x):
    hi = x.astype(BF16)
    return hi, (x - hi.astype(F32)).astype(BF16)


def _mxu(x, y, dims=(((1,), (0,)), ((), ())), split=False):
    def d(a, b):
        return lax.dot_general(a, b, dims, preferred_element_type=F32)
    if not split:
        return d(x.astype(BF16), y.astype(BF16))
    xh, xl = _split_bf16(x)
    yh, yl = _split_bf16(y)
    return d(xh, yh) + (d(xh, yl) + d(xl, yh))


def _rwkv_chunk_kernel(n_chunks, r_ref, lw_ref, k_ref, v_ref, a_ref, b_ref, h0_ref, y_ref, hfin_ref, h_ref):
    T, N = RWKV_CHUNK, RWKV_HEAD
    d = pl.program_id(2)
    c = pl.program_id(3)

    @pl.when(c == 0)
    def _():
        h_ref[...] = h0_ref[0, 0]

    lane = lax.broadcasted_iota(jnp.int32, (T, LANES), 1)
    row = lax.broadcasted_iota(jnp.int32, (T, LANES), 0)
    lo = lane < N
    col = lane % N
    order = jnp.where(d == 0, row - col, col - row)
    seen = order >= 0
    before = order > 0
    eye = row == col
    sq_r = lax.broadcasted_iota(jnp.int32, (T, T), 0)
    sq_c = lax.broadcasted_iota(jnp.int32, (T, T), 1)
    seen_sq = (jnp.where(d == 0, sq_r - sq_c, sq_c - sq_r) >= 0).astype(F32)
    row_dims = (((0,), (0,)), ((), ()))
    lane_dims = (((1,), (1,)), ((), ()))

    def bd(x):
        return jnp.concatenate([jnp.where(lo, x, 0.0), jnp.where(lo, 0.0, x)], axis=0)

    def pp(x, y, split=False):
        return _mxu(x, bd(y), split=split)

    def ptp(x, y):
        full = _mxu(x, y, row_dims, split=True)
        return jnp.where(lo, full[:N], full[N:])

    lw_all = lw_ref[0, 0]
    cs_all = jnp.dot(seen_sq, lw_all, precision=HI, preferred_element_type=F32)
    tot_all = jnp.sum(lw_all, axis=0, keepdims=True)
    pairs = range(RWKV_CPAIRS)
    sl = [slice(p * LANES, (p + 1) * LANES) for p in pairs]
    cs = [cs_all[:, s] for s in sl]
    tot = [tot_all[:, s] for s in sl]
    e_out = [jnp.exp(-cs[p]) for p in pairs]
    at = [a_ref[0, :, sl[p]] * jnp.exp(cs[p] - lw_all[:, sl[p]]) for p in pairs]
    rt = [r_ref[0, :, sl[p]] * jnp.exp(cs[p]) for p in pairs]
    ar = [jnp.concatenate([at[p], rt[p]], axis=0) for p in pairs]
    g1 = [_mxu(ar[p], bd(b_ref[0, 0, :, sl[p]] * e_out[p]), lane_dims) for p in pairs]
    g2 = [_mxu(ar[p], bd(k_ref[0, 0, :, sl[p]] * e_out[p]), lane_dims) for p in pairs]
    a_ab = [jnp.where(before, g1[p][:T], 0.0) for p in pairs]
    a_rb = [jnp.where(seen, g1[p][T:], 0.0) for p in pairs]
    a_ak = [jnp.where(before, g2[p][:T], 0.0) for p in pairs]
    a_rk = [jnp.where(seen, g2[p][T:], 0.0) for p in pairs]
    w = [jnp.where(eye, 1.0, a_ab[p]) for p in pairs]
    apow = a_ab
    for _ in range(5):
        apow = [pp(apow[p], apow[p]) for p in pairs]
        w = [w[p] + pp(w[p], apow[p]) for p in pairs]
    v = [v_ref[0, :, sl[p]] for p in pairs]
    akv = [pp(a_ak[p], v[p]) for p in pairs]
    u_loc = [pp(w[p], akv[p]) for p in pairs]
    a_hat = [pp(w[p], at[p]) for p in pairs]
    h0 = [h_ref[:, sl[p]] for p in pairs]
    q_hat = [rt[p] + pp(a_rb[p], a_hat[p]) for p in pairs]
    y_loc = [pp(a_rb[p], u_loc[p]) + pp(a_rk[p], v[p]) for p in pairs]
    for p in pairs:
        y_ref[0, 0, :, sl[p]] = pp(q_hat[p], h0[p]) + y_loc[p]
    e_end = [jnp.exp(tot[p] - cs[p]) for p in pairs]
    bh = [b_ref[0, 0, :, sl[p]] * e_end[p] for p in pairs]
    phi = [jnp.where(eye, jnp.exp(tot[p]), 0.0) + ptp(bh[p], a_hat[p]) for p in pairs]
    gam = [ptp(jnp.concatenate([bh[p], k_ref[0, 0, :, sl[p]] * e_end[p]], axis=0),
               jnp.concatenate([u_loc[p], v[p]], axis=0)) for p in pairs]
    for p in pairs:
        h_ref[:, sl[p]] = pp(phi[p], h0[p], split=True) + gam[p]

    @pl.when(c == n_chunks - 1)
    def _():
        hfin_ref[0, 0] = h_ref[...]


def rwkv_chunk_scan(r, v, a_neg, log_decay, k_d, b_d, s0):
    bsz, L, W = r.shape
    T, N, H = RWKV_CHUNK, RWKV_HEAD, RWKV_HEADS
    n = L // T
    gw = RWKV_CPAIRS * LANES
    h0 = s0.transpose(0, 1, 4, 2, 3).reshape(bsz, N_DIR, N, W)

    def blk(d, c):
        return c + d * (n - 1 - 2 * c)
    seq = pl.BlockSpec((1, T, gw), lambda b, g, d, c: (b, blk(d, c), g))
    seq_d = pl.BlockSpec((1, 1, T, gw), lambda b, g, d, c: (d, b, blk(d, c), g))
    st = pl.BlockSpec((1, 1, N, gw), lambda b, g, d, c: (b, d, 0, g))
    y, hfin = pl.pallas_call(
        functools.partial(_rwkv_chunk_kernel, n),
        grid=(bsz, W // gw, N_DIR, n),
        in_specs=[seq, seq_d, seq_d, seq, seq, seq_d, st],
        out_specs=[seq_d, st],
        out_shape=[jax.ShapeDtypeStruct((N_DIR, bsz, L, W), F32),
                   jax.ShapeDtypeStruct((bsz, N_DIR, N, W), F32)],
        scratch_shapes=[pltpu.VMEM((N, gw), F32)],
        compiler_params=pltpu.CompilerParams(
            dimension_semantics=("arbitrary",) * 4, vmem_limit_bytes=VMEM_LIMIT),
        name="rwkv_chunk_scan",
    )(r, log_decay, k_d, v, a_neg, b_d, h0)
    sfin = hfin.reshape(bsz, N_DIR, N, H, N).transpose(0, 1, 3, 4, 2)
    return y[0] + y[1], sfin


def _rwkv_fs_kernel(n_chunks, rev, r_ref, lw_ref, k_ref, v_ref, a_ref, b_ref, h0_ref, y_ref, hfin_ref, h_ref):
    T, N, SB = RWKV_CHUNK, RWKV_HEAD, RWKV_SUB
    NB = T // SB
    c = pl.program_id(2)

    @pl.when(c == 0)
    def _():
        h_ref[...] = h0_ref[0]

    lane = lax.broadcasted_iota(jnp.int32, (T, LANES), 1)
    row = lax.broadcasted_iota(jnp.int32, (T, LANES), 0)
    lo = lane < N
    col = lane % N
    order = (col - row) if rev else (row - col)
    seen = order >= 0
    before = order > 0
    eye = row == col
    sq_r = lax.broadcasted_iota(jnp.int32, (T, T), 0)
    sq_c = lax.broadcasted_iota(jnp.int32, (T, T), 1)
    seen_sq = (((sq_c - sq_r) if rev else (sq_r - sq_c)) >= 0).astype(F32)
    same_head = ((lax.broadcasted_iota(jnp.int32, (LANES, LANES), 0) < N)
                 == (lax.broadcasted_iota(jnp.int32, (LANES, LANES), 1) < N)).astype(BF16)
    col_sb = lax.broadcasted_iota(jnp.int32, (SB, LANES), 1) % N
    row_dims = (((0,), (0,)), ((), ()))
    lane_dims = (((1,), (1,)), ((), ()))

    def bd(x):
        return jnp.concatenate([jnp.where(lo, x, 0.0), jnp.where(lo, 0.0, x)], axis=0)

    def pp(x, y, split=False):
        return _mxu(x, bd(y), split=split)

    def ptp(x, y):
        full = _mxu(x, y, row_dims, split=True)
        return jnp.where(lo, full[:N], full[N:])

    lw_all = lw_ref[0]
    cs_all = jnp.dot(seen_sq, lw_all, precision=HI, preferred_element_type=F32)
    tot_all = jnp.sum(lw_all, axis=0, keepdims=True)
    pairs = range(RWKV_CPAIRS)
    sl = [slice(p * LANES, (p + 1) * LANES) for p in pairs]
    cs = [cs_all[:, s] for s in sl]
    tot = [tot_all[:, s] for s in sl]
    e_out = [jnp.exp(-cs[p]) for p in pairs]
    at = [a_ref[0, :, sl[p]] * jnp.exp(cs[p] - lw_all[:, sl[p]]) for p in pairs]
    rt = [r_ref[0, :, sl[p]] * jnp.exp(cs[p]) for p in pairs]
    ar = [jnp.concatenate([at[p], rt[p]], axis=0) for p in pairs]
    g1 = [_mxu(ar[p], bd(b_ref[0, :, sl[p]] * e_out[p]), lane_dims) for p in pairs]
    g2 = [_mxu(ar[p], bd(k_ref[0, :, sl[p]] * e_out[p]), lane_dims) for p in pairs]
    a_ab = [jnp.where(before, g1[p][:T], 0.0) for p in pairs]
    a_rb = [jnp.where(seen, g1[p][T:], 0.0) for p in pairs]
    a_ak = [jnp.where(before, g2[p][:T], 0.0) for p in pairs]
    a_rk = [jnp.where(seen, g2[p][T:], 0.0) for p in pairs]
    v = [v_ref[0, :, sl[p]] for p in pairs]
    akv = [pp(a_ak[p], v[p]) for p in pairs]
    za = [[None] * NB for _ in pairs]
    zu = [[None] * NB for _ in pairs]
    zero_blk = jnp.zeros((SB, LANES), F32)
    for kpos in range(NB):
        bk = NB - 1 - kpos if rev else kpos
        rows = slice(bk * SB, (bk + 1) * SB)
        done = [(m > bk) if rev else (m < bk) for m in range(NB)]
        cur_a = [at[p][rows] for p in pairs]
        cur_u = [akv[p][rows] for p in pairs]
        if kpos > 0:
            for p in pairs:
                zc_a = jnp.concatenate([za[p][m] if done[m] else zero_blk for m in range(NB)], axis=0)
                zc_u = jnp.concatenate([zu[p][m] if done[m] else zero_blk for m in range(NB)], axis=0)
                off = _mxu(a_ab[p][rows], jnp.concatenate([bd(zc_a), bd(zc_u)], axis=1))
                cur_a[p] = cur_a[p] + off[:, :LANES]
                cur_u[p] = cur_u[p] + off[:, LANES:]
        abc = []
        for p in pairs:
            ablk = a_ab[p][rows]
            picked = jnp.concatenate([jnp.where(col_sb == bk * SB + s, ablk, 0.0) for s in range(SB)], axis=0)
            abc.append(jnp.dot(picked.astype(BF16), same_head, preferred_element_type=F32))
        for j in range(SB - 1):
            s = SB - 1 - j if rev else j
            for p in pairs:
                coef = abc[p][s * SB:(s + 1) * SB]
                cur_a[p] = cur_a[p] + coef * cur_a[p][s:s + 1]
                cur_u[p] = cur_u[p] + coef * cur_u[p][s:s + 1]
        for p in pairs:
            za[p][bk] = cur_a[p]
            zu[p][bk] = cur_u[p]
    a_hat = [jnp.concatenate(za[p], axis=0) for p in pairs]
    u_loc = [jnp.concatenate(zu[p], axis=0) for p in pairs]
    h0 = [h_ref[:, sl[p]] for p in pairs]
    q_hat = [rt[p] + pp(a_rb[p], a_hat[p]) for p in pairs]
    y_loc = [pp(a_rb[p], u_loc[p]) + pp(a_rk[p], v[p]) for p in pairs]
    for p in pairs:
        y_ref[0, :, sl[p]] = pp(q_hat[p], h0[p]) + y_loc[p]
    e_end = [jnp.exp(tot[p] - cs[p]) for p in pairs]
    bh = [b_ref[0, :, sl[p]] * e_end[p] for p in pairs]
    phi = [jnp.where(eye, jnp.exp(tot[p]), 0.0) + ptp(bh[p], a_hat[p]) for p in pairs]
    gam = [ptp(jnp.concatenate([bh[p], k_ref[0, :, sl[p]] * e_end[p]], axis=0),
               jnp.concatenate([u_loc[p], v[p]], axis=0)) for p in pairs]
    for p in pairs:
        h_ref[:, sl[p]] = pp(phi[p], h0[p], split=True) + gam[p]

    @pl.when(c == n_chunks - 1)
    def _():
        hfin_ref[0] = h_ref[...]


def rwkv_direction(rev, r, v, a_neg, log_decay, k_d, b_d, s0):
    bsz, L, W = r.shape
    T, N, H = RWKV_CHUNK, RWKV_HEAD, RWKV_HEADS
    n = L // T
    gw = RWKV_CPAIRS * LANES
    h0 = s0.transpose(0, 3, 1, 2).reshape(bsz, N, W)
    seq = pl.BlockSpec((1, T, gw), lambda b, g, c: (b, (n - 1 - c) if rev else c, g))
    st = pl.BlockSpec((1, N, gw), lambda b, g, c: (b, 0, g))
    y, hfin = pl.pallas_call(
        functools.partial(_rwkv_fs_kernel, n, rev),
        grid=(bsz, W // gw, n),
        in_specs=[seq, seq, seq, seq, seq, seq, st],
        out_specs=[seq, st],
        out_shape=[jax.ShapeDtypeStruct((bsz, L, W), F32), jax.ShapeDtypeStruct((bsz, N, W), F32)],
        scratch_shapes=[pltpu.VMEM((N, gw), F32)],
        compiler_params=pltpu.CompilerParams(
            dimension_semantics=("arbitrary",) * 3, vmem_limit_bytes=VMEM_LIMIT),
        name="rwkv_bwd_chunks" if rev else "rwkv_fwd_chunks",
    )(r, log_decay, k_d, v, a_neg, b_d, h0)
    return y, hfin.reshape(bsz, N, H, N).transpose(0, 2, 3, 1)


def _split_cols(t, sizes):
    offsets, acc = [], 0
    for s in sizes[:-1]:
        acc += s
        offsets.append(acc)
    return jnp.split(t, offsets, axis=-1)


def _rms(x, w):
    return x * lax.rsqrt(jnp.mean(x * x, axis=-1, keepdims=True) + EPS) * w


def _adaln(cond, w, b):
    m = jnp.dot(jax.nn.silu(cond), w, precision=HI) + b
    shift, scale, gate = jnp.split(m, 3, axis=-1)
    return shift[..., None, :], scale[..., None, :], gate[..., None, :]


def _grid_pos_embed(n_tokens):
    rows = n_tokens // GRID_W
    row_id = jnp.broadcast_to(jnp.arange(rows, dtype=F32)[:, None], (rows, GRID_W)).reshape(-1)
    col_id = jnp.broadcast_to(jnp.arange(GRID_W, dtype=F32)[None, :], (rows, GRID_W)).reshape(-1)
    quarter = D_MODEL // 4
    omega = 1.0 / (POS_BASE ** (jnp.arange(quarter, dtype=F32) / quarter))

    def axis_emb(pos):
        ang = pos[:, None] * omega[None, :]
        return jnp.concatenate([jnp.sin(ang), jnp.cos(ang)], axis=-1)
    return jnp.concatenate([axis_emb(row_id), axis_emb(col_id)], axis=-1)


def _even_mixer(h, s5_re0, s5_im0, gla0, w_in, w_out, s5_ops, glu_w, glu_b, dec_up, dec_b, gla_nw):
    bsz, L, _ = h.shape
    n_main = sum(EVEN_SIZES[:-1])
    main = _mm3(h, w_in[:, :n_main])
    w_tail = jnp.pad(w_in[:, n_main:], ((0, 0), (0, LANES - N_DIR * GLA_RANK)))
    dec_lr = _mm3(h, w_tail)
    u, s5_g = main[..., :S5_W], main[..., S5_W:2 * S5_W]
    y, fin_re, fin_im = s5_scan(u, s5_ops, s5_re0, s5_im0)
    gy = jax.nn.gelu(y)
    s5_out = gy * jax.nn.sigmoid(_mm3(gy, glu_w) + glu_b) * jax.nn.silu(s5_g)
    gla_out, fin_gla = gla_mix(main, dec_lr, dec_up, dec_b, gla_nw, gla0)
    out = _mm3(jnp.concatenate([s5_out.astype(BF16), gla_out], axis=-1), w_out)
    return out, fin_re, fin_im, fin_gla


def _odd_mixer(h, rwkv0, w_in, w_out, mu, w0, w2, a0, a2, k_k, k_a, r_k, lnx_w, lnx_b):
    bsz, L, _ = h.shape
    zero = jnp.zeros_like(h[:, :1])
    h_prev = jnp.concatenate([zero, h[:, :-1]], axis=1)
    h_next = jnp.concatenate([h[:, 1:], zero], axis=1)
    xs = h + mu[0] * (h_prev - h) + mu[1] * (h_next - h)
    n_main = sum(ODD_SIZES[:4])
    main = _mm3(xs, w_in[:, :n_main])
    tail = _mm3(xs, w_in[:, n_main:])
    r, k, v, g = _split_cols(main, ODD_SIZES[:4])
    w_lr, a_lr = _split_cols(tail, ODD_SIZES[4:])
    w_lr = jnp.tanh(w_lr).reshape(bsz, L, N_DIR, RWKV_DECAY_RANK)
    a_lr = a_lr.reshape(bsz, L, N_DIR, RWKV_ICLR_RANK)

    def heads(t):
        return t.reshape(bsz, L, RWKV_HEADS, RWKV_HEAD)
    kk = heads(k * k_k)
    kk = kk / jnp.maximum(jnp.sqrt(jnp.sum(kk * kk, axis=-1, keepdims=True)), 1e-12)
    kk = kk.reshape(bsz, L, RWKV_W)
    r_h, v_h = heads(r), heads(v)
    wkv, bonus = 0.0, 0.0
    finals = []
    for d in range(N_DIR):
        w_log = -jax.nn.softplus(-(w0[d] + _mm3(w_lr[:, :, d], w2[d]))) - 0.5
        log_decay = -jnp.exp(w_log)
        a = jax.nn.sigmoid(a0[d] + _mm3(a_lr[:, :, d], a2[d]))
        k_d = k * (1.0 + (a - 1.0) * k_a)
        y_d, fin = rwkv_direction(bool(d), r, v, -kk, log_decay, k_d, kk * a, rwkv0[:, d])
        wkv = wkv + y_d
        finals.append(fin)
        bonus = bonus + jnp.sum(r_h * heads(k_d) * r_k, axis=-1, keepdims=True) * v_h
    finals = jnp.stack(finals, axis=1)
    wkv = heads(wkv)
    mean = jnp.mean(wkv, axis=-1, keepdims=True)
    var = jnp.mean(jnp.square(wkv - mean), axis=-1, keepdims=True)
    ln = ((wkv - mean) * lax.rsqrt(var + RWKV_LNX_EPS) * lnx_w.reshape(RWKV_HEADS, RWKV_HEAD)
          + lnx_b.reshape(RWKV_HEADS, RWKV_HEAD))
    out = (ln + bonus).reshape(bsz, L, RWKV_W) * jax.nn.silu(g)
    return _mm3(out, w_out), finals


def kernel(x_prompt, x_sample, state_s5_re, state_s5_im, state_gla, state_rwkv, c, c_ctx, norm_w, ada_w, ada_b, final_norm_w, e_w_in, e_w_out, s5_lambda_re, s5_lambda_im, s5_log_step, s5_b_re, s5_b_im, s5_c_re, s5_c_im, s5_d, s5_glu_w, s5_glu_b, gla_decay_up, gla_decay_b, gla_norm_w, o_w_in, o_w_out, rwkv_mu, rwkv_w0, rwkv_w2, rwkv_a0, rwkv_a2, rwkv_k_k, rwkv_k_a, rwkv_r_k, rwkv_lnx_w, rwkv_lnx_b):
    bp = x_prompt.shape[0]
    depth = norm_w.shape[0]
    x_ctx = x_prompt
    x_lat = x_sample + _grid_pos_embed(x_sample.shape[1])[None]
    z_s5 = jnp.zeros((bp, N_DIR, S5_GROUPS, S5_STATE), F32)
    z_gla = jnp.zeros((bp, N_DIR, GLA_HEADS, GLA_DK, GLA_DV), F32)
    z_rwkv = jnp.zeros((bp, N_DIR, RWKV_HEADS, RWKV_HEAD, RWKV_HEAD), F32)
    new_s5_re, new_s5_im, new_gla, new_rwkv = [], [], [], []
    for i in range(depth):
        j = i // 2
        sh_c, sc_c, gt_c = _adaln(c_ctx, ada_w[i], ada_b[i])
        sh_l, sc_l, gt_l = _adaln(c, ada_w[i], ada_b[i])
        h_ctx = _rms(x_ctx, norm_w[i]) * (1.0 + sc_c) + sh_c
        h_lat = _rms(x_lat, norm_w[i]) * (1.0 + sc_l) + sh_l
        if i % 2 == 0:
            s5_ops = s5_operators(s5_lambda_re[j], s5_lambda_im[j], s5_log_step[j], s5_b_re[j], s5_b_im[j],
                                  s5_c_re[j], s5_c_im[j], s5_d[j])
            p = (e_w_in[j], e_w_out[j], s5_ops, s5_glu_w[j], s5_glu_b[j], gla_decay_up[j], gla_decay_b[j],
                 gla_norm_w[j])
            o_ctx, fr, fi, fg = _even_mixer(h_ctx, z_s5, z_s5, z_gla, *p)
            o_lat, _, _, _ = _even_mixer(h_lat, state_s5_re[:, j], state_s5_im[:, j], state_gla[:, j], *p)
            new_s5_re.append(fr)
            new_s5_im.append(fi)
            new_gla.append(fg)
        else:
            p = (o_w_in[j], o_w_out[j], rwkv_mu[j], rwkv_w0[j], rwkv_w2[j], rwkv_a0[j], rwkv_a2[j],
                 rwkv_k_k[j], rwkv_k_a[j], rwkv_r_k[j], rwkv_lnx_w[j], rwkv_lnx_b[j])
            o_ctx, fw = _odd_mixer(h_ctx, z_rwkv, *p)
            o_lat, _ = _odd_mixer(h_lat, state_rwkv[:, j], *p)
            new_rwkv.append(fw)
        x_ctx = x_ctx + gt_c * o_ctx
        x_lat = x_lat + gt_l * o_lat
    y_prompt = _rms(x_ctx, final_norm_w)
    y_sample = _rms(x_lat, final_norm_w)
    return (y_prompt, y_sample, jnp.stack(new_s5_re, axis=1), jnp.stack(new_s5_im, axis=1),
            jnp.stack(new_gla, axis=1), jnp.stack(new_rwkv, axis=1))
```

```python
import functools
import math

import jax
import jax.numpy as jnp
from jax import lax
from jax.experimental import pallas as pl
from jax.experimental.pallas import tpu as pltpu

D_MODEL = 2048
GRID_W = 64
POS_BASE = 10000.0
N_DIR = 2
EPS = 1e-6
S5_W = 1024
S5_GROUP_CH = 16
S5_GROUPS = 64
S5_STATE = 64
S5_CHUNK = 16
GLA_HEADS = 6
GLA_DV = 512
GLA_DK = 256
GLA_DK_W = 1536
GLA_DV_W = 3072
GLA_RANK = 16
GLA_NORMALIZER = 16.0
GLA_CHUNK = 64
GLA_NC = 4
GLA_LOG_DECAY_MIN = -1.0
EVEN_SIZES = (S5_W, S5_W, GLA_DK_W, GLA_DK_W, GLA_DV_W, GLA_DV_W, N_DIR * GLA_RANK)
RWKV_W = 2048
RWKV_HEAD = 64
RWKV_HEADS = 32
RWKV_DECAY_RANK = 96
RWKV_ICLR_RANK = 96
RWKV_LNX_EPS = 64e-5
ODD_SIZES = (RWKV_W, RWKV_W, RWKV_W, RWKV_W, N_DIR * RWKV_DECAY_RANK, N_DIR * RWKV_ICLR_RANK)
RWKV_TBLK = 64
RWKV_PAIRS = 8
RWKV_GROUP_T = 8
RWKV_CHUNK = 64
RWKV_CPAIRS = 8
RWKV_SUB = 16
LANES = 128

VMEM_LIMIT = 48 * 1024 * 1024
HI = lax.Precision.HIGHEST
BF16 = jnp.bfloat16
F32 = jnp.float32


def _mm_kernel(x_ref, w_ref, o_ref):
    o_ref[...] = jnp.dot(x_ref[...], w_ref[...], preferred_element_type=F32)


def _pick(n, prefs):
    for p in prefs:
        if n % p == 0:
            return p
    return n


def matmul(x, w):
    m, k = x.shape
    n = w.shape[1]
    x = x.astype(BF16)
    w = w.astype(BF16)
    tm = _pick(m, (1024, 512, 256, 128, 64, 32, 16, 8))
    tn = _pick(n, (512, 384, 256, 128))
    return pl.pallas_call(
        _mm_kernel,
        grid=(m // tm, n // tn),
        in_specs=[pl.BlockSpec((tm, k), lambda i, j: (i, 0)),
                  pl.BlockSpec((k, tn), lambda i, j: (0, j))],
        out_specs=pl.BlockSpec((tm, tn), lambda i, j: (i, j)),
        out_shape=jax.ShapeDtypeStruct((m, n), F32),
        compiler_params=pltpu.CompilerParams(
            dimension_semantics=("arbitrary", "arbitrary"), vmem_limit_bytes=VMEM_LIMIT),
        name="proj_matmul",
    )(x, w)


def _mm3(h, w):
    b, l, k = h.shape
    return matmul(h.reshape(b * l, k), w).reshape(b, l, -1)


def _mm_residual_kernel(x_ref, w_ref, res_ref, gate_ref, o_ref):
    acc = jnp.dot(x_ref[...], w_ref[...], preferred_element_type=F32)
    o_ref[...] = res_ref[...] + gate_ref[0] * acc


def matmul_gated_residual(x, w, res, gate):
    bsz, L, k = x.shape
    n = w.shape[1]
    m = bsz * L
    tm = _pick(L, (1024, 512, 256, 128))
    tn = _pick(n, (512, 256, 128))
    per_b = L // tm
    out = pl.pallas_call(
        _mm_residual_kernel,
        grid=(m // tm, n // tn),
        in_specs=[pl.BlockSpec((tm, k), lambda i, j: (i, 0)),
                  pl.BlockSpec((k, tn), lambda i, j: (0, j)),
                  pl.BlockSpec((tm, tn), lambda i, j: (i, j)),
                  pl.BlockSpec((1, 1, tn), lambda i, j: (i // per_b, 0, j))],
        out_specs=pl.BlockSpec((tm, tn), lambda i, j: (i, j)),
        out_shape=jax.ShapeDtypeStruct((m, n), F32),
        compiler_params=pltpu.CompilerParams(
            dimension_semantics=("arbitrary", "arbitrary"), vmem_limit_bytes=VMEM_LIMIT),
        name="proj_residual",
    )(x.reshape(m, k).astype(BF16), w.astype(BF16), res.reshape(m, n), gate.reshape(bsz, 1, n))
    return out.reshape(bsz, L, n)


def _mm_glu_kernel(x_ref, w_ref, b_ref, g_ref, xt_ref, o_ref):
    acc = jnp.dot(x_ref[...], w_ref[...], preferred_element_type=F32) + b_ref[...]
    gy = xt_ref[...].astype(F32)
    gate = g_ref[...]
    o_ref[...] = (gy * jax.nn.sigmoid(acc) * (gate * jax.nn.sigmoid(gate))).astype(o_ref.dtype)


def matmul_glu(gy, w, b, main, g_col0, n_total):
    m, k = gy.shape
    n = w.shape[1]
    tm = _pick(m, (1024, 512, 256, 128))
    tn = 512
    return pl.pallas_call(
        _mm_glu_kernel,
        grid=(m // tm, n // tn),
        in_specs=[pl.BlockSpec((tm, k), lambda i, j: (i, 0)),
                  pl.BlockSpec((k, tn), lambda i, j: (0, j)),
                  pl.BlockSpec((1, tn), lambda i, j: (0, j)),
                  pl.BlockSpec((tm, tn), lambda i, j: (i, g_col0 // tn + j)),
                  pl.BlockSpec((tm, tn), lambda i, j: (i, j))],
        out_specs=pl.BlockSpec((tm, tn), lambda i, j: (i, j)),
        out_shape=jax.ShapeDtypeStruct((m, n_total), BF16),
        compiler_params=pltpu.CompilerParams(
            dimension_semantics=("arbitrary", "arbitrary"), vmem_limit_bytes=VMEM_LIMIT),
        name="s5_glu_gate",
    )(gy, w.astype(BF16), b.reshape(1, n), main, gy)


def _norm_mod_kernel(x_ref, nw_ref, sc_ref, sh_ref, o_ref):
    x = x_ref[0]
    inv = lax.rsqrt(jnp.mean(x * x, axis=-1, keepdims=True) + EPS)
    o_ref[0] = (x * inv * nw_ref[...] * (1.0 + sc_ref[0]) + sh_ref[0]).astype(o_ref.dtype)


def norm_mod(x, nw, scale, shift):
    bsz, L, dm = x.shape
    tr = _pick(L, (256, 128, 64))
    nb = scale.shape[0]
    cond = pl.BlockSpec((1, 1, dm), lambda b, i: (b if nb > 1 else 0, 0, 0))
    return pl.pallas_call(
        _norm_mod_kernel,
        grid=(bsz, L // tr),
        in_specs=[pl.BlockSpec((1, tr, dm), lambda b, i: (b, i, 0)),
                  pl.BlockSpec((1, dm), lambda b, i: (0, 0)), cond, cond],
        out_specs=pl.BlockSpec((1, tr, dm), lambda b, i: (b, i, 0)),
        out_shape=jax.ShapeDtypeStruct((bsz, L, dm), BF16),
        compiler_params=pltpu.CompilerParams(
            dimension_semantics=("arbitrary", "arbitrary"), vmem_limit_bytes=VMEM_LIMIT),
        name="norm_mod",
    )(x, nw.reshape(1, dm), scale.reshape(nb, 1, dm), shift.reshape(nb, 1, dm))


def _final_norm_kernel(x_ref, nw_ref, o_ref):
    x = x_ref[0]
    o_ref[0] = x * lax.rsqrt(jnp.mean(x * x, axis=-1, keepdims=True) + EPS) * nw_ref[...]


def final_norm(x, nw):
    bsz, L, dm = x.shape
    tr = _pick(L, (256, 128, 64))
    return pl.pallas_call(
        _final_norm_kernel,
        grid=(bsz, L // tr),
        in_specs=[pl.BlockSpec((1, tr, dm), lambda b, i: (b, i, 0)), pl.BlockSpec((1, dm), lambda b, i: (0, 0))],
        out_specs=pl.BlockSpec((1, tr, dm), lambda b, i: (b, i, 0)),
        out_shape=jax.ShapeDtypeStruct((bsz, L, dm), F32),
        compiler_params=pltpu.CompilerParams(
            dimension_semantics=("arbitrary", "arbitrary"), vmem_limit_bytes=VMEM_LIMIT),
        name="final_norm",
    )(x, nw.reshape(1, dm))


def s5_operators(lam_re, lam_im, log_step, b_re, b_im, c_re, c_im, d_skip):
    T = S5_CHUNK
    dt = jnp.exp(log_step)[..., None]
    mag = jnp.exp(lam_re * dt)
    ab_re, ab_im = mag * jnp.cos(lam_im * dt), mag * jnp.sin(lam_im * dt)
    den = lam_re * lam_re + lam_im * lam_im
    f_re = ((ab_re - 1.0) * lam_re + ab_im * lam_im) / den
    f_im = (ab_im * lam_re - (ab_re - 1.0) * lam_im) / den
    bb_re = f_re[..., None] * b_re - f_im[..., None] * b_im
    bb_im = f_re[..., None] * b_im + f_im[..., None] * b_re
    kk = jnp.arange(T + 1, dtype=F32)[:, None, None, None]
    pmag = jnp.exp(kk * (lam_re * dt))
    pr = pmag * jnp.cos(kk * (lam_im * dt))
    pi = pmag * jnp.sin(kk * (lam_im * dt))
    zr = pr[:T, :, :, :, None] * bb_re - pi[:T, :, :, :, None] * bb_im
    zi = pr[:T, :, :, :, None] * bb_im + pi[:T, :, :, :, None] * bb_re
    kern = (jnp.einsum('dghp,kdgpj->kdghj', c_re, zr, precision=HI)
            - jnp.einsum('dghp,kdgpj->kdghj', c_im, zi, precision=HI))
    t_idx = jnp.arange(T)[:, None]
    s_idx = jnp.arange(T)[None, :]
    lag_f = t_idx - s_idx
    lag_b = s_idx - t_idx
    m_f = jnp.where((lag_f >= 0)[:, :, None, None, None], kern[:, 0][jnp.clip(lag_f, 0, T - 1)], 0.0)
    m_b = jnp.where((lag_b >= 0)[:, :, None, None, None], kern[:, 1][jnp.clip(lag_b, 0, T - 1)], 0.0)
    m = m_f + m_b
    eye_t = jnp.eye(T, dtype=F32)[:, :, None, None, None]
    eye_h = jnp.eye(S5_GROUP_CH, dtype=F32)[None, None, None]
    m = m + eye_t * eye_h * d_skip.reshape(S5_GROUPS, S5_GROUP_CH)[None, None, :, :, None]
    g = m.shape[2]
    m_t = m.transpose(2, 1, 4, 0, 3).reshape(g, T * S5_GROUP_CH, T * S5_GROUP_CH)
    pf_r, pf_i = pr[T - 1::-1][:T, 0], pi[T - 1::-1][:T, 0]
    pb_r, pb_i = pr[:T, 1], pi[:T, 1]

    def f_mat(p_r, p_i, d):
        re = p_r[..., None] * bb_re[d][None] - p_i[..., None] * bb_im[d][None]
        im = p_r[..., None] * bb_im[d][None] + p_i[..., None] * bb_re[d][None]
        re = re.transpose(1, 0, 3, 2).reshape(g, T * S5_GROUP_CH, S5_STATE)
        im = im.transpose(1, 0, 3, 2).reshape(g, T * S5_GROUP_CH, S5_STATE)
        return re, im
    ff_re, ff_im = f_mat(pf_r, pf_i, 0)
    fb_re, fb_im = f_mat(pb_r, pb_i, 1)
    a_t = jnp.concatenate([m_t, ff_re, fb_re, ff_im, fb_im], axis=-1)
    ef_r, ef_i = pr[1:T + 1, 0], pi[1:T + 1, 0]
    eb_r, eb_i = pr[T:0:-1, 1], pi[T:0:-1, 1]

    def e_mat(p_r, p_i, d):
        er = c_re[d][None] * p_r[:, :, None, :] - c_im[d][None] * p_i[:, :, None, :]
        ei = -(c_re[d][None] * p_i[:, :, None, :] + c_im[d][None] * p_r[:, :, None, :])
        er = er.transpose(1, 3, 0, 2).reshape(g, S5_STATE, T * S5_GROUP_CH)
        ei = ei.transpose(1, 3, 0, 2).reshape(g, S5_STATE, T * S5_GROUP_CH)
        return er, ei
    efr, efi = e_mat(ef_r, ef_i, 0)
    ebr, ebi = e_mat(eb_r, eb_i, 1)
    e_t = jnp.concatenate([efr, ebr, efi, ebi], axis=1)
    lam_t = jnp.concatenate([pr[T, 0], pr[T, 1], pi[T, 0], pi[T, 1]], axis=-1)[:, None, :]
    return a_t.astype(BF16), e_t.astype(BF16), lam_t


def _s5_kernel(n_chunks, bsz, ut_ref, at_ref, et_ref, lam_ref, h0_ref, y_ref, hfin_ref, z_ref, hent_ref):
    P = S5_STATE
    z_ref[...] = jnp.dot(ut_ref[0], at_ref[0], preferred_element_type=F32)
    lam = lam_ref[0]
    a_re, a_im = lam[:, 0:2 * P], lam[:, 2 * P:4 * P]
    h0 = h0_ref[0]
    fwd_lanes = lax.broadcasted_iota(jnp.int32, (bsz, 2 * P), 1) < P

    def step(c, carry):
        h_re, h_im = carry
        rf = pl.ds(pl.multiple_of(c * bsz, 8), bsz)
        rb = pl.ds(pl.multiple_of((n_chunks - 1 - c) * bsz, 8), bsz)
        hent_ref[rf, 0:P] = h_re[:, 0:P]
        hent_ref[rb, P:2 * P] = h_re[:, P:2 * P]
        hent_ref[rf, 2 * P:3 * P] = h_im[:, 0:P]
        hent_ref[rb, 3 * P:4 * P] = h_im[:, P:2 * P]
        g_re = jnp.where(fwd_lanes, z_ref[rf, 4 * P:6 * P], z_ref[rb, 4 * P:6 * P])
        g_im = jnp.where(fwd_lanes, z_ref[rf, 6 * P:8 * P], z_ref[rb, 6 * P:8 * P])
        return a_re * h_re - a_im * h_im + g_re, a_re * h_im + a_im * h_re + g_im
    h_re, h_im = lax.fori_loop(0, n_chunks, step, (h0[:, 0:2 * P], h0[:, 2 * P:4 * P]))
    hfin_ref[0, :, 0:2 * P] = h_re
    hfin_ref[0, :, 2 * P:4 * P] = h_im
    y_ref[0] =z_ref[:, 0:4 * P] + jnp.dot(hent_ref[...].astype(BF16), et_ref[0], preferred_element_type=F32)


def s5_scan(u, ops, h0_re, h0_im):
    a_t, e_t, lam_t = ops
    b_real, L, _ = u.shape
    T, G, H, P = S5_CHUNK, S5_GROUPS, S5_GROUP_CH, S5_STATE
    n = L // T
    bsz = -(-b_real // 8) * 8
    cols = n * bsz
    ut = u.reshape(b_real, n, T, G, H).transpose(3, 1, 0, 2, 4).astype(BF16)
    ut = jnp.pad(ut, ((0, 0), (0, 0), (0, bsz - b_real), (0, 0), (0, 0))).reshape(G, cols, T * H)
    h0 = jnp.concatenate([h0_re[:, 0], h0_re[:, 1], h0_im[:, 0], h0_im[:, 1]], axis=-1)
    h0 = jnp.pad(h0.transpose(1, 0, 2), ((0, 0), (0, bsz - b_real), (0, 0)))
    yt, hfin = pl.pallas_call(
        functools.partial(_s5_kernel, n, bsz),
        grid=(G,),
        in_specs=[pl.BlockSpec((1, cols, T * H), lambda g: (g, 0, 0)),
                  pl.BlockSpec((1, T * H, 8 * P), lambda g: (g, 0, 0)),
                  pl.BlockSpec((1, 4 * P, T * H), lambda g: (g, 0, 0)),
                  pl.BlockSpec((1, 1, 4 * P), lambda g: (g, 0, 0)),
                  pl.BlockSpec((1, bsz, 4 * P), lambda g: (g, 0, 0))],
        out_specs=[pl.BlockSpec((1, cols, T * H), lambda g: (g, 0, 0)),
                   pl.BlockSpec((1, bsz, 4 * P), lambda g: (g, 0, 0))],
        out_shape=[jax.ShapeDtypeStruct((G, cols, T * H), F32),
                   jax.ShapeDtypeStruct((G, bsz, 4 * P), F32)],
        scratch_shapes=[pltpu.VMEM((cols, 8 * P), F32), pltpu.VMEM((cols, 4 * P), F32)],
        compiler_params=pltpu.CompilerParams(dimension_semantics=("arbitrary",), vmem_limit_bytes=VMEM_LIMIT),
        name="s5_chunk_scan",
    )(ut, a_t, e_t, lam_t, h0)
    y = yt.reshape(G, n, bsz, T, H)[:, :, :b_real].transpose(2, 1, 3, 0, 4).reshape(b_real, L, G * H)
    hfin = hfin[:, :b_real].transpose(1, 0, 2)
    fin_re = jnp.stack([hfin[..., 0:P], hfin[..., P:2 * P]], axis=1)
    fin_im = jnp.stack([hfin[..., 2 * P:3 * P], hfin[..., 3 * P:4 * P]], axis=1)
    return y, fin_re, fin_im


def _dot_t(a, b):
    return lax.dot_general(a, b, (((1,), (1,)), ((), ())), preferred_element_type=F32)


def _gla_kernel(n_chunks, q_ref, k_ref, v_ref, g_ref, lr_ref, up_ref, db_ref, nw_ref, s0_ref,
                out_ref, sfin_ref, s_ref, of_ref):
    C = GLA_CHUNK
    d = pl.program_id(2)
    c = pl.program_id(3)
    cidx = jnp.where(d == 0, c, n_chunks - 1 - c)

    @pl.when(c == 0)
    def _():
        s_ref[...] = s0_ref[0, 0, 0]

    z = jnp.dot(lr_ref[0], up_ref[0], precision=HI, preferred_element_type=F32) + db_ref[0]
    gc = jnp.maximum(jax.nn.log_sigmoid(z) * (1.0 / GLA_NORMALIZER), GLA_LOG_DECAY_MIN)
    row = lax.broadcasted_iota(jnp.int32, (C, C), 0)
    col = lax.broadcasted_iota(jnp.int32, (C, C), 1)
    seen = jnp.where(d == 0, row - col, col - row) >= 0
    bcum = jnp.dot(seen.astype(F32), gc, precision=HI, preferred_element_type=F32)
    b_last = jnp.sum(gc, axis=0, keepdims=True)
    q = q_ref[0] * (GLA_DK ** -0.5)
    k = k_ref[0]
    v = v_ref[0].astype(BF16)
    q_dec = (q * jnp.exp(bcum)).astype(BF16)
    k_inv = (k * jnp.exp(-bcum)).astype(BF16)
    k_end = (k * jnp.exp(b_last - bcum)).astype(BF16)
    att = jnp.where(seen, _dot_t(q_dec, k_inv), 0.0).astype(BF16)
    s_old = s_ref[...]
    o = (jnp.dot(att, v, preferred_element_type=F32)
         + jnp.dot(q_dec, s_old.astype(BF16), preferred_element_type=F32))
    eye = (lax.broadcasted_iota(jnp.int32, (GLA_DK, GLA_DK), 0)
           == lax.broadcasted_iota(jnp.int32, (GLA_DK, GLA_DK), 1)).astype(F32)
    bl_col = lax.dot_general(eye, jnp.broadcast_to(b_last, (8, GLA_DK)), (((1,), (1,)), ((), ())),
                             precision=HI, preferred_element_type=F32)[:, 0:1]
    kv = lax.dot_general(k_end, v, (((0,), (0,)), ((), ())), preferred_element_type=F32)
    s_ref[...] = jnp.exp(bl_col) * s_old + kv

    rows = pl.ds(pl.multiple_of(cidx * C, C), C)

    @pl.when(d == 0)
    def _():
        of_ref[rows, :] = o

    @pl.when(d == 1)
    def _():
        tot = of_ref[rows, :] + o
        nrm = tot * lax.rsqrt(jnp.mean(tot * tot, axis=-1, keepdims=True) + EPS) * nw_ref[0]
        gate = g_ref[0]
        out_ref[0] = (nrm * (gate * jax.nn.sigmoid(gate))).astype(out_ref.dtype)

    @pl.when(c == n_chunks - 1)
    def _():
        sfin_ref[0, 0, 0] = s_ref[...]


def _dot_mask(mask_bf16, x, x_rows_to_sublanes=False):
    def d(b):
        if x_rows_to_sublanes:
            return lax.dot_general(b, mask_bf16, (((0,), (0,)), ((), ())), preferred_element_type=F32)
        return jnp.dot(mask_bf16, b, preferred_element_type=F32)
    x1 = x.astype(BF16)
    r1 = x - x1.astype(F32)
    x2 = r1.astype(BF16)
    x3 = (r1 - x2.astype(F32)).astype(BF16)
    return d(x1) + (d(x2) + d(x3))


def _gla_block_kernel(n_blocks, q_ref, k_ref, v_ref, g_ref, lr_ref, up_ref, db_ref, nw_ref, s0_ref, dst_ref,
                      out_ref, sfin_ref, s_ref, of_ref, qd_ref, ov_ref, kv_ref, dc_ref):
    C, NC = GLA_CHUNK, GLA_NC
    R = C * NC
    d = pl.program_id(2)
    c = pl.program_id(3)
    bidx = jnp.where(d == 0, c, n_blocks - 1 - c)

    @pl.when(c == 0)
    def _():
        s_ref[...] = s0_ref[0, 0, 0]

    z = jnp.dot(lr_ref[0], up_ref[0], precision=HI, preferred_element_type=F32) + db_ref[0]
    gc = jnp.maximum(jax.nn.log_sigmoid(z) * (1.0 / GLA_NORMALIZER), GLA_LOG_DECAY_MIN)
    row = lax.broadcasted_iota(jnp.int32, (R, R), 0)
    col = lax.broadcasted_iota(jnp.int32, (R, R), 1)
    same_chunk = (row // C) == (col // C)
    order = jnp.where(d == 0, row - col, col - row)
    seen_blk = (same_chunk & (order >= 0)).astype(BF16)
    bcum = _dot_mask(seen_blk, gc)
    btot = _dot_mask(same_chunk.astype(BF16), gc)
    q_dec = (q_ref[0] * (GLA_DK ** -0.5) * jnp.exp(bcum)).astype(BF16)
    k = k_ref[0]
    k_inv = (k * jnp.exp(-bcum)).astype(BF16)
    k_end = (k * jnp.exp(btot - bcum)).astype(BF16)
    v = v_ref[0].astype(BF16)
    row_c = lax.broadcasted_iota(jnp.int32, (C, C), 0)
    col_c = lax.broadcasted_iota(jnp.int32, (C, C), 1)
    seen_c = jnp.where(d == 0, row_c - col_c, col_c - row_c) >= 0
    ones_c = jnp.ones((C, LANES), BF16)
    qd_ref[...] = q_dec.reshape(NC, C, GLA_DK)
    for i in range(NC):
        rs = slice(i * C, (i + 1) * C)
        att = jnp.where(seen_c, _dot_t(q_dec[rs], k_inv[rs]), 0.0).astype(BF16)
        ov_ref[i] = jnp.dot(att, v[rs], preferred_element_type=F32)
        kv_ref[i] = lax.dot_general(k_end[rs], v[rs], (((0,), (0,)), ((), ())), preferred_element_type=F32)
        dc_ref[i] = _dot_mask(ones_c, gc[rs], x_rows_to_sublanes=True)

    for i in range(NC):
        ci = jnp.where(d == 0, i, NC - 1 - i)
        s_old = s_ref[...]
        o = ov_ref[ci] + jnp.dot(qd_ref[ci], s_old.astype(BF16), preferred_element_type=F32)
        s_ref[...] = jnp.exp(dc_ref[ci][:, 0:1]) * s_old + kv_ref[ci]
        rows = pl.ds(pl.multiple_of(bidx * R + ci * C, C), C)
        orow = pl.ds(pl.multiple_of(ci * C, C), C)

        @pl.when(d == 0)
        def _():
            of_ref[rows, :] = o

        @pl.when(d == 1)
        def _():
            tot = of_ref[rows, :] + o
            nrm = tot * lax.rsqrt(jnp.mean(tot * tot, axis=-1, keepdims=True) + EPS) * nw_ref[0]
            gate = g_ref[0, orow, :]
            out_ref[0, orow, :] = (nrm * (gate * jax.nn.sigmoid(gate))).astype(out_ref.dtype)

    @pl.when(c == n_blocks - 1)
    def _():
        sfin_ref[0, 0, 0] = s_ref[...]


def gla_mix(main, dec_lr, dec_up, dec_b, gla_nw, s0, dst):
    bsz, L, _ = main.shape
    H, DK, DV = GLA_HEADS, GLA_DK, GLA_DV
    C = GLA_CHUNK * GLA_NC
    n = L // C
    q_blk = sum(EVEN_SIZES[:2]) // DK
    k_blk = sum(EVEN_SIZES[:3]) // DK
    v_blk = sum(EVEN_SIZES[:4]) // DV
    g_blk = sum(EVEN_SIZES[:5]) // DV
    up = jnp.zeros((N_DIR, LANES, GLA_DK_W), F32)
    for d in range(N_DIR):
        up = up.at[d, d * GLA_RANK:(d + 1) * GLA_RANK].set(dec_up[d])
    db = dec_b.reshape(N_DIR, 1, GLA_DK_W)
    nw = gla_nw.reshape(1, GLA_DV_W)

    def chunk(d, c):
        return c + d * (n - 1 - 2 * c)

    def out_chunk(d, c):
        return (n - 1) - d * c
    out, sfin = pl.pallas_call(
        functools.partial(_gla_block_kernel, n),
        grid=(bsz, H, N_DIR, n),
        in_specs=[pl.BlockSpec((1, C, DK), lambda b, h, d, c: (b, chunk(d, c), q_blk + h)),
                  pl.BlockSpec((1, C, DK), lambda b, h, d, c: (b, chunk(d, c), k_blk + h)),
                  pl.BlockSpec((1, C, DV), lambda b, h, d, c: (b, chunk(d, c), v_blk + h)),
                  pl.BlockSpec((1, C, DV), lambda b, h, d, c: (b, chunk(d, c), g_blk + h)),
                  pl.BlockSpec((1, C, LANES), lambda b, h, d, c: (b, chunk(d, c), 0)),
                  pl.BlockSpec((1, LANES, DK), lambda b, h, d, c: (d, 0, h)),
                  pl.BlockSpec((1, 1, DK), lambda b, h, d, c: (d, 0, h)),
                  pl.BlockSpec((1, DV), lambda b, h, d, c: (0, h)),
                  pl.BlockSpec((1, 1, 1, DK, DV), lambda b, h, d, c: (b, d, h, 0, 0)),
                  pl.BlockSpec(memory_space=pl.ANY)],
        input_output_aliases={9: 0},
        out_specs=[pl.BlockSpec((1, C, DV), lambda b, h, d, c: (b, out_chunk(d, c), S5_W // DV + h)),
                   pl.BlockSpec((1, 1, 1, DK, DV), lambda b, h, d, c: (b, d, h, 0, 0))],
        out_shape=[jax.ShapeDtypeStruct(dst.shape, BF16),
                   jax.ShapeDtypeStruct((bsz, N_DIR, H, DK, DV), F32)],
        scratch_shapes=[pltpu.VMEM((DK, DV), F32), pltpu.VMEM((L, DV), F32),
                        pltpu.VMEM((GLA_NC, GLA_CHUNK, DK), BF16), pltpu.VMEM((GLA_NC, GLA_CHUNK, DV), F32),
                        pltpu.VMEM((GLA_NC, DK, DV), F32), pltpu.VMEM((GLA_NC, DK, LANES), F32)],
        compiler_params=pltpu.CompilerParams(
            dimension_semantics=("arbitrary",) * 4, vmem_limit_bytes=VMEM_LIMIT),
        name="gla_chunk_scan",
    )(main, main, main, main, dec_lr, up, db, nw, s0, dst)
    return out, sfin


def _rwkv_kernel(n_blk, r_ref, v_ref, a_ref, w_ref, k_ref, b_ref, s0_ref, y_ref, sfin_ref,
                 s_ref, vc_ref, sr_ref):
    TB, N, GT = RWKV_TBLK, RWKV_HEAD, RWKV_GROUP_T
    d = pl.program_id(2)
    c = pl.program_id(3)

    @pl.when(c == 0)
    def _():
        s_ref[...] = s0_ref[0, 0]

    lane = lax.broadcasted_iota(jnp.int32, (N, LANES), 1)
    row = lax.broadcasted_iota(jnp.int32, (N, LANES), 0)
    lo_half = lane < N
    diag = (lane % N) == row
    same_head = ((lax.broadcasted_iota(jnp.int32, (LANES, LANES), 0) < N)
                 == (lax.broadcasted_iota(jnp.int32, (LANES, LANES), 1) < N)).astype(BF16)
    fwd = d == 0
    y_ref[...] = jnp.zeros_like(y_ref)

    def row_of(tile, j):
        return jnp.where(fwd, tile[j:j + 1], tile[GT - 1 - j:GT - j])

    def group(i, carry):
        g8 = jnp.where(fwd, i, TB // GT - 1 - i)
        rs = pl.ds(pl.multiple_of(g8 * GT, GT), GT)
        v_tile = v_ref[0, rs, :]
        for p in range(RWKV_PAIRS):
            ls = slice(p * LANES, (p + 1) * LANES)
            vd = jnp.concatenate([jnp.where(diag, row_of(v_tile, j)[:, ls], 0.0) for j in range(GT)], axis=0)
            vd_hi = vd.astype(BF16)
            vd_lo = (vd - vd_hi.astype(F32)).astype(BF16)
            vcol = (jnp.dot(vd_hi, same_head, preferred_element_type=F32)
                    + jnp.dot(vd_lo, same_head, preferred_element_type=F32))
            vc_ref[:, ls] = vcol
        tiles = (r_ref[0, rs, :], a_ref[0, rs, :], w_ref[0, 0, rs, :], k_ref[0, 0, rs, :], b_ref[0, 0, rs, :])
        for j in range(GT):
            r_row, a_row, w_row, k_row, b_row = (row_of(x, j) for x in tiles)
            for p in range(RWKV_PAIRS):
                ls = slice(p * LANES, (p + 1) * LANES)
                s = s_ref[:, ls]
                prod = s * a_row[:, ls]
                sa_lo = jnp.sum(jnp.where(lo_half, prod, 0.0), axis=1, keepdims=True)
                sa_hi = jnp.sum(jnp.where(lo_half, 0.0, prod), axis=1, keepdims=True)
                sa = jnp.where(lo_half, sa_lo, sa_hi)
                s_new = s * w_row[:, ls] + sa * b_row[:, ls] + vc_ref[j * N:(j + 1) * N, ls] * k_row[:, ls]
                s_ref[:, ls] = s_new
                sr_ref[j * N:(j + 1) * N, ls] = (s_new * r_row[:, ls]).astype(BF16)
        for p in range(RWKV_PAIRS):
            ls = slice(p * LANES, (p + 1) * LANES)
            ycol = jnp.dot(sr_ref[:, ls], same_head, preferred_element_type=F32)
            acc = jnp.zeros((N, LANES), F32)
            for j in range(GT):
                t = g8 * GT + jnp.where(fwd, j, GT - 1 - j)
                acc = jnp.where((lane % N) == t, ycol[j * N:(j + 1) * N], acc)
            y_ref[0, 0, 0, :, ls] = jnp.where((lane % N) // GT == g8, acc, y_ref[0, 0, 0, :, ls])
        return carry
    lax.fori_loop(0, TB // GT, group, 0)

    @pl.when(c == n_blk - 1)
    def _():
        sfin_ref[0, 0] = s_ref[...]


def rwkv_scan(r, v, a_neg, decay, k_d, b_d, s0):
    bsz, L, W = r.shape
    TB, N, H = RWKV_TBLK, RWKV_HEAD, RWKV_HEADS
    n = L // TB
    gw = RWKV_PAIRS * LANES
    s0t = s0.transpose(0, 1, 3, 2, 4).reshape(bsz, N_DIR, N, W)

    def blk(d, c):
        return c + d * (n - 1 - 2 * c)
    seq = pl.BlockSpec((1, TB, gw), lambda b, g, d, c: (b, blk(d, c), g))
    seq_d = pl.BlockSpec((1, 1, TB, gw), lambda b, g, d, c: (d, b, blk(d, c), g))
    st = pl.BlockSpec((1, 1, N, gw), lambda b, g, d, c: (b, d, 0, g))
    gt = RWKV_GROUP_T
    yt, sfin = pl.pallas_call(
        functools.partial(_rwkv_kernel, n),
        grid=(bsz, W // gw, N_DIR, n),
        in_specs=[seq, seq, seq, seq_d, seq_d, seq_d, st],
        out_specs=[pl.BlockSpec((1, 1, 1, N, gw), lambda b, g, d, c: (d, b, blk(d, c), 0, g)), st],
        out_shape=[jax.ShapeDtypeStruct((N_DIR, bsz, n, N, W), F32),
                   jax.ShapeDtypeStruct((bsz, N_DIR, N, W), F32)],
        scratch_shapes=[pltpu.VMEM((N, gw), F32), pltpu.VMEM((gt * N, gw), F32), pltpu.VMEM((gt * N, gw), BF16)],
        compiler_params=pltpu.CompilerParams(
            dimension_semantics=("arbitrary",) * 4, vmem_limit_bytes=VMEM_LIMIT),
        name="rwkv_scan",
    )(r, v, a_neg, decay, k_d, b_d, s0t)
    yt = yt[0] + yt[1]
    wkv = yt.reshape(bsz, n, N, W // LANES, 2, TB).transpose(0, 1, 5, 3, 4, 2).reshape(bsz, L, W)
    sfin = sfin.reshape(bsz, N_DIR, N, H, N).transpose(0, 1, 3, 2, 4)
    return wkv, sfin


def _split_bf16(x):
    hi = x.astype(BF16)
    return hi, (x - hi.astype(F32)).astype(BF16)


def _mxu(x, y, dims=(((1,), (0,)), ((), ())), split=False):
    def d(a, b):
        return lax.dot_general(a, b, dims, preferred_element_type=F32)
    if not split:
        return d(x.astype(BF16), y.astype(BF16))
    xh, xl = _split_bf16(x)
    yh, yl = _split_bf16(y)
    return d(xh, yh) + (d(xh, yl) + d(xl, yh))


def _rwkv_chunk_kernel(n_chunks, r_ref, lw_ref, k_ref, v_ref, a_ref, b_ref, h0_ref, y_ref, hfin_ref, h_ref):
    T, N = RWKV_CHUNK, RWKV_HEAD
    d = pl.program_id(2)
    c = pl.program_id(3)

    @pl.when(c == 0)
    def _():
        h_ref[...] = h0_ref[0, 0]

    lane = lax.broadcasted_iota(jnp.int32, (T, LANES), 1)
    row = lax.broadcasted_iota(jnp.int32, (T, LANES), 0)
    lo = lane < N
    col = lane % N
    order = jnp.where(d == 0, row - col, col - row)
    seen = order >= 0
    before = order > 0
    eye = row == col
    sq_r = lax.broadcasted_iota(jnp.int32, (T, T), 0)
    sq_c = lax.broadcasted_iota(jnp.int32, (T, T), 1)
    seen_sq = (jnp.where(d == 0, sq_r - sq_c, sq_c - sq_r) >= 0).astype(F32)
    row_dims = (((0,), (0,)), ((), ()))
    lane_dims = (((1,), (1,)), ((), ()))

    def bd(x):
        return jnp.concatenate([jnp.where(lo, x, 0.0), jnp.where(lo, 0.0, x)], axis=0)

    def pp(x, y, split=False):
        return _mxu(x, bd(y), split=split)

    def ptp(x, y):
        full = _mxu(x, y, row_dims, split=True)
        return jnp.where(lo, full[:N], full[N:])

    lw_all = lw_ref[0, 0]
    cs_all = jnp.dot(seen_sq, lw_all, precision=HI, preferred_element_type=F32)
    tot_all = jnp.sum(lw_all, axis=0, keepdims=True)
    pairs = range(RWKV_CPAIRS)
    sl = [slice(p * LANES, (p + 1) * LANES) for p in pairs]
    cs = [cs_all[:, s] for s in sl]
    tot = [tot_all[:, s] for s in sl]
    e_out = [jnp.exp(-cs[p]) for p in pairs]
    at = [a_ref[0, :, sl[p]] * jnp.exp(cs[p] - lw_all[:, sl[p]]) for p in pairs]
    rt = [r_ref[0, :, sl[p]] * jnp.exp(cs[p]) for p in pairs]
    ar = [jnp.concatenate([at[p], rt[p]], axis=0) for p in pairs]
    g1 = [_mxu(ar[p], bd(b_ref[0, 0, :, sl[p]] * e_out[p]), lane_dims) for p in pairs]
    g2 = [_mxu(ar[p], bd(k_ref[0, 0, :, sl[p]] * e_out[p]), lane_dims) for p in pairs]
    a_ab = [jnp.where(before, g1[p][:T], 0.0) for p in pairs]
    a_rb = [jnp.where(seen, g1[p][T:], 0.0) for p in pairs]
    a_ak = [jnp.where(before, g2[p][:T], 0.0) for p in pairs]
    a_rk = [jnp.where(seen, g2[p][T:], 0.0) for p in pairs]
    w = [jnp.where(eye, 1.0, a_ab[p]) for p in pairs]
    apow = a_ab
    for _ in range(5):
        apow = [pp(apow[p], apow[p]) for p in pairs]
        w = [w[p] + pp(w[p], apow[p]) for p in pairs]
    v = [v_ref[0, :, sl[p]] for p in pairs]
    akv = [pp(a_ak[p], v[p]) for p in pairs]
    u_loc = [pp(w[p], akv[p]) for p in pairs]
    a_hat = [pp(w[p], at[p]) for p in pairs]
    h0 = [h_ref[:, sl[p]] for p in pairs]
    q_hat = [rt[p] + pp(a_rb[p], a_hat[p]) for p in pairs]
    y_loc = [pp(a_rb[p], u_loc[p]) + pp(a_rk[p], v[p]) for p in pairs]
    for p in pairs:
        y_ref[0, 0, :, sl[p]] = pp(q_hat[p], h0[p]) + y_loc[p]
    e_end = [jnp.exp(tot[p] - cs[p]) for p in pairs]
    bh = [b_ref[0, 0, :, sl[p]] * e_end[p] for p in pairs]
    phi = [jnp.where(eye, jnp.exp(tot[p]), 0.0) + ptp(bh[p], a_hat[p]) for p in pairs]
    gam = [ptp(jnp.concatenate([bh[p], k_ref[0, 0, :, sl[p]] * e_end[p]], axis=0),
               jnp.concatenate([u_loc[p], v[p]], axis=0)) for p in pairs]
    for p in pairs:
        h_ref[:, sl[p]] = pp(phi[p], h0[p], split=True) + gam[p]

    @pl.when(c == n_chunks - 1)
    def _():
        hfin_ref[0, 0] = h_ref[...]


def rwkv_chunk_scan(r, v, a_neg, log_decay, k_d, b_d, s0):
    bsz, L, W = r.shape
    T, N, H = RWKV_CHUNK, RWKV_HEAD, RWKV_HEADS
    n = L // T
    gw = RWKV_CPAIRS * LANES
    h0 = s0.transpose(0, 1, 4, 2, 3).reshape(bsz, N_DIR, N, W)

    def blk(d, c):
        return c + d * (n - 1 - 2 * c)
    seq = pl.BlockSpec((1, T, gw), lambda b, g, d, c: (b, blk(d, c), g))
    seq_d = pl.BlockSpec((1, 1, T, gw), lambda b, g, d, c: (d, b, blk(d, c), g))
    st = pl.BlockSpec((1, 1, N, gw), lambda b, g, d, c: (b, d, 0, g))
    y, hfin = pl.pallas_call(
        functools.partial(_rwkv_chunk_kernel, n),
        grid=(bsz, W // gw, N_DIR, n),
        in_specs=[seq, seq_d, seq_d, seq, seq, seq_d, st],
        out_specs=[seq_d, st],
        out_shape=[jax.ShapeDtypeStruct((N_DIR, bsz, L, W), F32),
                   jax.ShapeDtypeStruct((bsz, N_DIR, N, W), F32)],
        scratch_shapes=[pltpu.VMEM((N, gw), F32)],
        compiler_params=pltpu.CompilerParams(
            dimension_semantics=("arbitrary",) * 4, vmem_limit_bytes=VMEM_LIMIT),
        name="rwkv_chunk_scan",
    )(r, log_decay, k_d, v, a_neg, b_d, h0)
    sfin = hfin.reshape(bsz, N_DIR, N, H, N).transpose(0, 1, 3, 4, 2)
    return y[0] + y[1], sfin


def _rwkv_fs_kernel(n_chunks, rev, r_ref, k_ref, v_ref, wp_ref, ap_ref, w0_ref, a0_ref, kk_ref, ka_ref, h0_ref,
                    y_ref, hfin_ref, h_ref):
    T, N, SB = RWKV_CHUNK, RWKV_HEAD, RWKV_SUB
    NB = T // SB
    c = pl.program_id(2)

    @pl.when(c == 0)
    def _():
        h_ref[...] = h0_ref[0]

    lane = lax.broadcasted_iota(jnp.int32, (T, LANES), 1)
    row = lax.broadcasted_iota(jnp.int32, (T, LANES), 0)
    lo = lane < N
    col = lane % N
    order = (col - row) if rev else (row - col)
    seen = order >= 0
    before = order > 0
    eye = row == col
    sq_r = lax.broadcasted_iota(jnp.int32, (T, T), 0)
    sq_c = lax.broadcasted_iota(jnp.int32, (T, T), 1)
    seen_sq = (((sq_c - sq_r) if rev else (sq_r - sq_c)) >= 0).astype(F32)
    same_head = ((lax.broadcasted_iota(jnp.int32, (LANES, LANES), 0) < N)
                 == (lax.broadcasted_iota(jnp.int32, (LANES, LANES), 1) < N)).astype(BF16)
    col_sb = lax.broadcasted_iota(jnp.int32, (SB, LANES), 1) % N
    row_dims = (((0,), (0,)), ((), ()))
    lane_dims = (((1,), (1,)), ((), ()))

    def bd(x):
        return jnp.concatenate([jnp.where(lo, x, 0.0), jnp.where(lo, 0.0, x)], axis=0)

    def pp(x, y, split=False):
        return _mxu(x, bd(y), split=split)

    def ptp(x, y):
        full = _mxu(x, y, row_dims, split=True)
        return jnp.where(lo, full[:N], full[N:])

    w_log = -jax.nn.softplus(-(wp_ref[0] + w0_ref[...])) - 0.5
    lw_all = -jnp.exp(w_log)
    iclr_all = jax.nn.sigmoid(ap_ref[0] + a0_ref[...])
    k_all = k_ref[0]
    kd_all = k_all * (1.0 + (iclr_all - 1.0) * ka_ref[...])
    kkr_all = k_all * kk_ref[...]
    cs_all = jnp.dot(seen_sq, lw_all, precision=HI, preferred_element_type=F32)
    tot_all = jnp.sum(lw_all, axis=0, keepdims=True)
    pairs = range(RWKV_CPAIRS)
    sl = [slice(p * LANES, (p + 1) * LANES) for p in pairs]
    sq_hi = [_split_bf16(kkr_all[:, s] * kkr_all[:, s]) for s in sl]
    ssq = [jnp.dot(sq_hi[p][0], same_head, preferred_element_type=F32)
           + jnp.dot(sq_hi[p][1], same_head, preferred_element_type=F32) for p in pairs]
    kk = [kkr_all[:, sl[p]] / jnp.maximum(jnp.sqrt(ssq[p]), 1e-12) for p in pairs]
    b_in = [kk[p] * iclr_all[:, sl[p]] for p in pairs]
    cs = [cs_all[:, s] for s in sl]
    tot = [tot_all[:, s] for s in sl]
    e_out = [jnp.exp(-cs[p]) for p in pairs]
    at = [-kk[p] * jnp.exp(cs[p] - lw_all[:, sl[p]]) for p in pairs]
    rt = [r_ref[0, :, sl[p]] * jnp.exp(cs[p]) for p in pairs]
    ar = [jnp.concatenate([at[p], rt[p]], axis=0) for p in pairs]
    g1 = [_mxu(ar[p], bd(b_in[p] * e_out[p]), lane_dims) for p in pairs]
    g2 = [_mxu(ar[p], bd(kd_all[:, sl[p]] * e_out[p]), lane_dims) for p in pairs]
    a_ab = [jnp.where(before, g1[p][:T], 0.0) for p in pairs]
    a_rb = [jnp.where(seen, g1[p][T:], 0.0) for p in pairs]
    a_ak = [jnp.where(before, g2[p][:T], 0.0) for p in pairs]
    a_rk = [jnp.where(seen, g2[p][T:], 0.0) for p in pairs]
    v = [v_ref[0, :, sl[p]] for p in pairs]
    akv = [pp(a_ak[p], v[p]) for p in pairs]
    za = [[None] * NB for _ in pairs]
    zu = [[None] * NB for _ in pairs]
    zero_blk = jnp.zeros((SB, LANES), F32)
    for kpos in range(NB):
        bk = NB - 1 - kpos if rev else kpos
        rows = slice(bk * SB, (bk + 1) * SB)
        done = [(m > bk) if rev else (m < bk) for m in range(NB)]
        cur_a = [at[p][rows] for p in pairs]
        cur_u = [akv[p][rows] for p in pairs]
        if kpos > 0:
            for p in pairs:
                zc_a = jnp.concatenate([za[p][m] if done[m] else zero_blk for m in range(NB)], axis=0)
                zc_u = jnp.concatenate([zu[p][m] if done[m] else zero_blk for m in range(NB)], axis=0)
                off = _mxu(a_ab[p][rows], jnp.concatenate([bd(zc_a), bd(zc_u)], axis=1))
                cur_a[p] = cur_a[p] + off[:, :LANES]
                cur_u[p] = cur_u[p] + off[:, LANES:]
        abc = []
        for p in pairs:
            ablk = a_ab[p][rows]
            picked = jnp.concatenate([jnp.where(col_sb == bk * SB + s, ablk, 0.0) for s in range(SB)], axis=0)
            abc.append(jnp.dot(picked.astype(BF16), same_head, preferred_element_type=F32))
        for j in range(SB - 1):
            s = SB - 1 - j if rev else j
            for p in pairs:
                coef = abc[p][s * SB:(s + 1) * SB]
                cur_a[p] = cur_a[p] + coef * cur_a[p][s:s + 1]
                cur_u[p] = cur_u[p] + coef * cur_u[p][s:s + 1]
        for p in pairs:
            za[p][bk] = cur_a[p]
            zu[p][bk] = cur_u[p]
    a_hat = [jnp.concatenate(za[p], axis=0) for p in pairs]
    u_loc = [jnp.concatenate(zu[p], axis=0) for p in pairs]
    h0 = [h_ref[:, sl[p]] for p in pairs]
    q_hat = [rt[p] + pp(a_rb[p], a_hat[p]) for p in pairs]
    y_loc = [pp(a_rb[p], u_loc[p]) + pp(a_rk[p], v[p]) for p in pairs]
    for p in pairs:
        y_ref[0, :, sl[p]] = pp(q_hat[p], h0[p]) + y_loc[p]
    e_end = [jnp.exp(tot[p] - cs[p]) for p in pairs]
    bh = [b_in[p] * e_end[p] for p in pairs]
    phi = [jnp.where(eye, jnp.exp(tot[p]), 0.0) + ptp(bh[p], a_hat[p]) for p in pairs]
    gam = [ptp(jnp.concatenate([bh[p], kd_all[:, sl[p]] * e_end[p]], axis=0),
               jnp.concatenate([u_loc[p], v[p]], axis=0)) for p in pairs]
    for p in pairs:
        h_ref[:, sl[p]] = pp(phi[p], h0[p], split=True) + gam[p]

    @pl.when(c == n_chunks - 1)
    def _():
        hfin_ref[0] = h_ref[...]


def rwkv_direction(rev, main, w_pre, a_pre, w0, a0, k_k, k_a, s0):
    bsz, L, _ = main.shape
    W = RWKV_W
    T, N, H = RWKV_CHUNK, RWKV_HEAD, RWKV_HEADS
    n = L // T
    gw = RWKV_CPAIRS * LANES
    ng = W // gw
    h0 = s0.transpose(0, 3, 1, 2).reshape(bsz, N, W)

    def seq(col0):
        return pl.BlockSpec((1, T, gw), lambda b, g, c: (b, (n - 1 - c) if rev else c, col0 * ng + g))
    vec = pl.BlockSpec((1, gw), lambda b, g, c: (0, g))
    st = pl.BlockSpec((1, N, gw), lambda b, g, c: (b, 0, g))
    y, hfin = pl.pallas_call(
        functools.partial(_rwkv_fs_kernel, n, rev),
        grid=(bsz, ng, n),
        in_specs=[seq(0), seq(1), seq(2), seq(0), seq(0), vec, vec, vec, vec, st],
        out_specs=[seq(0), st],
        out_shape=[jax.ShapeDtypeStruct((bsz, L, W), F32), jax.ShapeDtypeStruct((bsz, N, W), F32)],
        scratch_shapes=[pltpu.VMEM((N, gw), F32)],
        compiler_params=pltpu.CompilerParams(
            dimension_semantics=("arbitrary",) * 3, vmem_limit_bytes=VMEM_LIMIT),
        name="rwkv_bwd_chunks" if rev else "rwkv_fwd_chunks",
    )(main, main, main, w_pre, a_pre, w0.reshape(1, W), a0.reshape(1, W), k_k.reshape(1, W), k_a.reshape(1, W), h0)
    return y, hfin.reshape(bsz, N, H, N).transpose(0, 2, 3, 1)


def _segsum(x, same_head):
    x1 = x.astype(BF16)
    r1 = x - x1.astype(F32)
    x2 = r1.astype(BF16)
    x3 = (r1 - x2.astype(F32)).astype(BF16)

    def d(a):
        return jnp.dot(a, same_head, preferred_element_type=F32)
    return d(x1) + (d(x2) + d(x3))


def _rwkv_post_kernel(yf_ref, yb_ref, r_ref, k_ref, v_ref, g_ref, af_ref, ab_ref, a0_ref, ka_ref, rk_ref,
                      lw_ref, lb_ref, o_ref):
    N = RWKV_HEAD
    same_head = ((lax.broadcasted_iota(jnp.int32, (LANES, LANES), 0) < N)
                 == (lax.broadcasted_iota(jnp.int32, (LANES, LANES), 1) < N)).astype(BF16)
    for t in range(o_ref.shape[2] // LANES):
        ls = slice(t * LANES, (t + 1) * LANES)
        wkv = yf_ref[0, :, ls] + yb_ref[0, :, ls]
        mean = _segsum(wkv, same_head) * (1.0 / N)
        cen = wkv - mean
        var = _segsum(cen * cen, same_head) * (1.0 / N)
        ln = cen * lax.rsqrt(var + RWKV_LNX_EPS) * lw_ref[:, ls] + lb_ref[:, ls]
        ka = ka_ref[:, ls]
        k_mix = ((1.0 + (jax.nn.sigmoid(af_ref[0, :, ls] + a0_ref[0:1, ls]) - 1.0) * ka)
                 + (1.0 + (jax.nn.sigmoid(ab_ref[0, :, ls] + a0_ref[1:2, ls]) - 1.0) * ka))
        bonus = _segsum(r_ref[0, :, ls] * k_ref[0, :, ls] * k_mix * rk_ref[:, ls], same_head) * v_ref[0, :, ls]
        gate = g_ref[0, :, ls]
        o_ref[0, :, ls] = ((ln + bonus) * (gate * jax.nn.sigmoid(gate))).astype(o_ref.dtype)


def rwkv_post(y_f, y_b, main, a_pre_f, a_pre_b, a0, k_a, r_k, lnx_w, lnx_b):
    bsz, L, W = y_f.shape
    tr = _pick(L, (256, 128, 64))
    tw = 512
    nw = W // tw

    def seq(col0):
        return pl.BlockSpec((1, tr, tw), lambda b, i, j: (b, i, col0 * nw + j))
    vec = pl.BlockSpec((1, tw), lambda b, i, j: (0, j))
    vec2 = pl.BlockSpec((N_DIR, tw), lambda b, i, j: (0, j))
    return pl.pallas_call(
        _rwkv_post_kernel,
        grid=(bsz, L // tr, nw),
        in_specs=[seq(0), seq(0), seq(0), seq(1), seq(2), seq(3), seq(0), seq(0), vec2, vec, vec, vec, vec],
        out_specs=seq(0),
        out_shape=jax.ShapeDtypeStruct((bsz, L, W), BF16),
        compiler_params=pltpu.CompilerParams(
            dimension_semantics=("arbitrary",) * 3, vmem_limit_bytes=VMEM_LIMIT),
        name="rwkv_post",
    )(y_f, y_b, main, main, main, main, a_pre_f, a_pre_b, a0, k_a.reshape(1, W), r_k.reshape(1, W),
      lnx_w.reshape(1, W), lnx_b.reshape(1, W))


def _split_cols(t, sizes):
    offsets, acc = [], 0
    for s in sizes[:-1]:
        acc += s
        offsets.append(acc)
    return jnp.split(t, offsets, axis=-1)


def _rms(x, w):
    return x * lax.rsqrt(jnp.mean(x * x, axis=-1, keepdims=True) + EPS) * w


def _adaln_kernel(c_ref, w_ref, b_ref, o_ref):
    cond = c_ref[...]
    act = cond * jax.nn.sigmoid(cond)
    o_ref[...] = jnp.dot(act, w_ref[...], precision=HI, preferred_element_type=F32) + b_ref[...]


def adaln(cond, w, b):
    rows, dm = cond.shape
    n = w.shape[1]
    rp = -(-rows // 8) * 8
    tn = 512
    m = pl.pallas_call(
        _adaln_kernel,
        grid=(n // tn,),
        in_specs=[pl.BlockSpec((rp, dm), lambda j: (0, 0)),
                  pl.BlockSpec((dm, tn), lambda j: (0, j)),
                  pl.BlockSpec((1, tn), lambda j: (0, j))],
        out_specs=pl.BlockSpec((rp, tn), lambda j: (0, j)),
        out_shape=jax.ShapeDtypeStruct((rp, n), F32),
        compiler_params=pltpu.CompilerParams(dimension_semantics=("arbitrary",), vmem_limit_bytes=VMEM_LIMIT),
        name="adaln",
    )(jnp.pad(cond, ((0, rp - rows), (0, 0))), w, b.reshape(1, n))[:rows]
    return jnp.split(m, 3, axis=-1)


def _grid_pos_embed(n_tokens):
    rows = n_tokens // GRID_W
    row_id = jnp.broadcast_to(jnp.arange(rows, dtype=F32)[:, None], (rows, GRID_W)).reshape(-1)
    col_id = jnp.broadcast_to(jnp.arange(GRID_W, dtype=F32)[None, :], (rows, GRID_W)).reshape(-1)
    quarter = D_MODEL // 4
    omega = 1.0 / (POS_BASE ** (jnp.arange(quarter, dtype=F32) / quarter))

    def axis_emb(pos):
        ang = pos[:, None] * omega[None, :]
        return jnp.concatenate([jnp.sin(ang), jnp.cos(ang)], axis=-1)
    return jnp.concatenate([axis_emb(row_id), axis_emb(col_id)], axis=-1)


def _even_mixer(x, gate, h, s5_re0, s5_im0, gla0, w_in, w_out, s5_ops, glu_w, glu_b, dec_up, dec_b, gla_nw):
    bsz, L, _ = h.shape
    n_main = sum(EVEN_SIZES[:-1])
    main = _mm3(h, w_in[:, :n_main])
    w_tail = jnp.pad(w_in[:, n_main:], ((0, 0), (0, LANES - N_DIR * GLA_RANK)))
    dec_lr = _mm3(h, w_tail)
    y, fin_re, fin_im = s5_scan(main[..., :S5_W], s5_ops, s5_re0, s5_im0)
    gy = jax.nn.gelu(y).astype(BF16).reshape(bsz * L, S5_W)
    mixed = matmul_glu(gy, glu_w, glu_b, main.reshape(bsz * L, n_main), S5_W, S5_W + GLA_DV_W)
    mixed, fin_gla = gla_mix(main, dec_lr, dec_up, dec_b, gla_nw, gla0, mixed.reshape(bsz, L, -1))
    return matmul_gated_residual(mixed, w_out, x, gate), fin_re, fin_im, fin_gla


def _odd_mixer(x, gate, h, rwkv0, w_in, w_out, mu, w0, w2, a0, a2, k_k, k_a, r_k, lnx_w, lnx_b):
    bsz, L, _ = h.shape
    zero = jnp.zeros_like(h[:, :1])
    h_prev = jnp.concatenate([zero, h[:, :-1]], axis=1)
    h_next = jnp.concatenate([h[:, 1:], zero], axis=1)
    xs = h + mu[0] * (h_prev - h) + mu[1] * (h_next - h)
    n_main = sum(ODD_SIZES[:4])
    main = _mm3(xs, w_in[:, :n_main])
    tail = _mm3(xs, w_in[:, n_main:])
    w_lr, a_lr = _split_cols(tail, ODD_SIZES[4:])
    w_lr = jnp.tanh(w_lr).reshape(bsz, L, N_DIR, RWKV_DECAY_RANK)
    a_lr = a_lr.reshape(bsz, L, N_DIR, RWKV_ICLR_RANK)
    ys, a_pres, finals = [], [], []
    for d in range(N_DIR):
        w_pre = _mm3(w_lr[:, :, d], w2[d])
        a_pre = _mm3(a_lr[:, :, d], a2[d])
        y_d, fin = rwkv_direction(bool(d), main, w_pre, a_pre, w0[d], a0[d], k_k, k_a, rwkv0[:, d])
        ys.append(y_d)
        a_pres.append(a_pre)
        finals.append(fin)
    out = rwkv_post(ys[0], ys[1], main, a_pres[0], a_pres[1], a0, k_a, r_k.reshape(-1), lnx_w, lnx_b)
    return matmul_gated_residual(out, w_out, x, gate), jnp.stack(finals, axis=1)


def kernel(x_prompt, x_sample, state_s5_re, state_s5_im, state_gla, state_rwkv, c, c_ctx, norm_w, ada_w, ada_b, final_norm_w, e_w_in, e_w_out, s5_lambda_re, s5_lambda_im, s5_log_step, s5_b_re, s5_b_im, s5_c_re, s5_c_im, s5_d, s5_glu_w, s5_glu_b, gla_decay_up, gla_decay_b, gla_norm_w, o_w_in, o_w_out, rwkv_mu, rwkv_w0, rwkv_w2, rwkv_a0, rwkv_a2, rwkv_k_k, rwkv_k_a, rwkv_r_k, rwkv_lnx_w, rwkv_lnx_b):
    bp = x_prompt.shape[0]
    depth = norm_w.shape[0]
    x_ctx = x_prompt
    x_lat = x_sample + _grid_pos_embed(x_sample.shape[1])[None]
    z_s5 = jnp.zeros((bp, N_DIR, S5_GROUPS, S5_STATE), F32)
    z_gla = jnp.zeros((bp, N_DIR, GLA_HEADS, GLA_DK, GLA_DV), F32)
    z_rwkv = jnp.zeros((bp, N_DIR, RWKV_HEADS, RWKV_HEAD, RWKV_HEAD), F32)
    new_s5_re, new_s5_im, new_gla, new_rwkv = [], [], [], []
    n_lat = c.shape[0]
    cond = jnp.concatenate([c, c_ctx[None]], axis=0)
    for i in range(depth):
        j = i // 2
        shift, scale, gate = adaln(cond, ada_w[i], ada_b[i])
        gt_l, gt_c = gate[:n_lat], jnp.broadcast_to(gate[n_lat:], (bp, D_MODEL))
        h_ctx = norm_mod(x_ctx, norm_w[i], scale[n_lat:], shift[n_lat:])
        h_lat = norm_mod(x_lat, norm_w[i], scale[:n_lat], shift[:n_lat])
        if i % 2 == 0:
            s5_ops = s5_operators(s5_lambda_re[j], s5_lambda_im[j], s5_log_step[j], s5_b_re[j], s5_b_im[j],
                                  s5_c_re[j], s5_c_im[j], s5_d[j])
            p = (e_w_in[j], e_w_out[j], s5_ops, s5_glu_w[j], s5_glu_b[j], gla_decay_up[j], gla_decay_b[j],
                 gla_norm_w[j])
            x_ctx, fr, fi, fg = _even_mixer(x_ctx, gt_c, h_ctx, z_s5, z_s5, z_gla, *p)
            x_lat, _, _, _ = _even_mixer(x_lat, gt_l, h_lat, state_s5_re[:, j], state_s5_im[:, j],
                                         state_gla[:, j], *p)
            new_s5_re.append(fr)
            new_s5_im.append(fi)
            new_gla.append(fg)
        else:
            p = (o_w_in[j], o_w_out[j], rwkv_mu[j], rwkv_w0[j], rwkv_w2[j], rwkv_a0[j], rwkv_a2[j],
                 rwkv_k_k[j], rwkv_k_a[j], rwkv_r_k[j], rwkv_lnx_w[j], rwkv_lnx_b[j])
            x_ctx, fw = _odd_mixer(x_ctx, gt_c, h_ctx, z_rwkv, *p)
            x_lat, _ = _odd_mixer(x_lat, gt_l, h_lat, state_rwkv[:, j], *p)
            new_rwkv.append(fw)
    y_prompt = final_norm(x_ctx, final_norm_w)
    y_sample = final_norm(x_lat, final_norm_w)
    return (y_prompt, y_sample, jnp.stack(new_s5_re, axis=1), jnp.stack(new_s5_im, axis=1),
            jnp.stack(new_gla, axis=1), jnp.stack(new_rwkv, axis=1))
```

```python
import functools
import math

import jax
import jax.numpy as jnp
from jax import lax
from jax.experimental import pallas as pl
from jax.experimental.pallas import tpu as pltpu

D_MODEL = 2048
GRID_W = 64
POS_BASE = 10000.0
N_DIR = 2
EPS = 1e-6
S5_W = 1024
S5_GROUP_CH = 16
S5_GROUPS = 64
S5_STATE = 64
S5_CHUNK = 16
GLA_HEADS = 6
GLA_DV = 512
GLA_DK = 256
GLA_DK_W = 1536
GLA_DV_W = 3072
GLA_RANK = 16
GLA_NORMALIZER = 16.0
GLA_CHUNK = 64
GLA_NC = 8
GLA_LOG_DECAY_MIN = -1.0
EVEN_SIZES = (S5_W, S5_W, GLA_DK_W, GLA_DK_W, GLA_DV_W, GLA_DV_W, N_DIR * GLA_RANK)
RWKV_W = 2048
RWKV_HEAD = 64
RWKV_HEADS = 32
RWKV_DECAY_RANK = 96
RWKV_ICLR_RANK = 96
RWKV_LNX_EPS = 64e-5
ODD_SIZES = (RWKV_W, RWKV_W, RWKV_W, RWKV_W, N_DIR * RWKV_DECAY_RANK, N_DIR * RWKV_ICLR_RANK)
RWKV_TBLK = 64
RWKV_PAIRS = 8
RWKV_GROUP_T = 8
RWKV_CHUNK = 64
RWKV_CPAIRS = 8
RWKV_SUB = 16
LANES = 128

VMEM_LIMIT = 48 * 1024 * 1024
HI = lax.Precision.HIGHEST
BF16 = jnp.bfloat16
F32 = jnp.float32


def _mm_kernel(x_ref, w_ref, o_ref):
    o_ref[...] = jnp.dot(x_ref[...], w_ref[...], preferred_element_type=F32)


def _pick(n, prefs):
    for p in prefs:
        if n % p == 0:
            return p
    return n


def matmul(x, w):
    m, k = x.shape
    n = w.shape[1]
    x = x.astype(BF16)
    w = w.astype(BF16)
    tm = _pick(m, (1024, 512, 256, 128, 64, 32, 16, 8))
    tn = _pick(n, (512, 384, 256, 128))
    return pl.pallas_call(
        _mm_kernel,
        grid=(m // tm, n // tn),
        in_specs=[pl.BlockSpec((tm, k), lambda i, j: (i, 0)),
                  pl.BlockSpec((k, tn), lambda i, j: (0, j))],
        out_specs=pl.BlockSpec((tm, tn), lambda i, j: (i, j)),
        out_shape=jax.ShapeDtypeStruct((m, n), F32),
        compiler_params=pltpu.CompilerParams(
            dimension_semantics=("arbitrary", "arbitrary"), vmem_limit_bytes=VMEM_LIMIT),
        name="proj_matmul",
    )(x, w)


def _mm3(h, w):
    b, l, k = h.shape
    return matmul(h.reshape(b * l, k), w).reshape(b, l, -1)


def _mm_residual_kernel(x_ref, w_ref, res_ref, gate_ref, o_ref):
    acc = jnp.dot(x_ref[...], w_ref[...], preferred_element_type=F32)
    o_ref[...] = res_ref[...] + gate_ref[0] * acc


def matmul_gated_residual(x, w, res, gate):
    bsz, L, k = x.shape
    n = w.shape[1]
    m = bsz * L
    tm = _pick(L, (1024, 512, 256, 128))
    tn = _pick(n, (512, 256, 128))
    per_b = L // tm
    out = pl.pallas_call(
        _mm_residual_kernel,
        grid=(m // tm, n // tn),
        in_specs=[pl.BlockSpec((tm, k), lambda i, j: (i, 0)),
                  pl.BlockSpec((k, tn), lambda i, j: (0, j)),
                  pl.BlockSpec((tm, tn), lambda i, j: (i, j)),
                  pl.BlockSpec((1, 1, tn), lambda i, j: (i // per_b, 0, j))],
        out_specs=pl.BlockSpec((tm, tn), lambda i, j: (i, j)),
        out_shape=jax.ShapeDtypeStruct((m, n), F32),
        compiler_params=pltpu.CompilerParams(
            dimension_semantics=("arbitrary", "arbitrary"), vmem_limit_bytes=VMEM_LIMIT),
        name="proj_residual",
    )(x.reshape(m, k).astype(BF16), w.astype(BF16), res.reshape(m, n), gate.reshape(bsz, 1, n))
    return out.reshape(bsz, L, n)


def _mm_glu_kernel(x_ref, w_ref, b_ref, g_ref, xt_ref, o_ref):
    acc = jnp.dot(x_ref[...], w_ref[...], preferred_element_type=F32) + b_ref[...]
    gy = xt_ref[...].astype(F32)
    gate = g_ref[...]
    o_ref[...] = (gy * jax.nn.sigmoid(acc) * (gate * jax.nn.sigmoid(gate))).astype(o_ref.dtype)


def matmul_glu(gy, w, b, main, g_col0, n_total):
    m, k = gy.shape
    n = w.shape[1]
    tm = _pick(m, (1024, 512, 256, 128))
    tn = 512
    return pl.pallas_call(
        _mm_glu_kernel,
        grid=(m // tm, n // tn),
        in_specs=[pl.BlockSpec((tm, k), lambda i, j: (i, 0)),
                  pl.BlockSpec((k, tn), lambda i, j: (0, j)),
                  pl.BlockSpec((1, tn), lambda i, j: (0, j)),
                  pl.BlockSpec((tm, tn), lambda i, j: (i, g_col0 // tn + j)),
                  pl.BlockSpec((tm, tn), lambda i, j: (i, j))],
        out_specs=pl.BlockSpec((tm, tn), lambda i, j: (i, j)),
        out_shape=jax.ShapeDtypeStruct((m, n_total), BF16),
        compiler_params=pltpu.CompilerParams(
            dimension_semantics=("arbitrary", "arbitrary"), vmem_limit_bytes=VMEM_LIMIT),
        name="s5_glu_gate",
    )(gy, w.astype(BF16), b.reshape(1, n), main, gy)


def _norm_mod_kernel(x_ref, nw_ref, sc_ref, sh_ref, o_ref):
    x = x_ref[0]
    inv = lax.rsqrt(jnp.mean(x * x, axis=-1, keepdims=True) + EPS)
    o_ref[0] = (x * inv * nw_ref[...] * (1.0 + sc_ref[0]) + sh_ref[0]).astype(o_ref.dtype)


def norm_mod(x, nw, scale, shift):
    bsz, L, dm = x.shape
    tr = _pick(L, (256, 128, 64))
    nb = scale.shape[0]
    cond = pl.BlockSpec((1, 1, dm), lambda b, i: (b if nb > 1 else 0, 0, 0))
    return pl.pallas_call(
        _norm_mod_kernel,
        grid=(bsz, L // tr),
        in_specs=[pl.BlockSpec((1, tr, dm), lambda b, i: (b, i, 0)),
                  pl.BlockSpec((1, dm), lambda b, i: (0, 0)), cond, cond],
        out_specs=pl.BlockSpec((1, tr, dm), lambda b, i: (b, i, 0)),
        out_shape=jax.ShapeDtypeStruct((bsz, L, dm), BF16),
        compiler_params=pltpu.CompilerParams(
            dimension_semantics=("arbitrary", "arbitrary"), vmem_limit_bytes=VMEM_LIMIT),
        name="norm_mod",
    )(x, nw.reshape(1, dm), scale.reshape(nb, 1, dm), shift.reshape(nb, 1, dm))


def _final_norm_kernel(x_ref, nw_ref, o_ref):
    x = x_ref[0]
    o_ref[0] = x * lax.rsqrt(jnp.mean(x * x, axis=-1, keepdims=True) + EPS) * nw_ref[...]


def final_norm(x, nw):
    bsz, L, dm = x.shape
    tr = _pick(L, (256, 128, 64))
    return pl.pallas_call(
        _final_norm_kernel,
        grid=(bsz, L // tr),
        in_specs=[pl.BlockSpec((1, tr, dm), lambda b, i: (b, i, 0)), pl.BlockSpec((1, dm), lambda b, i: (0, 0))],
        out_specs=pl.BlockSpec((1, tr, dm), lambda b, i: (b, i, 0)),
        out_shape=jax.ShapeDtypeStruct((bsz, L, dm), F32),
        compiler_params=pltpu.CompilerParams(
            dimension_semantics=("arbitrary", "arbitrary"), vmem_limit_bytes=VMEM_LIMIT),
        name="final_norm",
    )(x, nw.reshape(1, dm))


def s5_operators(lam_re, lam_im, log_step, b_re, b_im, c_re, c_im, d_skip):
    T = S5_CHUNK
    dt = jnp.exp(log_step)[..., None]
    mag = jnp.exp(lam_re * dt)
    ab_re, ab_im = mag * jnp.cos(lam_im * dt), mag * jnp.sin(lam_im * dt)
    den = lam_re * lam_re + lam_im * lam_im
    f_re = ((ab_re - 1.0) * lam_re + ab_im * lam_im) / den
    f_im = (ab_im * lam_re - (ab_re - 1.0) * lam_im) / den
    bb_re = f_re[..., None] * b_re - f_im[..., None] * b_im
    bb_im = f_re[..., None] * b_im + f_im[..., None] * b_re
    kk = jnp.arange(T + 1, dtype=F32)[:, None, None, None]
    pmag = jnp.exp(kk * (lam_re * dt))
    pr = pmag * jnp.cos(kk * (lam_im * dt))
    pi = pmag * jnp.sin(kk * (lam_im * dt))
    zr = pr[:T, :, :, :, None] * bb_re - pi[:T, :, :, :, None] * bb_im
    zi = pr[:T, :, :, :, None] * bb_im + pi[:T, :, :, :, None] * bb_re
    kern = (jnp.einsum('dghp,kdgpj->kdghj', c_re, zr, precision=HI)
            - jnp.einsum('dghp,kdgpj->kdghj', c_im, zi, precision=HI))
    t_idx = jnp.arange(T)[:, None]
    s_idx = jnp.arange(T)[None, :]
    lag_f = t_idx - s_idx
    lag_b = s_idx - t_idx
    m_f = jnp.where((lag_f >= 0)[:, :, None, None, None], kern[:, 0][jnp.clip(lag_f, 0, T - 1)], 0.0)
    m_b = jnp.where((lag_b >= 0)[:, :, None, None, None], kern[:, 1][jnp.clip(lag_b, 0, T - 1)], 0.0)
    m = m_f + m_b
    eye_t = jnp.eye(T, dtype=F32)[:, :, None, None, None]
    eye_h = jnp.eye(S5_GROUP_CH, dtype=F32)[None, None, None]
    m = m + eye_t * eye_h * d_skip.reshape(S5_GROUPS, S5_GROUP_CH)[None, None, :, :, None]
    g = m.shape[2]
    m_t = m.transpose(2, 1, 4, 0, 3).reshape(g, T * S5_GROUP_CH, T * S5_GROUP_CH)
    pf_r, pf_i = pr[T - 1::-1][:T, 0], pi[T - 1::-1][:T, 0]
    pb_r, pb_i = pr[:T, 1], pi[:T, 1]

    def f_mat(p_r, p_i, d):
        re = p_r[..., None] * bb_re[d][None] - p_i[..., None] * bb_im[d][None]
        im = p_r[..., None] * bb_im[d][None] + p_i[..., None] * bb_re[d][None]
        re = re.transpose(1, 0, 3, 2).reshape(g, T * S5_GROUP_CH, S5_STATE)
        im = im.transpose(1, 0, 3, 2).reshape(g, T * S5_GROUP_CH, S5_STATE)
        return re, im
    ff_re, ff_im = f_mat(pf_r, pf_i, 0)
    fb_re, fb_im = f_mat(pb_r, pb_i, 1)
    a_t = jnp.concatenate([m_t, ff_re, fb_re, ff_im, fb_im], axis=-1)
    ef_r, ef_i = pr[1:T + 1, 0], pi[1:T + 1, 0]
    eb_r, eb_i = pr[T:0:-1, 1], pi[T:0:-1, 1]

    def e_mat(p_r, p_i, d):
        er = c_re[d][None] * p_r[:, :, None, :] - c_im[d][None] * p_i[:, :, None, :]
        ei = -(c_re[d][None] * p_i[:, :, None, :] + c_im[d][None] * p_r[:, :, None, :])
        er = er.transpose(1, 3, 0, 2).reshape(g, S5_STATE, T * S5_GROUP_CH)
        ei = ei.transpose(1, 3, 0, 2).reshape(g, S5_STATE, T * S5_GROUP_CH)
        return er, ei
    efr, efi = e_mat(ef_r, ef_i, 0)
    ebr, ebi = e_mat(eb_r, eb_i, 1)
    e_t = jnp.concatenate([efr, ebr, efi, ebi], axis=1)
    lam_t = jnp.concatenate([pr[T, 0], pr[T, 1], pi[T, 0], pi[T, 1]], axis=-1)[:, None, :]
    return a_t.astype(BF16), e_t.astype(BF16), lam_t


def _s5_kernel(n_chunks, bsz, ut_ref, at_ref, et_ref, lam_ref, h0_ref, y_ref, hfin_ref, z_ref, hent_ref):
    P = S5_STATE
    z_ref[...] = jnp.dot(ut_ref[0], at_ref[0], preferred_element_type=F32)
    lam = lam_ref[0]
    a_re, a_im = lam[:, 0:2 * P], lam[:, 2 * P:4 * P]
    h0 = h0_ref[0]
    fwd_lanes = lax.broadcasted_iota(jnp.int32, (bsz, 2 * P), 1) < P

    def step(c, carry):
        h_re, h_im = carry
        rf = pl.ds(pl.multiple_of(c * bsz, 8), bsz)
        rb = pl.ds(pl.multiple_of((n_chunks - 1 - c) * bsz, 8), bsz)
        hent_ref[rf, 0:P] = h_re[:, 0:P]
        hent_ref[rb, P:2 * P] = h_re[:, P:2 * P]
        hent_ref[rf, 2 * P:3 * P] = h_im[:, 0:P]
        hent_ref[rb, 3 * P:4 * P] = h_im[:, P:2 * P]
        g_re = jnp.where(fwd_lanes, z_ref[rf, 4 * P:6 * P], z_ref[rb, 4 * P:6 * P])
        g_im = jnp.where(fwd_lanes, z_ref[rf, 6 * P:8 * P], z_ref[rb, 6 * P:8 * P])
        return a_re * h_re - a_im * h_im + g_re, a_re * h_im + a_im * h_re + g_im
    h_re, h_im = lax.fori_loop(0, n_chunks, step, (h0[:, 0:2 * P], h0[:, 2 * P:4 * P]))
    hfin_ref[0, :, 0:2 * P] = h_re
    hfin_ref[0, :, 2 * P:4 * P] = h_im
    y_ref[0] =z_ref[:, 0:4 * P] + jnp.dot(hent_ref[...].astype(BF16), et_ref[0], preferred_element_type=F32)


def s5_scan(u, ops, h0_re, h0_im):
    a_t, e_t, lam_t = ops
    b_real, L, _ = u.shape
    T, G, H, P = S5_CHUNK, S5_GROUPS, S5_GROUP_CH, S5_STATE
    n = L // T
    bsz = -(-b_real // 8) * 8
    cols = n * bsz
    ut = u.reshape(b_real, n, T, G, H).transpose(3, 1, 0, 2, 4).astype(BF16)
    ut = jnp.pad(ut, ((0, 0), (0, 0), (0, bsz - b_real), (0, 0), (0, 0))).reshape(G, cols, T * H)
    h0 = jnp.concatenate([h0_re[:, 0], h0_re[:, 1], h0_im[:, 0], h0_im[:, 1]], axis=-1)
    h0 = jnp.pad(h0.transpose(1, 0, 2), ((0, 0), (0, bsz - b_real), (0, 0)))
    yt, hfin = pl.pallas_call(
        functools.partial(_s5_kernel, n, bsz),
        grid=(G,),
        in_specs=[pl.BlockSpec((1, cols, T * H), lambda g: (g, 0, 0)),
                  pl.BlockSpec((1, T * H, 8 * P), lambda g: (g, 0, 0)),
                  pl.BlockSpec((1, 4 * P, T * H), lambda g: (g, 0, 0)),
                  pl.BlockSpec((1, 1, 4 * P), lambda g: (g, 0, 0)),
                  pl.BlockSpec((1, bsz, 4 * P), lambda g: (g, 0, 0))],
        out_specs=[pl.BlockSpec((1, cols, T * H), lambda g: (g, 0, 0)),
                   pl.BlockSpec((1, bsz, 4 * P), lambda g: (g, 0, 0))],
        out_shape=[jax.ShapeDtypeStruct((G, cols, T * H), F32),
                   jax.ShapeDtypeStruct((G, bsz, 4 * P), F32)],
        scratch_shapes=[pltpu.VMEM((cols, 8 * P), F32), pltpu.VMEM((cols, 4 * P), F32)],
        compiler_params=pltpu.CompilerParams(dimension_semantics=("arbitrary",), vmem_limit_bytes=VMEM_LIMIT),
        name="s5_chunk_scan",
    )(ut, a_t, e_t, lam_t, h0)
    y = yt.reshape(G, n, bsz, T, H)[:, :, :b_real].transpose(2, 1, 3, 0, 4).reshape(b_real, L, G * H)
    hfin = hfin[:, :b_real].transpose(1, 0, 2)
    fin_re = jnp.stack([hfin[..., 0:P], hfin[..., P:2 * P]], axis=1)
    fin_im = jnp.stack([hfin[..., 2 * P:3 * P], hfin[..., 3 * P:4 * P]], axis=1)
    return y, fin_re, fin_im


def _dot_t(a, b):
    return lax.dot_general(a, b, (((1,), (1,)), ((), ())), preferred_element_type=F32)


def _gla_kernel(n_chunks, q_ref, k_ref, v_ref, g_ref, lr_ref, up_ref, db_ref, nw_ref, s0_ref,
                out_ref, sfin_ref, s_ref, of_ref):
    C = GLA_CHUNK
    d = pl.program_id(2)
    c = pl.program_id(3)
    cidx = jnp.where(d == 0, c, n_chunks - 1 - c)

    @pl.when(c == 0)
    def _():
        s_ref[...] = s0_ref[0, 0, 0]

    z = jnp.dot(lr_ref[0], up_ref[0], precision=HI, preferred_element_type=F32) + db_ref[0]
    gc = jnp.maximum(jax.nn.log_sigmoid(z) * (1.0 / GLA_NORMALIZER), GLA_LOG_DECAY_MIN)
    row = lax.broadcasted_iota(jnp.int32, (C, C), 0)
    col = lax.broadcasted_iota(jnp.int32, (C, C), 1)
    seen = jnp.where(d == 0, row - col, col - row) >= 0
    bcum = jnp.dot(seen.astype(F32), gc, precision=HI, preferred_element_type=F32)
    b_last = jnp.sum(gc, axis=0, keepdims=True)
    q = q_ref[0] * (GLA_DK ** -0.5)
    k = k_ref[0]
    v = v_ref[0].astype(BF16)
    q_dec = (q * jnp.exp(bcum)).astype(BF16)
    k_inv = (k * jnp.exp(-bcum)).astype(BF16)
    k_end = (k * jnp.exp(b_last - bcum)).astype(BF16)
    att = jnp.where(seen, _dot_t(q_dec, k_inv), 0.0).astype(BF16)
    s_old = s_ref[...]
    o = (jnp.dot(att, v, preferred_element_type=F32)
         + jnp.dot(q_dec, s_old.astype(BF16), preferred_element_type=F32))
    eye = (lax.broadcasted_iota(jnp.int32, (GLA_DK, GLA_DK), 0)
           == lax.broadcasted_iota(jnp.int32, (GLA_DK, GLA_DK), 1)).astype(F32)
    bl_col = lax.dot_general(eye, jnp.broadcast_to(b_last, (8, GLA_DK)), (((1,), (1,)), ((), ())),
                             precision=HI, preferred_element_type=F32)[:, 0:1]
    kv = lax.dot_general(k_end, v, (((0,), (0,)), ((), ())), preferred_element_type=F32)
    s_ref[...] = jnp.exp(bl_col) * s_old + kv

    rows = pl.ds(pl.multiple_of(cidx * C, C), C)

    @pl.when(d == 0)
    def _():
        of_ref[rows, :] = o

    @pl.when(d == 1)
    def _():
        tot = of_ref[rows, :] + o
        nrm = tot * lax.rsqrt(jnp.mean(tot * tot, axis=-1, keepdims=True) + EPS) * nw_ref[0]
        gate = g_ref[0]
        out_ref[0] = (nrm * (gate * jax.nn.sigmoid(gate))).astype(out_ref.dtype)

    @pl.when(c == n_chunks - 1)
    def _():
        sfin_ref[0, 0, 0] = s_ref[...]


def _dot_mask(mask_bf16, x, x_rows_to_sublanes=False):
    def d(b):
        if x_rows_to_sublanes:
            return lax.dot_general(b, mask_bf16, (((0,), (0,)), ((), ())), preferred_element_type=F32)
        return jnp.dot(mask_bf16, b, preferred_element_type=F32)
    x1 = x.astype(BF16)
    r1 = x - x1.astype(F32)
    x2 = r1.astype(BF16)
    x3 = (r1 - x2.astype(F32)).astype(BF16)
    return d(x1) + (d(x2) + d(x3))


def _gla_block_kernel(n_blocks, NC, q_ref, k_ref, v_ref, g_ref, lr_ref, up_ref, db_ref, nw_ref, s0_ref, dst_ref,
                      out_ref, sfin_ref, s_ref, of_ref, qd_ref, ov_ref, kv_ref, dc_ref):
    C = GLA_CHUNK
    R = C * NC
    d = pl.program_id(2)
    c = pl.program_id(3)
    bidx = jnp.where(d == 0, c, n_blocks - 1 - c)

    @pl.when(c == 0)
    def _():
        s_ref[...] = s0_ref[0, 0, 0]

    z = _mxu(lr_ref[0], up_ref[0], split=True) + db_ref[0]
    gc = jnp.maximum(jax.nn.log_sigmoid(z) * (1.0 / GLA_NORMALIZER), GLA_LOG_DECAY_MIN)
    row_c = lax.broadcasted_iota(jnp.int32, (C, C), 0)
    col_c = lax.broadcasted_iota(jnp.int32, (C, C), 1)
    seen_c = jnp.where(d == 0, row_c - col_c, col_c - row_c) >= 0
    seen_bf = seen_c.astype(BF16)
    rs = [slice(i * C, (i + 1) * C) for i in range(NC)]
    bcum_c = [_dot_mask(seen_bf, gc[r]) for r in rs]
    btot_c = [jnp.broadcast_to(jnp.where(d == 0, b[C - 1:C], b[0:1]), (C, GLA_DK)) for b in bcum_c]
    bcum = jnp.concatenate(bcum_c, axis=0)
    btot = jnp.concatenate(btot_c, axis=0)
    q_dec = (q_ref[0] * (GLA_DK ** -0.5) * jnp.exp(bcum)).astype(BF16)
    k = k_ref[0]
    k_inv = (k * jnp.exp(-bcum)).astype(BF16)
    k_end = (k * jnp.exp(btot - bcum)).astype(BF16)
    v = v_ref[0].astype(BF16)
    ones_c = jnp.ones((C, LANES), BF16)
    qd_ref[...] = q_dec.reshape(NC, C, GLA_DK)
    att = [jnp.where(seen_c, _dot_t(q_dec[r], k_inv[r]), 0.0).astype(BF16) for r in rs]
    for i in range(NC):
        kv_ref[i] = lax.dot_general(k_end[rs[i]], v[rs[i]], (((0,), (0,)), ((), ())), preferred_element_type=F32)
    for i in range(NC):
        ov_ref[i] = jnp.dot(att[i], v[rs[i]], preferred_element_type=F32)
    for i in range(NC):
        dc_ref[i] = _dot_mask(ones_c, gc[rs[i]], x_rows_to_sublanes=True)

    for i in range(NC):
        ci = jnp.where(d == 0, i, NC - 1 - i)
        s_old = s_ref[...]
        o = ov_ref[ci] + jnp.dot(qd_ref[ci], s_old.astype(BF16), preferred_element_type=F32)
        s_ref[...] = jnp.exp(dc_ref[ci][:, 0:1]) * s_old + kv_ref[ci]
        rows = pl.ds(pl.multiple_of(bidx * R + ci * C, C), C)
        orow = pl.ds(pl.multiple_of(ci * C, C), C)

        @pl.when(d == 0)
        def _():
            of_ref[rows, :] = o

        @pl.when(d == 1)
        def _():
            tot = of_ref[rows, :] + o
            nrm = tot * lax.rsqrt(jnp.mean(tot * tot, axis=-1, keepdims=True) + EPS) * nw_ref[0]
            gate = g_ref[0, orow, :]
            out_ref[0, orow, :] = (nrm * (gate * jax.nn.sigmoid(gate))).astype(out_ref.dtype)

    @pl.when(c == n_blocks - 1)
    def _():
        sfin_ref[0, 0, 0] = s_ref[...]


def gla_mix(main, dec_lr, dec_up, dec_b, gla_nw, s0, dst):
    bsz, L, _ = main.shape
    H, DK, DV = GLA_HEADS, GLA_DK, GLA_DV
    nc = min(GLA_NC, L // GLA_CHUNK)
    C = GLA_CHUNK * nc
    n = L // C
    q_blk = sum(EVEN_SIZES[:2]) // DK
    k_blk = sum(EVEN_SIZES[:3]) // DK
    v_blk = sum(EVEN_SIZES[:4]) // DV
    g_blk = sum(EVEN_SIZES[:5]) // DV
    up = jnp.zeros((N_DIR, LANES, GLA_DK_W), F32)
    for d in range(N_DIR):
        up = up.at[d, d * GLA_RANK:(d + 1) * GLA_RANK].set(dec_up[d])
    db = dec_b.reshape(N_DIR, 1, GLA_DK_W)
    nw = gla_nw.reshape(1, GLA_DV_W)

    def chunk(d, c):
        return c + d * (n - 1 - 2 * c)

    def out_chunk(d, c):
        return (n - 1) - d * c
    out, sfin = pl.pallas_call(
        functools.partial(_gla_block_kernel, n, nc),
        grid=(bsz, H, N_DIR, n),
        in_specs=[pl.BlockSpec((1, C, DK), lambda b, h, d, c: (b, chunk(d, c), q_blk + h)),
                  pl.BlockSpec((1, C, DK), lambda b, h, d, c: (b, chunk(d, c), k_blk + h)),
                  pl.BlockSpec((1, C, DV), lambda b, h, d, c: (b, chunk(d, c), v_blk + h)),
                  pl.BlockSpec((1, C, DV), lambda b, h, d, c: (b, chunk(d, c), g_blk + h)),
                  pl.BlockSpec((1, C, LANES), lambda b, h, d, c: (b, chunk(d, c), 0)),
                  pl.BlockSpec((1, LANES, DK), lambda b, h, d, c: (d, 0, h)),
                  pl.BlockSpec((1, 1, DK), lambda b, h, d, c: (d, 0, h)),
                  pl.BlockSpec((1, DV), lambda b, h, d, c: (0, h)),
                  pl.BlockSpec((1, 1, 1, DK, DV), lambda b, h, d, c: (b, d, h, 0, 0)),
                  pl.BlockSpec(memory_space=pl.ANY)],
        input_output_aliases={9: 0},
        out_specs=[pl.BlockSpec((1, C, DV), lambda b, h, d, c: (b, out_chunk(d, c), S5_W // DV + h)),
                   pl.BlockSpec((1, 1, 1, DK, DV), lambda b, h, d, c: (b, d, h, 0, 0))],
        out_shape=[jax.ShapeDtypeStruct(dst.shape, BF16),
                   jax.ShapeDtypeStruct((bsz, N_DIR, H, DK, DV), F32)],
        scratch_shapes=[pltpu.VMEM((DK, DV), F32), pltpu.VMEM((L, DV), F32),
                        pltpu.VMEM((nc, GLA_CHUNK, DK), BF16), pltpu.VMEM((nc, GLA_CHUNK, DV), F32),
                        pltpu.VMEM((nc, DK, DV), F32), pltpu.VMEM((nc, DK, LANES), F32)],
        compiler_params=pltpu.CompilerParams(
            dimension_semantics=("arbitrary",) * 4, vmem_limit_bytes=VMEM_LIMIT),
        name="gla_chunk_scan",
    )(main, main, main, main, dec_lr, up, db, nw, s0, dst)
    return out, sfin


def _rwkv_kernel(n_blk, r_ref, v_ref, a_ref, w_ref, k_ref, b_ref, s0_ref, y_ref, sfin_ref,
                 s_ref, vc_ref, sr_ref):
    TB, N, GT = RWKV_TBLK, RWKV_HEAD, RWKV_GROUP_T
    d = pl.program_id(2)
    c = pl.program_id(3)

    @pl.when(c == 0)
    def _():
        s_ref[...] = s0_ref[0, 0]

    lane = lax.broadcasted_iota(jnp.int32, (N, LANES), 1)
    row = lax.broadcasted_iota(jnp.int32, (N, LANES), 0)
    lo_half = lane < N
    diag = (lane % N) == row
    same_head = ((lax.broadcasted_iota(jnp.int32, (LANES, LANES), 0) < N)
                 == (lax.broadcasted_iota(jnp.int32, (LANES, LANES), 1) < N)).astype(BF16)
    fwd = d == 0
    y_ref[...] = jnp.zeros_like(y_ref)

    def row_of(tile, j):
        return jnp.where(fwd, tile[j:j + 1], tile[GT - 1 - j:GT - j])

    def group(i, carry):
        g8 = jnp.where(fwd, i, TB // GT - 1 - i)
        rs = pl.ds(pl.multiple_of(g8 * GT, GT), GT)
        v_tile = v_ref[0, rs, :]
        for p in range(RWKV_PAIRS):
            ls = slice(p * LANES, (p + 1) * LANES)
            vd = jnp.concatenate([jnp.where(diag, row_of(v_tile, j)[:, ls], 0.0) for j in range(GT)], axis=0)
            vd_hi = vd.astype(BF16)
            vd_lo = (vd - vd_hi.astype(F32)).astype(BF16)
            vcol = (jnp.dot(vd_hi, same_head, preferred_element_type=F32)
                    + jnp.dot(vd_lo, same_head, preferred_element_type=F32))
            vc_ref[:, ls] = vcol
        tiles = (r_ref[0, rs, :], a_ref[0, rs, :], w_ref[0, 0, rs, :], k_ref[0, 0, rs, :], b_ref[0, 0, rs, :])
        for j in range(GT):
            r_row, a_row, w_row, k_row, b_row = (row_of(x, j) for x in tiles)
            for p in range(RWKV_PAIRS):
                ls = slice(p * LANES, (p + 1) * LANES)
                s = s_ref[:, ls]
                prod = s * a_row[:, ls]
                sa_lo = jnp.sum(jnp.where(lo_half, prod, 0.0), axis=1, keepdims=True)
                sa_hi = jnp.sum(jnp.where(lo_half, 0.0, prod), axis=1, keepdims=True)
                sa = jnp.where(lo_half, sa_lo, sa_hi)
                s_new = s * w_row[:, ls] + sa * b_row[:, ls] + vc_ref[j * N:(j + 1) * N, ls] * k_row[:, ls]
                s_ref[:, ls] = s_new
                sr_ref[j * N:(j + 1) * N, ls] = (s_new * r_row[:, ls]).astype(BF16)
        for p in range(RWKV_PAIRS):
            ls = slice(p * LANES, (p + 1) * LANES)
            ycol = jnp.dot(sr_ref[:, ls], same_head, preferred_element_type=F32)
            acc = jnp.zeros((N, LANES), F32)
            for j in range(GT):
                t = g8 * GT + jnp.where(fwd, j, GT - 1 - j)
                acc = jnp.where((lane % N) == t, ycol[j * N:(j + 1) * N], acc)
            y_ref[0, 0, 0, :, ls] = jnp.where((lane % N) // GT == g8, acc, y_ref[0, 0, 0, :, ls])
        return carry
    lax.fori_loop(0, TB // GT, group, 0)

    @pl.when(c == n_blk - 1)
    def _():
        sfin_ref[0, 0] = s_ref[...]


def rwkv_scan(r, v, a_neg, decay, k_d, b_d, s0):
    bsz, L, W = r.shape
    TB, N, H = RWKV_TBLK, RWKV_HEAD, RWKV_HEADS
    n = L // TB
    gw = RWKV_PAIRS * LANES
    s0t = s0.transpose(0, 1, 3, 2, 4).reshape(bsz, N_DIR, N, W)

    def blk(d, c):
        return c + d * (n - 1 - 2 * c)
    seq = pl.BlockSpec((1, TB, gw), lambda b, g, d, c: (b, blk(d, c), g))
    seq_d = pl.BlockSpec((1, 1, TB, gw), lambda b, g, d, c: (d, b, blk(d, c), g))
    st = pl.BlockSpec((1, 1, N, gw), lambda b, g, d, c: (b, d, 0, g))
    gt = RWKV_GROUP_T
    yt, sfin = pl.pallas_call(
        functools.partial(_rwkv_kernel, n),
        grid=(bsz, W // gw, N_DIR, n),
        in_specs=[seq, seq, seq, seq_d, seq_d, seq_d, st],
        out_specs=[pl.BlockSpec((1, 1, 1, N, gw), lambda b, g, d, c: (d, b, blk(d, c), 0, g)), st],
        out_shape=[jax.ShapeDtypeStruct((N_DIR, bsz, n, N, W), F32),
                   jax.ShapeDtypeStruct((bsz, N_DIR, N, W), F32)],
        scratch_shapes=[pltpu.VMEM((N, gw), F32), pltpu.VMEM((gt * N, gw), F32), pltpu.VMEM((gt * N, gw), BF16)],
        compiler_params=pltpu.CompilerParams(
            dimension_semantics=("arbitrary",) * 4, vmem_limit_bytes=VMEM_LIMIT),
        name="rwkv_scan",
    )(r, v, a_neg, decay, k_d, b_d, s0t)
    yt = yt[0] + yt[1]
    wkv = yt.reshape(bsz, n, N, W // LANES, 2, TB).transpose(0, 1, 5, 3, 4, 2).reshape(bsz, L, W)
    sfin = sfin.reshape(bsz, N_DIR, N, H, N).transpose(0, 1, 3, 2, 4)
    return wkv, sfin


def _split_bf16(x):
    hi = x.astype(BF16)
    return hi, (x - hi.astype(F32)).astype(BF16)


def _mxu(x, y, dims=(((1,), (0,)), ((), ())), split=False):
    def d(a, b):
        return lax.dot_general(a, b, dims, preferred_element_type=F32)
    if not split:
        return d(x.astype(BF16), y.astype(BF16))
    xh, xl = _split_bf16(x)
    yh, yl = _split_bf16(y)
    return d(xh, yh) + (d(xh, yl) + d(xl, yh))


def _rwkv_chunk_kernel(n_chunks, r_ref, lw_ref, k_ref, v_ref, a_ref, b_ref, h0_ref, y_ref, hfin_ref, h_ref):
    T, N = RWKV_CHUNK, RWKV_HEAD
    d = pl.program_id(2)
    c = pl.program_id(3)

    @pl.when(c == 0)
    def _():
        h_ref[...] = h0_ref[0, 0]

    lane = lax.broadcasted_iota(jnp.int32, (T, LANES), 1)
    row = lax.broadcasted_iota(jnp.int32, (T, LANES), 0)
    lo = lane < N
    col = lane % N
    order = jnp.where(d == 0, row - col, col - row)
    seen = order >= 0
    before = order > 0
    eye = row == col
    sq_r = lax.broadcasted_iota(jnp.int32, (T, T), 0)
    sq_c = lax.broadcasted_iota(jnp.int32, (T, T), 1)
    seen_sq = (jnp.where(d == 0, sq_r - sq_c, sq_c - sq_r) >= 0).astype(F32)
    row_dims = (((0,), (0,)), ((), ()))
    lane_dims = (((1,), (1,)), ((), ()))

    def bd(x):
        return jnp.concatenate([jnp.where(lo, x, 0.0), jnp.where(lo, 0.0, x)], axis=0)

    def pp(x, y, split=False):
        return _mxu(x, bd(y), split=split)

    def ptp(x, y):
        full = _mxu(x, y, row_dims, split=True)
        return jnp.where(lo, full[:N], full[N:])

    lw_all = lw_ref[0, 0]
    cs_all = jnp.dot(seen_sq, lw_all, precision=HI, preferred_element_type=F32)
    tot_all = jnp.sum(lw_all, axis=0, keepdims=True)
    pairs = range(RWKV_CPAIRS)
    sl = [slice(p * LANES, (p + 1) * LANES) for p in pairs]
    cs = [cs_all[:, s] for s in sl]
    tot = [tot_all[:, s] for s in sl]
    e_out = [jnp.exp(-cs[p]) for p in pairs]
    at = [a_ref[0, :, sl[p]] * jnp.exp(cs[p] - lw_all[:, sl[p]]) for p in pairs]
    rt = [r_ref[0, :, sl[p]] * jnp.exp(cs[p]) for p in pairs]
    ar = [jnp.concatenate([at[p], rt[p]], axis=0) for p in pairs]
    g1 = [_mxu(ar[p], bd(b_ref[0, 0, :, sl[p]] * e_out[p]), lane_dims) for p in pairs]
    g2 = [_mxu(ar[p], bd(k_ref[0, 0, :, sl[p]] * e_out[p]), lane_dims) for p in pairs]
    a_ab = [jnp.where(before, g1[p][:T], 0.0) for p in pairs]
    a_rb = [jnp.where(seen, g1[p][T:], 0.0) for p in pairs]
    a_ak = [jnp.where(before, g2[p][:T], 0.0) for p in pairs]
    a_rk = [jnp.where(seen, g2[p][T:], 0.0) for p in pairs]
    w = [jnp.where(eye, 1.0, a_ab[p]) for p in pairs]
    apow = a_ab
    for _ in range(5):
        apow = [pp(apow[p], apow[p]) for p in pairs]
        w = [w[p] + pp(w[p], apow[p]) for p in pairs]
    v = [v_ref[0, :, sl[p]] for p in pairs]
    akv = [pp(a_ak[p], v[p]) for p in pairs]
    u_loc = [pp(w[p], akv[p]) for p in pairs]
    a_hat = [pp(w[p], at[p]) for p in pairs]
    h0 = [h_ref[:, sl[p]] for p in pairs]
    q_hat = [rt[p] + pp(a_rb[p], a_hat[p]) for p in pairs]
    y_loc = [pp(a_rb[p], u_loc[p]) + pp(a_rk[p], v[p]) for p in pairs]
    for p in pairs:
        y_ref[0, 0, :, sl[p]] = pp(q_hat[p], h0[p]) + y_loc[p]
    e_end = [jnp.exp(tot[p] - cs[p]) for p in pairs]
    bh = [b_ref[0, 0, :, sl[p]] * e_end[p] for p in pairs]
    phi = [jnp.where(eye, jnp.exp(tot[p]), 0.0) + ptp(bh[p], a_hat[p]) for p in pairs]
    gam = [ptp(jnp.concatenate([bh[p], k_ref[0, 0, :, sl[p]] * e_end[p]], axis=0),
               jnp.concatenate([u_loc[p], v[p]], axis=0)) for p in pairs]
    for p in pairs:
        h_ref[:, sl[p]] = pp(phi[p], h0[p], split=True) + gam[p]

    @pl.when(c == n_chunks - 1)
    def _():
        hfin_ref[0, 0] = h_ref[...]


def rwkv_chunk_scan(r, v, a_neg, log_decay, k_d, b_d, s0):
    bsz, L, W = r.shape
    T, N, H = RWKV_CHUNK, RWKV_HEAD, RWKV_HEADS
    n = L // T
    gw = RWKV_CPAIRS * LANES
    h0 = s0.transpose(0, 1, 4, 2, 3).reshape(bsz, N_DIR, N, W)

    def blk(d, c):
        return c + d * (n - 1 - 2 * c)
    seq = pl.BlockSpec((1, T, gw), lambda b, g, d, c: (b, blk(d, c), g))
    seq_d = pl.BlockSpec((1, 1, T, gw), lambda b, g, d, c: (d, b, blk(d, c), g))
    st = pl.BlockSpec((1, 1, N, gw), lambda b, g, d, c: (b, d, 0, g))
    y, hfin = pl.pallas_call(
        functools.partial(_rwkv_chunk_kernel, n),
        grid=(bsz, W // gw, N_DIR, n),
        in_specs=[seq, seq_d, seq_d, seq, seq, seq_d, st],
        out_specs=[seq_d, st],
        out_shape=[jax.ShapeDtypeStruct((N_DIR, bsz, L, W), F32),
                   jax.ShapeDtypeStruct((bsz, N_DIR, N, W), F32)],
        scratch_shapes=[pltpu.VMEM((N, gw), F32)],
        compiler_params=pltpu.CompilerParams(
            dimension_semantics=("arbitrary",) * 4, vmem_limit_bytes=VMEM_LIMIT),
        name="rwkv_chunk_scan",
    )(r, log_decay, k_d, v, a_neg, b_d, h0)
    sfin = hfin.reshape(bsz, N_DIR, N, H, N).transpose(0, 1, 3, 4, 2)
    return y[0] + y[1], sfin


def _rwkv_fs_kernel(n_chunks, rev, r_ref, k_ref, v_ref, wp_ref, ap_ref, w0_ref, a0_ref, kk_ref, ka_ref, h0_ref,
                    y_ref, hfin_ref, h_ref):
    T, N, SB = RWKV_CHUNK, RWKV_HEAD, RWKV_SUB
    NB = T // SB
    c = pl.program_id(2)

    @pl.when(c == 0)
    def _():
        h_ref[...] = h0_ref[0]

    lane = lax.broadcasted_iota(jnp.int32, (T, LANES), 1)
    row = lax.broadcasted_iota(jnp.int32, (T, LANES), 0)
    lo = lane < N
    col = lane % N
    order = (col - row) if rev else (row - col)
    seen = order >= 0
    before = order > 0
    eye = row == col
    sq_r = lax.broadcasted_iota(jnp.int32, (T, T), 0)
    sq_c = lax.broadcasted_iota(jnp.int32, (T, T), 1)
    seen_sq = (((sq_c - sq_r) if rev else (sq_r - sq_c)) >= 0).astype(BF16)
    same_head = ((lax.broadcasted_iota(jnp.int32, (LANES, LANES), 0) < N)
                 == (lax.broadcasted_iota(jnp.int32, (LANES, LANES), 1) < N)).astype(BF16)
    col_sb =lax.broadcasted_iota(jnp.int32, (SB, LANES), 1) % N
    row_dims = (((0,), (0,)), ((), ()))
    lane_dims = (((1,), (1,)), ((), ()))

    def bd(x):
        return jnp.concatenate([jnp.where(lo, x, 0.0), jnp.where(lo, 0.0, x)], axis=0)

    def pp(x, y, split=False):
        return _mxu(x, bd(y), split=split)

    def ptp(x, y):
        full = _mxu(x, y, row_dims, split=True)
        return jnp.where(lo, full[:N], full[N:])

    w_log = -jax.nn.softplus(-(wp_ref[0] + w0_ref[...])) - 0.5
    lw_all = -jnp.exp(w_log)
    iclr_all = jax.nn.sigmoid(ap_ref[0] + a0_ref[...])
    k_all = k_ref[0]
    kd_all = k_all * (1.0 + (iclr_all - 1.0) * ka_ref[...])
    kkr_all = k_all * kk_ref[...]
    cs_all = _dot_mask(seen_sq, lw_all)
    tot_all = jnp.sum(lw_all, axis=0, keepdims=True)
    pairs = range(RWKV_CPAIRS)
    sl = [slice(p * LANES, (p + 1) * LANES) for p in pairs]
    sq_hi = [_split_bf16(kkr_all[:, s] * kkr_all[:, s]) for s in sl]
    ssq = [jnp.dot(sq_hi[p][0], same_head, preferred_element_type=F32)
           + jnp.dot(sq_hi[p][1], same_head, preferred_element_type=F32) for p in pairs]
    kk = [kkr_all[:, sl[p]] / jnp.maximum(jnp.sqrt(ssq[p]), 1e-12) for p in pairs]
    b_in = [kk[p] * iclr_all[:, sl[p]] for p in pairs]
    cs = [cs_all[:, s] for s in sl]
    tot = [tot_all[:, s] for s in sl]
    e_out = [jnp.exp(-cs[p]) for p in pairs]
    at = [-kk[p] * jnp.exp(cs[p] - lw_all[:, sl[p]]) for p in pairs]
    rt = [r_ref[0, :, sl[p]] * jnp.exp(cs[p]) for p in pairs]
    ar = [jnp.concatenate([at[p], rt[p]], axis=0) for p in pairs]
    g1 = [_mxu(ar[p], bd(b_in[p] * e_out[p]), lane_dims) for p in pairs]
    g2 = [_mxu(ar[p], bd(kd_all[:, sl[p]] * e_out[p]), lane_dims) for p in pairs]
    a_ab = [jnp.where(before, g1[p][:T], 0.0) for p in pairs]
    a_rb = [jnp.where(seen, g1[p][T:], 0.0) for p in pairs]
    a_ak = [jnp.where(before, g2[p][:T], 0.0) for p in pairs]
    a_rk = [jnp.where(seen, g2[p][T:], 0.0) for p in pairs]
    v = [v_ref[0, :, sl[p]] for p in pairs]
    akv = [pp(a_ak[p], v[p]) for p in pairs]
    za = [[None] * NB for _ in pairs]
    zu = [[None] * NB for _ in pairs]
    zero_blk = jnp.zeros((SB, LANES), F32)
    for kpos in range(NB):
        bk = NB - 1 - kpos if rev else kpos
        rows = slice(bk * SB, (bk + 1) * SB)
        done = [(m > bk) if rev else (m < bk) for m in range(NB)]
        cur_a = [at[p][rows] for p in pairs]
        cur_u = [akv[p][rows] for p in pairs]
        if kpos > 0:
            for p in pairs:
                zc_a = jnp.concatenate([za[p][m] if done[m] else zero_blk for m in range(NB)], axis=0)
                zc_u = jnp.concatenate([zu[p][m] if done[m] else zero_blk for m in range(NB)], axis=0)
                off = _mxu(a_ab[p][rows], jnp.concatenate([bd(zc_a), bd(zc_u)], axis=1))
                cur_a[p] = cur_a[p] + off[:, :LANES]
                cur_u[p] = cur_u[p] + off[:, LANES:]
        abc = []
        for p in pairs:
            ablk = a_ab[p][rows]
            picked = jnp.concatenate([jnp.where(col_sb == bk * SB + s, ablk, 0.0) for s in range(SB)], axis=0)
            abc.append(jnp.dot(picked.astype(BF16), same_head, preferred_element_type=F32))
        for j in range(SB - 1):
            s = SB - 1 - j if rev else j
            for p in pairs:
                coef = abc[p][s * SB:(s + 1) * SB]
                cur_a[p] = cur_a[p] + coef * cur_a[p][s:s + 1]
                cur_u[p] = cur_u[p] + coef * cur_u[p][s:s + 1]
        for p in pairs:
            za[p][bk] = cur_a[p]
            zu[p][bk] = cur_u[p]
    a_hat = [jnp.concatenate(za[p], axis=0) for p in pairs]
    u_loc = [jnp.concatenate(zu[p], axis=0) for p in pairs]
    h0 = [h_ref[:, sl[p]] for p in pairs]
    q_hat = [rt[p] + pp(a_rb[p], a_hat[p]) for p in pairs]
    y_loc = [pp(a_rb[p], u_loc[p]) + pp(a_rk[p], v[p]) for p in pairs]
    for p in pairs:
        y_ref[0, :, sl[p]] = pp(q_hat[p], h0[p]) + y_loc[p]
    e_end = [jnp.exp(tot[p] - cs[p]) for p in pairs]
    bh = [b_in[p] * e_end[p] for p in pairs]
    phi = [jnp.where(eye, jnp.exp(tot[p]), 0.0) + ptp(bh[p], a_hat[p]) for p in pairs]
    gam = [ptp(jnp.concatenate([bh[p], kd_all[:, sl[p]] * e_end[p]], axis=0),
               jnp.concatenate([u_loc[p], v[p]], axis=0)) for p in pairs]
    for p in pairs:
        h_ref[:, sl[p]] = pp(phi[p], h0[p], split=True) + gam[p]

    @pl.when(c == n_chunks - 1)
    def _():
        hfin_ref[0] = h_ref[...]


def rwkv_direction(rev, main, w_pre, a_pre, w0, a0, k_k, k_a, s0):
    bsz, L, _ = main.shape
    W = RWKV_W
    T, N, H = RWKV_CHUNK, RWKV_HEAD, RWKV_HEADS
    n = L // T
    gw = RWKV_CPAIRS * LANES
    ng = W // gw
    h0 = s0.transpose(0, 3, 1, 2).reshape(bsz, N, W)

    def seq(col0):
        return pl.BlockSpec((1, T, gw), lambda b, g, c: (b, (n - 1 - c) if rev else c, col0 * ng + g))
    vec = pl.BlockSpec((1, gw), lambda b, g, c: (0, g))
    st = pl.BlockSpec((1, N, gw), lambda b, g, c: (b, 0, g))
    y, hfin = pl.pallas_call(
        functools.partial(_rwkv_fs_kernel, n, rev),
        grid=(bsz, ng, n),
        in_specs=[seq(0), seq(1), seq(2), seq(0), seq(0), vec, vec, vec, vec, st],
        out_specs=[seq(0), st],
        out_shape=[jax.ShapeDtypeStruct((bsz, L, W), F32), jax.ShapeDtypeStruct((bsz, N, W), F32)],
        scratch_shapes=[pltpu.VMEM((N, gw), F32)],
        compiler_params=pltpu.CompilerParams(
            dimension_semantics=("arbitrary",) * 3, vmem_limit_bytes=VMEM_LIMIT),
        name="rwkv_bwd_chunks" if rev else "rwkv_fwd_chunks",
    )(main, main, main, w_pre, a_pre, w0.reshape(1, W), a0.reshape(1, W), k_k.reshape(1, W), k_a.reshape(1, W), h0)
    return y, hfin.reshape(bsz, N, H, N).transpose(0, 2, 3, 1)


def _segsum(x, same_head):
    x1 = x.astype(BF16)
    r1 = x - x1.astype(F32)
    x2 = r1.astype(BF16)
    x3 = (r1 - x2.astype(F32)).astype(BF16)

    def d(a):
        return jnp.dot(a, same_head, preferred_element_type=F32)
    return d(x1) + (d(x2) + d(x3))


def _rwkv_post_kernel(yf_ref, yb_ref, r_ref, k_ref, v_ref, g_ref, af_ref, ab_ref, a0_ref, ka_ref, rk_ref,
                      lw_ref, lb_ref, o_ref):
    N = RWKV_HEAD
    same_head = ((lax.broadcasted_iota(jnp.int32, (LANES, LANES), 0) < N)
                 == (lax.broadcasted_iota(jnp.int32, (LANES, LANES), 1) < N)).astype(BF16)
    for t in range(o_ref.shape[2] // LANES):
        ls = slice(t * LANES, (t + 1) * LANES)
        wkv = yf_ref[0, :, ls] + yb_ref[0, :, ls]
        mean = _segsum(wkv, same_head) * (1.0 / N)
        cen = wkv - mean
        var = _segsum(cen * cen, same_head) * (1.0 / N)
        ln = cen * lax.rsqrt(var + RWKV_LNX_EPS) * lw_ref[:, ls] + lb_ref[:, ls]
        ka = ka_ref[:, ls]
        k_mix = ((1.0 + (jax.nn.sigmoid(af_ref[0, :, ls] + a0_ref[0:1, ls]) - 1.0) * ka)
                 + (1.0 + (jax.nn.sigmoid(ab_ref[0, :, ls] + a0_ref[1:2, ls]) - 1.0) * ka))
        bonus = _segsum(r_ref[0, :, ls] * k_ref[0, :, ls] * k_mix * rk_ref[:, ls], same_head) * v_ref[0, :, ls]
        gate = g_ref[0, :, ls]
        o_ref[0, :, ls] = ((ln + bonus) * (gate * jax.nn.sigmoid(gate))).astype(o_ref.dtype)


def rwkv_post(y_f, y_b, main, a_pre_f, a_pre_b, a0, k_a, r_k, lnx_w, lnx_b):
    bsz, L, W = y_f.shape
    tr = _pick(L, (256, 128, 64))
    tw = 512
    nw = W // tw

    def seq(col0):
        return pl.BlockSpec((1, tr, tw), lambda b, i, j: (b, i, col0 * nw + j))
    vec = pl.BlockSpec((1, tw), lambda b, i, j: (0, j))
    vec2 = pl.BlockSpec((N_DIR, tw), lambda b, i, j: (0, j))
    return pl.pallas_call(
        _rwkv_post_kernel,
        grid=(bsz, L // tr, nw),
        in_specs=[seq(0), seq(0), seq(0), seq(1), seq(2), seq(3), seq(0), seq(0), vec2, vec, vec, vec, vec],
        out_specs=seq(0),
        out_shape=jax.ShapeDtypeStruct((bsz, L, W), BF16),
        compiler_params=pltpu.CompilerParams(
            dimension_semantics=("arbitrary",) * 3, vmem_limit_bytes=VMEM_LIMIT),
        name="rwkv_post",
    )(y_f, y_b, main, main, main, main, a_pre_f, a_pre_b, a0, k_a.reshape(1, W), r_k.reshape(1, W),
      lnx_w.reshape(1, W), lnx_b.reshape(1, W))


def _split_cols(t, sizes):
    offsets, acc = [], 0
    for s in sizes[:-1]:
        acc += s
        offsets.append(acc)
    return jnp.split(t, offsets, axis=-1)


def _rms(x, w):
    return x * lax.rsqrt(jnp.mean(x * x, axis=-1, keepdims=True) + EPS) * w


def _adaln_kernel(c_ref, w_ref, b_ref, o_ref):
    cond = c_ref[...]
    act = cond * jax.nn.sigmoid(cond)
    o_ref[...] = jnp.dot(act, w_ref[...], precision=HI, preferred_element_type=F32) + b_ref[...]


def adaln(cond, w, b):
    rows, dm = cond.shape
    n = w.shape[1]
    rp = -(-rows // 8) * 8
    tn = 512
    m = pl.pallas_call(
        _adaln_kernel,
        grid=(n // tn,),
        in_specs=[pl.BlockSpec((rp, dm), lambda j: (0, 0)),
                  pl.BlockSpec((dm, tn), lambda j: (0, j)),
                  pl.BlockSpec((1, tn), lambda j: (0, j))],
        out_specs=pl.BlockSpec((rp, tn), lambda j: (0, j)),
        out_shape=jax.ShapeDtypeStruct((rp, n), F32),
        compiler_params=pltpu.CompilerParams(dimension_semantics=("arbitrary",), vmem_limit_bytes=VMEM_LIMIT),
        name="adaln",
    )(jnp.pad(cond, ((0, rp - rows), (0, 0))), w, b.reshape(1, n))[:rows]
    return jnp.split(m, 3, axis=-1)


def _grid_pos_embed(n_tokens):
    rows = n_tokens // GRID_W
    row_id = jnp.broadcast_to(jnp.arange(rows, dtype=F32)[:, None], (rows, GRID_W)).reshape(-1)
    col_id = jnp.broadcast_to(jnp.arange(GRID_W, dtype=F32)[None, :], (rows, GRID_W)).reshape(-1)
    quarter = D_MODEL // 4
    omega = 1.0 / (POS_BASE ** (jnp.arange(quarter, dtype=F32) / quarter))

    def axis_emb(pos):
        ang = pos[:, None] * omega[None, :]
        return jnp.concatenate([jnp.sin(ang), jnp.cos(ang)], axis=-1)
    return jnp.concatenate([axis_emb(row_id), axis_emb(col_id)], axis=-1)


def _even_mixer(x, gate, h, s5_re0, s5_im0, gla0, w_in, w_out, s5_ops, glu_w, glu_b, dec_up, dec_b, gla_nw):
    bsz, L, _ = h.shape
    n_main = sum(EVEN_SIZES[:-1])
    main = _mm3(h, w_in[:, :n_main])
    w_tail = jnp.pad(w_in[:, n_main:], ((0, 0), (0, LANES - N_DIR * GLA_RANK)))
    dec_lr = _mm3(h, w_tail)
    y, fin_re, fin_im = s5_scan(main[..., :S5_W], s5_ops, s5_re0, s5_im0)
    gy = jax.nn.gelu(y).astype(BF16).reshape(bsz * L, S5_W)
    mixed = matmul_glu(gy, glu_w, glu_b, main.reshape(bsz * L, n_main), S5_W, S5_W + GLA_DV_W)
    mixed, fin_gla = gla_mix(main, dec_lr, dec_up, dec_b, gla_nw, gla0, mixed.reshape(bsz, L, -1))
    return matmul_gated_residual(mixed, w_out, x, gate), fin_re, fin_im, fin_gla


def _odd_mixer(x, gate, h, rwkv0, w_in, w_out, mu, w0, w2, a0, a2, k_k, k_a, r_k, lnx_w, lnx_b):
    bsz, L, _ = h.shape
    zero = jnp.zeros_like(h[:, :1])
    h_prev = jnp.concatenate([zero, h[:, :-1]], axis=1)
    h_next = jnp.concatenate([h[:, 1:], zero], axis=1)
    xs = h + mu[0] * (h_prev - h) + mu[1] * (h_next - h)
    n_main = sum(ODD_SIZES[:4])
    main = _mm3(xs, w_in[:, :n_main])
    tail = _mm3(xs, w_in[:, n_main:])
    w_lr, a_lr = _split_cols(tail, ODD_SIZES[4:])
    w_lr = jnp.tanh(w_lr).reshape(bsz, L, N_DIR, RWKV_DECAY_RANK)
    a_lr = a_lr.reshape(bsz, L, N_DIR, RWKV_ICLR_RANK)
    ys, a_pres, finals = [], [], []
    for d in range(N_DIR):
        w_pre = _mm3(w_lr[:, :, d], w2[d])
        a_pre = _mm3(a_lr[:, :, d], a2[d])
        y_d, fin = rwkv_direction(bool(d), main, w_pre, a_pre, w0[d], a0[d], k_k, k_a, rwkv0[:, d])
        ys.append(y_d)
        a_pres.append(a_pre)
        finals.append(fin)
    out = rwkv_post(ys[0], ys[1], main, a_pres[0], a_pres[1], a0, k_a, r_k.reshape(-1), lnx_w, lnx_b)
    return matmul_gated_residual(out, w_out, x, gate), jnp.stack(finals, axis=1)


def kernel(x_prompt, x_sample, state_s5_re, state_s5_im, state_gla, state_rwkv, c, c_ctx, norm_w, ada_w, ada_b, final_norm_w, e_w_in, e_w_out, s5_lambda_re, s5_lambda_im, s5_log_step, s5_b_re, s5_b_im, s5_c_re, s5_c_im, s5_d, s5_glu_w, s5_glu_b, gla_decay_up, gla_decay_b, gla_norm_w, o_w_in, o_w_out, rwkv_mu, rwkv_w0, rwkv_w2, rwkv_a0, rwkv_a2, rwkv_k_k, rwkv_k_a, rwkv_r_k, rwkv_lnx_w, rwkv_lnx_b):
    bp = x_prompt.shape[0]
    depth = norm_w.shape[0]
    x_ctx = x_prompt
    x_lat = x_sample + _grid_pos_embed(x_sample.shape[1])[None]
    z_s5 = jnp.zeros((bp, N_DIR, S5_GROUPS, S5_STATE), F32)
    z_gla = jnp.zeros((bp, N_DIR, GLA_HEADS, GLA_DK, GLA_DV), F32)
    z_rwkv = jnp.zeros((bp, N_DIR, RWKV_HEADS, RWKV_HEAD, RWKV_HEAD), F32)
    new_s5_re, new_s5_im, new_gla, new_rwkv = [], [], [], []
    n_lat = c.shape[0]
    cond = jnp.concatenate([c, c_ctx[None]], axis=0)
    for i in range(depth):
        j = i // 2
        shift, scale, gate = adaln(cond, ada_w[i], ada_b[i])
        gt_l, gt_c = gate[:n_lat], jnp.broadcast_to(gate[n_lat:], (bp, D_MODEL))
        h_ctx = norm_mod(x_ctx, norm_w[i], scale[n_lat:], shift[n_lat:])
        h_lat = norm_mod(x_lat, norm_w[i], scale[:n_lat], shift[:n_lat])
        if i % 2 == 0:
            s5_ops = s5_operators(s5_lambda_re[j], s5_lambda_im[j], s5_log_step[j], s5_b_re[j], s5_b_im[j],
                                  s5_c_re[j], s5_c_im[j], s5_d[j])
            p = (e_w_in[j], e_w_out[j], s5_ops, s5_glu_w[j], s5_glu_b[j], gla_decay_up[j], gla_decay_b[j],
                 gla_norm_w[j])
            x_ctx, fr, fi, fg = _even_mixer(x_ctx, gt_c, h_ctx, z_s5, z_s5, z_gla, *p)
            x_lat, _, _, _ = _even_mixer(x_lat, gt_l, h_lat, state_s5_re[:, j], state_s5_im[:, j],
                                         state_gla[:, j], *p)
            new_s5_re.append(fr)
            new_s5_im.append(fi)
            new_gla.append(fg)
        else:
            p = (o_w_in[j], o_w_out[j], rwkv_mu[j], rwkv_w0[j], rwkv_w2[j], rwkv_a0[j], rwkv_a2[j],
                 rwkv_k_k[j], rwkv_k_a[j], rwkv_r_k[j], rwkv_lnx_w[j], rwkv_lnx_b[j])
            x_ctx, fw = _odd_mixer(x_ctx, gt_c, h_ctx, z_rwkv, *p)
            x_lat, _ = _odd_mixer(x_lat, gt_l, h_lat, state_rwkv[:, j], *p)
            new_rwkv.append(fw)
    y_prompt = final_norm(x_ctx, final_norm_w)
    y_sample = final_norm(x_lat, final_norm_w)
    return (y_prompt, y_sample, jnp.stack(new_s5_re, axis=1), jnp.stack(new_s5_im, axis=1),
            jnp.stack(new_gla, axis=1), jnp.stack(new_rwkv, axis=1))
```

```python
import functools
import math

import jax
import jax.numpy as jnp
from jax import lax
from jax.experimental import pallas as pl
from jax.experimental.pallas import tpu as pltpu

D_MODEL = 2048
GRID_W = 64
POS_BASE = 10000.0
N_DIR = 2
EPS = 1e-6
S5_W = 1024
S5_GROUP_CH = 16
S5_GROUPS = 64
S5_STATE = 64
S5_CHUNK = 16
S5_TILE_GROUPS = 8
GLA_HEADS = 6
GLA_DV = 512
GLA_DK = 256
GLA_DK_W = 1536
GLA_DV_W = 3072
GLA_RANK = 16
GLA_NORMALIZER = 16.0
GLA_CHUNK = 64
GLA_NC = 8
GLA_LOG_DECAY_MIN = -1.0
EVEN_SIZES = (S5_W, S5_W, GLA_DK_W, GLA_DK_W, GLA_DV_W, GLA_DV_W, N_DIR * GLA_RANK)
RWKV_W = 2048
RWKV_HEAD = 64
RWKV_HEADS = 32
RWKV_DECAY_RANK = 96
RWKV_ICLR_RANK = 96
RWKV_LNX_EPS = 64e-5
ODD_SIZES = (RWKV_W, RWKV_W, RWKV_W, RWKV_W, N_DIR * RWKV_DECAY_RANK, N_DIR * RWKV_ICLR_RANK)
RWKV_TBLK = 64
RWKV_PAIRS = 8
RWKV_GROUP_T = 8
RWKV_CHUNK = 64
RWKV_CPAIRS = 16
RWKV_SUB = 16
LANES = 128

VMEM_LIMIT = 48 * 1024 * 1024
HI = lax.Precision.HIGHEST
BF16 = jnp.bfloat16
F32 = jnp.float32


def _mm_kernel(x_ref, w_ref, o_ref):
    o_ref[...] = jnp.dot(x_ref[...], w_ref[...], preferred_element_type=F32)


def _pick(n, prefs):
    for p in prefs:
        if n % p == 0:
            return p
    return n


def matmul(x, w):
    m, k = x.shape
    n = w.shape[1]
    x = x.astype(BF16)
    w = w.astype(BF16)
    tm = _pick(m, (1024, 512, 256, 128, 64, 32, 16, 8))
    tn = _pick(n, (1024, 512, 384, 256, 128))
    return pl.pallas_call(
        _mm_kernel,
        grid=(m // tm, n // tn),
        in_specs=[pl.BlockSpec((tm, k), lambda i, j: (i, 0)),
                  pl.BlockSpec((k, tn), lambda i, j: (0, j))],
        out_specs=pl.BlockSpec((tm, tn), lambda i, j: (i, j)),
        out_shape=jax.ShapeDtypeStruct((m, n), F32),
        compiler_params=pltpu.CompilerParams(
            dimension_semantics=("arbitrary", "arbitrary"), vmem_limit_bytes=VMEM_LIMIT),
        name="proj_matmul",
    )(x, w)


def _mm3(h, w):
    b, l, k = h.shape
    return matmul(h.reshape(b * l, k), w).reshape(b, l, -1)


def _mm_residual_kernel(x_ref, w_ref, res_ref, gate_ref, o_ref):
    acc = jnp.dot(x_ref[...], w_ref[...], preferred_element_type=F32)
    o_ref[...] = res_ref[...] + gate_ref[0] * acc


def matmul_gated_residual(x, w, res, gate):
    bsz, L, k = x.shape
    n = w.shape[1]
    m = bsz * L
    tm = _pick(L, (1024, 512, 256, 128))
    tn = _pick(n, (1024, 512, 256, 128) if k <= 2048 else (512, 256, 128))
    per_b = L // tm
    out = pl.pallas_call(
        _mm_residual_kernel,
        grid=(m // tm, n // tn),
        in_specs=[pl.BlockSpec((tm, k), lambda i, j: (i, 0)),
                  pl.BlockSpec((k, tn), lambda i, j: (0, j)),
                  pl.BlockSpec((tm, tn), lambda i, j: (i, j)),
                  pl.BlockSpec((1, 1, tn), lambda i, j: (i // per_b, 0, j))],
        out_specs=pl.BlockSpec((tm, tn), lambda i, j: (i, j)),
        out_shape=jax.ShapeDtypeStruct((m, n), F32),
        compiler_params=pltpu.CompilerParams(
            dimension_semantics=("arbitrary", "arbitrary"), vmem_limit_bytes=VMEM_LIMIT),
        name="proj_residual",
    )(x.reshape(m, k).astype(BF16), w.astype(BF16), res.reshape(m, n), gate.reshape(bsz, 1, n))
    return out.reshape(bsz, L, n)


def _mm_glu_kernel(x_ref, w_ref, b_ref, g_ref, xt_ref, o_ref):
    acc = jnp.dot(x_ref[...], w_ref[...], preferred_element_type=F32) + b_ref[...]
    gy = xt_ref[...].astype(F32)
    gate = g_ref[...]
    o_ref[...] = (gy * jax.nn.sigmoid(acc) * (gate * jax.nn.sigmoid(gate))).astype(o_ref.dtype)


def matmul_glu(gy, w, b, main, g_col0, n_total):
    m, k = gy.shape
    n = w.shape[1]
    tm = _pick(m, (1024, 512, 256, 128))
    tn = 512
    return pl.pallas_call(
        _mm_glu_kernel,
        grid=(m // tm, n // tn),
        in_specs=[pl.BlockSpec((tm, k), lambda i, j: (i, 0)),
                  pl.BlockSpec((k, tn), lambda i, j: (0, j)),
                  pl.BlockSpec((1, tn), lambda i, j: (0, j)),
                  pl.BlockSpec((tm, tn), lambda i, j: (i, g_col0 // tn + j)),
                  pl.BlockSpec((tm, tn), lambda i, j: (i, j))],
        out_specs=pl.BlockSpec((tm, tn), lambda i, j: (i, j)),
        out_shape=jax.ShapeDtypeStruct((m, n_total), BF16),
        compiler_params=pltpu.CompilerParams(
            dimension_semantics=("arbitrary", "arbitrary"), vmem_limit_bytes=VMEM_LIMIT),
        name="s5_glu_gate",
    )(gy, w.astype(BF16), b.reshape(1, n), main, gy)


def _norm_mod_kernel(x_ref, nw_ref, sc_ref, sh_ref, o_ref):
    x = x_ref[0]
    inv = lax.rsqrt(jnp.mean(x * x, axis=-1, keepdims=True) + EPS)
    o_ref[0] = (x * inv * nw_ref[...] * (1.0 + sc_ref[0]) + sh_ref[0]).astype(o_ref.dtype)


def norm_mod(x, nw, scale, shift):
    bsz, L, dm = x.shape
    tr = _pick(L, (256, 128, 64))
    nb = scale.shape[0]
    cond = pl.BlockSpec((1, 1, dm), lambda b, i: (b if nb > 1 else 0, 0, 0))
    return pl.pallas_call(
        _norm_mod_kernel,
        grid=(bsz, L // tr),
        in_specs=[pl.BlockSpec((1, tr, dm), lambda b, i: (b, i, 0)),
                  pl.BlockSpec((1, dm), lambda b, i: (0, 0)), cond, cond],
        out_specs=pl.BlockSpec((1, tr, dm), lambda b, i: (b, i, 0)),
        out_shape=jax.ShapeDtypeStruct((bsz, L, dm), BF16),
        compiler_params=pltpu.CompilerParams(
            dimension_semantics=("arbitrary", "arbitrary"), vmem_limit_bytes=VMEM_LIMIT),
        name="norm_mod",
    )(x, nw.reshape(1, dm), scale.reshape(nb, 1, dm), shift.reshape(nb, 1, dm))


def _final_norm_kernel(x_ref, nw_ref, o_ref):
    x = x_ref[0]
    o_ref[0] = x * lax.rsqrt(jnp.mean(x * x, axis=-1, keepdims=True) + EPS) * nw_ref[...]


def final_norm(x, nw):
    bsz, L, dm = x.shape
    tr = _pick(L, (256, 128, 64))
    return pl.pallas_call(
        _final_norm_kernel,
        grid=(bsz, L // tr),
        in_specs=[pl.BlockSpec((1, tr, dm), lambda b, i: (b, i, 0)), pl.BlockSpec((1, dm), lambda b, i: (0, 0))],
        out_specs=pl.BlockSpec((1, tr, dm), lambda b, i: (b, i, 0)),
        out_shape=jax.ShapeDtypeStruct((bsz, L, dm), F32),
        compiler_params=pltpu.CompilerParams(
            dimension_semantics=("arbitrary", "arbitrary"), vmem_limit_bytes=VMEM_LIMIT),
        name="final_norm",
    )(x, nw.reshape(1, dm))


def s5_operators(lam_re, lam_im, log_step, b_re, b_im, c_re, c_im, d_skip):
    T = S5_CHUNK
    dt = jnp.exp(log_step)[..., None]
    mag = jnp.exp(lam_re * dt)
    ab_re, ab_im = mag * jnp.cos(lam_im * dt), mag * jnp.sin(lam_im * dt)
    den = lam_re * lam_re + lam_im * lam_im
    f_re = ((ab_re - 1.0) * lam_re + ab_im * lam_im) / den
    f_im = (ab_im * lam_re - (ab_re - 1.0) * lam_im) / den
    bb_re = f_re[..., None] * b_re - f_im[..., None] * b_im
    bb_im = f_re[..., None] * b_im + f_im[..., None] * b_re
    kk = jnp.arange(T + 1, dtype=F32)[:, None, None, None]
    pmag = jnp.exp(kk * (lam_re * dt))
    pr = pmag * jnp.cos(kk * (lam_im * dt))
    pi = pmag * jnp.sin(kk * (lam_im * dt))
    zr = pr[:T, :, :, :, None] * bb_re - pi[:T, :, :, :, None] * bb_im
    zi = pr[:T, :, :, :, None] * bb_im + pi[:T, :, :, :, None] * bb_re
    kern = (jnp.einsum('dghp,kdgpj->kdghj', c_re, zr, precision=HI)
            - jnp.einsum('dghp,kdgpj->kdghj', c_im, zi, precision=HI))
    t_idx = jnp.arange(T)[:, None]
    s_idx = jnp.arange(T)[None, :]
    lag_f = t_idx - s_idx
    lag_b = s_idx - t_idx
    m_f = jnp.where((lag_f >= 0)[:, :, None, None, None], kern[:, 0][jnp.clip(lag_f, 0, T - 1)], 0.0)
    m_b = jnp.where((lag_b >= 0)[:, :, None, None, None], kern[:, 1][jnp.clip(lag_b, 0, T - 1)], 0.0)
    m = m_f + m_b
    eye_t = jnp.eye(T, dtype=F32)[:, :, None, None, None]
    eye_h = jnp.eye(S5_GROUP_CH, dtype=F32)[None, None, None]
    m = m + eye_t * eye_h * d_skip.reshape(S5_GROUPS, S5_GROUP_CH)[None, None, :, :, None]
    g = m.shape[2]
    m_t = m.transpose(2, 1, 4, 0, 3).reshape(g, T * S5_GROUP_CH, T * S5_GROUP_CH)
    pf_r, pf_i = pr[T - 1::-1][:T, 0], pi[T - 1::-1][:T, 0]
    pb_r, pb_i = pr[:T, 1], pi[:T, 1]

    def f_mat(p_r, p_i, d):
        re = p_r[..., None] * bb_re[d][None] - p_i[..., None] * bb_im[d][None]
        im = p_r[..., None] * bb_im[d][None] + p_i[..., None] * bb_re[d][None]
        re = re.transpose(1, 0, 3, 2).reshape(g, T * S5_GROUP_CH, S5_STATE)
        im = im.transpose(1, 0, 3, 2).reshape(g, T * S5_GROUP_CH, S5_STATE)
        return re, im
    ff_re, ff_im = f_mat(pf_r, pf_i, 0)
    fb_re, fb_im = f_mat(pb_r, pb_i, 1)
    a_t = jnp.concatenate([m_t, ff_re, fb_re, ff_im, fb_im], axis=-1)
    ef_r, ef_i = pr[1:T + 1, 0], pi[1:T + 1, 0]
    eb_r, eb_i = pr[T:0:-1, 1], pi[T:0:-1, 1]

    def e_mat(p_r, p_i, d):
        er = c_re[d][None] * p_r[:, :, None, :] - c_im[d][None] * p_i[:, :, None, :]
        ei = -(c_re[d][None] * p_i[:, :, None, :] + c_im[d][None] * p_r[:, :, None, :])
        er = er.transpose(1, 3, 0, 2).reshape(g, S5_STATE, T * S5_GROUP_CH)
        ei = ei.transpose(1, 3, 0, 2).reshape(g, S5_STATE, T * S5_GROUP_CH)
        return er, ei
    efr, efi = e_mat(ef_r, ef_i, 0)
    ebr, ebi = e_mat(eb_r, eb_i, 1)
    e_t = jnp.concatenate([efr, ebr, efi, ebi], axis=1)
    lam_t = jnp.concatenate([pr[T, 0], pr[T, 1], pi[T, 0], pi[T, 1]], axis=-1)[:, None, :]
    return a_t.astype(BF16), e_t.astype(BF16), lam_t


def _s5_kernel(n_chunks, bsz, x8_ref, sel_ref, at_ref, et_ref, lam_ref, h0_ref, y_ref, hfin_ref, z_ref, hent_ref):
    P = S5_STATE
    ut = jnp.dot(x8_ref[0], sel_ref[0], preferred_element_type=F32).astype(BF16)
    z_ref[...] = jnp.dot(ut, at_ref[0], preferred_element_type=F32)
    lam = lam_ref[0]
    a_re, a_im = lam[:, 0:2 * P], lam[:, 2 * P:4 * P]
    h0 = h0_ref[0]
    fwd_lanes = lax.broadcasted_iota(jnp.int32, (bsz, 2 * P), 1) < P

    def step(c, carry):
        h_re, h_im = carry
        rf = pl.ds(pl.multiple_of(c * bsz, 8), bsz)
        rb = pl.ds(pl.multiple_of((n_chunks - 1 - c) * bsz, 8), bsz)
        hent_ref[rf, 0:P] = h_re[:, 0:P]
        hent_ref[rb, P:2 * P] = h_re[:, P:2 * P]
        hent_ref[rf, 2 * P:3 * P] = h_im[:, 0:P]
        hent_ref[rb, 3 * P:4 * P] = h_im[:, P:2 * P]
        g_re = jnp.where(fwd_lanes, z_ref[rf, 4 * P:6 * P], z_ref[rb, 4 * P:6 * P])
        g_im = jnp.where(fwd_lanes, z_ref[rf, 6 * P:8 * P], z_ref[rb, 6 * P:8 * P])
        return a_re * h_re - a_im * h_im + g_re, a_re * h_im + a_im * h_re + g_im
    h_re, h_im = lax.fori_loop(0, n_chunks, step, (h0[:, 0:2 * P], h0[:, 2 * P:4 * P]))
    hfin_ref[0, :, 0:2 * P] = h_re
    hfin_ref[0, :, 2 * P:4 * P] = h_im
    y = z_ref[:, 0:4 * P] + jnp.dot(hent_ref[...].astype(BF16), et_ref[0], preferred_element_type=F32)
    y_ref[0] = jax.nn.gelu(y).astype(y_ref.dtype)


def _s5_unpack_kernel(yt_ref, selt_ref, o_ref):
    acc = jnp.dot(yt_ref[0], selt_ref[0], preferred_element_type=F32)
    for gl in range(1, S5_TILE_GROUPS):
        acc = acc + jnp.dot(yt_ref[gl], selt_ref[gl], preferred_element_type=F32)
    o_ref[0] = acc.astype(o_ref.dtype)


def s5_scan(u, ops, h0_re, h0_im):
    a_t, e_t, lam_t = ops
    b_real, L, _ = u.shape
    T, G, H, P = S5_CHUNK, S5_GROUPS, S5_GROUP_CH, S5_STATE
    TG = S5_TILE_GROUPS
    n = L // T
    bsz = -(-b_real // 8) * 8
    cols = n * bsz
    x8 = u.reshape(b_real, n, T, G // TG, LANES).transpose(3, 1, 0, 2, 4).astype(BF16)
    x8 = jnp.pad(x8, ((0, 0), (0, 0), (0, bsz - b_real), (0, 0), (0, 0))).reshape(G // TG, cols, T * LANES)
    src = jnp.arange(T * LANES)
    dst = jnp.arange(T * H)
    sel = ((src[None, :, None] // LANES == dst[None, None, :] // H)
           & (src[None, :, None] % H == dst[None, None, :] % H)
           & ((src[None, :, None] % LANES) // H == jnp.arange(TG)[:, None, None])).astype(BF16)
    h0 = jnp.concatenate([h0_re[:, 0], h0_re[:, 1], h0_im[:, 0], h0_im[:, 1]], axis=-1)
    h0 = jnp.pad(h0.transpose(1, 0, 2), ((0, 0), (0, bsz - b_real), (0, 0)))
    yt, hfin = pl.pallas_call(
        functools.partial(_s5_kernel, n, bsz),
        grid=(G,),
        in_specs=[pl.BlockSpec((1, cols, T * LANES), lambda g: (g // TG, 0, 0)),
                  pl.BlockSpec((1, T * LANES, T * H), lambda g: (g % TG, 0, 0)),
                  pl.BlockSpec((1, T * H, 8 * P), lambda g: (g, 0, 0)),
                  pl.BlockSpec((1, 4 * P, T * H), lambda g: (g, 0, 0)),
                  pl.BlockSpec((1, 1, 4 * P), lambda g: (g, 0, 0)),
                  pl.BlockSpec((1, bsz, 4 * P), lambda g: (g, 0, 0))],
        out_specs=[pl.BlockSpec((1, cols, T * H), lambda g: (g, 0, 0)),
                   pl.BlockSpec((1, bsz, 4 * P), lambda g: (g, 0, 0))],
        out_shape=[jax.ShapeDtypeStruct((G, cols, T * H), BF16),
                   jax.ShapeDtypeStruct((G, bsz, 4 * P), F32)],
        scratch_shapes=[pltpu.VMEM((cols, 8 * P), F32), pltpu.VMEM((cols, 4 * P), F32)],
        compiler_params=pltpu.CompilerParams(dimension_semantics=("arbitrary",), vmem_limit_bytes=VMEM_LIMIT),
        name="s5_chunk_scan",
    )(x8, sel, a_t, e_t, lam_t, h0)
    tr = _pick(cols, (512, 256, 128))
    y8 = pl.pallas_call(
        _s5_unpack_kernel,
        grid=(G // TG, cols // tr),
        in_specs=[pl.BlockSpec((TG, tr, T * H), lambda t, i: (t, i, 0)),
                  pl.BlockSpec((TG, T * H, T * LANES), lambda t, i: (0, 0, 0))],
        out_specs=pl.BlockSpec((1, tr, T * LANES), lambda t, i: (t, i, 0)),
        out_shape=jax.ShapeDtypeStruct((G // TG, cols, T * LANES), BF16),
        compiler_params=pltpu.CompilerParams(
            dimension_semantics=("arbitrary", "arbitrary"), vmem_limit_bytes=VMEM_LIMIT),
        name="s5_unpack",
    )(yt, sel.transpose(0, 2, 1))
    y = y8.reshape(G // TG, n, bsz, T, LANES)[:, :, :b_real].transpose(2, 1, 3, 0, 4).reshape(b_real, L, G * H)
    hfin = hfin[:, :b_real].transpose(1, 0, 2)
    fin_re = jnp.stack([hfin[..., 0:P], hfin[..., P:2 * P]], axis=1)
    fin_im = jnp.stack([hfin[..., 2 * P:3 * P], hfin[..., 3 * P:4 * P]], axis=1)
    return y, fin_re, fin_im


def _dot_t(a, b):
    return lax.dot_general(a, b, (((1,), (1,)), ((), ())), preferred_element_type=F32)


def _gla_kernel(n_chunks, q_ref, k_ref, v_ref, g_ref, lr_ref, up_ref, db_ref, nw_ref, s0_ref,
                out_ref, sfin_ref, s_ref, of_ref):
    C = GLA_CHUNK
    d = pl.program_id(2)
    c = pl.program_id(3)
    cidx = jnp.where(d == 0, c, n_chunks - 1 - c)

    @pl.when(c == 0)
    def _():
        s_ref[...] = s0_ref[0, 0, 0]

    z = jnp.dot(lr_ref[0], up_ref[0], precision=HI, preferred_element_type=F32) + db_ref[0]
    gc = jnp.maximum(jax.nn.log_sigmoid(z) * (1.0 / GLA_NORMALIZER), GLA_LOG_DECAY_MIN)
    row = lax.broadcasted_iota(jnp.int32, (C, C), 0)
    col = lax.broadcasted_iota(jnp.int32, (C, C), 1)
    seen = jnp.where(d == 0, row - col, col - row) >= 0
    bcum = jnp.dot(seen.astype(F32), gc, precision=HI, preferred_element_type=F32)
    b_last = jnp.sum(gc, axis=0, keepdims=True)
    q = q_ref[0] * (GLA_DK ** -0.5)
    k = k_ref[0]
    v = v_ref[0].astype(BF16)
    q_dec = (q * jnp.exp(bcum)).astype(BF16)
    k_inv = (k * jnp.exp(-bcum)).astype(BF16)
    k_end = (k * jnp.exp(b_last - bcum)).astype(BF16)
    att = jnp.where(seen, _dot_t(q_dec, k_inv), 0.0).astype(BF16)
    s_old = s_ref[...]
    o = (jnp.dot(att, v, preferred_element_type=F32)
         + jnp.dot(q_dec, s_old.astype(BF16), preferred_element_type=F32))
    eye = (lax.broadcasted_iota(jnp.int32, (GLA_DK, GLA_DK), 0)
           == lax.broadcasted_iota(jnp.int32, (GLA_DK, GLA_DK), 1)).astype(F32)
    bl_col = lax.dot_general(eye, jnp.broadcast_to(b_last, (8, GLA_DK)), (((1,), (1,)), ((), ())),
                             precision=HI, preferred_element_type=F32)[:, 0:1]
    kv = lax.dot_general(k_end, v, (((0,), (0,)), ((), ())), preferred_element_type=F32)
    s_ref[...] = jnp.exp(bl_col) * s_old + kv

    rows = pl.ds(pl.multiple_of(cidx * C, C), C)

    @pl.when(d == 0)
    def _():
        of_ref[rows, :] = o

    @pl.when(d == 1)
    def _():
        tot = of_ref[rows, :] + o
        nrm = tot * lax.rsqrt(jnp.mean(tot * tot, axis=-1, keepdims=True) + EPS) * nw_ref[0]
        gate = g_ref[0]
        out_ref[0] = (nrm * (gate * jax.nn.sigmoid(gate))).astype(out_ref.dtype)

    @pl.when(c == n_chunks - 1)
    def _():
        sfin_ref[0, 0, 0] = s_ref[...]


def _dot_mask(mask_bf16, x, x_rows_to_sublanes=False):
    def d(b):
        if x_rows_to_sublanes:
            return lax.dot_general(b, mask_bf16, (((0,), (0,)), ((), ())), preferred_element_type=F32)
        return jnp.dot(mask_bf16, b, preferred_element_type=F32)
    x1 = x.astype(BF16)
    r1 = x - x1.astype(F32)
    x2 = r1.astype(BF16)
    x3 = (r1 - x2.astype(F32)).astype(BF16)
    return d(x1) + (d(x2) + d(x3))


def _gla_block_kernel(n_blocks, NC, q_ref, k_ref, v_ref, g_ref, lr_ref, up_ref, db_ref, nw_ref, s0_ref, dst_ref,
                      out_ref, sfin_ref, s_ref, of_ref, qd_ref, ov_ref, kv_ref, dc_ref):
    C = GLA_CHUNK
    R = C * NC
    d = pl.program_id(2)
    c = pl.program_id(3)
    bidx = jnp.where(d == 0, c, n_blocks - 1 - c)

    @pl.when(c == 0)
    def _():
        s_ref[...] = s0_ref[0, 0, 0]

    z = _mxu(lr_ref[0], up_ref[0], split=True) + db_ref[0]
    gc = jnp.maximum(jax.nn.log_sigmoid(z) * (1.0 / GLA_NORMALIZER), GLA_LOG_DECAY_MIN)
    row_c = lax.broadcasted_iota(jnp.int32, (C, C), 0)
    col_c = lax.broadcasted_iota(jnp.int32, (C, C), 1)
    seen_c = jnp.where(d == 0, row_c - col_c, col_c - row_c) >= 0
    seen_bf = seen_c.astype(BF16)
    rs = [slice(i * C, (i + 1) * C) for i in range(NC)]
    bcum_c = [_dot_mask(seen_bf, gc[r]) for r in rs]
    btot_c = [jnp.broadcast_to(jnp.where(d == 0, b[C - 1:C], b[0:1]), (C, GLA_DK)) for b in bcum_c]
    bcum = jnp.concatenate(bcum_c, axis=0)
    btot = jnp.concatenate(btot_c, axis=0)
    q_dec = (q_ref[0] * (GLA_DK ** -0.5) * jnp.exp(bcum)).astype(BF16)
    k = k_ref[0]
    k_inv = (k * jnp.exp(-bcum)).astype(BF16)
    k_end = (k * jnp.exp(btot - bcum)).astype(BF16)
    v = v_ref[0].astype(BF16)
    ones_c = jnp.ones((C, LANES), BF16)
    qd_ref[...] = q_dec.reshape(NC, C, GLA_DK)
    att = [jnp.where(seen_c, _dot_t(q_dec[r], k_inv[r]), 0.0).astype(BF16) for r in rs]
    for i in range(NC):
        kv_ref[i] = lax.dot_general(k_end[rs[i]], v[rs[i]], (((0,), (0,)), ((), ())), preferred_element_type=F32)
    for i in range(NC):
        ov_ref[i] = jnp.dot(att[i], v[rs[i]], preferred_element_type=F32)
    for i in range(NC):
        dc_ref[i] = _dot_mask(ones_c, gc[rs[i]], x_rows_to_sublanes=True)

    for i in range(NC):
        ci = jnp.where(d == 0, i, NC - 1 - i)
        s_old = s_ref[...]
        o = ov_ref[ci] + jnp.dot(qd_ref[ci], s_old.astype(BF16), preferred_element_type=F32)
        s_ref[...] = jnp.exp(dc_ref[ci][:, 0:1]) * s_old + kv_ref[ci]
        rows = pl.ds(pl.multiple_of(bidx * R + ci * C, C), C)
        orow = pl.ds(pl.multiple_of(ci * C, C), C)

        @pl.when(d == 0)
        def _():
            of_ref[rows, :] = o

        @pl.when(d == 1)
        def _():
            tot = of_ref[rows, :] + o
            nrm = tot * lax.rsqrt(jnp.mean(tot * tot, axis=-1, keepdims=True) + EPS) * nw_ref[0]
            gate = g_ref[0, orow, :]
            out_ref[0, orow, :] = (nrm * (gate * jax.nn.sigmoid(gate))).astype(out_ref.dtype)

    @pl.when(c == n_blocks - 1)
    def _():
        sfin_ref[0, 0, 0] = s_ref[...]


def gla_mix(main, dec_lr, dec_up, dec_b, gla_nw, s0, dst):
    bsz, L, _ = main.shape
    H, DK, DV = GLA_HEADS, GLA_DK, GLA_DV
    nc = min(GLA_NC, L // GLA_CHUNK)
    C = GLA_CHUNK * nc
    n = L // C
    q_blk = sum(EVEN_SIZES[:2]) // DK
    k_blk = sum(EVEN_SIZES[:3]) // DK
    v_blk = sum(EVEN_SIZES[:4]) // DV
    g_blk = sum(EVEN_SIZES[:5]) // DV
    up = jnp.zeros((N_DIR, LANES, GLA_DK_W), F32)
    for d in range(N_DIR):
        up = up.at[d, d * GLA_RANK:(d + 1) * GLA_RANK].set(dec_up[d])
    db = dec_b.reshape(N_DIR, 1, GLA_DK_W)
    nw = gla_nw.reshape(1, GLA_DV_W)

    def chunk(d, c):
        return c + d * (n - 1 - 2 * c)

    def out_chunk(d, c):
        return (n - 1) - d * c
    out, sfin = pl.pallas_call(
        functools.partial(_gla_block_kernel, n, nc),
        grid=(bsz, H, N_DIR, n),
        in_specs=[pl.BlockSpec((1, C, DK), lambda b, h, d, c: (b, chunk(d, c), q_blk + h)),
                  pl.BlockSpec((1, C, DK), lambda b, h, d, c: (b, chunk(d, c), k_blk + h)),
                  pl.BlockSpec((1, C, DV), lambda b, h, d, c: (b, chunk(d, c), v_blk + h)),
                  pl.BlockSpec((1, C, DV), lambda b, h, d, c: (b, chunk(d, c), g_blk + h)),
                  pl.BlockSpec((1, C, LANES), lambda b, h, d, c: (b, chunk(d, c), 0)),
                  pl.BlockSpec((1, LANES, DK), lambda b, h, d, c: (d, 0, h)),
                  pl.BlockSpec((1, 1, DK), lambda b, h, d, c: (d, 0, h)),
                  pl.BlockSpec((1, DV), lambda b, h, d, c: (0, h)),
                  pl.BlockSpec((1, 1, 1, DK, DV), lambda b, h, d, c: (b, d, h, 0, 0)),
                  pl.BlockSpec(memory_space=pl.ANY)],
        input_output_aliases={9: 0},
        out_specs=[pl.BlockSpec((1, C, DV), lambda b, h, d, c: (b, out_chunk(d, c), S5_W // DV + h)),
                   pl.BlockSpec((1, 1, 1, DK, DV), lambda b, h, d, c: (b, d, h, 0, 0))],
        out_shape=[jax.ShapeDtypeStruct(dst.shape, BF16),
                   jax.ShapeDtypeStruct((bsz, N_DIR, H, DK, DV), F32)],
        scratch_shapes=[pltpu.VMEM((DK, DV), F32), pltpu.VMEM((L, DV), F32),
                        pltpu.VMEM((nc, GLA_CHUNK, DK), BF16), pltpu.VMEM((nc, GLA_CHUNK, DV), F32),
                        pltpu.VMEM((nc, DK, DV), F32), pltpu.VMEM((nc, DK, LANES), F32)],
        compiler_params=pltpu.CompilerParams(
            dimension_semantics=("arbitrary",) * 4, vmem_limit_bytes=VMEM_LIMIT),
        name="gla_chunk_scan",
    )(main, main, main, main, dec_lr, up, db, nw, s0, dst)
    return out, sfin


def _rwkv_kernel(n_blk, r_ref, v_ref, a_ref, w_ref, k_ref, b_ref, s0_ref, y_ref, sfin_ref,
                 s_ref, vc_ref, sr_ref):
    TB, N, GT = RWKV_TBLK, RWKV_HEAD, RWKV_GROUP_T
    d = pl.program_id(2)
    c = pl.program_id(3)

    @pl.when(c == 0)
    def _():
        s_ref[...] = s0_ref[0, 0]

    lane = lax.broadcasted_iota(jnp.int32, (N, LANES), 1)
    row = lax.broadcasted_iota(jnp.int32, (N, LANES), 0)
    lo_half = lane < N
    diag = (lane % N) == row
    same_head = ((lax.broadcasted_iota(jnp.int32, (LANES, LANES), 0) < N)
                 == (lax.broadcasted_iota(jnp.int32, (LANES, LANES), 1) < N)).astype(BF16)
    fwd = d == 0
    y_ref[...] = jnp.zeros_like(y_ref)

    def row_of(tile, j):
        return jnp.where(fwd, tile[j:j + 1], tile[GT - 1 - j:GT - j])

    def group(i, carry):
        g8 = jnp.where(fwd, i, TB // GT - 1 - i)
        rs = pl.ds(pl.multiple_of(g8 * GT, GT), GT)
        v_tile = v_ref[0, rs, :]
        for p in range(RWKV_PAIRS):
            ls = slice(p * LANES, (p + 1) * LANES)
            vd = jnp.concatenate([jnp.where(diag, row_of(v_tile, j)[:, ls], 0.0) for j in range(GT)], axis=0)
            vd_hi = vd.astype(BF16)
            vd_lo = (vd - vd_hi.astype(F32)).astype(BF16)
            vcol = (jnp.dot(vd_hi, same_head, preferred_element_type=F32)
                    + jnp.dot(vd_lo, same_head, preferred_element_type=F32))
            vc_ref[:, ls] = vcol
        tiles = (r_ref[0, rs, :], a_ref[0, rs, :], w_ref[0, 0, rs, :], k_ref[0, 0, rs, :], b_ref[0, 0, rs, :])
        for j in range(GT):
            r_row, a_row, w_row, k_row, b_row = (row_of(x, j) for x in tiles)
            for p in range(RWKV_PAIRS):
                ls = slice(p * LANES, (p + 1) * LANES)
                s = s_ref[:, ls]
                prod = s * a_row[:, ls]
                sa_lo = jnp.sum(jnp.where(lo_half, prod, 0.0), axis=1, keepdims=True)
                sa_hi = jnp.sum(jnp.where(lo_half, 0.0, prod), axis=1, keepdims=True)
                sa = jnp.where(lo_half, sa_lo, sa_hi)
                s_new = s * w_row[:, ls] + sa * b_row[:, ls] + vc_ref[j * N:(j + 1) * N, ls] * k_row[:, ls]
                s_ref[:, ls] = s_new
                sr_ref[j * N:(j + 1) * N, ls] = (s_new * r_row[:, ls]).astype(BF16)
        for p in range(RWKV_PAIRS):
            ls = slice(p * LANES, (p + 1) * LANES)
            ycol = jnp.dot(sr_ref[:, ls], same_head, preferred_element_type=F32)
            acc = jnp.zeros((N, LANES), F32)
            for j in range(GT):
                t = g8 * GT + jnp.where(fwd, j, GT - 1 - j)
                acc = jnp.where((lane % N) == t, ycol[j * N:(j + 1) * N], acc)
            y_ref[0, 0, 0, :, ls] = jnp.where((lane % N) // GT == g8, acc, y_ref[0, 0, 0, :, ls])
        return carry
    lax.fori_loop(0, TB // GT, group, 0)

    @pl.when(c == n_blk - 1)
    def _():
        sfin_ref[0, 0] = s_ref[...]


def rwkv_scan(r, v, a_neg, decay, k_d, b_d, s0):
    bsz, L, W = r.shape
    TB, N, H = RWKV_TBLK, RWKV_HEAD, RWKV_HEADS
    n = L // TB
    gw = RWKV_PAIRS * LANES
    s0t = s0.transpose(0, 1, 3, 2, 4).reshape(bsz, N_DIR, N, W)

    def blk(d, c):
        return c + d * (n - 1 - 2 * c)
    seq = pl.BlockSpec((1, TB, gw), lambda b, g, d, c: (b, blk(d, c), g))
    seq_d = pl.BlockSpec((1, 1, TB, gw), lambda b, g, d, c: (d, b, blk(d, c), g))
    st = pl.BlockSpec((1, 1, N, gw), lambda b, g, d, c: (b, d, 0, g))
    gt = RWKV_GROUP_T
    yt, sfin = pl.pallas_call(
        functools.partial(_rwkv_kernel, n),
        grid=(bsz, W // gw, N_DIR, n),
        in_specs=[seq, seq, seq, seq_d, seq_d, seq_d, st],
        out_specs=[pl.BlockSpec((1, 1, 1, N, gw), lambda b, g, d, c: (d, b, blk(d, c), 0, g)), st],
        out_shape=[jax.ShapeDtypeStruct((N_DIR, bsz, n, N, W), F32),
                   jax.ShapeDtypeStruct((bsz, N_DIR, N, W), F32)],
        scratch_shapes=[pltpu.VMEM((N, gw), F32), pltpu.VMEM((gt * N, gw), F32), pltpu.VMEM((gt * N, gw), BF16)],
        compiler_params=pltpu.CompilerParams(
            dimension_semantics=("arbitrary",) * 4, vmem_limit_bytes=VMEM_LIMIT),
        name="rwkv_scan",
    )(r, v, a_neg, decay, k_d, b_d, s0t)
    yt = yt[0] + yt[1]
    wkv = yt.reshape(bsz, n, N, W // LANES, 2, TB).transpose(0, 1, 5, 3, 4, 2).reshape(bsz, L, W)
    sfin = sfin.reshape(bsz, N_DIR, N, H, N).transpose(0, 1, 3, 2, 4)
    return wkv, sfin


def _split_bf16(x):
    hi = x.astype(BF16)
    return hi, (x - hi.astype(F32)).astype(BF16)


def _mxu(x, y, dims=(((1,), (0,)), ((), ())), split=False):
    def d(a, b):
        return lax.dot_general(a, b, dims, preferred_element_type=F32)
    if not split:
        return d(x.astype(BF16), y.astype(BF16))
    xh, xl = _split_bf16(x)
    yh, yl = _split_bf16(y)
    return d(xh, yh) + (d(xh, yl) + d(xl, yh))


def _rwkv_chunk_kernel(n_chunks, r_ref, lw_ref, k_ref, v_ref, a_ref, b_ref, h0_ref, y_ref, hfin_ref, h_ref):
    T, N = RWKV_CHUNK, RWKV_HEAD
    d = pl.program_id(2)
    c = pl.program_id(3)

    @pl.when(c == 0)
    def _():
        h_ref[...] = h0_ref[0, 0]

    lane = lax.broadcasted_iota(jnp.int32, (T, LANES), 1)
    row = lax.broadcasted_iota(jnp.int32, (T, LANES), 0)
    lo = lane < N
    col = lane % N
    order = jnp.where(d == 0, row - col, col - row)
    seen = order >= 0
    before = order > 0
    eye = row == col
    sq_r = lax.broadcasted_iota(jnp.int32, (T, T), 0)
    sq_c = lax.broadcasted_iota(jnp.int32, (T, T), 1)
    seen_sq = (jnp.where(d == 0, sq_r - sq_c, sq_c - sq_r) >= 0).astype(F32)
    row_dims = (((0,), (0,)), ((), ()))
    lane_dims = (((1,), (1,)), ((), ()))

    def bd(x):
        return jnp.concatenate([jnp.where(lo, x, 0.0), jnp.where(lo, 0.0, x)], axis=0)

    def pp(x, y, split=False):
        return _mxu(x, bd(y), split=split)

    def ptp(x, y):
        full = _mxu(x, y, row_dims, split=True)
        return jnp.where(lo, full[:N], full[N:])

    lw_all = lw_ref[0, 0]
    cs_all = jnp.dot(seen_sq, lw_all, precision=HI, preferred_element_type=F32)
    tot_all = jnp.sum(lw_all, axis=0, keepdims=True)
    pairs = range(RWKV_CPAIRS)
    sl = [slice(p * LANES, (p + 1) * LANES) for p in pairs]
    cs = [cs_all[:, s] for s in sl]
    tot = [tot_all[:, s] for s in sl]
    e_out = [jnp.exp(-cs[p]) for p in pairs]
    at = [a_ref[0, :, sl[p]] * jnp.exp(cs[p] - lw_all[:, sl[p]]) for p in pairs]
    rt = [r_ref[0, :, sl[p]] * jnp.exp(cs[p]) for p in pairs]
    ar = [jnp.concatenate([at[p], rt[p]], axis=0) for p in pairs]
    g1 = [_mxu(ar[p], bd(b_ref[0, 0, :, sl[p]] * e_out[p]), lane_dims) for p in pairs]
    g2 = [_mxu(ar[p], bd(k_ref[0, 0, :, sl[p]] * e_out[p]), lane_dims) for p in pairs]
    a_ab = [jnp.where(before, g1[p][:T], 0.0) for p in pairs]
    a_rb = [jnp.where(seen, g1[p][T:], 0.0) for p in pairs]
    a_ak = [jnp.where(before, g2[p][:T], 0.0) for p in pairs]
    a_rk = [jnp.where(seen, g2[p][T:], 0.0) for p in pairs]
    w = [jnp.where(eye, 1.0, a_ab[p]) for p in pairs]
    apow = a_ab
    for _ in range(5):
        apow = [pp(apow[p], apow[p]) for p in pairs]
        w = [w[p] + pp(w[p], apow[p]) for p in pairs]
    v = [v_ref[0, :, sl[p]] for p in pairs]
    akv = [pp(a_ak[p], v[p]) for p in pairs]
    u_loc = [pp(w[p], akv[p]) for p in pairs]
    a_hat = [pp(w[p], at[p]) for p in pairs]
    h0 = [h_ref[:, sl[p]] for p in pairs]
    q_hat = [rt[p] + pp(a_rb[p], a_hat[p]) for p in pairs]
    y_loc = [pp(a_rb[p], u_loc[p]) + pp(a_rk[p], v[p]) for p in pairs]
    for p in pairs:
        y_ref[0, 0, :, sl[p]] = pp(q_hat[p], h0[p]) + y_loc[p]
    e_end = [jnp.exp(tot[p] - cs[p]) for p in pairs]
    bh = [b_ref[0, 0, :, sl[p]] * e_end[p] for p in pairs]
    phi = [jnp.where(eye, jnp.exp(tot[p]), 0.0) + ptp(bh[p], a_hat[p]) for p in pairs]
    gam = [ptp(jnp.concatenate([bh[p], k_ref[0, 0, :, sl[p]] * e_end[p]], axis=0),
               jnp.concatenate([u_loc[p], v[p]], axis=0)) for p in pairs]
    for p in pairs:
        h_ref[:, sl[p]] = pp(phi[p], h0[p], split=True) + gam[p]

    @pl.when(c == n_chunks - 1)
    def _():
        hfin_ref[0, 0] = h_ref[...]


def rwkv_chunk_scan(r, v, a_neg, log_decay, k_d, b_d, s0):
    bsz, L, W = r.shape
    T, N, H = RWKV_CHUNK, RWKV_HEAD, RWKV_HEADS
    n = L // T
    gw = RWKV_CPAIRS * LANES
    h0 = s0.transpose(0, 1, 4, 2, 3).reshape(bsz, N_DIR, N, W)

    def blk(d, c):
        return c + d * (n - 1 - 2 * c)
    seq = pl.BlockSpec((1, T, gw), lambda b, g, d, c: (b, blk(d, c), g))
    seq_d = pl.BlockSpec((1, 1, T, gw), lambda b, g, d, c: (d, b, blk(d, c), g))
    st = pl.BlockSpec((1, 1, N, gw), lambda b, g, d, c: (b, d, 0, g))
    y, hfin = pl.pallas_call(
        functools.partial(_rwkv_chunk_kernel, n),
        grid=(bsz, W // gw, N_DIR, n),
        in_specs=[seq, seq_d, seq_d, seq, seq, seq_d, st],
        out_specs=[seq_d, st],
        out_shape=[jax.ShapeDtypeStruct((N_DIR, bsz, L, W), F32),
                   jax.ShapeDtypeStruct((bsz, N_DIR, N, W), F32)],
        scratch_shapes=[pltpu.VMEM((N, gw), F32)],
        compiler_params=pltpu.CompilerParams(
            dimension_semantics=("arbitrary",) * 4, vmem_limit_bytes=VMEM_LIMIT),
        name="rwkv_chunk_scan",
    )(r, log_decay, k_d, v, a_neg, b_d, h0)
    sfin = hfin.reshape(bsz, N_DIR, N, H, N).transpose(0, 1, 3, 4, 2)
    return y[0] + y[1], sfin


def _rwkv_fs_kernel(n_chunks, rev, r_ref, k_ref, v_ref, wp_ref, ap_ref, w0_ref, a0_ref, kk_ref, ka_ref, h0_ref,
                    y_ref, hfin_ref, h_ref):
    T, N, SB = RWKV_CHUNK, RWKV_HEAD, RWKV_SUB
    NB = T // SB
    c = pl.program_id(2)

    @pl.when(c == 0)
    def _():
        h_ref[...] = h0_ref[0]

    lane = lax.broadcasted_iota(jnp.int32, (T, LANES), 1)
    row = lax.broadcasted_iota(jnp.int32, (T, LANES), 0)
    lo = lane < N
    col = lane % N
    order = (col - row) if rev else (row - col)
    seen = order >= 0
    before = order > 0
    eye = row == col
    sq_r = lax.broadcasted_iota(jnp.int32, (T, T), 0)
    sq_c = lax.broadcasted_iota(jnp.int32, (T, T), 1)
    seen_sq = (((sq_c - sq_r) if rev else (sq_r - sq_c)) >= 0).astype(BF16)
    same_head = ((lax.broadcasted_iota(jnp.int32, (LANES, LANES), 0) < N)
                 == (lax.broadcasted_iota(jnp.int32, (LANES, LANES), 1) < N)).astype(BF16)
    col_sb =lax.broadcasted_iota(jnp.int32, (SB, LANES), 1) % N
    row_dims = (((0,), (0,)), ((), ()))
    lane_dims = (((1,), (1,)), ((), ()))

    def bd(x):
        return jnp.concatenate([jnp.where(lo, x, 0.0), jnp.where(lo, 0.0, x)], axis=0)

    def pp(x, y, split=False):
        return _mxu(x, bd(y), split=split)

    def ptp(x, y):
        full = _mxu(x, y, row_dims)
        return jnp.where(lo, full[:N], full[N:])

    w_log = -jax.nn.softplus(-(wp_ref[0] + w0_ref[...])) - 0.5
    lw_all = -jnp.exp(w_log)
    iclr_all = jax.nn.sigmoid(ap_ref[0] + a0_ref[...])
    k_all = k_ref[0]
    kd_all = k_all * (1.0 + (iclr_all - 1.0) * ka_ref[...])
    kkr_all = k_all * kk_ref[...]
    cs_all = _dot_mask(seen_sq, lw_all)
    tot_all = jnp.sum(lw_all, axis=0, keepdims=True)
    pairs = range(RWKV_CPAIRS)
    sl = [slice(p * LANES, (p + 1) * LANES) for p in pairs]
    sq_hi = [_split_bf16(kkr_all[:, s] * kkr_all[:, s]) for s in sl]
    ssq = [jnp.dot(sq_hi[p][0], same_head, preferred_element_type=F32)
           + jnp.dot(sq_hi[p][1], same_head, preferred_element_type=F32) for p in pairs]
    kk = [kkr_all[:, sl[p]] / jnp.maximum(jnp.sqrt(ssq[p]), 1e-12) for p in pairs]
    b_in = [kk[p] * iclr_all[:, sl[p]] for p in pairs]
    cs = [cs_all[:, s] for s in sl]
    tot = [tot_all[:, s] for s in sl]
    e_out = [jnp.exp(-cs[p]) for p in pairs]
    at = [-kk[p] * jnp.exp(cs[p] - lw_all[:, sl[p]]) for p in pairs]
    rt = [r_ref[0, :, sl[p]] * jnp.exp(cs[p]) for p in pairs]
    ar = [jnp.concatenate([at[p], rt[p]], axis=0) for p in pairs]
    g1 = [_mxu(ar[p], bd(b_in[p] * e_out[p]), lane_dims) for p in pairs]
    g2 = [_mxu(ar[p], bd(kd_all[:, sl[p]] * e_out[p]), lane_dims) for p in pairs]
    a_ab = [jnp.where(before, g1[p][:T], 0.0) for p in pairs]
    a_rb = [jnp.where(seen, g1[p][T:], 0.0) for p in pairs]
    a_ak = [jnp.where(before, g2[p][:T], 0.0) for p in pairs]
    a_rk = [jnp.where(seen, g2[p][T:], 0.0) for p in pairs]
    v = [v_ref[0, :, sl[p]] for p in pairs]
    akv = [pp(a_ak[p], v[p]) for p in pairs]
    za = [[None] * NB for _ in pairs]
    zu = [[None] * NB for _ in pairs]
    zero_blk = jnp.zeros((SB, LANES), F32)
    for kpos in range(NB):
        bk = NB - 1 - kpos if rev else kpos
        rows = slice(bk * SB, (bk + 1) * SB)
        done = [(m > bk) if rev else (m < bk) for m in range(NB)]
        cur_a = [at[p][rows] for p in pairs]
        cur_u = [akv[p][rows] for p in pairs]
        if kpos > 0:
            for p in pairs:
                zc_a = jnp.concatenate([za[p][m] if done[m] else zero_blk for m in range(NB)], axis=0)
                zc_u = jnp.concatenate([zu[p][m] if done[m] else zero_blk for m in range(NB)], axis=0)
                off = _mxu(a_ab[p][rows], jnp.concatenate([bd(zc_a), bd(zc_u)], axis=1))
                cur_a[p] = cur_a[p] + off[:, :LANES]
                cur_u[p] = cur_u[p] + off[:, LANES:]
        abc = []
        for p in pairs:
            ablk = a_ab[p][rows]
            picked = jnp.concatenate([jnp.where(col_sb == bk * SB + s, ablk, 0.0) for s in range(SB)], axis=0)
            abc.append(jnp.dot(picked.astype(BF16), same_head, preferred_element_type=F32))
        for j in range(SB - 1):
            s = SB - 1 - j if rev else j
            for p in pairs:
                coef = abc[p][s * SB:(s + 1) * SB]
                cur_a[p] = cur_a[p] + coef * cur_a[p][s:s + 1]
                cur_u[p] = cur_u[p] + coef * cur_u[p][s:s + 1]
        for p in pairs:
            za[p][bk] = cur_a[p]
            zu[p][bk] = cur_u[p]
    a_hat = [jnp.concatenate(za[p], axis=0) for p in pairs]
    u_loc = [jnp.concatenate(zu[p], axis=0) for p in pairs]
    h0 = [h_ref[:, sl[p]] for p in pairs]
    q_hat = [rt[p] + pp(a_rb[p], a_hat[p]) for p in pairs]
    y_loc = [pp(a_rb[p], u_loc[p]) + pp(a_rk[p], v[p]) for p in pairs]
    for p in pairs:
        y_ref[0, :, sl[p]] = pp(q_hat[p], h0[p]) + y_loc[p]
    e_end = [jnp.exp(tot[p] - cs[p]) for p in pairs]
    bh = [b_in[p] * e_end[p] for p in pairs]
    p_end = [_split_bf16(jnp.where(eye, jnp.exp(tot[p]), 0.0)) for p in pairs]
    decay = [jnp.dot(p_end[p][0], same_head, preferred_element_type=F32)
             + jnp.dot(p_end[p][1], same_head, preferred_element_type=F32) for p in pairs]
    corr = [ptp(bh[p], a_hat[p]) for p in pairs]
    gam = [ptp(jnp.concatenate([bh[p], kd_all[:, sl[p]] * e_end[p]], axis=0),
               jnp.concatenate([u_loc[p], v[p]], axis=0)) for p in pairs]
    for p in pairs:
        h_ref[:, sl[p]] = decay[p] * h0[p] + (pp(corr[p], h0[p], split=True) + gam[p])

    @pl.when(c == n_chunks - 1)
    def _():
        hfin_ref[0] = h_ref[...]


def rwkv_direction(rev, main, w_pre, a_pre, w0, a0, k_k, k_a, s0):
    bsz, L, _ = main.shape
    W = RWKV_W
    T, N, H = RWKV_CHUNK, RWKV_HEAD, RWKV_HEADS
    n = L // T
    gw = RWKV_CPAIRS * LANES
    ng = W // gw
    h0 = s0.transpose(0, 3, 1, 2).reshape(bsz, N, W)

    def seq(col0):
        return pl.BlockSpec((1, T, gw), lambda b, g, c: (b, (n - 1 - c) if rev else c, col0 * ng + g))
    vec = pl.BlockSpec((1, gw), lambda b, g, c: (0, g))
    st = pl.BlockSpec((1, N, gw), lambda b, g, c: (b, 0, g))
    y, hfin = pl.pallas_call(
        functools.partial(_rwkv_fs_kernel, n, rev),
        grid=(bsz, ng, n),
        in_specs=[seq(0), seq(1), seq(2), seq(0), seq(0), vec, vec, vec, vec, st],
        out_specs=[seq(0), st],
        out_shape=[jax.ShapeDtypeStruct((bsz, L, W), F32), jax.ShapeDtypeStruct((bsz, N, W), F32)],
        scratch_shapes=[pltpu.VMEM((N, gw), F32)],
        compiler_params=pltpu.CompilerParams(
            dimension_semantics=("arbitrary",) * 3, vmem_limit_bytes=VMEM_LIMIT),
        name="rwkv_bwd_chunks" if rev else "rwkv_fwd_chunks",
    )(main, main, main, w_pre, a_pre, w0.reshape(1, W), a0.reshape(1, W), k_k.reshape(1, W), k_a.reshape(1, W), h0)
    return y, hfin.reshape(bsz, N, H, N).transpose(0, 2, 3, 1)


def _segsum(x, same_head):
    x1, x2 = _split_bf16(x)
    return (jnp.dot(x1, same_head, preferred_element_type=F32)
            + jnp.dot(x2, same_head, preferred_element_type=F32))


def _rwkv_post_kernel(yf_ref, yb_ref, r_ref, k_ref, v_ref, g_ref, af_ref, ab_ref, a0_ref, ka_ref, rk_ref,
                      lw_ref, lb_ref, o_ref):
    N = RWKV_HEAD
    same_head = ((lax.broadcasted_iota(jnp.int32, (LANES, LANES), 0) < N)
                 == (lax.broadcasted_iota(jnp.int32, (LANES, LANES), 1) < N)).astype(BF16)
    for t in range(o_ref.shape[2] // LANES):
        ls = slice(t * LANES, (t + 1) * LANES)
        wkv = yf_ref[0, :, ls] + yb_ref[0, :, ls]
        mean = _segsum(wkv, same_head) * (1.0 / N)
        cen = wkv - mean
        var = _segsum(cen * cen, same_head) * (1.0 / N)
        ln = cen * lax.rsqrt(var + RWKV_LNX_EPS) * lw_ref[:, ls] + lb_ref[:, ls]
        ka = ka_ref[:, ls]
        k_mix = ((1.0 + (jax.nn.sigmoid(af_ref[0, :, ls] + a0_ref[0:1, ls]) - 1.0) * ka)
                 + (1.0 + (jax.nn.sigmoid(ab_ref[0, :, ls] + a0_ref[1:2, ls]) - 1.0) * ka))
        bonus = _segsum(r_ref[0, :, ls] * k_ref[0, :, ls] * k_mix * rk_ref[:, ls], same_head) * v_ref[0, :, ls]
        gate = g_ref[0, :, ls]
        o_ref[0, :, ls] = ((ln + bonus) * (gate * jax.nn.sigmoid(gate))).astype(o_ref.dtype)


def rwkv_post(y_f, y_b, main, a_pre_f, a_pre_b, a0, k_a, r_k, lnx_w, lnx_b):
    bsz, L, W = y_f.shape
    tr = _pick(L, (256, 128, 64))
    tw = 512
    nw = W // tw

    def seq(col0):
        return pl.BlockSpec((1, tr, tw), lambda b, i, j: (b, i, col0 * nw + j))
    vec = pl.BlockSpec((1, tw), lambda b, i, j: (0, j))
    vec2 = pl.BlockSpec((N_DIR, tw), lambda b, i, j: (0, j))
    return pl.pallas_call(
        _rwkv_post_kernel,
        grid=(bsz, L // tr, nw),
        in_specs=[seq(0), seq(0), seq(0), seq(1), seq(2), seq(3), seq(0), seq(0), vec2, vec, vec, vec, vec],
        out_specs=seq(0),
        out_shape=jax.ShapeDtypeStruct((bsz, L, W), BF16),
        compiler_params=pltpu.CompilerParams(
            dimension_semantics=("arbitrary",) * 3, vmem_limit_bytes=VMEM_LIMIT),
        name="rwkv_post",
    )(y_f, y_b, main, main, main, main, a_pre_f, a_pre_b, a0, k_a.reshape(1, W), r_k.reshape(1, W),
      lnx_w.reshape(1, W), lnx_b.reshape(1, W))


def _split_cols(t, sizes):
    offsets, acc = [], 0
    for s in sizes[:-1]:
        acc += s
        offsets.append(acc)
    return jnp.split(t, offsets, axis=-1)


def _rms(x, w):
    return x * lax.rsqrt(jnp.mean(x * x, axis=-1, keepdims=True) + EPS) * w


def _adaln_kernel(c_ref, w_ref, b_ref, o_ref):
    cond = c_ref[...]
    act = cond * jax.nn.sigmoid(cond)
    o_ref[...] = jnp.dot(act, w_ref[...], precision=HI, preferred_element_type=F32) + b_ref[...]


def adaln(cond, w, b):
    rows, dm = cond.shape
    n = w.shape[1]
    rp = -(-rows // 8) * 8
    tn = 512
    m = pl.pallas_call(
        _adaln_kernel,
        grid=(n // tn,),
        in_specs=[pl.BlockSpec((rp, dm), lambda j: (0, 0)),
                  pl.BlockSpec((dm, tn), lambda j: (0, j)),
                  pl.BlockSpec((1, tn), lambda j: (0, j))],
        out_specs=pl.BlockSpec((rp, tn), lambda j: (0, j)),
        out_shape=jax.ShapeDtypeStruct((rp, n), F32),
        compiler_params=pltpu.CompilerParams(dimension_semantics=("arbitrary",), vmem_limit_bytes=VMEM_LIMIT),
        name="adaln",
    )(jnp.pad(cond, ((0, rp - rows), (0, 0))), w, b.reshape(1, n))[:rows]
    return jnp.split(m, 3, axis=-1)


def _grid_pos_embed(n_tokens):
    rows = n_tokens // GRID_W
    row_id = jnp.broadcast_to(jnp.arange(rows, dtype=F32)[:, None], (rows, GRID_W)).reshape(-1)
    col_id = jnp.broadcast_to(jnp.arange(GRID_W, dtype=F32)[None, :], (rows, GRID_W)).reshape(-1)
    quarter = D_MODEL // 4
    omega = 1.0 / (POS_BASE ** (jnp.arange(quarter, dtype=F32) / quarter))

    def axis_emb(pos):
        ang = pos[:, None] * omega[None, :]
        return jnp.concatenate([jnp.sin(ang), jnp.cos(ang)], axis=-1)
    return jnp.concatenate([axis_emb(row_id), axis_emb(col_id)], axis=-1)


def _even_mixer(x, gate, h, s5_re0, s5_im0, gla0, w_in, w_out, s5_ops, glu_w, glu_b, dec_up, dec_b, gla_nw):
    bsz, L, _ = h.shape
    n_main = sum(EVEN_SIZES[:-1])
    main = _mm3(h, w_in[:, :n_main])
    w_tail = jnp.pad(w_in[:, n_main:], ((0, 0), (0, LANES - N_DIR * GLA_RANK)))
    dec_lr = _mm3(h, w_tail)
    gy, fin_re, fin_im = s5_scan(main[..., :S5_W], s5_ops, s5_re0, s5_im0)
    gy = gy.reshape(bsz * L, S5_W)
    mixed = matmul_glu(gy, glu_w, glu_b, main.reshape(bsz * L, n_main), S5_W, S5_W + GLA_DV_W)
    mixed, fin_gla = gla_mix(main, dec_lr, dec_up, dec_b, gla_nw, gla0, mixed.reshape(bsz, L, -1))
    return matmul_gated_residual(mixed, w_out, x, gate), fin_re, fin_im, fin_gla


def _odd_mixer(x, gate, h, rwkv0, w_in, w_out, mu, w0, w2, a0, a2, k_k, k_a, r_k, lnx_w, lnx_b):
    bsz, L, _ = h.shape
    zero = jnp.zeros_like(h[:, :1])
    h_prev = jnp.concatenate([zero, h[:, :-1]], axis=1)
    h_next = jnp.concatenate([h[:, 1:], zero], axis=1)
    xs = h + mu[0] * (h_prev - h) + mu[1] * (h_next - h)
    n_main = sum(ODD_SIZES[:4])
    main = _mm3(xs, w_in[:, :n_main])
    tail = _mm3(xs, w_in[:, n_main:])
    w_lr, a_lr = _split_cols(tail, ODD_SIZES[4:])
    w_lr = jnp.tanh(w_lr).reshape(bsz, L, N_DIR, RWKV_DECAY_RANK)
    a_lr = a_lr.reshape(bsz, L, N_DIR, RWKV_ICLR_RANK)
    ys, a_pres, finals = [], [], []
    for d in range(N_DIR):
        w_pre = _mm3(w_lr[:, :, d], w2[d])
        a_pre = _mm3(a_lr[:, :, d], a2[d])
        y_d, fin = rwkv_direction(bool(d), main, w_pre, a_pre, w0[d], a0[d], k_k, k_a, rwkv0[:, d])
        ys.append(y_d)
        a_pres.append(a_pre)
        finals.append(fin)
    out = rwkv_post(ys[0], ys[1], main, a_pres[0], a_pres[1], a0, k_a, r_k.reshape(-1), lnx_w, lnx_b)
    return matmul_gated_residual(out, w_out, x, gate), jnp.stack(finals, axis=1)


def kernel(x_prompt, x_sample, state_s5_re, state_s5_im, state_gla, state_rwkv, c, c_ctx, norm_w, ada_w, ada_b, final_norm_w, e_w_in, e_w_out, s5_lambda_re, s5_lambda_im, s5_log_step, s5_b_re, s5_b_im, s5_c_re, s5_c_im, s5_d, s5_glu_w, s5_glu_b, gla_decay_up, gla_decay_b, gla_norm_w, o_w_in, o_w_out, rwkv_mu, rwkv_w0, rwkv_w2, rwkv_a0, rwkv_a2, rwkv_k_k, rwkv_k_a, rwkv_r_k, rwkv_lnx_w, rwkv_lnx_b):
    bp = x_prompt.shape[0]
    depth = norm_w.shape[0]
    x_ctx = x_prompt
    x_lat = x_sample + _grid_pos_embed(x_sample.shape[1])[None]
    z_s5 = jnp.zeros((bp, N_DIR, S5_GROUPS, S5_STATE), F32)
    z_gla = jnp.zeros((bp, N_DIR, GLA_HEADS, GLA_DK, GLA_DV), F32)
    z_rwkv = jnp.zeros((bp, N_DIR, RWKV_HEADS, RWKV_HEAD, RWKV_HEAD), F32)
    new_s5_re, new_s5_im, new_gla, new_rwkv = [], [], [], []
    n_lat = c.shape[0]
    cond = jnp.concatenate([c, c_ctx[None]], axis=0)
    for i in range(depth):
        j = i // 2
        shift, scale, gate = adaln(cond, ada_w[i], ada_b[i])
        gt_l, gt_c = gate[:n_lat], jnp.broadcast_to(gate[n_lat:], (bp, D_MODEL))
        h_ctx = norm_mod(x_ctx, norm_w[i], scale[n_lat:], shift[n_lat:])
        h_lat = norm_mod(x_lat, norm_w[i], scale[:n_lat], shift[:n_lat])
        if i % 2 == 0:
            s5_ops = s5_operators(s5_lambda_re[j], s5_lambda_im[j], s5_log_step[j], s5_b_re[j], s5_b_im[j],
                                  s5_c_re[j], s5_c_im[j], s5_d[j])
            p = (e_w_in[j], e_w_out[j], s5_ops, s5_glu_w[j], s5_glu_b[j], gla_decay_up[j], gla_decay_b[j],
                 gla_norm_w[j])
            x_ctx, fr, fi, fg = _even_mixer(x_ctx, gt_c, h_ctx, z_s5, z_s5, z_gla, *p)
            x_lat, _, _, _ = _even_mixer(x_lat, gt_l, h_lat, state_s5_re[:, j], state_s5_im[:, j],
                                         state_gla[:, j], *p)
            new_s5_re.append(fr)
            new_s5_im.append(fi)
            new_gla.append(fg)
        else:
            p = (o_w_in[j], o_w_out[j], rwkv_mu[j], rwkv_w0[j], rwkv_w2[j], rwkv_a0[j], rwkv_a2[j],
                 rwkv_k_k[j], rwkv_k_a[j], rwkv_r_k[j], rwkv_lnx_w[j], rwkv_lnx_b[j])
            x_ctx, fw = _odd_mixer(x_ctx, gt_c, h_ctx, z_rwkv, *p)
            x_lat, _ = _odd_mixer(x_lat, gt_l, h_lat, state_rwkv[:, j], *p)
            new_rwkv.append(fw)
    y_prompt = final_norm(x_ctx, final_norm_w)
    y_sample = final_norm(x_lat, final_norm_w)
    return (y_prompt, y_sample, jnp.stack(new_s5_re, axis=1), jnp.stack(new_s5_im, axis=1),
            jnp.stack(new_gla, axis=1), jnp.stack(new_rwkv, axis=1))
```

```python
import functools

import jax
import jax.numpy as jnp
from jax import lax
from jax.experimental import pallas as pl
from jax.experimental.pallas import tpu as pltpu

D_MODEL = 2048
GRID_W = 64
POS_BASE = 10000.0
N_DIR = 2
EPS = 1e-6
S5_W = 1024
S5_GROUP_CH = 16
S5_GROUPS = 64
S5_STATE = 64
S5_CHUNK = 16
S5_TILE_GROUPS = 8
GLA_HEADS = 6
GLA_DV = 512
GLA_DK = 256
GLA_DK_W = 1536
GLA_DV_W = 3072
GLA_RANK = 16
GLA_NORMALIZER = 16.0
GLA_CHUNK = 64
GLA_NC = 16
GLA_LOG_DECAY_MIN = -1.0
EVEN_SIZES = (S5_W, S5_W, GLA_DK_W, GLA_DK_W, GLA_DV_W, GLA_DV_W, N_DIR * GLA_RANK)
RWKV_W = 2048
RWKV_HEAD = 64
RWKV_HEADS = 32
RWKV_DECAY_RANK = 96
RWKV_ICLR_RANK = 96
RWKV_LNX_EPS = 64e-5
ODD_SIZES = (RWKV_W, RWKV_W, RWKV_W, RWKV_W, N_DIR * RWKV_DECAY_RANK, N_DIR * RWKV_ICLR_RANK)
RWKV_CHUNK = 64
RWKV_CPAIRS = 16
RWKV_SUB = 16
LANES = 128

VMEM_LIMIT = 48 * 1024 * 1024
HI = lax.Precision.HIGHEST
BF16 = jnp.bfloat16
F32 = jnp.float32


def _mm_kernel(x_ref, w_ref, o_ref):
    o_ref[...] = jnp.dot(x_ref[...], w_ref[...], preferred_element_type=F32)


def _pick(n, prefs):
    for p in prefs:
        if n % p == 0:
            return p
    return n


def matmul(x, w):
    m, k = x.shape
    n = w.shape[1]
    x = x.astype(BF16)
    w = w.astype(BF16)
    tm = _pick(m, (1024, 512, 256, 128, 64, 32, 16, 8))
    tn = _pick(n, (1024, 512, 384, 256, 128))
    return pl.pallas_call(
        _mm_kernel,
        grid=(m // tm, n // tn),
        in_specs=[pl.BlockSpec((tm, k), lambda i, j: (i, 0)),
                  pl.BlockSpec((k, tn), lambda i, j: (0, j))],
        out_specs=pl.BlockSpec((tm, tn), lambda i, j: (i, j)),
        out_shape=jax.ShapeDtypeStruct((m, n), F32),
        compiler_params=pltpu.CompilerParams(
            dimension_semantics=("arbitrary", "arbitrary"), vmem_limit_bytes=VMEM_LIMIT),
        name="proj_matmul",
    )(x, w)


def _mm3(h, w):
    b, l, k = h.shape
    return matmul(h.reshape(b * l, k), w).reshape(b, l, -1)


def _mm_residual_kernel(x_ref, w_ref, res_ref, gate_ref, o_ref):
    acc = jnp.dot(x_ref[...], w_ref[...], preferred_element_type=F32)
    o_ref[...] = res_ref[...] + gate_ref[0] * acc


def matmul_gated_residual(x, w, res, gate):
    bsz, L, k = x.shape
    n = w.shape[1]
    m = bsz * L
    tm = _pick(L, (1024, 512, 256, 128))
    tn = _pick(n, (1024, 512, 256, 128) if k <= 2048 else (512, 256, 128))
    per_b = L // tm
    out = pl.pallas_call(
        _mm_residual_kernel,
        grid=(m // tm, n // tn),
        in_specs=[pl.BlockSpec((tm, k), lambda i, j: (i, 0)),
                  pl.BlockSpec((k, tn), lambda i, j: (0, j)),
                  pl.BlockSpec((tm, tn), lambda i, j: (i, j)),
                  pl.BlockSpec((1, 1, tn), lambda i, j: (i // per_b, 0, j))],
        out_specs=pl.BlockSpec((tm, tn), lambda i, j: (i, j)),
        out_shape=jax.ShapeDtypeStruct((m, n), F32),
        compiler_params=pltpu.CompilerParams(
            dimension_semantics=("arbitrary", "arbitrary"), vmem_limit_bytes=VMEM_LIMIT),
        name="proj_residual",
    )(x.reshape(m, k).astype(BF16), w.astype(BF16), res.reshape(m, n), gate.reshape(bsz, 1, n))
    return out.reshape(bsz, L, n)


def _mm_glu_kernel(x_ref, w_ref, b_ref, g_ref, xt_ref, o_ref):
    acc = jnp.dot(x_ref[...], w_ref[...], preferred_element_type=F32) + b_ref[...]
    gy = xt_ref[...].astype(F32)
    gate = g_ref[...]
    o_ref[...] = (gy * jax.nn.sigmoid(acc) * (gate * jax.nn.sigmoid(gate))).astype(o_ref.dtype)


def matmul_glu(gy, w, b, main, g_col0, n_total):
    m, k = gy.shape
    n = w.shape[1]
    tm = _pick(m, (1024, 512, 256, 128))
    tn = 512
    return pl.pallas_call(
        _mm_glu_kernel,
        grid=(m // tm, n // tn),
        in_specs=[pl.BlockSpec((tm, k), lambda i, j: (i, 0)),
                  pl.BlockSpec((k, tn), lambda i, j: (0, j)),
                  pl.BlockSpec((1, tn), lambda i, j: (0, j)),
                  pl.BlockSpec((tm, tn), lambda i, j: (i, g_col0 // tn + j)),
                  pl.BlockSpec((tm, tn), lambda i, j: (i, j))],
        out_specs=pl.BlockSpec((tm, tn), lambda i, j: (i, j)),
        out_shape=jax.ShapeDtypeStruct((m, n_total), BF16),
        compiler_params=pltpu.CompilerParams(
            dimension_semantics=("arbitrary", "arbitrary"), vmem_limit_bytes=VMEM_LIMIT),
        name="s5_glu_gate",
    )(gy, w.astype(BF16), b.reshape(1, n), main, gy)


def _norm_mod_kernel(x_ref, nw_ref, sc_ref, sh_ref, o_ref):
    x = x_ref[0]
    inv = lax.rsqrt(jnp.mean(x * x, axis=-1, keepdims=True) + EPS)
    o_ref[0] = (x * inv * nw_ref[...] * (1.0 + sc_ref[0]) + sh_ref[0]).astype(o_ref.dtype)


def norm_mod(x, nw, scale, shift):
    bsz, L, dm = x.shape
    tr = _pick(L, (256, 128, 64))
    nb = scale.shape[0]
    cond = pl.BlockSpec((1, 1, dm), lambda b, i: (b if nb > 1 else 0, 0, 0))
    return pl.pallas_call(
        _norm_mod_kernel,
        grid=(bsz, L // tr),
        in_specs=[pl.BlockSpec((1, tr, dm), lambda b, i: (b, i, 0)),
                  pl.BlockSpec((1, dm), lambda b, i: (0, 0)), cond, cond],
        out_specs=pl.BlockSpec((1, tr, dm), lambda b, i: (b, i, 0)),
        out_shape=jax.ShapeDtypeStruct((bsz, L, dm), BF16),
        compiler_params=pltpu.CompilerParams(
            dimension_semantics=("arbitrary", "arbitrary"), vmem_limit_bytes=VMEM_LIMIT),
        name="norm_mod",
    )(x, nw.reshape(1, dm), scale.reshape(nb, 1, dm), shift.reshape(nb, 1, dm))


def _final_norm_kernel(x_ref, nw_ref, o_ref):
    x = x_ref[0]
    o_ref[0] = x * lax.rsqrt(jnp.mean(x * x, axis=-1, keepdims=True) + EPS) * nw_ref[...]


def final_norm(x, nw):
    bsz, L, dm = x.shape
    tr = _pick(L, (256, 128, 64))
    return pl.pallas_call(
        _final_norm_kernel,
        grid=(bsz, L // tr),
        in_specs=[pl.BlockSpec((1, tr, dm), lambda b, i: (b, i, 0)), pl.BlockSpec((1, dm), lambda b, i: (0, 0))],
        out_specs=pl.BlockSpec((1, tr, dm), lambda b, i: (b, i, 0)),
        out_shape=jax.ShapeDtypeStruct((bsz, L, dm), F32),
        compiler_params=pltpu.CompilerParams(
            dimension_semantics=("arbitrary", "arbitrary"), vmem_limit_bytes=VMEM_LIMIT),
        name="final_norm",
    )(x, nw.reshape(1, dm))


def s5_operators(lam_re, lam_im, log_step, b_re, b_im, c_re, c_im, d_skip):
    T = S5_CHUNK
    dt = jnp.exp(log_step)[..., None]
    mag = jnp.exp(lam_re * dt)
    ab_re, ab_im = mag * jnp.cos(lam_im * dt), mag * jnp.sin(lam_im * dt)
    den = lam_re * lam_re + lam_im * lam_im
    f_re = ((ab_re - 1.0) * lam_re + ab_im * lam_im) / den
    f_im = (ab_im * lam_re - (ab_re - 1.0) * lam_im) / den
    bb_re = f_re[..., None] * b_re - f_im[..., None] * b_im
    bb_im = f_re[..., None] * b_im + f_im[..., None] * b_re
    kk = jnp.arange(T + 1, dtype=F32)[:, None, None, None]
    pmag = jnp.exp(kk * (lam_re * dt))
    pr = pmag * jnp.cos(kk * (lam_im * dt))
    pi = pmag * jnp.sin(kk * (lam_im * dt))
    zr = pr[:T, :, :, :, None] * bb_re - pi[:T, :, :, :, None] * bb_im
    zi = pr[:T, :, :, :, None] * bb_im + pi[:T, :, :, :, None] * bb_re
    kern = (jnp.einsum('dghp,kdgpj->kdghj', c_re, zr, precision=HI)
            - jnp.einsum('dghp,kdgpj->kdghj', c_im, zi, precision=HI))
    t_idx = jnp.arange(T)[:, None]
    s_idx = jnp.arange(T)[None, :]
    lag_f = t_idx - s_idx
    lag_b = s_idx - t_idx
    m_f = jnp.where((lag_f >= 0)[:, :, None, None, None], kern[:, 0][jnp.clip(lag_f, 0, T - 1)], 0.0)
    m_b = jnp.where((lag_b >= 0)[:, :, None, None, None], kern[:, 1][jnp.clip(lag_b, 0, T - 1)], 0.0)
    m = m_f + m_b
    eye_t = jnp.eye(T, dtype=F32)[:, :, None, None, None]
    eye_h = jnp.eye(S5_GROUP_CH, dtype=F32)[None, None, None]
    m = m + eye_t * eye_h * d_skip.reshape(S5_GROUPS, S5_GROUP_CH)[None, None, :, :, None]
    g = m.shape[2]
    m_t = m.transpose(2, 1, 4, 0, 3).reshape(g, T * S5_GROUP_CH, T * S5_GROUP_CH)
    pf_r, pf_i = pr[T - 1::-1][:T, 0], pi[T - 1::-1][:T, 0]
    pb_r, pb_i = pr[:T, 1], pi[:T, 1]

    def f_mat(p_r, p_i, d):
        re = p_r[..., None] * bb_re[d][None] - p_i[..., None] * bb_im[d][None]
        im = p_r[..., None] * bb_im[d][None] + p_i[..., None] * bb_re[d][None]
        re = re.transpose(1, 0, 3, 2).reshape(g, T * S5_GROUP_CH, S5_STATE)
        im = im.transpose(1, 0, 3, 2).reshape(g, T * S5_GROUP_CH, S5_STATE)
        return re, im
    ff_re, ff_im = f_mat(pf_r, pf_i, 0)
    fb_re, fb_im = f_mat(pb_r, pb_i, 1)
    a_t = jnp.concatenate([m_t, ff_re, fb_re, ff_im, fb_im], axis=-1)
    ef_r, ef_i = pr[1:T + 1, 0], pi[1:T + 1, 0]
    eb_r, eb_i = pr[T:0:-1, 1], pi[T:0:-1, 1]

    def e_mat(p_r, p_i, d):
        er = c_re[d][None] * p_r[:, :, None, :] - c_im[d][None] * p_i[:, :, None, :]
        ei = -(c_re[d][None] * p_i[:, :, None, :] + c_im[d][None] * p_r[:, :, None, :])
        er = er.transpose(1, 3, 0, 2).reshape(g, S5_STATE, T * S5_GROUP_CH)
        ei = ei.transpose(1, 3, 0, 2).reshape(g, S5_STATE, T * S5_GROUP_CH)
        return er, ei
    efr, efi = e_mat(ef_r, ef_i, 0)
    ebr, ebi = e_mat(eb_r, eb_i, 1)
    e_t = jnp.concatenate([efr, ebr, efi, ebi], axis=1)
    lam_t = jnp.concatenate([pr[T, 0], pr[T, 1], pi[T, 0], pi[T, 1]], axis=-1)[:, None, :]
    return a_t.astype(BF16), e_t.astype(BF16), lam_t


def _s5_kernel(n_chunks, bsz, x8_ref, sel_ref, at_ref, et_ref, lam_ref, h0_ref, y_ref, hfin_ref, z_ref, hent_ref):
    P = S5_STATE
    ut = jnp.dot(x8_ref[0], sel_ref[0], preferred_element_type=F32).astype(BF16)
    z_ref[...] = jnp.dot(ut, at_ref[0], preferred_element_type=F32)
    lam = lam_ref[0]
    a_re, a_im = lam[:, 0:2 * P], lam[:, 2 * P:4 * P]
    h0 = h0_ref[0]
    fwd_lanes = lax.broadcasted_iota(jnp.int32, (bsz, 2 * P), 1) < P

    def step(c, carry):
        h_re, h_im = carry
        rf = pl.ds(pl.multiple_of(c * bsz, 8), bsz)
        rb = pl.ds(pl.multiple_of((n_chunks - 1 - c) * bsz, 8), bsz)
        hent_ref[rf, 0:P] = h_re[:, 0:P]
        hent_ref[rb, P:2 * P] = h_re[:, P:2 * P]
        hent_ref[rf, 2 * P:3 * P] = h_im[:, 0:P]
        hent_ref[rb, 3 * P:4 * P] = h_im[:, P:2 * P]
        g_re = jnp.where(fwd_lanes, z_ref[rf, 4 * P:6 * P], z_ref[rb, 4 * P:6 * P])
        g_im = jnp.where(fwd_lanes, z_ref[rf, 6 * P:8 * P], z_ref[rb, 6 * P:8 * P])
        return a_re * h_re - a_im * h_im + g_re, a_re * h_im + a_im * h_re + g_im
    h_re, h_im = lax.fori_loop(0, n_chunks, step, (h0[:, 0:2 * P], h0[:, 2 * P:4 * P]))
    hfin_ref[0, :, 0:2 * P] = h_re
    hfin_ref[0, :, 2 * P:4 * P] = h_im
    y = z_ref[:, 0:4 * P] + jnp.dot(hent_ref[...].astype(BF16), et_ref[0], preferred_element_type=F32)
    y_ref[0] = jax.nn.gelu(y).astype(y_ref.dtype)


def _s5_unpack_kernel(yt_ref, selt_ref, o_ref):
    acc = jnp.dot(yt_ref[0], selt_ref[0], preferred_element_type=F32)
    for gl in range(1, S5_TILE_GROUPS):
        acc = acc + jnp.dot(yt_ref[gl], selt_ref[gl], preferred_element_type=F32)
    o_ref[0] = acc.astype(o_ref.dtype)


def s5_scan(u, ops, h0_re, h0_im):
    a_t, e_t, lam_t = ops
    b_real, L, _ = u.shape
    T, G, H, P = S5_CHUNK, S5_GROUPS, S5_GROUP_CH, S5_STATE
    TG = S5_TILE_GROUPS
    n = L // T
    bsz = -(-b_real // 8) * 8
    cols = n * bsz
    x8 = u.reshape(b_real, n, T, G // TG, LANES).transpose(3, 1, 0, 2, 4).astype(BF16)
    x8 = jnp.pad(x8, ((0, 0), (0, 0), (0, bsz - b_real), (0, 0), (0, 0))).reshape(G // TG, cols, T * LANES)
    src = jnp.arange(T * LANES)
    dst = jnp.arange(T * H)
    sel = ((src[None, :, None] // LANES == dst[None, None, :] // H)
           & (src[None, :, None] % H == dst[None, None, :] % H)
           & ((src[None, :, None] % LANES) // H == jnp.arange(TG)[:, None, None])).astype(BF16)
    h0 = jnp.concatenate([h0_re[:, 0], h0_re[:, 1], h0_im[:, 0], h0_im[:, 1]], axis=-1)
    h0 = jnp.pad(h0.transpose(1, 0, 2), ((0, 0), (0, bsz - b_real), (0, 0)))
    yt, hfin = pl.pallas_call(
        functools.partial(_s5_kernel, n, bsz),
        grid=(G,),
        in_specs=[pl.BlockSpec((1, cols, T * LANES), lambda g: (g // TG, 0, 0)),
                  pl.BlockSpec((1, T * LANES, T * H), lambda g: (g % TG, 0, 0)),
                  pl.BlockSpec((1, T * H, 8 * P), lambda g: (g, 0, 0)),
                  pl.BlockSpec((1, 4 * P, T * H), lambda g: (g, 0, 0)),
                  pl.BlockSpec((1, 1, 4 * P), lambda g: (g, 0, 0)),
                  pl.BlockSpec((1, bsz, 4 * P), lambda g: (g, 0, 0))],
        out_specs=[pl.BlockSpec((1, cols, T * H), lambda g: (g, 0, 0)),
                   pl.BlockSpec((1, bsz, 4 * P), lambda g: (g, 0, 0))],
        out_shape=[jax.ShapeDtypeStruct((G, cols, T * H), BF16),
                   jax.ShapeDtypeStruct((G, bsz, 4 * P), F32)],
        scratch_shapes=[pltpu.VMEM((cols, 8 * P), F32), pltpu.VMEM((cols, 4 * P), F32)],
        compiler_params=pltpu.CompilerParams(dimension_semantics=("arbitrary",), vmem_limit_bytes=VMEM_LIMIT),
        name="s5_chunk_scan",
    )(x8, sel, a_t, e_t, lam_t, h0)
    tr = _pick(cols, (512, 256, 128))
    y8 = pl.pallas_call(
        _s5_unpack_kernel,
        grid=(G // TG, cols // tr),
        in_specs=[pl.BlockSpec((TG, tr, T * H), lambda t, i: (t, i, 0)),
                  pl.BlockSpec((TG, T * H, T * LANES), lambda t, i: (0, 0, 0))],
        out_specs=pl.BlockSpec((1, tr, T * LANES), lambda t, i: (t, i, 0)),
        out_shape=jax.ShapeDtypeStruct((G // TG, cols, T * LANES), BF16),
        compiler_params=pltpu.CompilerParams(
            dimension_semantics=("arbitrary", "arbitrary"), vmem_limit_bytes=VMEM_LIMIT),
        name="s5_unpack",
    )(yt, sel.transpose(0, 2, 1))
    y = y8.reshape(G // TG, n, bsz, T, LANES)[:, :, :b_real].transpose(2, 1, 3, 0, 4).reshape(b_real, L, G * H)
    hfin = hfin[:, :b_real].transpose(1, 0, 2)
    fin_re = jnp.stack([hfin[..., 0:P], hfin[..., P:2 * P]], axis=1)
    fin_im = jnp.stack([hfin[..., 2 * P:3 * P], hfin[..., 3 * P:4 * P]], axis=1)
    return y, fin_re, fin_im


def _dot_t(a, b):
    return lax.dot_general(a, b, (((1,), (1,)), ((), ())), preferred_element_type=F32)


def _dot_mask(mask_bf16, x, x_rows_to_sublanes=False):
    def d(b):
        if x_rows_to_sublanes:
            return lax.dot_general(b, mask_bf16, (((0,), (0,)), ((), ())), preferred_element_type=F32)
        return jnp.dot(mask_bf16, b, preferred_element_type=F32)
    x1 = x.astype(BF16)
    r1 = x - x1.astype(F32)
    x2 = r1.astype(BF16)
    x3 = (r1 - x2.astype(F32)).astype(BF16)
    return d(x1) + (d(x2) + d(x3))


def _gla_block_kernel(n_blocks, NC, has_s0, q_ref, k_ref, v_ref, g_ref, lr_ref, up_ref, db_ref, nw_ref, dst_ref,
                      *refs):
    s0_ref = refs[0] if has_s0 else None
    out_ref, sfin_ref, s_ref, of_ref, qd_ref, ov_ref, kv_ref, dc_ref = refs[1:] if has_s0 else refs
    C = GLA_CHUNK
    R = C * NC
    d = pl.program_id(2)
    c = pl.program_id(3)
    bidx = jnp.where(d == 0, c, n_blocks - 1 - c)

    @pl.when(c == 0)
    def _():
        s_ref[...] = s0_ref[0, 0, 0] if has_s0 else jnp.zeros_like(s_ref)

    z = _mxu(lr_ref[0], up_ref[0], split=True) + db_ref[0]
    gc = jnp.maximum(jax.nn.log_sigmoid(z) * (1.0 / GLA_NORMALIZER), GLA_LOG_DECAY_MIN)
    row_c = lax.broadcasted_iota(jnp.int32, (C, C), 0)
    col_c = lax.broadcasted_iota(jnp.int32, (C, C), 1)
    seen_c = jnp.where(d == 0, row_c - col_c, col_c - row_c) >= 0
    seen_bf = seen_c.astype(BF16)
    rs = [slice(i * C, (i + 1) * C) for i in range(NC)]
    bcum_c = [_dot_mask(seen_bf, gc[r]) for r in rs]
    btot_c = [jnp.broadcast_to(jnp.where(d == 0, b[C - 1:C], b[0:1]), (C, GLA_DK)) for b in bcum_c]
    bcum = jnp.concatenate(bcum_c, axis=0)
    btot = jnp.concatenate(btot_c, axis=0)
    q_dec = (q_ref[0] * (GLA_DK ** -0.5) * jnp.exp(bcum)).astype(BF16)
    k = k_ref[0]
    k_inv = (k * jnp.exp(-bcum)).astype(BF16)
    k_end = (k * jnp.exp(btot - bcum)).astype(BF16)
    v = v_ref[0].astype(BF16)
    ones_c = jnp.ones((C, LANES), BF16)
    qd_ref[...] = q_dec.reshape(NC, C, GLA_DK)
    att = [jnp.where(seen_c, _dot_t(q_dec[r], k_inv[r]), 0.0).astype(BF16) for r in rs]
    for i in range(NC):
        kv_ref[i] = lax.dot_general(k_end[rs[i]], v[rs[i]], (((0,), (0,)), ((), ())), preferred_element_type=F32)
    for i in range(NC):
        ov_ref[i] = jnp.dot(att[i], v[rs[i]], preferred_element_type=F32)
    for i in range(NC):
        dc_ref[i] = _dot_mask(ones_c, gc[rs[i]], x_rows_to_sublanes=True)

    for i in range(NC):
        ci = jnp.where(d == 0, i, NC - 1 - i)
        s_old = s_ref[...]
        o = ov_ref[ci] + jnp.dot(qd_ref[ci], s_old.astype(BF16), preferred_element_type=F32)
        s_ref[...] = jnp.exp(dc_ref[ci][:, 0:1]) * s_old + kv_ref[ci]
        rows = pl.ds(pl.multiple_of(bidx * R + ci * C, C), C)
        orow = pl.ds(pl.multiple_of(ci * C, C), C)

        @pl.when(d == 0)
        def _():
            of_ref[rows, :] = o

        @pl.when(d == 1)
        def _():
            tot = of_ref[rows, :] + o
            nrm = tot * lax.rsqrt(jnp.mean(tot * tot, axis=-1, keepdims=True) + EPS) * nw_ref[0]
            gate = g_ref[0, orow, :]
            out_ref[0, orow, :] = (nrm * (gate * jax.nn.sigmoid(gate))).astype(out_ref.dtype)

    @pl.when(c == n_blocks - 1)
    def _():
        sfin_ref[0, 0, 0] = s_ref[...]


def gla_mix(main, dec_lr, dec_up, dec_b, gla_nw, s0, dst):
    bsz, L, _ = main.shape
    H, DK, DV = GLA_HEADS, GLA_DK, GLA_DV
    nc = min(GLA_NC, L // GLA_CHUNK)
    C = GLA_CHUNK * nc
    n = L // C
    q_blk = sum(EVEN_SIZES[:2]) // DK
    k_blk = sum(EVEN_SIZES[:3]) // DK
    v_blk = sum(EVEN_SIZES[:4]) // DV
    g_blk = sum(EVEN_SIZES[:5]) // DV
    up = jnp.zeros((N_DIR, LANES, GLA_DK_W), F32)
    for d in range(N_DIR):
        up = up.at[d, d * GLA_RANK:(d + 1) * GLA_RANK].set(dec_up[d])
    db = dec_b.reshape(N_DIR, 1, GLA_DK_W)
    nw = gla_nw.reshape(1, GLA_DV_W)

    def chunk(d, c):
        return c + d * (n - 1 - 2 * c)

    def out_chunk(d, c):
        return (n - 1) - d * c
    state = pl.BlockSpec((1, 1, 1, DK, DV), lambda b, h, d, c: (b, d, h, 0, 0))
    has_s0 = s0 is not None
    out, sfin = pl.pallas_call(
        functools.partial(_gla_block_kernel, n, nc, has_s0),
        grid=(bsz, H, N_DIR, n),
        in_specs=[pl.BlockSpec((1, C, DK), lambda b, h, d, c: (b, chunk(d, c), q_blk + h)),
                  pl.BlockSpec((1, C, DK), lambda b, h, d, c: (b, chunk(d, c), k_blk + h)),
                  pl.BlockSpec((1, C, DV), lambda b, h, d, c: (b, chunk(d, c), v_blk + h)),
                  pl.BlockSpec((1, C, DV), lambda b, h, d, c: (b, chunk(d, c), g_blk + h)),
                  pl.BlockSpec((1, C, LANES), lambda b, h, d, c: (b, chunk(d, c), 0)),
                  pl.BlockSpec((1, LANES, DK), lambda b, h, d, c: (d, 0, h)),
                  pl.BlockSpec((1, 1, DK), lambda b, h, d, c: (d, 0, h)),
                  pl.BlockSpec((1, DV), lambda b, h, d, c: (0, h)),
                  pl.BlockSpec(memory_space=pl.ANY)] + ([state] if has_s0 else []),
        input_output_aliases={8: 0},
        out_specs=[pl.BlockSpec((1, C, DV), lambda b, h, d, c: (b, out_chunk(d, c), S5_W // DV + h)), state],
        out_shape=[jax.ShapeDtypeStruct(dst.shape, BF16),
                   jax.ShapeDtypeStruct((bsz, N_DIR, H, DK, DV), F32)],
        scratch_shapes=[pltpu.VMEM((DK, DV), F32), pltpu.VMEM((L, DV), F32),
                        pltpu.VMEM((nc, GLA_CHUNK, DK), BF16), pltpu.VMEM((nc, GLA_CHUNK, DV), F32),
                        pltpu.VMEM((nc, DK, DV), F32), pltpu.VMEM((nc, DK, LANES), F32)],
        compiler_params=pltpu.CompilerParams(
            dimension_semantics=("arbitrary",) * 4, vmem_limit_bytes=VMEM_LIMIT),
        name="gla_chunk_scan",
    )(main, main, main, main, dec_lr, up, db, nw, dst, *([s0] if has_s0 else []))
    return out, sfin


def _split_bf16(x):
    hi = x.astype(BF16)
    return hi, (x - hi.astype(F32)).astype(BF16)


def _mxu(x, y, dims=(((1,), (0,)), ((), ())), split=False):
    def d(a, b):
        return lax.dot_general(a, b, dims, preferred_element_type=F32)
    if not split:
        return d(x.astype(BF16), y.astype(BF16))
    xh, xl = _split_bf16(x)
    yh, yl = _split_bf16(y)
    return d(xh, yh) + (d(xh, yl) + d(xl, yh))


def _rwkv_fs_kernel(n_chunks, rev, r_ref, k_ref, v_ref, wp_ref, ap_ref, w0_ref, a0_ref, kk_ref, ka_ref, h0_ref,
                    y_ref, hfin_ref, h_ref):
    T, N, SB = RWKV_CHUNK, RWKV_HEAD, RWKV_SUB
    NB = T // SB
    c = pl.program_id(2)

    @pl.when(c == 0)
    def _():
        h_ref[...] = h0_ref[0]

    lane = lax.broadcasted_iota(jnp.int32, (T, LANES), 1)
    row = lax.broadcasted_iota(jnp.int32, (T, LANES), 0)
    lo = lane < N
    col = lane % N
    order = (col - row) if rev else (row - col)
    seen = order >= 0
    before = order > 0
    eye = row == col
    sq_r = lax.broadcasted_iota(jnp.int32, (T, T), 0)
    sq_c = lax.broadcasted_iota(jnp.int32, (T, T), 1)
    seen_sq = (((sq_c - sq_r) if rev else (sq_r - sq_c)) >= 0).astype(BF16)
    same_head = ((lax.broadcasted_iota(jnp.int32, (LANES, LANES), 0) < N)
                 == (lax.broadcasted_iota(jnp.int32, (LANES, LANES), 1) < N)).astype(BF16)
    col_sb = lax.broadcasted_iota(jnp.int32, (SB, LANES), 1) % N
    row_dims = (((0,), (0,)), ((), ()))
    lane_dims = (((1,), (1,)), ((), ()))

    def bd(x):
        return jnp.concatenate([jnp.where(lo, x, 0.0), jnp.where(lo, 0.0, x)], axis=0)

    def pp(x, y, split=False):
        return _mxu(x, bd(y), split=split)

    def ptp(x, y):
        full = _mxu(x, y, row_dims)
        return jnp.where(lo, full[:N], full[N:])

    w_log = -jax.nn.softplus(-(wp_ref[0] + w0_ref[...])) - 0.5
    lw_all = -jnp.exp(w_log)
    iclr_all = jax.nn.sigmoid(ap_ref[0] + a0_ref[...])
    k_all = k_ref[0]
    kd_all = k_all * (1.0 + (iclr_all - 1.0) * ka_ref[...])
    kkr_all = k_all * kk_ref[...]
    cs_all = _dot_mask(seen_sq, lw_all)
    tot_all = jnp.sum(lw_all, axis=0, keepdims=True)
    pairs = range(RWKV_CPAIRS)
    sl = [slice(p * LANES, (p + 1) * LANES) for p in pairs]
    sq_hi = [_split_bf16(kkr_all[:, s] * kkr_all[:, s]) for s in sl]
    ssq = [jnp.dot(sq_hi[p][0], same_head, preferred_element_type=F32)
           + jnp.dot(sq_hi[p][1], same_head, preferred_element_type=F32) for p in pairs]
    kk = [kkr_all[:, sl[p]] / jnp.maximum(jnp.sqrt(ssq[p]), 1e-12) for p in pairs]
    b_in = [kk[p] * iclr_all[:, sl[p]] for p in pairs]
    cs = [cs_all[:, s] for s in sl]
    tot = [tot_all[:, s] for s in sl]
    e_out = [jnp.exp(-cs[p]) for p in pairs]
    at = [-kk[p] * jnp.exp(cs[p] - lw_all[:, sl[p]]) for p in pairs]
    rt = [r_ref[0, :, sl[p]] * jnp.exp(cs[p]) for p in pairs]
    ar = [jnp.concatenate([at[p], rt[p]], axis=0) for p in pairs]
    g1 = [_mxu(ar[p], bd(b_in[p] * e_out[p]), lane_dims) for p in pairs]
    g2 = [_mxu(ar[p], bd(kd_all[:, sl[p]] * e_out[p]), lane_dims) for p in pairs]
    a_ab = [jnp.where(before, g1[p][:T], 0.0) for p in pairs]
    a_rb = [jnp.where(seen, g1[p][T:], 0.0) for p in pairs]
    a_ak = [jnp.where(before, g2[p][:T], 0.0) for p in pairs]
    a_rk = [jnp.where(seen, g2[p][T:], 0.0) for p in pairs]
    v = [v_ref[0, :, sl[p]] for p in pairs]
    akv = [pp(a_ak[p], v[p]) for p in pairs]
    za = [[None] * NB for _ in pairs]
    zu = [[None] * NB for _ in pairs]
    zero_blk = jnp.zeros((SB, LANES), F32)
    for kpos in range(NB):
        bk = NB - 1 - kpos if rev else kpos
        rows = slice(bk * SB, (bk + 1) * SB)
        done = [(m > bk) if rev else (m < bk) for m in range(NB)]
        cur_a = [at[p][rows] for p in pairs]
        cur_u = [akv[p][rows] for p in pairs]
        if kpos > 0:
            for p in pairs:
                zc_a = jnp.concatenate([za[p][m] if done[m] else zero_blk for m in range(NB)], axis=0)
                zc_u = jnp.concatenate([zu[p][m] if done[m] else zero_blk for m in range(NB)], axis=0)
                off = _mxu(a_ab[p][rows], jnp.concatenate([bd(zc_a), bd(zc_u)], axis=1))
                cur_a[p] = cur_a[p] + off[:, :LANES]
                cur_u[p] = cur_u[p] + off[:, LANES:]
        abc = []
        for p in pairs:
            ablk = a_ab[p][rows]
            picked = jnp.concatenate([jnp.where(col_sb == bk * SB + s, ablk, 0.0) for s in range(SB)], axis=0)
            abc.append(jnp.dot(picked.astype(BF16), same_head, preferred_element_type=F32))
        ha = [[cur_a[p][:8], cur_a[p][8:]] for p in pairs]
        hu = [[cur_u[p][:8], cur_u[p][8:]] for p in pairs]
        for j in range(SB - 1):
            s = SB - 1 - j if rev else j
            src, r8 = s // 8, s % 8
            halves = (0, 1) if (s >= 8) == rev else ((0,) if rev else (1,))
            for p in pairs:
                row_a = ha[p][src][r8:r8 + 1]
                row_u = hu[p][src][r8:r8 + 1]
                for hf in halves:
                    coef = abc[p][s * SB + hf * 8:s * SB + hf * 8 + 8]
                    ha[p][hf] = ha[p][hf] + coef * row_a
                    hu[p][hf] = hu[p][hf] + coef * row_u
        for p in pairs:
            za[p][bk] = jnp.concatenate(ha[p], axis=0)
            zu[p][bk] = jnp.concatenate(hu[p], axis=0)
    a_hat = [jnp.concatenate(za[p], axis=0) for p in pairs]
    u_loc = [jnp.concatenate(zu[p], axis=0) for p in pairs]
    h0 = [h_ref[:, sl[p]] for p in pairs]
    q_hat = [rt[p] + pp(a_rb[p], a_hat[p]) for p in pairs]
    y_loc = [pp(a_rb[p], u_loc[p]) + pp(a_rk[p], v[p]) for p in pairs]
    for p in pairs:
        y_ref[0, :, sl[p]] = pp(q_hat[p], h0[p]) + y_loc[p]
    e_end = [jnp.exp(tot[p] - cs[p]) for p in pairs]
    bh = [b_in[p] * e_end[p] for p in pairs]
    p_end = [_split_bf16(jnp.where(eye, jnp.exp(tot[p]), 0.0)) for p in pairs]
    decay = [jnp.dot(p_end[p][0], same_head, preferred_element_type=F32)
             + jnp.dot(p_end[p][1], same_head, preferred_element_type=F32) for p in pairs]
    corr = [ptp(bh[p], a_hat[p]) for p in pairs]
    gam = [ptp(jnp.concatenate([bh[p], kd_all[:, sl[p]] * e_end[p]], axis=0),
               jnp.concatenate([u_loc[p], v[p]], axis=0)) for p in pairs]
    for p in pairs:
        h_ref[:, sl[p]] = decay[p] * h0[p] + (pp(corr[p], h0[p], split=True) + gam[p])

    @pl.when(c == n_chunks - 1)
    def _():
        hfin_ref[0] = h_ref[...]


def rwkv_direction(rev, main, w_pre, a_pre, w0, a0, k_k, k_a, s0):
    bsz, L, _ = main.shape
    W = RWKV_W
    T, N, H = RWKV_CHUNK, RWKV_HEAD, RWKV_HEADS
    n = L // T
    gw = RWKV_CPAIRS * LANES
    ng = W // gw
    h0 = s0.transpose(0, 3, 1, 2).reshape(bsz, N, W)

    def seq(col0):
        return pl.BlockSpec((1, T, gw), lambda b, g, c: (b, (n - 1 - c) if rev else c, col0 * ng + g))
    vec = pl.BlockSpec((1, gw), lambda b, g, c: (0, g))
    st = pl.BlockSpec((1, N, gw), lambda b, g, c: (b, 0, g))
    y, hfin = pl.pallas_call(
        functools.partial(_rwkv_fs_kernel, n, rev),
        grid=(bsz, ng, n),
        in_specs=[seq(0), seq(1), seq(2), seq(0), seq(0), vec, vec, vec, vec, st],
        out_specs=[seq(0), st],
        out_shape=[jax.ShapeDtypeStruct((bsz, L, W), F32), jax.ShapeDtypeStruct((bsz, N, W), F32)],
        scratch_shapes=[pltpu.VMEM((N, gw), F32)],
        compiler_params=pltpu.CompilerParams(
            dimension_semantics=("arbitrary",) * 3, vmem_limit_bytes=VMEM_LIMIT),
        name="rwkv_bwd_chunks" if rev else "rwkv_fwd_chunks",
    )(main, main, main, w_pre, a_pre, w0.reshape(1, W), a0.reshape(1, W), k_k.reshape(1, W), k_a.reshape(1, W), h0)
    return y, hfin.reshape(bsz, N, H, N).transpose(0, 2, 3, 1)


def _segsum(x, same_head):
    x1, x2 = _split_bf16(x)
    return (jnp.dot(x1, same_head, preferred_element_type=F32)
            + jnp.dot(x2, same_head, preferred_element_type=F32))


def _rwkv_post_kernel(yf_ref, yb_ref, r_ref, k_ref, v_ref, g_ref, af_ref, ab_ref, a0_ref, ka_ref, rk_ref,
                      lw_ref, lb_ref, o_ref):
    N = RWKV_HEAD
    same_head = ((lax.broadcasted_iota(jnp.int32, (LANES, LANES), 0) < N)
                 == (lax.broadcasted_iota(jnp.int32, (LANES, LANES), 1) < N)).astype(BF16)
    for t in range(o_ref.shape[2] // LANES):
        ls = slice(t * LANES, (t + 1) * LANES)
        wkv = yf_ref[0, :, ls] + yb_ref[0, :, ls]
        mean = _segsum(wkv, same_head) * (1.0 / N)
        cen = wkv - mean
        var = _segsum(cen * cen, same_head) * (1.0 / N)
        ln = cen * lax.rsqrt(var + RWKV_LNX_EPS) * lw_ref[:, ls] + lb_ref[:, ls]
        ka = ka_ref[:, ls]
        k_mix = ((1.0 + (jax.nn.sigmoid(af_ref[0, :, ls] + a0_ref[0:1, ls]) - 1.0) * ka)
                 + (1.0 + (jax.nn.sigmoid(ab_ref[0, :, ls] + a0_ref[1:2, ls]) - 1.0) * ka))
        bonus = _segsum(r_ref[0, :, ls] * k_ref[0, :, ls] * k_mix * rk_ref[:, ls], same_head) * v_ref[0, :, ls]
        gate = g_ref[0, :, ls]
        o_ref[0, :, ls] = ((ln + bonus) * (gate * jax.nn.sigmoid(gate))).astype(o_ref.dtype)


def rwkv_post(y_f, y_b, main, a_pre_f, a_pre_b, a0, k_a, r_k, lnx_w, lnx_b):
    bsz, L, W = y_f.shape
    tr = _pick(L, (256, 128, 64))
    tw = 512
    nw = W // tw

    def seq(col0):
        return pl.BlockSpec((1, tr, tw), lambda b, i, j: (b, i, col0 * nw + j))
    vec = pl.BlockSpec((1, tw), lambda b, i, j: (0, j))
    vec2 = pl.BlockSpec((N_DIR, tw), lambda b, i, j: (0, j))
    return pl.pallas_call(
        _rwkv_post_kernel,
        grid=(bsz, L // tr, nw),
        in_specs=[seq(0), seq(0), seq(0), seq(1), seq(2), seq(3), seq(0), seq(0), vec2, vec, vec, vec, vec],
        out_specs=seq(0),
        out_shape=jax.ShapeDtypeStruct((bsz, L, W), BF16),
        compiler_params=pltpu.CompilerParams(
            dimension_semantics=("arbitrary",) * 3, vmem_limit_bytes=VMEM_LIMIT),
        name="rwkv_post",
    )(y_f, y_b, main, main, main, main, a_pre_f, a_pre_b, a0, k_a.reshape(1, W), r_k.reshape(1, W),
      lnx_w.reshape(1, W), lnx_b.reshape(1, W))


def _split_cols(t, sizes):
    offsets, acc = [], 0
    for s in sizes[:-1]:
        acc += s
        offsets.append(acc)
    return jnp.split(t, offsets, axis=-1)


def _adaln_kernel(c_ref, w_ref, b_ref, o_ref):
    cond = c_ref[...]
    act = cond * jax.nn.sigmoid(cond)
    o_ref[...] = jnp.dot(act, w_ref[...], precision=HI, preferred_element_type=F32) + b_ref[...]


def adaln(cond, w, b):
    rows, dm = cond.shape
    n = w.shape[1]
    rp = -(-rows // 8) * 8
    tn = 512
    m = pl.pallas_call(
        _adaln_kernel,
        grid=(n // tn,),
        in_specs=[pl.BlockSpec((rp, dm), lambda j: (0, 0)),
                  pl.BlockSpec((dm, tn), lambda j: (0, j)),
                  pl.BlockSpec((1, tn), lambda j: (0, j))],
        out_specs=pl.BlockSpec((rp, tn), lambda j: (0, j)),
        out_shape=jax.ShapeDtypeStruct((rp, n), F32),
        compiler_params=pltpu.CompilerParams(dimension_semantics=("arbitrary",), vmem_limit_bytes=VMEM_LIMIT),
        name="adaln",
    )(jnp.pad(cond, ((0, rp - rows), (0, 0))), w, b.reshape(1, n))[:rows]
    return jnp.split(m, 3, axis=-1)


def _grid_pos_embed(n_tokens):
    rows = n_tokens // GRID_W
    row_id = jnp.broadcast_to(jnp.arange(rows, dtype=F32)[:, None], (rows, GRID_W)).reshape(-1)
    col_id = jnp.broadcast_to(jnp.arange(GRID_W, dtype=F32)[None, :], (rows, GRID_W)).reshape(-1)
    quarter = D_MODEL // 4
    omega = 1.0 / (POS_BASE ** (jnp.arange(quarter, dtype=F32) / quarter))

    def axis_emb(pos):
        ang = pos[:, None] * omega[None, :]
        return jnp.concatenate([jnp.sin(ang), jnp.cos(ang)], axis=-1)
    return jnp.concatenate([axis_emb(row_id), axis_emb(col_id)], axis=-1)


def _even_mixer(x, gate, h, s5_re0, s5_im0, gla0, w_in, w_out, s5_ops, glu_w, glu_b, dec_up, dec_b, gla_nw):
    bsz, L, _ = h.shape
    n_main = sum(EVEN_SIZES[:-1])
    main = _mm3(h, w_in[:, :n_main])
    w_tail = jnp.pad(w_in[:, n_main:], ((0, 0), (0, LANES - N_DIR * GLA_RANK)))
    dec_lr = _mm3(h, w_tail)
    gy, fin_re, fin_im = s5_scan(main[..., :S5_W], s5_ops, s5_re0, s5_im0)
    gy = gy.reshape(bsz * L, S5_W)
    mixed = matmul_glu(gy, glu_w, glu_b, main.reshape(bsz * L, n_main), S5_W, S5_W + GLA_DV_W)
    mixed, fin_gla = gla_mix(main, dec_lr, dec_up, dec_b, gla_nw, gla0, mixed.reshape(bsz, L, -1))
    return matmul_gated_residual(mixed, w_out, x, gate), fin_re, fin_im, fin_gla


def _odd_mixer(x, gate, h, rwkv0, w_in, w_out, mu, w0, w2, a0, a2, k_k, k_a, r_k, lnx_w, lnx_b):
    bsz, L, _ = h.shape
    zero = jnp.zeros_like(h[:, :1])
    h_prev = jnp.concatenate([zero, h[:, :-1]], axis=1)
    h_next = jnp.concatenate([h[:, 1:], zero], axis=1)
    xs = h + mu[0] * (h_prev - h) + mu[1] * (h_next - h)
    n_main = sum(ODD_SIZES[:4])
    main = _mm3(xs, w_in[:, :n_main])
    tail = _mm3(xs, w_in[:, n_main:])
    w_lr, a_lr = _split_cols(tail, ODD_SIZES[4:])
    w_lr = jnp.tanh(w_lr).reshape(bsz, L, N_DIR, RWKV_DECAY_RANK)
    a_lr = a_lr.reshape(bsz, L, N_DIR, RWKV_ICLR_RANK)
    ys, a_pres, finals = [], [], []
    for d in range(N_DIR):
        w_pre = _mm3(w_lr[:, :, d], w2[d])
        a_pre = _mm3(a_lr[:, :, d], a2[d])
        y_d, fin = rwkv_direction(bool(d), main, w_pre, a_pre, w0[d], a0[d], k_k, k_a, rwkv0[:, d])
        ys.append(y_d)
        a_pres.append(a_pre)
        finals.append(fin)
    out = rwkv_post(ys[0], ys[1], main, a_pres[0], a_pres[1], a0, k_a, r_k.reshape(-1), lnx_w, lnx_b)
    return matmul_gated_residual(out, w_out, x, gate), jnp.stack(finals, axis=1)


def kernel(x_prompt, x_sample, state_s5_re, state_s5_im, state_gla, state_rwkv, c, c_ctx, norm_w, ada_w, ada_b, final_norm_w, e_w_in, e_w_out, s5_lambda_re, s5_lambda_im, s5_log_step, s5_b_re, s5_b_im, s5_c_re, s5_c_im, s5_d, s5_glu_w, s5_glu_b, gla_decay_up, gla_decay_b, gla_norm_w, o_w_in, o_w_out, rwkv_mu, rwkv_w0, rwkv_w2, rwkv_a0, rwkv_a2, rwkv_k_k, rwkv_k_a, rwkv_r_k, rwkv_lnx_w, rwkv_lnx_b):
    bp = x_prompt.shape[0]
    depth = norm_w.shape[0]
    x_ctx = x_prompt
    x_lat = x_sample + _grid_pos_embed(x_sample.shape[1])[None]
    z_s5 = jnp.zeros((bp, N_DIR, S5_GROUPS, S5_STATE), F32)
    z_rwkv = jnp.zeros((bp, N_DIR, RWKV_HEADS, RWKV_HEAD, RWKV_HEAD), F32)
    new_s5_re, new_s5_im, new_gla, new_rwkv = [], [], [], []
    n_lat = c.shape[0]
    cond = jnp.concatenate([c, c_ctx[None]], axis=0)
    for i in range(depth):
        j = i // 2
        shift, scale, gate = adaln(cond, ada_w[i], ada_b[i])
        gt_l, gt_c = gate[:n_lat], jnp.broadcast_to(gate[n_lat:], (bp, D_MODEL))
        h_ctx = norm_mod(x_ctx, norm_w[i], scale[n_lat:], shift[n_lat:])
        h_lat = norm_mod(x_lat, norm_w[i], scale[:n_lat], shift[:n_lat])
        if i % 2 == 0:
            s5_ops = s5_operators(s5_lambda_re[j], s5_lambda_im[j], s5_log_step[j], s5_b_re[j], s5_b_im[j],
                                  s5_c_re[j], s5_c_im[j], s5_d[j])
            p = (e_w_in[j], e_w_out[j], s5_ops, s5_glu_w[j], s5_glu_b[j], gla_decay_up[j], gla_decay_b[j],
                 gla_norm_w[j])
            x_ctx, fr, fi, fg = _even_mixer(x_ctx, gt_c, h_ctx, z_s5, z_s5, None, *p)
            x_lat, _, _, _ = _even_mixer(x_lat, gt_l, h_lat, state_s5_re[:, j], state_s5_im[:, j],
                                         state_gla[:, j], *p)
            new_s5_re.append(fr)
            new_s5_im.append(fi)
            new_gla.append(fg)
        else:
            p = (o_w_in[j], o_w_out[j], rwkv_mu[j], rwkv_w0[j], rwkv_w2[j], rwkv_a0[j], rwkv_a2[j],
                 rwkv_k_k[j], rwkv_k_a[j], rwkv_r_k[j], rwkv_lnx_w[j], rwkv_lnx_b[j])
            x_ctx, fw = _odd_mixer(x_ctx, gt_c, h_ctx, z_rwkv, *p)
            x_lat, _ = _odd_mixer(x_lat, gt_l, h_lat, state_rwkv[:, j], *p)
            new_rwkv.append(fw)
    y_prompt = final_norm(x_ctx, final_norm_w)
    y_sample = final_norm(x_lat, final_norm_w)
    return (y_prompt, y_sample, jnp.stack(new_s5_re, axis=1), jnp.stack(new_s5_im, axis=1),
            jnp.stack(new_gla, axis=1), jnp.stack(new_rwkv, axis=1))
```

```python
import functools

import jax
import jax.numpy as jnp
from jax import lax
from jax.experimental import pallas as pl
from jax.experimental.pallas import tpu as pltpu

D_MODEL = 2048
GRID_W = 64
POS_BASE = 10000.0
N_DIR = 2
EPS = 1e-6
S5_W = 1024
S5_GROUP_CH = 16
S5_GROUPS = 64
S5_STATE = 64
S5_CHUNK = 16
S5_TILE_GROUPS = 8
GLA_HEADS = 6
GLA_DV = 512
GLA_DK = 256
GLA_DK_W = 1536
GLA_DV_W = 3072
GLA_RANK = 16
GLA_NORMALIZER = 16.0
GLA_CHUNK = 64
GLA_NC = 16
GLA_LOG_DECAY_MIN = -1.0
EVEN_SIZES = (S5_W, S5_W, GLA_DK_W, GLA_DK_W, GLA_DV_W, GLA_DV_W, N_DIR * GLA_RANK)
RWKV_W = 2048
RWKV_HEAD = 64
RWKV_HEADS = 32
RWKV_DECAY_RANK = 96
RWKV_ICLR_RANK = 96
RWKV_LNX_EPS = 64e-5
ODD_SIZES = (RWKV_W, RWKV_W, RWKV_W, RWKV_W, N_DIR * RWKV_DECAY_RANK, N_DIR * RWKV_ICLR_RANK)
RWKV_CHUNK = 64
RWKV_CPAIRS = 16
RWKV_SUB = 16
LANES = 128

VMEM_LIMIT = 48 * 1024 * 1024
HI = lax.Precision.HIGHEST
BF16 = jnp.bfloat16
F32 = jnp.float32


def _mm_kernel(x_ref, w_ref, o_ref):
    o_ref[...] = jnp.dot(x_ref[...], w_ref[...], preferred_element_type=F32)


def _pick(n, prefs):
    for p in prefs:
        if n % p == 0:
            return p
    return n


def matmul(x, w):
    m, k = x.shape
    n = w.shape[1]
    x = x.astype(BF16)
    w = w.astype(BF16)
    tm = _pick(m, (1024, 512, 256, 128, 64, 32, 16, 8))
    tn = _pick(n, (1024, 512, 384, 256, 128))
    return pl.pallas_call(
        _mm_kernel,
        grid=(m // tm, n // tn),
        in_specs=[pl.BlockSpec((tm, k), lambda i, j: (i, 0)),
                  pl.BlockSpec((k, tn), lambda i, j: (0, j))],
        out_specs=pl.BlockSpec((tm, tn), lambda i, j: (i, j)),
        out_shape=jax.ShapeDtypeStruct((m, n), F32),
        compiler_params=pltpu.CompilerParams(
            dimension_semantics=("arbitrary", "arbitrary"), vmem_limit_bytes=VMEM_LIMIT),
        name="proj_matmul",
    )(x, w)


def _mm3(h, w):
    b, l, k = h.shape
    return matmul(h.reshape(b * l, k), w).reshape(b, l, -1)


def _mm_residual_kernel(x_ref, w_ref, res_ref, gate_ref, o_ref):
    acc = jnp.dot(x_ref[...], w_ref[...], preferred_element_type=F32)
    o_ref[...] = res_ref[...] + gate_ref[0] * acc


def matmul_gated_residual(x, w, res, gate):
    bsz, L, k = x.shape
    n = w.shape[1]
    m = bsz * L
    tm = _pick(L, (1024, 512, 256, 128))
    tn = _pick(n, (1024, 512, 256, 128) if k <= 2048 else (512, 256, 128))
    per_b = L // tm
    out = pl.pallas_call(
        _mm_residual_kernel,
        grid=(m // tm, n // tn),
        in_specs=[pl.BlockSpec((tm, k), lambda i, j: (i, 0)),
                  pl.BlockSpec((k, tn), lambda i, j: (0, j)),
                  pl.BlockSpec((tm, tn), lambda i, j: (i, j)),
                  pl.BlockSpec((1, 1, tn), lambda i, j: (i // per_b, 0, j))],
        out_specs=pl.BlockSpec((tm, tn), lambda i, j: (i, j)),
        out_shape=jax.ShapeDtypeStruct((m, n), F32),
        compiler_params=pltpu.CompilerParams(
            dimension_semantics=("arbitrary", "arbitrary"), vmem_limit_bytes=VMEM_LIMIT),
        name="proj_residual",
    )(x.reshape(m, k).astype(BF16), w.astype(BF16), res.reshape(m, n), gate.reshape(bsz, 1, n))
    return out.reshape(bsz, L, n)


def _mm_glu_kernel(x_ref, w_ref, b_ref, g_ref, xt_ref, o_ref):
    acc = jnp.dot(x_ref[...], w_ref[...], preferred_element_type=F32) + b_ref[...]
    gy = xt_ref[...].astype(F32)
    gate = g_ref[...]
    o_ref[...] = (gy * jax.nn.sigmoid(acc) * (gate * jax.nn.sigmoid(gate))).astype(o_ref.dtype)


def matmul_glu(gy, w, b, main, g_col0, n_total):
    m, k = gy.shape
    n = w.shape[1]
    tm = _pick(m, (1024, 512, 256, 128))
    tn = 512
    return pl.pallas_call(
        _mm_glu_kernel,
        grid=(m // tm, n // tn),
        in_specs=[pl.BlockSpec((tm, k), lambda i, j: (i, 0)),
                  pl.BlockSpec((k, tn), lambda i, j: (0, j)),
                  pl.BlockSpec((1, tn), lambda i, j: (0, j)),
                  pl.BlockSpec((tm, tn), lambda i, j: (i, g_col0 // tn + j)),
                  pl.BlockSpec((tm, tn), lambda i, j: (i, j))],
        out_specs=pl.BlockSpec((tm, tn), lambda i, j: (i, j)),
        out_shape=jax.ShapeDtypeStruct((m, n_total), BF16),
        compiler_params=pltpu.CompilerParams(
            dimension_semantics=("arbitrary", "arbitrary"), vmem_limit_bytes=VMEM_LIMIT),
        name="s5_glu_gate",
    )(gy, w.astype(BF16), b.reshape(1, n), main, gy)


def _norm_mod_kernel(x_ref, nw_ref, sc_ref, sh_ref, o_ref):
    x = x_ref[0]
    inv = lax.rsqrt(jnp.mean(x * x, axis=-1, keepdims=True) + EPS)
    o_ref[0] = (x * inv * nw_ref[...] * (1.0 + sc_ref[0]) + sh_ref[0]).astype(o_ref.dtype)


def norm_mod(x, nw, scale, shift):
    bsz, L, dm = x.shape
    tr = _pick(L, (256, 128, 64))
    nb = scale.shape[0]
    cond = pl.BlockSpec((1, 1, dm), lambda b, i: (b if nb > 1 else 0, 0, 0))
    return pl.pallas_call(
        _norm_mod_kernel,
        grid=(bsz, L // tr),
        in_specs=[pl.BlockSpec((1, tr, dm), lambda b, i: (b, i, 0)),
                  pl.BlockSpec((1, dm), lambda b, i: (0, 0)), cond, cond],
        out_specs=pl.BlockSpec((1, tr, dm), lambda b, i: (b, i, 0)),
        out_shape=jax.ShapeDtypeStruct((bsz, L, dm), BF16),
        compiler_params=pltpu.CompilerParams(
            dimension_semantics=("arbitrary", "arbitrary"), vmem_limit_bytes=VMEM_LIMIT),
        name="norm_mod",
    )(x, nw.reshape(1, dm), scale.reshape(nb, 1, dm), shift.reshape(nb, 1, dm))


def _final_norm_kernel(x_ref, nw_ref, o_ref):
    x = x_ref[0]
    o_ref[0] = x * lax.rsqrt(jnp.mean(x * x, axis=-1, keepdims=True) + EPS) * nw_ref[...]


def final_norm(x, nw):
    bsz, L, dm = x.shape
    tr = _pick(L, (256, 128, 64))
    return pl.pallas_call(
        _final_norm_kernel,
        grid=(bsz, L // tr),
        in_specs=[pl.BlockSpec((1, tr, dm), lambda b, i: (b, i, 0)), pl.BlockSpec((1, dm), lambda b, i: (0, 0))],
        out_specs=pl.BlockSpec((1, tr, dm), lambda b, i: (b, i, 0)),
        out_shape=jax.ShapeDtypeStruct((bsz, L, dm), F32),
        compiler_params=pltpu.CompilerParams(
            dimension_semantics=("arbitrary", "arbitrary"), vmem_limit_bytes=VMEM_LIMIT),
        name="final_norm",
    )(x, nw.reshape(1, dm))


def s5_operators(lam_re, lam_im, log_step, b_re, b_im, c_re, c_im, d_skip):
    T = S5_CHUNK
    dt = jnp.exp(log_step)[..., None]
    mag = jnp.exp(lam_re * dt)
    ab_re, ab_im = mag * jnp.cos(lam_im * dt), mag * jnp.sin(lam_im * dt)
    den = lam_re * lam_re + lam_im * lam_im
    f_re = ((ab_re - 1.0) * lam_re + ab_im * lam_im) / den
    f_im = (ab_im * lam_re - (ab_re - 1.0) * lam_im) / den
    bb_re = f_re[..., None] * b_re - f_im[..., None] * b_im
    bb_im = f_re[..., None] * b_im + f_im[..., None] * b_re
    kk = jnp.arange(T + 1, dtype=F32)[:, None, None, None]
    pmag = jnp.exp(kk * (lam_re * dt))
    pr = pmag * jnp.cos(kk * (lam_im * dt))
    pi = pmag * jnp.sin(kk * (lam_im * dt))
    zr = pr[:T, :, :, :, None] * bb_re - pi[:T, :, :, :, None] * bb_im
    zi = pr[:T, :, :, :, None] * bb_im + pi[:T, :, :, :, None] * bb_re
    kern = (jnp.einsum('dghp,kdgpj->kdghj', c_re, zr, precision=HI)
            - jnp.einsum('dghp,kdgpj->kdghj', c_im, zi, precision=HI))
    t_idx = jnp.arange(T)[:, None]
    s_idx = jnp.arange(T)[None, :]
    lag_f = t_idx - s_idx
    lag_b = s_idx - t_idx
    m_f = jnp.where((lag_f >= 0)[:, :, None, None, None], kern[:, 0][jnp.clip(lag_f, 0, T - 1)], 0.0)
    m_b = jnp.where((lag_b >= 0)[:, :, None, None, None], kern[:, 1][jnp.clip(lag_b, 0, T - 1)], 0.0)
    m = m_f + m_b
    eye_t = jnp.eye(T, dtype=F32)[:, :, None, None, None]
    eye_h = jnp.eye(S5_GROUP_CH, dtype=F32)[None, None, None]
    m = m + eye_t * eye_h * d_skip.reshape(S5_GROUPS, S5_GROUP_CH)[None, None, :, :, None]
    g = m.shape[2]
    m_t = m.transpose(2, 1, 4, 0, 3).reshape(g, T * S5_GROUP_CH, T * S5_GROUP_CH)
    pf_r, pf_i = pr[T - 1::-1][:T, 0], pi[T - 1::-1][:T, 0]
    pb_r, pb_i = pr[:T, 1], pi[:T, 1]

    def f_mat(p_r, p_i, d):
        re = p_r[..., None] * bb_re[d][None] - p_i[..., None] * bb_im[d][None]
        im = p_r[..., None] * bb_im[d][None] + p_i[..., None] * bb_re[d][None]
        re = re.transpose(1, 0, 3, 2).reshape(g, T * S5_GROUP_CH, S5_STATE)
        im = im.transpose(1, 0, 3, 2).reshape(g, T * S5_GROUP_CH, S5_STATE)
        return re, im
    ff_re, ff_im = f_mat(pf_r, pf_i, 0)
    fb_re, fb_im = f_mat(pb_r, pb_i, 1)
    a_t = jnp.concatenate([m_t, ff_re, fb_re, ff_im, fb_im], axis=-1)
    ef_r, ef_i = pr[1:T + 1, 0], pi[1:T + 1, 0]
    eb_r, eb_i = pr[T:0:-1, 1], pi[T:0:-1, 1]

    def e_mat(p_r, p_i, d):
        er = c_re[d][None] * p_r[:, :, None, :] - c_im[d][None] * p_i[:, :, None, :]
        ei = -(c_re[d][None] * p_i[:, :, None, :] + c_im[d][None] * p_r[:, :, None, :])
        er = er.transpose(1, 3, 0, 2).reshape(g, S5_STATE, T * S5_GROUP_CH)
        ei = ei.transpose(1, 3, 0, 2).reshape(g, S5_STATE, T * S5_GROUP_CH)
        return er, ei
    efr, efi = e_mat(ef_r, ef_i, 0)
    ebr, ebi = e_mat(eb_r, eb_i, 1)
    e_t = jnp.concatenate([efr, ebr, efi, ebi], axis=1)
    lam_t = jnp.concatenate([pr[T, 0], pr[T, 1], pi[T, 0], pi[T, 1]], axis=-1)[:, None, :]
    return a_t.astype(BF16), e_t.astype(BF16), lam_t


def _s5_kernel(n_chunks, bsz, x8_ref, sel_ref, at_ref, et_ref, lam_ref, h0_ref, y_ref, hfin_ref, z_ref, hent_ref):
    P = S5_STATE
    ut = jnp.dot(x8_ref[0], sel_ref[0], preferred_element_type=F32).astype(BF16)
    z_ref[...] = jnp.dot(ut, at_ref[0], preferred_element_type=F32)
    lam = lam_ref[0]
    a_re, a_im = lam[:, 0:2 * P], lam[:, 2 * P:4 * P]
    h0 = h0_ref[0]
    fwd_lanes = lax.broadcasted_iota(jnp.int32, (bsz, 2 * P), 1) < P

    def step(c, carry):
        h_re, h_im = carry
        rf = pl.ds(pl.multiple_of(c * bsz, 8), bsz)
        rb = pl.ds(pl.multiple_of((n_chunks - 1 - c) * bsz, 8), bsz)
        hent_ref[rf, 0:P] = h_re[:, 0:P]
        hent_ref[rb, P:2 * P] = h_re[:, P:2 * P]
        hent_ref[rf, 2 * P:3 * P] = h_im[:, 0:P]
        hent_ref[rb, 3 * P:4 * P] = h_im[:, P:2 * P]
        g_re = jnp.where(fwd_lanes, z_ref[rf, 4 * P:6 * P], z_ref[rb, 4 * P:6 * P])
        g_im = jnp.where(fwd_lanes, z_ref[rf, 6 * P:8 * P], z_ref[rb, 6 * P:8 * P])
        return a_re * h_re - a_im * h_im + g_re, a_re * h_im + a_im * h_re + g_im
    h_re, h_im = lax.fori_loop(0, n_chunks, step, (h0[:, 0:2 * P], h0[:, 2 * P:4 * P]))
    hfin_ref[0, :, 0:2 * P] = h_re
    hfin_ref[0, :, 2 * P:4 * P] = h_im
    y = z_ref[:, 0:4 * P] + jnp.dot(hent_ref[...].astype(BF16), et_ref[0], preferred_element_type=F32)
    y_ref[0] = jax.nn.gelu(y).astype(y_ref.dtype)


def _s5_unpack_kernel(yt_ref, selt_ref, o_ref):
    acc = jnp.dot(yt_ref[0], selt_ref[0], preferred_element_type=F32)
    for gl in range(1, S5_TILE_GROUPS):
        acc = acc + jnp.dot(yt_ref[gl], selt_ref[gl], preferred_element_type=F32)
    o_ref[0] = acc.astype(o_ref.dtype)


def s5_scan(u, ops, h0_re, h0_im):
    a_t, e_t, lam_t = ops
    b_real, L, _ = u.shape
    T, G, H, P = S5_CHUNK, S5_GROUPS, S5_GROUP_CH, S5_STATE
    TG = S5_TILE_GROUPS
    n = L // T
    bsz = -(-b_real // 8) * 8
    cols = n * bsz
    x8 = u.reshape(b_real, n, T, G // TG, LANES).transpose(3, 1, 0, 2, 4).astype(BF16)
    x8 = jnp.pad(x8, ((0, 0), (0, 0), (0, bsz - b_real), (0, 0), (0, 0))).reshape(G // TG, cols, T * LANES)
    src = jnp.arange(T * LANES)
    dst = jnp.arange(T * H)
    sel = ((src[None, :, None] // LANES == dst[None, None, :] // H)
           & (src[None, :, None] % H == dst[None, None, :] % H)
           & ((src[None, :, None] % LANES) // H == jnp.arange(TG)[:, None, None])).astype(BF16)
    h0 = jnp.concatenate([h0_re[:, 0], h0_re[:, 1], h0_im[:, 0], h0_im[:, 1]], axis=-1)
    h0 = jnp.pad(h0.transpose(1, 0, 2), ((0, 0), (0, bsz - b_real), (0, 0)))
    yt, hfin = pl.pallas_call(
        functools.partial(_s5_kernel, n, bsz),
        grid=(G,),
        in_specs=[pl.BlockSpec((1, cols, T * LANES), lambda g: (g // TG, 0, 0)),
                  pl.BlockSpec((1, T * LANES, T * H), lambda g: (g % TG, 0, 0)),
                  pl.BlockSpec((1, T * H, 8 * P), lambda g: (g, 0, 0)),
                  pl.BlockSpec((1, 4 * P, T * H), lambda g: (g, 0, 0)),
                  pl.BlockSpec((1, 1, 4 * P), lambda g: (g, 0, 0)),
                  pl.BlockSpec((1, bsz, 4 * P), lambda g: (g, 0, 0))],
        out_specs=[pl.BlockSpec((1, cols, T * H), lambda g: (g, 0, 0)),
                   pl.BlockSpec((1, bsz, 4 * P), lambda g: (g, 0, 0))],
        out_shape=[jax.ShapeDtypeStruct((G, cols, T * H), BF16),
                   jax.ShapeDtypeStruct((G, bsz, 4 * P), F32)],
        scratch_shapes=[pltpu.VMEM((cols, 8 * P), F32), pltpu.VMEM((cols, 4 * P), F32)],
        compiler_params=pltpu.CompilerParams(dimension_semantics=("arbitrary",), vmem_limit_bytes=VMEM_LIMIT),
        name="s5_chunk_scan",
    )(x8, sel, a_t, e_t, lam_t, h0)
    tr = _pick(cols, (512, 256, 128))
    y8 = pl.pallas_call(
        _s5_unpack_kernel,
        grid=(G // TG, cols // tr),
        in_specs=[pl.BlockSpec((TG, tr, T * H), lambda t, i: (t, i, 0)),
                  pl.BlockSpec((TG, T * H, T * LANES), lambda t, i: (0, 0, 0))],
        out_specs=pl.BlockSpec((1, tr, T * LANES), lambda t, i: (t, i, 0)),
        out_shape=jax.ShapeDtypeStruct((G // TG, cols, T * LANES), BF16),
        compiler_params=pltpu.CompilerParams(
            dimension_semantics=("arbitrary", "arbitrary"), vmem_limit_bytes=VMEM_LIMIT),
        name="s5_unpack",
    )(yt, sel.transpose(0, 2, 1))
    y = y8.reshape(G // TG, n, bsz, T, LANES)[:, :, :b_real].transpose(2, 1, 3, 0, 4).reshape(b_real, L, G * H)
    hfin = hfin[:, :b_real].transpose(1, 0, 2)
    fin_re = jnp.stack([hfin[..., 0:P], hfin[..., P:2 * P]], axis=1)
    fin_im = jnp.stack([hfin[..., 2 * P:3 * P], hfin[..., 3 * P:4 * P]], axis=1)
    return y, fin_re, fin_im


def _dot_t(a, b):
    return lax.dot_general(a, b, (((1,), (1,)), ((), ())), preferred_element_type=F32)


def _dot_mask(mask_bf16, x, x_rows_to_sublanes=False):
    def d(b):
        if x_rows_to_sublanes:
            return lax.dot_general(b, mask_bf16, (((0,), (0,)), ((), ())), preferred_element_type=F32)
        return jnp.dot(mask_bf16, b, preferred_element_type=F32)
    x1 = x.astype(BF16)
    r1 = x - x1.astype(F32)
    x2 = r1.astype(BF16)
    x3 = (r1 - x2.astype(F32)).astype(BF16)
    return d(x1) + (d(x2) + d(x3))


def _gla_block_kernel(n_blocks, NC, has_s0, q_ref, k_ref, v_ref, g_ref, lr_ref, up_ref, db_ref, nw_ref, dst_ref,
                      *refs):
    s0_ref = refs[0] if has_s0 else None
    out_ref, sfin_ref, s_ref, of_ref, qd_ref, ov_ref, kv_ref, dc_ref = refs[1:] if has_s0 else refs
    C = GLA_CHUNK
    R = C * NC
    d = pl.program_id(2)
    c = pl.program_id(3)
    bidx = jnp.where(d == 0, c, n_blocks - 1 - c)

    @pl.when(c == 0)
    def _():
        s_ref[...] = s0_ref[0, 0, 0] if has_s0 else jnp.zeros_like(s_ref)

    z = _mxu(lr_ref[0], up_ref[0], split=True) + db_ref[0]
    gc = jnp.maximum(jax.nn.log_sigmoid(z) * (1.0 / GLA_NORMALIZER), GLA_LOG_DECAY_MIN)
    row_c = lax.broadcasted_iota(jnp.int32, (C, C), 0)
    col_c = lax.broadcasted_iota(jnp.int32, (C, C), 1)
    seen_c = jnp.where(d == 0, row_c - col_c, col_c - row_c) >= 0
    seen_bf = seen_c.astype(BF16)
    rs = [slice(i * C, (i + 1) * C) for i in range(NC)]
    bcum_c = [_dot_mask(seen_bf, gc[r]) for r in rs]
    btot_c = [jnp.broadcast_to(jnp.where(d == 0, b[C - 1:C], b[0:1]), (C, GLA_DK)) for b in bcum_c]
    bcum = jnp.concatenate(bcum_c, axis=0)
    btot = jnp.concatenate(btot_c, axis=0)
    q_dec = (q_ref[0] * (GLA_DK ** -0.5) * jnp.exp(bcum)).astype(BF16)
    k = k_ref[0]
    k_inv = (k * jnp.exp(-bcum)).astype(BF16)
    k_end = (k * jnp.exp(btot - bcum)).astype(BF16)
    v = v_ref[0].astype(BF16)
    ones_c = jnp.ones((C, LANES), BF16)
    qd_ref[...] = q_dec.reshape(NC, C, GLA_DK)
    att = [jnp.where(seen_c, _dot_t(q_dec[r], k_inv[r]), 0.0).astype(BF16) for r in rs]
    for i in range(NC):
        kv_ref[i] = lax.dot_general(k_end[rs[i]], v[rs[i]], (((0,), (0,)), ((), ())), preferred_element_type=F32)
    for i in range(NC):
        ov_ref[i] = jnp.dot(att[i], v[rs[i]], preferred_element_type=F32)
    for i in range(NC):
        dc_ref[i] = _dot_mask(ones_c, gc[rs[i]], x_rows_to_sublanes=True)

    for i in range(NC):
        ci = jnp.where(d == 0, i, NC - 1 - i)
        s_old = s_ref[...]
        o = ov_ref[ci] + jnp.dot(qd_ref[ci], s_old.astype(BF16), preferred_element_type=F32)
        s_ref[...] = jnp.exp(dc_ref[ci][:, 0:1]) * s_old + kv_ref[ci]
        rows = pl.ds(pl.multiple_of(bidx * R + ci * C, C), C)
        orow = pl.ds(pl.multiple_of(ci * C, C), C)

        @pl.when(d == 0)
        def _():
            of_ref[rows, :] = o

        @pl.when(d == 1)
        def _():
            tot = of_ref[rows, :] + o
            nrm = tot * lax.rsqrt(jnp.mean(tot * tot, axis=-1, keepdims=True) + EPS) * nw_ref[0]
            gate = g_ref[0, orow, :]
            out_ref[0, orow, :] = (nrm * (gate * jax.nn.sigmoid(gate))).astype(out_ref.dtype)

    @pl.when(c == n_blocks - 1)
    def _():
        sfin_ref[0, 0, 0] = s_ref[...]


def gla_mix(main, dec_lr, dec_up, dec_b, gla_nw, s0, dst):
    bsz, L, _ = main.shape
    H, DK, DV = GLA_HEADS, GLA_DK, GLA_DV
    nc = min(GLA_NC, L // GLA_CHUNK)
    C = GLA_CHUNK * nc
    n = L // C
    q_blk = sum(EVEN_SIZES[:2]) // DK
    k_blk = sum(EVEN_SIZES[:3]) // DK
    v_blk = sum(EVEN_SIZES[:4]) // DV
    g_blk = sum(EVEN_SIZES[:5]) // DV
    up = jnp.zeros((N_DIR, LANES, GLA_DK_W), F32)
    for d in range(N_DIR):
        up = up.at[d, d * GLA_RANK:(d + 1) * GLA_RANK].set(dec_up[d])
    db = dec_b.reshape(N_DIR, 1, GLA_DK_W)
    nw = gla_nw.reshape(1, GLA_DV_W)

    def chunk(d, c):
        return c + d * (n - 1 - 2 * c)

    def out_chunk(d, c):
        return (n - 1) - d * c
    state = pl.BlockSpec((1, 1, 1, DK, DV), lambda b, h, d, c: (b, d, h, 0, 0))
    has_s0 = s0 is not None
    out, sfin = pl.pallas_call(
        functools.partial(_gla_block_kernel, n, nc, has_s0),
        grid=(bsz, H, N_DIR, n),
        in_specs=[pl.BlockSpec((1, C, DK), lambda b, h, d, c: (b, chunk(d, c), q_blk + h)),
                  pl.BlockSpec((1, C, DK), lambda b, h, d, c: (b, chunk(d, c), k_blk + h)),
                  pl.BlockSpec((1, C, DV), lambda b, h, d, c: (b, chunk(d, c), v_blk + h)),
                  pl.BlockSpec((1, C, DV), lambda b, h, d, c: (b, chunk(d, c), g_blk + h)),
                  pl.BlockSpec((1, C, LANES), lambda b, h, d, c: (b, chunk(d, c), 0)),
                  pl.BlockSpec((1, LANES, DK), lambda b, h, d, c: (d, 0, h)),
                  pl.BlockSpec((1, 1, DK), lambda b, h, d, c: (d, 0, h)),
                  pl.BlockSpec((1, DV), lambda b, h, d, c: (0, h)),
                  pl.BlockSpec(memory_space=pl.ANY)] + ([state] if has_s0 else []),
        input_output_aliases={8: 0},
        out_specs=[pl.BlockSpec((1, C, DV), lambda b, h, d, c: (b, out_chunk(d, c), S5_W // DV + h)), state],
        out_shape=[jax.ShapeDtypeStruct(dst.shape, BF16),
                   jax.ShapeDtypeStruct((bsz, N_DIR, H, DK, DV), F32)],
        scratch_shapes=[pltpu.VMEM((DK, DV), F32), pltpu.VMEM((L, DV), F32),
                        pltpu.VMEM((nc, GLA_CHUNK, DK), BF16), pltpu.VMEM((nc, GLA_CHUNK, DV), F32),
                        pltpu.VMEM((nc, DK, DV), F32), pltpu.VMEM((nc, DK, LANES), F32)],
        compiler_params=pltpu.CompilerParams(
            dimension_semantics=("arbitrary",) * 4, vmem_limit_bytes=VMEM_LIMIT),
        name="gla_chunk_scan",
    )(main, main, main, main, dec_lr, up, db, nw, dst, *([s0] if has_s0 else []))
    return out, sfin


def _split_bf16(x):
    hi = x.astype(BF16)
    return hi, (x - hi.astype(F32)).astype(BF16)


def _mxu(x, y, dims=(((1,), (0,)), ((), ())), split=False):
    def d(a, b):
        return lax.dot_general(a, b, dims, preferred_element_type=F32)
    if not split:
        return d(x.astype(BF16), y.astype(BF16))
    xh, xl = _split_bf16(x)
    yh, yl = _split_bf16(y)
    return d(xh, yh) + (d(xh, yl) + d(xl, yh))


def _rwkv_fs_kernel(n_chunks, rev, r_ref, k_ref, v_ref, wp_ref, ap_ref, w0_ref, a0_ref, kk_ref, ka_ref, h0_ref,
                    y_ref, hfin_ref, h_ref):
    T, N, SB = RWKV_CHUNK, RWKV_HEAD, RWKV_SUB
    NB = T // SB
    c = pl.program_id(2)

    @pl.when(c == 0)
    def _():
        h_ref[...] = h0_ref[0]

    lane = lax.broadcasted_iota(jnp.int32, (T, LANES), 1)
    row = lax.broadcasted_iota(jnp.int32, (T, LANES), 0)
    lo = lane < N
    col = lane % N
    order = (col - row) if rev else (row - col)
    seen = order >= 0
    before = order > 0
    eye = row == col
    sq_r = lax.broadcasted_iota(jnp.int32, (T, T), 0)
    sq_c = lax.broadcasted_iota(jnp.int32, (T, T), 1)
    seen_sq = (((sq_c - sq_r) if rev else (sq_r - sq_c)) >= 0).astype(BF16)
    same_head = ((lax.broadcasted_iota(jnp.int32, (LANES, LANES), 0) < N)
                 == (lax.broadcasted_iota(jnp.int32, (LANES, LANES), 1) < N)).astype(BF16)
    col_sb = lax.broadcasted_iota(jnp.int32, (SB, LANES), 1) % N
    row_dims = (((0,), (0,)), ((), ()))
    lane_dims = (((1,), (1,)), ((), ()))

    def bd(x):
        return jnp.concatenate([jnp.where(lo, x, 0.0), jnp.where(lo, 0.0, x)], axis=0)

    def pp(x, y, split=False):
        return _mxu(x, bd(y), split=split)

    def ptp(x, y):
        full = _mxu(x, y, row_dims)
        return jnp.where(lo, full[:N], full[N:])

    w_log = -jax.nn.softplus(-(wp_ref[0] + w0_ref[...])) - 0.5
    lw_all = -jnp.exp(w_log)
    iclr_all = jax.nn.sigmoid(ap_ref[0] + a0_ref[...])
    k_all = k_ref[0]
    kd_all = k_all * (1.0 + (iclr_all - 1.0) * ka_ref[...])
    kkr_all = k_all * kk_ref[...]
    cs_all = _dot_mask(seen_sq, lw_all)
    tot_all = jnp.sum(lw_all, axis=0, keepdims=True)
    pairs = range(RWKV_CPAIRS)
    sl = [slice(p * LANES, (p + 1) * LANES) for p in pairs]
    sq_hi = [_split_bf16(kkr_all[:, s] * kkr_all[:, s]) for s in sl]
    ssq = [jnp.dot(sq_hi[p][0], same_head, preferred_element_type=F32)
           + jnp.dot(sq_hi[p][1], same_head, preferred_element_type=F32) for p in pairs]
    kk = [kkr_all[:, sl[p]] / jnp.maximum(jnp.sqrt(ssq[p]), 1e-12) for p in pairs]
    b_in = [kk[p] * iclr_all[:, sl[p]] for p in pairs]
    cs = [cs_all[:, s] for s in sl]
    tot = [tot_all[:, s] for s in sl]
    e_out = [jnp.exp(-cs[p]) for p in pairs]
    at = [-kk[p] * jnp.exp(cs[p] - lw_all[:, sl[p]]) for p in pairs]
    rt = [r_ref[0, :, sl[p]] * jnp.exp(cs[p]) for p in pairs]
    ar = [jnp.concatenate([at[p], rt[p]], axis=0) for p in pairs]
    g1 = [_mxu(ar[p], bd(b_in[p] * e_out[p]), lane_dims) for p in pairs]
    g2 = [_mxu(ar[p], bd(kd_all[:, sl[p]] * e_out[p]), lane_dims) for p in pairs]
    a_ab = [jnp.where(before, g1[p][:T], 0.0) for p in pairs]
    a_rb = [jnp.where(seen, g1[p][T:], 0.0) for p in pairs]
    a_ak = [jnp.where(before, g2[p][:T], 0.0) for p in pairs]
    a_rk = [jnp.where(seen, g2[p][T:], 0.0) for p in pairs]
    v = [v_ref[0, :, sl[p]] for p in pairs]
    akv = [pp(a_ak[p], v[p]) for p in pairs]
    za = [[None] * NB for _ in pairs]
    zu = [[None] * NB for _ in pairs]
    zero_blk = jnp.zeros((SB, LANES), F32)
    for kpos in range(NB):
        bk = NB - 1 - kpos if rev else kpos
        rows = slice(bk * SB, (bk + 1) * SB)
        done = [(m > bk) if rev else (m < bk) for m in range(NB)]
        cur_a = [at[p][rows] for p in pairs]
        cur_u = [akv[p][rows] for p in pairs]
        if kpos > 0:
            for p in pairs:
                zc_a = jnp.concatenate([za[p][m] if done[m] else zero_blk for m in range(NB)], axis=0)
                zc_u = jnp.concatenate([zu[p][m] if done[m] else zero_blk for m in range(NB)], axis=0)
                off = _mxu(a_ab[p][rows], jnp.concatenate([bd(zc_a), bd(zc_u)], axis=1))
                cur_a[p] = cur_a[p] + off[:, :LANES]
                cur_u[p] = cur_u[p] + off[:, LANES:]
        abc = []
        for p in pairs:
            ablk = a_ab[p][rows]
            picked = jnp.concatenate([jnp.where(col_sb == bk * SB + s, ablk, 0.0) for s in range(SB)], axis=0)
            abc.append(jnp.dot(picked.astype(BF16), same_head, preferred_element_type=F32))
        ha = [[cur_a[p][:8], cur_a[p][8:]] for p in pairs]
        hu = [[cur_u[p][:8], cur_u[p][8:]] for p in pairs]
        for j in range(SB - 1):
            s = SB - 1 - j if rev else j
            src, r8 = s // 8, s % 8
            halves = (0, 1) if (s >= 8) == rev else ((0,) if rev else (1,))
            for p in pairs:
                row_a = ha[p][src][r8:r8 + 1]
                row_u = hu[p][src][r8:r8 + 1]
                for hf in halves:
                    coef = abc[p][s * SB + hf * 8:s * SB + hf * 8 + 8]
                    ha[p][hf] = ha[p][hf] + coef * row_a
                    hu[p][hf] = hu[p][hf] + coef * row_u
        for p in pairs:
            za[p][bk] = jnp.concatenate(ha[p], axis=0)
            zu[p][bk] = jnp.concatenate(hu[p], axis=0)
    a_hat = [jnp.concatenate(za[p], axis=0) for p in pairs]
    u_loc = [jnp.concatenate(zu[p], axis=0) for p in pairs]
    h0 = [h_ref[:, sl[p]] for p in pairs]
    q_hat = [rt[p] + pp(a_rb[p], a_hat[p]) for p in pairs]
    y_loc = [pp(a_rb[p], u_loc[p]) + pp(a_rk[p], v[p]) for p in pairs]
    for p in pairs:
        y_ref[0, :, sl[p]] = (pp(q_hat[p], h0[p]) + y_loc[p]).astype(y_ref.dtype)
    e_end = [jnp.exp(tot[p] - cs[p]) for p in pairs]
    bh = [b_in[p] * e_end[p] for p in pairs]
    p_end = [_split_bf16(jnp.where(eye, jnp.exp(tot[p]), 0.0)) for p in pairs]
    decay = [jnp.dot(p_end[p][0], same_head, preferred_element_type=F32)
             + jnp.dot(p_end[p][1], same_head, preferred_element_type=F32) for p in pairs]
    corr = [ptp(bh[p], a_hat[p]) for p in pairs]
    gam = [ptp(jnp.concatenate([bh[p], kd_all[:, sl[p]] * e_end[p]], axis=0),
               jnp.concatenate([u_loc[p], v[p]], axis=0)) for p in pairs]
    for p in pairs:
        h_ref[:, sl[p]] = decay[p] * h0[p] + (pp(corr[p], h0[p], split=True) + gam[p])

    @pl.when(c == n_chunks - 1)
    def _():
        hfin_ref[0] = h_ref[...]


def rwkv_direction(rev, main, w_pre, a_pre, w0, a0, k_k, k_a, s0):
    bsz, L, _ = main.shape
    W = RWKV_W
    T, N, H = RWKV_CHUNK, RWKV_HEAD, RWKV_HEADS
    n = L // T
    gw = RWKV_CPAIRS * LANES
    ng = W // gw
    h0 = s0.transpose(0, 3, 1, 2).reshape(bsz, N, W)

    def seq(col0):
        return pl.BlockSpec((1, T, gw), lambda b, g, c: (b, (n - 1 - c) if rev else c, col0 * ng + g))
    vec = pl.BlockSpec((1, gw), lambda b, g, c: (0, g))
    st = pl.BlockSpec((1, N, gw), lambda b, g, c: (b, 0, g))
    y, hfin = pl.pallas_call(
        functools.partial(_rwkv_fs_kernel, n, rev),
        grid=(bsz, ng, n),
        in_specs=[seq(0), seq(1), seq(2), seq(0), seq(0), vec, vec, vec, vec, st],
        out_specs=[seq(0), st],
        out_shape=[jax.ShapeDtypeStruct((bsz, L, W), BF16), jax.ShapeDtypeStruct((bsz, N, W), F32)],
        scratch_shapes=[pltpu.VMEM((N, gw), F32)],
        compiler_params=pltpu.CompilerParams(
            dimension_semantics=("arbitrary",) * 3, vmem_limit_bytes=VMEM_LIMIT),
        name="rwkv_bwd_chunks" if rev else "rwkv_fwd_chunks",
    )(main, main, main, w_pre, a_pre, w0.reshape(1, W), a0.reshape(1, W), k_k.reshape(1, W), k_a.reshape(1, W), h0)
    return y, hfin.reshape(bsz, N, H, N).transpose(0, 2, 3, 1)


def _segsum(x, same_head):
    x1, x2 = _split_bf16(x)
    return (jnp.dot(x1, same_head, preferred_element_type=F32)
            + jnp.dot(x2, same_head, preferred_element_type=F32))


def _rwkv_post_kernel(yf_ref, yb_ref, r_ref, k_ref, v_ref, g_ref, af_ref, ab_ref, a0_ref, ka_ref, rk_ref,
                      lw_ref, lb_ref, o_ref):
    N = RWKV_HEAD
    same_head = ((lax.broadcasted_iota(jnp.int32, (LANES, LANES), 0) < N)
                 == (lax.broadcasted_iota(jnp.int32, (LANES, LANES), 1) < N)).astype(BF16)
    for t in range(o_ref.shape[2] // LANES):
        ls = slice(t * LANES, (t + 1) * LANES)
        wkv = yf_ref[0, :, ls].astype(F32) + yb_ref[0, :, ls].astype(F32)
        mean = _segsum(wkv, same_head) * (1.0 / N)
        cen = wkv - mean
        var = _segsum(cen * cen, same_head) * (1.0 / N)
        ln = cen * lax.rsqrt(var + RWKV_LNX_EPS) * lw_ref[:, ls] + lb_ref[:, ls]
        ka = ka_ref[:, ls]
        k_mix = ((1.0 + (jax.nn.sigmoid(af_ref[0, :, ls] + a0_ref[0:1, ls]) - 1.0) * ka)
                 + (1.0 + (jax.nn.sigmoid(ab_ref[0, :, ls] + a0_ref[1:2, ls]) - 1.0) * ka))
        bonus = _segsum(r_ref[0, :, ls] * k_ref[0, :, ls] * k_mix * rk_ref[:, ls], same_head) * v_ref[0, :, ls]
        gate = g_ref[0, :, ls]
        o_ref[0, :, ls] = ((ln + bonus) * (gate * jax.nn.sigmoid(gate))).astype(o_ref.dtype)


def rwkv_post(y_f, y_b, main, a_pre_f, a_pre_b, a0, k_a, r_k, lnx_w, lnx_b):
    bsz, L, W = y_f.shape
    tr = _pick(L, (256, 128, 64))
    tw = 1024
    nw = W // tw

    def seq(col0):
        return pl.BlockSpec((1, tr, tw), lambda b, i, j: (b, i, col0 * nw + j))
    vec = pl.BlockSpec((1, tw), lambda b, i, j: (0, j))
    vec2 = pl.BlockSpec((N_DIR, tw), lambda b, i, j: (0, j))
    return pl.pallas_call(
        _rwkv_post_kernel,
        grid=(bsz, L // tr, nw),
        in_specs=[seq(0), seq(0), seq(0), seq(1), seq(2), seq(3), seq(0), seq(0), vec2, vec, vec, vec, vec],
        out_specs=seq(0),
        out_shape=jax.ShapeDtypeStruct((bsz, L, W), BF16),
        compiler_params=pltpu.CompilerParams(
            dimension_semantics=("arbitrary",) * 3, vmem_limit_bytes=VMEM_LIMIT),
        name="rwkv_post",
    )(y_f, y_b, main, main, main, main, a_pre_f, a_pre_b, a0, k_a.reshape(1, W), r_k.reshape(1, W),
      lnx_w.reshape(1, W), lnx_b.reshape(1, W))


def _split_cols(t, sizes):
    offsets, acc = [], 0
    for s in sizes[:-1]:
        acc += s
        offsets.append(acc)
    return jnp.split(t, offsets, axis=-1)


def _adaln_kernel(c_ref, w_ref, b_ref, o_ref):
    cond = c_ref[...]
    act = cond * jax.nn.sigmoid(cond)
    o_ref[...] = jnp.dot(act, w_ref[...], precision=HI, preferred_element_type=F32) + b_ref[...]


def adaln(cond, w, b):
    rows, dm = cond.shape
    n = w.shape[1]
    rp = -(-rows // 8) * 8
    tn = 512
    m = pl.pallas_call(
        _adaln_kernel,
        grid=(n // tn,),
        in_specs=[pl.BlockSpec((rp, dm), lambda j: (0, 0)),
                  pl.BlockSpec((dm, tn), lambda j: (0, j)),
                  pl.BlockSpec((1, tn), lambda j: (0, j))],
        out_specs=pl.BlockSpec((rp, tn), lambda j: (0, j)),
        out_shape=jax.ShapeDtypeStruct((rp, n), F32),
        compiler_params=pltpu.CompilerParams(dimension_semantics=("arbitrary",), vmem_limit_bytes=VMEM_LIMIT),
        name="adaln",
    )(jnp.pad(cond, ((0, rp - rows), (0, 0))), w, b.reshape(1, n))[:rows]
    return jnp.split(m, 3, axis=-1)


def _grid_pos_embed(n_tokens):
    rows = n_tokens // GRID_W
    row_id = jnp.broadcast_to(jnp.arange(rows, dtype=F32)[:, None], (rows, GRID_W)).reshape(-1)
    col_id = jnp.broadcast_to(jnp.arange(GRID_W, dtype=F32)[None, :], (rows, GRID_W)).reshape(-1)
    quarter = D_MODEL // 4
    omega = 1.0 / (POS_BASE ** (jnp.arange(quarter, dtype=F32) / quarter))

    def axis_emb(pos):
        ang = pos[:, None] * omega[None, :]
        return jnp.concatenate([jnp.sin(ang), jnp.cos(ang)], axis=-1)
    return jnp.concatenate([axis_emb(row_id), axis_emb(col_id)], axis=-1)


def _even_mixer(x, gate, h, s5_re0, s5_im0, gla0, w_in, w_out, s5_ops, glu_w, glu_b, dec_up, dec_b, gla_nw):
    bsz, L, _ = h.shape
    n_main = sum(EVEN_SIZES[:-1])
    main = _mm3(h, w_in[:, :n_main])
    w_tail = jnp.pad(w_in[:, n_main:], ((0, 0), (0, LANES - N_DIR * GLA_RANK)))
    dec_lr = _mm3(h, w_tail)
    gy, fin_re, fin_im = s5_scan(main[..., :S5_W], s5_ops, s5_re0, s5_im0)
    gy = gy.reshape(bsz * L, S5_W)
    mixed = matmul_glu(gy, glu_w, glu_b, main.reshape(bsz * L, n_main), S5_W, S5_W + GLA_DV_W)
    mixed, fin_gla = gla_mix(main, dec_lr, dec_up, dec_b, gla_nw, gla0, mixed.reshape(bsz, L, -1))
    return matmul_gated_residual(mixed, w_out, x, gate), fin_re, fin_im, fin_gla


def _odd_mixer(x, gate, h, rwkv0, w_in, w_out, mu, w0, w2, a0, a2, k_k, k_a, r_k, lnx_w, lnx_b):
    bsz, L, _ = h.shape
    zero = jnp.zeros_like(h[:, :1])
    h_prev = jnp.concatenate([zero, h[:, :-1]], axis=1)
    h_next = jnp.concatenate([h[:, 1:], zero], axis=1)
    xs = h + mu[0] * (h_prev - h) + mu[1] * (h_next - h)
    n_main = sum(ODD_SIZES[:4])
    main = _mm3(xs, w_in[:, :n_main])
    tail = _mm3(xs, w_in[:, n_main:])
    w_lr, a_lr = _split_cols(tail, ODD_SIZES[4:])
    w_lr = jnp.tanh(w_lr).reshape(bsz, L, N_DIR, RWKV_DECAY_RANK)
    a_lr = a_lr.reshape(bsz, L, N_DIR, RWKV_ICLR_RANK)
    ys, a_pres, finals = [], [], []
    for d in range(N_DIR):
        w_pre = _mm3(w_lr[:, :, d], w2[d])
        a_pre = _mm3(a_lr[:, :, d], a2[d])
        y_d, fin = rwkv_direction(bool(d), main, w_pre, a_pre, w0[d], a0[d], k_k, k_a, rwkv0[:, d])
        ys.append(y_d)
        a_pres.append(a_pre)
        finals.append(fin)
    out = rwkv_post(ys[0], ys[1], main, a_pres[0], a_pres[1], a0, k_a, r_k.reshape(-1), lnx_w, lnx_b)
    return matmul_gated_residual(out, w_out, x, gate), jnp.stack(finals, axis=1)


def kernel(x_prompt, x_sample, state_s5_re, state_s5_im, state_gla, state_rwkv, c, c_ctx, norm_w, ada_w, ada_b, final_norm_w, e_w_in, e_w_out, s5_lambda_re, s5_lambda_im, s5_log_step, s5_b_re, s5_b_im, s5_c_re, s5_c_im, s5_d, s5_glu_w, s5_glu_b, gla_decay_up, gla_decay_b, gla_norm_w, o_w_in, o_w_out, rwkv_mu, rwkv_w0, rwkv_w2, rwkv_a0, rwkv_a2, rwkv_k_k, rwkv_k_a, rwkv_r_k, rwkv_lnx_w, rwkv_lnx_b):
    bp = x_prompt.shape[0]
    depth = norm_w.shape[0]
    x_ctx = x_prompt
    x_lat = x_sample + _grid_pos_embed(x_sample.shape[1])[None]
    z_s5 = jnp.zeros((bp, N_DIR, S5_GROUPS, S5_STATE), F32)
    z_rwkv = jnp.zeros((bp, N_DIR, RWKV_HEADS, RWKV_HEAD, RWKV_HEAD), F32)
    new_s5_re, new_s5_im, new_gla, new_rwkv = [], [], [], []
    n_lat = c.shape[0]
    cond = jnp.concatenate([c, c_ctx[None]], axis=0)
    for i in range(depth):
        j = i // 2
        shift, scale, gate = adaln(cond, ada_w[i], ada_b[i])
        gt_l, gt_c = gate[:n_lat], jnp.broadcast_to(gate[n_lat:], (bp, D_MODEL))
        h_ctx = norm_mod(x_ctx, norm_w[i], scale[n_lat:], shift[n_lat:])
        h_lat = norm_mod(x_lat, norm_w[i], scale[:n_lat], shift[:n_lat])
        if i % 2 == 0:
            s5_ops = s5_operators(s5_lambda_re[j], s5_lambda_im[j], s5_log_step[j], s5_b_re[j], s5_b_im[j],
                                  s5_c_re[j], s5_c_im[j], s5_d[j])
            p = (e_w_in[j], e_w_out[j], s5_ops, s5_glu_w[j], s5_glu_b[j], gla_decay_up[j], gla_decay_b[j],
                 gla_norm_w[j])
            x_ctx, fr, fi, fg = _even_mixer(x_ctx, gt_c, h_ctx, z_s5, z_s5, None, *p)
            x_lat, _, _, _ = _even_mixer(x_lat, gt_l, h_lat, state_s5_re[:, j], state_s5_im[:, j],
                                         state_gla[:, j], *p)
            new_s5_re.append(fr)
            new_s5_im.append(fi)
            new_gla.append(fg)
        else:
            p = (o_w_in[j], o_w_out[j], rwkv_mu[j], rwkv_w0[j], rwkv_w2[j], rwkv_a0[j], rwkv_a2[j],
                 rwkv_k_k[j], rwkv_k_a[j], rwkv_r_k[j], rwkv_lnx_w[j], rwkv_lnx_b[j])
            x_ctx, fw = _odd_mixer(x_ctx, gt_c, h_ctx, z_rwkv, *p)
            x_lat, _ = _odd_mixer(x_lat, gt_l, h_lat, state_rwkv[:, j], *p)
            new_rwkv.append(fw)
    y_prompt = final_norm(x_ctx, final_norm_w)
    y_sample = final_norm(x_lat, final_norm_w)
    return (y_prompt, y_sample, jnp.stack(new_s5_re, axis=1), jnp.stack(new_s5_im, axis=1),
            jnp.stack(new_gla, axis=1), jnp.stack(new_rwkv, axis=1))
```

```python
import functools

import jax
import jax.numpy as jnp
from jax import lax
from jax.experimental import pallas as pl
from jax.experimental.pallas import tpu as pltpu

D_MODEL = 2048
GRID_W = 64
POS_BASE = 10000.0
N_DIR = 2
EPS = 1e-6
S5_W = 1024
S5_GROUP_CH = 16
S5_GROUPS = 64
S5_STATE = 64
S5_CHUNK = 16
S5_TILE_GROUPS = 8
GLA_HEADS = 6
GLA_DV = 512
GLA_DK = 256
GLA_DK_W = 1536
GLA_DV_W = 3072
GLA_RANK = 16
GLA_NORMALIZER = 16.0
GLA_CHUNK = 64
GLA_NC = 16
GLA_LOG_DECAY_MIN = -1.0
EVEN_SIZES = (S5_W, S5_W, GLA_DK_W, GLA_DK_W, GLA_DV_W, GLA_DV_W, N_DIR * GLA_RANK)
RWKV_W = 2048
RWKV_HEAD = 64
RWKV_HEADS = 32
RWKV_DECAY_RANK = 96
RWKV_ICLR_RANK = 96
RWKV_LNX_EPS = 64e-5
ODD_SIZES = (RWKV_W, RWKV_W, RWKV_W, RWKV_W, N_DIR * RWKV_DECAY_RANK, N_DIR * RWKV_ICLR_RANK)
RWKV_CHUNK = 64
RWKV_CPAIRS = 16
RWKV_SUB = 16
LANES = 128

VMEM_LIMIT = 48 * 1024 * 1024
HI = lax.Precision.HIGHEST
BF16 = jnp.bfloat16
F32 = jnp.float32


def _mm_kernel(x_ref, w_ref, o_ref):
    o_ref[...] = jnp.dot(x_ref[...], w_ref[...], preferred_element_type=F32)


def _pick(n, prefs):
    for p in prefs:
        if n % p == 0:
            return p
    return n


def matmul(x, w):
    m, k = x.shape
    n = w.shape[1]
    x = x.astype(BF16)
    w = w.astype(BF16)
    tm = _pick(m, (1024, 512, 256, 128, 64, 32, 16, 8))
    tn = _pick(n, (1024, 512, 384, 256, 128))
    return pl.pallas_call(
        _mm_kernel,
        grid=(m // tm, n // tn),
        in_specs=[pl.BlockSpec((tm, k), lambda i, j: (i, 0)),
                  pl.BlockSpec((k, tn), lambda i, j: (0, j))],
        out_specs=pl.BlockSpec((tm, tn), lambda i, j: (i, j)),
        out_shape=jax.ShapeDtypeStruct((m, n), F32),
        compiler_params=pltpu.CompilerParams(
            dimension_semantics=("arbitrary", "arbitrary"), vmem_limit_bytes=VMEM_LIMIT),
        name="proj_matmul",
    )(x, w)


def _mm3(h, w):
    b, l, k = h.shape
    return matmul(h.reshape(b * l, k), w).reshape(b, l, -1)


def _mm_residual_kernel(x_ref, w_ref, res_ref, gate_ref, o_ref):
    acc = jnp.dot(x_ref[...], w_ref[...], preferred_element_type=F32)
    o_ref[...] = res_ref[...] + gate_ref[0] * acc


def matmul_gated_residual(x, w, res, gate):
    bsz, L, k = x.shape
    n = w.shape[1]
    m = bsz * L
    tm = _pick(L, (1024, 512, 256, 128))
    tn = _pick(n, (1024, 512, 256, 128) if k <= 2048 else (512, 256, 128))
    per_b = L // tm
    out = pl.pallas_call(
        _mm_residual_kernel,
        grid=(m // tm, n // tn),
        in_specs=[pl.BlockSpec((tm, k), lambda i, j: (i, 0)),
                  pl.BlockSpec((k, tn), lambda i, j: (0, j)),
                  pl.BlockSpec((tm, tn), lambda i, j: (i, j)),
                  pl.BlockSpec((1, 1, tn), lambda i, j: (i // per_b, 0, j))],
        out_specs=pl.BlockSpec((tm, tn), lambda i, j: (i, j)),
        out_shape=jax.ShapeDtypeStruct((m, n), F32),
        compiler_params=pltpu.CompilerParams(
            dimension_semantics=("arbitrary", "arbitrary"), vmem_limit_bytes=VMEM_LIMIT),
        name="proj_residual",
    )(x.reshape(m, k).astype(BF16), w.astype(BF16), res.reshape(m, n), gate.reshape(bsz, 1, n))
    return out.reshape(bsz, L, n)


def _mm_glu_kernel(x_ref, w_ref, b_ref, g_ref, xt_ref, o_ref):
    acc = jnp.dot(x_ref[...], w_ref[...], preferred_element_type=F32) + b_ref[...]
    gy = xt_ref[...].astype(F32)
    gate = g_ref[...]
    o_ref[...] = (gy * jax.nn.sigmoid(acc) * (gate * jax.nn.sigmoid(gate))).astype(o_ref.dtype)


def matmul_glu(gy, w, b, main, g_col0, n_total):
    m, k = gy.shape
    n = w.shape[1]
    tm = _pick(m, (1024, 512, 256, 128))
    tn = 512
    return pl.pallas_call(
        _mm_glu_kernel,
        grid=(m // tm, n // tn),
        in_specs=[pl.BlockSpec((tm, k), lambda i, j: (i, 0)),
                  pl.BlockSpec((k, tn), lambda i, j: (0, j)),
                  pl.BlockSpec((1, tn), lambda i, j: (0, j)),
                  pl.BlockSpec((tm, tn), lambda i, j: (i, g_col0 // tn + j)),
                  pl.BlockSpec((tm, tn), lambda i, j: (i, j))],
        out_specs=pl.BlockSpec((tm, tn), lambda i, j: (i, j)),
        out_shape=jax.ShapeDtypeStruct((m, n_total), BF16),
        compiler_params=pltpu.CompilerParams(
            dimension_semantics=("arbitrary", "arbitrary"), vmem_limit_bytes=VMEM_LIMIT),
        name="s5_glu_gate",
    )(gy, w.astype(BF16), b.reshape(1, n), main, gy)


def _norm_mod_kernel(x_ref, nw_ref, sc_ref, sh_ref, o_ref):
    x = x_ref[0]
    inv = lax.rsqrt(jnp.mean(x * x, axis=-1, keepdims=True) + EPS)
    o_ref[0] = (x * inv * nw_ref[...] * (1.0 + sc_ref[0]) + sh_ref[0]).astype(o_ref.dtype)


def norm_mod(x, nw, scale, shift):
    bsz, L, dm = x.shape
    tr = _pick(L, (256, 128, 64))
    nb = scale.shape[0]
    cond = pl.BlockSpec((1, 1, dm), lambda b, i: (b if nb > 1 else 0, 0, 0))
    return pl.pallas_call(
        _norm_mod_kernel,
        grid=(bsz, L // tr),
        in_specs=[pl.BlockSpec((1, tr, dm), lambda b, i: (b, i, 0)),
                  pl.BlockSpec((1, dm), lambda b, i: (0, 0)), cond, cond],
        out_specs=pl.BlockSpec((1, tr, dm), lambda b, i: (b, i, 0)),
        out_shape=jax.ShapeDtypeStruct((bsz, L, dm), BF16),
        compiler_params=pltpu.CompilerParams(
            dimension_semantics=("arbitrary", "arbitrary"), vmem_limit_bytes=VMEM_LIMIT),
        name="norm_mod",
    )(x, nw.reshape(1, dm), scale.reshape(nb, 1, dm), shift.reshape(nb, 1, dm))


def _final_norm_kernel(x_ref, nw_ref, o_ref):
    x = x_ref[0]
    o_ref[0] = x * lax.rsqrt(jnp.mean(x * x, axis=-1, keepdims=True) + EPS) * nw_ref[...]


def final_norm(x, nw):
    bsz, L, dm = x.shape
    tr = _pick(L, (256, 128, 64))
    return pl.pallas_call(
        _final_norm_kernel,
        grid=(bsz, L // tr),
        in_specs=[pl.BlockSpec((1, tr, dm), lambda b, i: (b, i, 0)), pl.BlockSpec((1, dm), lambda b, i: (0, 0))],
        out_specs=pl.BlockSpec((1, tr, dm), lambda b, i: (b, i, 0)),
        out_shape=jax.ShapeDtypeStruct((bsz, L, dm), F32),
        compiler_params=pltpu.CompilerParams(
            dimension_semantics=("arbitrary", "arbitrary"), vmem_limit_bytes=VMEM_LIMIT),
        name="final_norm",
    )(x, nw.reshape(1, dm))


def s5_operators(lam_re, lam_im, log_step, b_re, b_im, c_re, c_im, d_skip):
    T = S5_CHUNK
    dt = jnp.exp(log_step)[..., None]
    mag = jnp.exp(lam_re * dt)
    ab_re, ab_im = mag * jnp.cos(lam_im * dt), mag * jnp.sin(lam_im * dt)
    den = lam_re * lam_re + lam_im * lam_im
    f_re = ((ab_re - 1.0) * lam_re + ab_im * lam_im) / den
    f_im = (ab_im * lam_re - (ab_re - 1.0) * lam_im) / den
    bb_re = f_re[..., None] * b_re - f_im[..., None] * b_im
    bb_im = f_re[..., None] * b_im + f_im[..., None] * b_re
    kk = jnp.arange(T + 1, dtype=F32)[:, None, None, None]
    pmag = jnp.exp(kk * (lam_re * dt))
    pr = pmag * jnp.cos(kk * (lam_im * dt))
    pi = pmag * jnp.sin(kk * (lam_im * dt))
    zr = pr[:T, :, :, :, None] * bb_re - pi[:T, :, :, :, None] * bb_im
    zi = pr[:T, :, :, :, None] * bb_im + pi[:T, :, :, :, None] * bb_re
    kern = (jnp.einsum('dghp,kdgpj->kdghj', c_re, zr, precision=HI)
            - jnp.einsum('dghp,kdgpj->kdghj', c_im, zi, precision=HI))
    t_idx = jnp.arange(T)[:, None]
    s_idx = jnp.arange(T)[None, :]
    lag_f = t_idx - s_idx
    lag_b = s_idx - t_idx
    m_f = jnp.where((lag_f >= 0)[:, :, None, None, None], kern[:, 0][jnp.clip(lag_f, 0, T - 1)], 0.0)
    m_b = jnp.where((lag_b >= 0)[:, :, None, None, None], kern[:, 1][jnp.clip(lag_b, 0, T - 1)], 0.0)
    m = m_f + m_b
    eye_t = jnp.eye(T, dtype=F32)[:, :, None, None, None]
    eye_h = jnp.eye(S5_GROUP_CH, dtype=F32)[None, None, None]
    m = m + eye_t * eye_h * d_skip.reshape(S5_GROUPS, S5_GROUP_CH)[None, None, :, :, None]
    g = m.shape[2]
    m_t = m.transpose(2, 1, 4, 0, 3).reshape(g, T * S5_GROUP_CH, T * S5_GROUP_CH)
    pf_r, pf_i = pr[T - 1::-1][:T, 0], pi[T - 1::-1][:T, 0]
    pb_r, pb_i = pr[:T, 1], pi[:T, 1]

    def f_mat(p_r, p_i, d):
        re = p_r[..., None] * bb_re[d][None] - p_i[..., None] * bb_im[d][None]
        im = p_r[..., None] * bb_im[d][None] + p_i[..., None] * bb_re[d][None]
        re = re.transpose(1, 0, 3, 2).reshape(g, T * S5_GROUP_CH, S5_STATE)
        im = im.transpose(1, 0, 3, 2).reshape(g, T * S5_GROUP_CH, S5_STATE)
        return re, im
    ff_re, ff_im = f_mat(pf_r, pf_i, 0)
    fb_re, fb_im = f_mat(pb_r, pb_i, 1)
    a_t = jnp.concatenate([m_t, ff_re, fb_re, ff_im, fb_im], axis=-1)
    ef_r, ef_i = pr[1:T + 1, 0], pi[1:T + 1, 0]
    eb_r, eb_i = pr[T:0:-1, 1], pi[T:0:-1, 1]

    def e_mat(p_r, p_i, d):
        er = c_re[d][None] * p_r[:, :, None, :] - c_im[d][None] * p_i[:, :, None, :]
        ei = -(c_re[d][None] * p_i[:, :, None, :] + c_im[d][None] * p_r[:, :, None, :])
        er = er.transpose(1, 3, 0, 2).reshape(g, S5_STATE, T * S5_GROUP_CH)
        ei = ei.transpose(1, 3, 0, 2).reshape(g, S5_STATE, T * S5_GROUP_CH)
        return er, ei
    efr, efi = e_mat(ef_r, ef_i, 0)
    ebr, ebi = e_mat(eb_r, eb_i, 1)
    e_t = jnp.concatenate([efr, ebr, efi, ebi], axis=1)
    lam_t = jnp.concatenate([pr[T, 0], pr[T, 1], pi[T, 0], pi[T, 1]], axis=-1)[:, None, :]
    return a_t.astype(BF16), e_t.astype(BF16), lam_t


def _s5_kernel(n_chunks, bsz, x8_ref, sel_ref, at_ref, et_ref, lam_ref, h0_ref, y_ref, hfin_ref, z_ref, hent_ref):
    P = S5_STATE
    ut = jnp.dot(x8_ref[0], sel_ref[0], preferred_element_type=F32).astype(BF16)
    z_ref[...] = jnp.dot(ut, at_ref[0], preferred_element_type=F32)
    lam = lam_ref[0]
    a_re, a_im = lam[:, 0:2 * P], lam[:, 2 * P:4 * P]
    h0 = h0_ref[0]
    fwd_lanes = lax.broadcasted_iota(jnp.int32, (bsz, 2 * P), 1) < P

    def step(c, carry):
        h_re, h_im = carry
        rf = pl.ds(pl.multiple_of(c * bsz, 8), bsz)
        rb = pl.ds(pl.multiple_of((n_chunks - 1 - c) * bsz, 8), bsz)
        hent_ref[rf, 0:P] = h_re[:, 0:P]
        hent_ref[rb, P:2 * P] = h_re[:, P:2 * P]
        hent_ref[rf, 2 * P:3 * P] = h_im[:, 0:P]
        hent_ref[rb, 3 * P:4 * P] = h_im[:, P:2 * P]
        g_re = jnp.where(fwd_lanes, z_ref[rf, 4 * P:6 * P], z_ref[rb, 4 * P:6 * P])
        g_im = jnp.where(fwd_lanes, z_ref[rf, 6 * P:8 * P], z_ref[rb, 6 * P:8 * P])
        return a_re * h_re - a_im * h_im + g_re, a_re * h_im + a_im * h_re + g_im
    h_re, h_im = lax.fori_loop(0, n_chunks, step, (h0[:, 0:2 * P], h0[:, 2 * P:4 * P]))
    hfin_ref[0, :, 0:2 * P] = h_re
    hfin_ref[0, :, 2 * P:4 * P] = h_im
    y = z_ref[:, 0:4 * P] + jnp.dot(hent_ref[...].astype(BF16), et_ref[0], preferred_element_type=F32)
    y_ref[0] = jax.nn.gelu(y).astype(y_ref.dtype)


def _s5_unpack_kernel(yt_ref, selt_ref, o_ref):
    acc = jnp.dot(yt_ref[0], selt_ref[0], preferred_element_type=F32)
    for gl in range(1, S5_TILE_GROUPS):
        acc = acc + jnp.dot(yt_ref[gl], selt_ref[gl], preferred_element_type=F32)
    o_ref[0] = acc.astype(o_ref.dtype)


def s5_scan(u, ops, h0_re, h0_im):
    a_t, e_t, lam_t = ops
    b_real, L, _ = u.shape
    T, G, H, P = S5_CHUNK, S5_GROUPS, S5_GROUP_CH, S5_STATE
    TG = S5_TILE_GROUPS
    n = L // T
    bsz = -(-b_real // 8) * 8
    cols = n * bsz
    x8 = u.reshape(b_real, n, T, G // TG, LANES).transpose(3, 1, 0, 2, 4).astype(BF16)
    x8 = jnp.pad(x8, ((0, 0), (0, 0), (0, bsz - b_real), (0, 0), (0, 0))).reshape(G // TG, cols, T * LANES)
    src = jnp.arange(T * LANES)
    dst = jnp.arange(T * H)
    sel = ((src[None, :, None] // LANES == dst[None, None, :] // H)
           & (src[None, :, None] % H == dst[None, None, :] % H)
           & ((src[None, :, None] % LANES) // H == jnp.arange(TG)[:, None, None])).astype(BF16)
    h0 = jnp.concatenate([h0_re[:, 0], h0_re[:, 1], h0_im[:, 0], h0_im[:, 1]], axis=-1)
    h0 = jnp.pad(h0.transpose(1, 0, 2), ((0, 0), (0, bsz - b_real), (0, 0)))
    yt, hfin = pl.pallas_call(
        functools.partial(_s5_kernel, n, bsz),
        grid=(G,),
        in_specs=[pl.BlockSpec((1, cols, T * LANES), lambda g: (g // TG, 0, 0)),
                  pl.BlockSpec((1, T * LANES, T * H), lambda g: (g % TG, 0, 0)),
                  pl.BlockSpec((1, T * H, 8 * P), lambda g: (g, 0, 0)),
                  pl.BlockSpec((1, 4 * P, T * H), lambda g: (g, 0, 0)),
                  pl.BlockSpec((1, 1, 4 * P), lambda g: (g, 0, 0)),
                  pl.BlockSpec((1, bsz, 4 * P), lambda g: (g, 0, 0))],
        out_specs=[pl.BlockSpec((1, cols, T * H), lambda g: (g, 0, 0)),
                   pl.BlockSpec((1, bsz, 4 * P), lambda g: (g, 0, 0))],
        out_shape=[jax.ShapeDtypeStruct((G, cols, T * H), BF16),
                   jax.ShapeDtypeStruct((G, bsz, 4 * P), F32)],
        scratch_shapes=[pltpu.VMEM((cols, 8 * P), F32), pltpu.VMEM((cols, 4 * P), F32)],
        compiler_params=pltpu.CompilerParams(dimension_semantics=("arbitrary",), vmem_limit_bytes=VMEM_LIMIT),
        name="s5_chunk_scan",
    )(x8, sel, a_t, e_t, lam_t, h0)
    tr = _pick(cols, (512, 256, 128))
    y8 = pl.pallas_call(
        _s5_unpack_kernel,
        grid=(G // TG, cols // tr),
        in_specs=[pl.BlockSpec((TG, tr, T * H), lambda t, i: (t, i, 0)),
                  pl.BlockSpec((TG, T * H, T * LANES), lambda t, i: (0, 0, 0))],
        out_specs=pl.BlockSpec((1, tr, T * LANES), lambda t, i: (t, i, 0)),
        out_shape=jax.ShapeDtypeStruct((G // TG, cols, T * LANES), BF16),
        compiler_params=pltpu.CompilerParams(
            dimension_semantics=("arbitrary", "arbitrary"), vmem_limit_bytes=VMEM_LIMIT),
        name="s5_unpack",
    )(yt, sel.transpose(0, 2, 1))
    y = y8.reshape(G // TG, n, bsz, T, LANES)[:, :, :b_real].transpose(2, 1, 3, 0, 4).reshape(b_real, L, G * H)
    hfin = hfin[:, :b_real].transpose(1, 0, 2)
    fin_re = jnp.stack([hfin[..., 0:P], hfin[..., P:2 * P]], axis=1)
    fin_im = jnp.stack([hfin[..., 2 * P:3 * P], hfin[..., 3 * P:4 * P]], axis=1)
    return y, fin_re, fin_im


def _dot_t(a, b):
    return lax.dot_general(a, b, (((1,), (1,)), ((), ())), preferred_element_type=F32)


def _dot_mask(mask_bf16, x, x_rows_to_sublanes=False):
    def d(b):
        if x_rows_to_sublanes:
            return lax.dot_general(b, mask_bf16, (((0,), (0,)), ((), ())), preferred_element_type=F32)
        return jnp.dot(mask_bf16, b, preferred_element_type=F32)
    x1 = x.astype(BF16)
    r1 = x - x1.astype(F32)
    x2 = r1.astype(BF16)
    x3 = (r1 - x2.astype(F32)).astype(BF16)
    return d(x1) + (d(x2) + d(x3))


def _gla_block_kernel(n_blocks, NC, has_s0, q_ref, k_ref, v_ref, g_ref, lr_ref, up_ref, db_ref, nw_ref, dst_ref,
                      *refs):
    s0_ref = refs[0] if has_s0 else None
    out_ref, sfin_ref, s_ref, of_ref, qd_ref, ov_ref, kv_ref, dc_ref = refs[1:] if has_s0 else refs
    C = GLA_CHUNK
    R = C * NC
    d = pl.program_id(2)
    c = pl.program_id(3)
    bidx = jnp.where(d == 0, c, n_blocks - 1 - c)

    @pl.when(c == 0)
    def _():
        s_ref[...] = s0_ref[0, 0, 0] if has_s0 else jnp.zeros_like(s_ref)

    z = _mxu(lr_ref[0], up_ref[0], split=True) + db_ref[0]
    gc = jnp.maximum(jax.nn.log_sigmoid(z) * (1.0 / GLA_NORMALIZER), GLA_LOG_DECAY_MIN)
    row_c = lax.broadcasted_iota(jnp.int32, (C, C), 0)
    col_c = lax.broadcasted_iota(jnp.int32, (C, C), 1)
    seen_c = jnp.where(d == 0, row_c - col_c, col_c - row_c) >= 0
    seen_bf = seen_c.astype(BF16)
    rs = [slice(i * C, (i + 1) * C) for i in range(NC)]
    bcum_c = [_dot_mask(seen_bf, gc[r]) for r in rs]
    btot_c = [jnp.broadcast_to(jnp.where(d == 0, b[C - 1:C], b[0:1]), (C, GLA_DK)) for b in bcum_c]
    bcum = jnp.concatenate(bcum_c, axis=0)
    btot = jnp.concatenate(btot_c, axis=0)
    q_dec = (q_ref[0] * (GLA_DK ** -0.5) * jnp.exp(bcum)).astype(BF16)
    k = k_ref[0]
    k_inv = (k * jnp.exp(-bcum)).astype(BF16)
    k_end = (k * jnp.exp(btot - bcum)).astype(BF16)
    v = v_ref[0].astype(BF16)
    ones_c = jnp.ones((C, LANES), BF16)
    qd_ref[...] = q_dec.reshape(NC, C, GLA_DK)
    att = [jnp.where(seen_c, _dot_t(q_dec[r], k_inv[r]), 0.0).astype(BF16) for r in rs]
    for i in range(NC):
        kv_ref[i] = lax.dot_general(k_end[rs[i]], v[rs[i]], (((0,), (0,)), ((), ())), preferred_element_type=F32)
    for i in range(NC):
        ov_ref[i] = jnp.dot(att[i], v[rs[i]], preferred_element_type=F32)
    for i in range(NC):
        dc_ref[i] = _dot_mask(ones_c, gc[rs[i]], x_rows_to_sublanes=True)

    for i in range(NC):
        ci = jnp.where(d == 0, i, NC - 1 - i)
        s_old = s_ref[...]
        ov_ref[ci] = ov_ref[ci] + jnp.dot(qd_ref[ci], s_old.astype(BF16), preferred_element_type=F32)
        s_ref[...] = jnp.exp(dc_ref[ci][:, 0:1]) * s_old + kv_ref[ci]
    rows = pl.ds(pl.multiple_of(bidx * R, R), R)

    @pl.when(d == 0)
    def _():
        of_ref[rows, :] = ov_ref[...].reshape(R, GLA_DV)

    @pl.when(d == 1)
    def _():
        tot = of_ref[rows, :] + ov_ref[...].reshape(R, GLA_DV)
        nrm = tot * lax.rsqrt(jnp.mean(tot * tot, axis=-1, keepdims=True) + EPS) * nw_ref[0]
        gate = g_ref[0]
        out_ref[0] = (nrm * (gate * jax.nn.sigmoid(gate))).astype(out_ref.dtype)

    @pl.when(c == n_blocks - 1)
    def _():
        sfin_ref[0, 0, 0] = s_ref[...]


def gla_mix(main, dec_lr, dec_up, dec_b, gla_nw, s0, dst):
    bsz, L, _ = main.shape
    H, DK, DV = GLA_HEADS, GLA_DK, GLA_DV
    nc = min(GLA_NC, L // GLA_CHUNK)
    C = GLA_CHUNK * nc
    n = L // C
    q_blk = sum(EVEN_SIZES[:2]) // DK
    k_blk = sum(EVEN_SIZES[:3]) // DK
    v_blk = sum(EVEN_SIZES[:4]) // DV
    g_blk = sum(EVEN_SIZES[:5]) // DV
    up = jnp.zeros((N_DIR, LANES, GLA_DK_W), F32)
    for d in range(N_DIR):
        up = up.at[d, d * GLA_RANK:(d + 1) * GLA_RANK].set(dec_up[d])
    db = dec_b.reshape(N_DIR, 1, GLA_DK_W)
    nw = gla_nw.reshape(1, GLA_DV_W)

    def chunk(d, c):
        return c + d * (n - 1 - 2 * c)

    def out_chunk(d, c):
        return (n - 1) - d * c
    state = pl.BlockSpec((1, 1, 1, DK, DV), lambda b, h, d, c: (b, d, h, 0, 0))
    has_s0 = s0 is not None
    out, sfin = pl.pallas_call(
        functools.partial(_gla_block_kernel, n, nc, has_s0),
        grid=(bsz, H, N_DIR, n),
        in_specs=[pl.BlockSpec((1, C, DK), lambda b, h, d, c: (b, chunk(d, c), q_blk + h)),
                  pl.BlockSpec((1, C, DK), lambda b, h, d, c: (b, chunk(d, c), k_blk + h)),
                  pl.BlockSpec((1, C, DV), lambda b, h, d, c: (b, chunk(d, c), v_blk + h)),
                  pl.BlockSpec((1, C, DV), lambda b, h, d, c: (b, chunk(d, c), g_blk + h)),
                  pl.BlockSpec((1, C, LANES), lambda b, h, d, c: (b, chunk(d, c), 0)),
                  pl.BlockSpec((1, LANES, DK), lambda b, h, d, c: (d, 0, h)),
                  pl.BlockSpec((1, 1, DK), lambda b, h, d, c: (d, 0, h)),
                  pl.BlockSpec((1, DV), lambda b, h, d, c: (0, h)),
                  pl.BlockSpec(memory_space=pl.ANY)] + ([state] if has_s0 else []),
        input_output_aliases={8: 0},
        out_specs=[pl.BlockSpec((1, C, DV), lambda b, h, d, c: (b, out_chunk(d, c), S5_W // DV + h)), state],
        out_shape=[jax.ShapeDtypeStruct(dst.shape, BF16),
                   jax.ShapeDtypeStruct((bsz, N_DIR, H, DK, DV), F32)],
        scratch_shapes=[pltpu.VMEM((DK, DV), F32), pltpu.VMEM((L, DV), F32),
                        pltpu.VMEM((nc, GLA_CHUNK, DK), BF16), pltpu.VMEM((nc, GLA_CHUNK, DV), F32),
                        pltpu.VMEM((nc, DK, DV), F32), pltpu.VMEM((nc, DK, LANES), F32)],
        compiler_params=pltpu.CompilerParams(
            dimension_semantics=("arbitrary",) * 4, vmem_limit_bytes=VMEM_LIMIT),
        name="gla_chunk_scan",
    )(main, main, main, main, dec_lr, up, db, nw, dst, *([s0] if has_s0 else []))
    return out, sfin


def _split_bf16(x):
    hi = x.astype(BF16)
    return hi, (x - hi.astype(F32)).astype(BF16)


def _mxu(x, y, dims=(((1,), (0,)), ((), ())), split=False):
    def d(a, b):
        return lax.dot_general(a, b, dims, preferred_element_type=F32)
    if not split:
        return d(x.astype(BF16), y.astype(BF16))
    xh, xl = _split_bf16(x)
    yh, yl = _split_bf16(y)
    return d(xh, yh) + (d(xh, yl) + d(xl, yh))


def _rwkv_fs_kernel(n_chunks, rev, r_ref, k_ref, v_ref, wp_ref, ap_ref, w0_ref, a0_ref, kk_ref, ka_ref, h0_ref,
                    y_ref, hfin_ref, h_ref):
    T, N, SB = RWKV_CHUNK, RWKV_HEAD, RWKV_SUB
    NB = T // SB
    c = pl.program_id(2)

    @pl.when(c == 0)
    def _():
        h_ref[...] = h0_ref[0]

    lane = lax.broadcasted_iota(jnp.int32, (T, LANES), 1)
    row = lax.broadcasted_iota(jnp.int32, (T, LANES), 0)
    lo = lane < N
    col = lane % N
    order = (col - row) if rev else (row - col)
    seen = order >= 0
    before = order > 0
    eye = row == col
    sq_r = lax.broadcasted_iota(jnp.int32, (T, T), 0)
    sq_c = lax.broadcasted_iota(jnp.int32, (T, T), 1)
    seen_sq = (((sq_c - sq_r) if rev else (sq_r - sq_c)) >= 0).astype(BF16)
    same_head = ((lax.broadcasted_iota(jnp.int32, (LANES, LANES), 0) < N)
                 == (lax.broadcasted_iota(jnp.int32, (LANES, LANES), 1) < N)).astype(BF16)
    col_sb = lax.broadcasted_iota(jnp.int32, (SB, LANES), 1) % N
    row_dims = (((0,), (0,)), ((), ()))
    lane_dims = (((1,), (1,)), ((), ()))

    def bd(x):
        return jnp.concatenate([jnp.where(lo, x, 0.0), jnp.where(lo, 0.0, x)], axis=0)

    def pp(x, y, split=False):
        return _mxu(x, bd(y), split=split)

    def ptp(x, y):
        full = _mxu(x, y, row_dims)
        return jnp.where(lo, full[:N], full[N:])

    w_log = -jax.nn.softplus(-(wp_ref[0] + w0_ref[...])) - 0.5
    lw_all = -jnp.exp(w_log)
    iclr_all = jax.nn.sigmoid(ap_ref[0] + a0_ref[...])
    k_all = k_ref[0]
    kd_all = k_all * (1.0 + (iclr_all - 1.0) * ka_ref[...])
    kkr_all = k_all * kk_ref[...]
    cs_all = _dot_mask(seen_sq, lw_all)
    tot_all = jnp.sum(lw_all, axis=0, keepdims=True)
    pairs = range(RWKV_CPAIRS)
    sl = [slice(p * LANES, (p + 1) * LANES) for p in pairs]
    sq_hi = [_split_bf16(kkr_all[:, s] * kkr_all[:, s]) for s in sl]
    ssq = [jnp.dot(sq_hi[p][0], same_head, preferred_element_type=F32)
           + jnp.dot(sq_hi[p][1], same_head, preferred_element_type=F32) for p in pairs]
    kk = [kkr_all[:, sl[p]] / jnp.maximum(jnp.sqrt(ssq[p]), 1e-12) for p in pairs]
    b_in = [kk[p] * iclr_all[:, sl[p]] for p in pairs]
    cs = [cs_all[:, s] for s in sl]
    tot = [tot_all[:, s] for s in sl]
    e_out = [jnp.exp(-cs[p]) for p in pairs]
    at = [-kk[p] * jnp.exp(cs[p] - lw_all[:, sl[p]]) for p in pairs]
    rt = [r_ref[0, :, sl[p]] * jnp.exp(cs[p]) for p in pairs]
    ar = [jnp.concatenate([at[p], rt[p]], axis=0) for p in pairs]
    g1 = [_mxu(ar[p], bd(b_in[p] * e_out[p]), lane_dims) for p in pairs]
    g2 = [_mxu(ar[p], bd(kd_all[:, sl[p]] * e_out[p]), lane_dims) for p in pairs]
    a_ab = [jnp.where(before, g1[p][:T], 0.0) for p in pairs]
    a_rb = [jnp.where(seen, g1[p][T:], 0.0) for p in pairs]
    a_ak = [jnp.where(before, g2[p][:T], 0.0) for p in pairs]
    a_rk = [jnp.where(seen, g2[p][T:], 0.0) for p in pairs]
    v = [v_ref[0, :, sl[p]] for p in pairs]
    akv = [pp(a_ak[p], v[p]) for p in pairs]
    za = [[None] * NB for _ in pairs]
    zu = [[None] * NB for _ in pairs]
    zero_blk = jnp.zeros((SB, LANES), F32)
    for kpos in range(NB):
        bk = NB - 1 - kpos if rev else kpos
        rows = slice(bk * SB, (bk + 1) * SB)
        done = [(m > bk) if rev else (m < bk) for m in range(NB)]
        cur_a = [at[p][rows] for p in pairs]
        cur_u = [akv[p][rows] for p in pairs]
        if kpos > 0:
            for p in pairs:
                zc_a = jnp.concatenate([za[p][m] if done[m] else zero_blk for m in range(NB)], axis=0)
                zc_u = jnp.concatenate([zu[p][m] if done[m] else zero_blk for m in range(NB)], axis=0)
                off = _mxu(a_ab[p][rows], jnp.concatenate([bd(zc_a), bd(zc_u)], axis=1))
                cur_a[p] = cur_a[p] + off[:, :LANES]
                cur_u[p] = cur_u[p] + off[:, LANES:]
        abc = []
        for p in pairs:
            ablk = a_ab[p][rows]
            picked = jnp.concatenate([jnp.where(col_sb == bk * SB + s, ablk, 0.0) for s in range(SB)], axis=0)
            abc.append(jnp.dot(picked.astype(BF16), same_head, preferred_element_type=F32))
        ha = [[cur_a[p][:8], cur_a[p][8:]] for p in pairs]
        hu = [[cur_u[p][:8], cur_u[p][8:]] for p in pairs]
        for j in range(SB - 1):
            s = SB - 1 - j if rev else j
            src, r8 = s // 8, s % 8
            halves = (0, 1) if (s >= 8) == rev else ((0,) if rev else (1,))
            for p in pairs:
                row_a = ha[p][src][r8:r8 + 1]
                row_u = hu[p][src][r8:r8 + 1]
                for hf in halves:
                    coef = abc[p][s * SB + hf * 8:s * SB + hf * 8 + 8]
                    ha[p][hf] = ha[p][hf] + coef * row_a
                    hu[p][hf] = hu[p][hf] + coef * row_u
        for p in pairs:
            za[p][bk] = jnp.concatenate(ha[p], axis=0)
            zu[p][bk] = jnp.concatenate(hu[p], axis=0)
    a_hat = [jnp.concatenate(za[p], axis=0) for p in pairs]
    u_loc = [jnp.concatenate(zu[p], axis=0) for p in pairs]
    h0 = [h_ref[:, sl[p]] for p in pairs]
    q_hat = [rt[p] + pp(a_rb[p], a_hat[p]) for p in pairs]
    y_loc = [pp(a_rb[p], u_loc[p]) + pp(a_rk[p], v[p]) for p in pairs]
    for p in pairs:
        y_ref[0, :, sl[p]] = (pp(q_hat[p], h0[p]) + y_loc[p]).astype(y_ref.dtype)
    e_end = [jnp.exp(tot[p] - cs[p]) for p in pairs]
    bh = [b_in[p] * e_end[p] for p in pairs]
    p_end = [_split_bf16(jnp.where(eye, jnp.exp(tot[p]), 0.0)) for p in pairs]
    decay = [jnp.dot(p_end[p][0], same_head, preferred_element_type=F32)
             + jnp.dot(p_end[p][1], same_head, preferred_element_type=F32) for p in pairs]
    corr = [ptp(bh[p], a_hat[p]) for p in pairs]
    gam = [ptp(jnp.concatenate([bh[p], kd_all[:, sl[p]] * e_end[p]], axis=0),
               jnp.concatenate([u_loc[p], v[p]], axis=0)) for p in pairs]
    for p in pairs:
        h_ref[:, sl[p]] = decay[p] * h0[p] + (pp(corr[p], h0[p], split=True) + gam[p])

    @pl.when(c == n_chunks - 1)
    def _():
        hfin_ref[0] = h_ref[...]


def rwkv_direction(rev, main, w_pre, a_pre, w0, a0, k_k, k_a, s0):
    bsz, L, _ = main.shape
    W = RWKV_W
    T, N, H = RWKV_CHUNK, RWKV_HEAD, RWKV_HEADS
    n = L // T
    gw = RWKV_CPAIRS * LANES
    ng = W // gw
    h0 = s0.transpose(0, 3, 1, 2).reshape(bsz, N, W)

    def seq(col0):
        return pl.BlockSpec((1, T, gw), lambda b, g, c: (b, (n - 1 - c) if rev else c, col0 * ng + g))
    vec = pl.BlockSpec((1, gw), lambda b, g, c: (0, g))
    st = pl.BlockSpec((1, N, gw), lambda b, g, c: (b, 0, g))
    y, hfin = pl.pallas_call(
        functools.partial(_rwkv_fs_kernel, n, rev),
        grid=(bsz, ng, n),
        in_specs=[seq(0), seq(1), seq(2), seq(0), seq(0), vec, vec, vec, vec, st],
        out_specs=[seq(0), st],
        out_shape=[jax.ShapeDtypeStruct((bsz, L, W), BF16), jax.ShapeDtypeStruct((bsz, N, W), F32)],
        scratch_shapes=[pltpu.VMEM((N, gw), F32)],
        compiler_params=pltpu.CompilerParams(
            dimension_semantics=("arbitrary",) * 3, vmem_limit_bytes=VMEM_LIMIT),
        name="rwkv_bwd_chunks" if rev else "rwkv_fwd_chunks",
    )(main, main, main, w_pre, a_pre, w0.reshape(1, W), a0.reshape(1, W), k_k.reshape(1, W), k_a.reshape(1, W), h0)
    return y, hfin.reshape(bsz, N, H, N).transpose(0, 2, 3, 1)


def _segsum(x, same_head):
    x1, x2 = _split_bf16(x)
    return (jnp.dot(x1, same_head, preferred_element_type=F32)
            + jnp.dot(x2, same_head, preferred_element_type=F32))


def _rwkv_post_kernel(yf_ref, yb_ref, r_ref, k_ref, v_ref, g_ref, af_ref, ab_ref, a0_ref, ka_ref, rk_ref,
                      lw_ref, lb_ref, o_ref):
    N = RWKV_HEAD
    same_head = ((lax.broadcasted_iota(jnp.int32, (LANES, LANES), 0) < N)
                 == (lax.broadcasted_iota(jnp.int32, (LANES, LANES), 1) < N)).astype(BF16)
    for t in range(o_ref.shape[2] // LANES):
        ls = slice(t * LANES, (t + 1) * LANES)
        wkv = yf_ref[0, :, ls].astype(F32) + yb_ref[0, :, ls].astype(F32)
        mean = _segsum(wkv, same_head) * (1.0 / N)
        cen = wkv - mean
        var = _segsum(cen * cen, same_head) * (1.0 / N)
        ln = cen * lax.rsqrt(var + RWKV_LNX_EPS) * lw_ref[:, ls] + lb_ref[:, ls]
        ka = ka_ref[:, ls]
        k_mix = ((1.0 + (jax.nn.sigmoid(af_ref[0, :, ls] + a0_ref[0:1, ls]) - 1.0) * ka)
                 + (1.0 + (jax.nn.sigmoid(ab_ref[0, :, ls] + a0_ref[1:2, ls]) - 1.0) * ka))
        bonus = _segsum(r_ref[0, :, ls] * k_ref[0, :, ls] * k_mix * rk_ref[:, ls], same_head) * v_ref[0, :, ls]
        gate = g_ref[0, :, ls]
        o_ref[0, :, ls] = ((ln + bonus) * (gate * jax.nn.sigmoid(gate))).astype(o_ref.dtype)


def rwkv_post(y_f, y_b, main, a_pre_f, a_pre_b, a0, k_a, r_k, lnx_w, lnx_b):
    bsz, L, W = y_f.shape
    tr = _pick(L, (256, 128, 64))
    tw = 1024
    nw = W // tw

    def seq(col0):
        return pl.BlockSpec((1, tr, tw), lambda b, i, j: (b, i, col0 * nw + j))
    vec = pl.BlockSpec((1, tw), lambda b, i, j: (0, j))
    vec2 = pl.BlockSpec((N_DIR, tw), lambda b, i, j: (0, j))
    return pl.pallas_call(
        _rwkv_post_kernel,
        grid=(bsz, L // tr, nw),
        in_specs=[seq(0), seq(0), seq(0), seq(1), seq(2), seq(3), seq(0), seq(0), vec2, vec, vec, vec, vec],
        out_specs=seq(0),
        out_shape=jax.ShapeDtypeStruct((bsz, L, W), BF16),
        compiler_params=pltpu.CompilerParams(
            dimension_semantics=("arbitrary",) * 3, vmem_limit_bytes=VMEM_LIMIT),
        name="rwkv_post",
    )(y_f, y_b, main, main, main, main, a_pre_f, a_pre_b, a0, k_a.reshape(1, W), r_k.reshape(1, W),
      lnx_w.reshape(1, W), lnx_b.reshape(1, W))


def _split_cols(t, sizes):
    offsets, acc = [], 0
    for s in sizes[:-1]:
        acc += s
        offsets.append(acc)
    return jnp.split(t, offsets, axis=-1)


def _adaln_kernel(c_ref, w_ref, b_ref, o_ref):
    cond = c_ref[...]
    act = cond * jax.nn.sigmoid(cond)
    o_ref[...] = jnp.dot(act, w_ref[...], precision=HI, preferred_element_type=F32) + b_ref[...]


def adaln(cond, w, b):
    rows, dm = cond.shape
    n = w.shape[1]
    rp = -(-rows // 8) * 8
    tn = 512
    m = pl.pallas_call(
        _adaln_kernel,
        grid=(n // tn,),
        in_specs=[pl.BlockSpec((rp, dm), lambda j: (0, 0)),
                  pl.BlockSpec((dm, tn), lambda j: (0, j)),
                  pl.BlockSpec((1, tn), lambda j: (0, j))],
        out_specs=pl.BlockSpec((rp, tn), lambda j: (0, j)),
        out_shape=jax.ShapeDtypeStruct((rp, n), F32),
        compiler_params=pltpu.CompilerParams(dimension_semantics=("arbitrary",), vmem_limit_bytes=VMEM_LIMIT),
        name="adaln",
    )(jnp.pad(cond, ((0, rp - rows), (0, 0))), w, b.reshape(1, n))[:rows]
    return jnp.split(m, 3, axis=-1)


def _grid_pos_embed(n_tokens):
    rows = n_tokens // GRID_W
    row_id = jnp.broadcast_to(jnp.arange(rows, dtype=F32)[:, None], (rows, GRID_W)).reshape(-1)
    col_id = jnp.broadcast_to(jnp.arange(GRID_W, dtype=F32)[None, :], (rows, GRID_W)).reshape(-1)
    quarter = D_MODEL // 4
    omega = 1.0 / (POS_BASE ** (jnp.arange(quarter, dtype=F32) / quarter))

    def axis_emb(pos):
        ang = pos[:, None] * omega[None, :]
        return jnp.concatenate([jnp.sin(ang), jnp.cos(ang)], axis=-1)
    return jnp.concatenate([axis_emb(row_id), axis_emb(col_id)], axis=-1)


def _even_mixer(x, gate, h, s5_re0, s5_im0, gla0, w_in, w_out, s5_ops, glu_w, glu_b, dec_up, dec_b, gla_nw):
    bsz, L, _ = h.shape
    n_main = sum(EVEN_SIZES[:-1])
    main = _mm3(h, w_in[:, :n_main])
    w_tail = jnp.pad(w_in[:, n_main:], ((0, 0), (0, LANES - N_DIR * GLA_RANK)))
    dec_lr = _mm3(h, w_tail)
    gy, fin_re, fin_im = s5_scan(main[..., :S5_W], s5_ops, s5_re0, s5_im0)
    gy = gy.reshape(bsz * L, S5_W)
    mixed = matmul_glu(gy, glu_w, glu_b, main.reshape(bsz * L, n_main), S5_W, S5_W + GLA_DV_W)
    mixed, fin_gla = gla_mix(main, dec_lr, dec_up, dec_b, gla_nw, gla0, mixed.reshape(bsz, L, -1))
    return matmul_gated_residual(mixed, w_out, x, gate), fin_re, fin_im, fin_gla


def _odd_mixer(x, gate, h, rwkv0, w_in, w_out, mu, w0, w2, a0, a2, k_k, k_a, r_k, lnx_w, lnx_b):
    bsz, L, _ = h.shape
    zero = jnp.zeros_like(h[:, :1])
    h_prev = jnp.concatenate([zero, h[:, :-1]], axis=1)
    h_next = jnp.concatenate([h[:, 1:], zero], axis=1)
    xs = h + mu[0] * (h_prev - h) + mu[1] * (h_next - h)
    n_main = sum(ODD_SIZES[:4])
    main = _mm3(xs, w_in[:, :n_main])
    tail = _mm3(xs, w_in[:, n_main:])
    w_lr, a_lr = _split_cols(tail, ODD_SIZES[4:])
    w_lr = jnp.tanh(w_lr).reshape(bsz, L, N_DIR, RWKV_DECAY_RANK)
    a_lr = a_lr.reshape(bsz, L, N_DIR, RWKV_ICLR_RANK)
    ys, a_pres, finals = [], [], []
    for d in range(N_DIR):
        w_pre = _mm3(w_lr[:, :, d], w2[d])
        a_pre = _mm3(a_lr[:, :, d], a2[d])
        y_d, fin = rwkv_direction(bool(d), main, w_pre, a_pre, w0[d], a0[d], k_k, k_a, rwkv0[:, d])
        ys.append(y_d)
        a_pres.append(a_pre)
        finals.append(fin)
    out = rwkv_post(ys[0], ys[1], main, a_pres[0], a_pres[1], a0, k_a, r_k.reshape(-1), lnx_w, lnx_b)
    return matmul_gated_residual(out, w_out, x, gate), jnp.stack(finals, axis=1)


def kernel(x_prompt, x_sample, state_s5_re, state_s5_im, state_gla, state_rwkv, c, c_ctx, norm_w, ada_w, ada_b, final_norm_w, e_w_in, e_w_out, s5_lambda_re, s5_lambda_im, s5_log_step, s5_b_re, s5_b_im, s5_c_re, s5_c_im, s5_d, s5_glu_w, s5_glu_b, gla_decay_up, gla_decay_b, gla_norm_w, o_w_in, o_w_out, rwkv_mu, rwkv_w0, rwkv_w2, rwkv_a0, rwkv_a2, rwkv_k_k, rwkv_k_a, rwkv_r_k, rwkv_lnx_w, rwkv_lnx_b):
    bp = x_prompt.shape[0]
    depth = norm_w.shape[0]
    x_ctx = x_prompt
    x_lat = x_sample + _grid_pos_embed(x_sample.shape[1])[None]
    z_s5 = jnp.zeros((bp, N_DIR, S5_GROUPS, S5_STATE), F32)
    z_rwkv = jnp.zeros((bp, N_DIR, RWKV_HEADS, RWKV_HEAD, RWKV_HEAD), F32)
    new_s5_re, new_s5_im, new_gla, new_rwkv = [], [], [], []
    n_lat = c.shape[0]
    cond = jnp.concatenate([c, c_ctx[None]], axis=0)
    for i in range(depth):
        j = i // 2
        shift, scale, gate = adaln(cond, ada_w[i], ada_b[i])
        gt_l, gt_c = gate[:n_lat], jnp.broadcast_to(gate[n_lat:], (bp, D_MODEL))
        h_ctx = norm_mod(x_ctx, norm_w[i], scale[n_lat:], shift[n_lat:])
        h_lat = norm_mod(x_lat, norm_w[i], scale[:n_lat], shift[:n_lat])
        if i % 2 == 0:
            s5_ops = s5_operators(s5_lambda_re[j], s5_lambda_im[j], s5_log_step[j], s5_b_re[j], s5_b_im[j],
                                  s5_c_re[j], s5_c_im[j], s5_d[j])
            p = (e_w_in[j], e_w_out[j], s5_ops, s5_glu_w[j], s5_glu_b[j], gla_decay_up[j], gla_decay_b[j],
                 gla_norm_w[j])
            x_ctx, fr, fi, fg = _even_mixer(x_ctx, gt_c, h_ctx, z_s5, z_s5, None, *p)
            x_lat, _, _, _ = _even_mixer(x_lat, gt_l, h_lat, state_s5_re[:, j], state_s5_im[:, j],
                                         state_gla[:, j], *p)
            new_s5_re.append(fr)
            new_s5_im.append(fi)
            new_gla.append(fg)
        else:
            p = (o_w_in[j], o_w_out[j], rwkv_mu[j], rwkv_w0[j], rwkv_w2[j], rwkv_a0[j], rwkv_a2[j],
                 rwkv_k_k[j], rwkv_k_a[j], rwkv_r_k[j], rwkv_lnx_w[j], rwkv_lnx_b[j])
            x_ctx, fw = _odd_mixer(x_ctx, gt_c, h_ctx, z_rwkv, *p)
            x_lat, _ = _odd_mixer(x_lat, gt_l, h_lat, state_rwkv[:, j], *p)
            new_rwkv.append(fw)
    y_prompt = final_norm(x_ctx, final_norm_w)
    y_sample = final_norm(x_lat, final_norm_w)
    return (y_prompt, y_sample, jnp.stack(new_s5_re, axis=1), jnp.stack(new_s5_im, axis=1),
            jnp.stack(new_gla, axis=1), jnp.stack(new_rwkv, axis=1))
```

```python
import functools

import jax
import jax.numpy as jnp
from jax import lax
from jax.experimental import pallas as pl
from jax.experimental.pallas import tpu as pltpu

D_MODEL = 2048
GRID_W = 64
POS_BASE = 10000.0
N_DIR = 2
EPS = 1e-6
S5_W = 1024
S5_GROUP_CH = 16
S5_GROUPS = 64
S5_STATE = 64
S5_CHUNK = 16
S5_TILE_GROUPS = 8
GLA_HEADS = 6
GLA_DV = 512
GLA_DK = 256
GLA_DK_W = 1536
GLA_DV_W = 3072
GLA_RANK = 16
GLA_NORMALIZER = 16.0
GLA_CHUNK = 64
GLA_NC = 16
GLA_LOG_DECAY_MIN = -1.0
EVEN_SIZES = (S5_W, S5_W, GLA_DK_W, GLA_DK_W, GLA_DV_W, GLA_DV_W, N_DIR * GLA_RANK)
RWKV_W = 2048
RWKV_HEAD = 64
RWKV_HEADS = 32
RWKV_DECAY_RANK = 96
RWKV_ICLR_RANK = 96
RWKV_LNX_EPS = 64e-5
ODD_SIZES = (RWKV_W, RWKV_W, RWKV_W, RWKV_W, N_DIR * RWKV_DECAY_RANK, N_DIR * RWKV_ICLR_RANK)
RWKV_CHUNK = 64
RWKV_CPAIRS = 16
RWKV_SUB = 16
LANES = 128

VMEM_LIMIT = 48 * 1024 * 1024
HI = lax.Precision.HIGHEST
BF16 = jnp.bfloat16
F32 = jnp.float32


def _mm_kernel(x_ref, w_ref, o_ref):
    o_ref[...] = jnp.dot(x_ref[...], w_ref[...], preferred_element_type=F32)


def _pick(n, prefs):
    for p in prefs:
        if n % p == 0:
            return p
    return n


def matmul(x, w, n_cols=None):
    m, k = x.shape
    n = w.shape[1] if n_cols is None else n_cols
    x = x.astype(BF16)
    w = w.astype(BF16)
    tm = _pick(m, (1024, 512, 256, 128, 64, 32, 16, 8))
    tn = _pick(n, (1024, 512, 384, 256, 128))
    return pl.pallas_call(
        _mm_kernel,
        grid=(m // tm, n // tn),
        in_specs=[pl.BlockSpec((tm, k), lambda i, j: (i, 0)),
                  pl.BlockSpec((k, tn), lambda i, j: (0, j))],
        out_specs=pl.BlockSpec((tm, tn), lambda i, j: (i, j)),
        out_shape=jax.ShapeDtypeStruct((m, n), F32),
        compiler_params=pltpu.CompilerParams(
            dimension_semantics=("arbitrary", "arbitrary"), vmem_limit_bytes=VMEM_LIMIT),
        name="proj_matmul",
    )(x, w)


def _mm3(h, w, n_cols=None):
    b, l, k = h.shape
    return matmul(h.reshape(b * l, k), w, n_cols).reshape(b, l, -1)


def _mm_residual_kernel(has_add, x_ref, w_ref, res_ref, gate_ref, *refs):
    o_ref = refs[-1]
    acc = jnp.dot(x_ref[...], w_ref[...], preferred_element_type=F32)
    res = res_ref[...] + refs[0][...] if has_add else res_ref[...]
    o_ref[...] = res + gate_ref[0] * acc


def matmul_gated_residual(x, w, res, gate, res_add=None):
    bsz, L, k = x.shape
    n = w.shape[1]
    m = bsz * L
    tm = _pick(L, (1024, 512, 256, 128))
    tn = _pick(n, (1024, 512, 256, 128) if k <= 2048 else (512, 256, 128))
    per_b = L // tm
    in_specs = [pl.BlockSpec((tm, k), lambda i, j: (i, 0)),
                pl.BlockSpec((k, tn), lambda i, j: (0, j)),
                pl.BlockSpec((tm, tn), lambda i, j: (i, j)),
                pl.BlockSpec((1, 1, tn), lambda i, j: (i // per_b, 0, j))]
    args = [x.reshape(m, k).astype(BF16), w.astype(BF16), res.reshape(m, n), gate.reshape(bsz, 1, n)]
    if res_add is not None:
        in_specs.append(pl.BlockSpec((tm, tn), lambda i, j: (i % per_b, j)))
        args.append(res_add)
    out = pl.pallas_call(
        functools.partial(_mm_residual_kernel, res_add is not None),
        grid=(m // tm, n // tn),
        in_specs=in_specs,
        out_specs=pl.BlockSpec((tm, tn), lambda i, j: (i, j)),
        out_shape=jax.ShapeDtypeStruct((m, n), F32),
        compiler_params=pltpu.CompilerParams(
            dimension_semantics=("arbitrary", "arbitrary"), vmem_limit_bytes=VMEM_LIMIT),
        name="proj_residual",
    )(*args)
    return out.reshape(bsz, L, n)


def _mm_glu_kernel(x_ref, w_ref, b_ref, g_ref, xt_ref, o_ref):
    acc = jnp.dot(x_ref[...], w_ref[...], preferred_element_type=F32) + b_ref[...]
    gy = xt_ref[...].astype(F32)
    gate = g_ref[...]
    o_ref[...] = (gy * jax.nn.sigmoid(acc) * (gate * jax.nn.sigmoid(gate))).astype(o_ref.dtype)


def matmul_glu(gy, w, b, main, g_col0, n_total):
    m, k = gy.shape
    n = w.shape[1]
    tm = _pick(m, (1024, 512, 256, 128))
    tn = 512
    return pl.pallas_call(
        _mm_glu_kernel,
        grid=(m // tm, n // tn),
        in_specs=[pl.BlockSpec((tm, k), lambda i, j: (i, 0)),
                  pl.BlockSpec((k, tn), lambda i, j: (0, j)),
                  pl.BlockSpec((1, tn), lambda i, j: (0, j)),
                  pl.BlockSpec((tm, tn), lambda i, j: (i, g_col0 // tn + j)),
                  pl.BlockSpec((tm, tn), lambda i, j: (i, j))],
        out_specs=pl.BlockSpec((tm, tn), lambda i, j: (i, j)),
        out_shape=jax.ShapeDtypeStruct((m, n_total), BF16),
        compiler_params=pltpu.CompilerParams(
            dimension_semantics=("arbitrary", "arbitrary"), vmem_limit_bytes=VMEM_LIMIT),
        name="s5_glu_gate",
    )(gy, w.astype(BF16), b.reshape(1, n), main, gy)


def _norm_mod_kernel(has_add, has_shift, n_row_blocks, x_ref, nw_ref, sc_ref, sh_ref, *refs):
    o_ref = refs[-1]

    def modulated(x):
        inv = lax.rsqrt(jnp.mean(x * x, axis=-1, keepdims=True) + EPS)
        return x * inv * nw_ref[...] * (1.0 + sc_ref[0]) + sh_ref[0]
    x = x_ref[0]
    if has_add:
        x = x + refs[0][...]
    h = modulated(x)
    if has_shift:
        prev_ref, next_ref, mu_ref = refs[0], refs[1], refs[2]
        i = pl.program_id(1)
        tr = h.shape[0]
        row = lax.broadcasted_iota(jnp.int32, h.shape, 0)
        before = jnp.where(i > 0, modulated(prev_ref[0])[7:8], 0.0)
        after = jnp.where(i < n_row_blocks - 1, modulated(next_ref[0])[0:1], 0.0)
        h_prev = jnp.where(row == 0, before, pltpu.roll(h, 1, 0))
        h_next = jnp.where(row == tr - 1, after, pltpu.roll(h, tr - 1, 0))
        h = h + mu_ref[0:1] * (h_prev - h) + mu_ref[1:2] * (h_next - h)
    o_ref[0] = h.astype(o_ref.dtype)


def norm_mod(x, nw, scale, shift, add=None, mu=None):
    assert add is None or mu is None
    bsz, L, dm = x.shape
    tr = _pick(L, (256, 128, 64))
    nb = scale.shape[0]
    nblk = L // tr
    cond = pl.BlockSpec((1, 1, dm), lambda b, i: (b if nb > 1 else 0, 0, 0))
    in_specs = [pl.BlockSpec((1, tr, dm), lambda b, i: (b, i, 0)), pl.BlockSpec((1, dm), lambda b, i: (0, 0)),
                cond, cond]
    args = [x, nw.reshape(1, dm), scale.reshape(nb, 1, dm), shift.reshape(nb, 1, dm)]
    if add is not None:
        in_specs.append(pl.BlockSpec((tr, dm), lambda b, i: (i, 0)))
        args.append(add)
    if mu is not None:
        r8 = tr // 8
        in_specs += [pl.BlockSpec((1, 8, dm), lambda b, i: (b, jnp.maximum(i * r8 - 1, 0), 0)),
                     pl.BlockSpec((1, 8, dm), lambda b, i: (b, jnp.minimum((i + 1) * r8, L // 8 - 1), 0)),
                     pl.BlockSpec((2, dm), lambda b, i: (0, 0))]
        args += [x, x, mu]
    return pl.pallas_call(
        functools.partial(_norm_mod_kernel, add is not None, mu is not None, nblk),
        grid=(bsz, nblk),
        in_specs=in_specs,
        out_specs=pl.BlockSpec((1, tr, dm), lambda b, i: (b, i, 0)),
        out_shape=jax.ShapeDtypeStruct((bsz, L, dm), BF16),
        compiler_params=pltpu.CompilerParams(
            dimension_semantics=("arbitrary", "arbitrary"), vmem_limit_bytes=VMEM_LIMIT),
        name="norm_mod",
    )(*args)


def _final_norm_kernel(x_ref, nw_ref, o_ref):
    x = x_ref[0]
    o_ref[0] = x * lax.rsqrt(jnp.mean(x * x, axis=-1, keepdims=True) + EPS) * nw_ref[...]


def final_norm(x, nw):
    bsz, L, dm = x.shape
    tr = _pick(L, (256, 128, 64))
    return pl.pallas_call(
        _final_norm_kernel,
        grid=(bsz, L // tr),
        in_specs=[pl.BlockSpec((1, tr, dm), lambda b, i: (b, i, 0)), pl.BlockSpec((1, dm), lambda b, i: (0, 0))],
        out_specs=pl.BlockSpec((1, tr, dm), lambda b, i: (b, i, 0)),
        out_shape=jax.ShapeDtypeStruct((bsz, L, dm), F32),
        compiler_params=pltpu.CompilerParams(
            dimension_semantics=("arbitrary", "arbitrary"), vmem_limit_bytes=VMEM_LIMIT),
        name="final_norm",
    )(x, nw.reshape(1, dm))


def s5_operators(lam_re, lam_im, log_step, b_re, b_im, c_re, c_im, d_skip):
    T = S5_CHUNK
    dt = jnp.exp(log_step)[..., None]
    mag = jnp.exp(lam_re * dt)
    ab_re, ab_im = mag * jnp.cos(lam_im * dt), mag * jnp.sin(lam_im * dt)
    den = lam_re * lam_re + lam_im * lam_im
    f_re = ((ab_re - 1.0) * lam_re + ab_im * lam_im) / den
    f_im = (ab_im * lam_re - (ab_re - 1.0) * lam_im) / den
    bb_re = f_re[..., None] * b_re - f_im[..., None] * b_im
    bb_im = f_re[..., None] * b_im + f_im[..., None] * b_re
    kk = jnp.arange(T + 1, dtype=F32)[:, None, None, None]
    pmag = jnp.exp(kk * (lam_re * dt))
    pr = pmag * jnp.cos(kk * (lam_im * dt))
    pi = pmag * jnp.sin(kk * (lam_im * dt))
    zr = pr[:T, :, :, :, None] * bb_re - pi[:T, :, :, :, None] * bb_im
    zi = pr[:T, :, :, :, None] * bb_im + pi[:T, :, :, :, None] * bb_re
    kern = (jnp.einsum('dghp,kdgpj->kdghj', c_re, zr, precision=HI)
            - jnp.einsum('dghp,kdgpj->kdghj', c_im, zi, precision=HI))
    t_idx = jnp.arange(T)[:, None]
    s_idx = jnp.arange(T)[None, :]
    lag_f = t_idx - s_idx
    lag_b = s_idx - t_idx
    m_f = jnp.where((lag_f >= 0)[:, :, None, None, None], kern[:, 0][jnp.clip(lag_f, 0, T - 1)], 0.0)
    m_b = jnp.where((lag_b >= 0)[:, :, None, None, None], kern[:, 1][jnp.clip(lag_b, 0, T - 1)], 0.0)
    m = m_f + m_b
    eye_t = jnp.eye(T, dtype=F32)[:, :, None, None, None]
    eye_h = jnp.eye(S5_GROUP_CH, dtype=F32)[None, None, None]
    m = m + eye_t * eye_h * d_skip.reshape(S5_GROUPS, S5_GROUP_CH)[None, None, :, :, None]
    g = m.shape[2]
    m_t = m.transpose(2, 1, 4, 0, 3).reshape(g, T * S5_GROUP_CH, T * S5_GROUP_CH)
    pf_r, pf_i = pr[T - 1::-1][:T, 0], pi[T - 1::-1][:T, 0]
    pb_r, pb_i = pr[:T, 1], pi[:T, 1]

    def f_mat(p_r, p_i, d):
        re = p_r[..., None] * bb_re[d][None] - p_i[..., None] * bb_im[d][None]
        im = p_r[..., None] * bb_im[d][None] + p_i[..., None] * bb_re[d][None]
        re = re.transpose(1, 0, 3, 2).reshape(g, T * S5_GROUP_CH, S5_STATE)
        im = im.transpose(1, 0, 3, 2).reshape(g, T * S5_GROUP_CH, S5_STATE)
        return re, im
    ff_re, ff_im = f_mat(pf_r, pf_i, 0)
    fb_re, fb_im = f_mat(pb_r, pb_i, 1)
    a_t = jnp.concatenate([m_t, ff_re, fb_re, ff_im, fb_im], axis=-1)
    ef_r, ef_i = pr[1:T + 1, 0], pi[1:T + 1, 0]
    eb_r, eb_i = pr[T:0:-1, 1], pi[T:0:-1, 1]

    def e_mat(p_r, p_i, d):
        er = c_re[d][None] * p_r[:, :, None, :] - c_im[d][None] * p_i[:, :, None, :]
        ei = -(c_re[d][None] * p_i[:, :, None, :] + c_im[d][None] * p_r[:, :, None, :])
        er = er.transpose(1, 3, 0, 2).reshape(g, S5_STATE, T * S5_GROUP_CH)
        ei = ei.transpose(1, 3, 0, 2).reshape(g, S5_STATE, T * S5_GROUP_CH)
        return er, ei
    efr, efi = e_mat(ef_r, ef_i, 0)
    ebr, ebi = e_mat(eb_r, eb_i, 1)
    e_t = jnp.concatenate([efr, ebr, efi, ebi], axis=1)
    lam_t = jnp.concatenate([pr[T, 0], pr[T, 1], pi[T, 0], pi[T, 1]], axis=-1)[:, None, :]
    return a_t.astype(BF16), e_t.astype(BF16), lam_t


def _s5_kernel(n_chunks, bsz, x8_ref, sel_ref, at_ref, et_ref, lam_ref, h0_ref, y_ref, hfin_ref, z_ref, hent_ref):
    P = S5_STATE
    ut = jnp.dot(x8_ref[0], sel_ref[0], preferred_element_type=F32).astype(BF16)
    z_ref[...] = jnp.dot(ut, at_ref[0], preferred_element_type=F32)
    lam = lam_ref[0]
    a_re, a_im = lam[:, 0:2 * P], lam[:, 2 * P:4 * P]
    h0 = h0_ref[0]
    fwd_lanes = lax.broadcasted_iota(jnp.int32, (bsz, 2 * P), 1) < P

    def step(c, carry):
        h_re, h_im = carry
        rf = pl.ds(pl.multiple_of(c * bsz, 8), bsz)
        rb = pl.ds(pl.multiple_of((n_chunks - 1 - c) * bsz, 8), bsz)
        hent_ref[rf, 0:P] = h_re[:, 0:P]
        hent_ref[rb, P:2 * P] = h_re[:, P:2 * P]
        hent_ref[rf, 2 * P:3 * P] = h_im[:, 0:P]
        hent_ref[rb, 3 * P:4 * P] = h_im[:, P:2 * P]
        g_re = jnp.where(fwd_lanes, z_ref[rf, 4 * P:6 * P], z_ref[rb, 4 * P:6 * P])
        g_im = jnp.where(fwd_lanes, z_ref[rf, 6 * P:8 * P], z_ref[rb, 6 * P:8 * P])
        return a_re * h_re - a_im * h_im + g_re, a_re * h_im + a_im * h_re + g_im
    h_re, h_im = lax.fori_loop(0, n_chunks, step, (h0[:, 0:2 * P], h0[:, 2 * P:4 * P]))
    hfin_ref[0, :, 0:2 * P] = h_re
    hfin_ref[0, :, 2 * P:4 * P] = h_im
    y = z_ref[:, 0:4 * P] + jnp.dot(hent_ref[...].astype(BF16), et_ref[0], preferred_element_type=F32)
    y_ref[0] = jax.nn.gelu(y).astype(y_ref.dtype)


def _s5_unpack_kernel(yt_ref, selt_ref, o_ref):
    acc = jnp.dot(yt_ref[0], selt_ref[0], preferred_element_type=F32)
    for gl in range(1, S5_TILE_GROUPS):
        acc = acc + jnp.dot(yt_ref[gl], selt_ref[gl], preferred_element_type=F32)
    o_ref[0] = acc.astype(o_ref.dtype)


def s5_scan(u, ops, h0_re, h0_im):
    a_t, e_t, lam_t = ops
    b_real, L, _ = u.shape
    T, G, H, P = S5_CHUNK, S5_GROUPS, S5_GROUP_CH, S5_STATE
    TG = S5_TILE_GROUPS
    n = L // T
    bsz = -(-b_real // 8) * 8
    cols = n * bsz
    x8 = u.reshape(b_real, n, T, G // TG, LANES).transpose(3, 1, 0, 2, 4).astype(BF16)
    x8 = jnp.pad(x8, ((0, 0), (0, 0), (0, bsz - b_real), (0, 0), (0, 0))).reshape(G // TG, cols, T * LANES)
    src = jnp.arange(T * LANES)
    dst = jnp.arange(T * H)
    sel = ((src[None, :, None] // LANES == dst[None, None, :] // H)
           & (src[None, :, None] % H == dst[None, None, :] % H)
           & ((src[None, :, None] % LANES) // H == jnp.arange(TG)[:, None, None])).astype(BF16)
    h0 = jnp.concatenate([h0_re[:, 0], h0_re[:, 1], h0_im[:, 0], h0_im[:, 1]], axis=-1)
    h0 = jnp.pad(h0.transpose(1, 0, 2), ((0, 0), (0, bsz - b_real), (0, 0)))
    yt, hfin = pl.pallas_call(
        functools.partial(_s5_kernel, n, bsz),
        grid=(G,),
        in_specs=[pl.BlockSpec((1, cols, T * LANES), lambda g: (g // TG, 0, 0)),
                  pl.BlockSpec((1, T * LANES, T * H), lambda g: (g % TG, 0, 0)),
                  pl.BlockSpec((1, T * H, 8 * P), lambda g: (g, 0, 0)),
                  pl.BlockSpec((1, 4 * P, T * H), lambda g: (g, 0, 0)),
                  pl.BlockSpec((1, 1, 4 * P), lambda g: (g, 0, 0)),
                  pl.BlockSpec((1, bsz, 4 * P), lambda g: (g, 0, 0))],
        out_specs=[pl.BlockSpec((1, cols, T * H), lambda g: (g, 0, 0)),
                   pl.BlockSpec((1, bsz, 4 * P), lambda g: (g, 0, 0))],
        out_shape=[jax.ShapeDtypeStruct((G, cols, T * H), BF16),
                   jax.ShapeDtypeStruct((G, bsz, 4 * P), F32)],
        scratch_shapes=[pltpu.VMEM((cols, 8 * P), F32), pltpu.VMEM((cols, 4 * P), F32)],
        compiler_params=pltpu.CompilerParams(dimension_semantics=("arbitrary",), vmem_limit_bytes=VMEM_LIMIT),
        name="s5_chunk_scan",
    )(x8, sel, a_t, e_t, lam_t, h0)
    tr = _pick(cols, (512, 256, 128))
    y8 = pl.pallas_call(
        _s5_unpack_kernel,
        grid=(G // TG, cols // tr),
        in_specs=[pl.BlockSpec((TG, tr, T * H), lambda t, i: (t, i, 0)),
                  pl.BlockSpec((TG, T * H, T * LANES), lambda t, i: (0, 0, 0))],
        out_specs=pl.BlockSpec((1, tr, T * LANES), lambda t, i: (t, i, 0)),
        out_shape=jax.ShapeDtypeStruct((G // TG, cols, T * LANES), BF16),
        compiler_params=pltpu.CompilerParams(
            dimension_semantics=("arbitrary", "arbitrary"), vmem_limit_bytes=VMEM_LIMIT),
        name="s5_unpack",
    )(yt, sel.transpose(0, 2, 1))
    y = y8.reshape(G // TG, n, bsz, T, LANES)[:, :, :b_real].transpose(2, 1, 3, 0, 4).reshape(b_real, L, G * H)
    hfin = hfin[:, :b_real].transpose(1, 0, 2)
    fin_re = jnp.stack([hfin[..., 0:P], hfin[..., P:2 * P]], axis=1)
    fin_im = jnp.stack([hfin[..., 2 * P:3 * P], hfin[..., 3 * P:4 * P]], axis=1)
    return y, fin_re, fin_im


def _dot_t(a, b):
    return lax.dot_general(a, b, (((1,), (1,)), ((), ())), preferred_element_type=F32)


def _dot_mask(mask_bf16, x, x_rows_to_sublanes=False):
    def d(b):
        if x_rows_to_sublanes:
            return lax.dot_general(b, mask_bf16, (((0,), (0,)), ((), ())), preferred_element_type=F32)
        return jnp.dot(mask_bf16, b, preferred_element_type=F32)
    x1 = x.astype(BF16)
    r1 = x - x1.astype(F32)
    x2 = r1.astype(BF16)
    x3 = (r1 - x2.astype(F32)).astype(BF16)
    return d(x1) + (d(x2) + d(x3))


def _gla_block_kernel(n_blocks, NC, has_s0, q_ref, k_ref, v_ref, g_ref, lr_ref, up_ref, db_ref, nw_ref, dst_ref,
                      *refs):
    s0_ref = refs[0] if has_s0 else None
    out_ref, sfin_ref, s_ref, of_ref, qd_ref, ov_ref, kv_ref, dc_ref = refs[1:] if has_s0 else refs
    C = GLA_CHUNK
    R = C * NC
    d = pl.program_id(2)
    c = pl.program_id(3)
    bidx = jnp.where(d == 0, c, n_blocks - 1 - c)

    @pl.when(c == 0)
    def _():
        s_ref[...] = s0_ref[0, 0, 0] if has_s0 else jnp.zeros_like(s_ref)

    z = _mxu(lr_ref[0], up_ref[0], split=True) + db_ref[0]
    gc = jnp.maximum(jax.nn.log_sigmoid(z) * (1.0 / GLA_NORMALIZER), GLA_LOG_DECAY_MIN)
    row_c = lax.broadcasted_iota(jnp.int32, (C, C), 0)
    col_c = lax.broadcasted_iota(jnp.int32, (C, C), 1)
    seen_c = jnp.where(d == 0, row_c - col_c, col_c - row_c) >= 0
    seen_bf = seen_c.astype(BF16)
    rs = [slice(i * C, (i + 1) * C) for i in range(NC)]
    bcum_c = [_dot_mask(seen_bf, gc[r]) for r in rs]
    btot_c = [jnp.broadcast_to(jnp.where(d == 0, b[C - 1:C], b[0:1]), (C, GLA_DK)) for b in bcum_c]
    bcum = jnp.concatenate(bcum_c, axis=0)
    btot = jnp.concatenate(btot_c, axis=0)
    q_dec = (q_ref[0] * (GLA_DK ** -0.5) * jnp.exp(bcum)).astype(BF16)
    k = k_ref[0]
    k_inv = (k * jnp.exp(-bcum)).astype(BF16)
    k_end = (k * jnp.exp(btot - bcum)).astype(BF16)
    v = v_ref[0].astype(BF16)
    ones_c = jnp.ones((C, LANES), BF16)
    qd_ref[...] = q_dec.reshape(NC, C, GLA_DK)
    att = [jnp.where(seen_c, _dot_t(q_dec[r], k_inv[r]), 0.0).astype(BF16) for r in rs]
    for i in range(NC):
        kv_ref[i] = lax.dot_general(k_end[rs[i]], v[rs[i]], (((0,), (0,)), ((), ())), preferred_element_type=F32)
    for i in range(NC):
        ov_ref[i] = jnp.dot(att[i], v[rs[i]], preferred_element_type=F32)
    for i in range(NC):
        dc_ref[i] = _dot_mask(ones_c, gc[rs[i]], x_rows_to_sublanes=True)

    for i in range(NC):
        ci = jnp.where(d == 0, i, NC - 1 - i)
        s_old = s_ref[...]
        ov_ref[ci] = ov_ref[ci] + jnp.dot(qd_ref[ci], s_old.astype(BF16), preferred_element_type=F32)
        s_ref[...] = jnp.exp(dc_ref[ci][:, 0:1]) * s_old + kv_ref[ci]
    rows = pl.ds(pl.multiple_of(bidx * R, R), R)

    @pl.when(d == 0)
    def _():
        of_ref[rows, :] = ov_ref[...].reshape(R, GLA_DV)

    @pl.when(d == 1)
    def _():
        tot = of_ref[rows, :] + ov_ref[...].reshape(R, GLA_DV)
        nrm = tot * lax.rsqrt(jnp.mean(tot * tot, axis=-1, keepdims=True) + EPS) * nw_ref[0]
        gate = g_ref[0]
        out_ref[0] = (nrm * (gate * jax.nn.sigmoid(gate))).astype(out_ref.dtype)

    @pl.when(c == n_blocks - 1)
    def _():
        sfin_ref[0, 0, 0] = s_ref[...]


def gla_mix(main, dec_lr, dec_up, dec_b, gla_nw, s0, dst):
    bsz, L, _ = main.shape
    H, DK, DV = GLA_HEADS, GLA_DK, GLA_DV
    nc = min(GLA_NC, L // GLA_CHUNK)
    C = GLA_CHUNK * nc
    n = L // C
    q_blk = sum(EVEN_SIZES[:2]) // DK
    k_blk = sum(EVEN_SIZES[:3]) // DK
    v_blk = sum(EVEN_SIZES[:4]) // DV
    g_blk = sum(EVEN_SIZES[:5]) // DV
    up = jnp.zeros((N_DIR, LANES, GLA_DK_W), F32)
    for d in range(N_DIR):
        up = up.at[d, d * GLA_RANK:(d + 1) * GLA_RANK].set(dec_up[d])
    db = dec_b.reshape(N_DIR, 1, GLA_DK_W)
    nw = gla_nw.reshape(1, GLA_DV_W)

    def chunk(d, c):
        return c + d * (n - 1 - 2 * c)

    def out_chunk(d, c):
        return (n - 1) - d * c
    state = pl.BlockSpec((1, 1, 1, DK, DV), lambda b, h, d, c: (b, d, h, 0, 0))
    has_s0 = s0 is not None
    out, sfin = pl.pallas_call(
        functools.partial(_gla_block_kernel, n, nc, has_s0),
        grid=(bsz, H, N_DIR, n),
        in_specs=[pl.BlockSpec((1, C, DK), lambda b, h, d, c: (b, chunk(d, c), q_blk + h)),
                  pl.BlockSpec((1, C, DK), lambda b, h, d, c: (b, chunk(d, c), k_blk + h)),
                  pl.BlockSpec((1, C, DV), lambda b, h, d, c: (b, chunk(d, c), v_blk + h)),
                  pl.BlockSpec((1, C, DV), lambda b, h, d, c: (b, chunk(d, c), g_blk + h)),
                  pl.BlockSpec((1, C, LANES), lambda b, h, d, c: (b, chunk(d, c), 0)),
                  pl.BlockSpec((1, LANES, DK), lambda b, h, d, c: (d, 0, h)),
                  pl.BlockSpec((1, 1, DK), lambda b, h, d, c: (d, 0, h)),
                  pl.BlockSpec((1, DV), lambda b, h, d, c: (0, h)),
                  pl.BlockSpec(memory_space=pl.ANY)] + ([state] if has_s0 else []),
        input_output_aliases={8: 0},
        out_specs=[pl.BlockSpec((1, C, DV), lambda b, h, d, c: (b, out_chunk(d, c), S5_W // DV + h)), state],
        out_shape=[jax.ShapeDtypeStruct(dst.shape, BF16),
                   jax.ShapeDtypeStruct((bsz, N_DIR, H, DK, DV), F32)],
        scratch_shapes=[pltpu.VMEM((DK, DV), F32), pltpu.VMEM((L, DV), F32),
                        pltpu.VMEM((nc, GLA_CHUNK, DK), BF16), pltpu.VMEM((nc, GLA_CHUNK, DV), F32),
                        pltpu.VMEM((nc, DK, DV), F32), pltpu.VMEM((nc, DK, LANES), F32)],
        compiler_params=pltpu.CompilerParams(
            dimension_semantics=("arbitrary",) * 4, vmem_limit_bytes=VMEM_LIMIT),
        name="gla_chunk_scan",
    )(main, main, main, main, dec_lr, up, db, nw, dst, *([s0] if has_s0 else []))
    return out, sfin


def _split_bf16(x):
    hi = x.astype(BF16)
    return hi, (x - hi.astype(F32)).astype(BF16)


def _mxu(x, y, dims=(((1,), (0,)), ((), ())), split=False):
    def d(a, b):
        return lax.dot_general(a, b, dims, preferred_element_type=F32)
    if not split:
        return d(x.astype(BF16), y.astype(BF16))
    xh, xl = _split_bf16(x)
    yh, yl = _split_bf16(y)
    return d(xh, yh) + (d(xh, yl) + d(xl, yh))


def _rwkv_fs_kernel(n_chunks, rev, r_ref, k_ref, v_ref, wp_ref, ap_ref, w0_ref, a0_ref, kk_ref, ka_ref, h0_ref,
                    y_ref, hfin_ref, h_ref):
    T, N, SB = RWKV_CHUNK, RWKV_HEAD, RWKV_SUB
    NB = T // SB
    c = pl.program_id(2)

    @pl.when(c == 0)
    def _():
        h_ref[...] = h0_ref[0]

    lane = lax.broadcasted_iota(jnp.int32, (T, LANES), 1)
    row = lax.broadcasted_iota(jnp.int32, (T, LANES), 0)
    lo = lane < N
    col = lane % N
    order = (col - row) if rev else (row - col)
    seen = order >= 0
    before = order > 0
    eye = row == col
    sq_r = lax.broadcasted_iota(jnp.int32, (T, T), 0)
    sq_c = lax.broadcasted_iota(jnp.int32, (T, T), 1)
    seen_sq = (((sq_c - sq_r) if rev else (sq_r - sq_c)) >= 0).astype(BF16)
    same_head = ((lax.broadcasted_iota(jnp.int32, (LANES, LANES), 0) < N)
                 == (lax.broadcasted_iota(jnp.int32, (LANES, LANES), 1) < N)).astype(BF16)
    col_sb = lax.broadcasted_iota(jnp.int32, (SB, LANES), 1) % N
    row_dims = (((0,), (0,)), ((), ()))
    lane_dims = (((1,), (1,)), ((), ()))

    def bd(x):
        return jnp.concatenate([jnp.where(lo, x, 0.0), jnp.where(lo, 0.0, x)], axis=0)

    def pp(x, y, split=False):
        return _mxu(x, bd(y), split=split)

    def ptp(x, y):
        full = _mxu(x, y, row_dims)
        return jnp.where(lo, full[:N], full[N:])

    w_log = -jax.nn.softplus(-(wp_ref[0] + w0_ref[...])) - 0.5
    lw_all = -jnp.exp(w_log)
    iclr_all = jax.nn.sigmoid(ap_ref[0] + a0_ref[...])
    k_all = k_ref[0]
    kd_all = k_all * (1.0 + (iclr_all - 1.0) * ka_ref[...])
    kkr_all = k_all * kk_ref[...]
    cs_all = _dot_mask(seen_sq, lw_all)
    tot_all = jnp.sum(lw_all, axis=0, keepdims=True)
    pairs = range(RWKV_CPAIRS)
    sl = [slice(p * LANES, (p + 1) * LANES) for p in pairs]
    sq_hi = [_split_bf16(kkr_all[:, s] * kkr_all[:, s]) for s in sl]
    ssq = [jnp.dot(sq_hi[p][0], same_head, preferred_element_type=F32)
           + jnp.dot(sq_hi[p][1], same_head, preferred_element_type=F32) for p in pairs]
    kk = [kkr_all[:, sl[p]] / jnp.maximum(jnp.sqrt(ssq[p]), 1e-12) for p in pairs]
    b_in = [kk[p] * iclr_all[:, sl[p]] for p in pairs]
    cs = [cs_all[:, s] for s in sl]
    tot = [tot_all[:, s] for s in sl]
    e_out = [jnp.exp(-cs[p]) for p in pairs]
    at = [-kk[p] * jnp.exp(cs[p] - lw_all[:, sl[p]]) for p in pairs]
    rt = [r_ref[0, :, sl[p]] * jnp.exp(cs[p]) for p in pairs]
    ar = [jnp.concatenate([at[p], rt[p]], axis=0) for p in pairs]
    g1 = [_mxu(ar[p], bd(b_in[p] * e_out[p]), lane_dims) for p in pairs]
    g2 = [_mxu(ar[p], bd(kd_all[:, sl[p]] * e_out[p]), lane_dims) for p in pairs]
    a_ab = [jnp.where(before, g1[p][:T], 0.0) for p in pairs]
    a_rb = [jnp.where(seen, g1[p][T:], 0.0) for p in pairs]
    a_ak = [jnp.where(before, g2[p][:T], 0.0) for p in pairs]
    a_rk = [jnp.where(seen, g2[p][T:], 0.0) for p in pairs]
    v = [v_ref[0, :, sl[p]] for p in pairs]
    akv = [pp(a_ak[p], v[p]) for p in pairs]
    za = [[None] * NB for _ in pairs]
    zu = [[None] * NB for _ in pairs]
    zero_blk = jnp.zeros((SB, LANES), F32)
    for kpos in range(NB):
        bk = NB - 1 - kpos if rev else kpos
        rows = slice(bk * SB, (bk + 1) * SB)
        done = [(m > bk) if rev else (m < bk) for m in range(NB)]
        cur_a = [at[p][rows] for p in pairs]
        cur_u = [akv[p][rows] for p in pairs]
        if kpos > 0:
            for p in pairs:
                zc_a = jnp.concatenate([za[p][m] if done[m] else zero_blk for m in range(NB)], axis=0)
                zc_u = jnp.concatenate([zu[p][m] if done[m] else zero_blk for m in range(NB)], axis=0)
                off = _mxu(a_ab[p][rows], jnp.concatenate([bd(zc_a), bd(zc_u)], axis=1))
                cur_a[p] = cur_a[p] + off[:, :LANES]
                cur_u[p] = cur_u[p] + off[:, LANES:]
        abc = []
        for p in pairs:
            ablk = a_ab[p][rows]
            picked = jnp.concatenate([jnp.where(col_sb == bk * SB + s, ablk, 0.0) for s in range(SB)], axis=0)
            abc.append(jnp.dot(picked.astype(BF16), same_head, preferred_element_type=F32))
        ha = [[cur_a[p][:8], cur_a[p][8:]] for p in pairs]
        hu = [[cur_u[p][:8], cur_u[p][8:]] for p in pairs]
        for j in range(SB - 1):
            s = SB - 1 - j if rev else j
            src, r8 = s // 8, s % 8
            halves = (0, 1) if (s >= 8) == rev else ((0,) if rev else (1,))
            for p in pairs:
                row_a = ha[p][src][r8:r8 + 1]
                row_u = hu[p][src][r8:r8 + 1]
                for hf in halves:
                    coef = abc[p][s * SB + hf * 8:s * SB + hf * 8 + 8]
                    ha[p][hf] = ha[p][hf] + coef * row_a
                    hu[p][hf] = hu[p][hf] + coef * row_u
        for p in pairs:
            za[p][bk] = jnp.concatenate(ha[p], axis=0)
            zu[p][bk] = jnp.concatenate(hu[p], axis=0)
    a_hat = [jnp.concatenate(za[p], axis=0) for p in pairs]
    u_loc = [jnp.concatenate(zu[p], axis=0) for p in pairs]
    h0 = [h_ref[:, sl[p]] for p in pairs]
    q_hat = [rt[p] + pp(a_rb[p], a_hat[p]) for p in pairs]
    y_loc = [pp(a_rb[p], u_loc[p]) + pp(a_rk[p], v[p]) for p in pairs]
    for p in pairs:
        y_ref[0, :, sl[p]] = (pp(q_hat[p], h0[p]) + y_loc[p]).astype(y_ref.dtype)
    e_end = [jnp.exp(tot[p] - cs[p]) for p in pairs]
    bh = [b_in[p] * e_end[p] for p in pairs]
    p_end = [_split_bf16(jnp.where(eye, jnp.exp(tot[p]), 0.0)) for p in pairs]
    decay = [jnp.dot(p_end[p][0], same_head, preferred_element_type=F32)
             + jnp.dot(p_end[p][1], same_head, preferred_element_type=F32) for p in pairs]
    corr = [ptp(bh[p], a_hat[p]) for p in pairs]
    gam = [ptp(jnp.concatenate([bh[p], kd_all[:, sl[p]] * e_end[p]], axis=0),
               jnp.concatenate([u_loc[p], v[p]], axis=0)) for p in pairs]
    for p in pairs:
        h_ref[:, sl[p]] = decay[p] * h0[p] + (pp(corr[p], h0[p], split=True) + gam[p])

    @pl.when(c == n_chunks - 1)
    def _():
        hfin_ref[0] = h_ref[...]


def rwkv_direction(rev, main, w_pre, a_pre, w0, a0, k_k, k_a, s0):
    bsz, L, _ = main.shape
    W = RWKV_W
    T, N, H = RWKV_CHUNK, RWKV_HEAD, RWKV_HEADS
    n = L // T
    gw = RWKV_CPAIRS * LANES
    ng = W // gw
    h0 = s0.transpose(0, 3, 1, 2).reshape(bsz, N, W)

    def seq(col0):
        return pl.BlockSpec((1, T, gw), lambda b, g, c: (b, (n - 1 - c) if rev else c, col0 * ng + g))
    vec = pl.BlockSpec((1, gw), lambda b, g, c: (0, g))
    st = pl.BlockSpec((1, N, gw), lambda b, g, c: (b, 0, g))
    y, hfin = pl.pallas_call(
        functools.partial(_rwkv_fs_kernel, n, rev),
        grid=(bsz, ng, n),
        in_specs=[seq(0), seq(1), seq(2), seq(0), seq(0), vec, vec, vec, vec, st],
        out_specs=[seq(0), st],
        out_shape=[jax.ShapeDtypeStruct((bsz, L, W), BF16), jax.ShapeDtypeStruct((bsz, N, W), F32)],
        scratch_shapes=[pltpu.VMEM((N, gw), F32)],
        compiler_params=pltpu.CompilerParams(
            dimension_semantics=("arbitrary",) * 3, vmem_limit_bytes=VMEM_LIMIT),
        name="rwkv_bwd_chunks" if rev else "rwkv_fwd_chunks",
    )(main, main, main, w_pre, a_pre, w0.reshape(1, W), a0.reshape(1, W), k_k.reshape(1, W), k_a.reshape(1, W), h0)
    return y, hfin.reshape(bsz, N, H, N).transpose(0, 2, 3, 1)


def _segsum(x, same_head):
    x1, x2 = _split_bf16(x)
    return (jnp.dot(x1, same_head, preferred_element_type=F32)
            + jnp.dot(x2, same_head, preferred_element_type=F32))


def _rwkv_post_kernel(yf_ref, yb_ref, r_ref, k_ref, v_ref, g_ref, af_ref, ab_ref, a0_ref, ka_ref, rk_ref,
                      lw_ref, lb_ref, o_ref):
    N = RWKV_HEAD
    same_head = ((lax.broadcasted_iota(jnp.int32, (LANES, LANES), 0) < N)
                 == (lax.broadcasted_iota(jnp.int32, (LANES, LANES), 1) < N)).astype(BF16)
    for t in range(o_ref.shape[2] // LANES):
        ls = slice(t * LANES, (t + 1) * LANES)
        wkv = yf_ref[0, :, ls].astype(F32) + yb_ref[0, :, ls].astype(F32)
        mean = _segsum(wkv, same_head) * (1.0 / N)
        cen = wkv - mean
        var = _segsum(cen * cen, same_head) * (1.0 / N)
        ln = cen * lax.rsqrt(var + RWKV_LNX_EPS) * lw_ref[:, ls] + lb_ref[:, ls]
        ka = ka_ref[:, ls]
        k_mix = ((1.0 + (jax.nn.sigmoid(af_ref[0, :, ls] + a0_ref[0:1, ls]) - 1.0) * ka)
                 + (1.0 + (jax.nn.sigmoid(ab_ref[0, :, ls] + a0_ref[1:2, ls]) - 1.0) * ka))
        bonus = _segsum(r_ref[0, :, ls] * k_ref[0, :, ls] * k_mix * rk_ref[:, ls], same_head) * v_ref[0, :, ls]
        gate = g_ref[0, :, ls]
        o_ref[0, :, ls] = ((ln + bonus) * (gate * jax.nn.sigmoid(gate))).astype(o_ref.dtype)


def rwkv_post(y_f, y_b, main, a_pre_f, a_pre_b, a0, k_a, r_k, lnx_w, lnx_b):
    bsz, L, W = y_f.shape
    tr = _pick(L, (256, 128, 64))
    tw = 1024
    nw = W // tw

    def seq(col0):
        return pl.BlockSpec((1, tr, tw), lambda b, i, j: (b, i, col0 * nw + j))
    vec = pl.BlockSpec((1, tw), lambda b, i, j: (0, j))
    vec2 = pl.BlockSpec((N_DIR, tw), lambda b, i, j: (0, j))
    return pl.pallas_call(
        _rwkv_post_kernel,
        grid=(bsz, L // tr, nw),
        in_specs=[seq(0), seq(0), seq(0), seq(1), seq(2), seq(3), seq(0), seq(0), vec2, vec, vec, vec, vec],
        out_specs=seq(0),
        out_shape=jax.ShapeDtypeStruct((bsz, L, W), BF16),
        compiler_params=pltpu.CompilerParams(
            dimension_semantics=("arbitrary",) * 3, vmem_limit_bytes=VMEM_LIMIT),
        name="rwkv_post",
    )(y_f, y_b, main, main, main, main, a_pre_f, a_pre_b, a0, k_a.reshape(1, W), r_k.reshape(1, W),
      lnx_w.reshape(1, W), lnx_b.reshape(1, W))


def _split_cols(t, sizes):
    offsets, acc = [], 0
    for s in sizes[:-1]:
        acc += s
        offsets.append(acc)
    return jnp.split(t, offsets, axis=-1)


def _adaln_kernel(c_ref, w_ref, b_ref, o_ref):
    cond = c_ref[...]
    act = cond * jax.nn.sigmoid(cond)
    o_ref[...] = jnp.dot(act, w_ref[0], precision=HI, preferred_element_type=F32) + b_ref[0]


def adaln(cond, w, b, layer):
    rows, dm = cond.shape
    n = w.shape[2]
    rp = -(-rows // 8) * 8
    tn = 512
    m = pl.pallas_call(
        _adaln_kernel,
        grid=(n // tn,),
        in_specs=[pl.BlockSpec((rp, dm), lambda j: (0, 0)),
                  pl.BlockSpec((1, dm, tn), lambda j: (layer, 0, j)),
                  pl.BlockSpec((1, 1, tn), lambda j: (layer, 0, j))],
        out_specs=pl.BlockSpec((rp, tn), lambda j: (0, j)),
        out_shape=jax.ShapeDtypeStruct((rp, n), F32),
        compiler_params=pltpu.CompilerParams(dimension_semantics=("arbitrary",), vmem_limit_bytes=VMEM_LIMIT),
        name="adaln",
    )(jnp.pad(cond, ((0, rp - rows), (0, 0))), w, b.reshape(b.shape[0], 1, n))[:rows]
    return jnp.split(m, 3, axis=-1)


def _grid_pos_embed(n_tokens):
    rows = n_tokens // GRID_W
    row_id = jnp.broadcast_to(jnp.arange(rows, dtype=F32)[:, None], (rows, GRID_W)).reshape(-1)
    col_id = jnp.broadcast_to(jnp.arange(GRID_W, dtype=F32)[None, :], (rows, GRID_W)).reshape(-1)
    quarter = D_MODEL // 4
    omega = 1.0 / (POS_BASE ** (jnp.arange(quarter, dtype=F32) / quarter))

    def axis_emb(pos):
        ang = pos[:, None] * omega[None, :]
        return jnp.concatenate([jnp.sin(ang), jnp.cos(ang)], axis=-1)
    return jnp.concatenate([axis_emb(row_id), axis_emb(col_id)], axis=-1)


def _even_mixer(x, x_add, gate, h, s5_re0, s5_im0, gla0, w_in, w_out, s5_ops, glu_w, glu_b, dec_up, dec_b,
                gla_nw):
    bsz, L, _ = h.shape
    n_main = sum(EVEN_SIZES[:-1])
    main = _mm3(h, w_in, n_main)
    w_tail = jnp.pad(w_in[:, n_main:], ((0, 0), (0, LANES - N_DIR * GLA_RANK)))
    dec_lr = _mm3(h, w_tail)
    gy, fin_re, fin_im = s5_scan(main[..., :S5_W], s5_ops, s5_re0, s5_im0)
    gy = gy.reshape(bsz * L, S5_W)
    mixed = matmul_glu(gy, glu_w, glu_b, main.reshape(bsz * L, n_main), S5_W, S5_W + GLA_DV_W)
    mixed, fin_gla = gla_mix(main, dec_lr, dec_up, dec_b, gla_nw, gla0, mixed.reshape(bsz, L, -1))
    return matmul_gated_residual(mixed, w_out, x, gate, x_add), fin_re, fin_im, fin_gla


def _odd_mixer(x, gate, xs, rwkv0, w_in, w_out, w0, w2, a0, a2, k_k, k_a, r_k, lnx_w, lnx_b):
    bsz, L, _ = xs.shape
    n_main = sum(ODD_SIZES[:4])
    main = _mm3(xs, w_in, n_main)
    tail = _mm3(xs, w_in[:, n_main:])
    w_lr, a_lr = _split_cols(tail, ODD_SIZES[4:])
    w_lr = jnp.tanh(w_lr).reshape(bsz, L, N_DIR, RWKV_DECAY_RANK)
    a_lr = a_lr.reshape(bsz, L, N_DIR, RWKV_ICLR_RANK)
    ys, a_pres, finals = [], [], []
    for d in range(N_DIR):
        w_pre = _mm3(w_lr[:, :, d], w2[d])
        a_pre = _mm3(a_lr[:, :, d], a2[d])
        y_d, fin = rwkv_direction(bool(d), main, w_pre, a_pre, w0[d], a0[d], k_k, k_a, rwkv0[:, d])
        ys.append(y_d)
        a_pres.append(a_pre)
        finals.append(fin)
    out = rwkv_post(ys[0], ys[1], main, a_pres[0], a_pres[1], a0, k_a, r_k.reshape(-1), lnx_w, lnx_b)
    return matmul_gated_residual(out, w_out, x, gate), jnp.stack(finals, axis=1)


def kernel(x_prompt, x_sample, state_s5_re, state_s5_im, state_gla, state_rwkv, c, c_ctx, norm_w, ada_w, ada_b, final_norm_w, e_w_in, e_w_out, s5_lambda_re, s5_lambda_im, s5_log_step, s5_b_re, s5_b_im, s5_c_re, s5_c_im, s5_d, s5_glu_w, s5_glu_b, gla_decay_up, gla_decay_b, gla_norm_w, o_w_in, o_w_out, rwkv_mu, rwkv_w0, rwkv_w2, rwkv_a0, rwkv_a2, rwkv_k_k, rwkv_k_a, rwkv_r_k, rwkv_lnx_w, rwkv_lnx_b):
    bp = x_prompt.shape[0]
    depth = norm_w.shape[0]
    x_ctx = x_prompt
    x_lat, lat_add = x_sample, _grid_pos_embed(x_sample.shape[1])
    z_s5 = jnp.zeros((bp, N_DIR, S5_GROUPS, S5_STATE), F32)
    z_rwkv = jnp.zeros((bp, N_DIR, RWKV_HEADS, RWKV_HEAD, RWKV_HEAD), F32)
    new_s5_re, new_s5_im, new_gla, new_rwkv = [], [], [], []
    n_lat = c.shape[0]
    cond = jnp.concatenate([c, c_ctx[None]], axis=0)
    for i in range(depth):
        j = i // 2
        shift, scale, gate = adaln(cond, ada_w, ada_b, i)
        gt_l, gt_c = gate[:n_lat], jnp.broadcast_to(gate[n_lat:], (bp, D_MODEL))
        mu = rwkv_mu[j] if i % 2 else None
        h_ctx = norm_mod(x_ctx, norm_w[i], scale[n_lat:], shift[n_lat:], mu=mu)
        h_lat = norm_mod(x_lat, norm_w[i], scale[:n_lat], shift[:n_lat], add=lat_add, mu=mu)
        if i % 2 == 0:
            s5_ops = s5_operators(s5_lambda_re[j], s5_lambda_im[j], s5_log_step[j], s5_b_re[j], s5_b_im[j],
                                  s5_c_re[j], s5_c_im[j], s5_d[j])
            p = (e_w_in[j], e_w_out[j], s5_ops, s5_glu_w[j], s5_glu_b[j], gla_decay_up[j], gla_decay_b[j],
                 gla_norm_w[j])
            x_ctx, fr, fi, fg = _even_mixer(x_ctx, None, gt_c, h_ctx, z_s5, z_s5, None, *p)
            x_lat, _, _, _ = _even_mixer(x_lat, lat_add, gt_l, h_lat, state_s5_re[:, j], state_s5_im[:, j],
                                         state_gla[:, j], *p)
            lat_add = None
            new_s5_re.append(fr)
            new_s5_im.append(fi)
            new_gla.append(fg)
        else:
            p = (o_w_in[j], o_w_out[j], rwkv_w0[j], rwkv_w2[j], rwkv_a0[j], rwkv_a2[j],
                 rwkv_k_k[j], rwkv_k_a[j], rwkv_r_k[j], rwkv_lnx_w[j], rwkv_lnx_b[j])
            x_ctx, fw = _odd_mixer(x_ctx, gt_c, h_ctx, z_rwkv, *p)
            x_lat, _ = _odd_mixer(x_lat, gt_l, h_lat, state_rwkv[:, j], *p)
            new_rwkv.append(fw)
    if lat_add is not None:
        x_lat = x_lat + lat_add
    y_prompt = final_norm(x_ctx, final_norm_w)
    y_sample = final_norm(x_lat, final_norm_w)
    return (y_prompt, y_sample, jnp.stack(new_s5_re, axis=1), jnp.stack(new_s5_im, axis=1),
            jnp.stack(new_gla, axis=1), jnp.stack(new_rwkv, axis=1))
```

```python
import functools

import jax
import jax.numpy as jnp
from jax import lax
from jax.experimental import pallas as pl
from jax.experimental.pallas import tpu as pltpu

D_MODEL = 2048
GRID_W = 64
POS_BASE = 10000.0
N_DIR = 2
EPS = 1e-6
S5_W = 1024
S5_GROUP_CH = 16
S5_GROUPS = 64
S5_STATE = 64
S5_CHUNK = 16
S5_TILE_GROUPS = 8
GLA_HEADS = 6
GLA_DV = 512
GLA_DK = 256
GLA_DK_W = 1536
GLA_DV_W = 3072
GLA_RANK = 16
GLA_NORMALIZER = 16.0
GLA_CHUNK = 64
GLA_NC = 16
GLA_LOG_DECAY_MIN = -1.0
EVEN_SIZES = (S5_W, S5_W, GLA_DK_W, GLA_DK_W, GLA_DV_W, GLA_DV_W, N_DIR * GLA_RANK)
RWKV_W = 2048
RWKV_HEAD = 64
RWKV_HEADS = 32
RWKV_DECAY_RANK = 96
RWKV_ICLR_RANK = 96
RWKV_LNX_EPS = 64e-5
ODD_SIZES = (RWKV_W, RWKV_W, RWKV_W, RWKV_W, N_DIR * RWKV_DECAY_RANK, N_DIR * RWKV_ICLR_RANK)
RWKV_CHUNK = 64
RWKV_CPAIRS = 16
RWKV_SUB = 16
LANES = 128

VMEM_LIMIT = 48 * 1024 * 1024
HI = lax.Precision.HIGHEST
BF16 = jnp.bfloat16
F32 = jnp.float32


def _mm_kernel(x_ref, w_ref, o_ref):
    o_ref[...] = jnp.dot(x_ref[...], w_ref[...], preferred_element_type=F32)


def _pick(n, prefs):
    for p in prefs:
        if n % p == 0:
            return p
    return n


def matmul(x, w, n_cols=None):
    m, k = x.shape
    n = w.shape[1] if n_cols is None else n_cols
    x = x.astype(BF16)
    w = w.astype(BF16)
    tm = _pick(m, (1024, 512, 256, 128, 64, 32, 16, 8))
    tn = _pick(n, (1024, 512, 384, 256, 128))
    return pl.pallas_call(
        _mm_kernel,
        grid=(m // tm, n // tn),
        in_specs=[pl.BlockSpec((tm, k), lambda i, j: (i, 0)),
                  pl.BlockSpec((k, tn), lambda i, j: (0, j))],
        out_specs=pl.BlockSpec((tm, tn), lambda i, j: (i, j)),
        out_shape=jax.ShapeDtypeStruct((m, n), F32),
        compiler_params=pltpu.CompilerParams(
            dimension_semantics=("arbitrary", "arbitrary"), vmem_limit_bytes=VMEM_LIMIT),
        name="proj_matmul",
    )(x, w)


def _mm3(h, w, n_cols=None):
    b, l, k = h.shape
    return matmul(h.reshape(b * l, k), w, n_cols).reshape(b, l, -1)


def _mm_residual_kernel(has_add, x_ref, w_ref, res_ref, gate_ref, *refs):
    o_ref = refs[-1]
    acc = jnp.dot(x_ref[...], w_ref[...], preferred_element_type=F32)
    res = res_ref[...] + refs[0][...] if has_add else res_ref[...]
    o_ref[...] = res + gate_ref[0] * acc


def matmul_gated_residual(x, w, res, gate, res_add=None):
    bsz, L, k = x.shape
    n = w.shape[1]
    m = bsz * L
    tm = _pick(L, (1024, 512, 256, 128))
    tn = _pick(n, (1024, 512, 256, 128) if k <= 2048 else (512, 256, 128))
    per_b = L // tm
    in_specs = [pl.BlockSpec((tm, k), lambda i, j: (i, 0)),
                pl.BlockSpec((k, tn), lambda i, j: (0, j)),
                pl.BlockSpec((tm, tn), lambda i, j: (i, j)),
                pl.BlockSpec((1, 1, tn), lambda i, j: (i // per_b, 0, j))]
    args = [x.reshape(m, k).astype(BF16), w.astype(BF16), res.reshape(m, n), gate.reshape(bsz, 1, n)]
    if res_add is not None:
        in_specs.append(pl.BlockSpec((tm, tn), lambda i, j: (i % per_b, j)))
        args.append(res_add)
    out = pl.pallas_call(
        functools.partial(_mm_residual_kernel, res_add is not None),
        grid=(m // tm, n // tn),
        in_specs=in_specs,
        out_specs=pl.BlockSpec((tm, tn), lambda i, j: (i, j)),
        out_shape=jax.ShapeDtypeStruct((m, n), F32),
        compiler_params=pltpu.CompilerParams(
            dimension_semantics=("arbitrary", "arbitrary"), vmem_limit_bytes=VMEM_LIMIT),
        name="proj_residual",
    )(*args)
    return out.reshape(bsz, L, n)


def _mm_glu_kernel(x_ref, w_ref, b_ref, g_ref, xt_ref, o_ref):
    acc = jnp.dot(x_ref[...], w_ref[...], preferred_element_type=F32) + b_ref[...]
    gy = xt_ref[...].astype(F32)
    gate = g_ref[...]
    o_ref[...] = (gy * jax.nn.sigmoid(acc) * (gate * jax.nn.sigmoid(gate))).astype(o_ref.dtype)


def matmul_glu(gy, w, b, main, g_col0, n_total):
    m, k = gy.shape
    n = w.shape[1]
    tm = _pick(m, (1024, 512, 256, 128))
    tn = 512
    return pl.pallas_call(
        _mm_glu_kernel,
        grid=(m // tm, n // tn),
        in_specs=[pl.BlockSpec((tm, k), lambda i, j: (i, 0)),
                  pl.BlockSpec((k, tn), lambda i, j: (0, j)),
                  pl.BlockSpec((1, tn), lambda i, j: (0, j)),
                  pl.BlockSpec((tm, tn), lambda i, j: (i, g_col0 // tn + j)),
                  pl.BlockSpec((tm, tn), lambda i, j: (i, j))],
        out_specs=pl.BlockSpec((tm, tn), lambda i, j: (i, j)),
        out_shape=jax.ShapeDtypeStruct((m, n_total), BF16),
        compiler_params=pltpu.CompilerParams(
            dimension_semantics=("arbitrary", "arbitrary"), vmem_limit_bytes=VMEM_LIMIT),
        name="s5_glu_gate",
    )(gy, w.astype(BF16), b.reshape(1, n), main, gy)


def _norm_mod_kernel(has_add, has_shift, n_row_blocks, x_ref, nw_ref, sc_ref, sh_ref, *refs):
    o_ref = refs[-1]

    def modulated(x):
        inv = lax.rsqrt(jnp.mean(x * x, axis=-1, keepdims=True) + EPS)
        return x * inv * nw_ref[...] * (1.0 + sc_ref[0]) + sh_ref[0]
    x = x_ref[0]
    if has_add:
        x = x + refs[0][...]
    h = modulated(x)
    if has_shift:
        prev_ref, next_ref, mu_ref = refs[0], refs[1], refs[2]
        i = pl.program_id(1)
        tr = h.shape[0]
        row = lax.broadcasted_iota(jnp.int32, h.shape, 0)
        before = jnp.where(i > 0, modulated(prev_ref[0])[7:8], 0.0)
        after = jnp.where(i < n_row_blocks - 1, modulated(next_ref[0])[0:1], 0.0)
        h_prev = jnp.where(row == 0, before, pltpu.roll(h, 1, 0))
        h_next = jnp.where(row == tr - 1, after, pltpu.roll(h, tr - 1, 0))
        h = h + mu_ref[0:1] * (h_prev - h) + mu_ref[1:2] * (h_next - h)
    o_ref[0] = h.astype(o_ref.dtype)


def norm_mod(x, nw, scale, shift, add=None, mu=None):
    assert add is None or mu is None
    bsz, L, dm = x.shape
    tr = _pick(L, (256, 128, 64))
    nb = scale.shape[0]
    nblk = L // tr
    cond = pl.BlockSpec((1, 1, dm), lambda b, i: (b if nb > 1 else 0, 0, 0))
    in_specs = [pl.BlockSpec((1, tr, dm), lambda b, i: (b, i, 0)), pl.BlockSpec((1, dm), lambda b, i: (0, 0)),
                cond, cond]
    args = [x, nw.reshape(1, dm), scale.reshape(nb, 1, dm), shift.reshape(nb, 1, dm)]
    if add is not None:
        in_specs.append(pl.BlockSpec((tr, dm), lambda b, i: (i, 0)))
        args.append(add)
    if mu is not None:
        r8 = tr // 8
        in_specs += [pl.BlockSpec((1, 8, dm), lambda b, i: (b, jnp.maximum(i * r8 - 1, 0), 0)),
                     pl.BlockSpec((1, 8, dm), lambda b, i: (b, jnp.minimum((i + 1) * r8, L // 8 - 1), 0)),
                     pl.BlockSpec((2, dm), lambda b, i: (0, 0))]
        args += [x, x, mu]
    return pl.pallas_call(
        functools.partial(_norm_mod_kernel, add is not None, mu is not None, nblk),
        grid=(bsz, nblk),
        in_specs=in_specs,
        out_specs=pl.BlockSpec((1, tr, dm), lambda b, i: (b, i, 0)),
        out_shape=jax.ShapeDtypeStruct((bsz, L, dm), BF16),
        compiler_params=pltpu.CompilerParams(
            dimension_semantics=("arbitrary", "arbitrary"), vmem_limit_bytes=VMEM_LIMIT),
        name="norm_mod",
    )(*args)


def _final_norm_kernel(x_ref, nw_ref, o_ref):
    x = x_ref[0]
    o_ref[0] = x * lax.rsqrt(jnp.mean(x * x, axis=-1, keepdims=True) + EPS) * nw_ref[...]


def final_norm(x, nw):
    bsz, L, dm = x.shape
    tr = _pick(L, (256, 128, 64))
    return pl.pallas_call(
        _final_norm_kernel,
        grid=(bsz, L // tr),
        in_specs=[pl.BlockSpec((1, tr, dm), lambda b, i: (b, i, 0)), pl.BlockSpec((1, dm), lambda b, i: (0, 0))],
        out_specs=pl.BlockSpec((1, tr, dm), lambda b, i: (b, i, 0)),
        out_shape=jax.ShapeDtypeStruct((bsz, L, dm), F32),
        compiler_params=pltpu.CompilerParams(
            dimension_semantics=("arbitrary", "arbitrary"), vmem_limit_bytes=VMEM_LIMIT),
        name="final_norm",
    )(x, nw.reshape(1, dm))


def s5_operators(lam_re, lam_im, log_step, b_re, b_im, c_re, c_im, d_skip):
    T = S5_CHUNK
    dt = jnp.exp(log_step)[..., None]
    mag = jnp.exp(lam_re * dt)
    ab_re, ab_im = mag * jnp.cos(lam_im * dt), mag * jnp.sin(lam_im * dt)
    den = lam_re * lam_re + lam_im * lam_im
    f_re = ((ab_re - 1.0) * lam_re + ab_im * lam_im) / den
    f_im = (ab_im * lam_re - (ab_re - 1.0) * lam_im) / den
    bb_re = f_re[..., None] * b_re - f_im[..., None] * b_im
    bb_im = f_re[..., None] * b_im + f_im[..., None] * b_re
    kk = jnp.arange(T + 1, dtype=F32)[:, None, None, None]
    pmag = jnp.exp(kk * (lam_re * dt))
    pr = pmag * jnp.cos(kk * (lam_im * dt))
    pi = pmag * jnp.sin(kk * (lam_im * dt))
    zr = pr[:T, :, :, :, None] * bb_re - pi[:T, :, :, :, None] * bb_im
    zi = pr[:T, :, :, :, None] * bb_im + pi[:T, :, :, :, None] * bb_re
    kern = (jnp.einsum('dghp,kdgpj->kdghj', c_re, zr, precision=HI)
            - jnp.einsum('dghp,kdgpj->kdghj', c_im, zi, precision=HI))
    t_idx = jnp.arange(T)[:, None]
    s_idx = jnp.arange(T)[None, :]
    lag_f = t_idx - s_idx
    lag_b = s_idx - t_idx
    m_f = jnp.where((lag_f >= 0)[:, :, None, None, None], kern[:, 0][jnp.clip(lag_f, 0, T - 1)], 0.0)
    m_b = jnp.where((lag_b >= 0)[:, :, None, None, None], kern[:, 1][jnp.clip(lag_b, 0, T - 1)], 0.0)
    m = m_f + m_b
    eye_t = jnp.eye(T, dtype=F32)[:, :, None, None, None]
    eye_h = jnp.eye(S5_GROUP_CH, dtype=F32)[None, None, None]
    m = m + eye_t * eye_h * d_skip.reshape(S5_GROUPS, S5_GROUP_CH)[None, None, :, :, None]
    g = m.shape[2]
    m_t = m.transpose(2, 1, 4, 0, 3).reshape(g, T * S5_GROUP_CH, T * S5_GROUP_CH)
    pf_r, pf_i = pr[T - 1::-1][:T, 0], pi[T - 1::-1][:T, 0]
    pb_r, pb_i = pr[:T, 1], pi[:T, 1]

    def f_mat(p_r, p_i, d):
        re = p_r[..., None] * bb_re[d][None] - p_i[..., None] * bb_im[d][None]
        im = p_r[..., None] * bb_im[d][None] + p_i[..., None] * bb_re[d][None]
        re = re.transpose(1, 0, 3, 2).reshape(g, T * S5_GROUP_CH, S5_STATE)
        im = im.transpose(1, 0, 3, 2).reshape(g, T * S5_GROUP_CH, S5_STATE)
        return re, im
    ff_re, ff_im = f_mat(pf_r, pf_i, 0)
    fb_re, fb_im = f_mat(pb_r, pb_i, 1)
    a_t = jnp.concatenate([m_t, ff_re, fb_re, ff_im, fb_im], axis=-1)
    ef_r, ef_i = pr[1:T + 1, 0], pi[1:T + 1, 0]
    eb_r, eb_i = pr[T:0:-1, 1], pi[T:0:-1, 1]

    def e_mat(p_r, p_i, d):
        er = c_re[d][None] * p_r[:, :, None, :] - c_im[d][None] * p_i[:, :, None, :]
        ei = -(c_re[d][None] * p_i[:, :, None, :] + c_im[d][None] * p_r[:, :, None, :])
        er = er.transpose(1, 3, 0, 2).reshape(g, S5_STATE, T * S5_GROUP_CH)
        ei = ei.transpose(1, 3, 0, 2).reshape(g, S5_STATE, T * S5_GROUP_CH)
        return er, ei
    efr, efi = e_mat(ef_r, ef_i, 0)
    ebr, ebi = e_mat(eb_r, eb_i, 1)
    e_t = jnp.concatenate([efr, ebr, efi, ebi], axis=1)
    lam_t = jnp.concatenate([pr[T, 0], pr[T, 1], pi[T, 0], pi[T, 1]], axis=-1)[:, None, :]
    return a_t.astype(BF16), e_t.astype(BF16), lam_t


def _s5_kernel(n_steps, bsz, pair, x8_ref, sel_ref, at_ref, et_ref, lam_ref, h0_ref, y_ref, hfin_ref, z_ref,
               hent_ref):
    P = S5_STATE
    ut = jnp.dot(x8_ref[0], sel_ref[0], preferred_element_type=F32).astype(BF16)
    z_ref[...] = jnp.dot(ut, at_ref[0], preferred_element_type=F32)
    lam = lam_ref[0]
    a_re, a_im = lam[:, 0:2 * P], lam[:, 2 * P:4 * P]
    h0 = h0_ref[0]
    h0_re, h0_im = h0[:, 0:2 * P], h0[:, 2 * P:4 * P]
    fwd_lanes = lax.broadcasted_iota(jnp.int32, (bsz, 2 * P), 1) < P
    m_re, m_im = a_re, a_im
    if pair:
        cols, half = z_ref.shape[0], bsz // 2
        g_re, g_im = z_ref[:, 4 * P:6 * P], z_ref[:, 6 * P:8 * P]
        rows = lax.broadcasted_iota(jnp.int32, (cols, 2 * P), 0)
        fwd_all = lax.broadcasted_iota(jnp.int32, (cols, 2 * P), 1) < P

        def neighbour(g):
            up = jnp.where(rows < cols - half, pltpu.roll(g, cols - half, 0), 0.0)
            down = jnp.where(rows >= half, pltpu.roll(g, half, 0), 0.0)
            return jnp.where(fwd_all, up, down)
        z_ref[:, 4 * P:6 * P] = a_re * g_re - a_im * g_im + neighbour(g_re)
        z_ref[:, 6 * P:8 * P] = a_re * g_im + a_im * g_re + neighbour(g_im)
        lo = lax.broadcasted_iota(jnp.int32, (bsz, 2 * P), 0) < half
        ah_re = a_re * h0_re - a_im * h0_im
        ah_im = a_re * h0_im + a_im * h0_re

        def swap(x):
            return pltpu.roll(x, half, 0)
        f_re = h0_re + swap(ah_re + jnp.where(lo, g_re[0:bsz], 0.0))
        f_im = h0_im + swap(ah_im + jnp.where(lo, g_im[0:bsz], 0.0))
        b_re = swap(h0_re) + ah_re + swap(jnp.where(lo, 0.0, g_re[cols - bsz:cols]))
        b_im = swap(h0_im) + ah_im + swap(jnp.where(lo, 0.0, g_im[cols - bsz:cols]))
        h0_re, h0_im = jnp.where(fwd_lanes, f_re, b_re), jnp.where(fwd_lanes, f_im, b_im)
        m_re, m_im = a_re * a_re - a_im * a_im, 2.0 * a_re * a_im

    def step(c, carry):
        h_re, h_im = carry
        rf = pl.ds(pl.multiple_of(c * bsz, 8), bsz)
        rb = pl.ds(pl.multiple_of((n_steps - 1 - c) * bsz, 8), bsz)
        hent_ref[rf, 0:P] = h_re[:, 0:P]
        hent_ref[rb, P:2 * P] = h_re[:, P:2 * P]
        hent_ref[rf, 2 * P:3 * P] = h_im[:, 0:P]
        hent_ref[rb, 3 * P:4 * P] = h_im[:, P:2 * P]
        g_re = jnp.where(fwd_lanes, z_ref[rf, 4 * P:6 * P], z_ref[rb, 4 * P:6 * P])
        g_im = jnp.where(fwd_lanes, z_ref[rf, 6 * P:8 * P], z_ref[rb, 6 * P:8 * P])
        return m_re * h_re - m_im * h_im + g_re, m_re * h_im + m_im * h_re + g_im
    h_re, h_im = lax.fori_loop(0, n_steps, step, (h0_re, h0_im))
    if pair:
        h_re = jnp.where(fwd_lanes, h_re, pltpu.roll(h_re, bsz // 2, 0))
        h_im = jnp.where(fwd_lanes, h_im, pltpu.roll(h_im, bsz // 2, 0))
    hfin_ref[0, :, 0:2 * P] = h_re
    hfin_ref[0, :, 2 * P:4 * P] = h_im
    y = z_ref[:, 0:4 * P] + jnp.dot(hent_ref[...].astype(BF16), et_ref[0], preferred_element_type=F32)
    y_ref[0] = jax.nn.gelu(y).astype(y_ref.dtype)


def _s5_unpack_kernel(yt_ref, selt_ref, o_ref):
    acc = jnp.dot(yt_ref[0], selt_ref[0], preferred_element_type=F32)
    for gl in range(1, S5_TILE_GROUPS):
        acc = acc + jnp.dot(yt_ref[gl], selt_ref[gl], preferred_element_type=F32)
    o_ref[0] = acc.astype(o_ref.dtype)


def s5_scan(u, ops, h0_re, h0_im):
    a_t, e_t, lam_t = ops
    b_real, L, _ = u.shape
    T, G, H, P = S5_CHUNK, S5_GROUPS, S5_GROUP_CH, S5_STATE
    TG = S5_TILE_GROUPS
    n = L // T
    pair = b_real == 4 and n % 2 == 0
    bsz = -(-b_real // 8) * 8
    rpc = b_real if pair else bsz
    n_steps = n // 2 if pair else n
    cols = n * rpc
    x8 = u.reshape(b_real, n, T, G // TG, LANES).transpose(3, 1, 0, 2, 4).astype(BF16)
    x8 = jnp.pad(x8, ((0, 0), (0, 0), (0, rpc - b_real), (0, 0), (0, 0))).reshape(G // TG, cols, T * LANES)
    src = jnp.arange(T * LANES)
    dst = jnp.arange(T * H)
    sel = ((src[None, :, None] // LANES == dst[None, None, :] // H)
           & (src[None, :, None] % H == dst[None, None, :] % H)
           & ((src[None, :, None] % LANES) // H == jnp.arange(TG)[:, None, None])).astype(BF16)
    h0 = jnp.concatenate([h0_re[:, 0], h0_re[:, 1], h0_im[:, 0], h0_im[:, 1]], axis=-1)
    h0 = jnp.pad(h0.transpose(1, 0, 2), ((0, 0), (0, bsz - b_real), (0, 0)))
    yt, hfin = pl.pallas_call(
        functools.partial(_s5_kernel, n_steps, bsz, pair),
        grid=(G,),
        in_specs=[pl.BlockSpec((1, cols, T * LANES), lambda g: (g // TG, 0, 0)),
                  pl.BlockSpec((1, T * LANES, T * H), lambda g: (g % TG, 0, 0)),
                  pl.BlockSpec((1, T * H, 8 * P), lambda g: (g, 0, 0)),
                  pl.BlockSpec((1, 4 * P, T * H), lambda g: (g, 0, 0)),
                  pl.BlockSpec((1, 1, 4 * P), lambda g: (g, 0, 0)),
                  pl.BlockSpec((1, bsz, 4 * P), lambda g: (g, 0, 0))],
        out_specs=[pl.BlockSpec((1, cols, T * H), lambda g: (g, 0, 0)),
                   pl.BlockSpec((1, bsz, 4 * P), lambda g: (g, 0, 0))],
        out_shape=[jax.ShapeDtypeStruct((G, cols, T * H), BF16),
                   jax.ShapeDtypeStruct((G, bsz, 4 * P), F32)],
        scratch_shapes=[pltpu.VMEM((cols, 8 * P), F32), pltpu.VMEM((cols, 4 * P), F32)],
        compiler_params=pltpu.CompilerParams(dimension_semantics=("arbitrary",), vmem_limit_bytes=VMEM_LIMIT),
        name="s5_chunk_scan",
    )(x8, sel, a_t, e_t, lam_t, h0)
    tr = _pick(cols, (512, 256, 128))
    y8 = pl.pallas_call(
        _s5_unpack_kernel,
        grid=(G // TG, cols // tr),
        in_specs=[pl.BlockSpec((TG, tr, T * H), lambda t, i: (t, i, 0)),
                  pl.BlockSpec((TG, T * H, T * LANES), lambda t, i: (0, 0, 0))],
        out_specs=pl.BlockSpec((1, tr, T * LANES), lambda t, i: (t, i, 0)),
        out_shape=jax.ShapeDtypeStruct((G // TG, cols, T * LANES), BF16),
        compiler_params=pltpu.CompilerParams(
            dimension_semantics=("arbitrary", "arbitrary"), vmem_limit_bytes=VMEM_LIMIT),
        name="s5_unpack",
    )(yt, sel.transpose(0, 2, 1))
    y = y8.reshape(G // TG, n, rpc, T, LANES)[:, :, :b_real].transpose(2, 1, 3, 0, 4).reshape(b_real, L, G * H)
    hfin = hfin[:, :b_real].transpose(1, 0, 2)
    fin_re = jnp.stack([hfin[..., 0:P], hfin[..., P:2 * P]], axis=1)
    fin_im = jnp.stack([hfin[..., 2 * P:3 * P], hfin[..., 3 * P:4 * P]], axis=1)
    return y, fin_re, fin_im


def _dot_t(a, b):
    return lax.dot_general(a, b, (((1,), (1,)), ((), ())), preferred_element_type=F32)


def _dot_mask(mask_bf16, x, x_rows_to_sublanes=False):
    def d(b):
        if x_rows_to_sublanes:
            return lax.dot_general(b, mask_bf16, (((0,), (0,)), ((), ())), preferred_element_type=F32)
        return jnp.dot(mask_bf16, b, preferred_element_type=F32)
    x1 = x.astype(BF16)
    r1 = x - x1.astype(F32)
    x2 = r1.astype(BF16)
    x3 = (r1 - x2.astype(F32)).astype(BF16)
    return d(x1) + (d(x2) + d(x3))


def _gla_block_kernel(n_blocks, NC, has_s0, q_ref, k_ref, v_ref, g_ref, lr_ref, up_ref, db_ref, nw_ref, dst_ref,
                      *refs):
    s0_ref = refs[0] if has_s0 else None
    out_ref, sfin_ref, s_ref, of_ref, qd_ref, ov_ref, kv_ref, dc_ref = refs[1:] if has_s0 else refs
    C = GLA_CHUNK
    R = C * NC
    d = pl.program_id(2)
    c = pl.program_id(3)
    bidx = jnp.where(d == 0, c, n_blocks - 1 - c)

    @pl.when(c == 0)
    def _():
        s_ref[...] = s0_ref[0, 0, 0] if has_s0 else jnp.zeros_like(s_ref)

    z = _mxu(lr_ref[0], up_ref[0], split=True) + db_ref[0]
    gc = jnp.maximum(jax.nn.log_sigmoid(z) * (1.0 / GLA_NORMALIZER), GLA_LOG_DECAY_MIN)
    row_c = lax.broadcasted_iota(jnp.int32, (C, C), 0)
    col_c = lax.broadcasted_iota(jnp.int32, (C, C), 1)
    seen_c = jnp.where(d == 0, row_c - col_c, col_c - row_c) >= 0
    seen_bf = seen_c.astype(BF16)
    rs = [slice(i * C, (i + 1) * C) for i in range(NC)]
    bcum_c = [_dot_mask(seen_bf, gc[r]) for r in rs]
    btot_c = [jnp.broadcast_to(jnp.where(d == 0, b[C - 1:C], b[0:1]), (C, GLA_DK)) for b in bcum_c]
    bcum = jnp.concatenate(bcum_c, axis=0)
    btot = jnp.concatenate(btot_c, axis=0)
    q_dec = (q_ref[0] * (GLA_DK ** -0.5) * jnp.exp(bcum)).astype(BF16)
    k = k_ref[0]
    k_inv = (k * jnp.exp(-bcum)).astype(BF16)
    k_end = (k * jnp.exp(btot - bcum)).astype(BF16)
    v = v_ref[0].astype(BF16)
    ones_c = jnp.ones((C, LANES), BF16)
    qd_ref[...] = q_dec.reshape(NC, C, GLA_DK)
    att = [jnp.where(seen_c, _dot_t(q_dec[r], k_inv[r]), 0.0).astype(BF16) for r in rs]
    for i in range(NC):
        kv_ref[i] = lax.dot_general(k_end[rs[i]], v[rs[i]], (((0,), (0,)), ((), ())), preferred_element_type=F32)
    for i in range(NC):
        ov_ref[i] = jnp.dot(att[i], v[rs[i]], preferred_element_type=F32)
    for i in range(NC):
        dc_ref[i] = _dot_mask(ones_c, gc[rs[i]], x_rows_to_sublanes=True)

    for i in range(NC):
        ci = jnp.where(d == 0, i, NC - 1 - i)
        s_old = s_ref[...]
        ov_ref[ci] = ov_ref[ci] + jnp.dot(qd_ref[ci], s_old.astype(BF16), preferred_element_type=F32)
        s_ref[...] = jnp.exp(dc_ref[ci][:, 0:1]) * s_old + kv_ref[ci]
    rows = pl.ds(pl.multiple_of(bidx * R, R), R)

    @pl.when(d == 0)
    def _():
        of_ref[rows, :] = ov_ref[...].reshape(R, GLA_DV)

    @pl.when(d == 1)
    def _():
        tot = of_ref[rows, :] + ov_ref[...].reshape(R, GLA_DV)
        nrm = tot * lax.rsqrt(jnp.mean(tot * tot, axis=-1, keepdims=True) + EPS) * nw_ref[0]
        gate = g_ref[0]
        out_ref[0] = (nrm * (gate * jax.nn.sigmoid(gate))).astype(out_ref.dtype)

    @pl.when(c == n_blocks - 1)
    def _():
        sfin_ref[0, 0, 0] = s_ref[...]


def gla_mix(main, dec_lr, dec_up, dec_b, gla_nw, s0, dst):
    bsz, L, _ = main.shape
    H, DK, DV = GLA_HEADS, GLA_DK, GLA_DV
    nc = min(GLA_NC, L // GLA_CHUNK)
    C = GLA_CHUNK * nc
    n = L // C
    q_blk = sum(EVEN_SIZES[:2]) // DK
    k_blk = sum(EVEN_SIZES[:3]) // DK
    v_blk = sum(EVEN_SIZES[:4]) // DV
    g_blk = sum(EVEN_SIZES[:5]) // DV
    up = jnp.zeros((N_DIR, LANES, GLA_DK_W), F32)
    for d in range(N_DIR):
        up = up.at[d, d * GLA_RANK:(d + 1) * GLA_RANK].set(dec_up[d])
    db = dec_b.reshape(N_DIR, 1, GLA_DK_W)
    nw = gla_nw.reshape(1, GLA_DV_W)

    def chunk(d, c):
        return c + d * (n - 1 - 2 * c)

    def out_chunk(d, c):
        return (n - 1) - d * c
    state = pl.BlockSpec((1, 1, 1, DK, DV), lambda b, h, d, c: (b, d, h, 0, 0))
    has_s0 = s0 is not None
    out, sfin = pl.pallas_call(
        functools.partial(_gla_block_kernel, n, nc, has_s0),
        grid=(bsz, H, N_DIR, n),
        in_specs=[pl.BlockSpec((1, C, DK), lambda b, h, d, c: (b, chunk(d, c), q_blk + h)),
                  pl.BlockSpec((1, C, DK), lambda b, h, d, c: (b, chunk(d, c), k_blk + h)),
                  pl.BlockSpec((1, C, DV), lambda b, h, d, c: (b, chunk(d, c), v_blk + h)),
                  pl.BlockSpec((1, C, DV), lambda b, h, d, c: (b, chunk(d, c), g_blk + h)),
                  pl.BlockSpec((1, C, LANES), lambda b, h, d, c: (b, chunk(d, c), 0)),
                  pl.BlockSpec((1, LANES, DK), lambda b, h, d, c: (d, 0, h)),
                  pl.BlockSpec((1, 1, DK), lambda b, h, d, c: (d, 0, h)),
                  pl.BlockSpec((1, DV), lambda b, h, d, c: (0, h)),
                  pl.BlockSpec(memory_space=pl.ANY)] + ([state] if has_s0 else []),
        input_output_aliases={8: 0},
        out_specs=[pl.BlockSpec((1, C, DV), lambda b, h, d, c: (b, out_chunk(d, c), S5_W // DV + h)), state],
        out_shape=[jax.ShapeDtypeStruct(dst.shape, BF16),
                   jax.ShapeDtypeStruct((bsz, N_DIR, H, DK, DV), F32)],
        scratch_shapes=[pltpu.VMEM((DK, DV), F32), pltpu.VMEM((L, DV), F32),
                        pltpu.VMEM((nc, GLA_CHUNK, DK), BF16), pltpu.VMEM((nc, GLA_CHUNK, DV), F32),
                        pltpu.VMEM((nc, DK, DV), F32), pltpu.VMEM((nc, DK, LANES), F32)],
        compiler_params=pltpu.CompilerParams(
            dimension_semantics=("arbitrary",) * 4, vmem_limit_bytes=VMEM_LIMIT),
        name="gla_chunk_scan",
    )(main, main, main, main, dec_lr, up, db, nw, dst, *([s0] if has_s0 else []))
    return out, sfin


def _split_bf16(x):
    hi = x.astype(BF16)
    return hi, (x - hi.astype(F32)).astype(BF16)


def _mxu(x, y, dims=(((1,), (0,)), ((), ())), split=False):
    def d(a, b):
        return lax.dot_general(a, b, dims, preferred_element_type=F32)
    if not split:
        return d(x.astype(BF16), y.astype(BF16))
    xh, xl = _split_bf16(x)
    yh, yl = _split_bf16(y)
    return d(xh, yh) + (d(xh, yl) + d(xl, yh))


def _rwkv_fs_kernel(n_chunks, rev, r_ref, k_ref, v_ref, wp_ref, ap_ref, w0_ref, a0_ref, kk_ref, ka_ref, h0_ref,
                    y_ref, hfin_ref, h_ref):
    T, N, SB = RWKV_CHUNK, RWKV_HEAD, RWKV_SUB
    NB = T // SB
    c = pl.program_id(2)

    @pl.when(c == 0)
    def _():
        h_ref[...] = h0_ref[0]

    lane = lax.broadcasted_iota(jnp.int32, (T, LANES), 1)
    row = lax.broadcasted_iota(jnp.int32, (T, LANES), 0)
    lo = lane < N
    col = lane % N
    order = (col - row) if rev else (row - col)
    seen = order >= 0
    before = order > 0
    eye = row == col
    sq_r = lax.broadcasted_iota(jnp.int32, (T, T), 0)
    sq_c = lax.broadcasted_iota(jnp.int32, (T, T), 1)
    seen_sq = (((sq_c - sq_r) if rev else (sq_r - sq_c)) >= 0).astype(BF16)
    same_head = ((lax.broadcasted_iota(jnp.int32, (LANES, LANES), 0) < N)
                 == (lax.broadcasted_iota(jnp.int32, (LANES, LANES), 1) < N)).astype(BF16)
    col_sb = lax.broadcasted_iota(jnp.int32, (SB, LANES), 1) % N
    row_dims = (((0,), (0,)), ((), ()))
    lane_dims = (((1,), (1,)), ((), ()))

    def bd(x):
        return jnp.concatenate([jnp.where(lo, x, 0.0), jnp.where(lo, 0.0, x)], axis=0)

    def pp(x, y, split=False):
        return _mxu(x, bd(y), split=split)

    def ptp(x, y):
        full = _mxu(x, y, row_dims)
        return jnp.where(lo, full[:N], full[N:])

    w_log = -jax.nn.softplus(-(wp_ref[0] + w0_ref[...])) - 0.5
    lw_all = -jnp.exp(w_log)
    iclr_all = jax.nn.sigmoid(ap_ref[0] + a0_ref[...])
    k_all = k_ref[0]
    kd_all = k_all * (1.0 + (iclr_all - 1.0) * ka_ref[...])
    kkr_all = k_all * kk_ref[...]
    cs_all = _dot_mask(seen_sq, lw_all)
    tot_all = jnp.sum(lw_all, axis=0, keepdims=True)
    pairs = range(RWKV_CPAIRS)
    sl = [slice(p * LANES, (p + 1) * LANES) for p in pairs]
    sq_hi = [_split_bf16(kkr_all[:, s] * kkr_all[:, s]) for s in sl]
    ssq = [jnp.dot(sq_hi[p][0], same_head, preferred_element_type=F32)
           + jnp.dot(sq_hi[p][1], same_head, preferred_element_type=F32) for p in pairs]
    kk = [kkr_all[:, sl[p]] / jnp.maximum(jnp.sqrt(ssq[p]), 1e-12) for p in pairs]
    b_in = [kk[p] * iclr_all[:, sl[p]] for p in pairs]
    cs = [cs_all[:, s] for s in sl]
    tot = [tot_all[:, s] for s in sl]
    e_out = [jnp.exp(-cs[p]) for p in pairs]
    at = [-kk[p] * jnp.exp(cs[p] - lw_all[:, sl[p]]) for p in pairs]
    rt = [r_ref[0, :, sl[p]] * jnp.exp(cs[p]) for p in pairs]
    ar = [jnp.concatenate([at[p], rt[p]], axis=0) for p in pairs]
    g1 = [_mxu(ar[p], bd(b_in[p] * e_out[p]), lane_dims) for p in pairs]
    g2 = [_mxu(ar[p], bd(kd_all[:, sl[p]] * e_out[p]), lane_dims) for p in pairs]
    a_ab = [jnp.where(before, g1[p][:T], 0.0) for p in pairs]
    a_rb = [jnp.where(seen, g1[p][T:], 0.0) for p in pairs]
    a_ak = [jnp.where(before, g2[p][:T], 0.0) for p in pairs]
    a_rk = [jnp.where(seen, g2[p][T:], 0.0) for p in pairs]
    v = [v_ref[0, :, sl[p]] for p in pairs]
    akv = [pp(a_ak[p], v[p]) for p in pairs]
    za = [[None] * NB for _ in pairs]
    zu = [[None] * NB for _ in pairs]
    zero_blk = jnp.zeros((SB, LANES), F32)
    for kpos in range(NB):
        bk = NB - 1 - kpos if rev else kpos
        rows = slice(bk * SB, (bk + 1) * SB)
        done = [(m > bk) if rev else (m < bk) for m in range(NB)]
        cur_a = [at[p][rows] for p in pairs]
        cur_u = [akv[p][rows] for p in pairs]
        if kpos > 0:
            for p in pairs:
                zc_a = jnp.concatenate([za[p][m] if done[m] else zero_blk for m in range(NB)], axis=0)
                zc_u = jnp.concatenate([zu[p][m] if done[m] else zero_blk for m in range(NB)], axis=0)
                off = _mxu(a_ab[p][rows], jnp.concatenate([bd(zc_a), bd(zc_u)], axis=1))
                cur_a[p] = cur_a[p] + off[:, :LANES]
                cur_u[p] = cur_u[p] + off[:, LANES:]
        abc = []
        for p in pairs:
            ablk = a_ab[p][rows]
            picked = jnp.concatenate([jnp.where(col_sb == bk * SB + s, ablk, 0.0) for s in range(SB)], axis=0)
            abc.append(jnp.dot(picked.astype(BF16), same_head, preferred_element_type=F32))
        ha = [[cur_a[p][:8], cur_a[p][8:]] for p in pairs]
        hu = [[cur_u[p][:8], cur_u[p][8:]] for p in pairs]
        for j in range(SB - 1):
            s = SB - 1 - j if rev else j
            src, r8 = s // 8, s % 8
            halves = (0, 1) if (s >= 8) == rev else ((0,) if rev else (1,))
            for p in pairs:
                row_a = ha[p][src][r8:r8 + 1]
                row_u = hu[p][src][r8:r8 + 1]
                for hf in halves:
                    coef = abc[p][s * SB + hf * 8:s * SB + hf * 8 + 8]
                    ha[p][hf] = ha[p][hf] + coef * row_a
                    hu[p][hf] = hu[p][hf] + coef * row_u
        for p in pairs:
            za[p][bk] = jnp.concatenate(ha[p], axis=0)
            zu[p][bk] = jnp.concatenate(hu[p], axis=0)
    a_hat = [jnp.concatenate(za[p], axis=0) for p in pairs]
    u_loc = [jnp.concatenate(zu[p], axis=0) for p in pairs]
    h0 = [h_ref[:, sl[p]] for p in pairs]
    q_hat = [rt[p] + pp(a_rb[p], a_hat[p]) for p in pairs]
    y_loc = [pp(a_rb[p], u_loc[p]) + pp(a_rk[p], v[p]) for p in pairs]
    for p in pairs:
        y_ref[0, :, sl[p]] = (pp(q_hat[p], h0[p]) + y_loc[p]).astype(y_ref.dtype)
    e_end = [jnp.exp(tot[p] - cs[p]) for p in pairs]
    bh = [b_in[p] * e_end[p] for p in pairs]
    p_end = [_split_bf16(jnp.where(eye, jnp.exp(tot[p]), 0.0)) for p in pairs]
    decay = [jnp.dot(p_end[p][0], same_head, preferred_element_type=F32)
             + jnp.dot(p_end[p][1], same_head, preferred_element_type=F32) for p in pairs]
    corr = [ptp(bh[p], a_hat[p]) for p in pairs]
    gam = [ptp(jnp.concatenate([bh[p], kd_all[:, sl[p]] * e_end[p]], axis=0),
               jnp.concatenate([u_loc[p], v[p]], axis=0)) for p in pairs]
    for p in pairs:
        h_ref[:, sl[p]] = decay[p] * h0[p] + (pp(corr[p], h0[p], split=True) + gam[p])

    @pl.when(c == n_chunks - 1)
    def _():
        hfin_ref[0] = h_ref[...]


def rwkv_direction(rev, main, w_pre, a_pre, w0, a0, k_k, k_a, s0):
    bsz, L, _ = main.shape
    W = RWKV_W
    T, N, H = RWKV_CHUNK, RWKV_HEAD, RWKV_HEADS
    n = L // T
    gw = RWKV_CPAIRS * LANES
    ng = W // gw
    h0 = s0.transpose(0, 3, 1, 2).reshape(bsz, N, W)

    def seq(col0):
        return pl.BlockSpec((1, T, gw), lambda b, g, c: (b, (n - 1 - c) if rev else c, col0 * ng + g))
    vec = pl.BlockSpec((1, gw), lambda b, g, c: (0, g))
    st = pl.BlockSpec((1, N, gw), lambda b, g, c: (b, 0, g))
    y, hfin = pl.pallas_call(
        functools.partial(_rwkv_fs_kernel, n, rev),
        grid=(bsz, ng, n),
        in_specs=[seq(0), seq(1), seq(2), seq(0), seq(0), vec, vec, vec, vec, st],
        out_specs=[seq(0), st],
        out_shape=[jax.ShapeDtypeStruct((bsz, L, W), BF16), jax.ShapeDtypeStruct((bsz, N, W), F32)],
        scratch_shapes=[pltpu.VMEM((N, gw), F32)],
        compiler_params=pltpu.CompilerParams(
            dimension_semantics=("arbitrary",) * 3, vmem_limit_bytes=VMEM_LIMIT),
        name="rwkv_bwd_chunks" if rev else "rwkv_fwd_chunks",
    )(main, main, main, w_pre, a_pre, w0.reshape(1, W), a0.reshape(1, W), k_k.reshape(1, W), k_a.reshape(1, W), h0)
    return y, hfin.reshape(bsz, N, H, N).transpose(0, 2, 3, 1)


def _segsum(x, same_head):
    x1, x2 = _split_bf16(x)
    return (jnp.dot(x1, same_head, preferred_element_type=F32)
            + jnp.dot(x2, same_head, preferred_element_type=F32))


def _rwkv_post_kernel(yf_ref, yb_ref, r_ref, k_ref, v_ref, g_ref, af_ref, ab_ref, a0_ref, ka_ref, rk_ref,
                      lw_ref, lb_ref, o_ref):
    N = RWKV_HEAD
    same_head = ((lax.broadcasted_iota(jnp.int32, (LANES, LANES), 0) < N)
                 == (lax.broadcasted_iota(jnp.int32, (LANES, LANES), 1) < N)).astype(BF16)
    for t in range(o_ref.shape[2] // LANES):
        ls = slice(t * LANES, (t + 1) * LANES)
        wkv = yf_ref[0, :, ls].astype(F32) + yb_ref[0, :, ls].astype(F32)
        mean = _segsum(wkv, same_head) * (1.0 / N)
        cen = wkv - mean
        var = _segsum(cen * cen, same_head) * (1.0 / N)
        ln = cen * lax.rsqrt(var + RWKV_LNX_EPS) * lw_ref[:, ls] + lb_ref[:, ls]
        ka = ka_ref[:, ls]
        k_mix = ((1.0 + (jax.nn.sigmoid(af_ref[0, :, ls] + a0_ref[0:1, ls]) - 1.0) * ka)
                 + (1.0 + (jax.nn.sigmoid(ab_ref[0, :, ls] + a0_ref[1:2, ls]) - 1.0) * ka))
        bonus = _segsum(r_ref[0, :, ls] * k_ref[0, :, ls] * k_mix * rk_ref[:, ls], same_head) * v_ref[0, :, ls]
        gate = g_ref[0, :, ls]
        o_ref[0, :, ls] = ((ln + bonus) * (gate * jax.nn.sigmoid(gate))).astype(o_ref.dtype)


def rwkv_post(y_f, y_b, main, a_pre_f, a_pre_b, a0, k_a, r_k, lnx_w, lnx_b):
    bsz, L, W = y_f.shape
    tr = _pick(L, (256, 128, 64))
    tw = 1024
    nw = W // tw

    def seq(col0):
        return pl.BlockSpec((1, tr, tw), lambda b, i, j: (b, i, col0 * nw + j))
    vec = pl.BlockSpec((1, tw), lambda b, i, j: (0, j))
    vec2 = pl.BlockSpec((N_DIR, tw), lambda b, i, j: (0, j))
    return pl.pallas_call(
        _rwkv_post_kernel,
        grid=(bsz, L // tr, nw),
        in_specs=[seq(0), seq(0), seq(0), seq(1), seq(2), seq(3), seq(0), seq(0), vec2, vec, vec, vec, vec],
        out_specs=seq(0),
        out_shape=jax.ShapeDtypeStruct((bsz, L, W), BF16),
        compiler_params=pltpu.CompilerParams(
            dimension_semantics=("arbitrary",) * 3, vmem_limit_bytes=VMEM_LIMIT),
        name="rwkv_post",
    )(y_f, y_b, main, main, main, main, a_pre_f, a_pre_b, a0, k_a.reshape(1, W), r_k.reshape(1, W),
      lnx_w.reshape(1, W), lnx_b.reshape(1, W))


def _split_cols(t, sizes):
    offsets, acc = [], 0
    for s in sizes[:-1]:
        acc += s
        offsets.append(acc)
    return jnp.split(t, offsets, axis=-1)


def _adaln_kernel(c_ref, w_ref, b_ref, o_ref):
    cond = c_ref[...]
    act = cond * jax.nn.sigmoid(cond)
    o_ref[...] = jnp.dot(act, w_ref[0], precision=HI, preferred_element_type=F32) + b_ref[0]


def adaln(cond, w, b, layer):
    rows, dm = cond.shape
    n = w.shape[2]
    rp = -(-rows // 8) * 8
    tn = 512
    m = pl.pallas_call(
        _adaln_kernel,
        grid=(n // tn,),
        in_specs=[pl.BlockSpec((rp, dm), lambda j: (0, 0)),
                  pl.BlockSpec((1, dm, tn), lambda j: (layer, 0, j)),
                  pl.BlockSpec((1, 1, tn), lambda j: (layer, 0, j))],
        out_specs=pl.BlockSpec((rp, tn), lambda j: (0, j)),
        out_shape=jax.ShapeDtypeStruct((rp, n), F32),
        compiler_params=pltpu.CompilerParams(dimension_semantics=("arbitrary",), vmem_limit_bytes=VMEM_LIMIT),
        name="adaln",
    )(jnp.pad(cond, ((0, rp - rows), (0, 0))), w, b.reshape(b.shape[0], 1, n))[:rows]
    return jnp.split(m, 3, axis=-1)


def _grid_pos_embed(n_tokens):
    rows = n_tokens // GRID_W
    row_id = jnp.broadcast_to(jnp.arange(rows, dtype=F32)[:, None], (rows, GRID_W)).reshape(-1)
    col_id = jnp.broadcast_to(jnp.arange(GRID_W, dtype=F32)[None, :], (rows, GRID_W)).reshape(-1)
    quarter = D_MODEL // 4
    omega = 1.0 / (POS_BASE ** (jnp.arange(quarter, dtype=F32) / quarter))

    def axis_emb(pos):
        ang = pos[:, None] * omega[None, :]
        return jnp.concatenate([jnp.sin(ang), jnp.cos(ang)], axis=-1)
    return jnp.concatenate([axis_emb(row_id), axis_emb(col_id)], axis=-1)


def _even_mixer(x, x_add, gate, h, s5_re0, s5_im0, gla0, w_in, w_out, s5_ops, glu_w, glu_b, dec_up, dec_b,
                gla_nw):
    bsz, L, _ = h.shape
    n_main = sum(EVEN_SIZES[:-1])
    main = _mm3(h, w_in, n_main)
    w_tail = jnp.pad(w_in[:, n_main:], ((0, 0), (0, LANES - N_DIR * GLA_RANK)))
    dec_lr = _mm3(h, w_tail)
    gy, fin_re, fin_im = s5_scan(main[..., :S5_W], s5_ops, s5_re0, s5_im0)
    gy = gy.reshape(bsz * L, S5_W)
    mixed = matmul_glu(gy, glu_w, glu_b, main.reshape(bsz * L, n_main), S5_W, S5_W + GLA_DV_W)
    mixed, fin_gla = gla_mix(main, dec_lr, dec_up, dec_b, gla_nw, gla0, mixed.reshape(bsz, L, -1))
    return matmul_gated_residual(mixed, w_out, x, gate, x_add), fin_re, fin_im, fin_gla


def _odd_mixer(x, gate, xs, rwkv0, w_in, w_out, w0, w2, a0, a2, k_k, k_a, r_k, lnx_w, lnx_b):
    bsz, L, _ = xs.shape
    n_main = sum(ODD_SIZES[:4])
    main = _mm3(xs, w_in, n_main)
    tail = _mm3(xs, w_in[:, n_main:])
    w_lr, a_lr = _split_cols(tail, ODD_SIZES[4:])
    w_lr = jnp.tanh(w_lr).reshape(bsz, L, N_DIR, RWKV_DECAY_RANK)
    a_lr = a_lr.reshape(bsz, L, N_DIR, RWKV_ICLR_RANK)
    ys, a_pres, finals = [], [], []
    for d in range(N_DIR):
        w_pre = _mm3(w_lr[:, :, d], w2[d])
        a_pre = _mm3(a_lr[:, :, d], a2[d])
        y_d, fin = rwkv_direction(bool(d), main, w_pre, a_pre, w0[d], a0[d], k_k, k_a, rwkv0[:, d])
        ys.append(y_d)
        a_pres.append(a_pre)
        finals.append(fin)
    out = rwkv_post(ys[0], ys[1], main, a_pres[0], a_pres[1], a0, k_a, r_k.reshape(-1), lnx_w, lnx_b)
    return matmul_gated_residual(out, w_out, x, gate), jnp.stack(finals, axis=1)


def kernel(x_prompt, x_sample, state_s5_re, state_s5_im, state_gla, state_rwkv, c, c_ctx, norm_w, ada_w, ada_b, final_norm_w, e_w_in, e_w_out, s5_lambda_re, s5_lambda_im, s5_log_step, s5_b_re, s5_b_im, s5_c_re, s5_c_im, s5_d, s5_glu_w, s5_glu_b, gla_decay_up, gla_decay_b, gla_norm_w, o_w_in, o_w_out, rwkv_mu, rwkv_w0, rwkv_w2, rwkv_a0, rwkv_a2, rwkv_k_k, rwkv_k_a, rwkv_r_k, rwkv_lnx_w, rwkv_lnx_b):
    bp = x_prompt.shape[0]
    depth = norm_w.shape[0]
    x_ctx = x_prompt
    x_lat, lat_add = x_sample, _grid_pos_embed(x_sample.shape[1])
    z_s5 = jnp.zeros((bp, N_DIR, S5_GROUPS, S5_STATE), F32)
    z_rwkv = jnp.zeros((bp, N_DIR, RWKV_HEADS, RWKV_HEAD, RWKV_HEAD), F32)
    new_s5_re, new_s5_im, new_gla, new_rwkv = [], [], [], []
    n_lat = c.shape[0]
    cond = jnp.concatenate([c, c_ctx[None]], axis=0)
    for i in range(depth):
        j = i // 2
        shift, scale, gate = adaln(cond, ada_w, ada_b, i)
        gt_l, gt_c = gate[:n_lat], jnp.broadcast_to(gate[n_lat:], (bp, D_MODEL))
        mu = rwkv_mu[j] if i % 2 else None
        h_ctx = norm_mod(x_ctx, norm_w[i], scale[n_lat:], shift[n_lat:], mu=mu)
        h_lat = norm_mod(x_lat, norm_w[i], scale[:n_lat], shift[:n_lat], add=lat_add, mu=mu)
        if i % 2 == 0:
            s5_ops = s5_operators(s5_lambda_re[j], s5_lambda_im[j], s5_log_step[j], s5_b_re[j], s5_b_im[j],
                                  s5_c_re[j], s5_c_im[j], s5_d[j])
            p = (e_w_in[j], e_w_out[j], s5_ops, s5_glu_w[j], s5_glu_b[j], gla_decay_up[j], gla_decay_b[j],
                 gla_norm_w[j])
            x_ctx, fr, fi, fg = _even_mixer(x_ctx, None, gt_c, h_ctx, z_s5, z_s5, None, *p)
            x_lat, _, _, _ = _even_mixer(x_lat, lat_add, gt_l, h_lat, state_s5_re[:, j], state_s5_im[:, j],
                                         state_gla[:, j], *p)
            lat_add = None
            new_s5_re.append(fr)
            new_s5_im.append(fi)
            new_gla.append(fg)
        else:
            p = (o_w_in[j], o_w_out[j], rwkv_w0[j], rwkv_w2[j], rwkv_a0[j], rwkv_a2[j],
                 rwkv_k_k[j], rwkv_k_a[j], rwkv_r_k[j], rwkv_lnx_w[j], rwkv_lnx_b[j])
            x_ctx, fw = _odd_mixer(x_ctx, gt_c, h_ctx, z_rwkv, *p)
            x_lat, _ = _odd_mixer(x_lat, gt_l, h_lat, state_rwkv[:, j], *p)
            new_rwkv.append(fw)
    if lat_add is not None:
        x_lat = x_lat + lat_add
    y_prompt = final_norm(x_ctx, final_norm_w)
    y_sample = final_norm(x_lat, final_norm_w)
    return (y_prompt, y_sample, jnp.stack(new_s5_re, axis=1), jnp.stack(new_s5_im, axis=1),
            jnp.stack(new_gla, axis=1), jnp.stack(new_rwkv, axis=1))
```

```python
import functools

import jax
import jax.numpy as jnp
from jax import lax
from jax.experimental import pallas as pl
from jax.experimental.pallas import tpu as pltpu

D_MODEL = 2048
GRID_W = 64
POS_BASE = 10000.0
N_DIR = 2
EPS = 1e-6
S5_W = 1024
S5_GROUP_CH = 16
S5_GROUPS = 64
S5_STATE = 64
S5_CHUNK = 16
S5_TILE_GROUPS = 8
GLA_HEADS = 6
GLA_DV = 512
GLA_DK = 256
GLA_DK_W = 1536
GLA_DV_W = 3072
GLA_RANK = 16
GLA_NORMALIZER = 16.0
GLA_CHUNK = 64
GLA_NC = 16
GLA_LOG_DECAY_MIN = -1.0
EVEN_SIZES = (S5_W, S5_W, GLA_DK_W, GLA_DK_W, GLA_DV_W, GLA_DV_W, N_DIR * GLA_RANK)
RWKV_W = 2048
RWKV_HEAD = 64
RWKV_HEADS = 32
RWKV_DECAY_RANK = 96
RWKV_ICLR_RANK = 96
RWKV_LNX_EPS = 64e-5
ODD_SIZES = (RWKV_W, RWKV_W, RWKV_W, RWKV_W, N_DIR * RWKV_DECAY_RANK, N_DIR * RWKV_ICLR_RANK)
RWKV_CHUNK = 64
RWKV_CPAIRS = 16
RWKV_SUB = 16
LANES = 128

VMEM_LIMIT = 48 * 1024 * 1024
HI = lax.Precision.HIGHEST
BF16 = jnp.bfloat16
F32 = jnp.float32


def _mm_kernel(x_ref, w_ref, o_ref):
    o_ref[...] = jnp.dot(x_ref[...], w_ref[...], preferred_element_type=F32).astype(o_ref.dtype)


def _pick(n, prefs):
    for p in prefs:
        if n % p == 0:
            return p
    return n


def matmul(x, w, n_cols=None, out_dtype=F32):
    m, k = x.shape
    n = w.shape[1] if n_cols is None else n_cols
    x = x.astype(BF16)
    w = w.astype(BF16)
    tm = _pick(m, (1024, 512, 256, 128, 64, 32, 16, 8))
    tn = _pick(n, (1024, 512, 384, 256, 128))
    return pl.pallas_call(
        _mm_kernel,
        grid=(m // tm, n // tn),
        in_specs=[pl.BlockSpec((tm, k), lambda i, j: (i, 0)),
                  pl.BlockSpec((k, tn), lambda i, j: (0, j))],
        out_specs=pl.BlockSpec((tm, tn), lambda i, j: (i, j)),
        out_shape=jax.ShapeDtypeStruct((m, n), out_dtype),
        compiler_params=pltpu.CompilerParams(
            dimension_semantics=("arbitrary", "arbitrary"), vmem_limit_bytes=VMEM_LIMIT),
        name="proj_matmul",
    )(x, w)


def _mm3(h, w, n_cols=None, out_dtype=F32):
    b, l, k = h.shape
    return matmul(h.reshape(b * l, k), w, n_cols, out_dtype).reshape(b, l, -1)


def _mm_residual_kernel(has_add, x_ref, w_ref, res_ref, gate_ref, *refs):
    o_ref = refs[-1]
    acc = jnp.dot(x_ref[...], w_ref[...], preferred_element_type=F32)
    res = res_ref[...] + refs[0][...] if has_add else res_ref[...]
    o_ref[...] = res + gate_ref[0] * acc


def matmul_gated_residual(x, w, res, gate, res_add=None):
    bsz, L, k = x.shape
    n = w.shape[1]
    m = bsz * L
    tm = _pick(L, (1024, 512, 256, 128))
    tn = _pick(n, (1024, 512, 256, 128) if k <= 2048 else (512, 256, 128))
    per_b = L // tm
    in_specs = [pl.BlockSpec((tm, k), lambda i, j: (i, 0)),
                pl.BlockSpec((k, tn), lambda i, j: (0, j)),
                pl.BlockSpec((tm, tn), lambda i, j: (i, j)),
                pl.BlockSpec((1, 1, tn), lambda i, j: (i // per_b, 0, j))]
    args = [x.reshape(m, k).astype(BF16), w.astype(BF16), res.reshape(m, n), gate.reshape(bsz, 1, n)]
    if res_add is not None:
        in_specs.append(pl.BlockSpec((tm, tn), lambda i, j: (i % per_b, j)))
        args.append(res_add)
    out = pl.pallas_call(
        functools.partial(_mm_residual_kernel, res_add is not None),
        grid=(m // tm, n // tn),
        in_specs=in_specs,
        out_specs=pl.BlockSpec((tm, tn), lambda i, j: (i, j)),
        out_shape=jax.ShapeDtypeStruct((m, n), F32),
        compiler_params=pltpu.CompilerParams(
            dimension_semantics=("arbitrary", "arbitrary"), vmem_limit_bytes=VMEM_LIMIT),
        name="proj_residual",
    )(*args)
    return out.reshape(bsz, L, n)


def _mm_glu_kernel(x_ref, w_ref, b_ref, g_ref, xt_ref, o_ref):
    acc = jnp.dot(x_ref[...], w_ref[...], preferred_element_type=F32) + b_ref[...]
    gy = xt_ref[...].astype(F32)
    gate = g_ref[...].astype(F32)
    o_ref[...] = (gy * jax.nn.sigmoid(acc) * (gate * jax.nn.sigmoid(gate))).astype(o_ref.dtype)


def matmul_glu(gy, w, b, main, g_col0, n_total):
    m, k = gy.shape
    n = w.shape[1]
    tm = _pick(m, (1024, 512, 256, 128))
    tn = 512
    return pl.pallas_call(
        _mm_glu_kernel,
        grid=(m // tm, n // tn),
        in_specs=[pl.BlockSpec((tm, k), lambda i, j: (i, 0)),
                  pl.BlockSpec((k, tn), lambda i, j: (0, j)),
                  pl.BlockSpec((1, tn), lambda i, j: (0, j)),
                  pl.BlockSpec((tm, tn), lambda i, j: (i, g_col0 // tn + j)),
                  pl.BlockSpec((tm, tn), lambda i, j: (i, j))],
        out_specs=pl.BlockSpec((tm, tn), lambda i, j: (i, j)),
        out_shape=jax.ShapeDtypeStruct((m, n_total), BF16),
        compiler_params=pltpu.CompilerParams(
            dimension_semantics=("arbitrary", "arbitrary"), vmem_limit_bytes=VMEM_LIMIT),
        name="s5_glu_gate",
    )(gy, w.astype(BF16), b.reshape(1, n), main, gy)


def _norm_mod_kernel(has_add, has_shift, n_row_blocks, x_ref, nw_ref, sc_ref, sh_ref, *refs):
    o_ref = refs[-1]

    def modulated(x):
        inv = lax.rsqrt(jnp.mean(x * x, axis=-1, keepdims=True) + EPS)
        return x * inv * nw_ref[...] * (1.0 + sc_ref[0]) + sh_ref[0]
    x = x_ref[0]
    if has_add:
        x = x + refs[0][...]
    h = modulated(x)
    if has_shift:
        prev_ref, next_ref, mu_ref = refs[0], refs[1], refs[2]
        i = pl.program_id(1)
        tr = h.shape[0]
        row = lax.broadcasted_iota(jnp.int32, h.shape, 0)
        before = jnp.where(i > 0, modulated(prev_ref[0])[7:8], 0.0)
        after = jnp.where(i < n_row_blocks - 1, modulated(next_ref[0])[0:1], 0.0)
        h_prev = jnp.where(row == 0, before, pltpu.roll(h, 1, 0))
        h_next = jnp.where(row == tr - 1, after, pltpu.roll(h, tr - 1, 0))
        h = h + mu_ref[0:1] * (h_prev - h) + mu_ref[1:2] * (h_next - h)
    o_ref[0] = h.astype(o_ref.dtype)


def norm_mod(x, nw, scale, shift, add=None, mu=None):
    assert add is None or mu is None
    bsz, L, dm = x.shape
    tr = _pick(L, (256, 128, 64))
    nb = scale.shape[0]
    nblk = L // tr
    cond = pl.BlockSpec((1, 1, dm), lambda b, i: (b if nb > 1 else 0, 0, 0))
    in_specs = [pl.BlockSpec((1, tr, dm), lambda b, i: (b, i, 0)), pl.BlockSpec((1, dm), lambda b, i: (0, 0)),
                cond, cond]
    args = [x, nw.reshape(1, dm), scale.reshape(nb, 1, dm), shift.reshape(nb, 1, dm)]
    if add is not None:
        in_specs.append(pl.BlockSpec((tr, dm), lambda b, i: (i, 0)))
        args.append(add)
    if mu is not None:
        r8 = tr // 8
        in_specs += [pl.BlockSpec((1, 8, dm), lambda b, i: (b, jnp.maximum(i * r8 - 1, 0), 0)),
                     pl.BlockSpec((1, 8, dm), lambda b, i: (b, jnp.minimum((i + 1) * r8, L // 8 - 1), 0)),
                     pl.BlockSpec((2, dm), lambda b, i: (0, 0))]
        args += [x, x, mu]
    return pl.pallas_call(
        functools.partial(_norm_mod_kernel, add is not None, mu is not None, nblk),
        grid=(bsz, nblk),
        in_specs=in_specs,
        out_specs=pl.BlockSpec((1, tr, dm), lambda b, i: (b, i, 0)),
        out_shape=jax.ShapeDtypeStruct((bsz, L, dm), BF16),
        compiler_params=pltpu.CompilerParams(
            dimension_semantics=("arbitrary", "arbitrary"), vmem_limit_bytes=VMEM_LIMIT),
        name="norm_mod",
    )(*args)


def _final_norm_kernel(x_ref, nw_ref, o_ref):
    x = x_ref[0]
    o_ref[0] = x * lax.rsqrt(jnp.mean(x * x, axis=-1, keepdims=True) + EPS) * nw_ref[...]


def final_norm(x, nw):
    bsz, L, dm = x.shape
    tr = _pick(L, (256, 128, 64))
    return pl.pallas_call(
        _final_norm_kernel,
        grid=(bsz, L // tr),
        in_specs=[pl.BlockSpec((1, tr, dm), lambda b, i: (b, i, 0)), pl.BlockSpec((1, dm), lambda b, i: (0, 0))],
        out_specs=pl.BlockSpec((1, tr, dm), lambda b, i: (b, i, 0)),
        out_shape=jax.ShapeDtypeStruct((bsz, L, dm), F32),
        compiler_params=pltpu.CompilerParams(
            dimension_semantics=("arbitrary", "arbitrary"), vmem_limit_bytes=VMEM_LIMIT),
        name="final_norm",
    )(x, nw.reshape(1, dm))


def s5_operators(lam_re, lam_im, log_step, b_re, b_im, c_re, c_im, d_skip):
    T = S5_CHUNK
    dt = jnp.exp(log_step)[..., None]
    mag = jnp.exp(lam_re * dt)
    ab_re, ab_im = mag * jnp.cos(lam_im * dt), mag * jnp.sin(lam_im * dt)
    den = lam_re * lam_re + lam_im * lam_im
    f_re = ((ab_re - 1.0) * lam_re + ab_im * lam_im) / den
    f_im = (ab_im * lam_re - (ab_re - 1.0) * lam_im) / den
    bb_re = f_re[..., None] * b_re - f_im[..., None] * b_im
    bb_im = f_re[..., None] * b_im + f_im[..., None] * b_re
    kk = jnp.arange(T + 1, dtype=F32)[:, None, None, None]
    pmag = jnp.exp(kk * (lam_re * dt))
    pr = pmag * jnp.cos(kk * (lam_im * dt))
    pi = pmag * jnp.sin(kk * (lam_im * dt))
    zr = pr[:T, :, :, :, None] * bb_re - pi[:T, :, :, :, None] * bb_im
    zi = pr[:T, :, :, :, None] * bb_im + pi[:T, :, :, :, None] * bb_re
    kern = (jnp.einsum('dghp,kdgpj->kdghj', c_re, zr, precision=HI)
            - jnp.einsum('dghp,kdgpj->kdghj', c_im, zi, precision=HI))
    t_idx = jnp.arange(T)[:, None]
    s_idx = jnp.arange(T)[None, :]
    lag_f = t_idx - s_idx
    lag_b = s_idx - t_idx
    m_f = jnp.where((lag_f >= 0)[:, :, None, None, None], kern[:, 0][jnp.clip(lag_f, 0, T - 1)], 0.0)
    m_b = jnp.where((lag_b >= 0)[:, :, None, None, None], kern[:, 1][jnp.clip(lag_b, 0, T - 1)], 0.0)
    m = m_f + m_b
    eye_t = jnp.eye(T, dtype=F32)[:, :, None, None, None]
    eye_h = jnp.eye(S5_GROUP_CH, dtype=F32)[None, None, None]
    m = m + eye_t * eye_h * d_skip.reshape(S5_GROUPS, S5_GROUP_CH)[None, None, :, :, None]
    g = m.shape[2]
    m_t = m.transpose(2, 1, 4, 0, 3).reshape(g, T * S5_GROUP_CH, T * S5_GROUP_CH)
    pf_r, pf_i = pr[T - 1::-1][:T, 0], pi[T - 1::-1][:T, 0]
    pb_r, pb_i = pr[:T, 1], pi[:T, 1]

    def f_mat(p_r, p_i, d):
        re = p_r[..., None] * bb_re[d][None] - p_i[..., None] * bb_im[d][None]
        im = p_r[..., None] * bb_im[d][None] + p_i[..., None] * bb_re[d][None]
        re = re.transpose(1, 0, 3, 2).reshape(g, T * S5_GROUP_CH, S5_STATE)
        im = im.transpose(1, 0, 3, 2).reshape(g, T * S5_GROUP_CH, S5_STATE)
        return re, im
    ff_re, ff_im = f_mat(pf_r, pf_i, 0)
    fb_re, fb_im = f_mat(pb_r, pb_i, 1)
    a_t = jnp.concatenate([m_t, ff_re, fb_re, ff_im, fb_im], axis=-1)
    ef_r, ef_i = pr[1:T + 1, 0], pi[1:T + 1, 0]
    eb_r, eb_i = pr[T:0:-1, 1], pi[T:0:-1, 1]

    def e_mat(p_r, p_i, d):
        er = c_re[d][None] * p_r[:, :, None, :] - c_im[d][None] * p_i[:, :, None, :]
        ei = -(c_re[d][None] * p_i[:, :, None, :] + c_im[d][None] * p_r[:, :, None, :])
        er = er.transpose(1, 3, 0, 2).reshape(g, S5_STATE, T * S5_GROUP_CH)
        ei = ei.transpose(1, 3, 0, 2).reshape(g, S5_STATE, T * S5_GROUP_CH)
        return er, ei
    efr, efi = e_mat(ef_r, ef_i, 0)
    ebr, ebi = e_mat(eb_r, eb_i, 1)
    e_t = jnp.concatenate([efr, ebr, efi, ebi], axis=1)
    lam_t = jnp.concatenate([pr[T, 0], pr[T, 1], pi[T, 0], pi[T, 1]], axis=-1)[:, None, :]
    return a_t.astype(BF16), e_t.astype(BF16), lam_t


def _s5_kernel(n_steps, bsz, pair, x8_ref, sel_ref, at_ref, et_ref, lam_ref, h0_ref, y_ref, hfin_ref, z_ref,
               hent_ref):
    P = S5_STATE
    ut = jnp.dot(x8_ref[0], sel_ref[0], preferred_element_type=F32).astype(BF16)
    z_ref[...] = jnp.dot(ut, at_ref[0], preferred_element_type=F32)
    lam = lam_ref[0]
    a_re, a_im = lam[:, 0:2 * P], lam[:, 2 * P:4 * P]
    h0 = h0_ref[0]
    h0_re, h0_im = h0[:, 0:2 * P], h0[:, 2 * P:4 * P]
    fwd_lanes = lax.broadcasted_iota(jnp.int32, (bsz, 2 * P), 1) < P
    m_re, m_im = a_re, a_im
    if pair:
        cols, half = z_ref.shape[0], bsz // 2
        g_re, g_im = z_ref[:, 4 * P:6 * P], z_ref[:, 6 * P:8 * P]
        rows = lax.broadcasted_iota(jnp.int32, (cols, 2 * P), 0)
        fwd_all = lax.broadcasted_iota(jnp.int32, (cols, 2 * P), 1) < P

        def neighbour(g):
            up = jnp.where(rows < cols - half, pltpu.roll(g, cols - half, 0), 0.0)
            down = jnp.where(rows >= half, pltpu.roll(g, half, 0), 0.0)
            return jnp.where(fwd_all, up, down)
        z_ref[:, 4 * P:6 * P] = a_re * g_re - a_im * g_im + neighbour(g_re)
        z_ref[:, 6 * P:8 * P] = a_re * g_im + a_im * g_re + neighbour(g_im)
        lo = lax.broadcasted_iota(jnp.int32, (bsz, 2 * P), 0) < half
        ah_re = a_re * h0_re - a_im * h0_im
        ah_im = a_re * h0_im + a_im * h0_re

        def swap(x):
            return pltpu.roll(x, half, 0)
        f_re = h0_re + swap(ah_re + jnp.where(lo, g_re[0:bsz], 0.0))
        f_im = h0_im + swap(ah_im + jnp.where(lo, g_im[0:bsz], 0.0))
        b_re = swap(h0_re) + ah_re + swap(jnp.where(lo, 0.0, g_re[cols - bsz:cols]))
        b_im = swap(h0_im) + ah_im + swap(jnp.where(lo, 0.0, g_im[cols - bsz:cols]))
        h0_re, h0_im = jnp.where(fwd_lanes, f_re, b_re), jnp.where(fwd_lanes, f_im, b_im)
        m_re, m_im = a_re * a_re - a_im * a_im, 2.0 * a_re * a_im

    def step(c, carry):
        h_re, h_im = carry
        rf = pl.ds(pl.multiple_of(c * bsz, 8), bsz)
        rb = pl.ds(pl.multiple_of((n_steps - 1 - c) * bsz, 8), bsz)
        hent_ref[rf, 0:P] = h_re[:, 0:P]
        hent_ref[rb, P:2 * P] = h_re[:, P:2 * P]
        hent_ref[rf, 2 * P:3 * P] = h_im[:, 0:P]
        hent_ref[rb, 3 * P:4 * P] = h_im[:, P:2 * P]
        g_re = jnp.where(fwd_lanes, z_ref[rf, 4 * P:6 * P], z_ref[rb, 4 * P:6 * P])
        g_im = jnp.where(fwd_lanes, z_ref[rf, 6 * P:8 * P], z_ref[rb, 6 * P:8 * P])
        return m_re * h_re - m_im * h_im + g_re, m_re * h_im + m_im * h_re + g_im
    h_re, h_im = lax.fori_loop(0, n_steps, step, (h0_re, h0_im))
    if pair:
        h_re = jnp.where(fwd_lanes, h_re, pltpu.roll(h_re, bsz // 2, 0))
        h_im = jnp.where(fwd_lanes, h_im, pltpu.roll(h_im, bsz // 2, 0))
    hfin_ref[0, :, 0:2 * P] = h_re
    hfin_ref[0, :, 2 * P:4 * P] = h_im
    y = z_ref[:, 0:4 * P] + jnp.dot(hent_ref[...].astype(BF16), et_ref[0], preferred_element_type=F32)
    y_ref[0] = jax.nn.gelu(y).astype(y_ref.dtype)


def _s5_unpack_kernel(yt_ref, selt_ref, o_ref):
    acc = jnp.dot(yt_ref[0], selt_ref[0], preferred_element_type=F32)
    for gl in range(1, S5_TILE_GROUPS):
        acc = acc + jnp.dot(yt_ref[gl], selt_ref[gl], preferred_element_type=F32)
    o_ref[0] = acc.astype(o_ref.dtype)


def s5_scan(u, ops, h0_re, h0_im):
    a_t, e_t, lam_t = ops
    b_real, L, _ = u.shape
    T, G, H, P = S5_CHUNK, S5_GROUPS, S5_GROUP_CH, S5_STATE
    TG = S5_TILE_GROUPS
    n = L // T
    pair = b_real == 4 and n % 2 == 0
    bsz = -(-b_real // 8) * 8
    rpc = b_real if pair else bsz
    n_steps = n // 2 if pair else n
    cols = n * rpc
    x8 = u.reshape(b_real, n, T, G // TG, LANES).transpose(3, 1, 0, 2, 4).astype(BF16)
    x8 = jnp.pad(x8, ((0, 0), (0, 0), (0, rpc - b_real), (0, 0), (0, 0))).reshape(G // TG, cols, T * LANES)
    src = jnp.arange(T * LANES)
    dst = jnp.arange(T * H)
    sel = ((src[None, :, None] // LANES == dst[None, None, :] // H)
           & (src[None, :, None] % H == dst[None, None, :] % H)
           & ((src[None, :, None] % LANES) // H == jnp.arange(TG)[:, None, None])).astype(BF16)
    h0 = jnp.concatenate([h0_re[:, 0], h0_re[:, 1], h0_im[:, 0], h0_im[:, 1]], axis=-1)
    h0 = jnp.pad(h0.transpose(1, 0, 2), ((0, 0), (0, bsz - b_real), (0, 0)))
    yt, hfin = pl.pallas_call(
        functools.partial(_s5_kernel, n_steps, bsz, pair),
        grid=(G,),
        in_specs=[pl.BlockSpec((1, cols, T * LANES), lambda g: (g // TG, 0, 0)),
                  pl.BlockSpec((1, T * LANES, T * H), lambda g: (g % TG, 0, 0)),
                  pl.BlockSpec((1, T * H, 8 * P), lambda g: (g, 0, 0)),
                  pl.BlockSpec((1, 4 * P, T * H), lambda g: (g, 0, 0)),
                  pl.BlockSpec((1, 1, 4 * P), lambda g: (g, 0, 0)),
                  pl.BlockSpec((1, bsz, 4 * P), lambda g: (g, 0, 0))],
        out_specs=[pl.BlockSpec((1, cols, T * H), lambda g: (g, 0, 0)),
                   pl.BlockSpec((1, bsz, 4 * P), lambda g: (g, 0, 0))],
        out_shape=[jax.ShapeDtypeStruct((G, cols, T * H), BF16),
                   jax.ShapeDtypeStruct((G, bsz, 4 * P), F32)],
        scratch_shapes=[pltpu.VMEM((cols, 8 * P), F32), pltpu.VMEM((cols, 4 * P), F32)],
        compiler_params=pltpu.CompilerParams(dimension_semantics=("arbitrary",), vmem_limit_bytes=VMEM_LIMIT),
        name="s5_chunk_scan",
    )(x8, sel, a_t, e_t, lam_t, h0)
    tr = _pick(cols, (512, 256, 128))
    y8 = pl.pallas_call(
        _s5_unpack_kernel,
        grid=(G // TG, cols // tr),
        in_specs=[pl.BlockSpec((TG, tr, T * H), lambda t, i: (t, i, 0)),
                  pl.BlockSpec((TG, T * H, T * LANES), lambda t, i: (0, 0, 0))],
        out_specs=pl.BlockSpec((1, tr, T * LANES), lambda t, i: (t, i, 0)),
        out_shape=jax.ShapeDtypeStruct((G // TG, cols, T * LANES), BF16),
        compiler_params=pltpu.CompilerParams(
            dimension_semantics=("arbitrary", "arbitrary"), vmem_limit_bytes=VMEM_LIMIT),
        name="s5_unpack",
    )(yt, sel.transpose(0, 2, 1))
    y = y8.reshape(G // TG, n, rpc, T, LANES)[:, :, :b_real].transpose(2, 1, 3, 0, 4).reshape(b_real, L, G * H)
    hfin = hfin[:, :b_real].transpose(1, 0, 2)
    fin_re = jnp.stack([hfin[..., 0:P], hfin[..., P:2 * P]], axis=1)
    fin_im = jnp.stack([hfin[..., 2 * P:3 * P], hfin[..., 3 * P:4 * P]], axis=1)
    return y, fin_re, fin_im


def _dot_t(a, b):
    return lax.dot_general(a, b, (((1,), (1,)), ((), ())), preferred_element_type=F32)


def _dot_mask(mask_bf16, x, x_rows_to_sublanes=False):
    def d(b):
        if x_rows_to_sublanes:
            return lax.dot_general(b, mask_bf16, (((0,), (0,)), ((), ())), preferred_element_type=F32)
        return jnp.dot(mask_bf16, b, preferred_element_type=F32)
    x1 = x.astype(BF16)
    r1 = x - x1.astype(F32)
    x2 = r1.astype(BF16)
    x3 = (r1 - x2.astype(F32)).astype(BF16)
    return d(x1) + (d(x2) + d(x3))


def _gla_block_kernel(n_blocks, NC, has_s0, q_ref, k_ref, v_ref, g_ref, lr_ref, up_ref, db_ref, nw_ref, dst_ref,
                      *refs):
    s0_ref = refs[0] if has_s0 else None
    out_ref, sfin_ref, s_ref, of_ref, qd_ref, ov_ref, kv_ref, dc_ref = refs[1:] if has_s0 else refs
    C = GLA_CHUNK
    R = C * NC
    d = pl.program_id(2)
    c = pl.program_id(3)
    bidx = jnp.where(d == 0, c, n_blocks - 1 - c)

    @pl.when(c == 0)
    def _():
        s_ref[...] = s0_ref[0, 0, 0] if has_s0 else jnp.zeros_like(s_ref)

    z = _mxu(lr_ref[0], up_ref[0], split=True) + db_ref[0]
    gc = jnp.maximum(jax.nn.log_sigmoid(z) * (1.0 / GLA_NORMALIZER), GLA_LOG_DECAY_MIN)
    row_c = lax.broadcasted_iota(jnp.int32, (C, C), 0)
    col_c = lax.broadcasted_iota(jnp.int32, (C, C), 1)
    seen_c = jnp.where(d == 0, row_c - col_c, col_c - row_c) >= 0
    seen_bf = seen_c.astype(BF16)
    rs = [slice(i * C, (i + 1) * C) for i in range(NC)]
    bcum_c = [_dot_mask(seen_bf, gc[r]) for r in rs]
    btot_c = [jnp.broadcast_to(jnp.where(d == 0, b[C - 1:C], b[0:1]), (C, GLA_DK)) for b in bcum_c]
    bcum = jnp.concatenate(bcum_c, axis=0)
    btot = jnp.concatenate(btot_c, axis=0)
    q_dec = (q_ref[0].astype(F32) * (GLA_DK ** -0.5) * jnp.exp(bcum)).astype(BF16)
    k = k_ref[0].astype(F32)
    k_inv = (k * jnp.exp(-bcum)).astype(BF16)
    k_end = (k * jnp.exp(btot - bcum)).astype(BF16)
    v = v_ref[0].astype(BF16)
    ones_c = jnp.ones((C, LANES), BF16)
    qd_ref[...] = q_dec.reshape(NC, C, GLA_DK)
    att = [jnp.where(seen_c, _dot_t(q_dec[r], k_inv[r]), 0.0).astype(BF16) for r in rs]
    for i in range(NC):
        kv_ref[i] = lax.dot_general(k_end[rs[i]], v[rs[i]], (((0,), (0,)), ((), ())), preferred_element_type=F32)
    for i in range(NC):
        ov_ref[i] = jnp.dot(att[i], v[rs[i]], preferred_element_type=F32)
    for i in range(NC):
        dc_ref[i] = _dot_mask(ones_c, gc[rs[i]], x_rows_to_sublanes=True)

    for i in range(NC):
        ci = jnp.where(d == 0, i, NC - 1 - i)
        s_old = s_ref[...]
        ov_ref[ci] = ov_ref[ci] + jnp.dot(qd_ref[ci], s_old.astype(BF16), preferred_element_type=F32)
        s_ref[...] = jnp.exp(dc_ref[ci][:, 0:1]) * s_old + kv_ref[ci]
    rows = pl.ds(pl.multiple_of(bidx * R, R), R)

    @pl.when(d == 0)
    def _():
        of_ref[rows, :] = ov_ref[...].reshape(R, GLA_DV)

    @pl.when(d == 1)
    def _():
        tot = of_ref[rows, :] + ov_ref[...].reshape(R, GLA_DV)
        nrm = tot * lax.rsqrt(jnp.mean(tot * tot, axis=-1, keepdims=True) + EPS) * nw_ref[0]
        gate = g_ref[0].astype(F32)
        out_ref[0] = (nrm * (gate * jax.nn.sigmoid(gate))).astype(out_ref.dtype)

    @pl.when(c == n_blocks - 1)
    def _():
        sfin_ref[0, 0, 0] = s_ref[...]


def gla_mix(main, dec_lr, dec_up, dec_b, gla_nw, s0, dst):
    bsz, L, _ = main.shape
    H, DK, DV = GLA_HEADS, GLA_DK, GLA_DV
    nc = min(GLA_NC, L // GLA_CHUNK)
    C = GLA_CHUNK * nc
    n = L // C
    q_blk = sum(EVEN_SIZES[:2]) // DK
    k_blk = sum(EVEN_SIZES[:3]) // DK
    v_blk = sum(EVEN_SIZES[:4]) // DV
    g_blk = sum(EVEN_SIZES[:5]) // DV
    up = jnp.zeros((N_DIR, LANES, GLA_DK_W), F32)
    for d in range(N_DIR):
        up = up.at[d, d * GLA_RANK:(d + 1) * GLA_RANK].set(dec_up[d])
    db = dec_b.reshape(N_DIR, 1, GLA_DK_W)
    nw = gla_nw.reshape(1, GLA_DV_W)

    def chunk(d, c):
        return c + d * (n - 1 - 2 * c)

    def out_chunk(d, c):
        return (n - 1) - d * c
    state = pl.BlockSpec((1, 1, 1, DK, DV), lambda b, h, d, c: (b, d, h, 0, 0))
    has_s0 = s0 is not None
    out, sfin = pl.pallas_call(
        functools.partial(_gla_block_kernel, n, nc, has_s0),
        grid=(bsz, H, N_DIR, n),
        in_specs=[pl.BlockSpec((1, C, DK), lambda b, h, d, c: (b, chunk(d, c), q_blk + h)),
                  pl.BlockSpec((1, C, DK), lambda b, h, d, c: (b, chunk(d, c), k_blk + h)),
                  pl.BlockSpec((1, C, DV), lambda b, h, d, c: (b, chunk(d, c), v_blk + h)),
                  pl.BlockSpec((1, C, DV), lambda b, h, d, c: (b, chunk(d, c), g_blk + h)),
                  pl.BlockSpec((1, C, LANES), lambda b, h, d, c: (b, chunk(d, c), 0)),
                  pl.BlockSpec((1, LANES, DK), lambda b, h, d, c: (d, 0, h)),
                  pl.BlockSpec((1, 1, DK), lambda b, h, d, c: (d, 0, h)),
                  pl.BlockSpec((1, DV), lambda b, h, d, c: (0, h)),
                  pl.BlockSpec(memory_space=pl.ANY)] + ([state] if has_s0 else []),
        input_output_aliases={8: 0},
        out_specs=[pl.BlockSpec((1, C, DV), lambda b, h, d, c: (b, out_chunk(d, c), S5_W // DV + h)), state],
        out_shape=[jax.ShapeDtypeStruct(dst.shape, BF16),
                   jax.ShapeDtypeStruct((bsz, N_DIR, H, DK, DV), F32)],
        scratch_shapes=[pltpu.VMEM((DK, DV), F32), pltpu.VMEM((L, DV), F32),
                        pltpu.VMEM((nc, GLA_CHUNK, DK), BF16), pltpu.VMEM((nc, GLA_CHUNK, DV), F32),
                        pltpu.VMEM((nc, DK, DV), F32), pltpu.VMEM((nc, DK, LANES), F32)],
        compiler_params=pltpu.CompilerParams(
            dimension_semantics=("arbitrary",) * 4, vmem_limit_bytes=VMEM_LIMIT),
        name="gla_chunk_scan",
    )(main, main, main, main, dec_lr, up, db, nw, dst, *([s0] if has_s0 else []))
    return out, sfin


def _split_bf16(x):
    hi = x.astype(BF16)
    return hi, (x - hi.astype(F32)).astype(BF16)


def _mxu(x, y, dims=(((1,), (0,)), ((), ())), split=False):
    def d(a, b):
        return lax.dot_general(a, b, dims, preferred_element_type=F32)
    if not split:
        return d(x.astype(BF16), y.astype(BF16))
    xh, xl = _split_bf16(x)
    yh, yl = _split_bf16(y)
    return d(xh, yh) + (d(xh, yl) + d(xl, yh))


def _rwkv_fs_kernel(n_chunks, rev, r_ref, k_ref, v_ref, wp_ref, ap_ref, w0_ref, a0_ref, kk_ref, ka_ref, h0_ref,
                    y_ref, hfin_ref, h_ref):
    T, N, SB = RWKV_CHUNK, RWKV_HEAD, RWKV_SUB
    NB = T // SB
    c = pl.program_id(2)

    @pl.when(c == 0)
    def _():
        h_ref[...] = h0_ref[0]

    lane = lax.broadcasted_iota(jnp.int32, (T, LANES), 1)
    row = lax.broadcasted_iota(jnp.int32, (T, LANES), 0)
    lo = lane < N
    col = lane % N
    order = (col - row) if rev else (row - col)
    seen = order >= 0
    before = order > 0
    eye = row == col
    sq_r = lax.broadcasted_iota(jnp.int32, (T, T), 0)
    sq_c = lax.broadcasted_iota(jnp.int32, (T, T), 1)
    seen_sq = (((sq_c - sq_r) if rev else (sq_r - sq_c)) >= 0).astype(BF16)
    same_head = ((lax.broadcasted_iota(jnp.int32, (LANES, LANES), 0) < N)
                 == (lax.broadcasted_iota(jnp.int32, (LANES, LANES), 1) < N)).astype(BF16)
    col_sb = lax.broadcasted_iota(jnp.int32, (SB, LANES), 1) % N
    row_dims = (((0,), (0,)), ((), ()))
    lane_dims = (((1,), (1,)), ((), ()))

    def bd(x):
        return jnp.concatenate([jnp.where(lo, x, 0.0), jnp.where(lo, 0.0, x)], axis=0)

    def pp(x, y, split=False):
        return _mxu(x, bd(y), split=split)

    def ptp(x, y):
        full = _mxu(x, y, row_dims)
        return jnp.where(lo, full[:N], full[N:])

    w_log = -jax.nn.softplus(-(wp_ref[0] + w0_ref[...])) - 0.5
    lw_all = -jnp.exp(w_log)
    iclr_all = jax.nn.sigmoid(ap_ref[0] + a0_ref[...])
    k_all = k_ref[0].astype(F32)
    kd_all = k_all * (1.0 + (iclr_all - 1.0) * ka_ref[...])
    kkr_all = k_all * kk_ref[...]
    cs_all = _dot_mask(seen_sq, lw_all)
    tot_all = jnp.sum(lw_all, axis=0, keepdims=True)
    pairs = range(RWKV_CPAIRS)
    sl = [slice(p * LANES, (p + 1) * LANES) for p in pairs]
    sq_hi = [_split_bf16(kkr_all[:, s] * kkr_all[:, s]) for s in sl]
    ssq = [jnp.dot(sq_hi[p][0], same_head, preferred_element_type=F32)
           + jnp.dot(sq_hi[p][1], same_head, preferred_element_type=F32) for p in pairs]
    kk = [kkr_all[:, sl[p]] / jnp.maximum(jnp.sqrt(ssq[p]), 1e-12) for p in pairs]
    b_in = [kk[p] * iclr_all[:, sl[p]] for p in pairs]
    cs = [cs_all[:, s] for s in sl]
    tot = [tot_all[:, s] for s in sl]
    e_out = [jnp.exp(-cs[p]) for p in pairs]
    at = [-kk[p] * jnp.exp(cs[p] - lw_all[:, sl[p]]) for p in pairs]
    rt = [r_ref[0, :, sl[p]].astype(F32) * jnp.exp(cs[p]) for p in pairs]
    ar = [jnp.concatenate([at[p], rt[p]], axis=0) for p in pairs]
    g1 = [_mxu(ar[p], bd(b_in[p] * e_out[p]), lane_dims) for p in pairs]
    g2 = [_mxu(ar[p], bd(kd_all[:, sl[p]] * e_out[p]), lane_dims) for p in pairs]
    a_ab = [jnp.where(before, g1[p][:T], 0.0) for p in pairs]
    a_rb = [jnp.where(seen, g1[p][T:], 0.0) for p in pairs]
    a_ak = [jnp.where(before, g2[p][:T], 0.0) for p in pairs]
    a_rk = [jnp.where(seen, g2[p][T:], 0.0) for p in pairs]
    v = [v_ref[0, :, sl[p]].astype(F32) for p in pairs]
    akv = [pp(a_ak[p], v[p]) for p in pairs]
    za = [[None] * NB for _ in pairs]
    zu = [[None] * NB for _ in pairs]
    zero_blk = jnp.zeros((SB, LANES), F32)
    for kpos in range(NB):
        bk = NB - 1 - kpos if rev else kpos
        rows = slice(bk * SB, (bk + 1) * SB)
        done = [(m > bk) if rev else (m < bk) for m in range(NB)]
        cur_a = [at[p][rows] for p in pairs]
        cur_u = [akv[p][rows] for p in pairs]
        if kpos > 0:
            for p in pairs:
                zc_a = jnp.concatenate([za[p][m] if done[m] else zero_blk for m in range(NB)], axis=0)
                zc_u = jnp.concatenate([zu[p][m] if done[m] else zero_blk for m in range(NB)], axis=0)
                off = _mxu(a_ab[p][rows], jnp.concatenate([bd(zc_a), bd(zc_u)], axis=1))
                cur_a[p] = cur_a[p] + off[:, :LANES]
                cur_u[p] = cur_u[p] + off[:, LANES:]
        abc = []
        for p in pairs:
            ablk = a_ab[p][rows]
            picked = jnp.concatenate([jnp.where(col_sb == bk * SB + s, ablk, 0.0) for s in range(SB)], axis=0)
            abc.append(jnp.dot(picked.astype(BF16), same_head, preferred_element_type=F32))
        ha = [[cur_a[p][:8], cur_a[p][8:]] for p in pairs]
        hu = [[cur_u[p][:8], cur_u[p][8:]] for p in pairs]
        for j in range(SB - 1):
            s = SB - 1 - j if rev else j
            src, r8 = s // 8, s % 8
            halves = (0, 1) if (s >= 8) == rev else ((0,) if rev else (1,))
            for p in pairs:
                row_a = ha[p][src][r8:r8 + 1]
                row_u = hu[p][src][r8:r8 + 1]
                for hf in halves:
                    coef = abc[p][s * SB + hf * 8:s * SB + hf * 8 + 8]
                    ha[p][hf] = ha[p][hf] + coef * row_a
                    hu[p][hf] = hu[p][hf] + coef * row_u
        for p in pairs:
            za[p][bk] = jnp.concatenate(ha[p], axis=0)
            zu[p][bk] = jnp.concatenate(hu[p], axis=0)
    a_hat = [jnp.concatenate(za[p], axis=0) for p in pairs]
    u_loc = [jnp.concatenate(zu[p], axis=0) for p in pairs]
    h0 = [h_ref[:, sl[p]] for p in pairs]
    q_hat = [rt[p] + pp(a_rb[p], a_hat[p]) for p in pairs]
    y_loc = [pp(a_rb[p], u_loc[p]) + pp(a_rk[p], v[p]) for p in pairs]
    for p in pairs:
        y_ref[0, :, sl[p]] = (pp(q_hat[p], h0[p]) + y_loc[p]).astype(y_ref.dtype)
    e_end = [jnp.exp(tot[p] - cs[p]) for p in pairs]
    bh = [b_in[p] * e_end[p] for p in pairs]
    p_end = [_split_bf16(jnp.where(eye, jnp.exp(tot[p]), 0.0)) for p in pairs]
    decay = [jnp.dot(p_end[p][0], same_head, preferred_element_type=F32)
             + jnp.dot(p_end[p][1], same_head, preferred_element_type=F32) for p in pairs]
    corr = [ptp(bh[p], a_hat[p]) for p in pairs]
    gam = [ptp(jnp.concatenate([bh[p], kd_all[:, sl[p]] * e_end[p]], axis=0),
               jnp.concatenate([u_loc[p], v[p]], axis=0)) for p in pairs]
    for p in pairs:
        h_ref[:, sl[p]] = decay[p] * h0[p] + (pp(corr[p], h0[p], split=True) + gam[p])

    @pl.when(c == n_chunks - 1)
    def _():
        hfin_ref[0] = h_ref[...]


def rwkv_direction(rev, main, w_pre, a_pre, w0, a0, k_k, k_a, s0):
    bsz, L, _ = main.shape
    W = RWKV_W
    T, N, H = RWKV_CHUNK, RWKV_HEAD, RWKV_HEADS
    n = L // T
    gw = RWKV_CPAIRS * LANES
    ng = W // gw
    h0 = s0.transpose(0, 3, 1, 2).reshape(bsz, N, W)

    def seq(col0):
        return pl.BlockSpec((1, T, gw), lambda b, g, c: (b, (n - 1 - c) if rev else c, col0 * ng + g))
    vec = pl.BlockSpec((1, gw), lambda b, g, c: (0, g))
    st = pl.BlockSpec((1, N, gw), lambda b, g, c: (b, 0, g))
    y, hfin = pl.pallas_call(
        functools.partial(_rwkv_fs_kernel, n, rev),
        grid=(bsz, ng, n),
        in_specs=[seq(0), seq(1), seq(2), seq(0), seq(0), vec, vec, vec, vec, st],
        out_specs=[seq(0), st],
        out_shape=[jax.ShapeDtypeStruct((bsz, L, W), BF16), jax.ShapeDtypeStruct((bsz, N, W), F32)],
        scratch_shapes=[pltpu.VMEM((N, gw), F32)],
        compiler_params=pltpu.CompilerParams(
            dimension_semantics=("arbitrary",) * 3, vmem_limit_bytes=VMEM_LIMIT),
        name="rwkv_bwd_chunks" if rev else "rwkv_fwd_chunks",
    )(main, main, main, w_pre, a_pre, w0.reshape(1, W), a0.reshape(1, W), k_k.reshape(1, W), k_a.reshape(1, W), h0)
    return y, hfin.reshape(bsz, N, H, N).transpose(0, 2, 3, 1)


def _segsum(x, same_head):
    x1, x2 = _split_bf16(x)
    return (jnp.dot(x1, same_head, preferred_element_type=F32)
            + jnp.dot(x2, same_head, preferred_element_type=F32))


def _rwkv_post_kernel(yf_ref, yb_ref, r_ref, k_ref, v_ref, g_ref, af_ref, ab_ref, a0_ref, ka_ref, rk_ref,
                      lw_ref, lb_ref, o_ref):
    N = RWKV_HEAD
    same_head = ((lax.broadcasted_iota(jnp.int32, (LANES, LANES), 0) < N)
                 == (lax.broadcasted_iota(jnp.int32, (LANES, LANES), 1) < N)).astype(BF16)
    for t in range(o_ref.shape[2] // LANES):
        ls = slice(t * LANES, (t + 1) * LANES)
        wkv = yf_ref[0, :, ls].astype(F32) + yb_ref[0, :, ls].astype(F32)
        mean = _segsum(wkv, same_head) * (1.0 / N)
        cen = wkv - mean
        var = _segsum(cen * cen, same_head) * (1.0 / N)
        ln = cen * lax.rsqrt(var + RWKV_LNX_EPS) * lw_ref[:, ls] + lb_ref[:, ls]
        ka = ka_ref[:, ls]
        k_mix = ((1.0 + (jax.nn.sigmoid(af_ref[0, :, ls] + a0_ref[0:1, ls]) - 1.0) * ka)
                 + (1.0 + (jax.nn.sigmoid(ab_ref[0, :, ls] + a0_ref[1:2, ls]) - 1.0) * ka))
        rk = r_ref[0, :, ls].astype(F32) * k_ref[0, :, ls].astype(F32)
        bonus = _segsum(rk * k_mix * rk_ref[:, ls], same_head) * v_ref[0, :, ls].astype(F32)
        gate = g_ref[0, :, ls].astype(F32)
        o_ref[0, :, ls] = ((ln + bonus) * (gate * jax.nn.sigmoid(gate))).astype(o_ref.dtype)


def rwkv_post(y_f, y_b, main, a_pre_f, a_pre_b, a0, k_a, r_k, lnx_w, lnx_b):
    bsz, L, W = y_f.shape
    tr = _pick(L, (256, 128, 64))
    tw = 1024
    nw = W // tw

    def seq(col0):
        return pl.BlockSpec((1, tr, tw), lambda b, i, j: (b, i, col0 * nw + j))
    vec = pl.BlockSpec((1, tw), lambda b, i, j: (0, j))
    vec2 = pl.BlockSpec((N_DIR, tw), lambda b, i, j: (0, j))
    return pl.pallas_call(
        _rwkv_post_kernel,
        grid=(bsz, L // tr, nw),
        in_specs=[seq(0), seq(0), seq(0), seq(1), seq(2), seq(3), seq(0), seq(0), vec2, vec, vec, vec, vec],
        out_specs=seq(0),
        out_shape=jax.ShapeDtypeStruct((bsz, L, W), BF16),
        compiler_params=pltpu.CompilerParams(
            dimension_semantics=("arbitrary",) * 3, vmem_limit_bytes=VMEM_LIMIT),
        name="rwkv_post",
    )(y_f, y_b, main, main, main, main, a_pre_f, a_pre_b, a0, k_a.reshape(1, W), r_k.reshape(1, W),
      lnx_w.reshape(1, W), lnx_b.reshape(1, W))


def _split_cols(t, sizes):
    offsets, acc = [], 0
    for s in sizes[:-1]:
        acc += s
        offsets.append(acc)
    return jnp.split(t, offsets, axis=-1)


def _adaln_kernel(c_ref, w_ref, b_ref, o_ref):
    cond = c_ref[...]
    act = cond * jax.nn.sigmoid(cond)
    o_ref[...] = jnp.dot(act, w_ref[0], precision=HI, preferred_element_type=F32) + b_ref[0]


def adaln(cond, w, b, layer):
    rows, dm = cond.shape
    n = w.shape[2]
    rp = -(-rows // 8) * 8
    tn = 512
    m = pl.pallas_call(
        _adaln_kernel,
        grid=(n // tn,),
        in_specs=[pl.BlockSpec((rp, dm), lambda j: (0, 0)),
                  pl.BlockSpec((1, dm, tn), lambda j: (layer, 0, j)),
                  pl.BlockSpec((1, 1, tn), lambda j: (layer, 0, j))],
        out_specs=pl.BlockSpec((rp, tn), lambda j: (0, j)),
        out_shape=jax.ShapeDtypeStruct((rp, n), F32),
        compiler_params=pltpu.CompilerParams(dimension_semantics=("arbitrary",), vmem_limit_bytes=VMEM_LIMIT),
        name="adaln",
    )(jnp.pad(cond, ((0, rp - rows), (0, 0))), w, b.reshape(b.shape[0], 1, n))[:rows]
    return jnp.split(m, 3, axis=-1)


def _grid_pos_embed(n_tokens):
    rows = n_tokens // GRID_W
    row_id = jnp.broadcast_to(jnp.arange(rows, dtype=F32)[:, None], (rows, GRID_W)).reshape(-1)
    col_id = jnp.broadcast_to(jnp.arange(GRID_W, dtype=F32)[None, :], (rows, GRID_W)).reshape(-1)
    quarter = D_MODEL // 4
    omega = 1.0 / (POS_BASE ** (jnp.arange(quarter, dtype=F32) / quarter))

    def axis_emb(pos):
        ang = pos[:, None] * omega[None, :]
        return jnp.concatenate([jnp.sin(ang), jnp.cos(ang)], axis=-1)
    return jnp.concatenate([axis_emb(row_id), axis_emb(col_id)], axis=-1)


def _even_mixer(x, x_add, gate, h, s5_re0, s5_im0, gla0, w_in, w_out, s5_ops, glu_w, glu_b, dec_up, dec_b,
                gla_nw):
    bsz, L, _ = h.shape
    n_main = sum(EVEN_SIZES[:-1])
    main = _mm3(h, w_in, n_main, BF16)
    w_tail = jnp.pad(w_in[:, n_main:], ((0, 0), (0, LANES - N_DIR * GLA_RANK)))
    dec_lr = _mm3(h, w_tail)
    gy, fin_re, fin_im = s5_scan(main[..., :S5_W], s5_ops, s5_re0, s5_im0)
    gy = gy.reshape(bsz * L, S5_W)
    mixed = matmul_glu(gy, glu_w, glu_b, main.reshape(bsz * L, n_main), S5_W, S5_W + GLA_DV_W)
    mixed, fin_gla = gla_mix(main, dec_lr, dec_up, dec_b, gla_nw, gla0, mixed.reshape(bsz, L, -1))
    return matmul_gated_residual(mixed, w_out, x, gate, x_add), fin_re, fin_im, fin_gla


def _odd_mixer(x, gate, xs, rwkv0, w_in, w_out, w0, w2, a0, a2, k_k, k_a, r_k, lnx_w, lnx_b):
    bsz, L, _ = xs.shape
    n_main = sum(ODD_SIZES[:4])
    main = _mm3(xs, w_in, n_main, BF16)
    tail = _mm3(xs, w_in[:, n_main:])
    w_lr, a_lr = _split_cols(tail, ODD_SIZES[4:])
    w_lr = jnp.tanh(w_lr).reshape(bsz, L, N_DIR, RWKV_DECAY_RANK)
    a_lr = a_lr.reshape(bsz, L, N_DIR, RWKV_ICLR_RANK)
    ys, a_pres, finals = [], [], []
    for d in range(N_DIR):
        w_pre = _mm3(w_lr[:, :, d], w2[d])
        a_pre = _mm3(a_lr[:, :, d], a2[d])
        y_d, fin = rwkv_direction(bool(d), main, w_pre, a_pre, w0[d], a0[d], k_k, k_a, rwkv0[:, d])
        ys.append(y_d)
        a_pres.append(a_pre)
        finals.append(fin)
    out = rwkv_post(ys[0], ys[1], main, a_pres[0], a_pres[1], a0, k_a, r_k.reshape(-1), lnx_w, lnx_b)
    return matmul_gated_residual(out, w_out, x, gate), jnp.stack(finals, axis=1)


def kernel(x_prompt, x_sample, state_s5_re, state_s5_im, state_gla, state_rwkv, c, c_ctx, norm_w, ada_w, ada_b, final_norm_w, e_w_in, e_w_out, s5_lambda_re, s5_lambda_im, s5_log_step, s5_b_re, s5_b_im, s5_c_re, s5_c_im, s5_d, s5_glu_w, s5_glu_b, gla_decay_up, gla_decay_b, gla_norm_w, o_w_in, o_w_out, rwkv_mu, rwkv_w0, rwkv_w2, rwkv_a0, rwkv_a2, rwkv_k_k, rwkv_k_a, rwkv_r_k, rwkv_lnx_w, rwkv_lnx_b):
    bp = x_prompt.shape[0]
    depth = norm_w.shape[0]
    x_ctx = x_prompt
    x_lat, lat_add = x_sample, _grid_pos_embed(x_sample.shape[1])
    z_s5 = jnp.zeros((bp, N_DIR, S5_GROUPS, S5_STATE), F32)
    z_rwkv = jnp.zeros((bp, N_DIR, RWKV_HEADS, RWKV_HEAD, RWKV_HEAD), F32)
    new_s5_re, new_s5_im, new_gla, new_rwkv = [], [], [], []
    n_lat = c.shape[0]
    cond = jnp.concatenate([c, c_ctx[None]], axis=0)
    for i in range(depth):
        j = i // 2
        shift, scale, gate = adaln(cond, ada_w, ada_b, i)
        gt_l, gt_c = gate[:n_lat], jnp.broadcast_to(gate[n_lat:], (bp, D_MODEL))
        mu = rwkv_mu[j] if i % 2 else None
        h_ctx = norm_mod(x_ctx, norm_w[i], scale[n_lat:], shift[n_lat:], mu=mu)
        h_lat = norm_mod(x_lat, norm_w[i], scale[:n_lat], shift[:n_lat], add=lat_add, mu=mu)
        if i % 2 == 0:
            s5_ops = s5_operators(s5_lambda_re[j], s5_lambda_im[j], s5_log_step[j], s5_b_re[j], s5_b_im[j],
                                  s5_c_re[j], s5_c_im[j], s5_d[j])
            p = (e_w_in[j], e_w_out[j], s5_ops, s5_glu_w[j], s5_glu_b[j], gla_decay_up[j], gla_decay_b[j],
                 gla_norm_w[j])
            x_ctx, fr, fi, fg = _even_mixer(x_ctx, None, gt_c, h_ctx, z_s5, z_s5, None, *p)
            x_lat, _, _, _ = _even_mixer(x_lat, lat_add, gt_l, h_lat, state_s5_re[:, j], state_s5_im[:, j],
                                         state_gla[:, j], *p)
            lat_add = None
            new_s5_re.append(fr)
            new_s5_im.append(fi)
            new_gla.append(fg)
        else:
            p = (o_w_in[j], o_w_out[j], rwkv_w0[j], rwkv_w2[j], rwkv_a0[j], rwkv_a2[j],
                 rwkv_k_k[j], rwkv_k_a[j], rwkv_r_k[j], rwkv_lnx_w[j], rwkv_lnx_b[j])
            x_ctx, fw = _odd_mixer(x_ctx, gt_c, h_ctx, z_rwkv, *p)
            x_lat, _ = _odd_mixer(x_lat, gt_l, h_lat, state_rwkv[:, j], *p)
            new_rwkv.append(fw)
    if lat_add is not None:
        x_lat = x_lat + lat_add
    y_prompt = final_norm(x_ctx, final_norm_w)
    y_sample = final_norm(x_lat, final_norm_w)
    return (y_prompt, y_sample, jnp.stack(new_s5_re, axis=1), jnp.stack(new_s5_im, axis=1),
            jnp.stack(new_gla, axis=1), jnp.stack(new_rwkv, axis=1))
```

```python
import functools

import jax
import jax.numpy as jnp
from jax import lax
from jax.experimental import pallas as pl
from jax.experimental.pallas import tpu as pltpu

D_MODEL = 2048
GRID_W = 64
POS_BASE = 10000.0
N_DIR = 2
EPS = 1e-6
S5_W = 1024
S5_GROUP_CH = 16
S5_GROUPS = 64
S5_STATE = 64
S5_CHUNK = 16
S5_TILE_GROUPS = 8
GLA_HEADS = 6
GLA_DV = 512
GLA_DK = 256
GLA_DK_W = 1536
GLA_DV_W = 3072
GLA_RANK = 16
GLA_NORMALIZER = 16.0
GLA_CHUNK = 64
GLA_NC = 16
GLA_LOG_DECAY_MIN = -1.0
EVEN_SIZES = (S5_W, S5_W, GLA_DK_W, GLA_DK_W, GLA_DV_W, GLA_DV_W, N_DIR * GLA_RANK)
RWKV_W = 2048
RWKV_HEAD = 64
RWKV_HEADS = 32
RWKV_DECAY_RANK = 96
RWKV_ICLR_RANK = 96
RWKV_LNX_EPS = 64e-5
ODD_SIZES = (RWKV_W, RWKV_W, RWKV_W, RWKV_W, N_DIR * RWKV_DECAY_RANK, N_DIR * RWKV_ICLR_RANK)
RWKV_CHUNK = 64
RWKV_CPAIRS = 16
RWKV_SUB = 16
LANES = 128

VMEM_LIMIT = 48 * 1024 * 1024
HI = lax.Precision.HIGHEST
BF16 = jnp.bfloat16
F32 = jnp.float32


def _mm_kernel(x_ref, w_ref, o_ref):
    o_ref[...] = jnp.dot(x_ref[...], w_ref[...], preferred_element_type=F32).astype(o_ref.dtype)


def _pick(n, prefs):
    for p in prefs:
        if n % p == 0:
            return p
    return n


def matmul(x, w, n_cols=None, out_dtype=F32):
    m, k = x.shape
    n = w.shape[1] if n_cols is None else n_cols
    x = x.astype(BF16)
    w = w.astype(BF16)
    tm = _pick(m, (1024, 512, 256, 128, 64, 32, 16, 8))
    tn = _pick(n, (1024, 512, 384, 256, 128))
    return pl.pallas_call(
        _mm_kernel,
        grid=(m // tm, n // tn),
        in_specs=[pl.BlockSpec((tm, k), lambda i, j: (i, 0)),
                  pl.BlockSpec((k, tn), lambda i, j: (0, j))],
        out_specs=pl.BlockSpec((tm, tn), lambda i, j: (i, j)),
        out_shape=jax.ShapeDtypeStruct((m, n), out_dtype),
        compiler_params=pltpu.CompilerParams(
            dimension_semantics=("arbitrary", "arbitrary"), vmem_limit_bytes=VMEM_LIMIT),
        name="proj_matmul",
    )(x, w)


def _mm3(h, w, n_cols=None, out_dtype=F32):
    b, l, k = h.shape
    return matmul(h.reshape(b * l, k), w, n_cols, out_dtype).reshape(b, l, -1)


def _mm_residual_kernel(has_add, x_ref, w_ref, res_ref, gate_ref, *refs):
    o_ref = refs[-1]
    acc = jnp.dot(x_ref[...], w_ref[...], preferred_element_type=F32)
    res = res_ref[...] + refs[0][...] if has_add else res_ref[...]
    o_ref[...] = res + gate_ref[0] * acc


def matmul_gated_residual(x, w, res, gate, res_add=None):
    bsz, L, k = x.shape
    n = w.shape[1]
    m = bsz * L
    tm = _pick(L, (1024, 512, 256, 128))
    tn = _pick(n, (1024, 512, 256, 128) if k <= 2048 else (512, 256, 128))
    per_b = L // tm
    in_specs = [pl.BlockSpec((tm, k), lambda i, j: (i, 0)),
                pl.BlockSpec((k, tn), lambda i, j: (0, j)),
                pl.BlockSpec((tm, tn), lambda i, j: (i, j)),
                pl.BlockSpec((1, 1, tn), lambda i, j: (i // per_b, 0, j))]
    args = [x.reshape(m, k).astype(BF16), w.astype(BF16), res.reshape(m, n), gate.reshape(bsz, 1, n)]
    if res_add is not None:
        in_specs.append(pl.BlockSpec((tm, tn), lambda i, j: (i % per_b, j)))
        args.append(res_add)
    out = pl.pallas_call(
        functools.partial(_mm_residual_kernel, res_add is not None),
        grid=(m // tm, n // tn),
        in_specs=in_specs,
        out_specs=pl.BlockSpec((tm, tn), lambda i, j: (i, j)),
        out_shape=jax.ShapeDtypeStruct((m, n), F32),
        compiler_params=pltpu.CompilerParams(
            dimension_semantics=("arbitrary", "arbitrary"), vmem_limit_bytes=VMEM_LIMIT),
        name="proj_residual",
    )(*args)
    return out.reshape(bsz, L, n)


def _mm_glu_kernel(x_ref, w_ref, b_ref, g_ref, xt_ref, o_ref):
    acc = jnp.dot(x_ref[...], w_ref[...], preferred_element_type=F32) + b_ref[...]
    gy = xt_ref[...].astype(F32)
    gate = g_ref[...].astype(F32)
    o_ref[...] = (gy * jax.nn.sigmoid(acc) * (gate * jax.nn.sigmoid(gate))).astype(o_ref.dtype)


def matmul_glu(gy, w, b, main, g_col0, n_total):
    m, k = gy.shape
    n = w.shape[1]
    tm = _pick(m, (1024, 512, 256, 128))
    tn = 512
    return pl.pallas_call(
        _mm_glu_kernel,
        grid=(m // tm, n // tn),
        in_specs=[pl.BlockSpec((tm, k), lambda i, j: (i, 0)),
                  pl.BlockSpec((k, tn), lambda i, j: (0, j)),
                  pl.BlockSpec((1, tn), lambda i, j: (0, j)),
                  pl.BlockSpec((tm, tn), lambda i, j: (i, g_col0 // tn + j)),
                  pl.BlockSpec((tm, tn), lambda i, j: (i, j))],
        out_specs=pl.BlockSpec((tm, tn), lambda i, j: (i, j)),
        out_shape=jax.ShapeDtypeStruct((m, n_total), BF16),
        compiler_params=pltpu.CompilerParams(
            dimension_semantics=("arbitrary", "arbitrary"), vmem_limit_bytes=VMEM_LIMIT),
        name="s5_glu_gate",
    )(gy, w.astype(BF16), b.reshape(1, n), main, gy)


def _norm_mod_kernel(has_add, has_shift, n_row_blocks, x_ref, nw_ref, sc_ref, sh_ref, *refs):
    o_ref = refs[-1]

    def modulated(x):
        inv = lax.rsqrt(jnp.mean(x * x, axis=-1, keepdims=True) + EPS)
        return x * inv * nw_ref[...] * (1.0 + sc_ref[0]) + sh_ref[0]
    x = x_ref[0]
    if has_add:
        x = x + refs[0][...]
    h = modulated(x)
    if has_shift:
        prev_ref, next_ref, mu_ref = refs[0], refs[1], refs[2]
        i = pl.program_id(1)
        tr = h.shape[0]
        row = lax.broadcasted_iota(jnp.int32, h.shape, 0)
        before = jnp.where(i > 0, modulated(prev_ref[0])[7:8], 0.0)
        after = jnp.where(i < n_row_blocks - 1, modulated(next_ref[0])[0:1], 0.0)
        h_prev = jnp.where(row == 0, before, pltpu.roll(h, 1, 0))
        h_next = jnp.where(row == tr - 1, after, pltpu.roll(h, tr - 1, 0))
        h = h + mu_ref[0:1] * (h_prev - h) + mu_ref[1:2] * (h_next - h)
    o_ref[0] = h.astype(o_ref.dtype)


def norm_mod(x, nw, scale, shift, add=None, mu=None):
    assert add is None or mu is None
    bsz, L, dm = x.shape
    tr = _pick(L, (256, 128, 64))
    nb = scale.shape[0]
    nblk = L // tr
    cond = pl.BlockSpec((1, 1, dm), lambda b, i: (b if nb > 1 else 0, 0, 0))
    in_specs = [pl.BlockSpec((1, tr, dm), lambda b, i: (b, i, 0)), pl.BlockSpec((1, dm), lambda b, i: (0, 0)),
                cond, cond]
    args = [x, nw.reshape(1, dm), scale.reshape(nb, 1, dm), shift.reshape(nb, 1, dm)]
    if add is not None:
        in_specs.append(pl.BlockSpec((tr, dm), lambda b, i: (i, 0)))
        args.append(add)
    if mu is not None:
        r8 = tr // 8
        in_specs += [pl.BlockSpec((1, 8, dm), lambda b, i: (b, jnp.maximum(i * r8 - 1, 0), 0)),
                     pl.BlockSpec((1, 8, dm), lambda b, i: (b, jnp.minimum((i + 1) * r8, L // 8 - 1), 0)),
                     pl.BlockSpec((2, dm), lambda b, i: (0, 0))]
        args += [x, x, mu]
    return pl.pallas_call(
        functools.partial(_norm_mod_kernel, add is not None, mu is not None, nblk),
        grid=(bsz, nblk),
        in_specs=in_specs,
        out_specs=pl.BlockSpec((1, tr, dm), lambda b, i: (b, i, 0)),
        out_shape=jax.ShapeDtypeStruct((bsz, L, dm), BF16),
        compiler_params=pltpu.CompilerParams(
            dimension_semantics=("arbitrary", "arbitrary"), vmem_limit_bytes=VMEM_LIMIT),
        name="norm_mod",
    )(*args)


def _final_norm_kernel(x_ref, nw_ref, o_ref):
    x = x_ref[0]
    o_ref[0] = x * lax.rsqrt(jnp.mean(x * x, axis=-1, keepdims=True) + EPS) * nw_ref[...]


def final_norm(x, nw):
    bsz, L, dm = x.shape
    tr = _pick(L, (256, 128, 64))
    return pl.pallas_call(
        _final_norm_kernel,
        grid=(bsz, L // tr),
        in_specs=[pl.BlockSpec((1, tr, dm), lambda b, i: (b, i, 0)), pl.BlockSpec((1, dm), lambda b, i: (0, 0))],
        out_specs=pl.BlockSpec((1, tr, dm), lambda b, i: (b, i, 0)),
        out_shape=jax.ShapeDtypeStruct((bsz, L, dm), F32),
        compiler_params=pltpu.CompilerParams(
            dimension_semantics=("arbitrary", "arbitrary"), vmem_limit_bytes=VMEM_LIMIT),
        name="final_norm",
    )(x, nw.reshape(1, dm))


def s5_operators(lam_re, lam_im, log_step, b_re, b_im, c_re, c_im, d_skip):
    T = S5_CHUNK
    dt = jnp.exp(log_step)[..., None]
    mag = jnp.exp(lam_re * dt)
    ab_re, ab_im = mag * jnp.cos(lam_im * dt), mag * jnp.sin(lam_im * dt)
    den = lam_re * lam_re + lam_im * lam_im
    f_re = ((ab_re - 1.0) * lam_re + ab_im * lam_im) / den
    f_im = (ab_im * lam_re - (ab_re - 1.0) * lam_im) / den
    bb_re = f_re[..., None] * b_re - f_im[..., None] * b_im
    bb_im = f_re[..., None] * b_im + f_im[..., None] * b_re
    kk = jnp.arange(T + 1, dtype=F32)[:, None, None, None]
    pmag = jnp.exp(kk * (lam_re * dt))
    pr = pmag * jnp.cos(kk * (lam_im * dt))
    pi = pmag * jnp.sin(kk * (lam_im * dt))
    zr = pr[:T, :, :, :, None] * bb_re - pi[:T, :, :, :, None] * bb_im
    zi = pr[:T, :, :, :, None] * bb_im + pi[:T, :, :, :, None] * bb_re
    kern = (jnp.einsum('dghp,kdgpj->kdghj', c_re, zr, precision=HI)
            - jnp.einsum('dghp,kdgpj->kdghj', c_im, zi, precision=HI))
    t_idx = jnp.arange(T)[:, None]
    s_idx = jnp.arange(T)[None, :]
    lag_f = t_idx - s_idx
    lag_b = s_idx - t_idx
    m_f = jnp.where((lag_f >= 0)[:, :, None, None, None], kern[:, 0][jnp.clip(lag_f, 0, T - 1)], 0.0)
    m_b = jnp.where((lag_b >= 0)[:, :, None, None, None], kern[:, 1][jnp.clip(lag_b, 0, T - 1)], 0.0)
    m = m_f + m_b
    eye_t = jnp.eye(T, dtype=F32)[:, :, None, None, None]
    eye_h = jnp.eye(S5_GROUP_CH, dtype=F32)[None, None, None]
    m = m + eye_t * eye_h * d_skip.reshape(S5_GROUPS, S5_GROUP_CH)[None, None, :, :, None]
    g = m.shape[2]
    m_t = m.transpose(2, 1, 4, 0, 3).reshape(g, T * S5_GROUP_CH, T * S5_GROUP_CH)
    pf_r, pf_i = pr[T - 1::-1][:T, 0], pi[T - 1::-1][:T, 0]
    pb_r, pb_i = pr[:T, 1], pi[:T, 1]

    def f_mat(p_r, p_i, d):
        re = p_r[..., None] * bb_re[d][None] - p_i[..., None] * bb_im[d][None]
        im = p_r[..., None] * bb_im[d][None] + p_i[..., None] * bb_re[d][None]
        re = re.transpose(1, 0, 3, 2).reshape(g, T * S5_GROUP_CH, S5_STATE)
        im = im.transpose(1, 0, 3, 2).reshape(g, T * S5_GROUP_CH, S5_STATE)
        return re, im
    ff_re, ff_im = f_mat(pf_r, pf_i, 0)
    fb_re, fb_im = f_mat(pb_r, pb_i, 1)
    a_t = jnp.concatenate([m_t, ff_re, fb_re, ff_im, fb_im], axis=-1)
    ef_r, ef_i = pr[1:T + 1, 0], pi[1:T + 1, 0]
    eb_r, eb_i = pr[T:0:-1, 1], pi[T:0:-1, 1]

    def e_mat(p_r, p_i, d):
        er = c_re[d][None] * p_r[:, :, None, :] - c_im[d][None] * p_i[:, :, None, :]
        ei = -(c_re[d][None] * p_i[:, :, None, :] + c_im[d][None] * p_r[:, :, None, :])
        er = er.transpose(1, 3, 0, 2).reshape(g, S5_STATE, T * S5_GROUP_CH)
        ei = ei.transpose(1, 3, 0, 2).reshape(g, S5_STATE, T * S5_GROUP_CH)
        return er, ei
    efr, efi = e_mat(ef_r, ef_i, 0)
    ebr, ebi = e_mat(eb_r, eb_i, 1)
    e_t = jnp.concatenate([efr, ebr, efi, ebi], axis=1)
    lam_t = jnp.concatenate([pr[T, 0], pr[T, 1], pi[T, 0], pi[T, 1]], axis=-1)[:, None, :]
    return a_t.astype(BF16), e_t.astype(BF16), lam_t


def _s5_kernel(n_steps, bsz, pair, x8_ref, sel_ref, at_ref, et_ref, lam_ref, h0_ref, y_ref, hfin_ref, z_ref,
               hent_ref):
    P = S5_STATE
    ut = jnp.dot(x8_ref[0], sel_ref[0], preferred_element_type=F32).astype(BF16)
    z_ref[...] = jnp.dot(ut, at_ref[0], preferred_element_type=F32)
    lam = lam_ref[0]
    a_re, a_im = lam[:, 0:2 * P], lam[:, 2 * P:4 * P]
    h0 = h0_ref[0]
    h0_re, h0_im = h0[:, 0:2 * P], h0[:, 2 * P:4 * P]
    fwd_lanes = lax.broadcasted_iota(jnp.int32, (bsz, 2 * P), 1) < P
    m_re, m_im = a_re, a_im
    if pair:
        cols, half = z_ref.shape[0], bsz // 2
        g_re, g_im = z_ref[:, 4 * P:6 * P], z_ref[:, 6 * P:8 * P]
        rows = lax.broadcasted_iota(jnp.int32, (cols, 2 * P), 0)
        fwd_all = lax.broadcasted_iota(jnp.int32, (cols, 2 * P), 1) < P

        def neighbour(g):
            up = jnp.where(rows < cols - half, pltpu.roll(g, cols - half, 0), 0.0)
            down = jnp.where(rows >= half, pltpu.roll(g, half, 0), 0.0)
            return jnp.where(fwd_all, up, down)
        z_ref[:, 4 * P:6 * P] = a_re * g_re - a_im * g_im + neighbour(g_re)
        z_ref[:, 6 * P:8 * P] = a_re * g_im + a_im * g_re + neighbour(g_im)
        lo = lax.broadcasted_iota(jnp.int32, (bsz, 2 * P), 0) < half
        ah_re = a_re * h0_re - a_im * h0_im
        ah_im = a_re * h0_im + a_im * h0_re

        def swap(x):
            return pltpu.roll(x, half, 0)
        f_re = h0_re + swap(ah_re + jnp.where(lo, g_re[0:bsz], 0.0))
        f_im = h0_im + swap(ah_im + jnp.where(lo, g_im[0:bsz], 0.0))
        b_re = swap(h0_re) + ah_re + swap(jnp.where(lo, 0.0, g_re[cols - bsz:cols]))
        b_im = swap(h0_im) + ah_im + swap(jnp.where(lo, 0.0, g_im[cols - bsz:cols]))
        h0_re, h0_im = jnp.where(fwd_lanes, f_re, b_re), jnp.where(fwd_lanes, f_im, b_im)
        m_re, m_im = a_re * a_re - a_im * a_im, 2.0 * a_re * a_im

    def step(c, carry):
        h_re, h_im = carry
        rf = pl.ds(pl.multiple_of(c * bsz, 8), bsz)
        rb = pl.ds(pl.multiple_of((n_steps - 1 - c) * bsz, 8), bsz)
        hent_ref[rf, 0:P] = h_re[:, 0:P]
        hent_ref[rb, P:2 * P] = h_re[:, P:2 * P]
        hent_ref[rf, 2 * P:3 * P] = h_im[:, 0:P]
        hent_ref[rb, 3 * P:4 * P] = h_im[:, P:2 * P]
        g_re = jnp.where(fwd_lanes, z_ref[rf, 4 * P:6 * P], z_ref[rb, 4 * P:6 * P])
        g_im = jnp.where(fwd_lanes, z_ref[rf, 6 * P:8 * P], z_ref[rb, 6 * P:8 * P])
        return m_re * h_re - m_im * h_im + g_re, m_re * h_im + m_im * h_re + g_im
    h_re, h_im = lax.fori_loop(0, n_steps, step, (h0_re, h0_im))
    if pair:
        h_re = jnp.where(fwd_lanes, h_re, pltpu.roll(h_re, bsz // 2, 0))
        h_im = jnp.where(fwd_lanes, h_im, pltpu.roll(h_im, bsz // 2, 0))
    hfin_ref[0, :, 0:2 * P] = h_re
    hfin_ref[0, :, 2 * P:4 * P] = h_im
    y = z_ref[:, 0:4 * P] + jnp.dot(hent_ref[...].astype(BF16), et_ref[0], preferred_element_type=F32)
    y_ref[0] = jax.nn.gelu(y).astype(y_ref.dtype)


def _s5_unpack_kernel(yt_ref, selt_ref, o_ref):
    acc = jnp.dot(yt_ref[0], selt_ref[0], preferred_element_type=F32)
    for gl in range(1, S5_TILE_GROUPS):
        acc = acc + jnp.dot(yt_ref[gl], selt_ref[gl], preferred_element_type=F32)
    o_ref[0] = acc.astype(o_ref.dtype)


def s5_scan(u, ops, h0_re, h0_im):
    a_t, e_t, lam_t = ops
    b_real, L, _ = u.shape
    T, G, H, P = S5_CHUNK, S5_GROUPS, S5_GROUP_CH, S5_STATE
    TG = S5_TILE_GROUPS
    n = L // T
    pair = b_real == 4 and n % 2 == 0
    bsz = -(-b_real // 8) * 8
    rpc = b_real if pair else bsz
    n_steps = n // 2 if pair else n
    cols = n * rpc
    x8 = u.reshape(b_real, n, T, G // TG, LANES).transpose(3, 1, 0, 2, 4).astype(BF16)
    x8 = jnp.pad(x8, ((0, 0), (0, 0), (0, rpc - b_real), (0, 0), (0, 0))).reshape(G // TG, cols, T * LANES)
    src = jnp.arange(T * LANES)
    dst = jnp.arange(T * H)
    sel = ((src[None, :, None] // LANES == dst[None, None, :] // H)
           & (src[None, :, None] % H == dst[None, None, :] % H)
           & ((src[None, :, None] % LANES) // H == jnp.arange(TG)[:, None, None])).astype(BF16)
    h0 = jnp.concatenate([h0_re[:, 0], h0_re[:, 1], h0_im[:, 0], h0_im[:, 1]], axis=-1)
    h0 = jnp.pad(h0.transpose(1, 0, 2), ((0, 0), (0, bsz - b_real), (0, 0)))
    yt, hfin = pl.pallas_call(
        functools.partial(_s5_kernel, n_steps, bsz, pair),
        grid=(G,),
        in_specs=[pl.BlockSpec((1, cols, T * LANES), lambda g: (g // TG, 0, 0)),
                  pl.BlockSpec((1, T * LANES, T * H), lambda g: (g % TG, 0, 0)),
                  pl.BlockSpec((1, T * H, 8 * P), lambda g: (g, 0, 0)),
                  pl.BlockSpec((1, 4 * P, T * H), lambda g: (g, 0, 0)),
                  pl.BlockSpec((1, 1, 4 * P), lambda g: (g, 0, 0)),
                  pl.BlockSpec((1, bsz, 4 * P), lambda g: (g, 0, 0))],
        out_specs=[pl.BlockSpec((1, cols, T * H), lambda g: (g, 0, 0)),
                   pl.BlockSpec((1, bsz, 4 * P), lambda g: (g, 0, 0))],
        out_shape=[jax.ShapeDtypeStruct((G, cols, T * H), BF16),
                   jax.ShapeDtypeStruct((G, bsz, 4 * P), F32)],
        scratch_shapes=[pltpu.VMEM((cols, 8 * P), F32), pltpu.VMEM((cols, 4 * P), F32)],
        compiler_params=pltpu.CompilerParams(dimension_semantics=("arbitrary",), vmem_limit_bytes=VMEM_LIMIT),
        name="s5_chunk_scan",
    )(x8, sel, a_t, e_t, lam_t, h0)
    tr = _pick(cols, (512, 256, 128))
    y8 = pl.pallas_call(
        _s5_unpack_kernel,
        grid=(G // TG, cols // tr),
        in_specs=[pl.BlockSpec((TG, tr, T * H), lambda t, i: (t, i, 0)),
                  pl.BlockSpec((TG, T * H, T * LANES), lambda t, i: (0, 0, 0))],
        out_specs=pl.BlockSpec((1, tr, T * LANES), lambda t, i: (t, i, 0)),
        out_shape=jax.ShapeDtypeStruct((G // TG, cols, T * LANES), BF16),
        compiler_params=pltpu.CompilerParams(
            dimension_semantics=("arbitrary", "arbitrary"), vmem_limit_bytes=VMEM_LIMIT),
        name="s5_unpack",
    )(yt, sel.transpose(0, 2, 1))
    y = y8.reshape(G // TG, n, rpc, T, LANES)[:, :, :b_real].transpose(2, 1, 3, 0, 4).reshape(b_real, L, G * H)
    hfin = hfin[:, :b_real].transpose(1, 0, 2)
    fin_re = jnp.stack([hfin[..., 0:P], hfin[..., P:2 * P]], axis=1)
    fin_im = jnp.stack([hfin[..., 2 * P:3 * P], hfin[..., 3 * P:4 * P]], axis=1)
    return y, fin_re, fin_im


def _dot_t(a, b):
    return lax.dot_general(a, b, (((1,), (1,)), ((), ())), preferred_element_type=F32)


def _dot_mask(mask_bf16, x, x_rows_to_sublanes=False):
    def d(b):
        if x_rows_to_sublanes:
            return lax.dot_general(b, mask_bf16, (((0,), (0,)), ((), ())), preferred_element_type=F32)
        return jnp.dot(mask_bf16, b, preferred_element_type=F32)
    x1 = x.astype(BF16)
    r1 = x - x1.astype(F32)
    x2 = r1.astype(BF16)
    x3 = (r1 - x2.astype(F32)).astype(BF16)
    return d(x1) + (d(x2) + d(x3))


def _gla_block_kernel(n_blocks, NC, has_s0, q_ref, k_ref, v_ref, g_ref, lr_ref, up_ref, db_ref, nw_ref, dst_ref,
                      *refs):
    s0_ref = refs[0] if has_s0 else None
    out_ref, sfin_ref, s_ref, of_ref, qd_ref, ov_ref, kv_ref, dc_ref = refs[1:] if has_s0 else refs
    C = GLA_CHUNK
    R = C * NC
    d = pl.program_id(2)
    c = pl.program_id(3)
    bidx = jnp.where(d == 0, c, n_blocks - 1 - c)

    @pl.when(c == 0)
    def _():
        s_ref[...] = s0_ref[0, 0, 0] if has_s0 else jnp.zeros_like(s_ref)

    z = _mxu(lr_ref[0], up_ref[0], split=True) + db_ref[0]
    gc = jnp.maximum(jax.nn.log_sigmoid(z) * (1.0 / GLA_NORMALIZER), GLA_LOG_DECAY_MIN)
    row_c = lax.broadcasted_iota(jnp.int32, (C, C), 0)
    col_c = lax.broadcasted_iota(jnp.int32, (C, C), 1)
    seen_c = jnp.where(d == 0, row_c - col_c, col_c - row_c) >= 0
    seen_bf = seen_c.astype(BF16)
    rs = [slice(i * C, (i + 1) * C) for i in range(NC)]
    bcum_c = [_dot_mask(seen_bf, gc[r]) for r in rs]
    btot_c = [jnp.broadcast_to(jnp.where(d == 0, b[C - 1:C], b[0:1]), (C, GLA_DK)) for b in bcum_c]
    bcum = jnp.concatenate(bcum_c, axis=0)
    btot = jnp.concatenate(btot_c, axis=0)
    q_dec = (q_ref[0].astype(F32) * (GLA_DK ** -0.5) * jnp.exp(bcum)).astype(BF16)
    k = k_ref[0].astype(F32)
    k_inv = (k * jnp.exp(-bcum)).astype(BF16)
    k_end = (k * jnp.exp(btot - bcum)).astype(BF16)
    v = v_ref[0].astype(BF16)
    ones_c = jnp.ones((C, LANES), BF16)
    qd_ref[...] = q_dec.reshape(NC, C, GLA_DK)
    att = [jnp.where(seen_c, _dot_t(q_dec[r], k_inv[r]), 0.0).astype(BF16) for r in rs]
    for i in range(NC):
        kv_ref[i] = lax.dot_general(k_end[rs[i]], v[rs[i]], (((0,), (0,)), ((), ())), preferred_element_type=F32)
    for i in range(NC):
        ov_ref[i] = jnp.dot(att[i], v[rs[i]], preferred_element_type=F32)
    for i in range(NC):
        dc_ref[i] = _dot_mask(ones_c, gc[rs[i]], x_rows_to_sublanes=True)

    for i in range(NC):
        ci = jnp.where(d == 0, i, NC - 1 - i)
        s_old = s_ref[...]
        ov_ref[ci] = ov_ref[ci] + jnp.dot(qd_ref[ci], s_old.astype(BF16), preferred_element_type=F32)
        s_ref[...] = jnp.exp(dc_ref[ci][:, 0:1]) * s_old + kv_ref[ci]
    rows = pl.ds(pl.multiple_of(bidx * R, R), R)

    @pl.when(d == 0)
    def _():
        of_ref[rows, :] = ov_ref[...].reshape(R, GLA_DV)

    @pl.when(d == 1)
    def _():
        tot = of_ref[rows, :] + ov_ref[...].reshape(R, GLA_DV)
        nrm = tot * lax.rsqrt(jnp.mean(tot * tot, axis=-1, keepdims=True) + EPS) * nw_ref[0]
        gate = g_ref[0].astype(F32)
        out_ref[0] = (nrm * (gate * jax.nn.sigmoid(gate))).astype(out_ref.dtype)

    @pl.when(c == n_blocks - 1)
    def _():
        sfin_ref[0, 0, 0] = s_ref[...]


def gla_mix(main, dec_lr, dec_up, dec_b, gla_nw, s0, dst):
    bsz, L, _ = main.shape
    H, DK, DV = GLA_HEADS, GLA_DK, GLA_DV
    nc = min(GLA_NC, L // GLA_CHUNK)
    C = GLA_CHUNK * nc
    n = L // C
    q_blk = sum(EVEN_SIZES[:2]) // DK
    k_blk = sum(EVEN_SIZES[:3]) // DK
    v_blk = sum(EVEN_SIZES[:4]) // DV
    g_blk = sum(EVEN_SIZES[:5]) // DV
    up = jnp.zeros((N_DIR, LANES, GLA_DK_W), F32)
    for d in range(N_DIR):
        up = up.at[d, d * GLA_RANK:(d + 1) * GLA_RANK].set(dec_up[d])
    db = dec_b.reshape(N_DIR, 1, GLA_DK_W)
    nw = gla_nw.reshape(1, GLA_DV_W)

    def chunk(d, c):
        return c + d * (n - 1 - 2 * c)

    def out_chunk(d, c):
        return (n - 1) - d * c
    state = pl.BlockSpec((1, 1, 1, DK, DV), lambda b, h, d, c: (b, d, h, 0, 0))
    has_s0 = s0 is not None
    out, sfin = pl.pallas_call(
        functools.partial(_gla_block_kernel, n, nc, has_s0),
        grid=(bsz, H, N_DIR, n),
        in_specs=[pl.BlockSpec((1, C, DK), lambda b, h, d, c: (b, chunk(d, c), q_blk + h)),
                  pl.BlockSpec((1, C, DK), lambda b, h, d, c: (b, chunk(d, c), k_blk + h)),
                  pl.BlockSpec((1, C, DV), lambda b, h, d, c: (b, chunk(d, c), v_blk + h)),
                  pl.BlockSpec((1, C, DV), lambda b, h, d, c: (b, chunk(d, c), g_blk + h)),
                  pl.BlockSpec((1, C, LANES), lambda b, h, d, c: (b, chunk(d, c), 0)),
                  pl.BlockSpec((1, LANES, DK), lambda b, h, d, c: (d, 0, h)),
                  pl.BlockSpec((1, 1, DK), lambda b, h, d, c: (d, 0, h)),
                  pl.BlockSpec((1, DV), lambda b, h, d, c: (0, h)),
                  pl.BlockSpec(memory_space=pl.ANY)] + ([state] if has_s0 else []),
        input_output_aliases={8: 0},
        out_specs=[pl.BlockSpec((1, C, DV), lambda b, h, d, c: (b, out_chunk(d, c), S5_W // DV + h)), state],
        out_shape=[jax.ShapeDtypeStruct(dst.shape, BF16),
                   jax.ShapeDtypeStruct((bsz, N_DIR, H, DK, DV), F32)],
        scratch_shapes=[pltpu.VMEM((DK, DV), F32), pltpu.VMEM((L, DV), F32),
                        pltpu.VMEM((nc, GLA_CHUNK, DK), BF16), pltpu.VMEM((nc, GLA_CHUNK, DV), F32),
                        pltpu.VMEM((nc, DK, DV), F32), pltpu.VMEM((nc, DK, LANES), F32)],
        compiler_params=pltpu.CompilerParams(
            dimension_semantics=("arbitrary",) * 4, vmem_limit_bytes=VMEM_LIMIT),
        name="gla_chunk_scan",
    )(main, main, main, main, dec_lr, up, db, nw, dst, *([s0] if has_s0 else []))
    return out, sfin


def _split_bf16(x):
    hi = x.astype(BF16)
    return hi, (x - hi.astype(F32)).astype(BF16)


def _mxu(x, y, dims=(((1,), (0,)), ((), ())), split=False):
    def d(a, b):
        return lax.dot_general(a, b, dims, preferred_element_type=F32)
    if not split:
        return d(x.astype(BF16), y.astype(BF16))
    xh, xl = _split_bf16(x)
    yh, yl = _split_bf16(y)
    return d(xh, yh) + (d(xh, yl) + d(xl, yh))


def _rwkv_fs_kernel(n_chunks, rev, r_ref, k_ref, v_ref, wp_ref, ap_ref, w0_ref, a0_ref, kk_ref, ka_ref, h0_ref,
                    y_ref, hfin_ref, h_ref):
    T, N, SB = RWKV_CHUNK, RWKV_HEAD, RWKV_SUB
    NB = T // SB
    c = pl.program_id(2)

    @pl.when(c == 0)
    def _():
        h_ref[...] = h0_ref[0]

    lane = lax.broadcasted_iota(jnp.int32, (T, LANES), 1)
    row = lax.broadcasted_iota(jnp.int32, (T, LANES), 0)
    lo = lane < N
    col = lane % N
    order = (col - row) if rev else (row - col)
    seen = order >= 0
    before = order > 0
    eye = row == col
    sq_r = lax.broadcasted_iota(jnp.int32, (T, T), 0)
    sq_c = lax.broadcasted_iota(jnp.int32, (T, T), 1)
    seen_sq = (((sq_c - sq_r) if rev else (sq_r - sq_c)) >= 0).astype(BF16)
    same_head = ((lax.broadcasted_iota(jnp.int32, (LANES, LANES), 0) < N)
                 == (lax.broadcasted_iota(jnp.int32, (LANES, LANES), 1) < N)).astype(BF16)
    col_sb = lax.broadcasted_iota(jnp.int32, (SB, LANES), 1) % N
    row_dims = (((0,), (0,)), ((), ()))
    lane_dims = (((1,), (1,)), ((), ()))

    def bd(x):
        return jnp.concatenate([jnp.where(lo, x, 0.0), jnp.where(lo, 0.0, x)], axis=0)

    def pp(x, y, split=False):
        return _mxu(x, bd(y), split=split)

    def ptp(x, y):
        full = _mxu(x, y, row_dims)
        return jnp.where(lo, full[:N], full[N:])

    w_log = -jax.nn.softplus(-(wp_ref[0] + w0_ref[...])) - 0.5
    lw_all = -jnp.exp(w_log)
    iclr_all = jax.nn.sigmoid(ap_ref[0] + a0_ref[...])
    k_all = k_ref[0].astype(F32)
    kd_all = k_all * (1.0 + (iclr_all - 1.0) * ka_ref[...])
    kkr_all = k_all * kk_ref[...]
    cs_all = _dot_mask(seen_sq, lw_all)
    tot_all = jnp.sum(lw_all, axis=0, keepdims=True)
    pairs = range(RWKV_CPAIRS)
    sl = [slice(p * LANES, (p + 1) * LANES) for p in pairs]
    sq_hi = [_split_bf16(kkr_all[:, s] * kkr_all[:, s]) for s in sl]
    ssq = [jnp.dot(sq_hi[p][0], same_head, preferred_element_type=F32)
           + jnp.dot(sq_hi[p][1], same_head, preferred_element_type=F32) for p in pairs]
    kk = [kkr_all[:, sl[p]] / jnp.maximum(jnp.sqrt(ssq[p]), 1e-12) for p in pairs]
    b_in = [kk[p] * iclr_all[:, sl[p]] for p in pairs]
    cs = [cs_all[:, s] for s in sl]
    tot = [tot_all[:, s] for s in sl]
    e_out = [jnp.exp(-cs[p]) for p in pairs]
    at = [-kk[p] * jnp.exp(cs[p] - lw_all[:, sl[p]]) for p in pairs]
    rt = [r_ref[0, :, sl[p]].astype(F32) * jnp.exp(cs[p]) for p in pairs]
    ar = [jnp.concatenate([at[p], rt[p]], axis=0) for p in pairs]
    g1 = [_mxu(ar[p], bd(b_in[p] * e_out[p]), lane_dims) for p in pairs]
    g2 = [_mxu(ar[p], bd(kd_all[:, sl[p]] * e_out[p]), lane_dims) for p in pairs]
    a_ab = [jnp.where(before, g1[p][:T], 0.0) for p in pairs]
    a_rb = [jnp.where(seen, g1[p][T:], 0.0) for p in pairs]
    a_ak = [jnp.where(before, g2[p][:T], 0.0) for p in pairs]
    a_rk = [jnp.where(seen, g2[p][T:], 0.0) for p in pairs]
    v = [v_ref[0, :, sl[p]].astype(F32) for p in pairs]
    akv = [pp(a_ak[p], v[p]) for p in pairs]
    za = [[None] * NB for _ in pairs]
    zu = [[None] * NB for _ in pairs]
    zero_blk = jnp.zeros((SB, LANES), F32)
    for kpos in range(NB):
        bk = NB - 1 - kpos if rev else kpos
        rows = slice(bk * SB, (bk + 1) * SB)
        done = [(m > bk) if rev else (m < bk) for m in range(NB)]
        cur_a = [at[p][rows] for p in pairs]
        cur_u = [akv[p][rows] for p in pairs]
        if kpos > 0:
            for p in pairs:
                zc_a = jnp.concatenate([za[p][m] if done[m] else zero_blk for m in range(NB)], axis=0)
                zc_u = jnp.concatenate([zu[p][m] if done[m] else zero_blk for m in range(NB)], axis=0)
                off = _mxu(a_ab[p][rows], jnp.concatenate([bd(zc_a), bd(zc_u)], axis=1))
                cur_a[p] = cur_a[p] + off[:, :LANES]
                cur_u[p] = cur_u[p] + off[:, LANES:]
        abc = []
        for p in pairs:
            ablk = a_ab[p][rows]
            picked = jnp.concatenate([jnp.where(col_sb == bk * SB + s, ablk, 0.0) for s in range(SB)], axis=0)
            abc.append(jnp.dot(picked.astype(BF16), same_head, preferred_element_type=F32))
        ha = [[cur_a[p][:8], cur_a[p][8:]] for p in pairs]
        hu = [[cur_u[p][:8], cur_u[p][8:]] for p in pairs]
        for j in range(SB - 1):
            s = SB - 1 - j if rev else j
            src, r8 = s // 8, s % 8
            halves = (0, 1) if (s >= 8) == rev else ((0,) if rev else (1,))
            for p in pairs:
                row_a = ha[p][src][r8:r8 + 1]
                row_u = hu[p][src][r8:r8 + 1]
                for hf in halves:
                    coef = abc[p][s * SB + hf * 8:s * SB + hf * 8 + 8]
                    ha[p][hf] = ha[p][hf] + coef * row_a
                    hu[p][hf] = hu[p][hf] + coef * row_u
        for p in pairs:
            za[p][bk] = jnp.concatenate(ha[p], axis=0)
            zu[p][bk] = jnp.concatenate(hu[p], axis=0)
    a_hat = [jnp.concatenate(za[p], axis=0) for p in pairs]
    u_loc = [jnp.concatenate(zu[p], axis=0) for p in pairs]
    h0 = [h_ref[:, sl[p]] for p in pairs]
    q_hat = [rt[p] + pp(a_rb[p], a_hat[p]) for p in pairs]
    for p in pairs:
        lhs = jnp.concatenate([q_hat[p], a_rb[p], a_rk[p]], axis=1)
        rhs = jnp.concatenate([bd(h0[p]), bd(u_loc[p]), bd(v[p])], axis=0)
        y_ref[0, :, sl[p]] = _mxu(lhs, rhs).astype(y_ref.dtype)
    e_end = [jnp.exp(tot[p] - cs[p]) for p in pairs]
    bh = [b_in[p] * e_end[p] for p in pairs]
    p_end = [_split_bf16(jnp.where(eye, jnp.exp(tot[p]), 0.0)) for p in pairs]
    decay = [jnp.dot(p_end[p][0], same_head, preferred_element_type=F32)
             + jnp.dot(p_end[p][1], same_head, preferred_element_type=F32) for p in pairs]
    corr = [ptp(bh[p], a_hat[p]) for p in pairs]
    gam = [ptp(jnp.concatenate([bh[p], kd_all[:, sl[p]] * e_end[p]], axis=0),
               jnp.concatenate([u_loc[p], v[p]], axis=0)) for p in pairs]
    for p in pairs:
        h_ref[:, sl[p]] = decay[p] * h0[p] + (pp(corr[p], h0[p], split=True) + gam[p])

    @pl.when(c == n_chunks - 1)
    def _():
        hfin_ref[0] = h_ref[...]


def rwkv_direction(rev, main, w_pre, a_pre, w0, a0, k_k, k_a, s0):
    bsz, L, _ = main.shape
    W = RWKV_W
    T, N, H = RWKV_CHUNK, RWKV_HEAD, RWKV_HEADS
    n = L // T
    gw = RWKV_CPAIRS * LANES
    ng = W // gw
    h0 = s0.transpose(0, 3, 1, 2).reshape(bsz, N, W)

    def seq(col0):
        return pl.BlockSpec((1, T, gw), lambda b, g, c: (b, (n - 1 - c) if rev else c, col0 * ng + g))
    vec = pl.BlockSpec((1, gw), lambda b, g, c: (0, g))
    st = pl.BlockSpec((1, N, gw), lambda b, g, c: (b, 0, g))
    y, hfin = pl.pallas_call(
        functools.partial(_rwkv_fs_kernel, n, rev),
        grid=(bsz, ng, n),
        in_specs=[seq(0), seq(1), seq(2), seq(0), seq(0), vec, vec, vec, vec, st],
        out_specs=[seq(0), st],
        out_shape=[jax.ShapeDtypeStruct((bsz, L, W), BF16), jax.ShapeDtypeStruct((bsz, N, W), F32)],
        scratch_shapes=[pltpu.VMEM((N, gw), F32)],
        compiler_params=pltpu.CompilerParams(
            dimension_semantics=("arbitrary",) * 3, vmem_limit_bytes=VMEM_LIMIT),
        name="rwkv_bwd_chunks" if rev else "rwkv_fwd_chunks",
    )(main, main, main, w_pre, a_pre, w0.reshape(1, W), a0.reshape(1, W), k_k.reshape(1, W), k_a.reshape(1, W), h0)
    return y, hfin.reshape(bsz, N, H, N).transpose(0, 2, 3, 1)


def _segsum(x, same_head):
    x1, x2 = _split_bf16(x)
    return (jnp.dot(x1, same_head, preferred_element_type=F32)
            + jnp.dot(x2, same_head, preferred_element_type=F32))


def _rwkv_post_kernel(yf_ref, yb_ref, r_ref, k_ref, v_ref, g_ref, af_ref, ab_ref, a0_ref, ka_ref, rk_ref,
                      lw_ref, lb_ref, o_ref):
    N = RWKV_HEAD
    same_head = ((lax.broadcasted_iota(jnp.int32, (LANES, LANES), 0) < N)
                 == (lax.broadcasted_iota(jnp.int32, (LANES, LANES), 1) < N)).astype(BF16)
    for t in range(o_ref.shape[2] // LANES):
        ls = slice(t * LANES, (t + 1) * LANES)
        wkv = yf_ref[0, :, ls].astype(F32) + yb_ref[0, :, ls].astype(F32)
        mean = _segsum(wkv, same_head) * (1.0 / N)
        cen = wkv - mean
        var = _segsum(cen * cen, same_head) * (1.0 / N)
        ln = cen * lax.rsqrt(var + RWKV_LNX_EPS) * lw_ref[:, ls] + lb_ref[:, ls]
        ka = ka_ref[:, ls]
        k_mix = ((1.0 + (jax.nn.sigmoid(af_ref[0, :, ls] + a0_ref[0:1, ls]) - 1.0) * ka)
                 + (1.0 + (jax.nn.sigmoid(ab_ref[0, :, ls] + a0_ref[1:2, ls]) - 1.0) * ka))
        rk = r_ref[0, :, ls].astype(F32) * k_ref[0, :, ls].astype(F32)
        bonus = _segsum(rk * k_mix * rk_ref[:, ls], same_head) * v_ref[0, :, ls].astype(F32)
        gate = g_ref[0, :, ls].astype(F32)
        o_ref[0, :, ls] = ((ln + bonus) * (gate * jax.nn.sigmoid(gate))).astype(o_ref.dtype)


def rwkv_post(y_f, y_b, main, a_pre_f, a_pre_b, a0, k_a, r_k, lnx_w, lnx_b):
    bsz, L, W = y_f.shape
    tr = _pick(L, (256, 128, 64))
    tw = 1024
    nw = W // tw

    def seq(col0):
        return pl.BlockSpec((1, tr, tw), lambda b, i, j: (b, i, col0 * nw + j))
    vec = pl.BlockSpec((1, tw), lambda b, i, j: (0, j))
    vec2 = pl.BlockSpec((N_DIR, tw), lambda b, i, j: (0, j))
    return pl.pallas_call(
        _rwkv_post_kernel,
        grid=(bsz, L // tr, nw),
        in_specs=[seq(0), seq(0), seq(0), seq(1), seq(2), seq(3), seq(0), seq(0), vec2, vec, vec, vec, vec],
        out_specs=seq(0),
        out_shape=jax.ShapeDtypeStruct((bsz, L, W), BF16),
        compiler_params=pltpu.CompilerParams(
            dimension_semantics=("arbitrary",) * 3, vmem_limit_bytes=VMEM_LIMIT),
        name="rwkv_post",
    )(y_f, y_b, main, main, main, main, a_pre_f, a_pre_b, a0, k_a.reshape(1, W), r_k.reshape(1, W),
      lnx_w.reshape(1, W), lnx_b.reshape(1, W))


def _split_cols(t, sizes):
    offsets, acc = [], 0
    for s in sizes[:-1]:
        acc += s
        offsets.append(acc)
    return jnp.split(t, offsets, axis=-1)


def _adaln_kernel(c_ref, w_ref, b_ref, o_ref):
    cond = c_ref[...]
    act = cond * jax.nn.sigmoid(cond)
    o_ref[...] = jnp.dot(act, w_ref[0], precision=HI, preferred_element_type=F32) + b_ref[0]


def adaln(cond, w, b, layer):
    rows, dm = cond.shape
    n = w.shape[2]
    rp = -(-rows // 8) * 8
    tn = 512
    m = pl.pallas_call(
        _adaln_kernel,
        grid=(n // tn,),
        in_specs=[pl.BlockSpec((rp, dm), lambda j: (0, 0)),
                  pl.BlockSpec((1, dm, tn), lambda j: (layer, 0, j)),
                  pl.BlockSpec((1, 1, tn), lambda j: (layer, 0, j))],
        out_specs=pl.BlockSpec((rp, tn), lambda j: (0, j)),
        out_shape=jax.ShapeDtypeStruct((rp, n), F32),
        compiler_params=pltpu.CompilerParams(dimension_semantics=("arbitrary",), vmem_limit_bytes=VMEM_LIMIT),
        name="adaln",
    )(jnp.pad(cond, ((0, rp - rows), (0, 0))), w, b.reshape(b.shape[0], 1, n))[:rows]
    return jnp.split(m, 3, axis=-1)


def _grid_pos_embed(n_tokens):
    rows = n_tokens // GRID_W
    row_id = jnp.broadcast_to(jnp.arange(rows, dtype=F32)[:, None], (rows, GRID_W)).reshape(-1)
    col_id = jnp.broadcast_to(jnp.arange(GRID_W, dtype=F32)[None, :], (rows, GRID_W)).reshape(-1)
    quarter = D_MODEL // 4
    omega = 1.0 / (POS_BASE ** (jnp.arange(quarter, dtype=F32) / quarter))

    def axis_emb(pos):
        ang = pos[:, None] * omega[None, :]
        return jnp.concatenate([jnp.sin(ang), jnp.cos(ang)], axis=-1)
    return jnp.concatenate([axis_emb(row_id), axis_emb(col_id)], axis=-1)


def _even_mixer(x, x_add, gate, h, s5_re0, s5_im0, gla0, w_in, w_out, s5_ops, glu_w, glu_b, dec_up, dec_b,
                gla_nw):
    bsz, L, _ = h.shape
    n_main = sum(EVEN_SIZES[:-1])
    main = _mm3(h, w_in, n_main, BF16)
    w_tail = jnp.pad(w_in[:, n_main:], ((0, 0), (0, LANES - N_DIR * GLA_RANK)))
    dec_lr = _mm3(h, w_tail)
    gy, fin_re, fin_im = s5_scan(main[..., :S5_W], s5_ops, s5_re0, s5_im0)
    gy = gy.reshape(bsz * L, S5_W)
    mixed = matmul_glu(gy, glu_w, glu_b, main.reshape(bsz * L, n_main), S5_W, S5_W + GLA_DV_W)
    mixed, fin_gla = gla_mix(main, dec_lr, dec_up, dec_b, gla_nw, gla0, mixed.reshape(bsz, L, -1))
    return matmul_gated_residual(mixed, w_out, x, gate, x_add), fin_re, fin_im, fin_gla


def _odd_mixer(x, gate, xs, rwkv0, w_in, w_out, w0, w2, a0, a2, k_k, k_a, r_k, lnx_w, lnx_b):
    bsz, L, _ = xs.shape
    n_main = sum(ODD_SIZES[:4])
    main = _mm3(xs, w_in, n_main, BF16)
    tail = _mm3(xs, w_in[:, n_main:])
    w_lr, a_lr = _split_cols(tail, ODD_SIZES[4:])
    w_lr = jnp.tanh(w_lr).reshape(bsz, L, N_DIR, RWKV_DECAY_RANK)
    a_lr = a_lr.reshape(bsz, L, N_DIR, RWKV_ICLR_RANK)
    ys, a_pres, finals = [], [], []
    for d in range(N_DIR):
        w_pre = _mm3(w_lr[:, :, d], w2[d])
        a_pre = _mm3(a_lr[:, :, d], a2[d])
        y_d, fin = rwkv_direction(bool(d), main, w_pre, a_pre, w0[d], a0[d], k_k, k_a, rwkv0[:, d])
        ys.append(y_d)
        a_pres.append(a_pre)
        finals.append(fin)
    out = rwkv_post(ys[0], ys[1], main, a_pres[0], a_pres[1], a0, k_a, r_k.reshape(-1), lnx_w, lnx_b)
    return matmul_gated_residual(out, w_out, x, gate), jnp.stack(finals, axis=1)


def kernel(x_prompt, x_sample, state_s5_re, state_s5_im, state_gla, state_rwkv, c, c_ctx, norm_w, ada_w, ada_b, final_norm_w, e_w_in, e_w_out, s5_lambda_re, s5_lambda_im, s5_log_step, s5_b_re, s5_b_im, s5_c_re, s5_c_im, s5_d, s5_glu_w, s5_glu_b, gla_decay_up, gla_decay_b, gla_norm_w, o_w_in, o_w_out, rwkv_mu, rwkv_w0, rwkv_w2, rwkv_a0, rwkv_a2, rwkv_k_k, rwkv_k_a, rwkv_r_k, rwkv_lnx_w, rwkv_lnx_b):
    bp = x_prompt.shape[0]
    depth = norm_w.shape[0]
    x_ctx = x_prompt
    x_lat, lat_add = x_sample, _grid_pos_embed(x_sample.shape[1])
    z_s5 = jnp.zeros((bp, N_DIR, S5_GROUPS, S5_STATE), F32)
    z_rwkv = jnp.zeros((bp, N_DIR, RWKV_HEADS, RWKV_HEAD, RWKV_HEAD), F32)
    new_s5_re, new_s5_im, new_gla, new_rwkv = [], [], [], []
    n_lat = c.shape[0]
    cond = jnp.concatenate([c, c_ctx[None]], axis=0)
    for i in range(depth):
        j = i // 2
        shift, scale, gate = adaln(cond, ada_w, ada_b, i)
        gt_l, gt_c = gate[:n_lat], jnp.broadcast_to(gate[n_lat:], (bp, D_MODEL))
        mu = rwkv_mu[j] if i % 2 else None
        h_ctx = norm_mod(x_ctx, norm_w[i], scale[n_lat:], shift[n_lat:], mu=mu)
        h_lat = norm_mod(x_lat, norm_w[i], scale[:n_lat], shift[:n_lat], add=lat_add, mu=mu)
        if i % 2 == 0:
            s5_ops = s5_operators(s5_lambda_re[j], s5_lambda_im[j], s5_log_step[j], s5_b_re[j], s5_b_im[j],
                                  s5_c_re[j], s5_c_im[j], s5_d[j])
            p = (e_w_in[j], e_w_out[j], s5_ops, s5_glu_w[j], s5_glu_b[j], gla_decay_up[j], gla_decay_b[j],
                 gla_norm_w[j])
            x_ctx, fr, fi, fg = _even_mixer(x_ctx, None, gt_c, h_ctx, z_s5, z_s5, None, *p)
            x_lat, _, _, _ = _even_mixer(x_lat, lat_add, gt_l, h_lat, state_s5_re[:, j], state_s5_im[:, j],
                                         state_gla[:, j], *p)
            lat_add = None
            new_s5_re.append(fr)
            new_s5_im.append(fi)
            new_gla.append(fg)
        else:
            p = (o_w_in[j], o_w_out[j], rwkv_w0[j], rwkv_w2[j], rwkv_a0[j], rwkv_a2[j],
                 rwkv_k_k[j], rwkv_k_a[j], rwkv_r_k[j], rwkv_lnx_w[j], rwkv_lnx_b[j])
            x_ctx, fw = _odd_mixer(x_ctx, gt_c, h_ctx, z_rwkv, *p)
            x_lat, _ = _odd_mixer(x_lat, gt_l, h_lat, state_rwkv[:, j], *p)
            new_rwkv.append(fw)
    if lat_add is not None:
        x_lat = x_lat + lat_add
    y_prompt = final_norm(x_ctx, final_norm_w)
    y_sample = final_norm(x_lat, final_norm_w)
    return (y_prompt, y_sample, jnp.stack(new_s5_re, axis=1), jnp.stack(new_s5_im, axis=1),
            jnp.stack(new_gla, axis=1), jnp.stack(new_rwkv, axis=1))
```

```python
import functools

import jax
import jax.numpy as jnp
from jax import lax
from jax.experimental import pallas as pl
from jax.experimental.pallas import tpu as pltpu

D_MODEL = 2048
GRID_W = 64
POS_BASE = 10000.0
N_DIR = 2
EPS = 1e-6
S5_W = 1024
S5_GROUP_CH = 16
S5_GROUPS = 64
S5_STATE = 64
S5_CHUNK = 16
S5_TILE_GROUPS = 8
GLA_HEADS = 6
GLA_DV = 512
GLA_DK = 256
GLA_DK_W = 1536
GLA_DV_W = 3072
GLA_RANK = 16
GLA_NORMALIZER = 16.0
GLA_CHUNK = 64
GLA_NC = 16
GLA_LOG_DECAY_MIN = -1.0
EVEN_SIZES = (S5_W, S5_W, GLA_DK_W, GLA_DK_W, GLA_DV_W, GLA_DV_W, N_DIR * GLA_RANK)
RWKV_W = 2048
RWKV_HEAD = 64
RWKV_HEADS = 32
RWKV_DECAY_RANK = 96
RWKV_ICLR_RANK = 96
RWKV_LNX_EPS = 64e-5
ODD_SIZES = (RWKV_W, RWKV_W, RWKV_W, RWKV_W, N_DIR * RWKV_DECAY_RANK, N_DIR * RWKV_ICLR_RANK)
RWKV_CHUNK = 64
RWKV_CPAIRS = 16
RWKV_SUB = 16
LANES = 128

VMEM_LIMIT = 48 * 1024 * 1024
HI = lax.Precision.HIGHEST
BF16 = jnp.bfloat16
F32 = jnp.float32


def _mm_kernel(x_ref, w_ref, o_ref):
    o_ref[...] = jnp.dot(x_ref[...], w_ref[...], preferred_element_type=F32).astype(o_ref.dtype)


def _pick(n, prefs):
    for p in prefs:
        if n % p == 0:
            return p
    return n


def matmul(x, w, n_cols=None, out_dtype=F32):
    m, k = x.shape
    n = w.shape[1] if n_cols is None else n_cols
    x = x.astype(BF16)
    w = w.astype(BF16)
    tm = _pick(m, (1024, 512, 256, 128, 64, 32, 16, 8))
    tn = _pick(n, (1024, 512, 384, 256, 128))
    return pl.pallas_call(
        _mm_kernel,
        grid=(m // tm, n // tn),
        in_specs=[pl.BlockSpec((tm, k), lambda i, j: (i, 0)),
                  pl.BlockSpec((k, tn), lambda i, j: (0, j))],
        out_specs=pl.BlockSpec((tm, tn), lambda i, j: (i, j)),
        out_shape=jax.ShapeDtypeStruct((m, n), out_dtype),
        compiler_params=pltpu.CompilerParams(
            dimension_semantics=("arbitrary", "arbitrary"), vmem_limit_bytes=VMEM_LIMIT),
        name="proj_matmul",
    )(x, w)


def _mm3(h, w, n_cols=None, out_dtype=F32):
    b, l, k = h.shape
    return matmul(h.reshape(b * l, k), w, n_cols, out_dtype).reshape(b, l, -1)


def _mm_residual_kernel(has_add, has_norm, x_ref, w_ref, res_ref, gate_ref, *refs):
    o_ref = refs[-1]
    acc = jnp.dot(x_ref[...], w_ref[...], preferred_element_type=F32)
    res = res_ref[...] + refs[0][...] if has_add else res_ref[...]
    out = res + gate_ref[0] * acc
    if has_norm:
        nw_ref = refs[-2]
        out = out * lax.rsqrt(jnp.mean(out * out, axis=-1, keepdims=True) + EPS) * nw_ref[...]
    o_ref[...] = out


def matmul_gated_residual(x, w, res, gate, res_add=None, final_nw=None):
    bsz, L, k = x.shape
    n = w.shape[1]
    m = bsz * L
    if final_nw is None:
        tm = _pick(L, (1024, 512, 256, 128))
        tn = _pick(n, (1024, 512, 256, 128) if k <= 2048 else (512, 256, 128))
    else:
        tm, tn = _pick(L, (512, 256, 128)), n
    per_b = L // tm
    in_specs = [pl.BlockSpec((tm, k), lambda i, j: (i, 0)),
                pl.BlockSpec((k, tn), lambda i, j: (0, j)),
                pl.BlockSpec((tm, tn), lambda i, j: (i, j)),
                pl.BlockSpec((1, 1, tn), lambda i, j: (i // per_b, 0, j))]
    args = [x.reshape(m, k).astype(BF16), w.astype(BF16), res.reshape(m, n), gate.reshape(bsz, 1, n)]
    if res_add is not None:
        in_specs.append(pl.BlockSpec((tm, tn), lambda i, j: (i % per_b, j)))
        args.append(res_add)
    if final_nw is not None:
        in_specs.append(pl.BlockSpec((1, tn), lambda i, j: (0, j)))
        args.append(final_nw.reshape(1, n))
    out = pl.pallas_call(
        functools.partial(_mm_residual_kernel, res_add is not None, final_nw is not None),
        grid=(m // tm, n // tn),
        in_specs=in_specs,
        out_specs=pl.BlockSpec((tm, tn), lambda i, j: (i, j)),
        out_shape=jax.ShapeDtypeStruct((m, n), F32),
        compiler_params=pltpu.CompilerParams(
            dimension_semantics=("arbitrary", "arbitrary"), vmem_limit_bytes=VMEM_LIMIT),
        name="proj_residual",
    )(*args)
    return out.reshape(bsz, L, n)


def _mm_glu_kernel(x_ref, w_ref, b_ref, g_ref, xt_ref, o_ref):
    acc = jnp.dot(x_ref[...], w_ref[...], preferred_element_type=F32) + b_ref[...]
    gy = xt_ref[...].astype(F32)
    gate = g_ref[...].astype(F32)
    o_ref[...] = (gy * jax.nn.sigmoid(acc) * (gate * jax.nn.sigmoid(gate))).astype(o_ref.dtype)


def matmul_glu(gy, w, b, main, g_col0, n_total):
    m, k = gy.shape
    n = w.shape[1]
    tm = _pick(m, (1024, 512, 256, 128))
    tn = 512
    return pl.pallas_call(
        _mm_glu_kernel,
        grid=(m // tm, n // tn),
        in_specs=[pl.BlockSpec((tm, k), lambda i, j: (i, 0)),
                  pl.BlockSpec((k, tn), lambda i, j: (0, j)),
                  pl.BlockSpec((1, tn), lambda i, j: (0, j)),
                  pl.BlockSpec((tm, tn), lambda i, j: (i, g_col0 // tn + j)),
                  pl.BlockSpec((tm, tn), lambda i, j: (i, j))],
        out_specs=pl.BlockSpec((tm, tn), lambda i, j: (i, j)),
        out_shape=jax.ShapeDtypeStruct((m, n_total), BF16),
        compiler_params=pltpu.CompilerParams(
            dimension_semantics=("arbitrary", "arbitrary"), vmem_limit_bytes=VMEM_LIMIT),
        name="s5_glu_gate",
    )(gy, w.astype(BF16), b.reshape(1, n), main, gy)


def _norm_mod_kernel(has_add, has_shift, n_row_blocks, x_ref, nw_ref, sc_ref, sh_ref, *refs):
    o_ref = refs[-1]

    def modulated(x):
        inv = lax.rsqrt(jnp.mean(x * x, axis=-1, keepdims=True) + EPS)
        return x * inv * nw_ref[...] * (1.0 + sc_ref[0]) + sh_ref[0]
    x = x_ref[0]
    if has_add:
        x = x + refs[0][...]
    h = modulated(x)
    if has_shift:
        prev_ref, next_ref, mu_ref = refs[0], refs[1], refs[2]
        i = pl.program_id(1)
        tr = h.shape[0]
        row = lax.broadcasted_iota(jnp.int32, h.shape, 0)
        before = jnp.where(i > 0, modulated(prev_ref[0])[7:8], 0.0)
        after = jnp.where(i < n_row_blocks - 1, modulated(next_ref[0])[0:1], 0.0)
        h_prev = jnp.where(row == 0, before, pltpu.roll(h, 1, 0))
        h_next = jnp.where(row == tr - 1, after, pltpu.roll(h, tr - 1, 0))
        h = h + mu_ref[0:1] * (h_prev - h) + mu_ref[1:2] * (h_next - h)
    o_ref[0] = h.astype(o_ref.dtype)


def norm_mod(x, nw, scale, shift, add=None, mu=None):
    assert add is None or mu is None
    bsz, L, dm = x.shape
    tr = _pick(L, (256, 128, 64))
    nb = scale.shape[0]
    nblk = L // tr
    cond = pl.BlockSpec((1, 1, dm), lambda b, i: (b if nb > 1 else 0, 0, 0))
    in_specs = [pl.BlockSpec((1, tr, dm), lambda b, i: (b, i, 0)), pl.BlockSpec((1, dm), lambda b, i: (0, 0)),
                cond, cond]
    args = [x, nw.reshape(1, dm), scale.reshape(nb, 1, dm), shift.reshape(nb, 1, dm)]
    if add is not None:
        in_specs.append(pl.BlockSpec((tr, dm), lambda b, i: (i, 0)))
        args.append(add)
    if mu is not None:
        r8 = tr // 8
        in_specs += [pl.BlockSpec((1, 8, dm), lambda b, i: (b, jnp.maximum(i * r8 - 1, 0), 0)),
                     pl.BlockSpec((1, 8, dm), lambda b, i: (b, jnp.minimum((i + 1) * r8, L // 8 - 1), 0)),
                     pl.BlockSpec((2, dm), lambda b, i: (0, 0))]
        args += [x, x, mu]
    return pl.pallas_call(
        functools.partial(_norm_mod_kernel, add is not None, mu is not None, nblk),
        grid=(bsz, nblk),
        in_specs=in_specs,
        out_specs=pl.BlockSpec((1, tr, dm), lambda b, i: (b, i, 0)),
        out_shape=jax.ShapeDtypeStruct((bsz, L, dm), BF16),
        compiler_params=pltpu.CompilerParams(
            dimension_semantics=("arbitrary", "arbitrary"), vmem_limit_bytes=VMEM_LIMIT),
        name="norm_mod",
    )(*args)


def _final_norm_kernel(x_ref, nw_ref, o_ref):
    x = x_ref[0]
    o_ref[0] = x * lax.rsqrt(jnp.mean(x * x, axis=-1, keepdims=True) + EPS) * nw_ref[...]


def final_norm(x, nw):
    bsz, L, dm = x.shape
    tr = _pick(L, (256, 128, 64))
    return pl.pallas_call(
        _final_norm_kernel,
        grid=(bsz, L // tr),
        in_specs=[pl.BlockSpec((1, tr, dm), lambda b, i: (b, i, 0)), pl.BlockSpec((1, dm), lambda b, i: (0, 0))],
        out_specs=pl.BlockSpec((1, tr, dm), lambda b, i: (b, i, 0)),
        out_shape=jax.ShapeDtypeStruct((bsz, L, dm), F32),
        compiler_params=pltpu.CompilerParams(
            dimension_semantics=("arbitrary", "arbitrary"), vmem_limit_bytes=VMEM_LIMIT),
        name="final_norm",
    )(x, nw.reshape(1, dm))


def s5_operators(lam_re, lam_im, log_step, b_re, b_im, c_re, c_im, d_skip):
    T = S5_CHUNK
    dt = jnp.exp(log_step)[..., None]
    mag = jnp.exp(lam_re * dt)
    ab_re, ab_im = mag * jnp.cos(lam_im * dt), mag * jnp.sin(lam_im * dt)
    den = lam_re * lam_re + lam_im * lam_im
    f_re = ((ab_re - 1.0) * lam_re + ab_im * lam_im) / den
    f_im = (ab_im * lam_re - (ab_re - 1.0) * lam_im) / den
    bb_re = f_re[..., None] * b_re - f_im[..., None] * b_im
    bb_im = f_re[..., None] * b_im + f_im[..., None] * b_re
    kk = jnp.arange(T + 1, dtype=F32)[:, None, None, None]
    pmag = jnp.exp(kk * (lam_re * dt))
    pr = pmag * jnp.cos(kk * (lam_im * dt))
    pi = pmag * jnp.sin(kk * (lam_im * dt))
    zr = pr[:T, :, :, :, None] * bb_re - pi[:T, :, :, :, None] * bb_im
    zi = pr[:T, :, :, :, None] * bb_im + pi[:T, :, :, :, None] * bb_re
    kern = (jnp.einsum('dghp,kdgpj->kdghj', c_re, zr, precision=HI)
            - jnp.einsum('dghp,kdgpj->kdghj', c_im, zi, precision=HI))
    t_idx = jnp.arange(T)[:, None]
    s_idx = jnp.arange(T)[None, :]
    lag_f = t_idx - s_idx
    lag_b = s_idx - t_idx
    m_f = jnp.where((lag_f >= 0)[:, :, None, None, None], kern[:, 0][jnp.clip(lag_f, 0, T - 1)], 0.0)
    m_b = jnp.where((lag_b >= 0)[:, :, None, None, None], kern[:, 1][jnp.clip(lag_b, 0, T - 1)], 0.0)
    m = m_f + m_b
    eye_t = jnp.eye(T, dtype=F32)[:, :, None, None, None]
    eye_h = jnp.eye(S5_GROUP_CH, dtype=F32)[None, None, None]
    m = m + eye_t * eye_h * d_skip.reshape(S5_GROUPS, S5_GROUP_CH)[None, None, :, :, None]
    g = m.shape[2]
    m_t = m.transpose(2, 1, 4, 0, 3).reshape(g, T * S5_GROUP_CH, T * S5_GROUP_CH)
    pf_r, pf_i = pr[T - 1::-1][:T, 0], pi[T - 1::-1][:T, 0]
    pb_r, pb_i = pr[:T, 1], pi[:T, 1]

    def f_mat(p_r, p_i, d):
        re = p_r[..., None] * bb_re[d][None] - p_i[..., None] * bb_im[d][None]
        im = p_r[..., None] * bb_im[d][None] + p_i[..., None] * bb_re[d][None]
        re = re.transpose(1, 0, 3, 2).reshape(g, T * S5_GROUP_CH, S5_STATE)
        im = im.transpose(1, 0, 3, 2).reshape(g, T * S5_GROUP_CH, S5_STATE)
        return re, im
    ff_re, ff_im = f_mat(pf_r, pf_i, 0)
    fb_re, fb_im = f_mat(pb_r, pb_i, 1)
    a_t = jnp.concatenate([m_t, ff_re, fb_re, ff_im, fb_im], axis=-1)
    ef_r, ef_i = pr[1:T + 1, 0], pi[1:T + 1, 0]
    eb_r, eb_i = pr[T:0:-1, 1], pi[T:0:-1, 1]

    def e_mat(p_r, p_i, d):
        er = c_re[d][None] * p_r[:, :, None, :] - c_im[d][None] * p_i[:, :, None, :]
        ei = -(c_re[d][None] * p_i[:, :, None, :] + c_im[d][None] * p_r[:, :, None, :])
        er = er.transpose(1, 3, 0, 2).reshape(g, S5_STATE, T * S5_GROUP_CH)
        ei = ei.transpose(1, 3, 0, 2).reshape(g, S5_STATE, T * S5_GROUP_CH)
        return er, ei
    efr, efi = e_mat(ef_r, ef_i, 0)
    ebr, ebi = e_mat(eb_r, eb_i, 1)
    e_t = jnp.concatenate([efr, ebr, efi, ebi], axis=1)
    lam_t = jnp.concatenate([pr[T, 0], pr[T, 1], pi[T, 0], pi[T, 1]], axis=-1)[:, None, :]
    return a_t.astype(BF16), e_t.astype(BF16), lam_t


def _s5_kernel(n_steps, bsz, pair, x8_ref, sel_ref, at_ref, et_ref, lam_ref, h0_ref, y_ref, hfin_ref, z_ref,
               hent_ref):
    P = S5_STATE
    ut = jnp.dot(x8_ref[0], sel_ref[0], preferred_element_type=F32).astype(BF16)
    z_ref[...] = jnp.dot(ut, at_ref[0], preferred_element_type=F32)
    lam = lam_ref[0]
    a_re, a_im = lam[:, 0:2 * P], lam[:, 2 * P:4 * P]
    h0 = h0_ref[0]
    h0_re, h0_im = h0[:, 0:2 * P], h0[:, 2 * P:4 * P]
    fwd_lanes = lax.broadcasted_iota(jnp.int32, (bsz, 2 * P), 1) < P
    m_re, m_im = a_re, a_im
    if pair:
        cols, half = z_ref.shape[0], bsz // 2
        g_re, g_im = z_ref[:, 4 * P:6 * P], z_ref[:, 6 * P:8 * P]
        rows = lax.broadcasted_iota(jnp.int32, (cols, 2 * P), 0)
        fwd_all = lax.broadcasted_iota(jnp.int32, (cols, 2 * P), 1) < P

        def neighbour(g):
            up = jnp.where(rows < cols - half, pltpu.roll(g, cols - half, 0), 0.0)
            down = jnp.where(rows >= half, pltpu.roll(g, half, 0), 0.0)
            return jnp.where(fwd_all, up, down)
        z_ref[:, 4 * P:6 * P] = a_re * g_re - a_im * g_im + neighbour(g_re)
        z_ref[:, 6 * P:8 * P] = a_re * g_im + a_im * g_re + neighbour(g_im)
        lo = lax.broadcasted_iota(jnp.int32, (bsz, 2 * P), 0) < half
        ah_re = a_re * h0_re - a_im * h0_im
        ah_im = a_re * h0_im + a_im * h0_re

        def swap(x):
            return pltpu.roll(x, half, 0)
        f_re = h0_re + swap(ah_re + jnp.where(lo, g_re[0:bsz], 0.0))
        f_im = h0_im + swap(ah_im + jnp.where(lo, g_im[0:bsz], 0.0))
        b_re = swap(h0_re) + ah_re + swap(jnp.where(lo, 0.0, g_re[cols - bsz:cols]))
        b_im = swap(h0_im) + ah_im + swap(jnp.where(lo, 0.0, g_im[cols - bsz:cols]))
        h0_re, h0_im = jnp.where(fwd_lanes, f_re, b_re), jnp.where(fwd_lanes, f_im, b_im)
        m_re, m_im = a_re * a_re - a_im * a_im, 2.0 * a_re * a_im

    def step(c, carry):
        h_re, h_im = carry
        rf = pl.ds(pl.multiple_of(c * bsz, 8), bsz)
        rb = pl.ds(pl.multiple_of((n_steps - 1 - c) * bsz, 8), bsz)
        hent_ref[rf, 0:P] = h_re[:, 0:P]
        hent_ref[rb, P:2 * P] = h_re[:, P:2 * P]
        hent_ref[rf, 2 * P:3 * P] = h_im[:, 0:P]
        hent_ref[rb, 3 * P:4 * P] = h_im[:, P:2 * P]
        g_re = jnp.where(fwd_lanes, z_ref[rf, 4 * P:6 * P], z_ref[rb, 4 * P:6 * P])
        g_im = jnp.where(fwd_lanes, z_ref[rf, 6 * P:8 * P], z_ref[rb, 6 * P:8 * P])
        return m_re * h_re - m_im * h_im + g_re, m_re * h_im + m_im * h_re + g_im
    h_re, h_im = lax.fori_loop(0, n_steps, step, (h0_re, h0_im))
    if pair:
        h_re = jnp.where(fwd_lanes, h_re, pltpu.roll(h_re, bsz // 2, 0))
        h_im = jnp.where(fwd_lanes, h_im, pltpu.roll(h_im, bsz // 2, 0))
    hfin_ref[0, :, 0:2 * P] = h_re
    hfin_ref[0, :, 2 * P:4 * P] = h_im
    y = z_ref[:, 0:4 * P] + jnp.dot(hent_ref[...].astype(BF16), et_ref[0], preferred_element_type=F32)
    y_ref[0] = jax.nn.gelu(y).astype(y_ref.dtype)


def _s5_unpack_kernel(yt_ref, selt_ref, o_ref):
    acc = jnp.dot(yt_ref[0], selt_ref[0], preferred_element_type=F32)
    for gl in range(1, S5_TILE_GROUPS):
        acc = acc + jnp.dot(yt_ref[gl], selt_ref[gl], preferred_element_type=F32)
    o_ref[0] = acc.astype(o_ref.dtype)


def s5_scan(u, ops, h0_re, h0_im):
    a_t, e_t, lam_t = ops
    b_real, L, _ = u.shape
    T, G, H, P = S5_CHUNK, S5_GROUPS, S5_GROUP_CH, S5_STATE
    TG = S5_TILE_GROUPS
    n = L // T
    pair = b_real == 4 and n % 2 == 0
    bsz = -(-b_real // 8) * 8
    rpc = b_real if pair else bsz
    n_steps = n // 2 if pair else n
    cols = n * rpc
    x8 = u.reshape(b_real, n, T, G // TG, LANES).transpose(3, 1, 0, 2, 4).astype(BF16)
    x8 = jnp.pad(x8, ((0, 0), (0, 0), (0, rpc - b_real), (0, 0), (0, 0))).reshape(G // TG, cols, T * LANES)
    src = jnp.arange(T * LANES)
    dst = jnp.arange(T * H)
    sel = ((src[None, :, None] // LANES == dst[None, None, :] // H)
           & (src[None, :, None] % H == dst[None, None, :] % H)
           & ((src[None, :, None] % LANES) // H == jnp.arange(TG)[:, None, None])).astype(BF16)
    h0 = jnp.concatenate([h0_re[:, 0], h0_re[:, 1], h0_im[:, 0], h0_im[:, 1]], axis=-1)
    h0 = jnp.pad(h0.transpose(1, 0, 2), ((0, 0), (0, bsz - b_real), (0, 0)))
    yt, hfin = pl.pallas_call(
        functools.partial(_s5_kernel, n_steps, bsz, pair),
        grid=(G,),
        in_specs=[pl.BlockSpec((1, cols, T * LANES), lambda g: (g // TG, 0, 0)),
                  pl.BlockSpec((1, T * LANES, T * H), lambda g: (g % TG, 0, 0)),
                  pl.BlockSpec((1, T * H, 8 * P), lambda g: (g, 0, 0)),
                  pl.BlockSpec((1, 4 * P, T * H), lambda g: (g, 0, 0)),
                  pl.BlockSpec((1, 1, 4 * P), lambda g: (g, 0, 0)),
                  pl.BlockSpec((1, bsz, 4 * P), lambda g: (g, 0, 0))],
        out_specs=[pl.BlockSpec((1, cols, T * H), lambda g: (g, 0, 0)),
                   pl.BlockSpec((1, bsz, 4 * P), lambda g: (g, 0, 0))],
        out_shape=[jax.ShapeDtypeStruct((G, cols, T * H), BF16),
                   jax.ShapeDtypeStruct((G, bsz, 4 * P), F32)],
        scratch_shapes=[pltpu.VMEM((cols, 8 * P), F32), pltpu.VMEM((cols, 4 * P), F32)],
        compiler_params=pltpu.CompilerParams(dimension_semantics=("arbitrary",), vmem_limit_bytes=VMEM_LIMIT),
        name="s5_chunk_scan",
    )(x8, sel, a_t, e_t, lam_t, h0)
    tr = _pick(cols, (512, 256, 128))
    y8 = pl.pallas_call(
        _s5_unpack_kernel,
        grid=(G // TG, cols // tr),
        in_specs=[pl.BlockSpec((TG, tr, T * H), lambda t, i: (t, i, 0)),
                  pl.BlockSpec((TG, T * H, T * LANES), lambda t, i: (0, 0, 0))],
        out_specs=pl.BlockSpec((1, tr, T * LANES), lambda t, i: (t, i, 0)),
        out_shape=jax.ShapeDtypeStruct((G // TG, cols, T * LANES), BF16),
        compiler_params=pltpu.CompilerParams(
            dimension_semantics=("arbitrary", "arbitrary"), vmem_limit_bytes=VMEM_LIMIT),
        name="s5_unpack",
    )(yt, sel.transpose(0, 2, 1))
    y = y8.reshape(G // TG, n, rpc, T, LANES)[:, :, :b_real].transpose(2, 1, 3, 0, 4).reshape(b_real, L, G * H)
    hfin = hfin[:, :b_real].transpose(1, 0, 2)
    fin_re = jnp.stack([hfin[..., 0:P], hfin[..., P:2 * P]], axis=1)
    fin_im = jnp.stack([hfin[..., 2 * P:3 * P], hfin[..., 3 * P:4 * P]], axis=1)
    return y, fin_re, fin_im


def _dot_t(a, b):
    return lax.dot_general(a, b, (((1,), (1,)), ((), ())), preferred_element_type=F32)


def _dot_mask(mask_bf16, x, x_rows_to_sublanes=False):
    def d(b):
        if x_rows_to_sublanes:
            return lax.dot_general(b, mask_bf16, (((0,), (0,)), ((), ())), preferred_element_type=F32)
        return jnp.dot(mask_bf16, b, preferred_element_type=F32)
    x1 = x.astype(BF16)
    r1 = x - x1.astype(F32)
    x2 = r1.astype(BF16)
    x3 = (r1 - x2.astype(F32)).astype(BF16)
    return d(x1) + (d(x2) + d(x3))


def _gla_block_kernel(n_blocks, NC, has_s0, q_ref, k_ref, v_ref, g_ref, lr_ref, up_ref, db_ref, nw_ref, dst_ref,
                      *refs):
    s0_ref = refs[0] if has_s0 else None
    out_ref, sfin_ref, s_ref, of_ref, qd_ref, ov_ref, kv_ref, dc_ref = refs[1:] if has_s0 else refs
    C = GLA_CHUNK
    R = C * NC
    d = pl.program_id(2)
    c = pl.program_id(3)
    bidx = jnp.where(d == 0, c, n_blocks - 1 - c)

    @pl.when(c == 0)
    def _():
        s_ref[...] = s0_ref[0, 0, 0] if has_s0 else jnp.zeros_like(s_ref)

    z = _mxu(lr_ref[0], up_ref[0], split=True) + db_ref[0]
    gc = jnp.maximum(jax.nn.log_sigmoid(z) * (1.0 / GLA_NORMALIZER), GLA_LOG_DECAY_MIN)
    row_c = lax.broadcasted_iota(jnp.int32, (C, C), 0)
    col_c = lax.broadcasted_iota(jnp.int32, (C, C), 1)
    seen_c = jnp.where(d == 0, row_c - col_c, col_c - row_c) >= 0
    seen_bf = seen_c.astype(BF16)
    rs = [slice(i * C, (i + 1) * C) for i in range(NC)]
    bcum_c = [_dot_mask(seen_bf, gc[r]) for r in rs]
    btot_c = [jnp.broadcast_to(jnp.where(d == 0, b[C - 1:C], b[0:1]), (C, GLA_DK)) for b in bcum_c]
    bcum = jnp.concatenate(bcum_c, axis=0)
    btot = jnp.concatenate(btot_c, axis=0)
    q_dec = (q_ref[0].astype(F32) * (GLA_DK ** -0.5) * jnp.exp(bcum)).astype(BF16)
    k = k_ref[0].astype(F32)
    k_inv = (k * jnp.exp(-bcum)).astype(BF16)
    k_end = (k * jnp.exp(btot - bcum)).astype(BF16)
    v = v_ref[0].astype(BF16)
    ones_c = jnp.ones((C, LANES), BF16)
    qd_ref[...] = q_dec.reshape(NC, C, GLA_DK)
    att = [jnp.where(seen_c, _dot_t(q_dec[r], k_inv[r]), 0.0).astype(BF16) for r in rs]
    for i in range(NC):
        kv_ref[i] = lax.dot_general(k_end[rs[i]], v[rs[i]], (((0,), (0,)), ((), ())), preferred_element_type=F32)
    for i in range(NC):
        ov_ref[i] = jnp.dot(att[i], v[rs[i]], preferred_element_type=F32)
    for i in range(NC):
        dc_ref[i] = _dot_mask(ones_c, gc[rs[i]], x_rows_to_sublanes=True)

    for i in range(NC):
        ci = jnp.where(d == 0, i, NC - 1 - i)
        s_old = s_ref[...]
        ov_ref[ci] = ov_ref[ci] + jnp.dot(qd_ref[ci], s_old.astype(BF16), preferred_element_type=F32)
        s_ref[...] = jnp.exp(dc_ref[ci][:, 0:1]) * s_old + kv_ref[ci]
    rows = pl.ds(pl.multiple_of(bidx * R, R), R)

    @pl.when(d == 0)
    def _():
        of_ref[rows, :] = ov_ref[...].reshape(R, GLA_DV)

    @pl.when(d == 1)
    def _():
        tot = of_ref[rows, :] + ov_ref[...].reshape(R, GLA_DV)
        nrm = tot * lax.rsqrt(jnp.mean(tot * tot, axis=-1, keepdims=True) + EPS) * nw_ref[0]
        gate = g_ref[0].astype(F32)
        out_ref[0] = (nrm * (gate * jax.nn.sigmoid(gate))).astype(out_ref.dtype)

    @pl.when(c == n_blocks - 1)
    def _():
        sfin_ref[0, 0, 0] = s_ref[...]


def gla_mix(main, dec_lr, dec_up, dec_b, gla_nw, s0, dst):
    bsz, L, _ = main.shape
    H, DK, DV = GLA_HEADS, GLA_DK, GLA_DV
    nc = min(GLA_NC, L // GLA_CHUNK)
    C = GLA_CHUNK * nc
    n = L // C
    q_blk = sum(EVEN_SIZES[:2]) // DK
    k_blk = sum(EVEN_SIZES[:3]) // DK
    v_blk = sum(EVEN_SIZES[:4]) // DV
    g_blk = sum(EVEN_SIZES[:5]) // DV
    up = jnp.zeros((N_DIR, LANES, GLA_DK_W), F32)
    for d in range(N_DIR):
        up = up.at[d, d * GLA_RANK:(d + 1) * GLA_RANK].set(dec_up[d])
    db = dec_b.reshape(N_DIR, 1, GLA_DK_W)
    nw = gla_nw.reshape(1, GLA_DV_W)

    def chunk(d, c):
        return c + d * (n - 1 - 2 * c)

    def out_chunk(d, c):
        return (n - 1) - d * c
    state = pl.BlockSpec((1, 1, 1, DK, DV), lambda b, h, d, c: (b, d, h, 0, 0))
    has_s0 = s0 is not None
    out, sfin = pl.pallas_call(
        functools.partial(_gla_block_kernel, n, nc, has_s0),
        grid=(bsz, H, N_DIR, n),
        in_specs=[pl.BlockSpec((1, C, DK), lambda b, h, d, c: (b, chunk(d, c), q_blk + h)),
                  pl.BlockSpec((1, C, DK), lambda b, h, d, c: (b, chunk(d, c), k_blk + h)),
                  pl.BlockSpec((1, C, DV), lambda b, h, d, c: (b, chunk(d, c), v_blk + h)),
                  pl.BlockSpec((1, C, DV), lambda b, h, d, c: (b, chunk(d, c), g_blk + h)),
                  pl.BlockSpec((1, C, LANES), lambda b, h, d, c: (b, chunk(d, c), 0)),
                  pl.BlockSpec((1, LANES, DK), lambda b, h, d, c: (d, 0, h)),
                  pl.BlockSpec((1, 1, DK), lambda b, h, d, c: (d, 0, h)),
                  pl.BlockSpec((1, DV), lambda b, h, d, c: (0, h)),
                  pl.BlockSpec(memory_space=pl.ANY)] + ([state] if has_s0 else []),
        input_output_aliases={8: 0},
        out_specs=[pl.BlockSpec((1, C, DV), lambda b, h, d, c: (b, out_chunk(d, c), S5_W // DV + h)), state],
        out_shape=[jax.ShapeDtypeStruct(dst.shape, BF16),
                   jax.ShapeDtypeStruct((bsz, N_DIR, H, DK, DV), F32)],
        scratch_shapes=[pltpu.VMEM((DK, DV), F32), pltpu.VMEM((L, DV), F32),
                        pltpu.VMEM((nc, GLA_CHUNK, DK), BF16), pltpu.VMEM((nc, GLA_CHUNK, DV), F32),
                        pltpu.VMEM((nc, DK, DV), F32), pltpu.VMEM((nc, DK, LANES), F32)],
        compiler_params=pltpu.CompilerParams(
            dimension_semantics=("arbitrary",) * 4, vmem_limit_bytes=VMEM_LIMIT),
        name="gla_chunk_scan",
    )(main, main, main, main, dec_lr, up, db, nw, dst, *([s0] if has_s0 else []))
    return out, sfin


def _split_bf16(x):
    hi = x.astype(BF16)
    return hi, (x - hi.astype(F32)).astype(BF16)


def _mxu(x, y, dims=(((1,), (0,)), ((), ())), split=False):
    def d(a, b):
        return lax.dot_general(a, b, dims, preferred_element_type=F32)
    if not split:
        return d(x.astype(BF16), y.astype(BF16))
    xh, xl = _split_bf16(x)
    yh, yl = _split_bf16(y)
    return d(xh, yh) + (d(xh, yl) + d(xl, yh))


def _rwkv_fs_kernel(n_chunks, rev, r_ref, k_ref, v_ref, wp_ref, ap_ref, w0_ref, a0_ref, kk_ref, ka_ref, h0_ref,
                    y_ref, hfin_ref, h_ref):
    T, N, SB = RWKV_CHUNK, RWKV_HEAD, RWKV_SUB
    NB = T // SB
    c = pl.program_id(2)

    @pl.when(c == 0)
    def _():
        h_ref[...] = h0_ref[0]

    lane = lax.broadcasted_iota(jnp.int32, (T, LANES), 1)
    row = lax.broadcasted_iota(jnp.int32, (T, LANES), 0)
    lo = lane < N
    col = lane % N
    order = (col - row) if rev else (row - col)
    seen = order >= 0
    before = order > 0
    eye = row == col
    sq_r = lax.broadcasted_iota(jnp.int32, (T, T), 0)
    sq_c = lax.broadcasted_iota(jnp.int32, (T, T), 1)
    seen_sq = (((sq_c - sq_r) if rev else (sq_r - sq_c)) >= 0).astype(BF16)
    same_head = ((lax.broadcasted_iota(jnp.int32, (LANES, LANES), 0) < N)
                 == (lax.broadcasted_iota(jnp.int32, (LANES, LANES), 1) < N)).astype(BF16)
    col_sb = lax.broadcasted_iota(jnp.int32, (SB, LANES), 1) % N
    row_dims = (((0,), (0,)), ((), ()))
    lane_dims = (((1,), (1,)), ((), ()))

    def bd(x):
        return jnp.concatenate([jnp.where(lo, x, 0.0), jnp.where(lo, 0.0, x)], axis=0)

    def pp(x, y, split=False):
        return _mxu(x, bd(y), split=split)

    def ptp(x, y):
        full = _mxu(x, y, row_dims)
        return jnp.where(lo, full[:N], full[N:])

    w_log = -jax.nn.softplus(-(wp_ref[0] + w0_ref[...])) - 0.5
    lw_all = -jnp.exp(w_log)
    iclr_all = jax.nn.sigmoid(ap_ref[0] + a0_ref[...])
    k_all = k_ref[0].astype(F32)
    kd_all = k_all * (1.0 + (iclr_all - 1.0) * ka_ref[...])
    kkr_all = k_all * kk_ref[...]
    cs_all = _dot_mask(seen_sq, lw_all)
    tot_all = jnp.sum(lw_all, axis=0, keepdims=True)
    pairs = range(RWKV_CPAIRS)
    sl = [slice(p * LANES, (p + 1) * LANES) for p in pairs]
    sq_hi = [_split_bf16(kkr_all[:, s] * kkr_all[:, s]) for s in sl]
    ssq = [jnp.dot(sq_hi[p][0], same_head, preferred_element_type=F32)
           + jnp.dot(sq_hi[p][1], same_head, preferred_element_type=F32) for p in pairs]
    kk = [kkr_all[:, sl[p]] / jnp.maximum(jnp.sqrt(ssq[p]), 1e-12) for p in pairs]
    b_in = [kk[p] * iclr_all[:, sl[p]] for p in pairs]
    cs = [cs_all[:, s] for s in sl]
    tot = [tot_all[:, s] for s in sl]
    e_out = [jnp.exp(-cs[p]) for p in pairs]
    at = [-kk[p] * jnp.exp(cs[p] - lw_all[:, sl[p]]) for p in pairs]
    rt = [r_ref[0, :, sl[p]].astype(F32) * jnp.exp(cs[p]) for p in pairs]
    ar = [jnp.concatenate([at[p], rt[p]], axis=0) for p in pairs]
    g1 = [_mxu(ar[p], bd(b_in[p] * e_out[p]), lane_dims) for p in pairs]
    g2 = [_mxu(ar[p], bd(kd_all[:, sl[p]] * e_out[p]), lane_dims) for p in pairs]
    a_ab = [jnp.where(before, g1[p][:T], 0.0) for p in pairs]
    a_rb = [jnp.where(seen, g1[p][T:], 0.0) for p in pairs]
    a_ak = [jnp.where(before, g2[p][:T], 0.0) for p in pairs]
    a_rk = [jnp.where(seen, g2[p][T:], 0.0) for p in pairs]
    v = [v_ref[0, :, sl[p]].astype(F32) for p in pairs]
    akv = [pp(a_ak[p], v[p]) for p in pairs]
    za = [[None] * NB for _ in pairs]
    zu = [[None] * NB for _ in pairs]
    zero_blk = jnp.zeros((SB, LANES), F32)
    for kpos in range(NB):
        bk = NB - 1 - kpos if rev else kpos
        rows = slice(bk * SB, (bk + 1) * SB)
        done = [(m > bk) if rev else (m < bk) for m in range(NB)]
        cur_a = [at[p][rows] for p in pairs]
        cur_u = [akv[p][rows] for p in pairs]
        if kpos > 0:
            for p in pairs:
                zc_a = jnp.concatenate([za[p][m] if done[m] else zero_blk for m in range(NB)], axis=0)
                zc_u = jnp.concatenate([zu[p][m] if done[m] else zero_blk for m in range(NB)], axis=0)
                off = _mxu(a_ab[p][rows], jnp.concatenate([bd(zc_a), bd(zc_u)], axis=1))
                cur_a[p] = cur_a[p] + off[:, :LANES]
                cur_u[p] = cur_u[p] + off[:, LANES:]
        abc = []
        for p in pairs:
            ablk = a_ab[p][rows]
            picked = jnp.concatenate([jnp.where(col_sb == bk * SB + s, ablk, 0.0) for s in range(SB)], axis=0)
            abc.append(jnp.dot(picked.astype(BF16), same_head, preferred_element_type=F32))
        ha = [[cur_a[p][:8], cur_a[p][8:]] for p in pairs]
        hu = [[cur_u[p][:8], cur_u[p][8:]] for p in pairs]
        for j in range(SB - 1):
            s = SB - 1 - j if rev else j
            src, r8 = s // 8, s % 8
            halves = (0, 1) if (s >= 8) == rev else ((0,) if rev else (1,))
            for p in pairs:
                row_a = ha[p][src][r8:r8 + 1]
                row_u = hu[p][src][r8:r8 + 1]
                for hf in halves:
                    coef = abc[p][s * SB + hf * 8:s * SB + hf * 8 + 8]
                    ha[p][hf] = ha[p][hf] + coef * row_a
                    hu[p][hf] = hu[p][hf] + coef * row_u
        for p in pairs:
            za[p][bk] = jnp.concatenate(ha[p], axis=0)
            zu[p][bk] = jnp.concatenate(hu[p], axis=0)
    a_hat = [jnp.concatenate(za[p], axis=0) for p in pairs]
    u_loc = [jnp.concatenate(zu[p], axis=0) for p in pairs]
    h0 = [h_ref[:, sl[p]] for p in pairs]
    q_hat = [rt[p] + pp(a_rb[p], a_hat[p]) for p in pairs]
    for p in pairs:
        lhs = jnp.concatenate([q_hat[p], a_rb[p], a_rk[p]], axis=1)
        rhs = jnp.concatenate([bd(h0[p]), bd(u_loc[p]), bd(v[p])], axis=0)
        y_ref[0, :, sl[p]] = _mxu(lhs, rhs).astype(y_ref.dtype)
    e_end = [jnp.exp(tot[p] - cs[p]) for p in pairs]
    bh = [b_in[p] * e_end[p] for p in pairs]
    p_end = [_split_bf16(jnp.where(eye, jnp.exp(tot[p]), 0.0)) for p in pairs]
    decay = [jnp.dot(p_end[p][0], same_head, preferred_element_type=F32)
             + jnp.dot(p_end[p][1], same_head, preferred_element_type=F32) for p in pairs]
    corr = [ptp(bh[p], a_hat[p]) for p in pairs]
    gam = [ptp(jnp.concatenate([bh[p], kd_all[:, sl[p]] * e_end[p]], axis=0),
               jnp.concatenate([u_loc[p], v[p]], axis=0)) for p in pairs]
    for p in pairs:
        h_ref[:, sl[p]] = decay[p] * h0[p] + (pp(corr[p], h0[p], split=True) + gam[p])

    @pl.when(c == n_chunks - 1)
    def _():
        hfin_ref[0] = h_ref[...]


def rwkv_direction(rev, main, w_pre, a_pre, w0, a0, k_k, k_a, s0):
    bsz, L, _ = main.shape
    W = RWKV_W
    T, N, H = RWKV_CHUNK, RWKV_HEAD, RWKV_HEADS
    n = L // T
    gw = RWKV_CPAIRS * LANES
    ng = W // gw
    h0 = s0.transpose(0, 3, 1, 2).reshape(bsz, N, W)

    def seq(col0):
        return pl.BlockSpec((1, T, gw), lambda b, g, c: (b, (n - 1 - c) if rev else c, col0 * ng + g))
    vec = pl.BlockSpec((1, gw), lambda b, g, c: (0, g))
    st = pl.BlockSpec((1, N, gw), lambda b, g, c: (b, 0, g))
    y, hfin = pl.pallas_call(
        functools.partial(_rwkv_fs_kernel, n, rev),
        grid=(bsz, ng, n),
        in_specs=[seq(0), seq(1), seq(2), seq(0), seq(0), vec, vec, vec, vec, st],
        out_specs=[seq(0), st],
        out_shape=[jax.ShapeDtypeStruct((bsz, L, W), BF16), jax.ShapeDtypeStruct((bsz, N, W), F32)],
        scratch_shapes=[pltpu.VMEM((N, gw), F32)],
        compiler_params=pltpu.CompilerParams(
            dimension_semantics=("arbitrary",) * 3, vmem_limit_bytes=VMEM_LIMIT),
        name="rwkv_bwd_chunks" if rev else "rwkv_fwd_chunks",
    )(main, main, main, w_pre, a_pre, w0.reshape(1, W), a0.reshape(1, W), k_k.reshape(1, W), k_a.reshape(1, W), h0)
    return y, hfin.reshape(bsz, N, H, N).transpose(0, 2, 3, 1)


def _segsum(x, same_head):
    x1, x2 = _split_bf16(x)
    return (jnp.dot(x1, same_head, preferred_element_type=F32)
            + jnp.dot(x2, same_head, preferred_element_type=F32))


def _rwkv_post_kernel(yf_ref, yb_ref, r_ref, k_ref, v_ref, g_ref, af_ref, ab_ref, a0_ref, ka_ref, rk_ref,
                      lw_ref, lb_ref, o_ref):
    N = RWKV_HEAD
    same_head = ((lax.broadcasted_iota(jnp.int32, (LANES, LANES), 0) < N)
                 == (lax.broadcasted_iota(jnp.int32, (LANES, LANES), 1) < N)).astype(BF16)
    for t in range(o_ref.shape[2] // LANES):
        ls = slice(t * LANES, (t + 1) * LANES)
        wkv = yf_ref[0, :, ls].astype(F32) + yb_ref[0, :, ls].astype(F32)
        mean = _segsum(wkv, same_head) * (1.0 / N)
        cen = wkv - mean
        var = _segsum(cen * cen, same_head) * (1.0 / N)
        ln = cen * lax.rsqrt(var + RWKV_LNX_EPS) * lw_ref[:, ls] + lb_ref[:, ls]
        ka = ka_ref[:, ls]
        k_mix = ((1.0 + (jax.nn.sigmoid(af_ref[0, :, ls] + a0_ref[0:1, ls]) - 1.0) * ka)
                 + (1.0 + (jax.nn.sigmoid(ab_ref[0, :, ls] + a0_ref[1:2, ls]) - 1.0) * ka))
        rk = r_ref[0, :, ls].astype(F32) * k_ref[0, :, ls].astype(F32)
        bonus = _segsum(rk * k_mix * rk_ref[:, ls], same_head) * v_ref[0, :, ls].astype(F32)
        gate = g_ref[0, :, ls].astype(F32)
        o_ref[0, :, ls] = ((ln + bonus) * (gate * jax.nn.sigmoid(gate))).astype(o_ref.dtype)


def rwkv_post(y_f, y_b, main, a_pre_f, a_pre_b, a0, k_a, r_k, lnx_w, lnx_b):
    bsz, L, W = y_f.shape
    tr = _pick(L, (256, 128, 64))
    tw = 1024
    nw = W // tw

    def seq(col0):
        return pl.BlockSpec((1, tr, tw), lambda b, i, j: (b, i, col0 * nw + j))
    vec = pl.BlockSpec((1, tw), lambda b, i, j: (0, j))
    vec2 = pl.BlockSpec((N_DIR, tw), lambda b, i, j: (0, j))
    return pl.pallas_call(
        _rwkv_post_kernel,
        grid=(bsz, L // tr, nw),
        in_specs=[seq(0), seq(0), seq(0), seq(1), seq(2), seq(3), seq(0), seq(0), vec2, vec, vec, vec, vec],
        out_specs=seq(0),
        out_shape=jax.ShapeDtypeStruct((bsz, L, W), BF16),
        compiler_params=pltpu.CompilerParams(
            dimension_semantics=("arbitrary",) * 3, vmem_limit_bytes=VMEM_LIMIT),
        name="rwkv_post",
    )(y_f, y_b, main, main, main, main, a_pre_f, a_pre_b, a0, k_a.reshape(1, W), r_k.reshape(1, W),
      lnx_w.reshape(1, W), lnx_b.reshape(1, W))


def _split_cols(t, sizes):
    offsets, acc = [], 0
    for s in sizes[:-1]:
        acc += s
        offsets.append(acc)
    return jnp.split(t, offsets, axis=-1)


def _adaln_kernel(c_ref, w_ref, b_ref, o_ref):
    cond = c_ref[...]
    act = cond * jax.nn.sigmoid(cond)
    o_ref[...] = jnp.dot(act, w_ref[0], precision=HI, preferred_element_type=F32) + b_ref[0]


def adaln(cond, w, b, layer):
    rows, dm = cond.shape
    n = w.shape[2]
    rp = -(-rows // 8) * 8
    tn = 512
    m = pl.pallas_call(
        _adaln_kernel,
        grid=(n // tn,),
        in_specs=[pl.BlockSpec((rp, dm), lambda j: (0, 0)),
                  pl.BlockSpec((1, dm, tn), lambda j: (layer, 0, j)),
                  pl.BlockSpec((1, 1, tn), lambda j: (layer, 0, j))],
        out_specs=pl.BlockSpec((rp, tn), lambda j: (0, j)),
        out_shape=jax.ShapeDtypeStruct((rp, n), F32),
        compiler_params=pltpu.CompilerParams(dimension_semantics=("arbitrary",), vmem_limit_bytes=VMEM_LIMIT),
        name="adaln",
    )(jnp.pad(cond, ((0, rp - rows), (0, 0))), w, b.reshape(b.shape[0], 1, n))[:rows]
    return jnp.split(m, 3, axis=-1)


def _grid_pos_embed(n_tokens):
    rows = n_tokens // GRID_W
    row_id = jnp.broadcast_to(jnp.arange(rows, dtype=F32)[:, None], (rows, GRID_W)).reshape(-1)
    col_id = jnp.broadcast_to(jnp.arange(GRID_W, dtype=F32)[None, :], (rows, GRID_W)).reshape(-1)
    quarter = D_MODEL // 4
    omega = 1.0 / (POS_BASE ** (jnp.arange(quarter, dtype=F32) / quarter))

    def axis_emb(pos):
        ang = pos[:, None] * omega[None, :]
        return jnp.concatenate([jnp.sin(ang), jnp.cos(ang)], axis=-1)
    return jnp.concatenate([axis_emb(row_id), axis_emb(col_id)], axis=-1)


def _even_mixer(x, x_add, gate, h, s5_re0, s5_im0, gla0, w_in, w_out, s5_ops, glu_w, glu_b, dec_up, dec_b,
                gla_nw, final_nw=None):
    bsz, L, _ = h.shape
    n_main = sum(EVEN_SIZES[:-1])
    main = _mm3(h, w_in, n_main, BF16)
    w_tail = jnp.pad(w_in[:, n_main:], ((0, 0), (0, LANES - N_DIR * GLA_RANK)))
    dec_lr = _mm3(h, w_tail)
    gy, fin_re, fin_im = s5_scan(main[..., :S5_W], s5_ops, s5_re0, s5_im0)
    gy = gy.reshape(bsz * L, S5_W)
    mixed = matmul_glu(gy, glu_w, glu_b, main.reshape(bsz * L, n_main), S5_W, S5_W + GLA_DV_W)
    mixed, fin_gla = gla_mix(main, dec_lr, dec_up, dec_b, gla_nw, gla0, mixed.reshape(bsz, L, -1))
    return matmul_gated_residual(mixed, w_out, x, gate, x_add, final_nw), fin_re, fin_im, fin_gla


def _odd_mixer(x, gate, xs, rwkv0, w_in, w_out, w0, w2, a0, a2, k_k, k_a, r_k, lnx_w, lnx_b, final_nw=None):
    bsz, L, _ = xs.shape
    n_main = sum(ODD_SIZES[:4])
    main = _mm3(xs, w_in, n_main, BF16)
    tail = _mm3(xs, w_in[:, n_main:])
    w_lr, a_lr = _split_cols(tail, ODD_SIZES[4:])
    w_lr = jnp.tanh(w_lr).reshape(bsz, L, N_DIR, RWKV_DECAY_RANK)
    a_lr = a_lr.reshape(bsz, L, N_DIR, RWKV_ICLR_RANK)
    ys, a_pres, finals = [], [], []
    for d in range(N_DIR):
        w_pre = _mm3(w_lr[:, :, d], w2[d])
        a_pre = _mm3(a_lr[:, :, d], a2[d])
        y_d, fin = rwkv_direction(bool(d), main, w_pre, a_pre, w0[d], a0[d], k_k, k_a, rwkv0[:, d])
        ys.append(y_d)
        a_pres.append(a_pre)
        finals.append(fin)
    out = rwkv_post(ys[0], ys[1], main, a_pres[0], a_pres[1], a0, k_a, r_k.reshape(-1), lnx_w, lnx_b)
    return matmul_gated_residual(out, w_out, x, gate, None, final_nw), jnp.stack(finals, axis=1)


def kernel(x_prompt, x_sample, state_s5_re, state_s5_im, state_gla, state_rwkv, c, c_ctx, norm_w, ada_w, ada_b, final_norm_w, e_w_in, e_w_out, s5_lambda_re, s5_lambda_im, s5_log_step, s5_b_re, s5_b_im, s5_c_re, s5_c_im, s5_d, s5_glu_w, s5_glu_b, gla_decay_up, gla_decay_b, gla_norm_w, o_w_in, o_w_out, rwkv_mu, rwkv_w0, rwkv_w2, rwkv_a0, rwkv_a2, rwkv_k_k, rwkv_k_a, rwkv_r_k, rwkv_lnx_w, rwkv_lnx_b):
    bp = x_prompt.shape[0]
    depth = norm_w.shape[0]
    x_ctx = x_prompt
    x_lat, lat_add = x_sample, _grid_pos_embed(x_sample.shape[1])
    z_s5 = jnp.zeros((bp, N_DIR, S5_GROUPS, S5_STATE), F32)
    z_rwkv = jnp.zeros((bp, N_DIR, RWKV_HEADS, RWKV_HEAD, RWKV_HEAD), F32)
    new_s5_re, new_s5_im, new_gla, new_rwkv = [], [], [], []
    n_lat = c.shape[0]
    cond = jnp.concatenate([c, c_ctx[None]], axis=0)
    for i in range(depth):
        j = i // 2
        shift, scale, gate = adaln(cond, ada_w, ada_b, i)
        gt_l, gt_c = gate[:n_lat], jnp.broadcast_to(gate[n_lat:], (bp, D_MODEL))
        mu = rwkv_mu[j] if i % 2 else None
        h_ctx = norm_mod(x_ctx, norm_w[i], scale[n_lat:], shift[n_lat:], mu=mu)
        h_lat = norm_mod(x_lat, norm_w[i], scale[:n_lat], shift[:n_lat], add=lat_add, mu=mu)
        fnw = final_norm_w if i == depth - 1 else None
        if i % 2 == 0:
            s5_ops = s5_operators(s5_lambda_re[j], s5_lambda_im[j], s5_log_step[j], s5_b_re[j], s5_b_im[j],
                                  s5_c_re[j], s5_c_im[j], s5_d[j])
            p = (e_w_in[j], e_w_out[j], s5_ops, s5_glu_w[j], s5_glu_b[j], gla_decay_up[j], gla_decay_b[j],
                 gla_norm_w[j])
            x_ctx, fr, fi, fg = _even_mixer(x_ctx, None, gt_c, h_ctx, z_s5, z_s5, None, *p, final_nw=fnw)
            x_lat, _, _, _ = _even_mixer(x_lat, lat_add, gt_l, h_lat, state_s5_re[:, j], state_s5_im[:, j],
                                         state_gla[:, j], *p, final_nw=fnw)
            lat_add = None
            new_s5_re.append(fr)
            new_s5_im.append(fi)
            new_gla.append(fg)
        else:
            p = (o_w_in[j], o_w_out[j], rwkv_w0[j], rwkv_w2[j], rwkv_a0[j], rwkv_a2[j],
                 rwkv_k_k[j], rwkv_k_a[j], rwkv_r_k[j], rwkv_lnx_w[j], rwkv_lnx_b[j])
            x_ctx, fw = _odd_mixer(x_ctx, gt_c, h_ctx, z_rwkv, *p, final_nw=fnw)
            x_lat, _ = _odd_mixer(x_lat, gt_l, h_lat, state_rwkv[:, j], *p, final_nw=fnw)
            new_rwkv.append(fw)
    if depth == 0:
        x_ctx, x_lat = final_norm(x_ctx, final_norm_w), final_norm(x_lat + lat_add, final_norm_w)
    return (x_ctx, x_lat, jnp.stack(new_s5_re, axis=1), jnp.stack(new_s5_im, axis=1),
            jnp.stack(new_gla, axis=1), jnp.stack(new_rwkv, axis=1))
```

```python
import functools

import jax
import jax.numpy as jnp
from jax import lax
from jax.experimental import pallas as pl
from jax.experimental.pallas import tpu as pltpu

D_MODEL = 2048
GRID_W = 64
POS_BASE = 10000.0
N_DIR = 2
EPS = 1e-6
S5_W = 1024
S5_GROUP_CH = 16
S5_GROUPS = 64
S5_STATE = 64
S5_CHUNK = 16
S5_TILE_GROUPS = 8
GLA_HEADS = 6
GLA_DV = 512
GLA_DK = 256
GLA_DK_W = 1536
GLA_DV_W = 3072
GLA_RANK = 16
GLA_NORMALIZER = 16.0
GLA_CHUNK = 64
GLA_NC = 16
GLA_LOG_DECAY_MIN = -1.0
EVEN_SIZES = (S5_W, S5_W, GLA_DK_W, GLA_DK_W, GLA_DV_W, GLA_DV_W, N_DIR * GLA_RANK)
RWKV_W = 2048
RWKV_HEAD = 64
RWKV_HEADS = 32
RWKV_DECAY_RANK = 96
RWKV_ICLR_RANK = 96
RWKV_LNX_EPS = 64e-5
ODD_SIZES = (RWKV_W, RWKV_W, RWKV_W, RWKV_W, N_DIR * RWKV_DECAY_RANK, N_DIR * RWKV_ICLR_RANK)
RWKV_CHUNK = 64
RWKV_CPAIRS = 16
RWKV_SUB = 16
LANES = 128

VMEM_LIMIT = 48 * 1024 * 1024
HI = lax.Precision.HIGHEST
BF16 = jnp.bfloat16
F32 = jnp.float32


def _mm_kernel(x_ref, w_ref, o_ref):
    o_ref[...] = jnp.dot(x_ref[...], w_ref[...], preferred_element_type=F32).astype(o_ref.dtype)


def _pick(n, prefs):
    for p in prefs:
        if n % p == 0:
            return p
    return n


def matmul(x, w, n_cols=None, out_dtype=F32):
    m, k = x.shape
    n = w.shape[1] if n_cols is None else n_cols
    x = x.astype(BF16)
    w = w.astype(BF16)
    tm = _pick(m, (2048, 1024, 512, 256, 128, 64, 32, 16, 8))
    tn = _pick(n, (1024, 512, 384, 256, 128))
    return pl.pallas_call(
        _mm_kernel,
        grid=(m // tm, n // tn),
        in_specs=[pl.BlockSpec((tm, k), lambda i, j: (i, 0)),
                  pl.BlockSpec((k, tn), lambda i, j: (0, j))],
        out_specs=pl.BlockSpec((tm, tn), lambda i, j: (i, j)),
        out_shape=jax.ShapeDtypeStruct((m, n), out_dtype),
        compiler_params=pltpu.CompilerParams(
            dimension_semantics=("arbitrary", "arbitrary"), vmem_limit_bytes=VMEM_LIMIT),
        name="proj_matmul",
    )(x, w)


def _mm3(h, w, n_cols=None, out_dtype=F32):
    b, l, k = h.shape
    return matmul(h.reshape(b * l, k), w, n_cols, out_dtype).reshape(b, l, -1)


def _mm_residual_kernel(has_add, has_norm, x_ref, w_ref, res_ref, gate_ref, *refs):
    o_ref = refs[-1]
    acc = jnp.dot(x_ref[...], w_ref[...], preferred_element_type=F32)
    res = res_ref[...] + refs[0][...] if has_add else res_ref[...]
    out = res + gate_ref[0] * acc
    if has_norm:
        nw_ref = refs[-2]
        out = out * lax.rsqrt(jnp.mean(out * out, axis=-1, keepdims=True) + EPS) * nw_ref[...]
    o_ref[...] = out


def matmul_gated_residual(x, w, res, gate, res_add=None, final_nw=None):
    bsz, L, k = x.shape
    n = w.shape[1]
    m = bsz * L
    if final_nw is None:
        tm = _pick(L, (1024, 512, 256, 128))
        tn = _pick(n, (1024, 512, 256, 128) if k <= 2048 else (512, 256, 128))
    else:
        tm, tn = _pick(L, (512, 256, 128)), n
    per_b = L // tm
    in_specs = [pl.BlockSpec((tm, k), lambda i, j: (i, 0)),
                pl.BlockSpec((k, tn), lambda i, j: (0, j)),
                pl.BlockSpec((tm, tn), lambda i, j: (i, j)),
                pl.BlockSpec((1, 1, tn), lambda i, j: (i // per_b, 0, j))]
    args = [x.reshape(m, k).astype(BF16), w.astype(BF16), res.reshape(m, n), gate.reshape(bsz, 1, n)]
    if res_add is not None:
        in_specs.append(pl.BlockSpec((tm, tn), lambda i, j: (i % per_b, j)))
        args.append(res_add)
    if final_nw is not None:
        in_specs.append(pl.BlockSpec((1, tn), lambda i, j: (0, j)))
        args.append(final_nw.reshape(1, n))
    out = pl.pallas_call(
        functools.partial(_mm_residual_kernel, res_add is not None, final_nw is not None),
        grid=(m // tm, n // tn),
        in_specs=in_specs,
        out_specs=pl.BlockSpec((tm, tn), lambda i, j: (i, j)),
        out_shape=jax.ShapeDtypeStruct((m, n), F32),
        compiler_params=pltpu.CompilerParams(
            dimension_semantics=("arbitrary", "arbitrary"), vmem_limit_bytes=VMEM_LIMIT),
        name="proj_residual",
    )(*args)
    return out.reshape(bsz, L, n)


def _mm_glu_kernel(x_ref, w_ref, b_ref, g_ref, xt_ref, o_ref):
    acc = jnp.dot(x_ref[...], w_ref[...], preferred_element_type=F32) + b_ref[...]
    gy = xt_ref[...].astype(F32)
    gate = g_ref[...].astype(F32)
    o_ref[...] = (gy * jax.nn.sigmoid(acc) * (gate * jax.nn.sigmoid(gate))).astype(o_ref.dtype)


def matmul_glu(gy, w, b, main, g_col0, n_total):
    m, k = gy.shape
    n = w.shape[1]
    tm = _pick(m, (1024, 512, 256, 128))
    tn = 512
    return pl.pallas_call(
        _mm_glu_kernel,
        grid=(m // tm, n // tn),
        in_specs=[pl.BlockSpec((tm, k), lambda i, j: (i, 0)),
                  pl.BlockSpec((k, tn), lambda i, j: (0, j)),
                  pl.BlockSpec((1, tn), lambda i, j: (0, j)),
                  pl.BlockSpec((tm, tn), lambda i, j: (i, g_col0 // tn + j)),
                  pl.BlockSpec((tm, tn), lambda i, j: (i, j))],
        out_specs=pl.BlockSpec((tm, tn), lambda i, j: (i, j)),
        out_shape=jax.ShapeDtypeStruct((m, n_total), BF16),
        compiler_params=pltpu.CompilerParams(
            dimension_semantics=("arbitrary", "arbitrary"), vmem_limit_bytes=VMEM_LIMIT),
        name="s5_glu_gate",
    )(gy, w.astype(BF16), b.reshape(1, n), main, gy)


def _norm_mod_kernel(has_add, has_shift, n_row_blocks, x_ref, nw_ref, sc_ref, sh_ref, *refs):
    o_ref = refs[-1]

    def modulated(x):
        inv = lax.rsqrt(jnp.mean(x * x, axis=-1, keepdims=True) + EPS)
        return x * inv * nw_ref[...] * (1.0 + sc_ref[0]) + sh_ref[0]
    x = x_ref[0]
    if has_add:
        x = x + refs[0][...]
    h = modulated(x)
    if has_shift:
        prev_ref, next_ref, mu_ref = refs[0], refs[1], refs[2]
        i = pl.program_id(1)
        tr = h.shape[0]
        row = lax.broadcasted_iota(jnp.int32, h.shape, 0)
        before = jnp.where(i > 0, modulated(prev_ref[0])[7:8], 0.0)
        after = jnp.where(i < n_row_blocks - 1, modulated(next_ref[0])[0:1], 0.0)
        h_prev = jnp.where(row == 0, before, pltpu.roll(h, 1, 0))
        h_next = jnp.where(row == tr - 1, after, pltpu.roll(h, tr - 1, 0))
        h = h + mu_ref[0:1] * (h_prev - h) + mu_ref[1:2] * (h_next - h)
    o_ref[0] = h.astype(o_ref.dtype)


def norm_mod(x, nw, scale, shift, add=None, mu=None):
    assert add is None or mu is None
    bsz, L, dm = x.shape
    tr = _pick(L, (256, 128, 64))
    nb = scale.shape[0]
    nblk = L // tr
    cond = pl.BlockSpec((1, 1, dm), lambda b, i: (b if nb > 1 else 0, 0, 0))
    in_specs = [pl.BlockSpec((1, tr, dm), lambda b, i: (b, i, 0)), pl.BlockSpec((1, dm), lambda b, i: (0, 0)),
                cond, cond]
    args = [x, nw.reshape(1, dm), scale.reshape(nb, 1, dm), shift.reshape(nb, 1, dm)]
    if add is not None:
        in_specs.append(pl.BlockSpec((tr, dm), lambda b, i: (i, 0)))
        args.append(add)
    if mu is not None:
        r8 = tr // 8
        in_specs += [pl.BlockSpec((1, 8, dm), lambda b, i: (b, jnp.maximum(i * r8 - 1, 0), 0)),
                     pl.BlockSpec((1, 8, dm), lambda b, i: (b, jnp.minimum((i + 1) * r8, L // 8 - 1), 0)),
                     pl.BlockSpec((2, dm), lambda b, i: (0, 0))]
        args += [x, x, mu]
    return pl.pallas_call(
        functools.partial(_norm_mod_kernel, add is not None, mu is not None, nblk),
        grid=(bsz, nblk),
        in_specs=in_specs,
        out_specs=pl.BlockSpec((1, tr, dm), lambda b, i: (b, i, 0)),
        out_shape=jax.ShapeDtypeStruct((bsz, L, dm), BF16),
        compiler_params=pltpu.CompilerParams(
            dimension_semantics=("arbitrary", "arbitrary"), vmem_limit_bytes=VMEM_LIMIT),
        name="norm_mod",
    )(*args)


def _final_norm_kernel(x_ref, nw_ref, o_ref):
    x = x_ref[0]
    o_ref[0] = x * lax.rsqrt(jnp.mean(x * x, axis=-1, keepdims=True) + EPS) * nw_ref[...]


def final_norm(x, nw):
    bsz, L, dm = x.shape
    tr = _pick(L, (256, 128, 64))
    return pl.pallas_call(
        _final_norm_kernel,
        grid=(bsz, L // tr),
        in_specs=[pl.BlockSpec((1, tr, dm), lambda b, i: (b, i, 0)), pl.BlockSpec((1, dm), lambda b, i: (0, 0))],
        out_specs=pl.BlockSpec((1, tr, dm), lambda b, i: (b, i, 0)),
        out_shape=jax.ShapeDtypeStruct((bsz, L, dm), F32),
        compiler_params=pltpu.CompilerParams(
            dimension_semantics=("arbitrary", "arbitrary"), vmem_limit_bytes=VMEM_LIMIT),
        name="final_norm",
    )(x, nw.reshape(1, dm))


def s5_operators(lam_re, lam_im, log_step, b_re, b_im, c_re, c_im, d_skip):
    T = S5_CHUNK
    dt = jnp.exp(log_step)[..., None]
    mag = jnp.exp(lam_re * dt)
    ab_re, ab_im = mag * jnp.cos(lam_im * dt), mag * jnp.sin(lam_im * dt)
    den = lam_re * lam_re + lam_im * lam_im
    f_re = ((ab_re - 1.0) * lam_re + ab_im * lam_im) / den
    f_im = (ab_im * lam_re - (ab_re - 1.0) * lam_im) / den
    bb_re = f_re[..., None] * b_re - f_im[..., None] * b_im
    bb_im = f_re[..., None] * b_im + f_im[..., None] * b_re
    kk = jnp.arange(T + 1, dtype=F32)[:, None, None, None]
    pmag = jnp.exp(kk * (lam_re * dt))
    pr = pmag * jnp.cos(kk * (lam_im * dt))
    pi = pmag * jnp.sin(kk * (lam_im * dt))
    zr = pr[:T, :, :, :, None] * bb_re - pi[:T, :, :, :, None] * bb_im
    zi = pr[:T, :, :, :, None] * bb_im + pi[:T, :, :, :, None] * bb_re
    kern = (jnp.einsum('dghp,kdgpj->kdghj', c_re, zr, precision=HI)
            - jnp.einsum('dghp,kdgpj->kdghj', c_im, zi, precision=HI))
    t_idx = jnp.arange(T)[:, None]
    s_idx = jnp.arange(T)[None, :]
    lag_f = t_idx - s_idx
    lag_b = s_idx - t_idx
    m_f = jnp.where((lag_f >= 0)[:, :, None, None, None], kern[:, 0][jnp.clip(lag_f, 0, T - 1)], 0.0)
    m_b = jnp.where((lag_b >= 0)[:, :, None, None, None], kern[:, 1][jnp.clip(lag_b, 0, T - 1)], 0.0)
    m = m_f + m_b
    eye_t = jnp.eye(T, dtype=F32)[:, :, None, None, None]
    eye_h = jnp.eye(S5_GROUP_CH, dtype=F32)[None, None, None]
    m = m + eye_t * eye_h * d_skip.reshape(S5_GROUPS, S5_GROUP_CH)[None, None, :, :, None]
    g = m.shape[2]
    m_t = m.transpose(2, 1, 4, 0, 3).reshape(g, T * S5_GROUP_CH, T * S5_GROUP_CH)
    pf_r, pf_i = pr[T - 1::-1][:T, 0], pi[T - 1::-1][:T, 0]
    pb_r, pb_i = pr[:T, 1], pi[:T, 1]

    def f_mat(p_r, p_i, d):
        re = p_r[..., None] * bb_re[d][None] - p_i[..., None] * bb_im[d][None]
        im = p_r[..., None] * bb_im[d][None] + p_i[..., None] * bb_re[d][None]
        re = re.transpose(1, 0, 3, 2).reshape(g, T * S5_GROUP_CH, S5_STATE)
        im = im.transpose(1, 0, 3, 2).reshape(g, T * S5_GROUP_CH, S5_STATE)
        return re, im
    ff_re, ff_im = f_mat(pf_r, pf_i, 0)
    fb_re, fb_im = f_mat(pb_r, pb_i, 1)
    a_t = jnp.concatenate([m_t, ff_re, fb_re, ff_im, fb_im], axis=-1)
    ef_r, ef_i = pr[1:T + 1, 0], pi[1:T + 1, 0]
    eb_r, eb_i = pr[T:0:-1, 1], pi[T:0:-1, 1]

    def e_mat(p_r, p_i, d):
        er = c_re[d][None] * p_r[:, :, None, :] - c_im[d][None] * p_i[:, :, None, :]
        ei = -(c_re[d][None] * p_i[:, :, None, :] + c_im[d][None] * p_r[:, :, None, :])
        er = er.transpose(1, 3, 0, 2).reshape(g, S5_STATE, T * S5_GROUP_CH)
        ei = ei.transpose(1, 3, 0, 2).reshape(g, S5_STATE, T * S5_GROUP_CH)
        return er, ei
    efr, efi = e_mat(ef_r, ef_i, 0)
    ebr, ebi = e_mat(eb_r, eb_i, 1)
    e_t = jnp.concatenate([efr, ebr, efi, ebi], axis=1)
    lam_t = jnp.concatenate([pr[T, 0], pr[T, 1], pi[T, 0], pi[T, 1]], axis=-1)[:, None, :]
    return a_t.astype(BF16), e_t.astype(BF16), lam_t


def _s5_kernel(n_steps, bsz, pair, x8_ref, sel_ref, at_ref, et_ref, lam_ref, h0_ref, y_ref, hfin_ref, z_ref,
               hent_ref):
    P = S5_STATE
    ut = jnp.dot(x8_ref[0], sel_ref[0], preferred_element_type=F32).astype(BF16)
    z_ref[...] = jnp.dot(ut, at_ref[0], preferred_element_type=F32)
    lam = lam_ref[0]
    a_re, a_im = lam[:, 0:2 * P], lam[:, 2 * P:4 * P]
    h0 = h0_ref[0]
    h0_re, h0_im = h0[:, 0:2 * P], h0[:, 2 * P:4 * P]
    fwd_lanes = lax.broadcasted_iota(jnp.int32, (bsz, 2 * P), 1) < P
    m_re, m_im = a_re, a_im
    if pair:
        cols, half = z_ref.shape[0], bsz // 2
        g_re, g_im = z_ref[:, 4 * P:6 * P], z_ref[:, 6 * P:8 * P]
        rows = lax.broadcasted_iota(jnp.int32, (cols, 2 * P), 0)
        fwd_all = lax.broadcasted_iota(jnp.int32, (cols, 2 * P), 1) < P

        def neighbour(g):
            up = jnp.where(rows < cols - half, pltpu.roll(g, cols - half, 0), 0.0)
            down = jnp.where(rows >= half, pltpu.roll(g, half, 0), 0.0)
            return jnp.where(fwd_all, up, down)
        z_ref[:, 4 * P:6 * P] = a_re * g_re - a_im * g_im + neighbour(g_re)
        z_ref[:, 6 * P:8 * P] = a_re * g_im + a_im * g_re + neighbour(g_im)
        lo = lax.broadcasted_iota(jnp.int32, (bsz, 2 * P), 0) < half
        ah_re = a_re * h0_re - a_im * h0_im
        ah_im = a_re * h0_im + a_im * h0_re

        def swap(x):
            return pltpu.roll(x, half, 0)
        f_re = h0_re + swap(ah_re + jnp.where(lo, g_re[0:bsz], 0.0))
        f_im = h0_im + swap(ah_im + jnp.where(lo, g_im[0:bsz], 0.0))
        b_re = swap(h0_re) + ah_re + swap(jnp.where(lo, 0.0, g_re[cols - bsz:cols]))
        b_im = swap(h0_im) + ah_im + swap(jnp.where(lo, 0.0, g_im[cols - bsz:cols]))
        h0_re, h0_im = jnp.where(fwd_lanes, f_re, b_re), jnp.where(fwd_lanes, f_im, b_im)
        m_re, m_im = a_re * a_re - a_im * a_im, 2.0 * a_re * a_im

    def step(c, carry):
        h_re, h_im = carry
        rf = pl.ds(pl.multiple_of(c * bsz, 8), bsz)
        rb = pl.ds(pl.multiple_of((n_steps - 1 - c) * bsz, 8), bsz)
        hent_ref[rf, 0:P] = h_re[:, 0:P]
        hent_ref[rb, P:2 * P] = h_re[:, P:2 * P]
        hent_ref[rf, 2 * P:3 * P] = h_im[:, 0:P]
        hent_ref[rb, 3 * P:4 * P] = h_im[:, P:2 * P]
        g_re = jnp.where(fwd_lanes, z_ref[rf, 4 * P:6 * P], z_ref[rb, 4 * P:6 * P])
        g_im = jnp.where(fwd_lanes, z_ref[rf, 6 * P:8 * P], z_ref[rb, 6 * P:8 * P])
        return m_re * h_re - m_im * h_im + g_re, m_re * h_im + m_im * h_re + g_im
    h_re, h_im = lax.fori_loop(0, n_steps, step, (h0_re, h0_im))
    if pair:
        h_re = jnp.where(fwd_lanes, h_re, pltpu.roll(h_re, bsz // 2, 0))
        h_im = jnp.where(fwd_lanes, h_im, pltpu.roll(h_im, bsz // 2, 0))
    hfin_ref[0, :, 0:2 * P] = h_re
    hfin_ref[0, :, 2 * P:4 * P] = h_im
    y = z_ref[:, 0:4 * P] + jnp.dot(hent_ref[...].astype(BF16), et_ref[0], preferred_element_type=F32)
    y_ref[0] = jax.nn.gelu(y).astype(y_ref.dtype)


def _s5_unpack_kernel(yt_ref, selt_ref, o_ref):
    acc = jnp.dot(yt_ref[0], selt_ref[0], preferred_element_type=F32)
    for gl in range(1, S5_TILE_GROUPS):
        acc = acc + jnp.dot(yt_ref[gl], selt_ref[gl], preferred_element_type=F32)
    o_ref[0] = acc.astype(o_ref.dtype)


def s5_scan(u, ops, h0_re, h0_im):
    a_t, e_t, lam_t = ops
    b_real, L, _ = u.shape
    T, G, H, P = S5_CHUNK, S5_GROUPS, S5_GROUP_CH, S5_STATE
    TG = S5_TILE_GROUPS
    n = L // T
    pair = b_real == 4 and n % 2 == 0
    bsz = -(-b_real // 8) * 8
    rpc = b_real if pair else bsz
    n_steps = n // 2 if pair else n
    cols = n * rpc
    x8 = u.reshape(b_real, n, T, G // TG, LANES).transpose(3, 1, 0, 2, 4).astype(BF16)
    x8 = jnp.pad(x8, ((0, 0), (0, 0), (0, rpc - b_real), (0, 0), (0, 0))).reshape(G // TG, cols, T * LANES)
    src = jnp.arange(T * LANES)
    dst = jnp.arange(T * H)
    sel = ((src[None, :, None] // LANES == dst[None, None, :] // H)
           & (src[None, :, None] % H == dst[None, None, :] % H)
           & ((src[None, :, None] % LANES) // H == jnp.arange(TG)[:, None, None])).astype(BF16)
    h0 = jnp.concatenate([h0_re[:, 0], h0_re[:, 1], h0_im[:, 0], h0_im[:, 1]], axis=-1)
    h0 = jnp.pad(h0.transpose(1, 0, 2), ((0, 0), (0, bsz - b_real), (0, 0)))
    yt, hfin = pl.pallas_call(
        functools.partial(_s5_kernel, n_steps, bsz, pair),
        grid=(G,),
        in_specs=[pl.BlockSpec((1, cols, T * LANES), lambda g: (g // TG, 0, 0)),
                  pl.BlockSpec((1, T * LANES, T * H), lambda g: (g % TG, 0, 0)),
                  pl.BlockSpec((1, T * H, 8 * P), lambda g: (g, 0, 0)),
                  pl.BlockSpec((1, 4 * P, T * H), lambda g: (g, 0, 0)),
                  pl.BlockSpec((1, 1, 4 * P), lambda g: (g, 0, 0)),
                  pl.BlockSpec((1, bsz, 4 * P), lambda g: (g, 0, 0))],
        out_specs=[pl.BlockSpec((1, cols, T * H), lambda g: (g, 0, 0)),
                   pl.BlockSpec((1, bsz, 4 * P), lambda g: (g, 0, 0))],
        out_shape=[jax.ShapeDtypeStruct((G, cols, T * H), BF16),
                   jax.ShapeDtypeStruct((G, bsz, 4 * P), F32)],
        scratch_shapes=[pltpu.VMEM((cols, 8 * P), F32), pltpu.VMEM((cols, 4 * P), F32)],
        compiler_params=pltpu.CompilerParams(dimension_semantics=("arbitrary",), vmem_limit_bytes=VMEM_LIMIT),
        name="s5_chunk_scan",
    )(x8, sel, a_t, e_t, lam_t, h0)
    tr = _pick(cols, (512, 256, 128))
    y8 = pl.pallas_call(
        _s5_unpack_kernel,
        grid=(G // TG, cols // tr),
        in_specs=[pl.BlockSpec((TG, tr, T * H), lambda t, i: (t, i, 0)),
                  pl.BlockSpec((TG, T * H, T * LANES), lambda t, i: (0, 0, 0))],
        out_specs=pl.BlockSpec((1, tr, T * LANES), lambda t, i: (t, i, 0)),
        out_shape=jax.ShapeDtypeStruct((G // TG, cols, T * LANES), BF16),
        compiler_params=pltpu.CompilerParams(
            dimension_semantics=("arbitrary", "arbitrary"), vmem_limit_bytes=VMEM_LIMIT),
        name="s5_unpack",
    )(yt, sel.transpose(0, 2, 1))
    y = y8.reshape(G // TG, n, rpc, T, LANES)[:, :, :b_real].transpose(2, 1, 3, 0, 4).reshape(b_real, L, G * H)
    hfin = hfin[:, :b_real].transpose(1, 0, 2)
    fin_re = jnp.stack([hfin[..., 0:P], hfin[..., P:2 * P]], axis=1)
    fin_im = jnp.stack([hfin[..., 2 * P:3 * P], hfin[..., 3 * P:4 * P]], axis=1)
    return y, fin_re, fin_im


def _dot_t(a, b):
    return lax.dot_general(a, b, (((1,), (1,)), ((), ())), preferred_element_type=F32)


def _dot_mask(mask_bf16, x, x_rows_to_sublanes=False):
    def d(b):
        if x_rows_to_sublanes:
            return lax.dot_general(b, mask_bf16, (((0,), (0,)), ((), ())), preferred_element_type=F32)
        return jnp.dot(mask_bf16, b, preferred_element_type=F32)
    x1 = x.astype(BF16)
    r1 = x - x1.astype(F32)
    x2 = r1.astype(BF16)
    x3 = (r1 - x2.astype(F32)).astype(BF16)
    return d(x1) + (d(x2) + d(x3))


def _gla_block_kernel(n_blocks, NC, has_s0, q_ref, k_ref, v_ref, g_ref, lr_ref, up_ref, db_ref, nw_ref, dst_ref,
                      *refs):
    s0_ref = refs[0] if has_s0 else None
    out_ref, sfin_ref, s_ref, of_ref, qd_ref, ov_ref, kv_ref, dc_ref = refs[1:] if has_s0 else refs
    C = GLA_CHUNK
    R = C * NC
    d = pl.program_id(2)
    c = pl.program_id(3)
    bidx = jnp.where(d == 0, c, n_blocks - 1 - c)

    @pl.when(c == 0)
    def _():
        s_ref[...] = s0_ref[0, 0, 0] if has_s0 else jnp.zeros_like(s_ref)

    z = _mxu(lr_ref[0], up_ref[0], split=True) + db_ref[0]
    gc = jnp.maximum(jax.nn.log_sigmoid(z) * (1.0 / GLA_NORMALIZER), GLA_LOG_DECAY_MIN)
    row_c = lax.broadcasted_iota(jnp.int32, (C, C), 0)
    col_c = lax.broadcasted_iota(jnp.int32, (C, C), 1)
    seen_c = jnp.where(d == 0, row_c - col_c, col_c - row_c) >= 0
    seen_bf = seen_c.astype(BF16)
    rs = [slice(i * C, (i + 1) * C) for i in range(NC)]
    bcum_c = [_dot_mask(seen_bf, gc[r]) for r in rs]
    btot_c = [jnp.broadcast_to(jnp.where(d == 0, b[C - 1:C], b[0:1]), (C, GLA_DK)) for b in bcum_c]
    bcum = jnp.concatenate(bcum_c, axis=0)
    btot = jnp.concatenate(btot_c, axis=0)
    q_dec = (q_ref[0].astype(F32) * (GLA_DK ** -0.5) * jnp.exp(bcum)).astype(BF16)
    k = k_ref[0].astype(F32)
    k_inv = (k * jnp.exp(-bcum)).astype(BF16)
    k_end = (k * jnp.exp(btot - bcum)).astype(BF16)
    v = v_ref[0].astype(BF16)
    ones_c = jnp.ones((C, LANES), BF16)
    qd_ref[...] = q_dec.reshape(NC, C, GLA_DK)
    att = [jnp.where(seen_c, _dot_t(q_dec[r], k_inv[r]), 0.0).astype(BF16) for r in rs]
    for i in range(NC):
        kv_ref[i] = lax.dot_general(k_end[rs[i]], v[rs[i]], (((0,), (0,)), ((), ())), preferred_element_type=F32)
    for i in range(NC):
        ov_ref[i] = jnp.dot(att[i], v[rs[i]], preferred_element_type=F32)
    for i in range(NC):
        dc_ref[i] = _dot_mask(ones_c, gc[rs[i]], x_rows_to_sublanes=True)

    for i in range(NC):
        ci = jnp.where(d == 0, i, NC - 1 - i)
        s_old = s_ref[...]
        ov_ref[ci] = ov_ref[ci] + jnp.dot(qd_ref[ci], s_old.astype(BF16), preferred_element_type=F32)
        s_ref[...] = jnp.exp(dc_ref[ci][:, 0:1]) * s_old + kv_ref[ci]
    rows = pl.ds(pl.multiple_of(bidx * R, R), R)

    @pl.when(d == 0)
    def _():
        of_ref[rows, :] = ov_ref[...].reshape(R, GLA_DV)

    @pl.when(d == 1)
    def _():
        tot = of_ref[rows, :] + ov_ref[...].reshape(R, GLA_DV)
        nrm = tot * lax.rsqrt(jnp.mean(tot * tot, axis=-1, keepdims=True) + EPS) * nw_ref[0]
        gate = g_ref[0].astype(F32)
        out_ref[0] = (nrm * (gate * jax.nn.sigmoid(gate))).astype(out_ref.dtype)

    @pl.when(c == n_blocks - 1)
    def _():
        sfin_ref[0, 0, 0] = s_ref[...]


def gla_mix(main, dec_lr, dec_up, dec_b, gla_nw, s0, dst):
    bsz, L, _ = main.shape
    H, DK, DV = GLA_HEADS, GLA_DK, GLA_DV
    nc = min(GLA_NC, L // GLA_CHUNK)
    C = GLA_CHUNK * nc
    n = L // C
    q_blk = sum(EVEN_SIZES[:2]) // DK
    k_blk = sum(EVEN_SIZES[:3]) // DK
    v_blk = sum(EVEN_SIZES[:4]) // DV
    g_blk = sum(EVEN_SIZES[:5]) // DV
    up = jnp.zeros((N_DIR, LANES, GLA_DK_W), F32)
    for d in range(N_DIR):
        up = up.at[d, d * GLA_RANK:(d + 1) * GLA_RANK].set(dec_up[d])
    db = dec_b.reshape(N_DIR, 1, GLA_DK_W)
    nw = gla_nw.reshape(1, GLA_DV_W)

    def chunk(d, c):
        return c + d * (n - 1 - 2 * c)

    def out_chunk(d, c):
        return (n - 1) - d * c
    state = pl.BlockSpec((1, 1, 1, DK, DV), lambda b, h, d, c: (b, d, h, 0, 0))
    has_s0 = s0 is not None
    out, sfin = pl.pallas_call(
        functools.partial(_gla_block_kernel, n, nc, has_s0),
        grid=(bsz, H, N_DIR, n),
        in_specs=[pl.BlockSpec((1, C, DK), lambda b, h, d, c: (b, chunk(d, c), q_blk + h)),
                  pl.BlockSpec((1, C, DK), lambda b, h, d, c: (b, chunk(d, c), k_blk + h)),
                  pl.BlockSpec((1, C, DV), lambda b, h, d, c: (b, chunk(d, c), v_blk + h)),
                  pl.BlockSpec((1, C, DV), lambda b, h, d, c: (b, chunk(d, c), g_blk + h)),
                  pl.BlockSpec((1, C, LANES), lambda b, h, d, c: (b, chunk(d, c), 0)),
                  pl.BlockSpec((1, LANES, DK), lambda b, h, d, c: (d, 0, h)),
                  pl.BlockSpec((1, 1, DK), lambda b, h, d, c: (d, 0, h)),
                  pl.BlockSpec((1, DV), lambda b, h, d, c: (0, h)),
                  pl.BlockSpec(memory_space=pl.ANY)] + ([state] if has_s0 else []),
        input_output_aliases={8: 0},
        out_specs=[pl.BlockSpec((1, C, DV), lambda b, h, d, c: (b, out_chunk(d, c), S5_W // DV + h)), state],
        out_shape=[jax.ShapeDtypeStruct(dst.shape, BF16),
                   jax.ShapeDtypeStruct((bsz, N_DIR, H, DK, DV), F32)],
        scratch_shapes=[pltpu.VMEM((DK, DV), F32), pltpu.VMEM((L, DV), F32),
                        pltpu.VMEM((nc, GLA_CHUNK, DK), BF16), pltpu.VMEM((nc, GLA_CHUNK, DV), F32),
                        pltpu.VMEM((nc, DK, DV), F32), pltpu.VMEM((nc, DK, LANES), F32)],
        compiler_params=pltpu.CompilerParams(
            dimension_semantics=("arbitrary",) * 4, vmem_limit_bytes=VMEM_LIMIT),
        name="gla_chunk_scan",
    )(main, main, main, main, dec_lr, up, db, nw, dst, *([s0] if has_s0 else []))
    return out, sfin


def _split_bf16(x):
    hi = x.astype(BF16)
    return hi, (x - hi.astype(F32)).astype(BF16)


def _mxu(x, y, dims=(((1,), (0,)), ((), ())), split=False):
    def d(a, b):
        return lax.dot_general(a, b, dims, preferred_element_type=F32)
    if not split:
        return d(x.astype(BF16), y.astype(BF16))
    xh, xl = _split_bf16(x)
    yh, yl = _split_bf16(y)
    return d(xh, yh) + (d(xh, yl) + d(xl, yh))


def _rwkv_fs_kernel(n_chunks, rev, r_ref, k_ref, v_ref, wp_ref, ap_ref, w0_ref, a0_ref, kk_ref, ka_ref, h0_ref,
                    y_ref, hfin_ref, h_ref):
    T, N, SB = RWKV_CHUNK, RWKV_HEAD, RWKV_SUB
    NB = T // SB
    c = pl.program_id(2)

    @pl.when(c == 0)
    def _():
        h_ref[...] = h0_ref[0]

    lane = lax.broadcasted_iota(jnp.int32, (T, LANES), 1)
    row = lax.broadcasted_iota(jnp.int32, (T, LANES), 0)
    lo = lane < N
    col = lane % N
    order = (col - row) if rev else (row - col)
    seen = order >= 0
    before = order > 0
    eye = row == col
    sq_r = lax.broadcasted_iota(jnp.int32, (T, T), 0)
    sq_c = lax.broadcasted_iota(jnp.int32, (T, T), 1)
    seen_sq = (((sq_c - sq_r) if rev else (sq_r - sq_c)) >= 0).astype(BF16)
    same_head = ((lax.broadcasted_iota(jnp.int32, (LANES, LANES), 0) < N)
                 == (lax.broadcasted_iota(jnp.int32, (LANES, LANES), 1) < N)).astype(BF16)
    col_sb = lax.broadcasted_iota(jnp.int32, (SB, LANES), 1) % N
    row_dims = (((0,), (0,)), ((), ()))
    lane_dims = (((1,), (1,)), ((), ()))

    def bd(x):
        return jnp.concatenate([jnp.where(lo, x, 0.0), jnp.where(lo, 0.0, x)], axis=0)

    def pp(x, y, split=False):
        return _mxu(x, bd(y), split=split)

    def ptp(x, y):
        full = _mxu(x, y, row_dims)
        return jnp.where(lo, full[:N], full[N:])

    w_log = -jax.nn.softplus(-(wp_ref[0] + w0_ref[...])) - 0.5
    lw_all = -jnp.exp(w_log)
    iclr_all = jax.nn.sigmoid(ap_ref[0] + a0_ref[...])
    k_all = k_ref[0].astype(F32)
    kd_all = k_all * (1.0 + (iclr_all - 1.0) * ka_ref[...])
    kkr_all = k_all * kk_ref[...]
    cs_all = _dot_mask(seen_sq, lw_all)
    tot_all = jnp.sum(lw_all, axis=0, keepdims=True)
    pairs = range(RWKV_CPAIRS)
    sl = [slice(p * LANES, (p + 1) * LANES) for p in pairs]
    sq_hi = [_split_bf16(kkr_all[:, s] * kkr_all[:, s]) for s in sl]
    ssq = [jnp.dot(sq_hi[p][0], same_head, preferred_element_type=F32)
           + jnp.dot(sq_hi[p][1], same_head, preferred_element_type=F32) for p in pairs]
    kk = [kkr_all[:, sl[p]] / jnp.maximum(jnp.sqrt(ssq[p]), 1e-12) for p in pairs]
    b_in = [kk[p] * iclr_all[:, sl[p]] for p in pairs]
    cs = [cs_all[:, s] for s in sl]
    tot = [tot_all[:, s] for s in sl]
    e_out = [jnp.exp(-cs[p]) for p in pairs]
    at = [-kk[p] * jnp.exp(cs[p] - lw_all[:, sl[p]]) for p in pairs]
    rt = [r_ref[0, :, sl[p]].astype(F32) * jnp.exp(cs[p]) for p in pairs]
    ar = [jnp.concatenate([at[p], rt[p]], axis=0) for p in pairs]
    g1 = [_mxu(ar[p], bd(b_in[p] * e_out[p]), lane_dims) for p in pairs]
    g2 = [_mxu(ar[p], bd(kd_all[:, sl[p]] * e_out[p]), lane_dims) for p in pairs]
    a_ab = [jnp.where(before, g1[p][:T], 0.0) for p in pairs]
    a_rb = [jnp.where(seen, g1[p][T:], 0.0) for p in pairs]
    a_ak = [jnp.where(before, g2[p][:T], 0.0) for p in pairs]
    a_rk = [jnp.where(seen, g2[p][T:], 0.0) for p in pairs]
    v = [v_ref[0, :, sl[p]].astype(F32) for p in pairs]
    akv = [pp(a_ak[p], v[p]) for p in pairs]
    za = [[None] * NB for _ in pairs]
    zu = [[None] * NB for _ in pairs]
    zero_blk = jnp.zeros((SB, LANES), F32)
    for kpos in range(NB):
        bk = NB - 1 - kpos if rev else kpos
        rows = slice(bk * SB, (bk + 1) * SB)
        done = [(m > bk) if rev else (m < bk) for m in range(NB)]
        cur_a = [at[p][rows] for p in pairs]
        cur_u = [akv[p][rows] for p in pairs]
        if kpos > 0:
            for p in pairs:
                zc_a = jnp.concatenate([za[p][m] if done[m] else zero_blk for m in range(NB)], axis=0)
                zc_u = jnp.concatenate([zu[p][m] if done[m] else zero_blk for m in range(NB)], axis=0)
                off = _mxu(a_ab[p][rows], jnp.concatenate([bd(zc_a), bd(zc_u)], axis=1))
                cur_a[p] = cur_a[p] + off[:, :LANES]
                cur_u[p] = cur_u[p] + off[:, LANES:]
        abc = []
        for p in pairs:
            ablk = a_ab[p][rows]
            picked = jnp.concatenate([jnp.where(col_sb == bk * SB + s, ablk, 0.0) for s in range(SB)], axis=0)
            abc.append(jnp.dot(picked.astype(BF16), same_head, preferred_element_type=F32))
        ha = [[cur_a[p][:8], cur_a[p][8:]] for p in pairs]
        hu = [[cur_u[p][:8], cur_u[p][8:]] for p in pairs]
        for j in range(SB - 1):
            s = SB - 1 - j if rev else j
            src, r8 = s // 8, s % 8
            halves = (0, 1) if (s >= 8) == rev else ((0,) if rev else (1,))
            for p in pairs:
                row_a = ha[p][src][r8:r8 + 1]
                row_u = hu[p][src][r8:r8 + 1]
                for hf in halves:
                    coef = abc[p][s * SB + hf * 8:s * SB + hf * 8 + 8]
                    ha[p][hf] = ha[p][hf] + coef * row_a
                    hu[p][hf] = hu[p][hf] + coef * row_u
        for p in pairs:
            za[p][bk] = jnp.concatenate(ha[p], axis=0)
            zu[p][bk] = jnp.concatenate(hu[p], axis=0)
    a_hat = [jnp.concatenate(za[p], axis=0) for p in pairs]
    u_loc = [jnp.concatenate(zu[p], axis=0) for p in pairs]
    h0 = [h_ref[:, sl[p]] for p in pairs]
    q_hat = [rt[p] + pp(a_rb[p], a_hat[p]) for p in pairs]
    for p in pairs:
        lhs = jnp.concatenate([q_hat[p], a_rb[p], a_rk[p]], axis=1)
        rhs = jnp.concatenate([bd(h0[p]), bd(u_loc[p]), bd(v[p])], axis=0)
        y_ref[0, :, sl[p]] = _mxu(lhs, rhs).astype(y_ref.dtype)
    e_end = [jnp.exp(tot[p] - cs[p]) for p in pairs]
    bh = [b_in[p] * e_end[p] for p in pairs]
    p_end = [_split_bf16(jnp.where(eye, jnp.exp(tot[p]), 0.0)) for p in pairs]
    decay = [jnp.dot(p_end[p][0], same_head, preferred_element_type=F32)
             + jnp.dot(p_end[p][1], same_head, preferred_element_type=F32) for p in pairs]
    corr = [ptp(bh[p], a_hat[p]) for p in pairs]
    gam = [ptp(jnp.concatenate([bh[p], kd_all[:, sl[p]] * e_end[p]], axis=0),
               jnp.concatenate([u_loc[p], v[p]], axis=0)) for p in pairs]
    for p in pairs:
        h_ref[:, sl[p]] = decay[p] * h0[p] + (pp(corr[p], h0[p], split=True) + gam[p])

    @pl.when(c == n_chunks - 1)
    def _():
        hfin_ref[0] = h_ref[...]


def rwkv_direction(rev, main, w_pre, a_pre, w0, a0, k_k, k_a, s0):
    bsz, L, _ = main.shape
    W = RWKV_W
    T, N, H = RWKV_CHUNK, RWKV_HEAD, RWKV_HEADS
    n = L // T
    gw = RWKV_CPAIRS * LANES
    ng = W // gw
    h0 = s0.transpose(0, 3, 1, 2).reshape(bsz, N, W)

    def seq(col0):
        return pl.BlockSpec((1, T, gw), lambda b, g, c: (b, (n - 1 - c) if rev else c, col0 * ng + g))
    vec = pl.BlockSpec((1, gw), lambda b, g, c: (0, g))
    st = pl.BlockSpec((1, N, gw), lambda b, g, c: (b, 0, g))
    y, hfin = pl.pallas_call(
        functools.partial(_rwkv_fs_kernel, n, rev),
        grid=(bsz, ng, n),
        in_specs=[seq(0), seq(1), seq(2), seq(0), seq(0), vec, vec, vec, vec, st],
        out_specs=[seq(0), st],
        out_shape=[jax.ShapeDtypeStruct((bsz, L, W), BF16), jax.ShapeDtypeStruct((bsz, N, W), F32)],
        scratch_shapes=[pltpu.VMEM((N, gw), F32)],
        compiler_params=pltpu.CompilerParams(
            dimension_semantics=("arbitrary",) * 3, vmem_limit_bytes=VMEM_LIMIT),
        name="rwkv_bwd_chunks" if rev else "rwkv_fwd_chunks",
    )(main, main, main, w_pre, a_pre, w0.reshape(1, W), a0.reshape(1, W), k_k.reshape(1, W), k_a.reshape(1, W), h0)
    return y, hfin.reshape(bsz, N, H, N).transpose(0, 2, 3, 1)


def _segsum(x, same_head):
    x1, x2 = _split_bf16(x)
    return (jnp.dot(x1, same_head, preferred_element_type=F32)
            + jnp.dot(x2, same_head, preferred_element_type=F32))


def _rwkv_post_kernel(yf_ref, yb_ref, r_ref, k_ref, v_ref, g_ref, af_ref, ab_ref, a0_ref, ka_ref, rk_ref,
                      lw_ref, lb_ref, o_ref):
    N = RWKV_HEAD
    same_head = ((lax.broadcasted_iota(jnp.int32, (LANES, LANES), 0) < N)
                 == (lax.broadcasted_iota(jnp.int32, (LANES, LANES), 1) < N)).astype(BF16)
    for t in range(o_ref.shape[2] // LANES):
        ls = slice(t * LANES, (t + 1) * LANES)
        wkv = yf_ref[0, :, ls].astype(F32) + yb_ref[0, :, ls].astype(F32)
        mean = _segsum(wkv, same_head) * (1.0 / N)
        cen = wkv - mean
        var = _segsum(cen * cen, same_head) * (1.0 / N)
        ln = cen * lax.rsqrt(var + RWKV_LNX_EPS) * lw_ref[:, ls] + lb_ref[:, ls]
        ka = ka_ref[:, ls]
        k_mix = ((1.0 + (jax.nn.sigmoid(af_ref[0, :, ls] + a0_ref[0:1, ls]) - 1.0) * ka)
                 + (1.0 + (jax.nn.sigmoid(ab_ref[0, :, ls] + a0_ref[1:2, ls]) - 1.0) * ka))
        rk = r_ref[0, :, ls].astype(F32) * k_ref[0, :, ls].astype(F32)
        bonus = _segsum(rk * k_mix * rk_ref[:, ls], same_head) * v_ref[0, :, ls].astype(F32)
        gate = g_ref[0, :, ls].astype(F32)
        o_ref[0, :, ls] = ((ln + bonus) * (gate * jax.nn.sigmoid(gate))).astype(o_ref.dtype)


def rwkv_post(y_f, y_b, main, a_pre_f, a_pre_b, a0, k_a, r_k, lnx_w, lnx_b):
    bsz, L, W = y_f.shape
    tr = _pick(L, (256, 128, 64))
    tw = 1024
    nw = W // tw

    def seq(col0):
        return pl.BlockSpec((1, tr, tw), lambda b, i, j: (b, i, col0 * nw + j))
    vec = pl.BlockSpec((1, tw), lambda b, i, j: (0, j))
    vec2 = pl.BlockSpec((N_DIR, tw), lambda b, i, j: (0, j))
    return pl.pallas_call(
        _rwkv_post_kernel,
        grid=(bsz, L // tr, nw),
        in_specs=[seq(0), seq(0), seq(0), seq(1), seq(2), seq(3), seq(0), seq(0), vec2, vec, vec, vec, vec],
        out_specs=seq(0),
        out_shape=jax.ShapeDtypeStruct((bsz, L, W), BF16),
        compiler_params=pltpu.CompilerParams(
            dimension_semantics=("arbitrary",) * 3, vmem_limit_bytes=VMEM_LIMIT),
        name="rwkv_post",
    )(y_f, y_b, main, main, main, main, a_pre_f, a_pre_b, a0, k_a.reshape(1, W), r_k.reshape(1, W),
      lnx_w.reshape(1, W), lnx_b.reshape(1, W))


def _split_cols(t, sizes):
    offsets, acc = [], 0
    for s in sizes[:-1]:
        acc += s
        offsets.append(acc)
    return jnp.split(t, offsets, axis=-1)


def _adaln_kernel(c_ref, w_ref, b_ref, o_ref):
    cond = c_ref[...]
    act = cond * jax.nn.sigmoid(cond)
    o_ref[...] = _mxu(act, w_ref[0], split=True) + b_ref[0]


def adaln(cond, w, b, layer):
    rows, dm = cond.shape
    n = w.shape[2]
    rp = -(-rows // 8) * 8
    tn = 512
    m = pl.pallas_call(
        _adaln_kernel,
        grid=(n // tn,),
        in_specs=[pl.BlockSpec((rp, dm), lambda j: (0, 0)),
                  pl.BlockSpec((1, dm, tn), lambda j: (layer, 0, j)),
                  pl.BlockSpec((1, 1, tn), lambda j: (layer, 0, j))],
        out_specs=pl.BlockSpec((rp, tn), lambda j: (0, j)),
        out_shape=jax.ShapeDtypeStruct((rp, n), F32),
        compiler_params=pltpu.CompilerParams(dimension_semantics=("arbitrary",), vmem_limit_bytes=VMEM_LIMIT),
        name="adaln",
    )(jnp.pad(cond, ((0, rp - rows), (0, 0))), w, b.reshape(b.shape[0], 1, n))[:rows]
    return jnp.split(m, 3, axis=-1)


def _grid_pos_embed(n_tokens):
    rows = n_tokens // GRID_W
    row_id = jnp.broadcast_to(jnp.arange(rows, dtype=F32)[:, None], (rows, GRID_W)).reshape(-1)
    col_id = jnp.broadcast_to(jnp.arange(GRID_W, dtype=F32)[None, :], (rows, GRID_W)).reshape(-1)
    quarter = D_MODEL // 4
    omega = 1.0 / (POS_BASE ** (jnp.arange(quarter, dtype=F32) / quarter))

    def axis_emb(pos):
        ang = pos[:, None] * omega[None, :]
        return jnp.concatenate([jnp.sin(ang), jnp.cos(ang)], axis=-1)
    return jnp.concatenate([axis_emb(row_id), axis_emb(col_id)], axis=-1)


def _even_mixer(x, x_add, gate, h, s5_re0, s5_im0, gla0, w_in, w_out, s5_ops, glu_w, glu_b, dec_up, dec_b,
                gla_nw, final_nw=None):
    bsz, L, _ = h.shape
    n_main = sum(EVEN_SIZES[:-1])
    main = _mm3(h, w_in, n_main, BF16)
    w_tail = jnp.pad(w_in[:, n_main:], ((0, 0), (0, LANES - N_DIR * GLA_RANK)))
    dec_lr = _mm3(h, w_tail)
    gy, fin_re, fin_im = s5_scan(main[..., :S5_W], s5_ops, s5_re0, s5_im0)
    gy = gy.reshape(bsz * L, S5_W)
    mixed = matmul_glu(gy, glu_w, glu_b, main.reshape(bsz * L, n_main), S5_W, S5_W + GLA_DV_W)
    mixed, fin_gla = gla_mix(main, dec_lr, dec_up, dec_b, gla_nw, gla0, mixed.reshape(bsz, L, -1))
    return matmul_gated_residual(mixed, w_out, x, gate, x_add, final_nw), fin_re, fin_im, fin_gla


def _odd_mixer(x, gate, xs, rwkv0, w_in, w_out, w0, w2, a0, a2, k_k, k_a, r_k, lnx_w, lnx_b, final_nw=None):
    bsz, L, _ = xs.shape
    n_main = sum(ODD_SIZES[:4])
    main = _mm3(xs, w_in, n_main, BF16)
    tail = _mm3(xs, w_in[:, n_main:])
    w_lr, a_lr = _split_cols(tail, ODD_SIZES[4:])
    w_lr = jnp.tanh(w_lr).reshape(bsz, L, N_DIR, RWKV_DECAY_RANK)
    a_lr = a_lr.reshape(bsz, L, N_DIR, RWKV_ICLR_RANK)
    ys, a_pres, finals = [], [], []
    for d in range(N_DIR):
        w_pre = _mm3(w_lr[:, :, d], w2[d])
        a_pre = _mm3(a_lr[:, :, d], a2[d])
        y_d, fin = rwkv_direction(bool(d), main, w_pre, a_pre, w0[d], a0[d], k_k, k_a, rwkv0[:, d])
        ys.append(y_d)
        a_pres.append(a_pre)
        finals.append(fin)
    out = rwkv_post(ys[0], ys[1], main, a_pres[0], a_pres[1], a0, k_a, r_k.reshape(-1), lnx_w, lnx_b)
    return matmul_gated_residual(out, w_out, x, gate, None, final_nw), jnp.stack(finals, axis=1)


def kernel(x_prompt, x_sample, state_s5_re, state_s5_im, state_gla, state_rwkv, c, c_ctx, norm_w, ada_w, ada_b, final_norm_w, e_w_in, e_w_out, s5_lambda_re, s5_lambda_im, s5_log_step, s5_b_re, s5_b_im, s5_c_re, s5_c_im, s5_d, s5_glu_w, s5_glu_b, gla_decay_up, gla_decay_b, gla_norm_w, o_w_in, o_w_out, rwkv_mu, rwkv_w0, rwkv_w2, rwkv_a0, rwkv_a2, rwkv_k_k, rwkv_k_a, rwkv_r_k, rwkv_lnx_w, rwkv_lnx_b):
    bp = x_prompt.shape[0]
    depth = norm_w.shape[0]
    x_ctx = x_prompt
    x_lat, lat_add = x_sample, _grid_pos_embed(x_sample.shape[1])
    z_s5 = jnp.zeros((bp, N_DIR, S5_GROUPS, S5_STATE), F32)
    z_rwkv = jnp.zeros((bp, N_DIR, RWKV_HEADS, RWKV_HEAD, RWKV_HEAD), F32)
    new_s5_re, new_s5_im, new_gla, new_rwkv = [], [], [], []
    n_lat = c.shape[0]
    cond = jnp.concatenate([c, c_ctx[None]], axis=0)
    for i in range(depth):
        j = i // 2
        shift, scale, gate = adaln(cond, ada_w, ada_b, i)
        gt_l, gt_c = gate[:n_lat], jnp.broadcast_to(gate[n_lat:], (bp, D_MODEL))
        mu = rwkv_mu[j] if i % 2 else None
        h_ctx = norm_mod(x_ctx, norm_w[i], scale[n_lat:], shift[n_lat:], mu=mu)
        h_lat = norm_mod(x_lat, norm_w[i], scale[:n_lat], shift[:n_lat], add=lat_add, mu=mu)
        fnw = final_norm_w if i == depth - 1 else None
        if i % 2 == 0:
            s5_ops = s5_operators(s5_lambda_re[j], s5_lambda_im[j], s5_log_step[j], s5_b_re[j], s5_b_im[j],
                                  s5_c_re[j], s5_c_im[j], s5_d[j])
            p = (e_w_in[j], e_w_out[j], s5_ops, s5_glu_w[j], s5_glu_b[j], gla_decay_up[j], gla_decay_b[j],
                 gla_norm_w[j])
            x_ctx, fr, fi, fg = _even_mixer(x_ctx, None, gt_c, h_ctx, z_s5, z_s5, None, *p, final_nw=fnw)
            x_lat, _, _, _ = _even_mixer(x_lat, lat_add, gt_l, h_lat, state_s5_re[:, j], state_s5_im[:, j],
                                         state_gla[:, j], *p, final_nw=fnw)
            lat_add = None
            new_s5_re.append(fr)
            new_s5_im.append(fi)
            new_gla.append(fg)
        else:
            p = (o_w_in[j], o_w_out[j], rwkv_w0[j], rwkv_w2[j], rwkv_a0[j], rwkv_a2[j],
                 rwkv_k_k[j], rwkv_k_a[j], rwkv_r_k[j], rwkv_lnx_w[j], rwkv_lnx_b[j])
            x_ctx, fw = _odd_mixer(x_ctx, gt_c, h_ctx, z_rwkv, *p, final_nw=fnw)
            x_lat, _ = _odd_mixer(x_lat, gt_l, h_lat, state_rwkv[:, j], *p, final_nw=fnw)
            new_rwkv.append(fw)
    if depth == 0:
        x_ctx, x_lat = final_norm(x_ctx, final_norm_w), final_norm(x_lat + lat_add, final_norm_w)
    return (x_ctx, x_lat, jnp.stack(new_s5_re, axis=1), jnp.stack(new_s5_im, axis=1),
            jnp.stack(new_gla, axis=1), jnp.stack(new_rwkv, axis=1))
```

```python
import functools

import jax
import jax.numpy as jnp
from jax import lax
from jax.experimental import pallas as pl
from jax.experimental.pallas import tpu as pltpu

D_MODEL = 2048
GRID_W = 64
POS_BASE = 10000.0
N_DIR = 2
EPS = 1e-6
S5_W = 1024
S5_GROUP_CH = 16
S5_GROUPS = 64
S5_STATE = 64
S5_CHUNK = 16
S5_TILE_GROUPS = 8
GLA_HEADS = 6
GLA_DV = 512
GLA_DK = 256
GLA_DK_W = 1536
GLA_DV_W = 3072
GLA_RANK = 16
GLA_NORMALIZER = 16.0
GLA_CHUNK = 64
GLA_NC = 16
GLA_LOG_DECAY_MIN = -1.0
EVEN_SIZES = (S5_W, S5_W, GLA_DK_W, GLA_DK_W, GLA_DV_W, GLA_DV_W, N_DIR * GLA_RANK)
RWKV_W = 2048
RWKV_HEAD = 64
RWKV_HEADS = 32
RWKV_DECAY_RANK = 96
RWKV_ICLR_RANK = 96
RWKV_LNX_EPS = 64e-5
ODD_SIZES = (RWKV_W, RWKV_W, RWKV_W, RWKV_W, N_DIR * RWKV_DECAY_RANK, N_DIR * RWKV_ICLR_RANK)
RWKV_CHUNK = 64
RWKV_CPAIRS = 16
RWKV_SUB = 16
LANES = 128

VMEM_LIMIT = 48 * 1024 * 1024
HI = lax.Precision.HIGHEST
BF16 = jnp.bfloat16
F32 = jnp.float32


def _mm_kernel(x_ref, w_ref, o_ref):
    o_ref[...] = jnp.dot(x_ref[...], w_ref[...], preferred_element_type=F32).astype(o_ref.dtype)


def _pick(n, prefs):
    for p in prefs:
        if n % p == 0:
            return p
    return n


def matmul(x, w, n_cols=None, out_dtype=F32):
    m, k = x.shape
    n = w.shape[1] if n_cols is None else n_cols
    x = x.astype(BF16)
    w = w.astype(BF16)
    tm = _pick(m, (2048, 1024, 512, 256, 128, 64, 32, 16, 8))
    tn = _pick(n, (1024, 512, 384, 256, 128))
    return pl.pallas_call(
        _mm_kernel,
        grid=(m // tm, n // tn),
        in_specs=[pl.BlockSpec((tm, k), lambda i, j: (i, 0)),
                  pl.BlockSpec((k, tn), lambda i, j: (0, j))],
        out_specs=pl.BlockSpec((tm, tn), lambda i, j: (i, j)),
        out_shape=jax.ShapeDtypeStruct((m, n), out_dtype),
        compiler_params=pltpu.CompilerParams(
            dimension_semantics=("arbitrary", "arbitrary"), vmem_limit_bytes=VMEM_LIMIT),
        name="proj_matmul",
    )(x, w)


def _mm3(h, w, n_cols=None, out_dtype=F32):
    b, l, k = h.shape
    return matmul(h.reshape(b * l, k), w, n_cols, out_dtype).reshape(b, l, -1)


def _mm_residual_kernel(has_add, has_norm, x_ref, w_ref, res_ref, gate_ref, *refs):
    o_ref = refs[-1]
    acc = jnp.dot(x_ref[...], w_ref[...], preferred_element_type=F32)
    res = res_ref[...] + refs[0][...] if has_add else res_ref[...]
    out = res + gate_ref[0] * acc
    if has_norm:
        nw_ref = refs[-2]
        out = out * lax.rsqrt(jnp.mean(out * out, axis=-1, keepdims=True) + EPS) * nw_ref[...]
    o_ref[...] = out


def matmul_gated_residual(x, w, res, gate, res_add=None, final_nw=None):
    bsz, L, k = x.shape
    n = w.shape[1]
    m = bsz * L
    if final_nw is None:
        tm = _pick(L, (1024, 512, 256, 128))
        tn = _pick(n, (1024, 512, 256, 128) if k <= 2048 else (512, 256, 128))
    else:
        tm, tn = _pick(L, (512, 256, 128)), n
    per_b = L // tm
    in_specs = [pl.BlockSpec((tm, k), lambda i, j: (i, 0)),
                pl.BlockSpec((k, tn), lambda i, j: (0, j)),
                pl.BlockSpec((tm, tn), lambda i, j: (i, j)),
                pl.BlockSpec((1, 1, tn), lambda i, j: (i // per_b, 0, j))]
    args = [x.reshape(m, k).astype(BF16), w.astype(BF16), res.reshape(m, n), gate.reshape(bsz, 1, n)]
    if res_add is not None:
        in_specs.append(pl.BlockSpec((tm, tn), lambda i, j: (i % per_b, j)))
        args.append(res_add)
    if final_nw is not None:
        in_specs.append(pl.BlockSpec((1, tn), lambda i, j: (0, j)))
        args.append(final_nw.reshape(1, n))
    out = pl.pallas_call(
        functools.partial(_mm_residual_kernel, res_add is not None, final_nw is not None),
        grid=(m // tm, n // tn),
        in_specs=in_specs,
        out_specs=pl.BlockSpec((tm, tn), lambda i, j: (i, j)),
        out_shape=jax.ShapeDtypeStruct((m, n), F32),
        compiler_params=pltpu.CompilerParams(
            dimension_semantics=("arbitrary", "arbitrary"), vmem_limit_bytes=VMEM_LIMIT),
        name="proj_residual",
    )(*args)
    return out.reshape(bsz, L, n)


def _mm_glu_kernel(x_ref, w_ref, b_ref, g_ref, xt_ref, o_ref):
    acc = jnp.dot(x_ref[...], w_ref[...], preferred_element_type=F32) + b_ref[...]
    gy = xt_ref[...].astype(F32)
    gate = g_ref[...].astype(F32)
    o_ref[...] = (gy * jax.nn.sigmoid(acc) * (gate * jax.nn.sigmoid(gate))).astype(o_ref.dtype)


def matmul_glu(gy, w, b, main, g_col0, n_total):
    m, k = gy.shape
    n = w.shape[1]
    tm = _pick(m, (2048, 1024, 512, 256, 128))
    tn = 512
    return pl.pallas_call(
        _mm_glu_kernel,
        grid=(m // tm, n // tn),
        in_specs=[pl.BlockSpec((tm, k), lambda i, j: (i, 0)),
                  pl.BlockSpec((k, tn), lambda i, j: (0, j)),
                  pl.BlockSpec((1, tn), lambda i, j: (0, j)),
                  pl.BlockSpec((tm, tn), lambda i, j: (i, g_col0 // tn + j)),
                  pl.BlockSpec((tm, tn), lambda i, j: (i, j))],
        out_specs=pl.BlockSpec((tm, tn), lambda i, j: (i, j)),
        out_shape=jax.ShapeDtypeStruct((m, n_total), BF16),
        compiler_params=pltpu.CompilerParams(
            dimension_semantics=("arbitrary", "arbitrary"), vmem_limit_bytes=VMEM_LIMIT),
        name="s5_glu_gate",
    )(gy, w.astype(BF16), b.reshape(1, n), main, gy)


def _norm_mod_kernel(has_add, has_shift, n_row_blocks, x_ref, nw_ref, sc_ref, sh_ref, *refs):
    o_ref = refs[-1]

    def modulated(x):
        inv = lax.rsqrt(jnp.mean(x * x, axis=-1, keepdims=True) + EPS)
        return x * inv * nw_ref[...] * (1.0 + sc_ref[0]) + sh_ref[0]
    x = x_ref[0]
    if has_add:
        x = x + refs[0][...]
    h = modulated(x)
    if has_shift:
        prev_ref, next_ref, mu_ref = refs[0], refs[1], refs[2]
        i = pl.program_id(1)
        tr = h.shape[0]
        row = lax.broadcasted_iota(jnp.int32, h.shape, 0)
        before = jnp.where(i > 0, modulated(prev_ref[0])[7:8], 0.0)
        after = jnp.where(i < n_row_blocks - 1, modulated(next_ref[0])[0:1], 0.0)
        h_prev = jnp.where(row == 0, before, pltpu.roll(h, 1, 0))
        h_next = jnp.where(row == tr - 1, after, pltpu.roll(h, tr - 1, 0))
        h = h + mu_ref[0:1] * (h_prev - h) + mu_ref[1:2] * (h_next - h)
    o_ref[0] = h.astype(o_ref.dtype)


def norm_mod(x, nw, scale, shift, add=None, mu=None):
    assert add is None or mu is None
    bsz, L, dm = x.shape
    tr = _pick(L, (256, 128, 64) if mu is not None else (512, 256, 128, 64))
    nb = scale.shape[0]
    nblk = L // tr
    cond = pl.BlockSpec((1, 1, dm), lambda b, i: (b if nb > 1 else 0, 0, 0))
    in_specs = [pl.BlockSpec((1, tr, dm), lambda b, i: (b, i, 0)), pl.BlockSpec((1, dm), lambda b, i: (0, 0)),
                cond, cond]
    args = [x, nw.reshape(1, dm), scale.reshape(nb, 1, dm), shift.reshape(nb, 1, dm)]
    if add is not None:
        in_specs.append(pl.BlockSpec((tr, dm), lambda b, i: (i, 0)))
        args.append(add)
    if mu is not None:
        r8 = tr // 8
        in_specs += [pl.BlockSpec((1, 8, dm), lambda b, i: (b, jnp.maximum(i * r8 - 1, 0), 0)),
                     pl.BlockSpec((1, 8, dm), lambda b, i: (b, jnp.minimum((i + 1) * r8, L // 8 - 1), 0)),
                     pl.BlockSpec((2, dm), lambda b, i: (0, 0))]
        args += [x, x, mu]
    return pl.pallas_call(
        functools.partial(_norm_mod_kernel, add is not None, mu is not None, nblk),
        grid=(bsz, nblk),
        in_specs=in_specs,
        out_specs=pl.BlockSpec((1, tr, dm), lambda b, i: (b, i, 0)),
        out_shape=jax.ShapeDtypeStruct((bsz, L, dm), BF16),
        compiler_params=pltpu.CompilerParams(
            dimension_semantics=("arbitrary", "arbitrary"), vmem_limit_bytes=VMEM_LIMIT),
        name="norm_mod",
    )(*args)


def _final_norm_kernel(x_ref, nw_ref, o_ref):
    x = x_ref[0]
    o_ref[0] = x * lax.rsqrt(jnp.mean(x * x, axis=-1, keepdims=True) + EPS) * nw_ref[...]


def final_norm(x, nw):
    bsz, L, dm = x.shape
    tr = _pick(L, (256, 128, 64))
    return pl.pallas_call(
        _final_norm_kernel,
        grid=(bsz, L // tr),
        in_specs=[pl.BlockSpec((1, tr, dm), lambda b, i: (b, i, 0)), pl.BlockSpec((1, dm), lambda b, i: (0, 0))],
        out_specs=pl.BlockSpec((1, tr, dm), lambda b, i: (b, i, 0)),
        out_shape=jax.ShapeDtypeStruct((bsz, L, dm), F32),
        compiler_params=pltpu.CompilerParams(
            dimension_semantics=("arbitrary", "arbitrary"), vmem_limit_bytes=VMEM_LIMIT),
        name="final_norm",
    )(x, nw.reshape(1, dm))


def s5_operators(lam_re, lam_im, log_step, b_re, b_im, c_re, c_im, d_skip):
    T = S5_CHUNK
    dt = jnp.exp(log_step)[..., None]
    mag = jnp.exp(lam_re * dt)
    ab_re, ab_im = mag * jnp.cos(lam_im * dt), mag * jnp.sin(lam_im * dt)
    den = lam_re * lam_re + lam_im * lam_im
    f_re = ((ab_re - 1.0) * lam_re + ab_im * lam_im) / den
    f_im = (ab_im * lam_re - (ab_re - 1.0) * lam_im) / den
    bb_re = f_re[..., None] * b_re - f_im[..., None] * b_im
    bb_im = f_re[..., None] * b_im + f_im[..., None] * b_re
    kk = jnp.arange(T + 1, dtype=F32)[:, None, None, None]
    pmag = jnp.exp(kk * (lam_re * dt))
    pr = pmag * jnp.cos(kk * (lam_im * dt))
    pi = pmag * jnp.sin(kk * (lam_im * dt))
    zr = pr[:T, :, :, :, None] * bb_re - pi[:T, :, :, :, None] * bb_im
    zi = pr[:T, :, :, :, None] * bb_im + pi[:T, :, :, :, None] * bb_re
    kern = (jnp.einsum('dghp,kdgpj->kdghj', c_re, zr, precision=HI)
            - jnp.einsum('dghp,kdgpj->kdghj', c_im, zi, precision=HI))
    t_idx = jnp.arange(T)[:, None]
    s_idx = jnp.arange(T)[None, :]
    lag_f = t_idx - s_idx
    lag_b = s_idx - t_idx
    m_f = jnp.where((lag_f >= 0)[:, :, None, None, None], kern[:, 0][jnp.clip(lag_f, 0, T - 1)], 0.0)
    m_b = jnp.where((lag_b >= 0)[:, :, None, None, None], kern[:, 1][jnp.clip(lag_b, 0, T - 1)], 0.0)
    m = m_f + m_b
    eye_t = jnp.eye(T, dtype=F32)[:, :, None, None, None]
    eye_h = jnp.eye(S5_GROUP_CH, dtype=F32)[None, None, None]
    m = m + eye_t * eye_h * d_skip.reshape(S5_GROUPS, S5_GROUP_CH)[None, None, :, :, None]
    g = m.shape[2]
    m_t = m.transpose(2, 1, 4, 0, 3).reshape(g, T * S5_GROUP_CH, T * S5_GROUP_CH)
    pf_r, pf_i = pr[T - 1::-1][:T, 0], pi[T - 1::-1][:T, 0]
    pb_r, pb_i = pr[:T, 1], pi[:T, 1]

    def f_mat(p_r, p_i, d):
        re = p_r[..., None] * bb_re[d][None] - p_i[..., None] * bb_im[d][None]
        im = p_r[..., None] * bb_im[d][None] + p_i[..., None] * bb_re[d][None]
        re = re.transpose(1, 0, 3, 2).reshape(g, T * S5_GROUP_CH, S5_STATE)
        im = im.transpose(1, 0, 3, 2).reshape(g, T * S5_GROUP_CH, S5_STATE)
        return re, im
    ff_re, ff_im = f_mat(pf_r, pf_i, 0)
    fb_re, fb_im = f_mat(pb_r, pb_i, 1)
    a_t = jnp.concatenate([m_t, ff_re, fb_re, ff_im, fb_im], axis=-1)
    ef_r, ef_i = pr[1:T + 1, 0], pi[1:T + 1, 0]
    eb_r, eb_i = pr[T:0:-1, 1], pi[T:0:-1, 1]

    def e_mat(p_r, p_i, d):
        er = c_re[d][None] * p_r[:, :, None, :] - c_im[d][None] * p_i[:, :, None, :]
        ei = -(c_re[d][None] * p_i[:, :, None, :] + c_im[d][None] * p_r[:, :, None, :])
        er = er.transpose(1, 3, 0, 2).reshape(g, S5_STATE, T * S5_GROUP_CH)
        ei = ei.transpose(1, 3, 0, 2).reshape(g, S5_STATE, T * S5_GROUP_CH)
        return er, ei
    efr, efi = e_mat(ef_r, ef_i, 0)
    ebr, ebi = e_mat(eb_r, eb_i, 1)
    e_t = jnp.concatenate([efr, ebr, efi, ebi], axis=1)
    lam_t = jnp.concatenate([pr[T, 0], pr[T, 1], pi[T, 0], pi[T, 1]], axis=-1)[:, None, :]
    return a_t.astype(BF16), e_t.astype(BF16), lam_t


def _s5_kernel(n_steps, bsz, pair, x8_ref, sel_ref, at_ref, et_ref, lam_ref, h0_ref, y_ref, hfin_ref, z_ref,
               hent_ref):
    P = S5_STATE
    ut = jnp.dot(x8_ref[0], sel_ref[0], preferred_element_type=F32).astype(BF16)
    z_ref[...] = jnp.dot(ut, at_ref[0], preferred_element_type=F32)
    lam = lam_ref[0]
    a_re, a_im = lam[:, 0:2 * P], lam[:, 2 * P:4 * P]
    h0 = h0_ref[0]
    h0_re, h0_im = h0[:, 0:2 * P], h0[:, 2 * P:4 * P]
    fwd_lanes = lax.broadcasted_iota(jnp.int32, (bsz, 2 * P), 1) < P
    m_re, m_im = a_re, a_im
    if pair:
        cols, half = z_ref.shape[0], bsz // 2
        g_re, g_im = z_ref[:, 4 * P:6 * P], z_ref[:, 6 * P:8 * P]
        rows = lax.broadcasted_iota(jnp.int32, (cols, 2 * P), 0)
        fwd_all = lax.broadcasted_iota(jnp.int32, (cols, 2 * P), 1) < P

        def neighbour(g):
            up = jnp.where(rows < cols - half, pltpu.roll(g, cols - half, 0), 0.0)
            down = jnp.where(rows >= half, pltpu.roll(g, half, 0), 0.0)
            return jnp.where(fwd_all, up, down)
        z_ref[:, 4 * P:6 * P] = a_re * g_re - a_im * g_im + neighbour(g_re)
        z_ref[:, 6 * P:8 * P] = a_re * g_im + a_im * g_re + neighbour(g_im)
        lo = lax.broadcasted_iota(jnp.int32, (bsz, 2 * P), 0) < half
        ah_re = a_re * h0_re - a_im * h0_im
        ah_im = a_re * h0_im + a_im * h0_re

        def swap(x):
            return pltpu.roll(x, half, 0)
        f_re = h0_re + swap(ah_re + jnp.where(lo, g_re[0:bsz], 0.0))
        f_im = h0_im + swap(ah_im + jnp.where(lo, g_im[0:bsz], 0.0))
        b_re = swap(h0_re) + ah_re + swap(jnp.where(lo, 0.0, g_re[cols - bsz:cols]))
        b_im = swap(h0_im) + ah_im + swap(jnp.where(lo, 0.0, g_im[cols - bsz:cols]))
        h0_re, h0_im = jnp.where(fwd_lanes, f_re, b_re), jnp.where(fwd_lanes, f_im, b_im)
        m_re, m_im = a_re * a_re - a_im * a_im, 2.0 * a_re * a_im

    def step(c, carry):
        h_re, h_im = carry
        rf = pl.ds(pl.multiple_of(c * bsz, 8), bsz)
        rb = pl.ds(pl.multiple_of((n_steps - 1 - c) * bsz, 8), bsz)
        hent_ref[rf, 0:P] = h_re[:, 0:P]
        hent_ref[rb, P:2 * P] = h_re[:, P:2 * P]
        hent_ref[rf, 2 * P:3 * P] = h_im[:, 0:P]
        hent_ref[rb, 3 * P:4 * P] = h_im[:, P:2 * P]
        g_re = jnp.where(fwd_lanes, z_ref[rf, 4 * P:6 * P], z_ref[rb, 4 * P:6 * P])
        g_im = jnp.where(fwd_lanes, z_ref[rf, 6 * P:8 * P], z_ref[rb, 6 * P:8 * P])
        return m_re * h_re - m_im * h_im + g_re, m_re * h_im + m_im * h_re + g_im
    h_re, h_im = lax.fori_loop(0, n_steps, step, (h0_re, h0_im))
    if pair:
        h_re = jnp.where(fwd_lanes, h_re, pltpu.roll(h_re, bsz // 2, 0))
        h_im = jnp.where(fwd_lanes, h_im, pltpu.roll(h_im, bsz // 2, 0))
    hfin_ref[0, :, 0:2 * P] = h_re
    hfin_ref[0, :, 2 * P:4 * P] = h_im
    y = z_ref[:, 0:4 * P] + jnp.dot(hent_ref[...].astype(BF16), et_ref[0], preferred_element_type=F32)
    y_ref[0] = jax.nn.gelu(y).astype(y_ref.dtype)


def _s5_unpack_kernel(yt_ref, selt_ref, o_ref):
    acc = jnp.dot(yt_ref[0], selt_ref[0], preferred_element_type=F32)
    for gl in range(1, S5_TILE_GROUPS):
        acc = acc + jnp.dot(yt_ref[gl], selt_ref[gl], preferred_element_type=F32)
    o_ref[0] = acc.astype(o_ref.dtype)


def s5_scan(u, ops, h0_re, h0_im):
    a_t, e_t, lam_t = ops
    b_real, L, _ = u.shape
    T, G, H, P = S5_CHUNK, S5_GROUPS, S5_GROUP_CH, S5_STATE
    TG = S5_TILE_GROUPS
    n = L // T
    pair = b_real == 4 and n % 2 == 0
    bsz = -(-b_real // 8) * 8
    rpc = b_real if pair else bsz
    n_steps = n // 2 if pair else n
    cols = n * rpc
    x8 = u.reshape(b_real, n, T, G // TG, LANES).transpose(3, 1, 0, 2, 4).astype(BF16)
    x8 = jnp.pad(x8, ((0, 0), (0, 0), (0, rpc - b_real), (0, 0), (0, 0))).reshape(G // TG, cols, T * LANES)
    src = jnp.arange(T * LANES)
    dst = jnp.arange(T * H)
    sel = ((src[None, :, None] // LANES == dst[None, None, :] // H)
           & (src[None, :, None] % H == dst[None, None, :] % H)
           & ((src[None, :, None] % LANES) // H == jnp.arange(TG)[:, None, None])).astype(BF16)
    h0 = jnp.concatenate([h0_re[:, 0], h0_re[:, 1], h0_im[:, 0], h0_im[:, 1]], axis=-1)
    h0 = jnp.pad(h0.transpose(1, 0, 2), ((0, 0), (0, bsz - b_real), (0, 0)))
    yt, hfin = pl.pallas_call(
        functools.partial(_s5_kernel, n_steps, bsz, pair),
        grid=(G,),
        in_specs=[pl.BlockSpec((1, cols, T * LANES), lambda g: (g // TG, 0, 0)),
                  pl.BlockSpec((1, T * LANES, T * H), lambda g: (g % TG, 0, 0)),
                  pl.BlockSpec((1, T * H, 8 * P), lambda g: (g, 0, 0)),
                  pl.BlockSpec((1, 4 * P, T * H), lambda g: (g, 0, 0)),
                  pl.BlockSpec((1, 1, 4 * P), lambda g: (g, 0, 0)),
                  pl.BlockSpec((1, bsz, 4 * P), lambda g: (g, 0, 0))],
        out_specs=[pl.BlockSpec((1, cols, T * H), lambda g: (g, 0, 0)),
                   pl.BlockSpec((1, bsz, 4 * P), lambda g: (g, 0, 0))],
        out_shape=[jax.ShapeDtypeStruct((G, cols, T * H), BF16),
                   jax.ShapeDtypeStruct((G, bsz, 4 * P), F32)],
        scratch_shapes=[pltpu.VMEM((cols, 8 * P), F32), pltpu.VMEM((cols, 4 * P), F32)],
        compiler_params=pltpu.CompilerParams(dimension_semantics=("arbitrary",), vmem_limit_bytes=VMEM_LIMIT),
        name="s5_chunk_scan",
    )(x8, sel, a_t, e_t, lam_t, h0)
    tr = _pick(cols, (1024, 512, 256, 128))
    y8 = pl.pallas_call(
        _s5_unpack_kernel,
        grid=(G // TG, cols // tr),
        in_specs=[pl.BlockSpec((TG, tr, T * H), lambda t, i: (t, i, 0)),
                  pl.BlockSpec((TG, T * H, T * LANES), lambda t, i: (0, 0, 0))],
        out_specs=pl.BlockSpec((1, tr, T * LANES), lambda t, i: (t, i, 0)),
        out_shape=jax.ShapeDtypeStruct((G // TG, cols, T * LANES), BF16),
        compiler_params=pltpu.CompilerParams(
            dimension_semantics=("arbitrary", "arbitrary"), vmem_limit_bytes=VMEM_LIMIT),
        name="s5_unpack",
    )(yt, sel.transpose(0, 2, 1))
    y = y8.reshape(G // TG, n, rpc, T, LANES)[:, :, :b_real].transpose(2, 1, 3, 0, 4).reshape(b_real, L, G * H)
    hfin = hfin[:, :b_real].transpose(1, 0, 2)
    fin_re = jnp.stack([hfin[..., 0:P], hfin[..., P:2 * P]], axis=1)
    fin_im = jnp.stack([hfin[..., 2 * P:3 * P], hfin[..., 3 * P:4 * P]], axis=1)
    return y, fin_re, fin_im


def _dot_t(a, b):
    return lax.dot_general(a, b, (((1,), (1,)), ((), ())), preferred_element_type=F32)


def _dot_mask(mask_bf16, x, x_rows_to_sublanes=False):
    def d(b):
        if x_rows_to_sublanes:
            return lax.dot_general(b, mask_bf16, (((0,), (0,)), ((), ())), preferred_element_type=F32)
        return jnp.dot(mask_bf16, b, preferred_element_type=F32)
    x1 = x.astype(BF16)
    r1 = x - x1.astype(F32)
    x2 = r1.astype(BF16)
    x3 = (r1 - x2.astype(F32)).astype(BF16)
    return d(x1) + (d(x2) + d(x3))


def _gla_block_kernel(n_blocks, NC, has_s0, q_ref, k_ref, v_ref, g_ref, lr_ref, up_ref, db_ref, nw_ref, dst_ref,
                      *refs):
    s0_ref = refs[0] if has_s0 else None
    out_ref, sfin_ref, s_ref, of_ref, qd_ref, ov_ref, kv_ref, dc_ref = refs[1:] if has_s0 else refs
    C = GLA_CHUNK
    R = C * NC
    d = pl.program_id(2)
    c = pl.program_id(3)
    bidx = jnp.where(d == 0, c, n_blocks - 1 - c)

    @pl.when(c == 0)
    def _():
        s_ref[...] = s0_ref[0, 0, 0] if has_s0 else jnp.zeros_like(s_ref)

    z = _mxu(lr_ref[0], up_ref[0], split=True) + db_ref[0]
    gc = jnp.maximum(jax.nn.log_sigmoid(z) * (1.0 / GLA_NORMALIZER), GLA_LOG_DECAY_MIN)
    row_c = lax.broadcasted_iota(jnp.int32, (C, C), 0)
    col_c = lax.broadcasted_iota(jnp.int32, (C, C), 1)
    seen_c = jnp.where(d == 0, row_c - col_c, col_c - row_c) >= 0
    seen_bf = seen_c.astype(BF16)
    rs = [slice(i * C, (i + 1) * C) for i in range(NC)]
    bcum_c = [_dot_mask(seen_bf, gc[r]) for r in rs]
    btot_c = [jnp.broadcast_to(jnp.where(d == 0, b[C - 1:C], b[0:1]), (C, GLA_DK)) for b in bcum_c]
    bcum = jnp.concatenate(bcum_c, axis=0)
    btot = jnp.concatenate(btot_c, axis=0)
    q_dec = (q_ref[0].astype(F32) * (GLA_DK ** -0.5) * jnp.exp(bcum)).astype(BF16)
    k = k_ref[0].astype(F32)
    k_inv = (k * jnp.exp(-bcum)).astype(BF16)
    k_end = (k * jnp.exp(btot - bcum)).astype(BF16)
    v = v_ref[0].astype(BF16)
    ones_c = jnp.ones((C, LANES), BF16)
    qd_ref[...] = q_dec.reshape(NC, C, GLA_DK)
    att = [jnp.where(seen_c, _dot_t(q_dec[r], k_inv[r]), 0.0).astype(BF16) for r in rs]
    for i in range(NC):
        kv_ref[i] = lax.dot_general(k_end[rs[i]], v[rs[i]], (((0,), (0,)), ((), ())), preferred_element_type=F32)
    for i in range(NC):
        ov_ref[i] = jnp.dot(att[i], v[rs[i]], preferred_element_type=F32)
    for i in range(NC):
        dc_ref[i] = _dot_mask(ones_c, gc[rs[i]], x_rows_to_sublanes=True)

    for i in range(NC):
        ci = jnp.where(d == 0, i, NC - 1 - i)
        s_old = s_ref[...]
        ov_ref[ci] = ov_ref[ci] + jnp.dot(qd_ref[ci], s_old.astype(BF16), preferred_element_type=F32)
        s_ref[...] = jnp.exp(dc_ref[ci][:, 0:1]) * s_old + kv_ref[ci]
    rows = pl.ds(pl.multiple_of(bidx * R, R), R)

    @pl.when(d == 0)
    def _():
        of_ref[rows, :] = ov_ref[...].reshape(R, GLA_DV)

    @pl.when(d == 1)
    def _():
        tot = of_ref[rows, :] + ov_ref[...].reshape(R, GLA_DV)
        nrm = tot * lax.rsqrt(jnp.mean(tot * tot, axis=-1, keepdims=True) + EPS) * nw_ref[0]
        gate = g_ref[0].astype(F32)
        out_ref[0] = (nrm * (gate * jax.nn.sigmoid(gate))).astype(out_ref.dtype)

    @pl.when(c == n_blocks - 1)
    def _():
        sfin_ref[0, 0, 0] = s_ref[...]


def gla_mix(main, dec_lr, dec_up, dec_b, gla_nw, s0, dst):
    bsz, L, _ = main.shape
    H, DK, DV = GLA_HEADS, GLA_DK, GLA_DV
    nc = min(GLA_NC, L // GLA_CHUNK)
    C = GLA_CHUNK * nc
    n = L // C
    q_blk = sum(EVEN_SIZES[:2]) // DK
    k_blk = sum(EVEN_SIZES[:3]) // DK
    v_blk = sum(EVEN_SIZES[:4]) // DV
    g_blk = sum(EVEN_SIZES[:5]) // DV
    up = jnp.zeros((N_DIR, LANES, GLA_DK_W), F32)
    for d in range(N_DIR):
        up = up.at[d, d * GLA_RANK:(d + 1) * GLA_RANK].set(dec_up[d])
    db = dec_b.reshape(N_DIR, 1, GLA_DK_W)
    nw = gla_nw.reshape(1, GLA_DV_W)

    def chunk(d, c):
        return c + d * (n - 1 - 2 * c)

    def out_chunk(d, c):
        return (n - 1) - d * c
    state = pl.BlockSpec((1, 1, 1, DK, DV), lambda b, h, d, c: (b, d, h, 0, 0))
    has_s0 = s0 is not None
    out, sfin = pl.pallas_call(
        functools.partial(_gla_block_kernel, n, nc, has_s0),
        grid=(bsz, H, N_DIR, n),
        in_specs=[pl.BlockSpec((1, C, DK), lambda b, h, d, c: (b, chunk(d, c), q_blk + h)),
                  pl.BlockSpec((1, C, DK), lambda b, h, d, c: (b, chunk(d, c), k_blk + h)),
                  pl.BlockSpec((1, C, DV), lambda b, h, d, c: (b, chunk(d, c), v_blk + h)),
                  pl.BlockSpec((1, C, DV), lambda b, h, d, c: (b, chunk(d, c), g_blk + h)),
                  pl.BlockSpec((1, C, LANES), lambda b, h, d, c: (b, chunk(d, c), 0)),
                  pl.BlockSpec((1, LANES, DK), lambda b, h, d, c: (d, 0, h)),
                  pl.BlockSpec((1, 1, DK), lambda b, h, d, c: (d, 0, h)),
                  pl.BlockSpec((1, DV), lambda b, h, d, c: (0, h)),
                  pl.BlockSpec(memory_space=pl.ANY)] + ([state] if has_s0 else []),
        input_output_aliases={8: 0},
        out_specs=[pl.BlockSpec((1, C, DV), lambda b, h, d, c: (b, out_chunk(d, c), S5_W // DV + h)), state],
        out_shape=[jax.ShapeDtypeStruct(dst.shape, BF16),
                   jax.ShapeDtypeStruct((bsz, N_DIR, H, DK, DV), F32)],
        scratch_shapes=[pltpu.VMEM((DK, DV), F32), pltpu.VMEM((L, DV), F32),
                        pltpu.VMEM((nc, GLA_CHUNK, DK), BF16), pltpu.VMEM((nc, GLA_CHUNK, DV), F32),
                        pltpu.VMEM((nc, DK, DV), F32), pltpu.VMEM((nc, DK, LANES), F32)],
        compiler_params=pltpu.CompilerParams(
            dimension_semantics=("arbitrary",) * 4, vmem_limit_bytes=VMEM_LIMIT),
        name="gla_chunk_scan",
    )(main, main, main, main, dec_lr, up, db, nw, dst, *([s0] if has_s0 else []))
    return out, sfin


def _split_bf16(x):
    hi = x.astype(BF16)
    return hi, (x - hi.astype(F32)).astype(BF16)


def _mxu(x, y, dims=(((1,), (0,)), ((), ())), split=False):
    def d(a, b):
        return lax.dot_general(a, b, dims, preferred_element_type=F32)
    if not split:
        return d(x.astype(BF16), y.astype(BF16))
    xh, xl = _split_bf16(x)
    yh, yl = _split_bf16(y)
    return d(xh, yh) + (d(xh, yl) + d(xl, yh))


def _rwkv_fs_kernel(n_chunks, rev, r_ref, k_ref, v_ref, wp_ref, ap_ref, w0_ref, a0_ref, kk_ref, ka_ref, h0_ref,
                    y_ref, hfin_ref, h_ref):
    T, N, SB = RWKV_CHUNK, RWKV_HEAD, RWKV_SUB
    NB = T // SB
    c = pl.program_id(2)

    @pl.when(c == 0)
    def _():
        h_ref[...] = h0_ref[0]

    lane = lax.broadcasted_iota(jnp.int32, (T, LANES), 1)
    row = lax.broadcasted_iota(jnp.int32, (T, LANES), 0)
    lo = lane < N
    col = lane % N
    order = (col - row) if rev else (row - col)
    seen = order >= 0
    before = order > 0
    eye = row == col
    sq_r = lax.broadcasted_iota(jnp.int32, (T, T), 0)
    sq_c = lax.broadcasted_iota(jnp.int32, (T, T), 1)
    seen_sq = (((sq_c - sq_r) if rev else (sq_r - sq_c)) >= 0).astype(BF16)
    same_head = ((lax.broadcasted_iota(jnp.int32, (LANES, LANES), 0) < N)
                 == (lax.broadcasted_iota(jnp.int32, (LANES, LANES), 1) < N)).astype(BF16)
    col_sb = lax.broadcasted_iota(jnp.int32, (SB, LANES), 1) % N
    row_dims = (((0,), (0,)), ((), ()))
    lane_dims = (((1,), (1,)), ((), ()))

    def bd(x):
        return jnp.concatenate([jnp.where(lo, x, 0.0), jnp.where(lo, 0.0, x)], axis=0)

    def pp(x, y, split=False):
        return _mxu(x, bd(y), split=split)

    def ptp(x, y):
        full = _mxu(x, y, row_dims)
        return jnp.where(lo, full[:N], full[N:])

    w_log = -jax.nn.softplus(-(wp_ref[0] + w0_ref[...])) - 0.5
    lw_all = -jnp.exp(w_log)
    iclr_all = jax.nn.sigmoid(ap_ref[0] + a0_ref[...])
    k_all = k_ref[0].astype(F32)
    kd_all = k_all * (1.0 + (iclr_all - 1.0) * ka_ref[...])
    kkr_all = k_all * kk_ref[...]
    cs_all = _dot_mask(seen_sq, lw_all)
    tot_all = jnp.sum(lw_all, axis=0, keepdims=True)
    pairs = range(RWKV_CPAIRS)
    sl = [slice(p * LANES, (p + 1) * LANES) for p in pairs]
    sq_hi = [_split_bf16(kkr_all[:, s] * kkr_all[:, s]) for s in sl]
    ssq = [jnp.dot(sq_hi[p][0], same_head, preferred_element_type=F32)
           + jnp.dot(sq_hi[p][1], same_head, preferred_element_type=F32) for p in pairs]
    kk = [kkr_all[:, sl[p]] / jnp.maximum(jnp.sqrt(ssq[p]), 1e-12) for p in pairs]
    b_in = [kk[p] * iclr_all[:, sl[p]] for p in pairs]
    cs = [cs_all[:, s] for s in sl]
    tot = [tot_all[:, s] for s in sl]
    e_out = [jnp.exp(-cs[p]) for p in pairs]
    at = [-kk[p] * jnp.exp(cs[p] - lw_all[:, sl[p]]) for p in pairs]
    rt = [r_ref[0, :, sl[p]].astype(F32) * jnp.exp(cs[p]) for p in pairs]
    ar = [jnp.concatenate([at[p], rt[p]], axis=0) for p in pairs]
    g1 = [_mxu(ar[p], bd(b_in[p] * e_out[p]), lane_dims) for p in pairs]
    g2 = [_mxu(ar[p], bd(kd_all[:, sl[p]] * e_out[p]), lane_dims) for p in pairs]
    a_ab = [jnp.where(before, g1[p][:T], 0.0) for p in pairs]
    a_rb = [jnp.where(seen, g1[p][T:], 0.0) for p in pairs]
    a_ak = [jnp.where(before, g2[p][:T], 0.0) for p in pairs]
    a_rk = [jnp.where(seen, g2[p][T:], 0.0) for p in pairs]
    v = [v_ref[0, :, sl[p]].astype(F32) for p in pairs]
    akv = [pp(a_ak[p], v[p]) for p in pairs]
    za = [[None] * NB for _ in pairs]
    zu = [[None] * NB for _ in pairs]
    zero_blk = jnp.zeros((SB, LANES), F32)
    for kpos in range(NB):
        bk = NB - 1 - kpos if rev else kpos
        rows = slice(bk * SB, (bk + 1) * SB)
        done = [(m > bk) if rev else (m < bk) for m in range(NB)]
        cur_a = [at[p][rows] for p in pairs]
        cur_u = [akv[p][rows] for p in pairs]
        if kpos > 0:
            for p in pairs:
                zc_a = jnp.concatenate([za[p][m] if done[m] else zero_blk for m in range(NB)], axis=0)
                zc_u = jnp.concatenate([zu[p][m] if done[m] else zero_blk for m in range(NB)], axis=0)
                off = _mxu(a_ab[p][rows], jnp.concatenate([bd(zc_a), bd(zc_u)], axis=1))
                cur_a[p] = cur_a[p] + off[:, :LANES]
                cur_u[p] = cur_u[p] + off[:, LANES:]
        abc = []
        for p in pairs:
            ablk = a_ab[p][rows]
            picked = jnp.concatenate([jnp.where(col_sb == bk * SB + s, ablk, 0.0) for s in range(SB)], axis=0)
            abc.append(jnp.dot(picked.astype(BF16), same_head, preferred_element_type=F32))
        ha = [[cur_a[p][:8], cur_a[p][8:]] for p in pairs]
        hu = [[cur_u[p][:8], cur_u[p][8:]] for p in pairs]
        for j in range(SB - 1):
            s = SB - 1 - j if rev else j
            src, r8 = s // 8, s % 8
            halves = (0, 1) if (s >= 8) == rev else ((0,) if rev else (1,))
            for p in pairs:
                row_a = ha[p][src][r8:r8 + 1]
                row_u = hu[p][src][r8:r8 + 1]
                for hf in halves:
                    coef = abc[p][s * SB + hf * 8:s * SB + hf * 8 + 8]
                    ha[p][hf] = ha[p][hf] + coef * row_a
                    hu[p][hf] = hu[p][hf] + coef * row_u
        for p in pairs:
            za[p][bk] = jnp.concatenate(ha[p], axis=0)
            zu[p][bk] = jnp.concatenate(hu[p], axis=0)
    a_hat = [jnp.concatenate(za[p], axis=0) for p in pairs]
    u_loc = [jnp.concatenate(zu[p], axis=0) for p in pairs]
    h0 = [h_ref[:, sl[p]] for p in pairs]
    q_hat = [rt[p] + pp(a_rb[p], a_hat[p]) for p in pairs]
    for p in pairs:
        lhs = jnp.concatenate([q_hat[p], a_rb[p], a_rk[p]], axis=1)
        rhs = jnp.concatenate([bd(h0[p]), bd(u_loc[p]), bd(v[p])], axis=0)
        y_ref[0, :, sl[p]] = _mxu(lhs, rhs).astype(y_ref.dtype)
    e_end = [jnp.exp(tot[p] - cs[p]) for p in pairs]
    bh = [b_in[p] * e_end[p] for p in pairs]
    p_end = [_split_bf16(jnp.where(eye, jnp.exp(tot[p]), 0.0)) for p in pairs]
    decay = [jnp.dot(p_end[p][0], same_head, preferred_element_type=F32)
             + jnp.dot(p_end[p][1], same_head, preferred_element_type=F32) for p in pairs]
    corr = [ptp(bh[p], a_hat[p]) for p in pairs]
    gam = [ptp(jnp.concatenate([bh[p], kd_all[:, sl[p]] * e_end[p]], axis=0),
               jnp.concatenate([u_loc[p], v[p]], axis=0)) for p in pairs]
    for p in pairs:
        h_ref[:, sl[p]] = decay[p] * h0[p] + (pp(corr[p], h0[p], split=True) + gam[p])

    @pl.when(c == n_chunks - 1)
    def _():
        hfin_ref[0] = h_ref[...]


def rwkv_direction(rev, main, w_pre, a_pre, w0, a0, k_k, k_a, s0):
    bsz, L, _ = main.shape
    W = RWKV_W
    T, N, H = RWKV_CHUNK, RWKV_HEAD, RWKV_HEADS
    n = L // T
    gw = RWKV_CPAIRS * LANES
    ng = W // gw
    h0 = s0.transpose(0, 3, 1, 2).reshape(bsz, N, W)

    def seq(col0):
        return pl.BlockSpec((1, T, gw), lambda b, g, c: (b, (n - 1 - c) if rev else c, col0 * ng + g))
    vec = pl.BlockSpec((1, gw), lambda b, g, c: (0, g))
    st = pl.BlockSpec((1, N, gw), lambda b, g, c: (b, 0, g))
    y, hfin = pl.pallas_call(
        functools.partial(_rwkv_fs_kernel, n, rev),
        grid=(bsz, ng, n),
        in_specs=[seq(0), seq(1), seq(2), seq(0), seq(0), vec, vec, vec, vec, st],
        out_specs=[seq(0), st],
        out_shape=[jax.ShapeDtypeStruct((bsz, L, W), BF16), jax.ShapeDtypeStruct((bsz, N, W), F32)],
        scratch_shapes=[pltpu.VMEM((N, gw), F32)],
        compiler_params=pltpu.CompilerParams(
            dimension_semantics=("arbitrary",) * 3, vmem_limit_bytes=VMEM_LIMIT),
        name="rwkv_bwd_chunks" if rev else "rwkv_fwd_chunks",
    )(main, main, main, w_pre, a_pre, w0.reshape(1, W), a0.reshape(1, W), k_k.reshape(1, W), k_a.reshape(1, W), h0)
    return y, hfin.reshape(bsz, N, H, N).transpose(0, 2, 3, 1)


def _segsum(x, same_head):
    x1, x2 = _split_bf16(x)
    return (jnp.dot(x1, same_head, preferred_element_type=F32)
            + jnp.dot(x2, same_head, preferred_element_type=F32))


def _rwkv_post_kernel(yf_ref, yb_ref, r_ref, k_ref, v_ref, g_ref, af_ref, ab_ref, a0_ref, ka_ref, rk_ref,
                      lw_ref, lb_ref, o_ref):
    N = RWKV_HEAD
    same_head = ((lax.broadcasted_iota(jnp.int32, (LANES, LANES), 0) < N)
                 == (lax.broadcasted_iota(jnp.int32, (LANES, LANES), 1) < N)).astype(BF16)
    for t in range(o_ref.shape[2] // LANES):
        ls = slice(t * LANES, (t + 1) * LANES)
        wkv = yf_ref[0, :, ls].astype(F32) + yb_ref[0, :, ls].astype(F32)
        mean = _segsum(wkv, same_head) * (1.0 / N)
        cen = wkv - mean
        var = _segsum(cen * cen, same_head) * (1.0 / N)
        ln = cen * lax.rsqrt(var + RWKV_LNX_EPS) * lw_ref[:, ls] + lb_ref[:, ls]
        ka = ka_ref[:, ls]
        k_mix = ((1.0 + (jax.nn.sigmoid(af_ref[0, :, ls] + a0_ref[0:1, ls]) - 1.0) * ka)
                 + (1.0 + (jax.nn.sigmoid(ab_ref[0, :, ls] + a0_ref[1:2, ls]) - 1.0) * ka))
        rk = r_ref[0, :, ls].astype(F32) * k_ref[0, :, ls].astype(F32)
        bonus = _segsum(rk * k_mix * rk_ref[:, ls], same_head) * v_ref[0, :, ls].astype(F32)
        gate = g_ref[0, :, ls].astype(F32)
        o_ref[0, :, ls] = ((ln + bonus) * (gate * jax.nn.sigmoid(gate))).astype(o_ref.dtype)


def rwkv_post(y_f, y_b, main, a_pre_f, a_pre_b, a0, k_a, r_k, lnx_w, lnx_b):
    bsz, L, W = y_f.shape
    tr = _pick(L, (512, 256, 128, 64))
    tw = 1024
    nw = W // tw

    def seq(col0):
        return pl.BlockSpec((1, tr, tw), lambda b, i, j: (b, i, col0 * nw + j))
    vec = pl.BlockSpec((1, tw), lambda b, i, j: (0, j))
    vec2 = pl.BlockSpec((N_DIR, tw), lambda b, i, j: (0, j))
    return pl.pallas_call(
        _rwkv_post_kernel,
        grid=(bsz, L // tr, nw),
        in_specs=[seq(0), seq(0), seq(0), seq(1), seq(2), seq(3), seq(0), seq(0), vec2, vec, vec, vec, vec],
        out_specs=seq(0),
        out_shape=jax.ShapeDtypeStruct((bsz, L, W), BF16),
        compiler_params=pltpu.CompilerParams(
            dimension_semantics=("arbitrary",) * 3, vmem_limit_bytes=VMEM_LIMIT),
        name="rwkv_post",
    )(y_f, y_b, main, main, main, main, a_pre_f, a_pre_b, a0, k_a.reshape(1, W), r_k.reshape(1, W),
      lnx_w.reshape(1, W), lnx_b.reshape(1, W))


def _split_cols(t, sizes):
    offsets, acc = [], 0
    for s in sizes[:-1]:
        acc += s
        offsets.append(acc)
    return jnp.split(t, offsets, axis=-1)


def _adaln_kernel(c_ref, w_ref, b_ref, o_ref):
    cond = c_ref[...]
    act = cond * jax.nn.sigmoid(cond)
    o_ref[...] = _mxu(act, w_ref[0], split=True) + b_ref[0]


def adaln(cond, w, b, layer):
    rows, dm = cond.shape
    n = w.shape[2]
    rp = -(-rows // 8) * 8
    tn = 512
    m = pl.pallas_call(
        _adaln_kernel,
        grid=(n // tn,),
        in_specs=[pl.BlockSpec((rp, dm), lambda j: (0, 0)),
                  pl.BlockSpec((1, dm, tn), lambda j: (layer, 0, j)),
                  pl.BlockSpec((1, 1, tn), lambda j: (layer, 0, j))],
        out_specs=pl.BlockSpec((rp, tn), lambda j: (0, j)),
        out_shape=jax.ShapeDtypeStruct((rp, n), F32),
        compiler_params=pltpu.CompilerParams(dimension_semantics=("arbitrary",), vmem_limit_bytes=VMEM_LIMIT),
        name="adaln",
    )(jnp.pad(cond, ((0, rp - rows), (0, 0))), w, b.reshape(b.shape[0], 1, n))[:rows]
    return jnp.split(m, 3, axis=-1)


def _grid_pos_embed(n_tokens):
    rows = n_tokens // GRID_W
    row_id = jnp.broadcast_to(jnp.arange(rows, dtype=F32)[:, None], (rows, GRID_W)).reshape(-1)
    col_id = jnp.broadcast_to(jnp.arange(GRID_W, dtype=F32)[None, :], (rows, GRID_W)).reshape(-1)
    quarter = D_MODEL // 4
    omega = 1.0 / (POS_BASE ** (jnp.arange(quarter, dtype=F32) / quarter))

    def axis_emb(pos):
        ang = pos[:, None] * omega[None, :]
        return jnp.concatenate([jnp.sin(ang), jnp.cos(ang)], axis=-1)
    return jnp.concatenate([axis_emb(row_id), axis_emb(col_id)], axis=-1)


def _even_mixer(x, x_add, gate, h, s5_re0, s5_im0, gla0, w_in, w_out, s5_ops, glu_w, glu_b, dec_up, dec_b,
                gla_nw, final_nw=None):
    bsz, L, _ = h.shape
    n_main = sum(EVEN_SIZES[:-1])
    main = _mm3(h, w_in, n_main, BF16)
    w_tail = jnp.pad(w_in[:, n_main:], ((0, 0), (0, LANES - N_DIR * GLA_RANK)))
    dec_lr = _mm3(h, w_tail)
    gy, fin_re, fin_im = s5_scan(main[..., :S5_W], s5_ops, s5_re0, s5_im0)
    gy = gy.reshape(bsz * L, S5_W)
    mixed = matmul_glu(gy, glu_w, glu_b, main.reshape(bsz * L, n_main), S5_W, S5_W + GLA_DV_W)
    mixed, fin_gla = gla_mix(main, dec_lr, dec_up, dec_b, gla_nw, gla0, mixed.reshape(bsz, L, -1))
    return matmul_gated_residual(mixed, w_out, x, gate, x_add, final_nw), fin_re, fin_im, fin_gla


def _odd_mixer(x, gate, xs, rwkv0, w_in, w_out, w0, w2, a0, a2, k_k, k_a, r_k, lnx_w, lnx_b, final_nw=None):
    bsz, L, _ = xs.shape
    n_main = sum(ODD_SIZES[:4])
    main = _mm3(xs, w_in, n_main, BF16)
    tail = _mm3(xs, w_in[:, n_main:])
    w_lr, a_lr = _split_cols(tail, ODD_SIZES[4:])
    w_lr = jnp.tanh(w_lr).reshape(bsz, L, N_DIR, RWKV_DECAY_RANK)
    a_lr = a_lr.reshape(bsz, L, N_DIR, RWKV_ICLR_RANK)
    ys, a_pres, finals = [], [], []
    for d in range(N_DIR):
        w_pre = _mm3(w_lr[:, :, d], w2[d])
        a_pre = _mm3(a_lr[:, :, d], a2[d])
        y_d, fin = rwkv_direction(bool(d), main, w_pre, a_pre, w0[d], a0[d], k_k, k_a, rwkv0[:, d])
        ys.append(y_d)
        a_pres.append(a_pre)
        finals.append(fin)
    out = rwkv_post(ys[0], ys[1], main, a_pres[0], a_pres[1], a0, k_a, r_k.reshape(-1), lnx_w, lnx_b)
    return matmul_gated_residual(out, w_out, x, gate, None, final_nw), jnp.stack(finals, axis=1)


def kernel(x_prompt, x_sample, state_s5_re, state_s5_im, state_gla, state_rwkv, c, c_ctx, norm_w, ada_w, ada_b, final_norm_w, e_w_in, e_w_out, s5_lambda_re, s5_lambda_im, s5_log_step, s5_b_re, s5_b_im, s5_c_re, s5_c_im, s5_d, s5_glu_w, s5_glu_b, gla_decay_up, gla_decay_b, gla_norm_w, o_w_in, o_w_out, rwkv_mu, rwkv_w0, rwkv_w2, rwkv_a0, rwkv_a2, rwkv_k_k, rwkv_k_a, rwkv_r_k, rwkv_lnx_w, rwkv_lnx_b):
    bp = x_prompt.shape[0]
    depth = norm_w.shape[0]
    x_ctx = x_prompt
    x_lat, lat_add = x_sample, _grid_pos_embed(x_sample.shape[1])
    z_s5 = jnp.zeros((bp, N_DIR, S5_GROUPS, S5_STATE), F32)
    z_rwkv = jnp.zeros((bp, N_DIR, RWKV_HEADS, RWKV_HEAD, RWKV_HEAD), F32)
    new_s5_re, new_s5_im, new_gla, new_rwkv = [], [], [], []
    n_lat = c.shape[0]
    cond = jnp.concatenate([c, c_ctx[None]], axis=0)
    for i in range(depth):
        j = i // 2
        shift, scale, gate = adaln(cond, ada_w, ada_b, i)
        gt_l, gt_c = gate[:n_lat], jnp.broadcast_to(gate[n_lat:], (bp, D_MODEL))
        mu = rwkv_mu[j] if i % 2 else None
        h_ctx = norm_mod(x_ctx, norm_w[i], scale[n_lat:], shift[n_lat:], mu=mu)
        h_lat = norm_mod(x_lat, norm_w[i], scale[:n_lat], shift[:n_lat], add=lat_add, mu=mu)
        fnw = final_norm_w if i == depth - 1 else None
        if i % 2 == 0:
            s5_ops = s5_operators(s5_lambda_re[j], s5_lambda_im[j], s5_log_step[j], s5_b_re[j], s5_b_im[j],
                                  s5_c_re[j], s5_c_im[j], s5_d[j])
            p = (e_w_in[j], e_w_out[j], s5_ops, s5_glu_w[j], s5_glu_b[j], gla_decay_up[j], gla_decay_b[j],
                 gla_norm_w[j])
            x_ctx, fr, fi, fg = _even_mixer(x_ctx, None, gt_c, h_ctx, z_s5, z_s5, None, *p, final_nw=fnw)
            x_lat, _, _, _ = _even_mixer(x_lat, lat_add, gt_l, h_lat, state_s5_re[:, j], state_s5_im[:, j],
                                         state_gla[:, j], *p, final_nw=fnw)
            lat_add = None
            new_s5_re.append(fr)
            new_s5_im.append(fi)
            new_gla.append(fg)
        else:
            p = (o_w_in[j], o_w_out[j], rwkv_w0[j], rwkv_w2[j], rwkv_a0[j], rwkv_a2[j],
                 rwkv_k_k[j], rwkv_k_a[j], rwkv_r_k[j], rwkv_lnx_w[j], rwkv_lnx_b[j])
            x_ctx, fw = _odd_mixer(x_ctx, gt_c, h_ctx, z_rwkv, *p, final_nw=fnw)
            x_lat, _ = _odd_mixer(x_lat, gt_l, h_lat, state_rwkv[:, j], *p, final_nw=fnw)
            new_rwkv.append(fw)
    if depth == 0:
        x_ctx, x_lat = final_norm(x_ctx, final_norm_w), final_norm(x_lat + lat_add, final_norm_w)
    return (x_ctx, x_lat, jnp.stack(new_s5_re, axis=1), jnp.stack(new_s5_im, axis=1),
            jnp.stack(new_gla, axis=1), jnp.stack(new_rwkv, axis=1))
```
